```python
import math
import jax, jax.numpy as jnp
from jax import lax
import numpy as np

D_MODEL = 1024
BATCH = 8
SEQ = 4096
DEPTH = 2

SSM_GROUP_CH = 16
SSM_GROUPS = D_MODEL // 32
SSM_WIDTH = SSM_GROUPS * SSM_GROUP_CH
SSM_STATE = 64
DT_MIN = 1e-3
DT_MAX = 1e-1
EIG_CLIP = 1e-4
HEAD_DIM = 64
ATTN_HEADS = D_MODEL // 128
ATTN_WIDTH = ATTN_HEADS * HEAD_DIM
Q_BLOCK = 128
N_IN = SSM_WIDTH + 3 * ATTN_WIDTH + ATTN_HEADS + 2 * D_MODEL
D_FF = ((8 * D_MODEL + 3 * 256 - 1) // (3 * 256)) * 256
N_MOD = 6
RMS_EPS = 1e-6

kernel_name = "hybrid_s5_fox_gated_block"


def rmsnorm(x, g):
    xf = x.astype(jnp.float32)
    r = lax.rsqrt(jnp.mean(xf * xf, axis=-1, keepdims=True) + RMS_EPS)
    return (xf * r * g.astype(jnp.float32)).astype(x.dtype)


def _linear_recurrence(e1, e2):
    a1, b1 = e1
    a2, b2 = e2
    return a1 * a2, a2 * b1 + b2


def s5_branch(u, lam_re, lam_im, log_dt, b_re, b_im, c_re, c_im, d_skip, w_glu, b_glu):
    dtype = u.dtype
    bsz, s, _ = u.shape
    f32 = jnp.float32
    uf = u.astype(f32).reshape(bsz, s, SSM_GROUPS, SSM_GROUP_CH)
    lam = lax.complex(jnp.minimum(lam_re.astype(f32), -EIG_CLIP), lam_im.astype(f32))
    dt = jnp.exp(log_dt.astype(f32))[:, None]
    lam_bar = jnp.exp(lam * dt)
    b = lax.complex(b_re.astype(f32), b_im.astype(f32))
    b_bar = ((lam_bar - 1.0) / lam)[..., None] * b
    bu = jnp.einsum('bsgh,gph->bsgp', uf, b_bar)
    a = jnp.broadcast_to(lam_bar, bu.shape)
    _, states = lax.associative_scan(_linear_recurrence, (a, bu), axis=1)
    cm = lax.complex(c_re.astype(f32), c_im.astype(f32))
    y = jnp.real(jnp.einsum('bsgp,ghp->bsgh', states, cm))
    y = y + d_skip.astype(f32).reshape(SSM_GROUPS, SSM_GROUP_CH) * uf
    y = y.reshape(bsz, s, SSM_WIDTH).astype(dtype)
    z = jax.nn.gelu(y)
    return z * jax.nn.sigmoid(z @ w_glu + b_glu)


def forgetting_attention(q, k, v, f_logit, b_f):
    bsz, s, _ = q.shape
    nb = s // Q_BLOCK
    f32 = jnp.float32
    q = q.reshape(bsz, s, ATTN_HEADS, HEAD_DIM).transpose(0, 2, 1, 3)
    k = k.reshape(bsz, s, ATTN_HEADS, HEAD_DIM).transpose(0, 2, 1, 3)
    v = v.reshape(bsz, s, ATTN_HEADS, HEAD_DIM).transpose(0, 2, 1, 3)
    log_f = jax.nn.log_sigmoid(f_logit.astype(f32) + b_f.astype(f32))
    cum = jnp.cumsum(log_f, axis=1).transpose(0, 2, 1)
    q_blocks = q.reshape(bsz, ATTN_HEADS, nb, Q_BLOCK, HEAD_DIM).transpose(2, 0, 1, 3, 4)
    cum_blocks = cum.reshape(bsz, ATTN_HEADS, nb, Q_BLOCK).transpose(2, 0, 1, 3)
    k_pos = jnp.arange(s)
    scale = HEAD_DIM ** -0.5

    def one_block(args):
        qb, cb, i = args
        logits = jnp.einsum('bhqd,bhkd->bhqk', qb, k).astype(f32) * scale
        logits = logits + cb[..., None] - cum[:, :, None, :]
        q_pos = i * Q_BLOCK + jnp.arange(Q_BLOCK)
        logits = jnp.where(k_pos[None, :] <= q_pos[:, None], logits, -jnp.inf)
        p = jax.nn.softmax(logits, axis=-1).astype(v.dtype)
        return jnp.einsum('bhqk,bhkd->bhqd', p, v)

    out = lax.map(one_block, (q_blocks, cum_blocks, jnp.arange(nb)))
    return out.transpose(1, 0, 3, 2, 4).reshape(bsz, s, ATTN_WIDTH)


def _fwd_setup_inputs(seed: int = 0) -> dict:
    key = jax.random.key(seed)
    ks = jax.random.split(key, 32)
    f32 = jnp.float32
    nrm = lambda k, shape, s: jax.random.normal(k, shape, f32) * s
    L, D, G, P, H = DEPTH, D_MODEL, SSM_GROUPS, SSM_STATE, SSM_GROUP_CH
    lam_im0 = jnp.pi * jnp.arange(P, dtype=f32)
    return {
        "x": nrm(ks[0], (BATCH, SEQ, D), 1.0),
        "c": nrm(ks[1], (BATCH, D), 1.0),
        "w_ada": nrm(ks[2], (L, D, N_MOD * D), 0.5 * D ** -0.5),
        "b_ada": nrm(ks[3], (L, N_MOD * D), 0.02),
        "g_pre_mix": 1.0 + nrm(ks[4], (L, D), 0.02),
        "g_post_mix": 1.0 + nrm(ks[5], (L, D), 0.02),
        "g_pre_ffn": 1.0 + nrm(ks[6], (L, D), 0.02),
        "g_post_ffn": 1.0 + nrm(ks[7], (L, D), 0.02),
        "w_in": nrm(ks[8], (L, D, N_IN), D ** -0.5),
        "lam_re": -0.5 + nrm(ks[9], (L, G, P), 0.01),
        "lam_im": lam_im0 + nrm(ks[10], (L, G, P), 0.01),
        "log_dt": jax.random.uniform(ks[11], (L, G), f32, math.log(DT_MIN), math.log(DT_MAX)),
        "b_re": nrm(ks[12], (L, G, P, H), (2 * H) ** -0.5),
        "b_im": nrm(ks[13], (L, G, P, H), (2 * H) ** -0.5),
        "c_re": nrm(ks[14], (L, G, H, P), (2 * P) ** -0.5),
        "c_im": nrm(ks[15], (L, G, H, P), (2 * P) ** -0.5),
        "d_skip": nrm(ks[16], (L, SSM_WIDTH), 1.0),
        "w_glu": nrm(ks[17], (L, SSM_WIDTH, SSM_WIDTH), SSM_WIDTH ** -0.5),
        "b_glu": nrm(ks[18], (L, SSM_WIDTH), 0.02),
        "b_f": jax.random.uniform(ks[19], (L, ATTN_HEADS), f32, 1.0, 5.0),
        "w_pa": nrm(ks[20], (L, SSM_WIDTH, D), SSM_WIDTH ** -0.5),
        "w_pb": nrm(ks[21], (L, ATTN_WIDTH, D), ATTN_WIDTH ** -0.5),
        "w_o": nrm(ks[22], (L, D, D), D ** -0.5),
        "w_ffn_gate": nrm(ks[23], (L, D, D_FF), D ** -0.5),
        "w_ffn_up": nrm(ks[24], (L, D, D_FF), D ** -0.5),
        "w_ffn_down": nrm(ks[25], (L, D_FF, D), D_FF ** -0.5),
    }


def _fwd_reference(x, c, w_ada, b_ada, g_pre_mix, g_post_mix, g_pre_ffn, g_post_ffn, w_in,
              lam_re, lam_im, log_dt, b_re, b_im, c_re, c_im, d_skip, w_glu, b_glu, b_f,
              w_pa, w_pb, w_o, w_ffn_gate, w_ffn_up, w_ffn_down):
    split_at = np.cumsum([SSM_WIDTH, ATTN_WIDTH, ATTN_WIDTH, ATTN_WIDTH, ATTN_HEADS, D_MODEL]).tolist()
    cond = jax.nn.silu(c)
    for l in range(DEPTH):
        mod = cond @ w_ada[l] + b_ada[l]
        shift_m, scale_m, gate_m, shift_f, scale_f, gate_f = jnp.split(mod[:, None, :], N_MOD, axis=-1)

        h = rmsnorm(x, g_pre_mix[l]) * (1.0 + scale_m) + shift_m
        proj = h @ w_in[l]
        u_ssm, q, k, v, f_logit, g_a, g_b = jnp.split(proj, split_at, axis=-1)
        y_ssm = s5_branch(u_ssm, lam_re[l], lam_im[l], log_dt[l], b_re[l], b_im[l],
                          c_re[l], c_im[l], d_skip[l], w_glu[l], b_glu[l])
        y_att = forgetting_attention(q, k, v, f_logit, b_f[l])
        merged = jax.nn.sigmoid(g_a) * (y_ssm @ w_pa[l]) + jax.nn.sigmoid(g_b) * (y_att @ w_pb[l])
        y = merged @ w_o[l]
        x = x + gate_m * rmsnorm(y, g_post_mix[l])

        h = rmsnorm(x, g_pre_ffn[l]) * (1.0 + scale_f) + shift_f
        y = (jax.nn.silu(h @ w_ffn_gate[l]) * (h @ w_ffn_up[l])) @ w_ffn_down[l]
        x = x + gate_f * rmsnorm(y, g_post_ffn[l])
    return x


import jax as _jax
import jax.numpy as _jnp

TWIN_FORMAT = 'train_step'
FWD_PARAMS = ['x', 'c', 'w_ada', 'b_ada', 'g_pre_mix', 'g_post_mix', 'g_pre_ffn', 'g_post_ffn', 'w_in', 'lam_re', 'lam_im', 'log_dt', 'b_re', 'b_im', 'c_re', 'c_im', 'd_skip', 'w_glu', 'b_glu', 'b_f', 'w_pa', 'w_pb', 'w_o', 'w_ffn_gate', 'w_ffn_up', 'w_ffn_down']
TWIN_WEIGHTS = ['w_ada', 'b_ada', 'g_pre_mix', 'g_post_mix', 'g_pre_ffn', 'g_post_ffn', 'w_in', 'lam_re', 'lam_im', 'log_dt', 'b_re', 'b_im', 'c_re', 'c_im', 'd_skip', 'w_glu', 'b_glu', 'b_f', 'w_pa', 'w_pb', 'w_o', 'w_ffn_gate', 'w_ffn_up', 'w_ffn_down']
TWIN_DIFF_INPUT = 'x'
TWIN_INPUTS = ['x', 'c', 'w_ada', 'b_ada', 'g_pre_mix', 'g_post_mix', 'g_pre_ffn', 'g_post_ffn', 'w_in', 'lam_re', 'lam_im', 'log_dt', 'b_re', 'b_im', 'c_re', 'c_im', 'd_skip', 'w_glu', 'b_glu', 'b_f', 'w_pa', 'w_pb', 'w_o', 'w_ffn_gate', 'w_ffn_up', 'w_ffn_down', 'loss_target', 'm_w_ada', 'm_b_ada', 'm_g_pre_mix', 'm_g_post_mix', 'm_g_pre_ffn', 'm_g_post_ffn', 'm_w_in', 'm_lam_re', 'm_lam_im', 'm_log_dt', 'm_b_re', 'm_b_im', 'm_c_re', 'm_c_im', 'm_d_skip', 'm_w_glu', 'm_b_glu', 'm_b_f', 'm_w_pa', 'm_w_pb', 'm_w_o', 'm_w_ffn_gate', 'm_w_ffn_up', 'm_w_ffn_down', 'v_w_ada', 'v_b_ada', 'v_g_pre_mix', 'v_g_post_mix', 'v_g_pre_ffn', 'v_g_post_ffn', 'v_w_in', 'v_lam_re', 'v_lam_im', 'v_log_dt', 'v_b_re', 'v_b_im', 'v_c_re', 'v_c_im', 'v_d_skip', 'v_w_glu', 'v_b_glu', 'v_b_f', 'v_w_pa', 'v_w_pb', 'v_w_o', 'v_w_ffn_gate', 'v_w_ffn_up', 'v_w_ffn_down']
TWIN_OUTPUTS = ['loss', 'grad_x', 'grad_w_ada', 'grad_b_ada', 'grad_g_pre_mix', 'grad_g_post_mix', 'grad_g_pre_ffn', 'grad_g_post_ffn', 'grad_w_in', 'grad_lam_re', 'grad_lam_im', 'grad_log_dt', 'grad_b_re', 'grad_b_im', 'grad_c_re', 'grad_c_im', 'grad_d_skip', 'grad_w_glu', 'grad_b_glu', 'grad_b_f', 'grad_w_pa', 'grad_w_pb', 'grad_w_o', 'grad_w_ffn_gate', 'grad_w_ffn_up', 'grad_w_ffn_down', 'delta_w_ada', 'delta_b_ada', 'delta_g_pre_mix', 'delta_g_post_mix', 'delta_g_pre_ffn', 'delta_g_post_ffn', 'delta_w_in', 'delta_lam_re', 'delta_lam_im', 'delta_log_dt', 'delta_b_re', 'delta_b_im', 'delta_c_re', 'delta_c_im', 'delta_d_skip', 'delta_w_glu', 'delta_b_glu', 'delta_b_f', 'delta_w_pa', 'delta_w_pb', 'delta_w_o', 'delta_w_ffn_gate', 'delta_w_ffn_up', 'delta_w_ffn_down', 'new_m_w_ada', 'new_m_b_ada', 'new_m_g_pre_mix', 'new_m_g_post_mix', 'new_m_g_pre_ffn', 'new_m_g_post_ffn', 'new_m_w_in', 'new_m_lam_re', 'new_m_lam_im', 'new_m_log_dt', 'new_m_b_re', 'new_m_b_im', 'new_m_c_re', 'new_m_c_im', 'new_m_d_skip', 'new_m_w_glu', 'new_m_b_glu', 'new_m_b_f', 'new_m_w_pa', 'new_m_w_pb', 'new_m_w_o', 'new_m_w_ffn_gate', 'new_m_w_ffn_up', 'new_m_w_ffn_down', 'new_v_w_ada', 'new_v_b_ada', 'new_v_g_pre_mix', 'new_v_g_post_mix', 'new_v_g_pre_ffn', 'new_v_g_post_ffn', 'new_v_w_in', 'new_v_lam_re', 'new_v_lam_im', 'new_v_log_dt', 'new_v_b_re', 'new_v_b_im', 'new_v_c_re', 'new_v_c_im', 'new_v_d_skip', 'new_v_w_glu', 'new_v_b_glu', 'new_v_b_f', 'new_v_w_pa', 'new_v_w_pb', 'new_v_w_o', 'new_v_w_ffn_gate', 'new_v_w_ffn_up', 'new_v_w_ffn_down']
TWIN_LEAF_KINDS = {'loss': 'loss', 'grad_x': 'grad_x', 'grad_w_ada': 'grad_w', 'grad_b_ada': 'grad_w', 'grad_g_pre_mix': 'grad_w', 'grad_g_post_mix': 'grad_w', 'grad_g_pre_ffn': 'grad_w', 'grad_g_post_ffn': 'grad_w', 'grad_w_in': 'grad_w', 'grad_lam_re': 'grad_w', 'grad_lam_im': 'grad_w', 'grad_log_dt': 'grad_w', 'grad_b_re': 'grad_w', 'grad_b_im': 'grad_w', 'grad_c_re': 'grad_w', 'grad_c_im': 'grad_w', 'grad_d_skip': 'grad_w', 'grad_w_glu': 'grad_w', 'grad_b_glu': 'grad_w', 'grad_b_f': 'grad_w', 'grad_w_pa': 'grad_w', 'grad_w_pb': 'grad_w', 'grad_w_o': 'grad_w', 'grad_w_ffn_gate': 'grad_w', 'grad_w_ffn_up': 'grad_w', 'grad_w_ffn_down': 'grad_w', 'delta_w_ada': 'delta_w', 'delta_b_ada': 'delta_w', 'delta_g_pre_mix': 'delta_w', 'delta_g_post_mix': 'delta_w', 'delta_g_pre_ffn': 'delta_w', 'delta_g_post_ffn': 'delta_w', 'delta_w_in': 'delta_w', 'delta_lam_re': 'delta_w', 'delta_lam_im': 'delta_w', 'delta_log_dt': 'delta_w', 'delta_b_re': 'delta_w', 'delta_b_im': 'delta_w', 'delta_c_re': 'delta_w', 'delta_c_im': 'delta_w', 'delta_d_skip': 'delta_w', 'delta_w_glu': 'delta_w', 'delta_b_glu': 'delta_w', 'delta_b_f': 'delta_w', 'delta_w_pa': 'delta_w', 'delta_w_pb': 'delta_w', 'delta_w_o': 'delta_w', 'delta_w_ffn_gate': 'delta_w', 'delta_w_ffn_up': 'delta_w', 'delta_w_ffn_down': 'delta_w', 'new_m_w_ada': 'new_m', 'new_m_b_ada': 'new_m', 'new_m_g_pre_mix': 'new_m', 'new_m_g_post_mix': 'new_m', 'new_m_g_pre_ffn': 'new_m', 'new_m_g_post_ffn': 'new_m', 'new_m_w_in': 'new_m', 'new_m_lam_re': 'new_m', 'new_m_lam_im': 'new_m', 'new_m_log_dt': 'new_m', 'new_m_b_re': 'new_m', 'new_m_b_im': 'new_m', 'new_m_c_re': 'new_m', 'new_m_c_im': 'new_m', 'new_m_d_skip': 'new_m', 'new_m_w_glu': 'new_m', 'new_m_b_glu': 'new_m', 'new_m_b_f': 'new_m', 'new_m_w_pa': 'new_m', 'new_m_w_pb': 'new_m', 'new_m_w_o': 'new_m', 'new_m_w_ffn_gate': 'new_m', 'new_m_w_ffn_up': 'new_m', 'new_m_w_ffn_down': 'new_m', 'new_v_w_ada': 'new_v', 'new_v_b_ada': 'new_v', 'new_v_g_pre_mix': 'new_v', 'new_v_g_post_mix': 'new_v', 'new_v_g_pre_ffn': 'new_v', 'new_v_g_post_ffn': 'new_v', 'new_v_w_in': 'new_v', 'new_v_lam_re': 'new_v', 'new_v_lam_im': 'new_v', 'new_v_log_dt': 'new_v', 'new_v_b_re': 'new_v', 'new_v_b_im': 'new_v', 'new_v_c_re': 'new_v', 'new_v_c_im': 'new_v', 'new_v_d_skip': 'new_v', 'new_v_w_glu': 'new_v', 'new_v_b_glu': 'new_v', 'new_v_b_f': 'new_v', 'new_v_w_pa': 'new_v', 'new_v_w_pb': 'new_v', 'new_v_w_o': 'new_v', 'new_v_w_ffn_gate': 'new_v', 'new_v_w_ffn_up': 'new_v', 'new_v_w_ffn_down': 'new_v'}


def _forward(args):
    return _fwd_reference(*[args[k] for k in FWD_PARAMS])


def _output_shape():
    out = _jax.eval_shape(lambda: _forward(_fwd_setup_inputs(0)))
    return out.shape, out.dtype

N_MICROBATCH = 1
ADAM_LR = 0.001
ADAM_B1 = 0.9
ADAM_B2 = 0.999
ADAM_EPS = 1e-08
ADAM_WD = 0.01
ADAM_STEP = 10
PER_EXAMPLE_BATCH_AXIS = {'x': 0, 'c': 0, 'loss_target': 0}
SHARED_INPUTS = []
_WEIGHT_DTYPES = {'w_ada': _jnp.float32, 'b_ada': _jnp.float32, 'g_pre_mix': _jnp.float32, 'g_post_mix': _jnp.float32, 'g_pre_ffn': _jnp.float32, 'g_post_ffn': _jnp.float32, 'w_in': _jnp.float32, 'lam_re': _jnp.float32, 'lam_im': _jnp.float32, 'log_dt': _jnp.float32, 'b_re': _jnp.float32, 'b_im': _jnp.float32, 'c_re': _jnp.float32, 'c_im': _jnp.float32, 'd_skip': _jnp.float32, 'w_glu': _jnp.float32, 'b_glu': _jnp.float32, 'b_f': _jnp.float32, 'w_pa': _jnp.float32, 'w_pb': _jnp.float32, 'w_o': _jnp.float32, 'w_ffn_gate': _jnp.float32, 'w_ffn_up': _jnp.float32, 'w_ffn_down': _jnp.float32}
MOMENT_SCALE = {'w_ada': 2.019112e+00, 'b_ada': 3.672092e+00, 'g_pre_mix': 2.460508e-01, 'g_post_mix': 4.075187e+00, 'g_pre_ffn': 1.719607e-01, 'g_post_ffn': 3.725776e+00, 'w_in': 3.600789e-01, 'lam_re': 2.758544e-02, 'lam_im': 2.163918e-02, 'log_dt': 1.742827e+00, 'b_re': 1.965738e-02, 'b_im': 2.126220e-02, 'c_re': 4.280280e-02, 'c_im': 4.110333e-02, 'd_skip': 6.731371e-01, 'w_glu': 9.512354e-02, 'b_glu': 2.699911e-01, 'b_f': 5.273694e-01, 'w_pa': 4.300918e-01, 'w_pb': 7.485377e-01, 'w_o': 8.720870e-01, 'w_ffn_gate': 8.007105e-02, 'w_ffn_up': 1.069171e-01, 'w_ffn_down': 1.817811e-01}


def _to_microbatches(a, axis):
    t = _jnp.moveaxis(a, axis, 0)
    t = t.reshape((N_MICROBATCH, t.shape[0] // N_MICROBATCH) + t.shape[1:])
    return _jnp.moveaxis(t, 1, axis + 1)


def setup_inputs(seed: int = 0) -> dict:
    inp = _fwd_setup_inputs(seed)
    key = _jax.random.fold_in(_jax.random.key(seed), 7919)
    shape, _ = _output_shape()
    out = dict(inp)
    out["loss_target"] = _jax.random.normal(_jax.random.fold_in(key, 0), shape, _jnp.float32)
    for i, name in enumerate(TWIN_WEIGHTS):
        w = inp[name].astype(_jnp.float32)
        if MOMENT_SCALE is None:
            s = _jnp.sqrt(_jnp.mean(_jnp.square(w)) + 1e-30)
        else:
            s = MOMENT_SCALE[name]
        km, kv = _jax.random.split(_jax.random.fold_in(key, i + 1))
        out[name] = w
        out["m_" + name] = s * _jax.random.normal(km, w.shape, _jnp.float32)
        out["v_" + name] = (s * s) * _jax.random.uniform(kv, w.shape, _jnp.float32, 0.5, 1.5)
    if N_MICROBATCH > 1:
        for name, axis in PER_EXAMPLE_BATCH_AXIS.items():
            out[name] = _to_microbatches(out[name], axis)
    return {'x': out['x'], 'c': out['c'], 'w_ada': out['w_ada'], 'b_ada': out['b_ada'], 'g_pre_mix': out['g_pre_mix'], 'g_post_mix': out['g_post_mix'], 'g_pre_ffn': out['g_pre_ffn'], 'g_post_ffn': out['g_post_ffn'], 'w_in': out['w_in'], 'lam_re': out['lam_re'], 'lam_im': out['lam_im'], 'log_dt': out['log_dt'], 'b_re': out['b_re'], 'b_im': out['b_im'], 'c_re': out['c_re'], 'c_im': out['c_im'], 'd_skip': out['d_skip'], 'w_glu': out['w_glu'], 'b_glu': out['b_glu'], 'b_f': out['b_f'], 'w_pa': out['w_pa'], 'w_pb': out['w_pb'], 'w_o': out['w_o'], 'w_ffn_gate': out['w_ffn_gate'], 'w_ffn_up': out['w_ffn_up'], 'w_ffn_down': out['w_ffn_down'], 'loss_target': out['loss_target'], 'm_w_ada': out['m_w_ada'], 'm_b_ada': out['m_b_ada'], 'm_g_pre_mix': out['m_g_pre_mix'], 'm_g_post_mix': out['m_g_post_mix'], 'm_g_pre_ffn': out['m_g_pre_ffn'], 'm_g_post_ffn': out['m_g_post_ffn'], 'm_w_in': out['m_w_in'], 'm_lam_re': out['m_lam_re'], 'm_lam_im': out['m_lam_im'], 'm_log_dt': out['m_log_dt'], 'm_b_re': out['m_b_re'], 'm_b_im': out['m_b_im'], 'm_c_re': out['m_c_re'], 'm_c_im': out['m_c_im'], 'm_d_skip': out['m_d_skip'], 'm_w_glu': out['m_w_glu'], 'm_b_glu': out['m_b_glu'], 'm_b_f': out['m_b_f'], 'm_w_pa': out['m_w_pa'], 'm_w_pb': out['m_w_pb'], 'm_w_o': out['m_w_o'], 'm_w_ffn_gate': out['m_w_ffn_gate'], 'm_w_ffn_up': out['m_w_ffn_up'], 'm_w_ffn_down': out['m_w_ffn_down'], 'v_w_ada': out['v_w_ada'], 'v_b_ada': out['v_b_ada'], 'v_g_pre_mix': out['v_g_pre_mix'], 'v_g_post_mix': out['v_g_post_mix'], 'v_g_pre_ffn': out['v_g_pre_ffn'], 'v_g_post_ffn': out['v_g_post_ffn'], 'v_w_in': out['v_w_in'], 'v_lam_re': out['v_lam_re'], 'v_lam_im': out['v_lam_im'], 'v_log_dt': out['v_log_dt'], 'v_b_re': out['v_b_re'], 'v_b_im': out['v_b_im'], 'v_c_re': out['v_c_re'], 'v_c_im': out['v_c_im'], 'v_d_skip': out['v_d_skip'], 'v_w_glu': out['v_w_glu'], 'v_b_glu': out['v_b_glu'], 'v_b_f': out['v_b_f'], 'v_w_pa': out['v_w_pa'], 'v_w_pb': out['v_w_pb'], 'v_w_o': out['v_w_o'], 'v_w_ffn_gate': out['v_w_ffn_gate'], 'v_w_ffn_up': out['v_w_ffn_up'], 'v_w_ffn_down': out['v_w_ffn_down']}


def _loss(weights, diff, rest, loss_target):
    with _jax.named_scope("forward"):
        args = {**rest, TWIN_DIFF_INPUT: diff, **{k: w.astype(_WEIGHT_DTYPES[k]) for k, w in weights.items()}}
        y = _forward(args)
    with _jax.named_scope("loss_head"):
        err = _jnp.square(y.astype(_jnp.float32) - loss_target)
        return 0.5 * _jnp.sum(_jnp.mean(err, axis=-1)) if err.ndim else 0.5 * err


def _adamw(w, g, m, v):
    m = ADAM_B1 * m + (1.0 - ADAM_B1) * g
    v = ADAM_B2 * v + (1.0 - ADAM_B2) * _jnp.square(g)
    m_hat = m / (1.0 - ADAM_B1 ** ADAM_STEP)
    v_hat = v / (1.0 - ADAM_B2 ** ADAM_STEP)
    delta = -ADAM_LR * (m_hat / (_jnp.sqrt(v_hat) + ADAM_EPS) + ADAM_WD * w)
    return delta, m, v


def reference(x, c, w_ada, b_ada, g_pre_mix, g_post_mix, g_pre_ffn, g_post_ffn, w_in, lam_re, lam_im, log_dt, b_re, b_im, c_re, c_im, d_skip, w_glu, b_glu, b_f, w_pa, w_pb, w_o, w_ffn_gate, w_ffn_up, w_ffn_down, loss_target, m_w_ada, m_b_ada, m_g_pre_mix, m_g_post_mix, m_g_pre_ffn, m_g_post_ffn, m_w_in, m_lam_re, m_lam_im, m_log_dt, m_b_re, m_b_im, m_c_re, m_c_im, m_d_skip, m_w_glu, m_b_glu, m_b_f, m_w_pa, m_w_pb, m_w_o, m_w_ffn_gate, m_w_ffn_up, m_w_ffn_down, v_w_ada, v_b_ada, v_g_pre_mix, v_g_post_mix, v_g_pre_ffn, v_g_post_ffn, v_w_in, v_lam_re, v_lam_im, v_log_dt, v_b_re, v_b_im, v_c_re, v_c_im, v_d_skip, v_w_glu, v_b_glu, v_b_f, v_w_pa, v_w_pb, v_w_o, v_w_ffn_gate, v_w_ffn_up, v_w_ffn_down):
    given = dict(x=x, c=c, w_ada=w_ada, b_ada=b_ada, g_pre_mix=g_pre_mix, g_post_mix=g_post_mix, g_pre_ffn=g_pre_ffn, g_post_ffn=g_post_ffn, w_in=w_in, lam_re=lam_re, lam_im=lam_im, log_dt=log_dt, b_re=b_re, b_im=b_im, c_re=c_re, c_im=c_im, d_skip=d_skip, w_glu=w_glu, b_glu=b_glu, b_f=b_f, w_pa=w_pa, w_pb=w_pb, w_o=w_o, w_ffn_gate=w_ffn_gate, w_ffn_up=w_ffn_up, w_ffn_down=w_ffn_down, loss_target=loss_target, m_w_ada=m_w_ada, m_b_ada=m_b_ada, m_g_pre_mix=m_g_pre_mix, m_g_post_mix=m_g_post_mix, m_g_pre_ffn=m_g_pre_ffn, m_g_post_ffn=m_g_post_ffn, m_w_in=m_w_in, m_lam_re=m_lam_re, m_lam_im=m_lam_im, m_log_dt=m_log_dt, m_b_re=m_b_re, m_b_im=m_b_im, m_c_re=m_c_re, m_c_im=m_c_im, m_d_skip=m_d_skip, m_w_glu=m_w_glu, m_b_glu=m_b_glu, m_b_f=m_b_f, m_w_pa=m_w_pa, m_w_pb=m_w_pb, m_w_o=m_w_o, m_w_ffn_gate=m_w_ffn_gate, m_w_ffn_up=m_w_ffn_up, m_w_ffn_down=m_w_ffn_down, v_w_ada=v_w_ada, v_b_ada=v_b_ada, v_g_pre_mix=v_g_pre_mix, v_g_post_mix=v_g_post_mix, v_g_pre_ffn=v_g_pre_ffn, v_g_post_ffn=v_g_post_ffn, v_w_in=v_w_in, v_lam_re=v_lam_re, v_lam_im=v_lam_im, v_log_dt=v_log_dt, v_b_re=v_b_re, v_b_im=v_b_im, v_c_re=v_c_re, v_c_im=v_c_im, v_d_skip=v_d_skip, v_w_glu=v_w_glu, v_b_glu=v_b_glu, v_b_f=v_b_f, v_w_pa=v_w_pa, v_w_pb=v_w_pb, v_w_o=v_w_o, v_w_ffn_gate=v_w_ffn_gate, v_w_ffn_up=v_w_ffn_up, v_w_ffn_down=v_w_ffn_down)
    weights = {n: given[n] for n in TWIN_WEIGHTS}
    shared = {n: given[n] for n in SHARED_INPUTS}
    per_example = {n: given[n] for n in ['x', 'c']}
    grad_fn = _jax.value_and_grad(_loss, argnums=(0, 1))

    def one_microbatch(ex, loss_target):
        ex = dict(ex)
        diff = ex.pop(TWIN_DIFF_INPUT)
        return grad_fn(weights, diff, {**shared, **ex}, loss_target)

    if N_MICROBATCH == 1:
        loss, (grad_w, grad_x) = one_microbatch(per_example, given["loss_target"])
    else:
        def body(carry, xs):
            loss_sum, grad_sum = carry
            l_k, (gw_k, gx_k) = one_microbatch(xs[0], xs[1])
            with _jax.named_scope("update"):
                return (loss_sum + l_k, _jax.tree.map(_jnp.add, grad_sum, gw_k)), gx_k

        init = (_jnp.zeros((), _jnp.float32), _jax.tree.map(_jnp.zeros_like, weights))
        (loss, grad_w), grad_x = _jax.lax.scan(body, init, (per_example, given["loss_target"]))
    with _jax.named_scope("update"):
        delta_w, new_m, new_v = {}, {}, {}
        for n in TWIN_WEIGHTS:
            delta_w[n], new_m[n], new_v[n] = _adamw(weights[n], grad_w[n], given["m_" + n], given["v_" + n])
    return (loss, grad_x, *[grad_w[n] for n in TWIN_WEIGHTS], *[delta_w[n] for n in TWIN_WEIGHTS],
            *[new_m[n] for n in TWIN_WEIGHTS], *[new_v[n] for n in TWIN_WEIGHTS])
```

```python
import functools
import math

import jax
import jax.numpy as jnp
from jax import lax
from jax.experimental import pallas as pl
from jax.experimental.pallas import tpu as pltpu

F32 = jnp.float32
BF = jnp.bfloat16
NDEV = 8
LANES = 128
SUBLANES = 8
VMEM_LIMIT = 48 * 1024 * 1024

SSM_H = 16
HEAD_DIM = 64
RMS_EPS = 1e-6
EIG_CLIP = 1e-4
ADAM_LR = 0.001
ADAM_B1 = 0.9
ADAM_B2 = 0.999
ADAM_EPS = 1e-08
ADAM_WD = 0.01
ADAM_STEP = 10
NEG = -1e30
HI = lax.Precision.HIGHEST

WEIGHTS = ['w_ada', 'b_ada', 'g_pre_mix', 'g_post_mix', 'g_pre_ffn', 'g_post_ffn', 'w_in', 'lam_re', 'lam_im',
           'log_dt', 'b_re', 'b_im', 'c_re', 'c_im', 'd_skip', 'w_glu', 'b_glu', 'b_f', 'w_pa', 'w_pb', 'w_o',
           'w_ffn_gate', 'w_ffn_up', 'w_ffn_down']
COL_SHARDED = ['w_in', 'w_pa', 'w_pb', 'w_ffn_gate', 'w_ffn_up']
ROW_SHARDED = ['w_glu', 'w_o', 'w_ffn_down']
BIG = COL_SHARDED + ROW_SHARDED
SMALL = ['b_ada', 'g_pre_mix', 'g_post_mix', 'g_pre_ffn', 'g_post_ffn', 'lam_re', 'lam_im', 'log_dt', 'b_re',
         'b_im', 'c_re', 'c_im', 'd_skip', 'b_glu', 'b_f']


def _fit(dim, target, align):
    if dim <= target:
        return dim
    t = (target // align) * align
    while t >= align:
        if dim % t == 0:
            return t
        t -= align
    return dim


def _params(**kw):
    return pltpu.CompilerParams(vmem_limit_bytes=VMEM_LIMIT, **kw)


def _mm(a, b, *, ta=False, tb=False, out_dtype=F32, tm=512, tn=512, tk=2048, precision=None, name):
    m, k = (a.shape[1], a.shape[0]) if ta else a.shape
    n = b.shape[0] if tb else b.shape[1]
    assert (b.shape[1] if tb else b.shape[0]) == k
    tm = _fit(m, tm, LANES if ta else 16)
    tn = _fit(n, tn, LANES)
    tk = _fit(k, tk, LANES)
    nk = k // tk
    dims = (((0 if ta else 1,), (1 if tb else 0,)), ((), ()))

    def kern(a_ref, b_ref, o_ref, *scratch):
        av, bv = a_ref[...], b_ref[...]
        if precision is None:
            av, bv = av.astype(BF), bv.astype(BF)
        p = lax.dot_general(av, bv, dims, preferred_element_type=F32, precision=precision)
        if nk == 1:
            o_ref[...] = p.astype(out_dtype)
            return
        acc_ref, = scratch
        kk = pl.program_id(2)

        @pl.when(kk == 0)
        def _():
            acc_ref[...] = p

        @pl.when(kk > 0)
        def _():
            acc_ref[...] += p

        @pl.when(kk == nk - 1)
        def _():
            o_ref[...] = acc_ref[...].astype(out_dtype)

    a_spec = pl.BlockSpec((tk, tm), lambda i, j, kk: (kk, i)) if ta else pl.BlockSpec((tm, tk), lambda i, j, kk: (i, kk))
    b_spec = pl.BlockSpec((tn, tk), lambda i, j, kk: (j, kk)) if tb else pl.BlockSpec((tk, tn), lambda i, j, kk: (kk, j))
    return pl.pallas_call(
        kern, name=name,
        out_shape=jax.ShapeDtypeStruct((m, n), out_dtype),
        grid=(m // tm, n // tn, nk),
        in_specs=[a_spec, b_spec],
        out_specs=pl.BlockSpec((tm, tn), lambda i, j, kk: (i, j)),
        scratch_shapes=[] if nk == 1 else [pltpu.VMEM((tm, tn), F32)],
        compiler_params=_params(dimension_semantics=("parallel", "parallel", "arbitrary")),
    )(a, b)


def _rowwise(fn, tiles, params, outs, *, tr=256, name):
    tiles = [t if isinstance(t, tuple) else (t, t.shape[1], 0) for t in tiles]
    s = tiles[0][0].shape[0]
    tr = _fit(s, tr, 16)
    nt, npar = len(tiles), len(params)

    def kern(*refs):
        i = pl.program_id(0)
        res = fn(*[r[...] for r in refs[:nt + npar]])
        if not isinstance(res, (tuple, list)):
            res = (res,)
        for (w, dt, kind), o_ref, r in zip(outs, refs[nt + npar:], res):
            if kind == 'tile':
                o_ref[...] = r.astype(dt)
            else:
                part = jnp.sum(r.astype(F32), axis=0, keepdims=True)

                @pl.when(i == 0)
                def _(o_ref=o_ref, part=part):
                    o_ref[...] = part

                @pl.when(i > 0)
                def _(o_ref=o_ref, part=part):
                    o_ref[...] += part

    def tile_spec(w, cb):
        return pl.BlockSpec((tr, w), lambda i: (i, cb))

    in_specs = [tile_spec(w, cb) for _, w, cb in tiles]
    in_specs += [pl.BlockSpec(p.shape, lambda i, nd=p.ndim: (0,) * nd) for p in params]
    out_shape, out_specs = [], []
    for w, dt, kind in outs:
        if kind == 'tile':
            out_shape.append(jax.ShapeDtypeStruct((s, w), dt))
            out_specs.append(pl.BlockSpec((tr, w), lambda i: (i, 0)))
        else:
            out_shape.append(jax.ShapeDtypeStruct((1, w), F32))
            out_specs.append(pl.BlockSpec((1, w), lambda i: (0, 0)))
    res = pl.pallas_call(
        kern, name=name, out_shape=out_shape, grid=(s // tr,), in_specs=in_specs, out_specs=out_specs,
        compiler_params=_params(dimension_semantics=("arbitrary",)),
    )(*[t[0] for t in tiles], *params)
    return res


def _sigmoid(z):
    return 1.0 / (1.0 + jnp.exp(-z))


def _silu(z):
    return z * _sigmoid(z)


_GELU_K = math.sqrt(2.0 / math.pi)


def _gelu(y):
    return 0.5 * y * (1.0 + jnp.tanh(_GELU_K * (y + 0.044715 * y * y * y)))


def _gelu_grad(y):
    th = jnp.tanh(_GELU_K * (y + 0.044715 * y * y * y))
    return 0.5 * (1.0 + th) + 0.5 * y * (1.0 - th * th) * _GELU_K * (1.0 + 3.0 * 0.044715 * y * y)


def _rms(x):
    return lax.rsqrt(jnp.mean(x * x, axis=-1, keepdims=True) + RMS_EPS)


def _norm_bwd(dn, xhat, r):
    return r * (dn - xhat * jnp.mean(dn * xhat, axis=-1, keepdims=True))


def _cum_fwd(flog, bf_row, nh, *, name):
    s = flog.shape[0]
    w = nh * HEAD_DIM
    t = _fit(s, 256, SUBLANES)

    def kern(f_ref, b_ref, o_ref, carry_ref):
        i = pl.program_id(0)

        @pl.when(i == 0)
        def _():
            carry_ref[...] = jnp.zeros_like(carry_ref)

        z = f_ref[...] + b_ref[...]
        logf = jnp.minimum(z, 0.0) - jnp.log(1.0 + jnp.exp(-jnp.abs(z)))
        hh = lax.broadcasted_iota(jnp.int32, (LANES, w), 0)
        cc = lax.broadcasted_iota(jnp.int32, (LANES, w), 1)
        expand = (cc // HEAD_DIM == hh).astype(F32)
        lx = jnp.dot(logf, expand, preferred_element_type=F32, precision=HI)
        rr = lax.broadcasted_iota(jnp.int32, (t, t), 0)
        kk = lax.broadcasted_iota(jnp.int32, (t, t), 1)
        tri = (kk <= rr).astype(F32)
        cum = jnp.dot(tri, lx, preferred_element_type=F32, precision=HI) + carry_ref[...]
        o_ref[...] = cum
        carry_ref[...] = cum[t - 1:t, :]

    return pl.pallas_call(
        kern, name=name, out_shape=jax.ShapeDtypeStruct((s, w), F32), grid=(s // t,),
        in_specs=[pl.BlockSpec((t, LANES), lambda i: (i, 0)), pl.BlockSpec((1, LANES), lambda i: (0, 0))],
        out_specs=pl.BlockSpec((t, w), lambda i: (i, 0)),
        scratch_shapes=[pltpu.VMEM((1, w), F32)],
        compiler_params=_params(dimension_semantics=("arbitrary",)),
    )(flog, bf_row)


def _cum_bwd(dcrow, frow, bf_col, *, name):
    _, nh, s = dcrow.shape
    t = _fit(s, 512, LANES)
    nb = s // t

    def kern(d_ref, f_ref, b_ref, df_ref, db_ref):
        rr = lax.broadcasted_iota(jnp.int32, (t, t), 0)
        kk = lax.broadcasted_iota(jnp.int32, (t, t), 1)
        upper = (rr >= kk).astype(F32)
        carry = jnp.zeros((nh, 1), F32)
        db = jnp.zeros((nh, 1), F32)
        for blk in range(nb - 1, -1, -1):
            sl = slice(blk * t, (blk + 1) * t)
            rc = jnp.dot(d_ref[0, :, sl] + d_ref[1, :, sl], upper, preferred_element_type=F32, precision=HI) + carry
            carry = rc[:, 0:1]
            df = rc * _sigmoid(-(f_ref[:, sl] + b_ref[...]))
            df_ref[:, sl] = df
            db = db + jnp.sum(df, axis=1, keepdims=True)
        db_ref[...] = db

    return pl.pallas_call(
        kern, name=name,
        out_shape=[jax.ShapeDtypeStruct((nh, s), F32), jax.ShapeDtypeStruct((nh, 1), F32)],
        compiler_params=_params(),
    )(dcrow, frow, bf_col)


def _attn_fwd(proj, cumx, cumrow, qcol, *, name):
    s = proj.shape[0]
    w = cumx.shape[1]
    nhp = w // LANES
    t = _fit(s, 256, LANES)
    nq = s // t
    scale = HEAD_DIM ** -0.5
    qb, kb, vb = qcol // LANES, (qcol + w) // LANES, (qcol + 2 * w) // LANES

    def kern(q_ref, k_ref, v_ref, cx_ref, cr_ref, o_ref, l_ref):
        i = pl.program_id(1)
        lane = lax.broadcasted_iota(jnp.int32, (t, LANES), 1)
        rows = i * t + lax.broadcasted_iota(jnp.int32, (t, t), 0)
        cols0 = lax.broadcasted_iota(jnp.int32, (t, t), 1)
        q2 = q_ref[...]
        outs, lses = [], []
        for e in range(2):
            msk = (lane >= HEAD_DIM * e) & (lane < HEAD_DIM * (e + 1))
            qe = jnp.where(msk, q2, 0.0).astype(BF)
            cq = cx_ref[:, HEAD_DIM * e:HEAD_DIM * e + 1]

            def body(j, carry, qe=qe, cq=cq, e=e):
                m, l, acc = carry
                off = pl.multiple_of(j * t, LANES)
                kt = k_ref[pl.ds(off, t), :].astype(BF)
                vt = v_ref[pl.ds(off, t), :].astype(BF)
                ck = cr_ref[e:e + 1, pl.ds(off, t)]
                sc = lax.dot_general(qe, kt, (((1,), (1,)), ((), ())), preferred_element_type=F32) * scale
                sc = sc + (cq - ck)
                sc = jnp.where(cols0 + j * t <= rows, sc, NEG)
                m_new = jnp.maximum(m, jnp.max(sc, axis=1, keepdims=True))
                p = jnp.exp(sc - m_new)
                alpha = jnp.exp(m - m_new)
                l = alpha * l + jnp.sum(p, axis=1, keepdims=True)
                acc = alpha * acc + jnp.dot(p.astype(BF), vt, preferred_element_type=F32)
                return m_new, l, acc

            init = (jnp.full((t, 1), NEG, F32), jnp.zeros((t, 1), F32), jnp.zeros((t, LANES), F32))
            m, l, acc = lax.fori_loop(0, i + 1, body, init)
            outs.append(acc / l)
            lses.append(jnp.broadcast_to(m + jnp.log(l), (t, LANES)))
        o_ref[...] = jnp.where(lane < HEAD_DIM, outs[0], outs[1]).astype(BF)
        l_ref[...] = jnp.where(lane < HEAD_DIM, lses[0], lses[1])

    return pl.pallas_call(
        kern, name=name,
        out_shape=[jax.ShapeDtypeStruct((s, w), BF), jax.ShapeDtypeStruct((nhp, s, LANES), F32)],
        grid=(nhp, nq),
        in_specs=[pl.BlockSpec((t, LANES), lambda h, i: (i, qb + h)),
                  pl.BlockSpec((s, LANES), lambda h, i: (0, kb + h)),
                  pl.BlockSpec((s, LANES), lambda h, i: (0, vb + h)),
                  pl.BlockSpec((t, LANES), lambda h, i: (i, h)),
                  pl.BlockSpec((None, 2, s), lambda h, i: (h, 0, 0))],
        out_specs=[pl.BlockSpec((t, LANES), lambda h, i: (i, h)),
                   pl.BlockSpec((None, t, LANES), lambda h, i: (h, i, 0))],
        compiler_params=_params(dimension_semantics=("parallel", "arbitrary")),
    )(proj, proj, proj, cumx, cumrow)


def _attn_bwd(proj, do, o, lse, cumx, cumrow, qcol, *, name):
    s = proj.shape[0]
    w = cumx.shape[1]
    nhp = w // LANES
    t = _fit(s, 256, LANES)
    nq = s // t
    scale = HEAD_DIM ** -0.5
    qb, kb, vb = qcol // LANES, (qcol + w) // LANES, (qcol + 2 * w) // LANES
    tn_dims = (((0,), (0,)), ((), ()))
    nt_dims = (((1,), (1,)), ((), ()))

    def kern(q_ref, k_ref, v_ref, do_ref, o_ref, l_ref, cx_ref, cr_ref, dq_ref, dk_ref, dv_ref, dc_ref, dr_ref):
        j = pl.program_id(1)

        @pl.when(j == 0)
        def _():
            dq_ref[...] = jnp.zeros_like(dq_ref)
            dr_ref[...] = jnp.zeros_like(dr_ref)

        lane = lax.broadcasted_iota(jnp.int32, (t, LANES), 1)
        rows0 = lax.broadcasted_iota(jnp.int32, (t, t), 0)
        cols = j * t + lax.broadcasted_iota(jnp.int32, (t, t), 1)
        msks = [(lane >= HEAD_DIM * e) & (lane < HEAD_DIM * (e + 1)) for e in range(2)]
        k2 = k_ref[...]
        kt = k2.astype(BF)
        vt = v_ref[...].astype(BF)
        kes = [jnp.where(msks[e], k2, 0.0).astype(BF) for e in range(2)]
        joff = pl.multiple_of(j * t, LANES)
        cks = [cr_ref[e:e + 1, pl.ds(joff, t)] for e in range(2)]

        def body(i, carry):
            dk_acc, dv_acc, dc0, dc1 = carry
            dcs = [dc0, dc1]
            off = pl.multiple_of(i * t, LANES)
            q2 = q_ref[pl.ds(off, t), :]
            do2 = do_ref[pl.ds(off, t), :]
            dd = do2 * o_ref[pl.ds(off, t), :].astype(F32)
            lse2 = l_ref[pl.ds(off, t), :]
            cx2 = cx_ref[pl.ds(off, t), :]
            causal = cols <= rows0 + i * t
            dq_blk = jnp.zeros((t, LANES), F32)
            rowsums = []
            for e in range(2):
                qe = jnp.where(msks[e], q2, 0.0).astype(BF)
                doe = jnp.where(msks[e], do2, 0.0).astype(BF)
                delta = jnp.sum(jnp.where(msks[e], dd, 0.0), axis=1, keepdims=True)
                lse = lse2[:, HEAD_DIM * e:HEAD_DIM * e + 1]
                cq = cx2[:, HEAD_DIM * e:HEAD_DIM * e + 1]
                sc = lax.dot_general(qe, kt, nt_dims, preferred_element_type=F32) * scale + (cq - cks[e])
                p = jnp.where(causal, jnp.exp(sc - lse), 0.0)
                dp = lax.dot_general(doe, vt, nt_dims, preferred_element_type=F32)
                ds = p * (dp - delta)
                dsb = ds.astype(BF)
                dv_acc = dv_acc + lax.dot_general(p.astype(BF), doe, tn_dims, preferred_element_type=F32)
                dk_acc = dk_acc + lax.dot_general(dsb, qe, tn_dims, preferred_element_type=F32) * scale
                dq_blk = dq_blk + jnp.dot(dsb, kes[e], preferred_element_type=F32) * scale
                dcs[e] = dcs[e] - jnp.sum(ds, axis=0, keepdims=True)
                rowsums.append(jnp.broadcast_to(jnp.sum(ds, axis=1, keepdims=True), (t, LANES)))
            dq_ref[pl.ds(off, t), :] += dq_blk
            dr_ref[pl.ds(off, t), :] += jnp.where(msks[0], rowsums[0], rowsums[1])
            return dk_acc, dv_acc, dcs[0], dcs[1]

        init = (jnp.zeros((t, LANES), F32), jnp.zeros((t, LANES), F32), jnp.zeros((1, t), F32), jnp.zeros((1, t), F32))
        dk_acc, dv_acc, dc0, dc1 = lax.fori_loop(j, nq, body, init)
        dk_ref[...] = dk_acc
        dv_ref[...] = dv_acc
        dc_ref[0:1, :] = dc0
        dc_ref[1:2, :] = dc1

    full = lambda cb: pl.BlockSpec((s, LANES), lambda h, j: (0, cb + h))
    blk = lambda cb: pl.BlockSpec((t, LANES), lambda h, j: (j, cb + h))
    return pl.pallas_call(
        kern, name=name,
        out_shape=[jax.ShapeDtypeStruct((s, w), F32), jax.ShapeDtypeStruct((s, w), F32),
                   jax.ShapeDtypeStruct((s, w), F32), jax.ShapeDtypeStruct((nhp, 2, s), F32),
                   jax.ShapeDtypeStruct((s, w), F32)],
        grid=(nhp, nq),
        in_specs=[full(qb), blk(kb), blk(vb), full(0), full(0),
                  pl.BlockSpec((None, s, LANES), lambda h, j: (h, 0, 0)), full(0),
                  pl.BlockSpec((None, 2, s), lambda h, j: (h, 0, 0))],
        out_specs=[full(0), blk(0), blk(0), pl.BlockSpec((None, 2, t), lambda h, j: (h, 0, j)), full(0)],
        compiler_params=_params(dimension_semantics=("parallel", "arbitrary")),
    )(proj, proj, proj, do, o, lse, cumx, cumrow)


S5_STATES = 256
S5_ROWS = 512


def _cmul(ar, ai, br, bi):
    return ar * br - ai * bi, ar * bi + ai * br


def _scan_tables(lr, li, reverse):
    w = lr.shape[1]
    row = lax.broadcasted_iota(jnp.int32, (SUBLANES, w), 0)
    if reverse:
        row = SUBLANES - 1 - row
    lr1, li1 = jnp.broadcast_to(lr, (SUBLANES, w)), jnp.broadcast_to(li, (SUBLANES, w))
    lr2, li2 = _cmul(lr1, li1, lr1, li1)
    lr4, li4 = _cmul(lr2, li2, lr2, li2)
    steps = []
    for d, (pr, pi) in zip((1, 2, 4), ((lr1, li1), (lr2, li2), (lr4, li4))):
        keep = row >= d
        steps.append((jnp.where(keep, pr, 0.0), jnp.where(keep, pi, 0.0)))
    cr, ci = lr1, li1
    for bit, (pr, pi) in zip((1, 2, 4), ((lr1, li1), (lr2, li2), (lr4, li4))):
        nr, ni = _cmul(cr, ci, pr, pi)
        has = (row & bit) != 0
        cr, ci = jnp.where(has, nr, cr), jnp.where(has, ni, ci)
    return steps, (cr, ci)


def _scan_block(xr, xi, car_r, car_i, steps, carry_pow, reverse):
    for d, (pr, pi) in zip((1, 2, 4), steps):
        sh = (SUBLANES - d) if reverse else d
        sr, si = pltpu.roll(xr, sh, 0), pltpu.roll(xi, sh, 0)
        xr, xi = xr + (pr * sr - pi * si), xi + (pr * si + pi * sr)
    cr, ci = carry_pow
    xr, xi = xr + (cr * car_r - ci * car_i), xi + (cr * car_i + ci * car_r)
    return xr, xi


def _s5_specs(s, ncb):
    u_spec = pl.BlockSpec((s, LANES), lambda cb, hf: (0, cb))
    wb_spec = pl.BlockSpec((LANES, S5_STATES), lambda cb, hf: (cb, 2 * cb + hf))
    wc_spec = pl.BlockSpec((S5_STATES, LANES), lambda cb, hf: (2 * cb + hf, cb))
    lam_spec = pl.BlockSpec((1, S5_STATES), lambda cb, hf: (0, 2 * cb + hf))
    d_spec = pl.BlockSpec((1, LANES), lambda cb, hf: (0, cb))
    return u_spec, wb_spec, wc_spec, lam_spec, d_spec


def _s5_project_and_scan(u_ref, wbr_ref, wbi_ref, lr_ref, li_ref, xr_ref, xi_ref, s, rows):
    wbr, wbi = wbr_ref[...], wbi_ref[...]
    for r in range(s // rows):
        sl = pl.ds(r * rows, rows)
        ub = u_ref[sl, :].astype(BF)
        xr_ref[sl, :] = jnp.dot(ub, wbr, preferred_element_type=F32)
        xi_ref[sl, :] = jnp.dot(ub, wbi, preferred_element_type=F32)
    steps, cpow = _scan_tables(lr_ref[...], li_ref[...], False)

    def body(b, carry):
        car_r, car_i = carry
        sl = pl.ds(pl.multiple_of(b * SUBLANES, SUBLANES), SUBLANES)
        xr, xi = _scan_block(xr_ref[sl, :], xi_ref[sl, :], car_r, car_i, steps, cpow, False)
        xr_ref[sl, :] = xr
        xi_ref[sl, :] = xi
        return xr[SUBLANES - 1:SUBLANES, :], xi[SUBLANES - 1:SUBLANES, :]

    zero = jnp.zeros((1, S5_STATES), F32)
    lax.fori_loop(0, s // SUBLANES, body, (zero, zero), unroll=4)


def _s5_fwd(proj, wb_re, wb_im, wc_re, wc_im, lam_re, lam_im, dskip, *, name):
    s = proj.shape[0]
    w = dskip.shape[1]
    ncb = w // LANES
    rows = _fit(s, S5_ROWS, SUBLANES)

    def kern(u_ref, wbr_ref, wbi_ref, wcr_ref, wci_ref, lr_ref, li_ref, d_ref, y_ref, xr_ref, xi_ref):
        hf = pl.program_id(1)
        _s5_project_and_scan(u_ref, wbr_ref, wbi_ref, lr_ref, li_ref, xr_ref, xi_ref, s, rows)
        wcr, wci = wcr_ref[...], wci_ref[...]
        for r in range(s // rows):
            sl = pl.ds(r * rows, rows)
            y = (jnp.dot(xr_ref[sl, :].astype(BF), wcr, preferred_element_type=F32)
                 - jnp.dot(xi_ref[sl, :].astype(BF), wci, preferred_element_type=F32))

            @pl.when(hf == 0)
            def _(y=y, sl=sl):
                y_ref[sl, :] = y + d_ref[...] * u_ref[sl, :]

            @pl.when(hf == 1)
            def _(y=y, sl=sl):
                y_ref[sl, :] += y

    u_spec, wb_spec, wc_spec, lam_spec, d_spec = _s5_specs(s, ncb)
    return pl.pallas_call(
        kern, name=name, out_shape=jax.ShapeDtypeStruct((s, w), F32), grid=(ncb, 2),
        in_specs=[u_spec, wb_spec, wb_spec, wc_spec, wc_spec, lam_spec, lam_spec, d_spec],
        out_specs=u_spec,
        scratch_shapes=[pltpu.VMEM((s, S5_STATES), F32), pltpu.VMEM((s, S5_STATES), F32)],
        compiler_params=_params(dimension_semantics=("parallel", "arbitrary")),
    )(proj, wb_re, wb_im, wc_re, wc_im, lam_re, lam_im, dskip)


def _s5_bwd(proj, dy, wb_re, wb_im, wc_re, wc_im, lam_re, lam_im, dskip, *, name):
    s = proj.shape[0]
    w = dskip.shape[1]
    ncb = w // LANES
    rows = _fit(s, S5_ROWS, SUBLANES)
    tn_dims = (((0,), (0,)), ((), ()))
    nt_dims = (((1,), (1,)), ((), ()))

    def kern(u_ref, dy_ref, wbr_ref, wbi_ref, wcr_ref, wci_ref, lr_ref, li_ref, d_ref,
             du_ref, dwbr_ref, dwbi_ref, dwcr_ref, dwci_ref, dlr_ref, dli_ref, dd_ref,
             xr_ref, xi_ref, gr_ref, gi_ref):
        hf = pl.program_id(1)
        _s5_project_and_scan(u_ref, wbr_ref, wbi_ref, lr_ref, li_ref, xr_ref, xi_ref, s, rows)

        wcr, wci = wcr_ref[...], wci_ref[...]
        dwcr = jnp.zeros((S5_STATES, LANES), F32)
        dwci = jnp.zeros((S5_STATES, LANES), F32)
        ddsk = jnp.zeros((1, LANES), F32)
        for r in range(s // rows):
            sl = pl.ds(r * rows, rows)
            dyf = dy_ref[sl, :]
            dyb = dyf.astype(BF)
            gr_ref[sl, :] = lax.dot_general(dyb, wcr, nt_dims, preferred_element_type=F32)
            gi_ref[sl, :] = -lax.dot_general(dyb, wci, nt_dims, preferred_element_type=F32)
            dwcr = dwcr + lax.dot_general(xr_ref[sl, :].astype(BF), dyb, tn_dims, preferred_element_type=F32)
            dwci = dwci - lax.dot_general(xi_ref[sl, :].astype(BF), dyb, tn_dims, preferred_element_type=F32)
            ddsk = ddsk + jnp.sum(dyf * u_ref[sl, :], axis=0, keepdims=True)
        dwcr_ref[...] = dwcr
        dwci_ref[...] = dwci

        @pl.when(hf == 0)
        def _():
            dd_ref[...] = ddsk

        steps, cpow = _scan_tables(lr_ref[...], -li_ref[...], True)
        row = lax.broadcasted_iota(jnp.int32, (SUBLANES, S5_STATES), 0)
        nblk = s // SUBLANES

        def body(k, carry):
            car_r, car_i, ar, ai = carry
            b = nblk - 1 - k
            sl = pl.ds(pl.multiple_of(b * SUBLANES, SUBLANES), SUBLANES)
            g_r, g_i = _scan_block(gr_ref[sl, :], gi_ref[sl, :], car_r, car_i, steps, cpow, True)
            gr_ref[sl, :] = g_r
            gi_ref[sl, :] = g_i
            nr = jnp.where(row == SUBLANES - 1, car_r, pltpu.roll(g_r, SUBLANES - 1, 0))
            ni = jnp.where(row == SUBLANES - 1, car_i, pltpu.roll(g_i, SUBLANES - 1, 0))
            xr, xi = xr_ref[sl, :], xi_ref[sl, :]
            ar = ar + (xr * nr + xi * ni)
            ai = ai + (xr * ni - xi * nr)
            return g_r[0:1, :], g_i[0:1, :], ar, ai

        zero = jnp.zeros((1, S5_STATES), F32)
        zacc = jnp.zeros((SUBLANES, S5_STATES), F32)
        _, _, ar, ai = lax.fori_loop(0, nblk, body, (zero, zero, zacc, zacc), unroll=4)
        dlr_ref[...] = jnp.sum(ar, axis=0, keepdims=True)
        dli_ref[...] = jnp.sum(ai, axis=0, keepdims=True)

        wbr, wbi = wbr_ref[...], wbi_ref[...]
        dwbr = jnp.zeros((LANES, S5_STATES), F32)
        dwbi = jnp.zeros((LANES, S5_STATES), F32)
        for r in range(s // rows):
            sl = pl.ds(r * rows, rows)
            grb, gib = gr_ref[sl, :].astype(BF), gi_ref[sl, :].astype(BF)
            ub = u_ref[sl, :].astype(BF)
            dwbr = dwbr + lax.dot_general(ub, grb, tn_dims, preferred_element_type=F32)
            dwbi = dwbi + lax.dot_general(ub, gib, tn_dims, preferred_element_type=F32)
            du = (lax.dot_general(grb, wbr, nt_dims, preferred_element_type=F32)
                  + lax.dot_general(gib, wbi, nt_dims, preferred_element_type=F32))

            @pl.when(hf == 0)
            def _(du=du, sl=sl):
                du_ref[sl, :] = du + d_ref[...] * dy_ref[sl, :]

            @pl.when(hf == 1)
            def _(du=du, sl=sl):
                du_ref[sl, :] += du
        dwbr_ref[...] = dwbr
        dwbi_ref[...] = dwbi

    u_spec, wb_spec, wc_spec, lam_spec, d_spec = _s5_specs(s, ncb)
    dwb_spec = pl.BlockSpec((None, None, LANES, S5_STATES), lambda cb, hf: (cb, hf, 0, 0))
    dwc_spec = pl.BlockSpec((None, None, S5_STATES, LANES), lambda cb, hf: (cb, hf, 0, 0))
    state = pltpu.VMEM((s, S5_STATES), F32)
    return pl.pallas_call(
        kern, name=name,
        out_shape=[jax.ShapeDtypeStruct((s, w), F32),
                   jax.ShapeDtypeStruct((ncb, 2, LANES, S5_STATES), F32), jax.ShapeDtypeStruct((ncb, 2, LANES, S5_STATES), F32),
                   jax.ShapeDtypeStruct((ncb, 2, S5_STATES, LANES), F32), jax.ShapeDtypeStruct((ncb, 2, S5_STATES, LANES), F32),
                   jax.ShapeDtypeStruct((1, 4 * w), F32), jax.ShapeDtypeStruct((1, 4 * w), F32),
                   jax.ShapeDtypeStruct((1, w), F32)],
        grid=(ncb, 2),
        in_specs=[u_spec, u_spec, wb_spec, wb_spec, wc_spec, wc_spec, lam_spec, lam_spec, d_spec],
        out_specs=[u_spec, dwb_spec, dwb_spec, dwc_spec, dwc_spec, lam_spec, lam_spec, d_spec],
        scratch_shapes=[state, state, state, state],
        compiler_params=_params(dimension_semantics=("parallel", "arbitrary")),
    )(proj, dy, wb_re, wb_im, wc_re, wc_im, lam_re, lam_im, dskip)


def _s5_discretise(lam_re, lam_im, log_dt, b_re, b_im):
    lr = jnp.minimum(lam_re, -EIG_CLIP)
    li = lam_im
    dt = jnp.exp(log_dt)[:, None]
    mag = jnp.exp(lr * dt)
    lbr, lbi = mag * jnp.cos(li * dt), mag * jnp.sin(li * dt)
    den = lr * lr + li * li
    fr = ((lbr - 1.0) * lr + lbi * li) / den
    fi = (lbi * lr - (lbr - 1.0) * li) / den
    bbr = fr[..., None] * b_re - fi[..., None] * b_im
    bbi = fr[..., None] * b_im + fi[..., None] * b_re
    return lbr, lbi, bbr, bbi


def _block_diag(blocks):
    g, a, b = blocks.shape
    eye = jnp.eye(g, dtype=blocks.dtype)
    return (blocks[:, :, None, :] * eye[:, None, :, None]).reshape(g * a, g * b)


def _s5_block_grads(dwb, a, b, transpose):
    ncb = dwb.shape[0]
    gl = LANES // 2 // (a if not transpose else b)
    if not transpose:
        d = dwb.reshape(ncb, 2, 2, gl, a, gl, b)
        parts = [[d[:, hf, hf, g, :, g, :] for g in range(gl)] for hf in range(2)]
    else:
        d = dwb.reshape(ncb, 2, gl, a, 2, gl, b)
        parts = [[d[:, hf, g, :, hf, g, :] for g in range(gl)] for hf in range(2)]
    st = jnp.stack([jnp.stack(p, axis=1) for p in parts], axis=1)
    return st.reshape(ncb * 2 * gl, a, b)


def _adamw(parts, w, m, v, *, name):
    npart, r, c = parts.shape
    row_bytes = 4 * (-(-c // LANES) * LANES)
    tr = _fit(r, max(SUBLANES, VMEM_LIMIT // 2 // (2 * (npart + 7) * row_bytes) // SUBLANES * SUBLANES), SUBLANES)
    c1 = 1.0 / (1.0 - ADAM_B1 ** ADAM_STEP)
    c2 = 1.0 / (1.0 - ADAM_B2 ** ADAM_STEP)

    def kern(p_ref, w_ref, m_ref, v_ref, g_ref, d_ref, nm_ref, nv_ref):
        g = p_ref[0]
        for q in range(1, npart):
            g = g + p_ref[q]
        m2 = ADAM_B1 * m_ref[...] + (1.0 - ADAM_B1) * g
        v2 = ADAM_B2 * v_ref[...] + (1.0 - ADAM_B2) * (g * g)
        upd = (m2 * c1) / (jnp.sqrt(v2 * c2) + ADAM_EPS) + ADAM_WD * w_ref[...]
        g_ref[...] = g
        d_ref[...] = -ADAM_LR * upd
        nm_ref[...] = m2
        nv_ref[...] = v2

    spec = pl.BlockSpec((tr, c), lambda i: (i, 0))
    return pl.pallas_call(
        kern, name=name, out_shape=[jax.ShapeDtypeStruct((r, c), F32)] * 4, grid=(r // tr,),
        in_specs=[pl.BlockSpec((npart, tr, c), lambda i: (0, i, 0)), spec, spec, spec],
        out_specs=[spec] * 4,
        compiler_params=_params(dimension_semantics=("parallel",)),
    )(parts, w, m, v)


def _sum_parts(parts, *, name):
    npart, r, c = parts.shape

    def kern(p_ref, o_ref):
        g = p_ref[0]
        for q in range(1, npart):
            g = g + p_ref[q]
        o_ref[...] = g

    return pl.pallas_call(kern, name=name, out_shape=jax.ShapeDtypeStruct((r, c), F32), compiler_params=_params())(parts)


def _exchange(arrays, gather, *, name):
    n = len(arrays)
    out_shape = [jax.ShapeDtypeStruct(((NDEV,) + a.shape) if gather else a.shape, a.dtype) for a in arrays]

    def kern(*refs):
        srcs, dsts = refs[:n], refs[n:2 * n]
        send_sems, recv_sems, local_sems = refs[2 * n:]
        x, y, c = lax.axis_index("x"), lax.axis_index("y"), lax.axis_index("c")
        me = 4 * x + 2 * y + c
        copies = []
        for a in range(n):
            src_me = srcs[a] if gather else srcs[a].at[me]
            loc = pltpu.make_async_copy(src_me, dsts[a].at[me], local_sems.at[a])
            loc.start()
            copies.append(loc)
        remote = []
        for k in (1, 2, 4, 3, 5, 6, 7):
            px, py, pc = x ^ ((k >> 2) & 1), y ^ ((k >> 1) & 1), c ^ (k & 1)
            peer = 4 * px + 2 * py + pc
            for a in range(n):
                cp = pltpu.make_async_remote_copy(
                    src_ref=srcs[a] if gather else srcs[a].at[peer],
                    dst_ref=dsts[a].at[me],
                    send_sem=send_sems.at[a, k - 1], recv_sem=recv_sems.at[a, k - 1],
                    device_id=(px, py, pc), device_id_type=pl.DeviceIdType.MESH)
                cp.start()
                rv = pltpu.make_async_remote_copy(
                    src_ref=srcs[a] if gather else srcs[a].at[peer],
                    dst_ref=dsts[a].at[peer],
                    send_sem=send_sems.at[a, k - 1], recv_sem=recv_sems.at[a, k - 1],
                    device_id=(px, py, pc), device_id_type=pl.DeviceIdType.MESH)
                remote.append((cp, rv))
        for cp, rv in remote:
            cp.wait_send()
            rv.wait_recv()
        for loc in copies:
            loc.wait()

    hbm = pl.BlockSpec(memory_space=pl.ANY)
    return pl.pallas_call(
        kern, name=name, out_shape=out_shape, in_specs=[hbm] * n, out_specs=[hbm] * n,
        scratch_shapes=[pltpu.SemaphoreType.DMA((n, NDEV - 1)), pltpu.SemaphoreType.DMA((n, NDEV - 1)),
                        pltpu.SemaphoreType.DMA((n,))],
        compiler_params=pltpu.CompilerParams(has_side_effects=True),
    )(*arrays)


def _pack(arrays):
    flat = jnp.concatenate([a.reshape(-1).astype(F32) for a in arrays])
    pad = (-flat.shape[0]) % (SUBLANES * LANES)
    return jnp.pad(flat, (0, pad)).reshape(-1, LANES)


def _unpack(buf, like):
    flat = buf.reshape(-1)
    out, off = [], 0
    for a in like:
        sz = math.prod(a.shape)
        out.append(flat[off:off + sz].reshape(a.shape))
        off += sz
    return out


def _row(v):
    return v.reshape(1, -1)


def _layer_fwd(x, mod, p, l):
    s, d = x.shape
    sw = d // 2
    nh = d // LANES
    shift_m, scale_m, gate_m, shift_f, scale_f, gate_f = mod
    n = lambda tag: f"{tag}{l}"
    sv = {}

    h1, = _rowwise(lambda xv, g, sc, sh: (xv * _rms(xv) * g) * (1.0 + sc) + sh,
                   [x], [p['g_pre_mix'], scale_m, shift_m], [(d, BF, 'tile')], name=n("pre_mix"))
    proj_a = _mm(h1, p['w_in_a'], name=n("proj_a"))
    flog = _mm(h1, p['w_in_f'], name=n("proj_f"))
    gates = _mm(h1, p['w_in_g'], name=n("proj_g"))

    y_s5 = _s5_fwd(proj_a, p['wb_re'], p['wb_im'], p['wc_re'], p['wc_im'], p['lamb_re'], p['lamb_im'], p['d_skip'],
                   name=n("s5_fwd"))
    z, = _rowwise(_gelu, [y_s5], [], [(sw, BF, 'tile')], name=n("gelu"))
    tglu = _mm(z, p['w_glu'], name=n("glu_mm"))
    ys, = _rowwise(lambda yv, tv, b: _gelu(yv) * _sigmoid(tv + b), [y_s5, tglu], [p['b_glu']], [(sw, BF, 'tile')],
                   name=n("glu"))

    cumx = _cum_fwd(flog, p['b_f_row'], nh, name=n("cum_fwd"))
    cumrow = cumx[:, ::HEAD_DIM].T.reshape(nh // 2, 2, s)
    ya, lse = _attn_fwd(proj_a, cumx, cumrow, sw, name=n("attn_fwd"))

    am = _mm(ys, p['w_pa'], name=n("pa_mm"))
    bm = _mm(ya, p['w_pb'], name=n("pb_mm"))
    merged, = _rowwise(lambda a, b, ga, gb: _sigmoid(ga) * a + _sigmoid(gb) * b,
                       [am, bm, (gates, d, 0), (gates, d, 1)], [], [(d, BF, 'tile')], name=n("merge"))
    ym = _mm(merged, p['w_o'], name=n("o_mm"))
    x2, = _rowwise(lambda xv, yv, g, gt: xv + gt * (yv * _rms(yv) * g),
                   [x, ym], [p['g_post_mix'], gate_m], [(d, F32, 'tile')], name=n("post_mix"))

    h2, = _rowwise(lambda xv, g, sc, sh: (xv * _rms(xv) * g) * (1.0 + sc) + sh,
                   [x2], [p['g_pre_ffn'], scale_f, shift_f], [(d, BF, 'tile')], name=n("pre_ffn"))
    gt = _mm(h2, p['w_ffn_gate'], name=n("gate_mm"))
    up = _mm(h2, p['w_ffn_up'], name=n("up_mm"))
    dff = gt.shape[1]
    act, = _rowwise(lambda g, u: _silu(g) * u, [gt, up], [], [(dff, BF, 'tile')], name=n("swiglu"))
    yf = _mm(act, p['w_ffn_down'], name=n("down_mm"))
    x3, = _rowwise(lambda xv, yv, g, gt_: xv + gt_ * (yv * _rms(yv) * g),
                   [x2, yf], [p['g_post_ffn'], gate_f], [(d, F32, 'tile')], name=n("post_ffn"))

    sv.update(x=x, h1=h1, proj_a=proj_a, flog=flog, gates=gates, y_s5=y_s5, z=z, tglu=tglu, ys=ys, cumx=cumx,
              cumrow=cumrow, ya=ya, lse=lse, am=am, bm=bm, merged=merged, ym=ym, x2=x2, h2=h2, gt=gt, up=up,
              act=act, yf=yf)
    return x3, sv


def _layer_bwd(dx3, sv, mod, p, l):
    x, x2 = sv['x'], sv['x2']
    s, d = x.shape
    sw = d // 2
    nh = d // LANES
    shift_m, scale_m, gate_m, shift_f, scale_f, gate_f = mod
    n = lambda tag: f"{tag}{l}"
    gw, gs = {}, {}

    def post_bwd(dxo, yv, g, gate):
        r = _rms(yv)
        nf = yv * r
        dn = dxo * gate * g
        return _norm_bwd(dn, nf, r), dxo * (nf * g), dxo * gate * nf

    def pre_bwd(dh, dres, xv, g, sc):
        r = _rms(xv)
        xh = xv * r
        n3 = xh * g
        dn3 = dh * (1.0 + sc)
        return dres + _norm_bwd(dn3 * g, xh, r), dh, dh * n3, dn3 * xh

    dyf, dgate_f, gs['g_post_ffn'] = _rowwise(
        post_bwd, [dx3, sv['yf']], [p['g_post_ffn'], gate_f],
        [(d, BF, 'tile'), (d, F32, 'sum'), (d, F32, 'sum')], name=n("post_ffn_bwd"))
    dff = sv['gt'].shape[1]
    dact = _mm(dyf, p['w_ffn_down'], tb=True, name=n("down_bwd_x"))
    gw['w_ffn_down'] = _mm(sv['act'], dyf, ta=True, name=n("down_bwd_w"))

    def swiglu_bwd(da, g, u):
        sg = _sigmoid(g)
        return da * u * (sg * (1.0 + g * (1.0 - sg))), da * (g * sg)

    dgt, dup = _rowwise(swiglu_bwd, [dact, sv['gt'], sv['up']], [], [(dff, BF, 'tile'), (dff, BF, 'tile')],
                        name=n("swiglu_bwd"))
    dh2a = _mm(dgt, p['w_ffn_gate'], tb=True, name=n("gate_bwd_x"))
    dh2b = _mm(dup, p['w_ffn_up'], tb=True, name=n("up_bwd_x"))
    gw['w_ffn_gate'] = _mm(sv['h2'], dgt, ta=True, name=n("gate_bwd_w"))
    gw['w_ffn_up'] = _mm(sv['h2'], dup, ta=True, name=n("up_bwd_w"))
    dx2, dshift_f, dscale_f, gs['g_pre_ffn'] = _rowwise(
        lambda da, db, dres, xv, g, sc: pre_bwd(da + db, dres, xv, g, sc),
        [dh2a, dh2b, dx3, x2], [p['g_pre_ffn'], scale_f],
        [(d, F32, 'tile'), (d, F32, 'sum'), (d, F32, 'sum'), (d, F32, 'sum')], name=n("pre_ffn_bwd"))

    dym, dgate_m, gs['g_post_mix'] = _rowwise(
        post_bwd, [dx2, sv['ym']], [p['g_post_mix'], gate_m],
        [(d, BF, 'tile'), (d, F32, 'sum'), (d, F32, 'sum')], name=n("post_mix_bwd"))
    dmerged = _mm(dym, p['w_o'], tb=True, name=n("o_bwd_x"))
    gw['w_o'] = _mm(sv['merged'], dym, ta=True, name=n("o_bwd_w"))

    def merge_bwd(dm, a, b, ga, gb):
        sa, sb = _sigmoid(ga), _sigmoid(gb)
        return dm * sa, dm * sb, dm * a * sa * (1.0 - sa), dm * b * sb * (1.0 - sb)

    da_, db_, dga, dgb = _rowwise(
        merge_bwd, [dmerged, sv['am'], sv['bm'], (sv['gates'], d, 0), (sv['gates'], d, 1)], [],
        [(d, BF, 'tile')] * 4, name=n("merge_bwd"))
    dys = _mm(da_, p['w_pa'], tb=True, name=n("pa_bwd_x"))
    gw['w_pa'] = _mm(sv['ys'], da_, ta=True, name=n("pa_bwd_w"))
    dya = _mm(db_, p['w_pb'], tb=True, name=n("pb_bwd_x"))
    gw['w_pb'] = _mm(sv['ya'], db_, ta=True, name=n("pb_bwd_w"))

    dq, dk, dv, dcrow, dcq = _attn_bwd(sv['proj_a'], dya, sv['ya'], sv['lse'], sv['cumx'], sv['cumrow'], sw,
                                       name=n("attn_bwd"))
    frow = sv['flog'][:, :nh].T
    dcum = jnp.stack([dcrow.reshape(nh, s), dcq[:, ::HEAD_DIM].T])
    dfrow, dbf = _cum_bwd(dcum, frow, p['b_f_col'], name=n("cum_bwd"))
    gs['b_f'] = dbf.reshape(nh)
    dflog = jnp.pad(dfrow.T, ((0, 0), (0, LANES - nh))).astype(BF)

    def glu_bwd(dy_, yv, tv, b):
        zv = _gelu(yv)
        sg = _sigmoid(tv + b)
        dt = dy_ * zv * sg * (1.0 - sg)
        return dt, dy_ * sg, dt

    dt, dz1, gs['b_glu'] = _rowwise(glu_bwd, [dys, sv['y_s5'], sv['tglu']], [p['b_glu']],
                                    [(sw, BF, 'tile'), (sw, F32, 'tile'), (sw, F32, 'sum')], name=n("glu_bwd"))
    dz2 = _mm(dt, p['w_glu'], tb=True, name=n("glu_bwd_x"))
    gw['w_glu'] = _mm(sv['z'], dt, ta=True, name=n("glu_bwd_w"))
    dy_s5, = _rowwise(lambda a, b, yv: (a + b) * _gelu_grad(yv), [dz1, dz2, sv['y_s5']], [], [(sw, F32, 'tile')],
                      name=n("gelu_bwd"))
    du, dwbr, dwbi, dwcr, dwci, dlr, dli, gs['d_skip'] = _s5_bwd(
        sv['proj_a'], dy_s5, p['wb_re'], p['wb_im'], p['wc_re'], p['wc_im'], p['lamb_re'], p['lamb_im'], p['d_skip'],
        name=n("s5_bwd"))
    g_ = sw // SSM_H
    pst = p['lamb_re'].shape[1] // g_
    gs['lamb_re'], gs['lamb_im'] = dlr.reshape(g_, pst), dli.reshape(g_, pst)
    gs['bbar_re'] = _s5_block_grads(dwbr, SSM_H, pst, False).transpose(0, 2, 1)
    gs['bbar_im'] = _s5_block_grads(dwbi, SSM_H, pst, False).transpose(0, 2, 1)
    gs['c_re'] = _s5_block_grads(dwcr, pst, SSM_H, True).transpose(0, 2, 1)
    gs['c_im'] = _s5_block_grads(dwci, pst, SSM_H, True).transpose(0, 2, 1)

    dproj = jnp.concatenate([du.astype(BF), dq.astype(BF), dk.astype(BF), dv.astype(BF), dflog, dga, dgb], axis=1)
    dh1 = _mm(dproj, p['w_in_all'], tb=True, tk=1408, name=n("proj_bwd_x"))
    gw['w_in'] = _mm(sv['h1'], dproj, ta=True, name=n("proj_bwd_w"))
    dx, dshift_m, dscale_m, gs['g_pre_mix'] = _rowwise(
        pre_bwd, [dh1, dx2, x], [p['g_pre_mix'], scale_m],
        [(d, F32, 'tile'), (d, F32, 'sum'), (d, F32, 'sum'), (d, F32, 'sum')], name=n("pre_mix_bwd"))
    dmod = [dshift_m, dscale_m, dgate_m, dshift_f, dscale_f, dgate_f]
    return dx, gw, dmod, gs


def _prep_layer(wfull, small, l):
    d = wfull['w_o'].shape[0]
    sw = d // 2
    nh = d // LANES
    p = dict(wfull)
    w_in = wfull['w_in']
    fcol = sw + 3 * sw
    p['w_in_a'] = w_in[:, :fcol]
    p['w_in_f'] = jnp.pad(w_in[:, fcol:fcol + nh], ((0, 0), (0, LANES - nh)))
    p['w_in_g'] = w_in[:, fcol + nh:]
    p['w_in_all'] = jnp.concatenate([p['w_in_a'], p['w_in_f'], p['w_in_g']], axis=1)
    for k in ('g_pre_mix', 'g_post_mix', 'g_pre_ffn', 'g_post_ffn', 'd_skip', 'b_glu'):
        p[k] = _row(small[k])
    p['b_f_row'] = jnp.pad(_row(small['b_f']), ((0, 0), (0, LANES - nh)))
    p['b_f_col'] = small['b_f'].reshape(nh, 1)
    lbr, lbi, bbr, bbi = _s5_discretise(small['lam_re'], small['lam_im'], small['log_dt'], small['b_re'], small['b_im'])
    p['lamb_re'], p['lamb_im'] = _row(lbr), _row(lbi)
    p['wb_re'] = _block_diag(bbr.transpose(0, 2, 1)).astype(BF)
    p['wb_im'] = _block_diag(bbi.transpose(0, 2, 1)).astype(BF)
    p['wc_re'] = _block_diag(small['c_re'].transpose(0, 2, 1)).astype(BF)
    p['wc_im'] = _block_diag(small['c_im'].transpose(0, 2, 1)).astype(BF)
    return p


def _local_step(x, target, mods, wfull, small):
    depth = len(wfull)
    s, d = x.shape
    ps = [_prep_layer(wfull[l], small[l], l) for l in range(depth)]
    saved = []
    h = x
    for l in range(depth):
        h, sv = _layer_fwd(h, mods[l], ps[l], l)
        saved.append(sv)

    def loss_fn(yv, tv):
        e = yv - tv
        return e * (1.0 / d), jnp.sum(e * e, axis=1, keepdims=True) * (0.5 / d)

    dy, loss = _rowwise(loss_fn, [h, target], [], [(d, F32, 'tile'), (1, F32, 'sum')], name="loss")
    gws, dmods, gss = [None] * depth, [None] * depth, [None] * depth
    for l in range(depth - 1, -1, -1):
        dy, gws[l], dmods[l], gs = _layer_bwd(dy, saved[l], mods[l], ps[l], l)
        sm = small[l]
        _, vjp = jax.vjp(_s5_discretise, sm['lam_re'], sm['lam_im'], sm['log_dt'], sm['b_re'], sm['b_im'])
        gs['lam_re'], gs['lam_im'], gs['log_dt'], gs['b_re'], gs['b_im'] = vjp(
            (gs.pop('lamb_re'), gs.pop('lamb_im'), gs.pop('bbar_re'), gs.pop('bbar_im')))
        gss[l] = gs
    return loss, dy, gws, dmods, gss


SMALL_LOCAL = ['g_pre_mix', 'g_post_mix', 'g_pre_ffn', 'g_post_ffn', 'lam_re', 'lam_im', 'log_dt', 'b_re', 'b_im',
               'c_re', 'c_im', 'd_skip', 'b_glu', 'b_f']


def kernel(x, c, w_ada, b_ada, g_pre_mix, g_post_mix, g_pre_ffn, g_post_ffn, w_in, lam_re, lam_im, log_dt, b_re, b_im, c_re, c_im, d_skip, w_glu, b_glu, b_f, w_pa, w_pb, w_o, w_ffn_gate, w_ffn_up, w_ffn_down, loss_target, m_w_ada, m_b_ada, m_g_pre_mix, m_g_post_mix, m_g_pre_ffn, m_g_post_ffn, m_w_in, m_lam_re, m_lam_im, m_log_dt, m_b_re, m_b_im, m_c_re, m_c_im, m_d_skip, m_w_glu, m_b_glu, m_b_f, m_w_pa, m_w_pb, m_w_o, m_w_ffn_gate, m_w_ffn_up, m_w_ffn_down, v_w_ada, v_b_ada, v_g_pre_mix, v_g_post_mix, v_g_pre_ffn, v_g_post_ffn, v_w_in, v_lam_re, v_lam_im, v_log_dt, v_b_re, v_b_im, v_c_re, v_c_im, v_d_skip, v_w_glu, v_b_glu, v_b_f, v_w_pa, v_w_pb, v_w_o, v_w_ffn_gate, v_w_ffn_up, v_w_ffn_down):
    args = dict(locals())
    W = {k: args[k] for k in WEIGHTS}
    M = {k: args['m_' + k] for k in WEIGHTS}
    V = {k: args['v_' + k] for k in WEIGHTS}
    depth, d, ncol = w_ada.shape
    s = x.shape[1]
    me = 4 * lax.axis_index("x") + 2 * lax.axis_index("y") + lax.axis_index("c")

    c_all, = _exchange([jnp.pad(c, ((0, SUBLANES - 1), (0, 0)))], True, name="gather_c")
    c_all = c_all[:, 0, :]
    cond, = _rowwise(_silu, [c_all], [], [(d, F32, 'tile')], name="cond")
    mod_part = jnp.stack([_mm(cond, w_ada[l], name=f"ada_mm{l}") for l in range(depth)], axis=1)
    mod_recv, = _exchange([mod_part.reshape(NDEV, depth, 1, ncol)], False, name="scatter_mod")
    mod_cat = mod_recv.reshape(NDEV, depth, ncol).transpose(1, 0, 2).reshape(depth, NDEV * ncol)
    mod, = _rowwise(lambda a, b: a + b, [mod_cat, b_ada], [], [(NDEV * ncol, F32, 'tile')], name="mod_bias")
    mods = [[mod[l:l + 1, i * d:(i + 1) * d] for i in range(6)] for l in range(depth)]

    gathered = _exchange([W[k].astype(BF) for k in BIG], True, name="gather_w")
    wfull = [dict() for _ in range(depth)]
    for k, g in zip(BIG, gathered):
        for l in range(depth):
            gl = g[:, l]
            if k in COL_SHARDED:
                wfull[l][k] = gl.transpose(1, 0, 2).reshape(gl.shape[1], NDEV * gl.shape[2])
            else:
                wfull[l][k] = gl.reshape(NDEV * gl.shape[1], gl.shape[2])
    small = [{k: W[k][l] for k in SMALL_LOCAL} for l in range(depth)]

    loss, dx, gws, dmods, gss = _local_step(x[0], loss_target[0], mods, wfull, small)
    loss = lax.psum(loss[0, 0], ("x", "y", "c"))

    nh = d // LANES
    fcol = 2 * d
    slabs = []
    for k in BIG:
        per_layer = []
        for l in range(depth):
            g = gws[l][k]
            if k == 'w_in':
                g = jnp.concatenate([g[:, :fcol + nh], g[:, fcol + LANES:]], axis=1)
            if k in COL_SHARDED:
                g = g.reshape(g.shape[0], NDEV, g.shape[1] // NDEV).transpose(1, 0, 2)
            else:
                g = g.reshape(NDEV, g.shape[0] // NDEV, g.shape[1])
            per_layer.append(g)
        slabs.append(jnp.stack(per_layer, axis=1))
    received = _exchange(slabs, False, name="scatter_grads")
    out = {}
    for k, r in zip(BIG, received):
        shp = W[k].shape
        flat = lambda a: a.reshape(shp[0] * shp[1], shp[2])
        res = _adamw(r.reshape(NDEV, shp[0] * shp[1], shp[2]), flat(W[k]), flat(M[k]), flat(V[k]), name=f"adamw_{k}")
        out[k] = [a.reshape(shp) for a in res]

    dmod_mine = jnp.stack([jnp.concatenate(dmods[l], axis=1)[0] for l in range(depth)])
    small_mine = [dmod_mine] + [jnp.stack([gss[l][k] for l in range(depth)]) for k in SMALL_LOCAL]
    parts, = _exchange([_pack(small_mine)], True, name="gather_small")
    summed = _sum_parts(parts, name="sum_small")
    names = ['b_ada'] + SMALL_LOCAL
    grads = _unpack(summed, [W[k] for k in names])
    res = _adamw(_pack(grads)[None], _pack([W[k] for k in names]), _pack([M[k] for k in names]),
                 _pack([V[k] for k in names]), name="adamw_small")
    unpacked = [_unpack(a, [W[k] for k in names]) for a in res]
    for i, k in enumerate(names):
        out[k] = [unpacked[j][i] for j in range(4)]

    dmod_all = parts.reshape(NDEV, -1)[:, :depth * 6 * d].reshape(NDEV, depth, 6 * d)
    dmod_cols = lax.dynamic_slice_in_dim(dmod_all, me * ncol, ncol, axis=2)
    g_ada = jnp.stack([_mm(cond, dmod_cols[:, l], ta=True, precision=HI, name=f"ada_bwd{l}") for l in range(depth)])
    flat = lambda a: a.reshape(depth * d, ncol)
    res = _adamw(flat(g_ada)[None], flat(w_ada), flat(m_w_ada), flat(v_w_ada), name="adamw_w_ada")
    out['w_ada'] = [a.reshape(w_ada.shape) for a in res]

    return (loss, dx[None], *[out[k][0] for k in WEIGHTS], *[out[k][1] for k in WEIGHTS],
            *[out[k][2] for k in WEIGHTS], *[out[k][3] for k in WEIGHTS])
```

```python
import functools
import math

import jax
import jax.numpy as jnp
from jax import lax
from jax.experimental import pallas as pl
from jax.experimental.pallas import tpu as pltpu

F32 = jnp.float32
BF = jnp.bfloat16
NDEV = 8
LANES = 128
SUBLANES = 8
VMEM_LIMIT = 48 * 1024 * 1024

SSM_H = 16
HEAD_DIM = 64
RMS_EPS = 1e-6
EIG_CLIP = 1e-4
ADAM_LR = 0.001
ADAM_B1 = 0.9
ADAM_B2 = 0.999
ADAM_EPS = 1e-08
ADAM_WD = 0.01
ADAM_STEP = 10
NEG = -1e30
HI = lax.Precision.HIGHEST

WEIGHTS = ['w_ada', 'b_ada', 'g_pre_mix', 'g_post_mix', 'g_pre_ffn', 'g_post_ffn', 'w_in', 'lam_re', 'lam_im',
           'log_dt', 'b_re', 'b_im', 'c_re', 'c_im', 'd_skip', 'w_glu', 'b_glu', 'b_f', 'w_pa', 'w_pb', 'w_o',
           'w_ffn_gate', 'w_ffn_up', 'w_ffn_down']
COL_SHARDED = ['w_in', 'w_pa', 'w_pb', 'w_ffn_gate', 'w_ffn_up']
ROW_SHARDED = ['w_glu', 'w_o', 'w_ffn_down']
BIG = COL_SHARDED + ROW_SHARDED
SMALL = ['b_ada', 'g_pre_mix', 'g_post_mix', 'g_pre_ffn', 'g_post_ffn', 'lam_re', 'lam_im', 'log_dt', 'b_re',
         'b_im', 'c_re', 'c_im', 'd_skip', 'b_glu', 'b_f']


def _fit(dim, target, align):
    if dim <= target:
        return dim
    t = (target // align) * align
    while t >= align:
        if dim % t == 0:
            return t
        t -= align
    return dim


def _params(**kw):
    return pltpu.CompilerParams(vmem_limit_bytes=VMEM_LIMIT, **kw)


def _mm(a, b, *, ta=False, tb=False, out_dtype=F32, tm=512, tn=512, tk=2048, precision=None, name):
    m, k = (a.shape[1], a.shape[0]) if ta else a.shape
    n = b.shape[0] if tb else b.shape[1]
    assert (b.shape[1] if tb else b.shape[0]) == k
    tm = _fit(m, tm, LANES if ta else 16)
    tn = _fit(n, tn, LANES)
    tk = _fit(k, tk, LANES)
    nk = k // tk
    dims = (((0 if ta else 1,), (1 if tb else 0,)), ((), ()))

    def kern(a_ref, b_ref, o_ref, *scratch):
        av, bv = a_ref[...], b_ref[...]
        if precision is None:
            av, bv = av.astype(BF), bv.astype(BF)
        p = lax.dot_general(av, bv, dims, preferred_element_type=F32, precision=precision)
        if nk == 1:
            o_ref[...] = p.astype(out_dtype)
            return
        acc_ref, = scratch
        kk = pl.program_id(2)

        @pl.when(kk == 0)
        def _():
            acc_ref[...] = p

        @pl.when(kk > 0)
        def _():
            acc_ref[...] += p

        @pl.when(kk == nk - 1)
        def _():
            o_ref[...] = acc_ref[...].astype(out_dtype)

    a_spec = pl.BlockSpec((tk, tm), lambda i, j, kk: (kk, i)) if ta else pl.BlockSpec((tm, tk), lambda i, j, kk: (i, kk))
    b_spec = pl.BlockSpec((tn, tk), lambda i, j, kk: (j, kk)) if tb else pl.BlockSpec((tk, tn), lambda i, j, kk: (kk, j))
    return pl.pallas_call(
        kern, name=name,
        out_shape=jax.ShapeDtypeStruct((m, n), out_dtype),
        grid=(m // tm, n // tn, nk),
        in_specs=[a_spec, b_spec],
        out_specs=pl.BlockSpec((tm, tn), lambda i, j, kk: (i, j)),
        scratch_shapes=[] if nk == 1 else [pltpu.VMEM((tm, tn), F32)],
        compiler_params=_params(dimension_semantics=("parallel", "parallel", "arbitrary")),
    )(a, b)


def _rowwise(fn, tiles, params, outs, *, tr=256, name):
    tiles = [t if isinstance(t, tuple) else (t, t.shape[1], 0) for t in tiles]
    s = tiles[0][0].shape[0]
    tr = _fit(s, tr, 16)
    nt, npar = len(tiles), len(params)

    def kern(*refs):
        i = pl.program_id(0)
        res = fn(*[r[...] for r in refs[:nt + npar]])
        if not isinstance(res, (tuple, list)):
            res = (res,)
        for (w, dt, kind), o_ref, r in zip(outs, refs[nt + npar:], res):
            if kind == 'tile':
                o_ref[...] = r.astype(dt)
            else:
                part = jnp.sum(r.astype(F32), axis=0, keepdims=True)

                @pl.when(i == 0)
                def _(o_ref=o_ref, part=part):
                    o_ref[...] = part

                @pl.when(i > 0)
                def _(o_ref=o_ref, part=part):
                    o_ref[...] += part

    def tile_spec(w, cb):
        return pl.BlockSpec((tr, w), lambda i: (i, cb))

    in_specs = [tile_spec(w, cb) for _, w, cb in tiles]
    in_specs += [pl.BlockSpec(p.shape, lambda i, nd=p.ndim: (0,) * nd) for p in params]
    out_shape, out_specs = [], []
    for w, dt, kind in outs:
        if kind == 'tile':
            out_shape.append(jax.ShapeDtypeStruct((s, w), dt))
            out_specs.append(pl.BlockSpec((tr, w), lambda i: (i, 0)))
        else:
            out_shape.append(jax.ShapeDtypeStruct((1, w), F32))
            out_specs.append(pl.BlockSpec((1, w), lambda i: (0, 0)))
    res = pl.pallas_call(
        kern, name=name, out_shape=out_shape, grid=(s // tr,), in_specs=in_specs, out_specs=out_specs,
        compiler_params=_params(dimension_semantics=("arbitrary",)),
    )(*[t[0] for t in tiles], *params)
    return res


def _sigmoid(z):
    return 1.0 / (1.0 + jnp.exp(-z))


def _silu(z):
    return z * _sigmoid(z)


_GELU_K = math.sqrt(2.0 / math.pi)


def _gelu(y):
    return 0.5 * y * (1.0 + jnp.tanh(_GELU_K * (y + 0.044715 * y * y * y)))


def _gelu_grad(y):
    th = jnp.tanh(_GELU_K * (y + 0.044715 * y * y * y))
    return 0.5 * (1.0 + th) + 0.5 * y * (1.0 - th * th) * _GELU_K * (1.0 + 3.0 * 0.044715 * y * y)


def _rms(x):
    return lax.rsqrt(jnp.mean(x * x, axis=-1, keepdims=True) + RMS_EPS)


def _norm_bwd(dn, xhat, r):
    return r * (dn - xhat * jnp.mean(dn * xhat, axis=-1, keepdims=True))


def _cum_fwd(flog, bf_row, nh, *, name):
    s = flog.shape[0]
    w = nh * HEAD_DIM
    t = _fit(s, 256, SUBLANES)

    def kern(f_ref, b_ref, o_ref, carry_ref):
        i = pl.program_id(0)

        @pl.when(i == 0)
        def _():
            carry_ref[...] = jnp.zeros_like(carry_ref)

        z = f_ref[...] + b_ref[...]
        logf = jnp.minimum(z, 0.0) - jnp.log(1.0 + jnp.exp(-jnp.abs(z)))
        hh = lax.broadcasted_iota(jnp.int32, (LANES, w), 0)
        cc = lax.broadcasted_iota(jnp.int32, (LANES, w), 1)
        expand = (cc // HEAD_DIM == hh).astype(F32)
        lx = jnp.dot(logf, expand, preferred_element_type=F32, precision=HI)
        rr = lax.broadcasted_iota(jnp.int32, (t, t), 0)
        kk = lax.broadcasted_iota(jnp.int32, (t, t), 1)
        tri = (kk <= rr).astype(F32)
        cum = jnp.dot(tri, lx, preferred_element_type=F32, precision=HI) + carry_ref[...]
        o_ref[...] = cum
        carry_ref[...] = cum[t - 1:t, :]

    return pl.pallas_call(
        kern, name=name, out_shape=jax.ShapeDtypeStruct((s, w), F32), grid=(s // t,),
        in_specs=[pl.BlockSpec((t, LANES), lambda i: (i, 0)), pl.BlockSpec((1, LANES), lambda i: (0, 0))],
        out_specs=pl.BlockSpec((t, w), lambda i: (i, 0)),
        scratch_shapes=[pltpu.VMEM((1, w), F32)],
        compiler_params=_params(dimension_semantics=("arbitrary",)),
    )(flog, bf_row)


def _cum_bwd(dcrow, frow, bf_col, *, name):
    _, nh, s = dcrow.shape
    t = _fit(s, 512, LANES)
    nb = s // t

    def kern(d_ref, f_ref, b_ref, df_ref, db_ref):
        rr = lax.broadcasted_iota(jnp.int32, (t, t), 0)
        kk = lax.broadcasted_iota(jnp.int32, (t, t), 1)
        upper = (rr >= kk).astype(F32)
        carry = jnp.zeros((nh, 1), F32)
        db = jnp.zeros((nh, 1), F32)
        for blk in range(nb - 1, -1, -1):
            sl = slice(blk * t, (blk + 1) * t)
            rc = jnp.dot(d_ref[0, :, sl] + d_ref[1, :, sl], upper, preferred_element_type=F32, precision=HI) + carry
            carry = rc[:, 0:1]
            df = rc * _sigmoid(-(f_ref[:, sl] + b_ref[...]))
            df_ref[:, sl] = df
            db = db + jnp.sum(df, axis=1, keepdims=True)
        db_ref[...] = db

    return pl.pallas_call(
        kern, name=name,
        out_shape=[jax.ShapeDtypeStruct((nh, s), F32), jax.ShapeDtypeStruct((nh, 1), F32)],
        compiler_params=_params(),
    )(dcrow, frow, bf_col)


def _ride_split(ex, refs, n_in, n_out):
    n = ex.n if ex is not None else 0
    own_in, srcs = refs[:n_in], refs[n_in:n_in + n]
    own_out, dsts = refs[n_in + n:n_in + n + n_out], refs[n_in + n + n_out:n_in + 2 * n + n_out]
    sems = refs[n_in + 2 * n + n_out:n_in + 2 * n + n_out + 3] if n else ()
    rest = refs[n_in + 2 * n + n_out + (3 if n else 0):]
    return own_in, own_out, (srcs, dsts, sems), rest


def _attn_fwd(proj, cumx, cumrow, qcol, *, name, ride=None):
    s = proj.shape[0]
    w = cumx.shape[1]
    nhp = w // LANES
    t = _fit(s, 256, LANES)
    nq = s // t
    scale = HEAD_DIM ** -0.5
    qb, kb, vb = qcol // LANES, (qcol + w) // LANES, (qcol + 2 * w) // LANES
    ex, ex_arrays = ride if ride is not None else (None, [])

    def kern(*refs):
        (q_ref, k_ref, v_ref, cx_ref, cr_ref), (o_ref, l_ref), comm, _ = _ride_split(ex, refs, 5, 2)
        i = pl.program_id(1)
        if ex is not None:
            @pl.when((pl.program_id(0) == 0) & (i == 0))
            def _():
                ex.start(*comm)

        lane = lax.broadcasted_iota(jnp.int32, (t, LANES), 1)
        rows = i * t + lax.broadcasted_iota(jnp.int32, (t, t), 0)
        cols0 = lax.broadcasted_iota(jnp.int32, (t, t), 1)
        q2 = q_ref[...]
        outs, lses = [], []
        for e in range(2):
            msk = (lane >= HEAD_DIM * e) & (lane < HEAD_DIM * (e + 1))
            qe = jnp.where(msk, q2, 0.0).astype(BF)
            cq = cx_ref[:, HEAD_DIM * e:HEAD_DIM * e + 1]

            def body(j, carry, qe=qe, cq=cq, e=e):
                m, l, acc = carry
                off = pl.multiple_of(j * t, LANES)
                kt = k_ref[pl.ds(off, t), :].astype(BF)
                vt = v_ref[pl.ds(off, t), :].astype(BF)
                ck = cr_ref[e:e + 1, pl.ds(off, t)]
                sc = lax.dot_general(qe, kt, (((1,), (1,)), ((), ())), preferred_element_type=F32) * scale
                sc = sc + (cq - ck)
                sc = jnp.where(cols0 + j * t <= rows, sc, NEG)
                m_new = jnp.maximum(m, jnp.max(sc, axis=1, keepdims=True))
                p = jnp.exp(sc - m_new)
                alpha = jnp.exp(m - m_new)
                l = alpha * l + jnp.sum(p, axis=1, keepdims=True)
                acc = alpha * acc + jnp.dot(p.astype(BF), vt, preferred_element_type=F32)
                return m_new, l, acc

            init = (jnp.full((t, 1), NEG, F32), jnp.zeros((t, 1), F32), jnp.zeros((t, LANES), F32))
            m, l, acc = lax.fori_loop(0, i + 1, body, init)
            outs.append(acc / l)
            lses.append(jnp.broadcast_to(m + jnp.log(l), (t, LANES)))
        o_ref[...] = jnp.where(lane < HEAD_DIM, outs[0], outs[1]).astype(BF)
        l_ref[...] = jnp.where(lane < HEAD_DIM, lses[0], lses[1])
        if ex is not None:
            @pl.when((pl.program_id(0) == nhp - 1) & (i == nq - 1))
            def _():
                ex.wait(*comm)

    return pl.pallas_call(
        kern, name=name,
        out_shape=[jax.ShapeDtypeStruct((s, w), BF), jax.ShapeDtypeStruct((nhp, s, LANES), F32)]
        + (ex.out_shape if ex else []),
        grid=(nhp, nq),
        in_specs=[pl.BlockSpec((t, LANES), lambda h, i: (i, qb + h)),
                  pl.BlockSpec((s, LANES), lambda h, i: (0, kb + h)),
                  pl.BlockSpec((s, LANES), lambda h, i: (0, vb + h)),
                  pl.BlockSpec((t, LANES), lambda h, i: (i, h)),
                  pl.BlockSpec((None, 2, s), lambda h, i: (h, 0, 0))] + (ex.specs if ex else []),
        out_specs=[pl.BlockSpec((t, LANES), lambda h, i: (i, h)),
                   pl.BlockSpec((None, t, LANES), lambda h, i: (h, i, 0))] + (ex.specs if ex else []),
        scratch_shapes=ex.scratch if ex else [],
        compiler_params=_params(dimension_semantics=("arbitrary", "arbitrary"),
                                has_side_effects=ex is not None),
    )(proj, proj, proj, cumx, cumrow, *ex_arrays)


def _attn_bwd(proj, do, o, lse, cumx, cumrow, qcol, *, name, ride=None):
    s = proj.shape[0]
    w = cumx.shape[1]
    nhp = w // LANES
    t = _fit(s, 256, LANES)
    nq = s // t
    scale = HEAD_DIM ** -0.5
    qb, kb, vb = qcol // LANES, (qcol + w) // LANES, (qcol + 2 * w) // LANES
    tn_dims = (((0,), (0,)), ((), ()))
    nt_dims = (((1,), (1,)), ((), ()))
    ex, ex_arrays = ride if ride is not None else (None, [])

    def kern(*refs):
        own_in, own_out, comm, _ = _ride_split(ex, refs, 8, 5)
        q_ref, k_ref, v_ref, do_ref, o_ref, l_ref, cx_ref, cr_ref = own_in
        dq_ref, dk_ref, dv_ref, dc_ref, dr_ref = own_out
        j = pl.program_id(1)
        if ex is not None:
            @pl.when((pl.program_id(0) == 0) & (j == 0))
            def _():
                ex.start(*comm)

        @pl.when(j == 0)
        def _():
            dq_ref[...] = jnp.zeros_like(dq_ref)
            dr_ref[...] = jnp.zeros_like(dr_ref)

        lane = lax.broadcasted_iota(jnp.int32, (t, LANES), 1)
        rows0 = lax.broadcasted_iota(jnp.int32, (t, t), 0)
        cols = j * t + lax.broadcasted_iota(jnp.int32, (t, t), 1)
        msks = [(lane >= HEAD_DIM * e) & (lane < HEAD_DIM * (e + 1)) for e in range(2)]
        k2 = k_ref[...]
        kt = k2.astype(BF)
        vt = v_ref[...].astype(BF)
        kes = [jnp.where(msks[e], k2, 0.0).astype(BF) for e in range(2)]
        joff = pl.multiple_of(j * t, LANES)
        cks = [cr_ref[e:e + 1, pl.ds(joff, t)] for e in range(2)]

        def body(i, carry):
            dk_acc, dv_acc, dc0, dc1 = carry
            dcs = [dc0, dc1]
            off = pl.multiple_of(i * t, LANES)
            q2 = q_ref[pl.ds(off, t), :]
            do2 = do_ref[pl.ds(off, t), :]
            dd = do2 * o_ref[pl.ds(off, t), :].astype(F32)
            lse2 = l_ref[pl.ds(off, t), :]
            cx2 = cx_ref[pl.ds(off, t), :]
            causal = cols <= rows0 + i * t
            dq_blk = jnp.zeros((t, LANES), F32)
            rowsums = []
            for e in range(2):
                qe = jnp.where(msks[e], q2, 0.0).astype(BF)
                doe = jnp.where(msks[e], do2, 0.0).astype(BF)
                delta = jnp.sum(jnp.where(msks[e], dd, 0.0), axis=1, keepdims=True)
                lse = lse2[:, HEAD_DIM * e:HEAD_DIM * e + 1]
                cq = cx2[:, HEAD_DIM * e:HEAD_DIM * e + 1]
                sc = lax.dot_general(qe, kt, nt_dims, preferred_element_type=F32) * scale + (cq - cks[e])
                p = jnp.where(causal, jnp.exp(sc - lse), 0.0)
                dp = lax.dot_general(doe, vt, nt_dims, preferred_element_type=F32)
                ds = p * (dp - delta)
                dsb = ds.astype(BF)
                dv_acc = dv_acc + lax.dot_general(p.astype(BF), doe, tn_dims, preferred_element_type=F32)
                dk_acc = dk_acc + lax.dot_general(dsb, qe, tn_dims, preferred_element_type=F32) * scale
                dq_blk = dq_blk + jnp.dot(dsb, kes[e], preferred_element_type=F32) * scale
                dcs[e] = dcs[e] - jnp.sum(ds, axis=0, keepdims=True)
                rowsums.append(jnp.broadcast_to(jnp.sum(ds, axis=1, keepdims=True), (t, LANES)))
            dq_ref[pl.ds(off, t), :] += dq_blk
            dr_ref[pl.ds(off, t), :] += jnp.where(msks[0], rowsums[0], rowsums[1])
            return dk_acc, dv_acc, dcs[0], dcs[1]

        init = (jnp.zeros((t, LANES), F32), jnp.zeros((t, LANES), F32), jnp.zeros((1, t), F32), jnp.zeros((1, t), F32))
        dk_acc, dv_acc, dc0, dc1 = lax.fori_loop(j, nq, body, init)
        dk_ref[...] = dk_acc
        dv_ref[...] = dv_acc
        dc_ref[0:1, :] = dc0
        dc_ref[1:2, :] = dc1
        if ex is not None:
            @pl.when((pl.program_id(0) == nhp - 1) & (j == nq - 1))
            def _():
                ex.wait(*comm)

    full = lambda cb: pl.BlockSpec((s, LANES), lambda h, j: (0, cb + h))
    blk = lambda cb: pl.BlockSpec((t, LANES), lambda h, j: (j, cb + h))
    return pl.pallas_call(
        kern, name=name,
        out_shape=[jax.ShapeDtypeStruct((s, w), F32), jax.ShapeDtypeStruct((s, w), F32),
                   jax.ShapeDtypeStruct((s, w), F32), jax.ShapeDtypeStruct((nhp, 2, s), F32),
                   jax.ShapeDtypeStruct((s, w), F32)] + (ex.out_shape if ex else []),
        grid=(nhp, nq),
        in_specs=[full(qb), blk(kb), blk(vb), full(0), full(0),
                  pl.BlockSpec((None, s, LANES), lambda h, j: (h, 0, 0)), full(0),
                  pl.BlockSpec((None, 2, s), lambda h, j: (h, 0, 0))] + (ex.specs if ex else []),
        out_specs=[full(0), blk(0), blk(0), pl.BlockSpec((None, 2, t), lambda h, j: (h, 0, j)), full(0)]
        + (ex.specs if ex else []),
        scratch_shapes=ex.scratch if ex else [],
        compiler_params=_params(dimension_semantics=("arbitrary", "arbitrary"),
                                has_side_effects=ex is not None),
    )(proj, proj, proj, do, o, lse, cumx, cumrow, *ex_arrays)


S5_STATES = 256
S5_ROWS = 512


def _cmul(ar, ai, br, bi):
    return ar * br - ai * bi, ar * bi + ai * br


def _scan_tables(lr, li, reverse):
    w = lr.shape[1]
    row = lax.broadcasted_iota(jnp.int32, (SUBLANES, w), 0)
    if reverse:
        row = SUBLANES - 1 - row
    lr1, li1 = jnp.broadcast_to(lr, (SUBLANES, w)), jnp.broadcast_to(li, (SUBLANES, w))
    lr2, li2 = _cmul(lr1, li1, lr1, li1)
    lr4, li4 = _cmul(lr2, li2, lr2, li2)
    steps = []
    for d, (pr, pi) in zip((1, 2, 4), ((lr1, li1), (lr2, li2), (lr4, li4))):
        keep = row >= d
        steps.append((jnp.where(keep, pr, 0.0), jnp.where(keep, pi, 0.0)))
    cr, ci = lr1, li1
    for bit, (pr, pi) in zip((1, 2, 4), ((lr1, li1), (lr2, li2), (lr4, li4))):
        nr, ni = _cmul(cr, ci, pr, pi)
        has = (row & bit) != 0
        cr, ci = jnp.where(has, nr, cr), jnp.where(has, ni, ci)
    return steps, (cr, ci)


def _scan_block(xr, xi, car_r, car_i, steps, carry_pow, reverse):
    for d, (pr, pi) in zip((1, 2, 4), steps):
        sh = (SUBLANES - d) if reverse else d
        sr, si = pltpu.roll(xr, sh, 0), pltpu.roll(xi, sh, 0)
        xr, xi = xr + (pr * sr - pi * si), xi + (pr * si + pi * sr)
    cr, ci = carry_pow
    xr, xi = xr + (cr * car_r - ci * car_i), xi + (cr * car_i + ci * car_r)
    return xr, xi


def _s5_specs(s, ncb):
    u_spec = pl.BlockSpec((s, LANES), lambda cb, hf: (0, cb))
    wb_spec = pl.BlockSpec((LANES, S5_STATES), lambda cb, hf: (cb, 2 * cb + hf))
    wc_spec = pl.BlockSpec((S5_STATES, LANES), lambda cb, hf: (2 * cb + hf, cb))
    lam_spec = pl.BlockSpec((1, S5_STATES), lambda cb, hf: (0, 2 * cb + hf))
    d_spec = pl.BlockSpec((1, LANES), lambda cb, hf: (0, cb))
    return u_spec, wb_spec, wc_spec, lam_spec, d_spec


def _s5_project_and_scan(u_ref, wbr_ref, wbi_ref, lr_ref, li_ref, xr_ref, xi_ref, s, rows):
    wbr, wbi = wbr_ref[...], wbi_ref[...]
    for r in range(s // rows):
        sl = pl.ds(r * rows, rows)
        ub = u_ref[sl, :].astype(BF)
        xr_ref[sl, :] = jnp.dot(ub, wbr, preferred_element_type=F32)
        xi_ref[sl, :] = jnp.dot(ub, wbi, preferred_element_type=F32)
    steps, cpow = _scan_tables(lr_ref[...], li_ref[...], False)

    def body(b, carry):
        car_r, car_i = carry
        sl = pl.ds(pl.multiple_of(b * SUBLANES, SUBLANES), SUBLANES)
        xr, xi = _scan_block(xr_ref[sl, :], xi_ref[sl, :], car_r, car_i, steps, cpow, False)
        xr_ref[sl, :] = xr
        xi_ref[sl, :] = xi
        return xr[SUBLANES - 1:SUBLANES, :], xi[SUBLANES - 1:SUBLANES, :]

    zero = jnp.zeros((1, S5_STATES), F32)
    lax.fori_loop(0, s // SUBLANES, body, (zero, zero), unroll=4)


def _s5_fwd(proj, wb_re, wb_im, wc_re, wc_im, lam_re, lam_im, dskip, *, name):
    s = proj.shape[0]
    w = dskip.shape[1]
    ncb = w // LANES
    rows = _fit(s, S5_ROWS, SUBLANES)

    def kern(u_ref, wbr_ref, wbi_ref, wcr_ref, wci_ref, lr_ref, li_ref, d_ref, y_ref, xr_ref, xi_ref):
        hf = pl.program_id(1)
        _s5_project_and_scan(u_ref, wbr_ref, wbi_ref, lr_ref, li_ref, xr_ref, xi_ref, s, rows)
        wcr, wci = wcr_ref[...], wci_ref[...]
        for r in range(s // rows):
            sl = pl.ds(r * rows, rows)
            y = (jnp.dot(xr_ref[sl, :].astype(BF), wcr, preferred_element_type=F32)
                 - jnp.dot(xi_ref[sl, :].astype(BF), wci, preferred_element_type=F32))

            @pl.when(hf == 0)
            def _(y=y, sl=sl):
                y_ref[sl, :] = y + d_ref[...] * u_ref[sl, :]

            @pl.when(hf == 1)
            def _(y=y, sl=sl):
                y_ref[sl, :] += y

    u_spec, wb_spec, wc_spec, lam_spec, d_spec = _s5_specs(s, ncb)
    return pl.pallas_call(
        kern, name=name, out_shape=jax.ShapeDtypeStruct((s, w), F32), grid=(ncb, 2),
        in_specs=[u_spec, wb_spec, wb_spec, wc_spec, wc_spec, lam_spec, lam_spec, d_spec],
        out_specs=u_spec,
        scratch_shapes=[pltpu.VMEM((s, S5_STATES), F32), pltpu.VMEM((s, S5_STATES), F32)],
        compiler_params=_params(dimension_semantics=("parallel", "arbitrary")),
    )(proj, wb_re, wb_im, wc_re, wc_im, lam_re, lam_im, dskip)


def _s5_bwd(proj, dy, wb_re, wb_im, wc_re, wc_im, lam_re, lam_im, dskip, *, name):
    s = proj.shape[0]
    w = dskip.shape[1]
    ncb = w // LANES
    rows = _fit(s, S5_ROWS, SUBLANES)
    tn_dims = (((0,), (0,)), ((), ()))
    nt_dims = (((1,), (1,)), ((), ()))

    def kern(u_ref, dy_ref, wbr_ref, wbi_ref, wcr_ref, wci_ref, lr_ref, li_ref, d_ref,
             du_ref, dwbr_ref, dwbi_ref, dwcr_ref, dwci_ref, dlr_ref, dli_ref, dd_ref,
             xr_ref, xi_ref, gr_ref, gi_ref):
        hf = pl.program_id(1)
        _s5_project_and_scan(u_ref, wbr_ref, wbi_ref, lr_ref, li_ref, xr_ref, xi_ref, s, rows)

        wcr, wci = wcr_ref[...], wci_ref[...]
        dwcr = jnp.zeros((S5_STATES, LANES), F32)
        dwci = jnp.zeros((S5_STATES, LANES), F32)
        ddsk = jnp.zeros((1, LANES), F32)
        for r in range(s // rows):
            sl = pl.ds(r * rows, rows)
            dyf = dy_ref[sl, :]
            dyb = dyf.astype(BF)
            gr_ref[sl, :] = lax.dot_general(dyb, wcr, nt_dims, preferred_element_type=F32)
            gi_ref[sl, :] = -lax.dot_general(dyb, wci, nt_dims, preferred_element_type=F32)
            dwcr = dwcr + lax.dot_general(xr_ref[sl, :].astype(BF), dyb, tn_dims, preferred_element_type=F32)
            dwci = dwci - lax.dot_general(xi_ref[sl, :].astype(BF), dyb, tn_dims, preferred_element_type=F32)
            ddsk = ddsk + jnp.sum(dyf * u_ref[sl, :], axis=0, keepdims=True)
        dwcr_ref[...] = dwcr
        dwci_ref[...] = dwci

        @pl.when(hf == 0)
        def _():
            dd_ref[...] = ddsk

        steps, cpow = _scan_tables(lr_ref[...], -li_ref[...], True)
        row = lax.broadcasted_iota(jnp.int32, (SUBLANES, S5_STATES), 0)
        nblk = s // SUBLANES

        def body(k, carry):
            car_r, car_i, ar, ai = carry
            b = nblk - 1 - k
            sl = pl.ds(pl.multiple_of(b * SUBLANES, SUBLANES), SUBLANES)
            g_r, g_i = _scan_block(gr_ref[sl, :], gi_ref[sl, :], car_r, car_i, steps, cpow, True)
            gr_ref[sl, :] = g_r
            gi_ref[sl, :] = g_i
            nr = jnp.where(row == SUBLANES - 1, car_r, pltpu.roll(g_r, SUBLANES - 1, 0))
            ni = jnp.where(row == SUBLANES - 1, car_i, pltpu.roll(g_i, SUBLANES - 1, 0))
            xr, xi = xr_ref[sl, :], xi_ref[sl, :]
            ar = ar + (xr * nr + xi * ni)
            ai = ai + (xr * ni - xi * nr)
            return g_r[0:1, :], g_i[0:1, :], ar, ai

        zero = jnp.zeros((1, S5_STATES), F32)
        zacc = jnp.zeros((SUBLANES, S5_STATES), F32)
        _, _, ar, ai = lax.fori_loop(0, nblk, body, (zero, zero, zacc, zacc), unroll=4)
        dlr_ref[...] = jnp.sum(ar, axis=0, keepdims=True)
        dli_ref[...] = jnp.sum(ai, axis=0, keepdims=True)

        wbr, wbi = wbr_ref[...], wbi_ref[...]
        dwbr = jnp.zeros((LANES, S5_STATES), F32)
        dwbi = jnp.zeros((LANES, S5_STATES), F32)
        for r in range(s // rows):
            sl = pl.ds(r * rows, rows)
            grb, gib = gr_ref[sl, :].astype(BF), gi_ref[sl, :].astype(BF)
            ub = u_ref[sl, :].astype(BF)
            dwbr = dwbr + lax.dot_general(ub, grb, tn_dims, preferred_element_type=F32)
            dwbi = dwbi + lax.dot_general(ub, gib, tn_dims, preferred_element_type=F32)
            du = (lax.dot_general(grb, wbr, nt_dims, preferred_element_type=F32)
                  + lax.dot_general(gib, wbi, nt_dims, preferred_element_type=F32))

            @pl.when(hf == 0)
            def _(du=du, sl=sl):
                du_ref[sl, :] = du + d_ref[...] * dy_ref[sl, :]

            @pl.when(hf == 1)
            def _(du=du, sl=sl):
                du_ref[sl, :] += du
        dwbr_ref[...] = dwbr
        dwbi_ref[...] = dwbi

    u_spec, wb_spec, wc_spec, lam_spec, d_spec = _s5_specs(s, ncb)
    dwb_spec = pl.BlockSpec((None, None, LANES, S5_STATES), lambda cb, hf: (cb, hf, 0, 0))
    dwc_spec = pl.BlockSpec((None, None, S5_STATES, LANES), lambda cb, hf: (cb, hf, 0, 0))
    state = pltpu.VMEM((s, S5_STATES), F32)
    return pl.pallas_call(
        kern, name=name,
        out_shape=[jax.ShapeDtypeStruct((s, w), F32),
                   jax.ShapeDtypeStruct((ncb, 2, LANES, S5_STATES), F32), jax.ShapeDtypeStruct((ncb, 2, LANES, S5_STATES), F32),
                   jax.ShapeDtypeStruct((ncb, 2, S5_STATES, LANES), F32), jax.ShapeDtypeStruct((ncb, 2, S5_STATES, LANES), F32),
                   jax.ShapeDtypeStruct((1, 4 * w), F32), jax.ShapeDtypeStruct((1, 4 * w), F32),
                   jax.ShapeDtypeStruct((1, w), F32)],
        grid=(ncb, 2),
        in_specs=[u_spec, u_spec, wb_spec, wb_spec, wc_spec, wc_spec, lam_spec, lam_spec, d_spec],
        out_specs=[u_spec, dwb_spec, dwb_spec, dwc_spec, dwc_spec, lam_spec, lam_spec, d_spec],
        scratch_shapes=[state, state, state, state],
        compiler_params=_params(dimension_semantics=("parallel", "arbitrary")),
    )(proj, dy, wb_re, wb_im, wc_re, wc_im, lam_re, lam_im, dskip)


def _s5_discretise(lam_re, lam_im, log_dt, b_re, b_im):
    lr = jnp.minimum(lam_re, -EIG_CLIP)
    li = lam_im
    dt = jnp.exp(log_dt)[:, None]
    mag = jnp.exp(lr * dt)
    lbr, lbi = mag * jnp.cos(li * dt), mag * jnp.sin(li * dt)
    den = lr * lr + li * li
    fr = ((lbr - 1.0) * lr + lbi * li) / den
    fi = (lbi * lr - (lbr - 1.0) * li) / den
    bbr = fr[..., None] * b_re - fi[..., None] * b_im
    bbi = fr[..., None] * b_im + fi[..., None] * b_re
    return lbr, lbi, bbr, bbi


def _block_diag(blocks):
    g, a, b = blocks.shape
    eye = jnp.eye(g, dtype=blocks.dtype)
    return (blocks[:, :, None, :] * eye[:, None, :, None]).reshape(g * a, g * b)


def _s5_block_grads(dwb, a, b, transpose):
    ncb = dwb.shape[0]
    gl = LANES // 2 // (a if not transpose else b)
    if not transpose:
        d = dwb.reshape(ncb, 2, 2, gl, a, gl, b)
        parts = [[d[:, hf, hf, g, :, g, :] for g in range(gl)] for hf in range(2)]
    else:
        d = dwb.reshape(ncb, 2, gl, a, 2, gl, b)
        parts = [[d[:, hf, g, :, hf, g, :] for g in range(gl)] for hf in range(2)]
    st = jnp.stack([jnp.stack(p, axis=1) for p in parts], axis=1)
    return st.reshape(ncb * 2 * gl, a, b)


def _adamw(parts, w, m, v, *, name):
    depth, r, c = w.shape
    assert len(parts) == depth
    npart = parts[0].shape[0]
    row_bytes = 4 * (-(-c // LANES) * LANES)
    align = 16 if parts[0].dtype == BF else SUBLANES
    budget = VMEM_LIMIT // 2 // (2 * (depth * npart + 7) * row_bytes)
    tr = _fit(r, max(align, budget // align * align), align)
    nr = r // tr
    c1 = 1.0 / (1.0 - ADAM_B1 ** ADAM_STEP)
    c2 = 1.0 / (1.0 - ADAM_B2 ** ADAM_STEP)

    def kern(*refs):
        p_refs = refs[:depth]
        w_ref, m_ref, v_ref, g_ref, d_ref, nm_ref, nv_ref = refs[depth:]
        layer = pl.program_id(0)
        for l in range(depth):
            @pl.when(layer == l)
            def _(p_ref=p_refs[l]):
                g = p_ref[0].astype(F32)
                for q in range(1, npart):
                    g = g + p_ref[q].astype(F32)
                m2 = ADAM_B1 * m_ref[...] + (1.0 - ADAM_B1) * g
                v2 = ADAM_B2 * v_ref[...] + (1.0 - ADAM_B2) * (g * g)
                upd = (m2 * c1) / (jnp.sqrt(v2 * c2) + ADAM_EPS) + ADAM_WD * w_ref[...]
                g_ref[...] = g
                d_ref[...] = -ADAM_LR * upd
                nm_ref[...] = m2
                nv_ref[...] = v2

    def part_spec(l):
        return pl.BlockSpec((npart, tr, c),
                            lambda ly, i: (0, jnp.where(ly == l, i, jnp.where(ly < l, 0, nr - 1)), 0))

    spec = pl.BlockSpec((None, tr, c), lambda ly, i: (ly, i, 0))
    return pl.pallas_call(
        kern, name=name, out_shape=[jax.ShapeDtypeStruct((depth, r, c), F32)] * 4, grid=(depth, nr),
        in_specs=[part_spec(l) for l in range(depth)] + [spec, spec, spec],
        out_specs=[spec] * 4,
        compiler_params=_params(dimension_semantics=("arbitrary", "arbitrary")),
    )(*parts, w, m, v)


def _sum_parts(parts, *, name):
    npart, r, c = parts.shape

    def kern(p_ref, o_ref):
        g = p_ref[0]
        for q in range(1, npart):
            g = g + p_ref[q]
        o_ref[...] = g

    return pl.pallas_call(kern, name=name, out_shape=jax.ShapeDtypeStruct((r, c), F32), compiler_params=_params())(parts)


class _Exchange:
    def __init__(self, arrays, gather):
        self.n = len(arrays)
        self.gather = gather
        self.out_shape = [jax.ShapeDtypeStruct(((NDEV,) + a.shape) if gather else a.shape, a.dtype) for a in arrays]
        self.scratch = [pltpu.SemaphoreType.DMA((self.n, NDEV - 1)), pltpu.SemaphoreType.DMA((self.n, NDEV - 1)),
                        pltpu.SemaphoreType.DMA((self.n,))]
        self.specs = [pl.BlockSpec(memory_space=pl.ANY)] * self.n

    def _copies(self, srcs, dsts, sems):
        send_sems, recv_sems, local_sems = sems
        x, y, c = lax.axis_index("x"), lax.axis_index("y"), lax.axis_index("c")
        me = 4 * x + 2 * y + c
        local = [pltpu.make_async_copy(srcs[a] if self.gather else srcs[a].at[me], dsts[a].at[me], local_sems.at[a])
                 for a in range(self.n)]
        remote = []
        for k in (1, 2, 4, 3, 5, 6, 7):
            px, py, pc = x ^ ((k >> 2) & 1), y ^ ((k >> 1) & 1), c ^ (k & 1)
            peer = 4 * px + 2 * py + pc
            for a in range(self.n):
                src = srcs[a] if self.gather else srcs[a].at[peer]
                mk = functools.partial(
                    pltpu.make_async_remote_copy, src_ref=src,
                    send_sem=send_sems.at[a, k - 1], recv_sem=recv_sems.at[a, k - 1],
                    device_id=(px, py, pc), device_id_type=pl.DeviceIdType.MESH)
                remote.append((mk(dst_ref=dsts[a].at[me]), mk(dst_ref=dsts[a].at[peer])))
        return local, remote

    def start(self, srcs, dsts, sems):
        local, remote = self._copies(srcs, dsts, sems)
        for cp in local:
            cp.start()
        for send, _ in remote:
            send.start()

    def wait(self, srcs, dsts, sems):
        local, remote = self._copies(srcs, dsts, sems)
        for send, arrival in remote:
            send.wait_send()
            arrival.wait_recv()
        for cp in local:
            cp.wait()


def _exchange(arrays, gather, *, name):
    ex = _Exchange(arrays, gather)
    n = ex.n

    def kern(*refs):
        srcs, dsts, sems = refs[:n], refs[n:2 * n], refs[2 * n:]
        ex.start(srcs, dsts, sems)
        ex.wait(srcs, dsts, sems)

    return pl.pallas_call(
        kern, name=name, out_shape=ex.out_shape, in_specs=ex.specs, out_specs=ex.specs, scratch_shapes=ex.scratch,
        compiler_params=pltpu.CompilerParams(has_side_effects=True),
    )(*arrays)


def _pack(arrays):
    flat = jnp.concatenate([a.reshape(-1).astype(F32) for a in arrays])
    pad = (-flat.shape[0]) % (SUBLANES * LANES)
    return jnp.pad(flat, (0, pad)).reshape(-1, LANES)


def _unpack(buf, like):
    flat = buf.reshape(-1)
    out, off = [], 0
    for a in like:
        sz = math.prod(a.shape)
        out.append(flat[off:off + sz].reshape(a.shape))
        off += sz
    return out


def _row(v):
    return v.reshape(1, -1)


def _layer_fwd(x, mod, p, l, ride=None, on_receive=None):
    s, d = x.shape
    sw = d // 2
    nh = d // LANES
    shift_m, scale_m, gate_m, shift_f, scale_f, gate_f = mod
    n = lambda tag: f"{tag}{l}"
    sv = {}

    h1, = _rowwise(lambda xv, g, sc, sh: (xv * _rms(xv) * g) * (1.0 + sc) + sh,
                   [x], [p['g_pre_mix'], scale_m, shift_m], [(d, BF, 'tile')], name=n("pre_mix"))
    proj_a = _mm(h1, p['w_in_a'], name=n("proj_a"))
    flog = _mm(h1, p['w_in_f'], name=n("proj_f"))
    gates = _mm(h1, p['w_in_g'], name=n("proj_g"))

    y_s5 = _s5_fwd(proj_a, p['wb_re'], p['wb_im'], p['wc_re'], p['wc_im'], p['lamb_re'], p['lamb_im'], p['d_skip'],
                   name=n("s5_fwd"))
    z, = _rowwise(_gelu, [y_s5], [], [(sw, BF, 'tile')], name=n("gelu"))
    tglu = _mm(z, p['w_glu'], name=n("glu_mm"))
    ys, = _rowwise(lambda yv, tv, b: _gelu(yv) * _sigmoid(tv + b), [y_s5, tglu], [p['b_glu']], [(sw, BF, 'tile')],
                   name=n("glu"))

    cumx = _cum_fwd(flog, p['b_f_row'], nh, name=n("cum_fwd"))
    cumrow = cumx[:, ::HEAD_DIM].T.reshape(nh // 2, 2, s)
    ya, lse, *received = _attn_fwd(proj_a, cumx, cumrow, sw, name=n("attn_fwd"), ride=ride)
    if on_receive is not None:
        on_receive(received)

    am = _mm(ys, p['w_pa'], name=n("pa_mm"))
    bm = _mm(ya, p['w_pb'], name=n("pb_mm"))
    merged, = _rowwise(lambda a, b, ga, gb: _sigmoid(ga) * a + _sigmoid(gb) * b,
                       [am, bm, (gates, d, 0), (gates, d, 1)], [], [(d, BF, 'tile')], name=n("merge"))
    ym = _mm(merged, p['w_o'], name=n("o_mm"))
    x2, = _rowwise(lambda xv, yv, g, gt: xv + gt * (yv * _rms(yv) * g),
                   [x, ym], [p['g_post_mix'], gate_m], [(d, F32, 'tile')], name=n("post_mix"))

    h2, = _rowwise(lambda xv, g, sc, sh: (xv * _rms(xv) * g) * (1.0 + sc) + sh,
                   [x2], [p['g_pre_ffn'], scale_f, shift_f], [(d, BF, 'tile')], name=n("pre_ffn"))
    gt = _mm(h2, p['w_ffn_gate'], name=n("gate_mm"))
    up = _mm(h2, p['w_ffn_up'], name=n("up_mm"))
    dff = gt.shape[1]
    act, = _rowwise(lambda g, u: _silu(g) * u, [gt, up], [], [(dff, BF, 'tile')], name=n("swiglu"))
    yf = _mm(act, p['w_ffn_down'], name=n("down_mm"))
    x3, = _rowwise(lambda xv, yv, g, gt_: xv + gt_ * (yv * _rms(yv) * g),
                   [x2, yf], [p['g_post_ffn'], gate_f], [(d, F32, 'tile')], name=n("post_ffn"))

    sv.update(x=x, h1=h1, proj_a=proj_a, flog=flog, gates=gates, y_s5=y_s5, z=z, tglu=tglu, ys=ys, cumx=cumx,
              cumrow=cumrow, ya=ya, lse=lse, am=am, bm=bm, merged=merged, ym=ym, x2=x2, h2=h2, gt=gt, up=up,
              act=act, yf=yf)
    return x3, sv


def _layer_bwd(dx3, sv, mod, p, l, make_ride=None, on_receive=None):
    x, x2 = sv['x'], sv['x2']
    s, d = x.shape
    sw = d // 2
    nh = d // LANES
    shift_m, scale_m, gate_m, shift_f, scale_f, gate_f = mod
    n = lambda tag: f"{tag}{l}"
    gw, gs = {}, {}

    def post_bwd(dxo, yv, g, gate):
        r = _rms(yv)
        nf = yv * r
        dn = dxo * gate * g
        return _norm_bwd(dn, nf, r), dxo * (nf * g), dxo * gate * nf

    def pre_bwd(dh, dres, xv, g, sc):
        r = _rms(xv)
        xh = xv * r
        n3 = xh * g
        dn3 = dh * (1.0 + sc)
        return dres + _norm_bwd(dn3 * g, xh, r), dh, dh * n3, dn3 * xh

    dyf, dgate_f, gs['g_post_ffn'] = _rowwise(
        post_bwd, [dx3, sv['yf']], [p['g_post_ffn'], gate_f],
        [(d, BF, 'tile'), (d, F32, 'sum'), (d, F32, 'sum')], name=n("post_ffn_bwd"))
    dff = sv['gt'].shape[1]
    dact = _mm(dyf, p['w_ffn_down'], tb=True, name=n("down_bwd_x"))
    gw['w_ffn_down'] = _mm(sv['act'], dyf, ta=True, out_dtype=BF, name=n("down_bwd_w"))

    def swiglu_bwd(da, g, u):
        sg = _sigmoid(g)
        return da * u * (sg * (1.0 + g * (1.0 - sg))), da * (g * sg)

    dgt, dup = _rowwise(swiglu_bwd, [dact, sv['gt'], sv['up']], [], [(dff, BF, 'tile'), (dff, BF, 'tile')],
                        name=n("swiglu_bwd"))
    dh2a = _mm(dgt, p['w_ffn_gate'], tb=True, name=n("gate_bwd_x"))
    dh2b = _mm(dup, p['w_ffn_up'], tb=True, name=n("up_bwd_x"))
    gw['w_ffn_gate'] = _mm(sv['h2'], dgt, ta=True, out_dtype=BF, name=n("gate_bwd_w"))
    gw['w_ffn_up'] = _mm(sv['h2'], dup, ta=True, out_dtype=BF, name=n("up_bwd_w"))
    dx2, dshift_f, dscale_f, gs['g_pre_ffn'] = _rowwise(
        lambda da, db, dres, xv, g, sc: pre_bwd(da + db, dres, xv, g, sc),
        [dh2a, dh2b, dx3, x2], [p['g_pre_ffn'], scale_f],
        [(d, F32, 'tile'), (d, F32, 'sum'), (d, F32, 'sum'), (d, F32, 'sum')], name=n("pre_ffn_bwd"))

    dym, dgate_m, gs['g_post_mix'] = _rowwise(
        post_bwd, [dx2, sv['ym']], [p['g_post_mix'], gate_m],
        [(d, BF, 'tile'), (d, F32, 'sum'), (d, F32, 'sum')], name=n("post_mix_bwd"))
    dmerged = _mm(dym, p['w_o'], tb=True, name=n("o_bwd_x"))
    gw['w_o'] = _mm(sv['merged'], dym, ta=True, out_dtype=BF, name=n("o_bwd_w"))

    def merge_bwd(dm, a, b, ga, gb):
        sa, sb = _sigmoid(ga), _sigmoid(gb)
        return dm * sa, dm * sb, dm * a * sa * (1.0 - sa), dm * b * sb * (1.0 - sb)

    da_, db_, dga, dgb = _rowwise(
        merge_bwd, [dmerged, sv['am'], sv['bm'], (sv['gates'], d, 0), (sv['gates'], d, 1)], [],
        [(d, BF, 'tile')] * 4, name=n("merge_bwd"))
    dys = _mm(da_, p['w_pa'], tb=True, name=n("pa_bwd_x"))
    gw['w_pa'] = _mm(sv['ys'], da_, ta=True, out_dtype=BF, name=n("pa_bwd_w"))
    dya = _mm(db_, p['w_pb'], tb=True, name=n("pb_bwd_x"))
    gw['w_pb'] = _mm(sv['ya'], db_, ta=True, out_dtype=BF, name=n("pb_bwd_w"))

    dq, dk, dv, dcrow, dcq, *received = _attn_bwd(
        sv['proj_a'], dya, sv['ya'], sv['lse'], sv['cumx'], sv['cumrow'], sw, name=n("attn_bwd"),
        ride=make_ride(gw) if make_ride is not None else None)
    if on_receive is not None:
        on_receive(received)
    frow = sv['flog'][:, :nh].T
    dcum = jnp.stack([dcrow.reshape(nh, s), dcq[:, ::HEAD_DIM].T])
    dfrow, dbf = _cum_bwd(dcum, frow, p['b_f_col'], name=n("cum_bwd"))
    gs['b_f'] = dbf.reshape(nh)
    dflog = jnp.pad(dfrow.T, ((0, 0), (0, LANES - nh))).astype(BF)

    def glu_bwd(dy_, yv, tv, b):
        zv = _gelu(yv)
        sg = _sigmoid(tv + b)
        dt = dy_ * zv * sg * (1.0 - sg)
        return dt, dy_ * sg, dt

    dt, dz1, gs['b_glu'] = _rowwise(glu_bwd, [dys, sv['y_s5'], sv['tglu']], [p['b_glu']],
                                    [(sw, BF, 'tile'), (sw, F32, 'tile'), (sw, F32, 'sum')], name=n("glu_bwd"))
    dz2 = _mm(dt, p['w_glu'], tb=True, name=n("glu_bwd_x"))
    gw['w_glu'] = _mm(sv['z'], dt, ta=True, out_dtype=BF, name=n("glu_bwd_w"))
    dy_s5, = _rowwise(lambda a, b, yv: (a + b) * _gelu_grad(yv), [dz1, dz2, sv['y_s5']], [], [(sw, F32, 'tile')],
                      name=n("gelu_bwd"))
    du, dwbr, dwbi, dwcr, dwci, dlr, dli, gs['d_skip'] = _s5_bwd(
        sv['proj_a'], dy_s5, p['wb_re'], p['wb_im'], p['wc_re'], p['wc_im'], p['lamb_re'], p['lamb_im'], p['d_skip'],
        name=n("s5_bwd"))
    g_ = sw // SSM_H
    pst = p['lamb_re'].shape[1] // g_
    gs['lamb_re'], gs['lamb_im'] = dlr.reshape(g_, pst), dli.reshape(g_, pst)
    gs['bbar_re'] = _s5_block_grads(dwbr, SSM_H, pst, False).transpose(0, 2, 1)
    gs['bbar_im'] = _s5_block_grads(dwbi, SSM_H, pst, False).transpose(0, 2, 1)
    gs['c_re'] = _s5_block_grads(dwcr, pst, SSM_H, True).transpose(0, 2, 1)
    gs['c_im'] = _s5_block_grads(dwci, pst, SSM_H, True).transpose(0, 2, 1)

    dproj = jnp.concatenate([du.astype(BF), dq.astype(BF), dk.astype(BF), dv.astype(BF), dflog, dga, dgb], axis=1)
    dh1 = _mm(dproj, p['w_in_all'], tb=True, tk=1408, name=n("proj_bwd_x"))
    gw['w_in'] = _mm(sv['h1'], dproj, ta=True, out_dtype=BF, name=n("proj_bwd_w"))
    dx, dshift_m, dscale_m, gs['g_pre_mix'] = _rowwise(
        pre_bwd, [dh1, dx2, x], [p['g_pre_mix'], scale_m],
        [(d, F32, 'tile'), (d, F32, 'sum'), (d, F32, 'sum'), (d, F32, 'sum')], name=n("pre_mix_bwd"))
    dmod = [dshift_m, dscale_m, dgate_m, dshift_f, dscale_f, dgate_f]
    return dx, gw, dmod, gs


def _unshard(k, blocks):
    if k in COL_SHARDED:
        return blocks.transpose(1, 0, 2).reshape(blocks.shape[1], NDEV * blocks.shape[2])
    return blocks.reshape(NDEV * blocks.shape[1], blocks.shape[2])


def _to_slabs(k, g):
    if k == 'w_in':
        d = g.shape[0]
        nh = d // LANES
        g = jnp.concatenate([g[:, :2 * d + nh], g[:, 2 * d + LANES:]], axis=1)
    if k in COL_SHARDED:
        return g.reshape(g.shape[0], NDEV, g.shape[1] // NDEV).transpose(1, 0, 2)
    return g.reshape(NDEV, g.shape[0] // NDEV, g.shape[1])


def _prep_w_in(w_in):
    d = w_in.shape[0]
    nh = d // LANES
    fcol = 2 * d
    p = {}
    p['w_in_a'] = w_in[:, :fcol]
    p['w_in_f'] = jnp.pad(w_in[:, fcol:fcol + nh], ((0, 0), (0, LANES - nh)))
    p['w_in_g'] = w_in[:, fcol + nh:]
    p['w_in_all'] = jnp.concatenate([p['w_in_a'], p['w_in_f'], p['w_in_g']], axis=1)
    return p


def _prep_small(small):
    nh = small['b_f'].shape[0]
    p = {}
    for k in ('g_pre_mix', 'g_post_mix', 'g_pre_ffn', 'g_post_ffn', 'd_skip', 'b_glu'):
        p[k] = _row(small[k])
    p['b_f_row'] = jnp.pad(_row(small['b_f']), ((0, 0), (0, LANES - nh)))
    p['b_f_col'] = small['b_f'].reshape(nh, 1)
    lbr, lbi, bbr, bbi = _s5_discretise(small['lam_re'], small['lam_im'], small['log_dt'], small['b_re'], small['b_im'])
    p['lamb_re'], p['lamb_im'] = _row(lbr), _row(lbi)
    p['wb_re'] = _block_diag(bbr.transpose(0, 2, 1)).astype(BF)
    p['wb_im'] = _block_diag(bbi.transpose(0, 2, 1)).astype(BF)
    p['wc_re'] = _block_diag(small['c_re'].transpose(0, 2, 1)).astype(BF)
    p['wc_im'] = _block_diag(small['c_im'].transpose(0, 2, 1)).astype(BF)
    return p


def _local_step(x, target, mods, ps, small, hooks=None):
    depth = len(ps)
    s, d = x.shape
    hooks = hooks or {}
    saved = []
    h = x
    for l in range(depth):
        h, sv = _layer_fwd(h, mods[l], ps[l], l, ride=hooks['fwd_ride'](l) if hooks else None,
                           on_receive=functools.partial(hooks['fwd_recv'], l) if hooks else None)
        saved.append(sv)

    def loss_fn(yv, tv):
        e = yv - tv
        return e * (1.0 / d), jnp.sum(e * e, axis=1, keepdims=True) * (0.5 / d)

    dy, loss = _rowwise(loss_fn, [h, target], [], [(d, F32, 'tile'), (1, F32, 'sum')], name="loss")
    dmods, gss = [None] * depth, [None] * depth
    unsent, in_flight = {}, []
    for l in range(depth - 1, -1, -1):
        def make_ride(gw, l=l):
            unsent.update({(k, l): g for k, g in gw.items()})
            in_flight[:] = list(unsent)
            ride = hooks['bwd_ride'](dict(unsent))
            unsent.clear()
            return ride

        def on_receive(results):
            hooks['bwd_recv'](list(in_flight), results)

        dy, gw, dmods[l], gs = _layer_bwd(dy, saved[l], mods[l], ps[l], l, make_ride=make_ride if hooks else None,
                                          on_receive=on_receive if hooks else None)
        unsent.update({(k, l): g for k, g in gw.items() if not hooks or (k, l) not in in_flight})
        sm = small[l]
        _, vjp = jax.vjp(_s5_discretise, sm['lam_re'], sm['lam_im'], sm['log_dt'], sm['b_re'], sm['b_im'])
        gs['lam_re'], gs['lam_im'], gs['log_dt'], gs['b_re'], gs['b_im'] = vjp(
            (gs.pop('lamb_re'), gs.pop('lamb_im'), gs.pop('bbar_re'), gs.pop('bbar_im')))
        gss[l] = gs
    return loss, dy, unsent, dmods, gss


SMALL_LOCAL = ['g_pre_mix', 'g_post_mix', 'g_pre_ffn', 'g_post_ffn', 'lam_re', 'lam_im', 'log_dt', 'b_re', 'b_im',
               'c_re', 'c_im', 'd_skip', 'b_glu', 'b_f']


def kernel(x, c, w_ada, b_ada, g_pre_mix, g_post_mix, g_pre_ffn, g_post_ffn, w_in, lam_re, lam_im, log_dt, b_re, b_im, c_re, c_im, d_skip, w_glu, b_glu, b_f, w_pa, w_pb, w_o, w_ffn_gate, w_ffn_up, w_ffn_down, loss_target, m_w_ada, m_b_ada, m_g_pre_mix, m_g_post_mix, m_g_pre_ffn, m_g_post_ffn, m_w_in, m_lam_re, m_lam_im, m_log_dt, m_b_re, m_b_im, m_c_re, m_c_im, m_d_skip, m_w_glu, m_b_glu, m_b_f, m_w_pa, m_w_pb, m_w_o, m_w_ffn_gate, m_w_ffn_up, m_w_ffn_down, v_w_ada, v_b_ada, v_g_pre_mix, v_g_post_mix, v_g_pre_ffn, v_g_post_ffn, v_w_in, v_lam_re, v_lam_im, v_log_dt, v_b_re, v_b_im, v_c_re, v_c_im, v_d_skip, v_w_glu, v_b_glu, v_b_f, v_w_pa, v_w_pb, v_w_o, v_w_ffn_gate, v_w_ffn_up, v_w_ffn_down):
    args = dict(locals())
    W = {k: args[k] for k in WEIGHTS}
    M = {k: args['m_' + k] for k in WEIGHTS}
    V = {k: args['v_' + k] for k in WEIGHTS}
    depth, d, ncol = w_ada.shape
    s = x.shape[1]
    me = 4 * lax.axis_index("x") + 2 * lax.axis_index("y") + lax.axis_index("c")

    c_all, = _exchange([jnp.pad(c, ((0, SUBLANES - 1), (0, 0)))], True, name="gather_c")
    c_all = c_all[:, 0, :]
    cond, = _rowwise(_silu, [c_all], [], [(d, F32, 'tile')], name="cond")
    mod_part = jnp.stack([_mm(cond, w_ada[l], name=f"ada_mm{l}") for l in range(depth)], axis=1)
    mod_recv, = _exchange([mod_part.reshape(NDEV, depth, 1, ncol)], False, name="scatter_mod")
    mod_cat = mod_recv.reshape(NDEV, depth, ncol).transpose(1, 0, 2).reshape(depth, NDEV * ncol)
    mod, = _rowwise(lambda a, b: a + b, [mod_cat, b_ada], [], [(NDEV * ncol, F32, 'tile')], name="mod_bias")
    mods = [[mod[l:l + 1, i * d:(i + 1) * d] for i in range(6)] for l in range(depth)]

    small = [{k: W[k][l] for k in SMALL_LOCAL} for l in range(depth)]
    ps = [_prep_small(small[l]) for l in range(depth)]
    first = ['w_in', 'w_glu']
    later = [(k, l) for l in range(depth) for k in BIG if not (l == 0 and k in first)]

    def take_weights(keys, results):
        for (k, l), blocks in zip(keys, results):
            full = _unshard(k, blocks)
            ps[l].update(_prep_w_in(full) if k == 'w_in' else {k: full})

    take_weights([(k, 0) for k in first],
                 _exchange([W[k][0].astype(BF) for k in first], True, name="gather_w_first"))

    grad_parts = {}

    def fwd_ride(l):
        if l != 0:
            return None
        blocks = [W[k][ll].astype(BF) for k, ll in later]
        return _Exchange(blocks, True), blocks

    def bwd_ride(grads):
        slabs = [_to_slabs(k, g) for (k, _), g in grads.items()]
        return _Exchange(slabs, False), slabs

    hooks = dict(fwd_ride=fwd_ride, fwd_recv=lambda l, results: take_weights(later, results) if l == 0 else None,
                 bwd_ride=bwd_ride, bwd_recv=lambda keys, results: grad_parts.update(zip(keys, results)))

    loss, dx, unsent, dmods, gss = _local_step(x[0], loss_target[0], mods, ps, small, hooks)
    loss = lax.psum(loss[0, 0], ("x", "y", "c"))
    grad_parts.update(zip(unsent, _exchange([_to_slabs(k, g) for (k, _), g in unsent.items()], False,
                                            name="scatter_grads_last")))
    out = {}
    for k in BIG:
        out[k] = _adamw([grad_parts[(k, l)] for l in range(depth)], W[k], M[k], V[k], name=f"adamw_{k}")

    dmod_mine = jnp.stack([jnp.concatenate(dmods[l], axis=1)[0] for l in range(depth)])
    small_mine = [dmod_mine] + [jnp.stack([gss[l][k] for l in range(depth)]) for k in SMALL_LOCAL]
    parts, = _exchange([_pack(small_mine)], True, name="gather_small")
    summed = _sum_parts(parts, name="sum_small")
    names = ['b_ada'] + SMALL_LOCAL
    grads = _unpack(summed, [W[k] for k in names])
    res = _adamw([_pack(grads)[None]], _pack([W[k] for k in names])[None], _pack([M[k] for k in names])[None],
                 _pack([V[k] for k in names])[None], name="adamw_small")
    unpacked = [_unpack(a, [W[k] for k in names]) for a in res]
    for i, k in enumerate(names):
        out[k] = [unpacked[j][i] for j in range(4)]

    dmod_all = parts.reshape(NDEV, -1)[:, :depth * 6 * d].reshape(NDEV, depth, 6 * d)
    dmod_cols = lax.dynamic_slice_in_dim(dmod_all, me * ncol, ncol, axis=2)
    g_ada = [_mm(cond, dmod_cols[:, l], ta=True, precision=HI, name=f"ada_bwd{l}")[None] for l in range(depth)]
    out['w_ada'] = _adamw(g_ada, w_ada, m_w_ada, v_w_ada, name="adamw_w_ada")

    return (loss, dx[None], *[out[k][0] for k in WEIGHTS], *[out[k][1] for k in WEIGHTS],
            *[out[k][2] for k in WEIGHTS], *[out[k][3] for k in WEIGHTS])
```

```python
import functools
import math

import jax
import jax.numpy as jnp
from jax import lax
from jax.experimental import pallas as pl
from jax.experimental.pallas import tpu as pltpu

F32 = jnp.float32
BF = jnp.bfloat16
NDEV = 8
LANES = 128
SUBLANES = 8
VMEM_LIMIT = 48 * 1024 * 1024

SSM_H = 16
HEAD_DIM = 64
RMS_EPS = 1e-6
EIG_CLIP = 1e-4
ADAM_LR = 0.001
ADAM_B1 = 0.9
ADAM_B2 = 0.999
ADAM_EPS = 1e-08
ADAM_WD = 0.01
ADAM_STEP = 10
NEG = -1e30
HI = lax.Precision.HIGHEST

WEIGHTS = ['w_ada', 'b_ada', 'g_pre_mix', 'g_post_mix', 'g_pre_ffn', 'g_post_ffn', 'w_in', 'lam_re', 'lam_im',
           'log_dt', 'b_re', 'b_im', 'c_re', 'c_im', 'd_skip', 'w_glu', 'b_glu', 'b_f', 'w_pa', 'w_pb', 'w_o',
           'w_ffn_gate', 'w_ffn_up', 'w_ffn_down']
COL_SHARDED = ['w_in', 'w_pa', 'w_pb', 'w_ffn_gate', 'w_ffn_up']
ROW_SHARDED = ['w_glu', 'w_o', 'w_ffn_down']
BIG = COL_SHARDED + ROW_SHARDED
SMALL = ['b_ada', 'g_pre_mix', 'g_post_mix', 'g_pre_ffn', 'g_post_ffn', 'lam_re', 'lam_im', 'log_dt', 'b_re',
         'b_im', 'c_re', 'c_im', 'd_skip', 'b_glu', 'b_f']


def _fit(dim, target, align):
    if dim <= target:
        return dim
    t = (target // align) * align
    while t >= align:
        if dim % t == 0:
            return t
        t -= align
    return dim


def _params(**kw):
    return pltpu.CompilerParams(vmem_limit_bytes=VMEM_LIMIT, **kw)


def _mm(a, b, *, ta=False, tb=False, out_dtype=F32, tm=512, tn=512, tk=2048, precision=None, name):
    m, k = (a.shape[1], a.shape[0]) if ta else a.shape
    n = b.shape[0] if tb else b.shape[1]
    assert (b.shape[1] if tb else b.shape[0]) == k
    tm = _fit(m, tm, LANES if ta else 16)
    tn = _fit(n, tn, LANES)
    tk = _fit(k, tk, LANES)
    nk = k // tk
    dims = (((0 if ta else 1,), (1 if tb else 0,)), ((), ()))

    def kern(a_ref, b_ref, o_ref, *scratch):
        av, bv = a_ref[...], b_ref[...]
        if precision is None:
            av, bv = av.astype(BF), bv.astype(BF)
        p = lax.dot_general(av, bv, dims, preferred_element_type=F32, precision=precision)
        if nk == 1:
            o_ref[...] = p.astype(out_dtype)
            return
        acc_ref, = scratch
        kk = pl.program_id(2)

        @pl.when(kk == 0)
        def _():
            acc_ref[...] = p

        @pl.when(kk > 0)
        def _():
            acc_ref[...] += p

        @pl.when(kk == nk - 1)
        def _():
            o_ref[...] = acc_ref[...].astype(out_dtype)

    a_spec = pl.BlockSpec((tk, tm), lambda i, j, kk: (kk, i)) if ta else pl.BlockSpec((tm, tk), lambda i, j, kk: (i, kk))
    b_spec = pl.BlockSpec((tn, tk), lambda i, j, kk: (j, kk)) if tb else pl.BlockSpec((tk, tn), lambda i, j, kk: (kk, j))
    return pl.pallas_call(
        kern, name=name,
        out_shape=jax.ShapeDtypeStruct((m, n), out_dtype),
        grid=(m // tm, n // tn, nk),
        in_specs=[a_spec, b_spec],
        out_specs=pl.BlockSpec((tm, tn), lambda i, j, kk: (i, j)),
        scratch_shapes=[] if nk == 1 else [pltpu.VMEM((tm, tn), F32)],
        compiler_params=_params(dimension_semantics=("parallel", "parallel", "arbitrary")),
    )(a, b)


def _rowwise(fn, tiles, params, outs, *, tr=256, name):
    tiles = [t if isinstance(t, tuple) else (t, t.shape[1], 0) for t in tiles]
    s = tiles[0][0].shape[0]
    tr = _fit(s, tr, 16)
    nt, npar = len(tiles), len(params)

    def kern(*refs):
        i = pl.program_id(0)
        res = fn(*[r[...] for r in refs[:nt + npar]])
        if not isinstance(res, (tuple, list)):
            res = (res,)
        for (w, dt, kind), o_ref, r in zip(outs, refs[nt + npar:], res):
            if kind == 'tile':
                o_ref[...] = r.astype(dt)
            else:
                part = jnp.sum(r.astype(F32), axis=0, keepdims=True)

                @pl.when(i == 0)
                def _(o_ref=o_ref, part=part):
                    o_ref[...] = part

                @pl.when(i > 0)
                def _(o_ref=o_ref, part=part):
                    o_ref[...] += part

    def tile_spec(w, cb):
        return pl.BlockSpec((tr, w), lambda i: (i, cb))

    in_specs = [tile_spec(w, cb) for _, w, cb in tiles]
    in_specs += [pl.BlockSpec(p.shape, lambda i, nd=p.ndim: (0,) * nd) for p in params]
    out_shape, out_specs = [], []
    for w, dt, kind in outs:
        if kind == 'tile':
            out_shape.append(jax.ShapeDtypeStruct((s, w), dt))
            out_specs.append(pl.BlockSpec((tr, w), lambda i: (i, 0)))
        else:
            out_shape.append(jax.ShapeDtypeStruct((1, w), F32))
            out_specs.append(pl.BlockSpec((1, w), lambda i: (0, 0)))
    res = pl.pallas_call(
        kern, name=name, out_shape=out_shape, grid=(s // tr,), in_specs=in_specs, out_specs=out_specs,
        compiler_params=_params(dimension_semantics=("arbitrary",)),
    )(*[t[0] for t in tiles], *params)
    return res


def _sigmoid(z):
    return 1.0 / (1.0 + jnp.exp(-z))


def _silu(z):
    return z * _sigmoid(z)


_GELU_K = math.sqrt(2.0 / math.pi)


def _gelu(y):
    return 0.5 * y * (1.0 + jnp.tanh(_GELU_K * (y + 0.044715 * y * y * y)))


def _gelu_grad(y):
    th = jnp.tanh(_GELU_K * (y + 0.044715 * y * y * y))
    return 0.5 * (1.0 + th) + 0.5 * y * (1.0 - th * th) * _GELU_K * (1.0 + 3.0 * 0.044715 * y * y)


def _rms(x):
    return lax.rsqrt(jnp.mean(x * x, axis=-1, keepdims=True) + RMS_EPS)


def _norm_bwd(dn, xhat, r):
    return r * (dn - xhat * jnp.mean(dn * xhat, axis=-1, keepdims=True))


def _cum_fwd(flog, bf_row, nh, *, name):
    s = flog.shape[0]
    w = nh * HEAD_DIM
    t = _fit(s, 256, SUBLANES)

    def kern(f_ref, b_ref, o_ref, carry_ref):
        i = pl.program_id(0)

        @pl.when(i == 0)
        def _():
            carry_ref[...] = jnp.zeros_like(carry_ref)

        z = f_ref[...] + b_ref[...]
        logf = jnp.minimum(z, 0.0) - jnp.log(1.0 + jnp.exp(-jnp.abs(z)))
        hh = lax.broadcasted_iota(jnp.int32, (LANES, w), 0)
        cc = lax.broadcasted_iota(jnp.int32, (LANES, w), 1)
        expand = (cc // HEAD_DIM == hh).astype(F32)
        lx = jnp.dot(logf, expand, preferred_element_type=F32, precision=HI)
        rr = lax.broadcasted_iota(jnp.int32, (t, t), 0)
        kk = lax.broadcasted_iota(jnp.int32, (t, t), 1)
        tri = (kk <= rr).astype(F32)
        cum = jnp.dot(tri, lx, preferred_element_type=F32, precision=HI) + carry_ref[...]
        o_ref[...] = cum
        carry_ref[...] = cum[t - 1:t, :]

    return pl.pallas_call(
        kern, name=name, out_shape=jax.ShapeDtypeStruct((s, w), F32), grid=(s // t,),
        in_specs=[pl.BlockSpec((t, LANES), lambda i: (i, 0)), pl.BlockSpec((1, LANES), lambda i: (0, 0))],
        out_specs=pl.BlockSpec((t, w), lambda i: (i, 0)),
        scratch_shapes=[pltpu.VMEM((1, w), F32)],
        compiler_params=_params(dimension_semantics=("arbitrary",)),
    )(flog, bf_row)


def _cum_bwd(dcrow, frow, bf_col, *, name):
    _, nh, s = dcrow.shape
    t = _fit(s, 512, LANES)
    nb = s // t

    def kern(d_ref, f_ref, b_ref, df_ref, db_ref):
        rr = lax.broadcasted_iota(jnp.int32, (t, t), 0)
        kk = lax.broadcasted_iota(jnp.int32, (t, t), 1)
        upper = (rr >= kk).astype(F32)
        carry = jnp.zeros((nh, 1), F32)
        db = jnp.zeros((nh, 1), F32)
        for blk in range(nb - 1, -1, -1):
            sl = slice(blk * t, (blk + 1) * t)
            rc = jnp.dot(d_ref[0, :, sl] + d_ref[1, :, sl], upper, preferred_element_type=F32, precision=HI) + carry
            carry = rc[:, 0:1]
            df = rc * _sigmoid(-(f_ref[:, sl] + b_ref[...]))
            df_ref[:, sl] = df
            db = db + jnp.sum(df, axis=1, keepdims=True)
        db_ref[...] = db

    return pl.pallas_call(
        kern, name=name,
        out_shape=[jax.ShapeDtypeStruct((nh, s), F32), jax.ShapeDtypeStruct((nh, 1), F32)],
        compiler_params=_params(),
    )(dcrow, frow, bf_col)


def _ride_split(ex, refs, n_in, n_out):
    n = ex.n if ex is not None else 0
    own_in, srcs = refs[:n_in], refs[n_in:n_in + n]
    own_out, dsts = refs[n_in + n:n_in + n + n_out], refs[n_in + n + n_out:n_in + 2 * n + n_out]
    sems = refs[n_in + 2 * n + n_out:n_in + 2 * n + n_out + 3] if n else ()
    rest = refs[n_in + 2 * n + n_out + (3 if n else 0):]
    return own_in, own_out, (srcs, dsts, sems), rest


ATTN_STRIP = 32
BIAS_LANES = 3


def _head_masks(rows):
    lane = lax.broadcasted_iota(jnp.int32, (rows, LANES), 1)
    return [(lane >= HEAD_DIM * e) & (lane < HEAD_DIM * (e + 1)) for e in range(2)]


def _augment(feat, bias, e, *, bias_slot, ones_slot):
    rows = feat.shape[0]
    lane = lax.broadcasted_iota(jnp.int32, (rows, LANES), 1)
    own = (lane >= HEAD_DIM * e) & (lane < HEAD_DIM * (e + 1))
    off = lane - HEAD_DIM * (1 - e)
    out = jnp.where(own, feat, 0.0)
    if ones_slot is not None:
        out = jnp.where((off >= ones_slot * BIAS_LANES) & (off < (ones_slot + 1) * BIAS_LANES), 1.0, out)
    if bias is not None:
        rest = pltpu.roll(bias, HEAD_DIM, 1)
        for term in range(BIAS_LANES):
            part = rest.astype(BF).astype(F32)
            out = jnp.where(off == bias_slot * BIAS_LANES + term, part, out)
            rest = rest - part
    return out.astype(BF)


def _two_slot_pipeline(m, scores, tile):
    scores(0, 0)

    def pair(n, carry):
        k = 2 * n
        scores(k + 1, 1)
        tile(k, 0, False)
        scores(k + 2, 0)
        tile(k + 1, 1, False)
        return carry

    lax.fori_loop(0, m // 2, pair, 0)

    @pl.when(m % 2 == 0)
    def _():
        tile(m, 0, True)

    @pl.when(m % 2 == 1)
    def _():
        scores(m, 1)
        tile(m - 1, 0, False)
        tile(m, 1, True)


def _attn_fwd(proj, cumx, qcol, *, name, ride=None):
    s = proj.shape[0]
    w = cumx.shape[1]
    nhp = w // LANES
    t = _fit(s, 256, LANES)
    nq = s // t
    strip = _fit(t, ATTN_STRIP, 16)
    scale = HEAD_DIM ** -0.5
    qb, kb, vb = qcol // LANES, (qcol + w) // LANES, (qcol + 2 * w) // LANES
    ex, ex_arrays = ride if ride is not None else (None, [])
    nt_dims = (((1,), (1,)), ((), ()))

    def kern(*refs):
        own_in, (o_ref, l_ref), comm, scratch = _ride_split(ex, refs, 5, 2)
        q_ref, k_ref, v_ref, cxq_ref, cxk_ref = own_in
        ka_ref, vat_ref, s0_ref, s1_ref, p_ref, m_ref, acc_ref = scratch
        s_refs = (s0_ref, s1_ref)
        i = pl.program_id(1)
        if ex is not None:
            @pl.when((pl.program_id(0) == 0) & (i == 0))
            def _():
                ex.start(*comm)

        msks = _head_masks(t)

        @pl.when(i == 0)
        def _():
            def build(c, carry):
                rows = pl.ds(pl.multiple_of(c * t, LANES), t)
                k2, v2, cx = k_ref[rows, :], v_ref[rows, :], cxk_ref[rows, :]
                for e in range(2):
                    ka_ref[e, rows, :] = _augment(k2, -cx, e, bias_slot=1, ones_slot=0)
                    vat_ref[e, :, rows] = jnp.where(msks[e], v2, 1.0).T.astype(BF)
                return carry
            lax.fori_loop(0, nq, build, 0)

        q2 = q_ref[...] * scale
        qa = [_augment(q2, cxq_ref[...], e, bias_slot=0, ones_slot=1) for e in range(2)]
        m_ref[...] = jnp.full(m_ref.shape, NEG, F32)
        acc_ref[...] = jnp.zeros(acc_ref.shape, F32)
        slabs = strip // SUBLANES

        def scores(j, slot):
            rows_k = pl.ds(pl.multiple_of(j * t, LANES), t)
            for e in range(2):
                st = lax.dot_general(ka_ref[e, rows_k, :], qa[e], nt_dims, preferred_element_type=F32)
                s_refs[slot][e] = st.reshape(t // SUBLANES, SUBLANES, t)

        def tile(j, slot, diagonal):
            rows_k = pl.ds(pl.multiple_of(j * t, LANES), t)
            s_ref = s_refs[slot]
            for e in range(2):
                mx = jnp.full((SUBLANES, t), NEG, F32)
                for r in range(t // strip):
                    sl = slice(r * slabs, (r + 1) * slabs)
                    sv = s_ref[e,sl]
                    if diagonal:
                        shape = (slabs, SUBLANES, t)
                        key = (r * strip + lax.broadcasted_iota(jnp.int32, shape, 0) * SUBLANES
                               + lax.broadcasted_iota(jnp.int32, shape, 1))
                        sv = jnp.where(key <= lax.broadcasted_iota(jnp.int32, shape, 2), sv, NEG)
                        s_ref[e,sl] = sv
                    mx = jnp.maximum(mx, jnp.max(sv, axis=0))
                for sh in (4, 2, 1):
                    mx = jnp.maximum(mx, pltpu.roll(mx, sh, 0))
                m_old = m_ref[e]
                m_new = jnp.maximum(m_old, mx)
                alpha = jnp.exp(m_old - m_new)
                m_ref[e] = m_new
                for r in range(t // strip):
                    p = jnp.exp(s_ref[e,r * slabs:(r + 1) * slabs] - m_new[None])
                    p_ref[e, r * strip:(r + 1) * strip, :] = p.reshape(strip, t).astype(BF)
                acc = acc_ref[e].reshape(LANES // SUBLANES, SUBLANES, t) * alpha[None]
                acc_ref[e] = acc.reshape(LANES, t) + jnp.dot(vat_ref[e, :, rows_k], p_ref[e],
                                                             preferred_element_type=F32)

        _two_slot_pipeline(i, scores, tile)

        outs, lses = [], []
        for e in range(2):
            acc = acc_ref[e]
            other = HEAD_DIM * (1 - e)
            den = acc[other:other + 1, :]
            outs.append(acc / den)
            lses.append(jnp.broadcast_to(m_ref[e][0:1, :] + jnp.log(den), (LANES, t)))
        upper = lax.broadcasted_iota(jnp.int32, (LANES, t), 0) < HEAD_DIM
        o_ref[...] = jnp.where(upper, outs[0], outs[1]).T.astype(BF)
        l_ref[...] = jnp.where(upper, lses[0], lses[1]).T
        if ex is not None:
            @pl.when((pl.program_id(0) == nhp - 1) & (i == nq - 1))
            def _():
                ex.wait(*comm)

    own_scratch = [pltpu.VMEM((2, s, LANES), BF), pltpu.VMEM((2, LANES, s), BF),
                   pltpu.VMEM((2, t // SUBLANES, SUBLANES, t), F32),
                   pltpu.VMEM((2, t // SUBLANES, SUBLANES, t), F32), pltpu.VMEM((2, t, t), BF),
                   pltpu.VMEM((2, SUBLANES, t), F32), pltpu.VMEM((2, LANES, t), F32)]
    return pl.pallas_call(
        kern, name=name,
        out_shape=[jax.ShapeDtypeStruct((s, w), BF), jax.ShapeDtypeStruct((nhp, s, LANES), F32)]
        + (ex.out_shape if ex else []),
        grid=(nhp, nq),
        in_specs=[pl.BlockSpec((t, LANES), lambda h, i: (i, qb + h)),
                  pl.BlockSpec((s, LANES), lambda h, i: (0, kb + h)),
                  pl.BlockSpec((s, LANES), lambda h, i: (0, vb + h)),
                  pl.BlockSpec((t, LANES), lambda h, i: (i, h)),
                  pl.BlockSpec((s, LANES), lambda h, i: (0, h))] + (ex.specs if ex else []),
        out_specs=[pl.BlockSpec((t, LANES), lambda h, i: (i, h)),
                   pl.BlockSpec((None, t, LANES), lambda h, i: (h, i, 0))] + (ex.specs if ex else []),
        scratch_shapes=(ex.scratch if ex else []) + own_scratch,
        compiler_params=_params(dimension_semantics=("arbitrary", "arbitrary"),
                                has_side_effects=ex is not None),
    )(proj, proj, proj, cumx, cumx, *ex_arrays)


def _attn_bwd(proj, do, o, lse, cumx, qcol, *, name, ride=None):
    s = proj.shape[0]
    w = cumx.shape[1]
    nhp = w // LANES
    t = _fit(s, 256, LANES)
    nq = s // t
    strip = _fit(t, ATTN_STRIP, 16)
    scale = HEAD_DIM ** -0.5
    qb, kb, vb = qcol // LANES, (qcol + w) // LANES, (qcol + 2 * w) // LANES
    tn_dims = (((0,), (0,)), ((), ()))
    nt_dims = (((1,), (1,)), ((), ()))
    ex, ex_arrays = ride if ride is not None else (None, [])

    def kern(*refs):
        own_in, own_out, comm, scratch = _ride_split(ex, refs, 7, 5)
        q_ref, k_ref, v_ref, do_ref, o_ref, l_ref, cx_ref = own_in
        dq_ref, dk_ref, dv_ref, dkc_ref, dqc_ref = own_out
        qa_ref, da_ref, dqa_ref, dka_ref, dva_ref, st0_ref, st1_ref, dpt0_ref, dpt1_ref, pt_ref, dst_ref = scratch
        st_refs, dpt_refs = (st0_ref, st1_ref), (dpt0_ref, dpt1_ref)
        j = pl.program_id(1)
        if ex is not None:
            @pl.when((pl.program_id(0) == 0) & (j == 0))
            def _():
                ex.start(*comm)

        msks = _head_masks(t)

        @pl.when(j == 0)
        def _():
            def build(c, carry):
                rows = pl.ds(pl.multiple_of(c * t, LANES), t)
                q2 = q_ref[rows, :] * scale
                do2 = do_ref[rows, :]
                dd = do2 * o_ref[rows, :].astype(F32)
                delta = jnp.where(msks[0], jnp.sum(jnp.where(msks[0], dd, 0.0), axis=1, keepdims=True),
                                  jnp.sum(jnp.where(msks[1], dd, 0.0), axis=1, keepdims=True))
                bias = cx_ref[rows, :] - l_ref[rows, :]
                for e in range(2):
                    qa_ref[e, rows, :] = _augment(q2, bias, e, bias_slot=0, ones_slot=1)
                    da_ref[e, rows, :] = _augment(do2, -delta, e, bias_slot=0, ones_slot=None)
                return carry
            lax.fori_loop(0, nq, build, 0)
            dqa_ref[...] = jnp.zeros(dqa_ref.shape, F32)

        rows_k = pl.ds(pl.multiple_of(j * t, LANES), t)
        k2, v2 = k_ref[...], v_ref[...]
        ka = [_augment(k2, -cx_ref[rows_k, :], e, bias_slot=1, ones_slot=0) for e in range(2)]
        va = [_augment(v2, None, e, bias_slot=None, ones_slot=0) for e in range(2)]
        dka_ref[...] = jnp.zeros(dka_ref.shape, F32)
        dva_ref[...] = jnp.zeros(dva_ref.shape, F32)

        def scores(k, slot):
            rows_q = pl.ds(pl.multiple_of((nq - 1 - k) * t, LANES), t)
            for e in range(2):
                st_refs[slot][e] = lax.dot_general(ka[e], qa_ref[e, rows_q, :], nt_dims,
                                                   preferred_element_type=F32)
                dpt_refs[slot][e] = lax.dot_general(va[e], da_ref[e, rows_q, :], nt_dims,
                                                    preferred_element_type=F32)

        def tile(k, slot, diagonal):
            rows_q = pl.ds(pl.multiple_of((nq - 1 - k) * t, LANES), t)
            st_ref, dpt_ref = st_refs[slot], dpt_refs[slot]
            for e in range(2):
                for r in range(t // strip):
                    rows = slice(r * strip, (r + 1) * strip)
                    sv = st_ref[e, rows, :]
                    if diagonal:
                        key = r * strip + lax.broadcasted_iota(jnp.int32, (strip, t), 0)
                        qry = lax.broadcasted_iota(jnp.int32, (strip, t), 1)
                        sv = jnp.where(key <= qry, sv, NEG)
                    p = jnp.exp(sv)
                    pt_ref[e, rows, :] = p.astype(BF)
                    dst_ref[e, rows, :] = (p * dpt_ref[e, rows, :]).astype(BF)
            for e in range(2):
                dva_ref[e] += jnp.dot(pt_ref[e], da_ref[e, rows_q, :], preferred_element_type=F32)
                dka_ref[e] += jnp.dot(dst_ref[e], qa_ref[e, rows_q, :], preferred_element_type=F32)
                dqa_ref[e, rows_q, :] += lax.dot_general(dst_ref[e], ka[e], tn_dims, preferred_element_type=F32)

        _two_slot_pipeline(nq - 1 - j, scores, tile)

        dk_ref[...] = jnp.where(msks[0], dka_ref[0], dka_ref[1])
        dv_ref[...] = jnp.where(msks[0], dva_ref[0], dva_ref[1])
        dkc_ref[...] = jnp.where(msks[0], pltpu.roll(dka_ref[0], HEAD_DIM, 1), pltpu.roll(dka_ref[1], HEAD_DIM, 1))

        @pl.when(j == nq - 1)
        def _():
            def flush(c, carry):
                rows = pl.ds(pl.multiple_of(c * t, LANES), t)
                a0, a1 = dqa_ref[0, rows, :], dqa_ref[1, rows, :]
                dq_ref[rows, :] = jnp.where(msks[0], a0, a1) * scale
                dqc_ref[rows, :] = jnp.where(msks[0], pltpu.roll(a0, HEAD_DIM, 1), pltpu.roll(a1, HEAD_DIM, 1))
                return carry
            lax.fori_loop(0, nq, flush, 0)

        if ex is not None:
            @pl.when((pl.program_id(0) == nhp - 1) & (j == nq - 1))
            def _():
                ex.wait(*comm)

    full = lambda cb: pl.BlockSpec((s, LANES), lambda h, j: (0, cb + h))
    blk = lambda cb: pl.BlockSpec((t, LANES), lambda h, j: (j, cb + h))
    own_scratch = [pltpu.VMEM((2, s, LANES), BF), pltpu.VMEM((2, s, LANES), BF), pltpu.VMEM((2, s, LANES), F32),
                   pltpu.VMEM((2, t, LANES), F32), pltpu.VMEM((2, t, LANES), F32),
                   pltpu.VMEM((2, t, t), F32), pltpu.VMEM((2, t, t), F32),
                   pltpu.VMEM((2, t, t), F32), pltpu.VMEM((2, t, t), F32),
                   pltpu.VMEM((2, t, t), BF), pltpu.VMEM((2, t, t), BF)]
    return pl.pallas_call(
        kern, name=name,
        out_shape=[jax.ShapeDtypeStruct((s, w), F32)] * 5 + (ex.out_shape if ex else []),
        grid=(nhp, nq),
        in_specs=[full(qb), blk(kb), blk(vb), full(0), full(0),
                  pl.BlockSpec((None, s, LANES), lambda h, j: (h, 0, 0)), full(0)] + (ex.specs if ex else []),
        out_specs=[full(0), blk(0), blk(0), blk(0), full(0)] + (ex.specs if ex else []),
        scratch_shapes=(ex.scratch if ex else []) + own_scratch,
        compiler_params=_params(dimension_semantics=("arbitrary", "arbitrary"),
                                has_side_effects=ex is not None),
    )(proj, proj, proj, do, o, lse, cumx, *ex_arrays)


S5_STATES = 256
S5_ROWS = 512


def _cmul(ar, ai, br, bi):
    return ar * br - ai * bi, ar * bi + ai * br


def _scan_tables(lr, li, reverse):
    w = lr.shape[1]
    row = lax.broadcasted_iota(jnp.int32, (SUBLANES, w), 0)
    if reverse:
        row = SUBLANES - 1 - row
    lr1, li1 = jnp.broadcast_to(lr, (SUBLANES, w)), jnp.broadcast_to(li, (SUBLANES, w))
    lr2, li2 = _cmul(lr1, li1, lr1, li1)
    lr4, li4 = _cmul(lr2, li2, lr2, li2)
    steps = []
    for d, (pr, pi) in zip((1, 2, 4), ((lr1, li1), (lr2, li2), (lr4, li4))):
        keep = row >= d
        steps.append((jnp.where(keep, pr, 0.0), jnp.where(keep, pi, 0.0)))
    cr, ci = lr1, li1
    for bit, (pr, pi) in zip((1, 2, 4), ((lr1, li1), (lr2, li2), (lr4, li4))):
        nr, ni = _cmul(cr, ci, pr, pi)
        has = (row & bit) != 0
        cr, ci = jnp.where(has, nr, cr), jnp.where(has, ni, ci)
    return steps, (cr, ci)


def _scan_block(xr, xi, car_r, car_i, steps, carry_pow, reverse):
    for d, (pr, pi) in zip((1, 2, 4), steps):
        sh = (SUBLANES - d) if reverse else d
        sr, si = pltpu.roll(xr, sh, 0), pltpu.roll(xi, sh, 0)
        xr, xi = xr + (pr * sr - pi * si), xi + (pr * si + pi * sr)
    cr, ci = carry_pow
    xr, xi = xr + (cr * car_r - ci * car_i), xi + (cr * car_i + ci * car_r)
    return xr, xi


def _s5_specs(s, ncb):
    u_spec = pl.BlockSpec((s, LANES), lambda cb, hf: (0, cb))
    wb_spec = pl.BlockSpec((LANES, S5_STATES), lambda cb, hf: (cb, 2 * cb + hf))
    wc_spec = pl.BlockSpec((S5_STATES, LANES), lambda cb, hf: (2 * cb + hf, cb))
    lam_spec = pl.BlockSpec((1, S5_STATES), lambda cb, hf: (0, 2 * cb + hf))
    d_spec = pl.BlockSpec((1, LANES), lambda cb, hf: (0, cb))
    return u_spec, wb_spec, wc_spec, lam_spec, d_spec


def _s5_project_and_scan(u_ref, wbr_ref, wbi_ref, lr_ref, li_ref, xr_ref, xi_ref, s, rows):
    wbr, wbi = wbr_ref[...], wbi_ref[...]
    for r in range(s // rows):
        sl = pl.ds(r * rows, rows)
        ub = u_ref[sl, :].astype(BF)
        xr_ref[sl, :] = jnp.dot(ub, wbr, preferred_element_type=F32)
        xi_ref[sl, :] = jnp.dot(ub, wbi, preferred_element_type=F32)
    steps, cpow = _scan_tables(lr_ref[...], li_ref[...], False)

    def body(b, carry):
        car_r, car_i = carry
        sl = pl.ds(pl.multiple_of(b * SUBLANES, SUBLANES), SUBLANES)
        xr, xi = _scan_block(xr_ref[sl, :], xi_ref[sl, :], car_r, car_i, steps, cpow, False)
        xr_ref[sl, :] = xr
        xi_ref[sl, :] = xi
        return xr[SUBLANES - 1:SUBLANES, :], xi[SUBLANES - 1:SUBLANES, :]

    zero = jnp.zeros((1, S5_STATES), F32)
    lax.fori_loop(0, s // SUBLANES, body, (zero, zero), unroll=4)


def _s5_fwd(proj, wb_re, wb_im, wc_re, wc_im, lam_re, lam_im, dskip, *, name):
    s = proj.shape[0]
    w = dskip.shape[1]
    ncb = w // LANES
    rows = _fit(s, S5_ROWS, SUBLANES)

    def kern(u_ref, wbr_ref, wbi_ref, wcr_ref, wci_ref, lr_ref, li_ref, d_ref, y_ref, xr_ref, xi_ref):
        hf = pl.program_id(1)
        _s5_project_and_scan(u_ref, wbr_ref, wbi_ref, lr_ref, li_ref, xr_ref, xi_ref, s, rows)
        wcr, wci = wcr_ref[...], wci_ref[...]
        for r in range(s // rows):
            sl = pl.ds(r * rows, rows)
            y = (jnp.dot(xr_ref[sl, :].astype(BF), wcr, preferred_element_type=F32)
                 - jnp.dot(xi_ref[sl, :].astype(BF), wci, preferred_element_type=F32))

            @pl.when(hf == 0)
            def _(y=y, sl=sl):
                y_ref[sl, :] = y + d_ref[...] * u_ref[sl, :]

            @pl.when(hf == 1)
            def _(y=y, sl=sl):
                y_ref[sl, :] += y

    u_spec, wb_spec, wc_spec, lam_spec, d_spec = _s5_specs(s, ncb)
    return pl.pallas_call(
        kern, name=name, out_shape=jax.ShapeDtypeStruct((s, w), F32), grid=(ncb, 2),
        in_specs=[u_spec, wb_spec, wb_spec, wc_spec, wc_spec, lam_spec, lam_spec, d_spec],
        out_specs=u_spec,
        scratch_shapes=[pltpu.VMEM((s, S5_STATES), F32), pltpu.VMEM((s, S5_STATES), F32)],
        compiler_params=_params(dimension_semantics=("parallel", "arbitrary")),
    )(proj, wb_re, wb_im, wc_re, wc_im, lam_re, lam_im, dskip)


def _s5_bwd(proj, dy, wb_re, wb_im, wc_re, wc_im, lam_re, lam_im, dskip, *, name):
    s = proj.shape[0]
    w = dskip.shape[1]
    ncb = w // LANES
    rows = _fit(s, S5_ROWS, SUBLANES)
    tn_dims = (((0,), (0,)), ((), ()))
    nt_dims = (((1,), (1,)), ((), ()))

    def kern(u_ref, dy_ref, wbr_ref, wbi_ref, wcr_ref, wci_ref, lr_ref, li_ref, d_ref,
             du_ref, dwbr_ref, dwbi_ref, dwcr_ref, dwci_ref, dlr_ref, dli_ref, dd_ref,
             xr_ref, xi_ref, gr_ref, gi_ref):
        hf = pl.program_id(1)
        _s5_project_and_scan(u_ref, wbr_ref, wbi_ref, lr_ref, li_ref, xr_ref, xi_ref, s, rows)

        wcr, wci = wcr_ref[...], wci_ref[...]
        dwcr = jnp.zeros((S5_STATES, LANES), F32)
        dwci = jnp.zeros((S5_STATES, LANES), F32)
        ddsk = jnp.zeros((1, LANES), F32)
        for r in range(s // rows):
            sl = pl.ds(r * rows, rows)
            dyf = dy_ref[sl, :]
            dyb = dyf.astype(BF)
            gr_ref[sl, :] = lax.dot_general(dyb, wcr, nt_dims, preferred_element_type=F32)
            gi_ref[sl, :] = -lax.dot_general(dyb, wci, nt_dims, preferred_element_type=F32)
            dwcr = dwcr + lax.dot_general(xr_ref[sl, :].astype(BF), dyb, tn_dims, preferred_element_type=F32)
            dwci = dwci - lax.dot_general(xi_ref[sl, :].astype(BF), dyb, tn_dims, preferred_element_type=F32)
            ddsk = ddsk + jnp.sum(dyf * u_ref[sl, :], axis=0, keepdims=True)
        dwcr_ref[...] = dwcr
        dwci_ref[...] = dwci

        @pl.when(hf == 0)
        def _():
            dd_ref[...] = ddsk

        steps, cpow = _scan_tables(lr_ref[...], -li_ref[...], True)
        row = lax.broadcasted_iota(jnp.int32, (SUBLANES, S5_STATES), 0)
        nblk = s // SUBLANES

        def body(k, carry):
            car_r, car_i, ar, ai = carry
            b = nblk - 1 - k
            sl = pl.ds(pl.multiple_of(b * SUBLANES, SUBLANES), SUBLANES)
            g_r, g_i = _scan_block(gr_ref[sl, :], gi_ref[sl, :], car_r, car_i, steps, cpow, True)
            gr_ref[sl, :] = g_r
            gi_ref[sl, :] = g_i
            nr = jnp.where(row == SUBLANES - 1, car_r, pltpu.roll(g_r, SUBLANES - 1, 0))
            ni = jnp.where(row == SUBLANES - 1, car_i, pltpu.roll(g_i, SUBLANES - 1, 0))
            xr, xi = xr_ref[sl, :], xi_ref[sl, :]
            ar = ar + (xr * nr + xi * ni)
            ai = ai + (xr * ni - xi * nr)
            return g_r[0:1, :], g_i[0:1, :], ar, ai

        zero = jnp.zeros((1, S5_STATES), F32)
        zacc = jnp.zeros((SUBLANES, S5_STATES), F32)
        _, _, ar, ai = lax.fori_loop(0, nblk, body, (zero, zero, zacc, zacc), unroll=4)
        dlr_ref[...] = jnp.sum(ar, axis=0, keepdims=True)
        dli_ref[...] = jnp.sum(ai, axis=0, keepdims=True)

        wbr, wbi = wbr_ref[...], wbi_ref[...]
        dwbr = jnp.zeros((LANES, S5_STATES), F32)
        dwbi = jnp.zeros((LANES, S5_STATES), F32)
        for r in range(s // rows):
            sl = pl.ds(r * rows, rows)
            grb, gib = gr_ref[sl, :].astype(BF), gi_ref[sl, :].astype(BF)
            ub = u_ref[sl, :].astype(BF)
            dwbr = dwbr + lax.dot_general(ub, grb, tn_dims, preferred_element_type=F32)
            dwbi = dwbi + lax.dot_general(ub, gib, tn_dims, preferred_element_type=F32)
            du = (lax.dot_general(grb, wbr, nt_dims, preferred_element_type=F32)
                  + lax.dot_general(gib, wbi, nt_dims, preferred_element_type=F32))

            @pl.when(hf == 0)
            def _(du=du, sl=sl):
                du_ref[sl, :] = du + d_ref[...] * dy_ref[sl, :]

            @pl.when(hf == 1)
            def _(du=du, sl=sl):
                du_ref[sl, :] += du
        dwbr_ref[...] = dwbr
        dwbi_ref[...] = dwbi

    u_spec, wb_spec, wc_spec, lam_spec, d_spec = _s5_specs(s, ncb)
    dwb_spec = pl.BlockSpec((None, None, LANES, S5_STATES), lambda cb, hf: (cb, hf, 0, 0))
    dwc_spec = pl.BlockSpec((None, None, S5_STATES, LANES), lambda cb, hf: (cb, hf, 0, 0))
    state = pltpu.VMEM((s, S5_STATES), F32)
    return pl.pallas_call(
        kern, name=name,
        out_shape=[jax.ShapeDtypeStruct((s, w), F32),
                   jax.ShapeDtypeStruct((ncb, 2, LANES, S5_STATES), F32), jax.ShapeDtypeStruct((ncb, 2, LANES, S5_STATES), F32),
                   jax.ShapeDtypeStruct((ncb, 2, S5_STATES, LANES), F32), jax.ShapeDtypeStruct((ncb, 2, S5_STATES, LANES), F32),
                   jax.ShapeDtypeStruct((1, 4 * w), F32), jax.ShapeDtypeStruct((1, 4 * w), F32),
                   jax.ShapeDtypeStruct((1, w), F32)],
        grid=(ncb, 2),
        in_specs=[u_spec, u_spec, wb_spec, wb_spec, wc_spec, wc_spec, lam_spec, lam_spec, d_spec],
        out_specs=[u_spec, dwb_spec, dwb_spec, dwc_spec, dwc_spec, lam_spec, lam_spec, d_spec],
        scratch_shapes=[state, state, state, state],
        compiler_params=_params(dimension_semantics=("parallel", "arbitrary")),
    )(proj, dy, wb_re, wb_im, wc_re, wc_im, lam_re, lam_im, dskip)


def _s5_discretise(lam_re, lam_im, log_dt, b_re, b_im):
    lr = jnp.minimum(lam_re, -EIG_CLIP)
    li = lam_im
    dt = jnp.exp(log_dt)[:, None]
    mag = jnp.exp(lr * dt)
    lbr, lbi = mag * jnp.cos(li * dt), mag * jnp.sin(li * dt)
    den = lr * lr + li * li
    fr = ((lbr - 1.0) * lr + lbi * li) / den
    fi = (lbi * lr - (lbr - 1.0) * li) / den
    bbr = fr[..., None] * b_re - fi[..., None] * b_im
    bbi = fr[..., None] * b_im + fi[..., None] * b_re
    return lbr, lbi, bbr, bbi


def _block_diag(blocks):
    g, a, b = blocks.shape
    eye = jnp.eye(g, dtype=blocks.dtype)
    return (blocks[:, :, None, :] * eye[:, None, :, None]).reshape(g * a, g * b)


def _s5_block_grads(dwb, a, b, transpose):
    ncb = dwb.shape[0]
    gl = LANES // 2 // (a if not transpose else b)
    if not transpose:
        d = dwb.reshape(ncb, 2, 2, gl, a, gl, b)
        parts = [[d[:, hf, hf, g, :, g, :] for g in range(gl)] for hf in range(2)]
    else:
        d = dwb.reshape(ncb, 2, gl, a, 2, gl, b)
        parts = [[d[:, hf, g, :, hf, g, :] for g in range(gl)] for hf in range(2)]
    st = jnp.stack([jnp.stack(p, axis=1) for p in parts], axis=1)
    return st.reshape(ncb * 2 * gl, a, b)


def _adamw(parts, w, m, v, *, name):
    depth, r, c = w.shape
    assert len(parts) == depth
    npart = parts[0].shape[0]
    row_bytes = 4 * (-(-c // LANES) * LANES)
    align = 16 if parts[0].dtype == BF else SUBLANES
    budget = VMEM_LIMIT // 2 // (2 * (depth * npart + 7) * row_bytes)
    tr = _fit(r, max(align, budget // align * align), align)
    nr = r // tr
    c1 = 1.0 / (1.0 - ADAM_B1 ** ADAM_STEP)
    c2 = 1.0 / (1.0 - ADAM_B2 ** ADAM_STEP)

    def kern(*refs):
        p_refs = refs[:depth]
        w_ref, m_ref, v_ref, g_ref, d_ref, nm_ref, nv_ref = refs[depth:]
        layer = pl.program_id(0)
        for l in range(depth):
            @pl.when(layer == l)
            def _(p_ref=p_refs[l]):
                g = p_ref[0].astype(F32)
                for q in range(1, npart):
                    g = g + p_ref[q].astype(F32)
                m2 = ADAM_B1 * m_ref[...] + (1.0 - ADAM_B1) * g
                v2 = ADAM_B2 * v_ref[...] + (1.0 - ADAM_B2) * (g * g)
                upd = (m2 * c1) / (jnp.sqrt(v2 * c2) + ADAM_EPS) + ADAM_WD * w_ref[...]
                g_ref[...] = g
                d_ref[...] = -ADAM_LR * upd
                nm_ref[...] = m2
                nv_ref[...] = v2

    def part_spec(l):
        return pl.BlockSpec((npart, tr, c),
                            lambda ly, i: (0, jnp.where(ly == l, i, jnp.where(ly < l, 0, nr - 1)), 0))

    spec = pl.BlockSpec((None, tr, c), lambda ly, i: (ly, i, 0))
    return pl.pallas_call(
        kern, name=name, out_shape=[jax.ShapeDtypeStruct((depth, r, c), F32)] * 4, grid=(depth, nr),
        in_specs=[part_spec(l) for l in range(depth)] + [spec, spec, spec],
        out_specs=[spec] * 4,
        compiler_params=_params(dimension_semantics=("arbitrary", "arbitrary")),
    )(*parts, w, m, v)


def _sum_parts(parts, *, name):
    npart, r, c = parts.shape

    def kern(p_ref, o_ref):
        g = p_ref[0]
        for q in range(1, npart):
            g = g + p_ref[q]
        o_ref[...] = g

    return pl.pallas_call(kern, name=name, out_shape=jax.ShapeDtypeStruct((r, c), F32), compiler_params=_params())(parts)


class _Exchange:
    def __init__(self, arrays, gather):
        self.n = len(arrays)
        self.gather = gather
        self.out_shape = [jax.ShapeDtypeStruct(((NDEV,) + a.shape) if gather else a.shape, a.dtype) for a in arrays]
        self.scratch = [pltpu.SemaphoreType.DMA((self.n, NDEV - 1)), pltpu.SemaphoreType.DMA((self.n, NDEV - 1)),
                        pltpu.SemaphoreType.DMA((self.n,))]
        self.specs = [pl.BlockSpec(memory_space=pl.ANY)] * self.n

    def _copies(self, srcs, dsts, sems):
        send_sems, recv_sems, local_sems = sems
        x, y, c = lax.axis_index("x"), lax.axis_index("y"), lax.axis_index("c")
        me = 4 * x + 2 * y + c
        local = [pltpu.make_async_copy(srcs[a] if self.gather else srcs[a].at[me], dsts[a].at[me], local_sems.at[a])
                 for a in range(self.n)]
        remote = []
        for k in (1, 2, 4, 3, 5, 6, 7):
            px, py, pc = x ^ ((k >> 2) & 1), y ^ ((k >> 1) & 1), c ^ (k & 1)
            peer = 4 * px + 2 * py + pc
            for a in range(self.n):
                src = srcs[a] if self.gather else srcs[a].at[peer]
                mk = functools.partial(
                    pltpu.make_async_remote_copy, src_ref=src,
                    send_sem=send_sems.at[a, k - 1], recv_sem=recv_sems.at[a, k - 1],
                    device_id=(px, py, pc), device_id_type=pl.DeviceIdType.MESH)
                remote.append((mk(dst_ref=dsts[a].at[me]), mk(dst_ref=dsts[a].at[peer])))
        return local, remote

    def start(self, srcs, dsts, sems):
        local, remote = self._copies(srcs, dsts, sems)
        for cp in local:
            cp.start()
        for send, _ in remote:
            send.start()

    def wait(self, srcs, dsts, sems):
        local, remote = self._copies(srcs, dsts, sems)
        for send, arrival in remote:
            send.wait_send()
            arrival.wait_recv()
        for cp in local:
            cp.wait()


def _exchange(arrays, gather, *, name):
    ex = _Exchange(arrays, gather)
    n = ex.n

    def kern(*refs):
        srcs, dsts, sems = refs[:n], refs[n:2 * n], refs[2 * n:]
        ex.start(srcs, dsts, sems)
        ex.wait(srcs, dsts, sems)

    return pl.pallas_call(
        kern, name=name, out_shape=ex.out_shape, in_specs=ex.specs, out_specs=ex.specs, scratch_shapes=ex.scratch,
        compiler_params=pltpu.CompilerParams(has_side_effects=True),
    )(*arrays)


def _pack(arrays):
    flat = jnp.concatenate([a.reshape(-1).astype(F32) for a in arrays])
    pad = (-flat.shape[0]) % (SUBLANES * LANES)
    return jnp.pad(flat, (0, pad)).reshape(-1, LANES)


def _unpack(buf, like):
    flat = buf.reshape(-1)
    out, off = [], 0
    for a in like:
        sz = math.prod(a.shape)
        out.append(flat[off:off + sz].reshape(a.shape))
        off += sz
    return out


def _row(v):
    return v.reshape(1, -1)


def _layer_fwd(x, mod, p, l, ride=None, on_receive=None):
    s, d = x.shape
    sw = d // 2
    nh = d // LANES
    shift_m, scale_m, gate_m, shift_f, scale_f, gate_f = mod
    n = lambda tag: f"{tag}{l}"
    sv = {}

    h1, = _rowwise(lambda xv, g, sc, sh: (xv * _rms(xv) * g) * (1.0 + sc) + sh,
                   [x], [p['g_pre_mix'], scale_m, shift_m], [(d, BF, 'tile')], name=n("pre_mix"))
    proj_a = _mm(h1, p['w_in_a'], name=n("proj_a"))
    flog = _mm(h1, p['w_in_f'], name=n("proj_f"))
    gates = _mm(h1, p['w_in_g'], name=n("proj_g"))

    y_s5 = _s5_fwd(proj_a, p['wb_re'], p['wb_im'], p['wc_re'], p['wc_im'], p['lamb_re'], p['lamb_im'], p['d_skip'],
                   name=n("s5_fwd"))
    z, = _rowwise(_gelu, [y_s5], [], [(sw, BF, 'tile')], name=n("gelu"))
    tglu = _mm(z, p['w_glu'], name=n("glu_mm"))
    ys, = _rowwise(lambda yv, tv, b: _gelu(yv) * _sigmoid(tv + b), [y_s5, tglu], [p['b_glu']], [(sw, BF, 'tile')],
                   name=n("glu"))

    cumx = _cum_fwd(flog, p['b_f_row'], nh, name=n("cum_fwd"))
    ya, lse, *received = _attn_fwd(proj_a, cumx, sw, name=n("attn_fwd"), ride=ride)
    if on_receive is not None:
        on_receive(received)

    am = _mm(ys, p['w_pa'], name=n("pa_mm"))
    bm = _mm(ya, p['w_pb'], name=n("pb_mm"))
    merged, = _rowwise(lambda a, b, ga, gb: _sigmoid(ga) * a + _sigmoid(gb) * b,
                       [am, bm, (gates, d, 0), (gates, d, 1)], [], [(d, BF, 'tile')], name=n("merge"))
    ym = _mm(merged, p['w_o'], name=n("o_mm"))
    x2, = _rowwise(lambda xv, yv, g, gt: xv + gt * (yv * _rms(yv) * g),
                   [x, ym], [p['g_post_mix'], gate_m], [(d, F32, 'tile')], name=n("post_mix"))

    h2, = _rowwise(lambda xv, g, sc, sh: (xv * _rms(xv) * g) * (1.0 + sc) + sh,
                   [x2], [p['g_pre_ffn'], scale_f, shift_f], [(d, BF, 'tile')], name=n("pre_ffn"))
    gt = _mm(h2, p['w_ffn_gate'], name=n("gate_mm"))
    up = _mm(h2, p['w_ffn_up'], name=n("up_mm"))
    dff = gt.shape[1]
    act, = _rowwise(lambda g, u: _silu(g) * u, [gt, up], [], [(dff, BF, 'tile')], name=n("swiglu"))
    yf = _mm(act, p['w_ffn_down'], name=n("down_mm"))
    x3, = _rowwise(lambda xv, yv, g, gt_: xv + gt_ * (yv * _rms(yv) * g),
                   [x2, yf], [p['g_post_ffn'], gate_f], [(d, F32, 'tile')], name=n("post_ffn"))

    sv.update(x=x, h1=h1, proj_a=proj_a, flog=flog, gates=gates, y_s5=y_s5, z=z, tglu=tglu, ys=ys, cumx=cumx,
              ya=ya, lse=lse, am=am, bm=bm, merged=merged, ym=ym, x2=x2, h2=h2, gt=gt, up=up,
              act=act, yf=yf)
    return x3, sv


def _layer_bwd(dx3, sv, mod, p, l, make_ride=None, on_receive=None):
    x, x2 = sv['x'], sv['x2']
    s, d = x.shape
    sw = d // 2
    nh = d // LANES
    shift_m, scale_m, gate_m, shift_f, scale_f, gate_f = mod
    n = lambda tag: f"{tag}{l}"
    gw, gs = {}, {}

    def post_bwd(dxo, yv, g, gate):
        r = _rms(yv)
        nf = yv * r
        dn = dxo * gate * g
        return _norm_bwd(dn, nf, r), dxo * (nf * g), dxo * gate * nf

    def pre_bwd(dh, dres, xv, g, sc):
        r = _rms(xv)
        xh = xv * r
        n3 = xh * g
        dn3 = dh * (1.0 + sc)
        return dres + _norm_bwd(dn3 * g, xh, r), dh, dh * n3, dn3 * xh

    dyf, dgate_f, gs['g_post_ffn'] = _rowwise(
        post_bwd, [dx3, sv['yf']], [p['g_post_ffn'], gate_f],
        [(d, BF, 'tile'), (d, F32, 'sum'), (d, F32, 'sum')], name=n("post_ffn_bwd"))
    dff = sv['gt'].shape[1]
    dact = _mm(dyf, p['w_ffn_down'], tb=True, name=n("down_bwd_x"))
    gw['w_ffn_down'] = _mm(sv['act'], dyf, ta=True, out_dtype=BF, name=n("down_bwd_w"))

    def swiglu_bwd(da, g, u):
        sg = _sigmoid(g)
        return da * u * (sg * (1.0 + g * (1.0 - sg))), da * (g * sg)

    dgt, dup = _rowwise(swiglu_bwd, [dact, sv['gt'], sv['up']], [], [(dff, BF, 'tile'), (dff, BF, 'tile')],
                        name=n("swiglu_bwd"))
    dh2a = _mm(dgt, p['w_ffn_gate'], tb=True, name=n("gate_bwd_x"))
    dh2b = _mm(dup, p['w_ffn_up'], tb=True, name=n("up_bwd_x"))
    gw['w_ffn_gate'] = _mm(sv['h2'], dgt, ta=True, out_dtype=BF, name=n("gate_bwd_w"))
    gw['w_ffn_up'] = _mm(sv['h2'], dup, ta=True, out_dtype=BF, name=n("up_bwd_w"))
    dx2, dshift_f, dscale_f, gs['g_pre_ffn'] = _rowwise(
        lambda da, db, dres, xv, g, sc: pre_bwd(da + db, dres, xv, g, sc),
        [dh2a, dh2b, dx3, x2], [p['g_pre_ffn'], scale_f],
        [(d, F32, 'tile'), (d, F32, 'sum'), (d, F32, 'sum'), (d, F32, 'sum')], name=n("pre_ffn_bwd"))

    dym, dgate_m, gs['g_post_mix'] = _rowwise(
        post_bwd, [dx2, sv['ym']], [p['g_post_mix'], gate_m],
        [(d, BF, 'tile'), (d, F32, 'sum'), (d, F32, 'sum')], name=n("post_mix_bwd"))
    dmerged = _mm(dym, p['w_o'], tb=True, name=n("o_bwd_x"))
    gw['w_o'] = _mm(sv['merged'], dym, ta=True, out_dtype=BF, name=n("o_bwd_w"))

    def merge_bwd(dm, a, b, ga, gb):
        sa, sb = _sigmoid(ga), _sigmoid(gb)
        return dm * sa, dm * sb, dm * a * sa * (1.0 - sa), dm * b * sb * (1.0 - sb)

    da_, db_, dga, dgb = _rowwise(
        merge_bwd, [dmerged, sv['am'], sv['bm'], (sv['gates'], d, 0), (sv['gates'], d, 1)], [],
        [(d, BF, 'tile')] * 4, name=n("merge_bwd"))
    dys = _mm(da_, p['w_pa'], tb=True, name=n("pa_bwd_x"))
    gw['w_pa'] = _mm(sv['ys'], da_, ta=True, out_dtype=BF, name=n("pa_bwd_w"))
    dya = _mm(db_, p['w_pb'], tb=True, name=n("pb_bwd_x"))
    gw['w_pb'] = _mm(sv['ya'], db_, ta=True, out_dtype=BF, name=n("pb_bwd_w"))

    dq, dk, dv, dkc, dqc, *received = _attn_bwd(
        sv['proj_a'], dya, sv['ya'], sv['lse'], sv['cumx'], sw, name=n("attn_bwd"),
        ride=make_ride(gw) if make_ride is not None else None)
    if on_receive is not None:
        on_receive(received)
    frow = sv['flog'][:, :nh].T
    dcum = jnp.stack([-dkc[:, BIAS_LANES::HEAD_DIM].T, dqc[:, ::HEAD_DIM].T])
    dfrow, dbf = _cum_bwd(dcum, frow, p['b_f_col'], name=n("cum_bwd"))
    gs['b_f'] = dbf.reshape(nh)
    dflog = jnp.pad(dfrow.T, ((0, 0), (0, LANES - nh))).astype(BF)

    def glu_bwd(dy_, yv, tv, b):
        zv = _gelu(yv)
        sg = _sigmoid(tv + b)
        dt = dy_ * zv * sg * (1.0 - sg)
        return dt, dy_ * sg, dt

    dt, dz1, gs['b_glu'] = _rowwise(glu_bwd, [dys, sv['y_s5'], sv['tglu']], [p['b_glu']],
                                    [(sw, BF, 'tile'), (sw, F32, 'tile'), (sw, F32, 'sum')], name=n("glu_bwd"))
    dz2 = _mm(dt, p['w_glu'], tb=True, name=n("glu_bwd_x"))
    gw['w_glu'] = _mm(sv['z'], dt, ta=True, out_dtype=BF, name=n("glu_bwd_w"))
    dy_s5, = _rowwise(lambda a, b, yv: (a + b) * _gelu_grad(yv), [dz1, dz2, sv['y_s5']], [], [(sw, F32, 'tile')],
                      name=n("gelu_bwd"))
    du, dwbr, dwbi, dwcr, dwci, dlr, dli, gs['d_skip'] = _s5_bwd(
        sv['proj_a'], dy_s5, p['wb_re'], p['wb_im'], p['wc_re'], p['wc_im'], p['lamb_re'], p['lamb_im'], p['d_skip'],
        name=n("s5_bwd"))
    g_ = sw // SSM_H
    pst = p['lamb_re'].shape[1] // g_
    gs['lamb_re'], gs['lamb_im'] = dlr.reshape(g_, pst), dli.reshape(g_, pst)
    gs['bbar_re'] = _s5_block_grads(dwbr, SSM_H, pst, False).transpose(0, 2, 1)
    gs['bbar_im'] = _s5_block_grads(dwbi, SSM_H, pst, False).transpose(0, 2, 1)
    gs['c_re'] = _s5_block_grads(dwcr, pst, SSM_H, True).transpose(0, 2, 1)
    gs['c_im'] = _s5_block_grads(dwci, pst, SSM_H, True).transpose(0, 2, 1)

    dproj = jnp.concatenate([du.astype(BF), dq.astype(BF), dk.astype(BF), dv.astype(BF), dflog, dga, dgb], axis=1)
    dh1 = _mm(dproj, p['w_in_all'], tb=True, tk=1408, name=n("proj_bwd_x"))
    gw['w_in'] = _mm(sv['h1'], dproj, ta=True, out_dtype=BF, name=n("proj_bwd_w"))
    dx, dshift_m, dscale_m, gs['g_pre_mix'] = _rowwise(
        pre_bwd, [dh1, dx2, x], [p['g_pre_mix'], scale_m],
        [(d, F32, 'tile'), (d, F32, 'sum'), (d, F32, 'sum'), (d, F32, 'sum')], name=n("pre_mix_bwd"))
    dmod = [dshift_m, dscale_m, dgate_m, dshift_f, dscale_f, dgate_f]
    return dx, gw, dmod, gs


def _unshard(k, blocks):
    if k in COL_SHARDED:
        return blocks.transpose(1, 0, 2).reshape(blocks.shape[1], NDEV * blocks.shape[2])
    return blocks.reshape(NDEV * blocks.shape[1], blocks.shape[2])


def _to_slabs(k, g):
    if k == 'w_in':
        d = g.shape[0]
        nh = d // LANES
        g = jnp.concatenate([g[:, :2 * d + nh], g[:, 2 * d + LANES:]], axis=1)
    if k in COL_SHARDED:
        return g.reshape(g.shape[0], NDEV, g.shape[1] // NDEV).transpose(1, 0, 2)
    return g.reshape(NDEV, g.shape[0] // NDEV, g.shape[1])


def _prep_w_in(w_in):
    d = w_in.shape[0]
    nh = d // LANES
    fcol = 2 * d
    p = {}
    p['w_in_a'] = w_in[:, :fcol]
    p['w_in_f'] = jnp.pad(w_in[:, fcol:fcol + nh], ((0, 0), (0, LANES - nh)))
    p['w_in_g'] = w_in[:, fcol + nh:]
    p['w_in_all'] = jnp.concatenate([p['w_in_a'], p['w_in_f'], p['w_in_g']], axis=1)
    return p


def _prep_small(small):
    nh = small['b_f'].shape[0]
    p = {}
    for k in ('g_pre_mix', 'g_post_mix', 'g_pre_ffn', 'g_post_ffn', 'd_skip', 'b_glu'):
        p[k] = _row(small[k])
    p['b_f_row'] = jnp.pad(_row(small['b_f']), ((0, 0), (0, LANES - nh)))
    p['b_f_col'] = small['b_f'].reshape(nh, 1)
    lbr, lbi, bbr, bbi = _s5_discretise(small['lam_re'], small['lam_im'], small['log_dt'], small['b_re'], small['b_im'])
    p['lamb_re'], p['lamb_im'] = _row(lbr), _row(lbi)
    p['wb_re'] = _block_diag(bbr.transpose(0, 2, 1)).astype(BF)
    p['wb_im'] = _block_diag(bbi.transpose(0, 2, 1)).astype(BF)
    p['wc_re'] = _block_diag(small['c_re'].transpose(0, 2, 1)).astype(BF)
    p['wc_im'] = _block_diag(small['c_im'].transpose(0, 2, 1)).astype(BF)
    return p


def _local_step(x, target, mods, ps, small, hooks=None):
    depth = len(ps)
    s, d = x.shape
    hooks = hooks or {}
    saved = []
    h = x
    for l in range(depth):
        h, sv = _layer_fwd(h, mods[l], ps[l], l, ride=hooks['fwd_ride'](l) if hooks else None,
                           on_receive=functools.partial(hooks['fwd_recv'], l) if hooks else None)
        saved.append(sv)

    def loss_fn(yv, tv):
        e = yv - tv
        return e * (1.0 / d), jnp.sum(e * e, axis=1, keepdims=True) * (0.5 / d)

    dy, loss = _rowwise(loss_fn, [h, target], [], [(d, F32, 'tile'), (1, F32, 'sum')], name="loss")
    dmods, gss = [None] * depth, [None] * depth
    unsent, in_flight = {}, []
    for l in range(depth - 1, -1, -1):
        def make_ride(gw, l=l):
            unsent.update({(k, l): g for k, g in gw.items()})
            in_flight[:] = list(unsent)
            ride = hooks['bwd_ride'](dict(unsent))
            unsent.clear()
            return ride

        def on_receive(results):
            hooks['bwd_recv'](list(in_flight), results)

        dy, gw, dmods[l], gs = _layer_bwd(dy, saved[l], mods[l], ps[l], l, make_ride=make_ride if hooks else None,
                                          on_receive=on_receive if hooks else None)
        unsent.update({(k, l): g for k, g in gw.items() if not hooks or (k, l) not in in_flight})
        sm = small[l]
        _, vjp = jax.vjp(_s5_discretise, sm['lam_re'], sm['lam_im'], sm['log_dt'], sm['b_re'], sm['b_im'])
        gs['lam_re'], gs['lam_im'], gs['log_dt'], gs['b_re'], gs['b_im'] = vjp(
            (gs.pop('lamb_re'), gs.pop('lamb_im'), gs.pop('bbar_re'), gs.pop('bbar_im')))
        gss[l] = gs
    return loss, dy, unsent, dmods, gss


SMALL_LOCAL = ['g_pre_mix', 'g_post_mix', 'g_pre_ffn', 'g_post_ffn', 'lam_re', 'lam_im', 'log_dt', 'b_re', 'b_im',
               'c_re', 'c_im', 'd_skip', 'b_glu', 'b_f']


def kernel(x, c, w_ada, b_ada, g_pre_mix, g_post_mix, g_pre_ffn, g_post_ffn, w_in, lam_re, lam_im, log_dt, b_re, b_im, c_re, c_im, d_skip, w_glu, b_glu, b_f, w_pa, w_pb, w_o, w_ffn_gate, w_ffn_up, w_ffn_down, loss_target, m_w_ada, m_b_ada, m_g_pre_mix, m_g_post_mix, m_g_pre_ffn, m_g_post_ffn, m_w_in, m_lam_re, m_lam_im, m_log_dt, m_b_re, m_b_im, m_c_re, m_c_im, m_d_skip, m_w_glu, m_b_glu, m_b_f, m_w_pa, m_w_pb, m_w_o, m_w_ffn_gate, m_w_ffn_up, m_w_ffn_down, v_w_ada, v_b_ada, v_g_pre_mix, v_g_post_mix, v_g_pre_ffn, v_g_post_ffn, v_w_in, v_lam_re, v_lam_im, v_log_dt, v_b_re, v_b_im, v_c_re, v_c_im, v_d_skip, v_w_glu, v_b_glu, v_b_f, v_w_pa, v_w_pb, v_w_o, v_w_ffn_gate, v_w_ffn_up, v_w_ffn_down):
    args = dict(locals())
    W = {k: args[k] for k in WEIGHTS}
    M = {k: args['m_' + k] for k in WEIGHTS}
    V = {k: args['v_' + k] for k in WEIGHTS}
    depth, d, ncol = w_ada.shape
    s = x.shape[1]
    me = 4 * lax.axis_index("x") + 2 * lax.axis_index("y") + lax.axis_index("c")

    c_all, = _exchange([jnp.pad(c, ((0, SUBLANES - 1), (0, 0)))], True, name="gather_c")
    c_all = c_all[:, 0, :]
    cond, = _rowwise(_silu, [c_all], [], [(d, F32, 'tile')], name="cond")
    mod_part = jnp.stack([_mm(cond, w_ada[l], name=f"ada_mm{l}") for l in range(depth)], axis=1)
    mod_recv, = _exchange([mod_part.reshape(NDEV, depth, 1, ncol)], False, name="scatter_mod")
    mod_cat = mod_recv.reshape(NDEV, depth, ncol).transpose(1, 0, 2).reshape(depth, NDEV * ncol)
    mod, = _rowwise(lambda a, b: a + b, [mod_cat, b_ada], [], [(NDEV * ncol, F32, 'tile')], name="mod_bias")
    mods = [[mod[l:l + 1, i * d:(i + 1) * d] for i in range(6)] for l in range(depth)]

    small = [{k: W[k][l] for k in SMALL_LOCAL} for l in range(depth)]
    ps = [_prep_small(small[l]) for l in range(depth)]
    first = ['w_in', 'w_glu']
    later = [(k, l) for l in range(depth) for k in BIG if not (l == 0 and k in first)]

    def take_weights(keys, results):
        for (k, l), blocks in zip(keys, results):
            full = _unshard(k, blocks)
            ps[l].update(_prep_w_in(full) if k == 'w_in' else {k: full})

    take_weights([(k, 0) for k in first],
                 _exchange([W[k][0].astype(BF) for k in first], True, name="gather_w_first"))

    grad_parts = {}

    def fwd_ride(l):
        if l != 0:
            return None
        blocks = [W[k][ll].astype(BF) for k, ll in later]
        return _Exchange(blocks, True), blocks

    def bwd_ride(grads):
        slabs = [_to_slabs(k, g) for (k, _), g in grads.items()]
        return _Exchange(slabs, False), slabs

    hooks = dict(fwd_ride=fwd_ride, fwd_recv=lambda l, results: take_weights(later, results) if l == 0 else None,
                 bwd_ride=bwd_ride, bwd_recv=lambda keys, results: grad_parts.update(zip(keys, results)))

    loss, dx, unsent, dmods, gss = _local_step(x[0], loss_target[0], mods, ps, small, hooks)
    loss = lax.psum(loss[0, 0], ("x", "y", "c"))
    grad_parts.update(zip(unsent, _exchange([_to_slabs(k, g) for (k, _), g in unsent.items()], False,
                                            name="scatter_grads_last")))
    out = {}
    for k in BIG:
        out[k] = _adamw([grad_parts[(k, l)] for l in range(depth)], W[k], M[k], V[k], name=f"adamw_{k}")

    dmod_mine = jnp.stack([jnp.concatenate(dmods[l], axis=1)[0] for l in range(depth)])
    small_mine = [dmod_mine] + [jnp.stack([gss[l][k] for l in range(depth)]) for k in SMALL_LOCAL]
    parts, = _exchange([_pack(small_mine)], True, name="gather_small")
    summed = _sum_parts(parts, name="sum_small")
    names = ['b_ada'] + SMALL_LOCAL
    grads = _unpack(summed, [W[k] for k in names])
    res = _adamw([_pack(grads)[None]], _pack([W[k] for k in names])[None], _pack([M[k] for k in names])[None],
                 _pack([V[k] for k in names])[None], name="adamw_small")
    unpacked = [_unpack(a, [W[k] for k in names]) for a in res]
    for i, k in enumerate(names):
        out[k] = [unpacked[j][i] for j in range(4)]

    dmod_all = parts.reshape(NDEV, -1)[:, :depth * 6 * d].reshape(NDEV, depth, 6 * d)
    dmod_cols = lax.dynamic_slice_in_dim(dmod_all, me * ncol, ncol, axis=2)
    g_ada = [_mm(cond, dmod_cols[:, l], ta=True, precision=HI, name=f"ada_bwd{l}")[None] for l in range(depth)]
    out['w_ada'] = _adamw(g_ada, w_ada, m_w_ada, v_w_ada, name="adamw_w_ada")

    return (loss, dx[None], *[out[k][0] for k in WEIGHTS], *[out[k][1] for k in WEIGHTS],
            *[out[k][2] for k in WEIGHTS], *[out[k][3] for k in WEIGHTS])
```

```python
import functools
import math

import jax
import jax.numpy as jnp
from jax import lax
from jax.experimental import pallas as pl
from jax.experimental.pallas import tpu as pltpu

F32 = jnp.float32
BF = jnp.bfloat16
NDEV = 8
LANES = 128
SUBLANES = 8
VMEM_LIMIT = 48 * 1024 * 1024

SSM_H = 16
HEAD_DIM = 64
RMS_EPS = 1e-6
EIG_CLIP = 1e-4
ADAM_LR = 0.001
ADAM_B1 = 0.9
ADAM_B2 = 0.999
ADAM_EPS = 1e-08
ADAM_WD = 0.01
ADAM_STEP = 10
NEG = -1e30
HI = lax.Precision.HIGHEST

WEIGHTS = ['w_ada', 'b_ada', 'g_pre_mix', 'g_post_mix', 'g_pre_ffn', 'g_post_ffn', 'w_in', 'lam_re', 'lam_im',
           'log_dt', 'b_re', 'b_im', 'c_re', 'c_im', 'd_skip', 'w_glu', 'b_glu', 'b_f', 'w_pa', 'w_pb', 'w_o',
           'w_ffn_gate', 'w_ffn_up', 'w_ffn_down']
COL_SHARDED = ['w_in', 'w_pa', 'w_pb', 'w_ffn_gate', 'w_ffn_up']
ROW_SHARDED = ['w_glu', 'w_o', 'w_ffn_down']
BIG = COL_SHARDED + ROW_SHARDED
SMALL = ['b_ada', 'g_pre_mix', 'g_post_mix', 'g_pre_ffn', 'g_post_ffn', 'lam_re', 'lam_im', 'log_dt', 'b_re',
         'b_im', 'c_re', 'c_im', 'd_skip', 'b_glu', 'b_f']


def _fit(dim, target, align):
    if dim <= target:
        return dim
    t = (target // align) * align
    while t >= align:
        if dim % t == 0:
            return t
        t -= align
    return dim


def _params(**kw):
    return pltpu.CompilerParams(vmem_limit_bytes=VMEM_LIMIT, **kw)


def _mm(a, b, *, ta=False, tb=False, out_dtype=F32, tm=512, tn=512, tk=2048, precision=None, name, ride=None):
    m, k = (a.shape[1], a.shape[0]) if ta else a.shape
    n = b.shape[0] if tb else b.shape[1]
    assert (b.shape[1] if tb else b.shape[0]) == k
    tm = _fit(m, tm, LANES if ta else 16)
    tn = _fit(n, tn, LANES)
    tk = _fit(k, tk, LANES)
    nk = k // tk
    grid = (m // tm, n // tn, nk)
    dims = (((0 if ta else 1,), (1 if tb else 0,)), ((), ()))
    ex, ex_arrays = ride if ride is not None else (None, [])

    def kern(*refs):
        (a_ref, b_ref), (o_ref,), comm, scratch = _ride_split(ex, refs, 2, 1)
        step = (pl.program_id(0) * grid[1] + pl.program_id(1)) * grid[2] + pl.program_id(2)
        if ex is not None:
            @pl.when(step == 0)
            def _():
                ex.start(*comm)

            @pl.when(step == (grid[0] * grid[1] * grid[2]) // 2)
            def _():
                ex.forward(*comm)

        av, bv = a_ref[...], b_ref[...]
        if precision is None:
            av, bv = av.astype(BF), bv.astype(BF)
        p = lax.dot_general(av, bv, dims, preferred_element_type=F32, precision=precision)
        if nk == 1:
            o_ref[...] = p.astype(out_dtype)
        else:
            acc_ref, = scratch
            kk = pl.program_id(2)

            @pl.when(kk == 0)
            def _():
                acc_ref[...] = p

            @pl.when(kk > 0)
            def _():
                acc_ref[...] += p

            @pl.when(kk == nk - 1)
            def _():
                o_ref[...] = acc_ref[...].astype(out_dtype)

        if ex is not None:
            @pl.when(step == grid[0] * grid[1] * grid[2] - 1)
            def _():
                ex.wait(*comm)

    a_spec = pl.BlockSpec((tk, tm), lambda i, j, kk: (kk, i)) if ta else pl.BlockSpec((tm, tk), lambda i, j, kk: (i, kk))
    b_spec = pl.BlockSpec((tn, tk), lambda i, j, kk: (j, kk)) if tb else pl.BlockSpec((tk, tn), lambda i, j, kk: (kk, j))
    res = pl.pallas_call(
        kern, name=name,
        out_shape=[jax.ShapeDtypeStruct((m, n), out_dtype)] + (ex.out_shape if ex else []),
        grid=grid,
        in_specs=[a_spec, b_spec] + (ex.specs if ex else []),
        out_specs=[pl.BlockSpec((tm, tn), lambda i, j, kk: (i, j))] + (ex.specs if ex else []),
        scratch_shapes=(ex.scratch if ex else []) + ([] if nk == 1 else [pltpu.VMEM((tm, tn), F32)]),
        compiler_params=_params(dimension_semantics=("arbitrary",) * 3 if ex else ("parallel", "parallel", "arbitrary"),
                                has_side_effects=ex is not None),
    )(a, b, *ex_arrays)
    return (res[0], res[1:]) if ex else res[0]


def _rowwise(fn, tiles, params, outs, *, tr=256, name):
    tiles = [t if isinstance(t, tuple) else (t, t.shape[1], 0) for t in tiles]
    s = tiles[0][0].shape[0]
    tr = _fit(s, tr, 16)
    nt, npar = len(tiles), len(params)

    def kern(*refs):
        i = pl.program_id(0)
        res = fn(*[r[...] for r in refs[:nt + npar]])
        if not isinstance(res, (tuple, list)):
            res = (res,)
        for (w, dt, kind), o_ref, r in zip(outs, refs[nt + npar:], res):
            if kind == 'tile':
                o_ref[...] = r.astype(dt)
            else:
                part = jnp.sum(r.astype(F32), axis=0, keepdims=True)

                @pl.when(i == 0)
                def _(o_ref=o_ref, part=part):
                    o_ref[...] = part

                @pl.when(i > 0)
                def _(o_ref=o_ref, part=part):
                    o_ref[...] += part

    def tile_spec(w, cb):
        return pl.BlockSpec((tr, w), lambda i: (i, cb))

    in_specs = [tile_spec(w, cb) for _, w, cb in tiles]
    in_specs += [pl.BlockSpec(p.shape, lambda i, nd=p.ndim: (0,) * nd) for p in params]
    out_shape, out_specs = [], []
    for w, dt, kind in outs:
        if kind == 'tile':
            out_shape.append(jax.ShapeDtypeStruct((s, w), dt))
            out_specs.append(pl.BlockSpec((tr, w), lambda i: (i, 0)))
        else:
            out_shape.append(jax.ShapeDtypeStruct((1, w), F32))
            out_specs.append(pl.BlockSpec((1, w), lambda i: (0, 0)))
    res = pl.pallas_call(
        kern, name=name, out_shape=out_shape, grid=(s // tr,), in_specs=in_specs, out_specs=out_specs,
        compiler_params=_params(dimension_semantics=("arbitrary",)),
    )(*[t[0] for t in tiles], *params)
    return res


def _sigmoid(z):
    return 1.0 / (1.0 + jnp.exp(-z))


def _silu(z):
    return z * _sigmoid(z)


_GELU_K = math.sqrt(2.0 / math.pi)


def _gelu(y):
    return 0.5 * y * (1.0 + jnp.tanh(_GELU_K * (y + 0.044715 * y * y * y)))


def _gelu_grad(y):
    th = jnp.tanh(_GELU_K * (y + 0.044715 * y * y * y))
    return 0.5 * (1.0 + th) + 0.5 * y * (1.0 - th * th) * _GELU_K * (1.0 + 3.0 * 0.044715 * y * y)


def _rms(x):
    return lax.rsqrt(jnp.mean(x * x, axis=-1, keepdims=True) + RMS_EPS)


def _norm_bwd(dn, xhat, r):
    return r * (dn - xhat * jnp.mean(dn * xhat, axis=-1, keepdims=True))


def _cum_fwd(flog, bf_row, nh, *, name):
    s = flog.shape[0]
    w = nh * HEAD_DIM
    t = _fit(s, 256, SUBLANES)

    def kern(f_ref, b_ref, o_ref, carry_ref):
        i = pl.program_id(0)

        @pl.when(i == 0)
        def _():
            carry_ref[...] = jnp.zeros_like(carry_ref)

        z = f_ref[...] + b_ref[...]
        logf = jnp.minimum(z, 0.0) - jnp.log(1.0 + jnp.exp(-jnp.abs(z)))
        hh = lax.broadcasted_iota(jnp.int32, (LANES, w), 0)
        cc = lax.broadcasted_iota(jnp.int32, (LANES, w), 1)
        expand = (cc // HEAD_DIM == hh).astype(F32)
        lx = jnp.dot(logf, expand, preferred_element_type=F32, precision=HI)
        rr = lax.broadcasted_iota(jnp.int32, (t, t), 0)
        kk = lax.broadcasted_iota(jnp.int32, (t, t), 1)
        tri = (kk <= rr).astype(F32)
        cum = jnp.dot(tri, lx, preferred_element_type=F32, precision=HI) + carry_ref[...]
        o_ref[...] = cum
        carry_ref[...] = cum[t - 1:t, :]

    return pl.pallas_call(
        kern, name=name, out_shape=jax.ShapeDtypeStruct((s, w), F32), grid=(s // t,),
        in_specs=[pl.BlockSpec((t, LANES), lambda i: (i, 0)), pl.BlockSpec((1, LANES), lambda i: (0, 0))],
        out_specs=pl.BlockSpec((t, w), lambda i: (i, 0)),
        scratch_shapes=[pltpu.VMEM((1, w), F32)],
        compiler_params=_params(dimension_semantics=("arbitrary",)),
    )(flog, bf_row)


def _cum_bwd(dcrow, frow, bf_col, *, name):
    _, nh, s = dcrow.shape
    t = _fit(s, 512, LANES)
    nb = s // t

    def kern(d_ref, f_ref, b_ref, df_ref, db_ref):
        rr = lax.broadcasted_iota(jnp.int32, (t, t), 0)
        kk = lax.broadcasted_iota(jnp.int32, (t, t), 1)
        upper = (rr >= kk).astype(F32)
        carry = jnp.zeros((nh, 1), F32)
        db = jnp.zeros((nh, 1), F32)
        for blk in range(nb - 1, -1, -1):
            sl = slice(blk * t, (blk + 1) * t)
            rc = jnp.dot(d_ref[0, :, sl] + d_ref[1, :, sl], upper, preferred_element_type=F32, precision=HI) + carry
            carry = rc[:, 0:1]
            df = rc * _sigmoid(-(f_ref[:, sl] + b_ref[...]))
            df_ref[:, sl] = df
            db = db + jnp.sum(df, axis=1, keepdims=True)
        db_ref[...] = db

    return pl.pallas_call(
        kern, name=name,
        out_shape=[jax.ShapeDtypeStruct((nh, s), F32), jax.ShapeDtypeStruct((nh, 1), F32)],
        compiler_params=_params(),
    )(dcrow, frow, bf_col)


def _ride_split(ex, refs, n_in, n_out):
    n = ex.n if ex is not None else 0
    own_in, srcs = refs[:n_in], refs[n_in:n_in + n]
    own_out, dsts = refs[n_in + n:n_in + n + n_out], refs[n_in + n + n_out:n_in + 2 * n + n_out]
    sems = refs[n_in + 2 * n + n_out:n_in + 2 * n + n_out + 3] if n else ()
    rest = refs[n_in + 2 * n + n_out + (3 if n else 0):]
    return own_in, own_out, (srcs, dsts, sems), rest


ATTN_STRIP = 32
BIAS_LANES = 3


def _head_masks(rows):
    lane = lax.broadcasted_iota(jnp.int32, (rows, LANES), 1)
    return [(lane >= HEAD_DIM * e) & (lane < HEAD_DIM * (e + 1)) for e in range(2)]


def _augment(feat, bias, e, *, bias_slot, ones_slot):
    rows = feat.shape[0]
    lane = lax.broadcasted_iota(jnp.int32, (rows, LANES), 1)
    own = (lane >= HEAD_DIM * e) & (lane < HEAD_DIM * (e + 1))
    off = lane - HEAD_DIM * (1 - e)
    out = jnp.where(own, feat, 0.0)
    if ones_slot is not None:
        out = jnp.where((off >= ones_slot * BIAS_LANES) & (off < (ones_slot + 1) * BIAS_LANES), 1.0, out)
    if bias is not None:
        rest = pltpu.roll(bias, HEAD_DIM, 1)
        for term in range(BIAS_LANES):
            part = rest.astype(BF).astype(F32)
            out = jnp.where(off == bias_slot * BIAS_LANES + term, part, out)
            rest = rest - part
    return out.astype(BF)


def _two_slot_pipeline(m, scores, tile):
    scores(0, 0)

    def pair(n, carry):
        k = 2 * n
        scores(k + 1, 1)
        tile(k, 0, False)
        scores(k + 2, 0)
        tile(k + 1, 1, False)
        return carry

    lax.fori_loop(0, m // 2, pair, 0)

    @pl.when(m % 2 == 0)
    def _():
        tile(m, 0, True)

    @pl.when(m % 2 == 1)
    def _():
        scores(m, 1)
        tile(m - 1, 0, False)
        tile(m, 1, True)


def _attn_fwd(proj, cumx, qcol, *, name, ride=None):
    s = proj.shape[0]
    w = cumx.shape[1]
    nhp = w // LANES
    t = _fit(s, 256, LANES)
    nq = s // t
    strip = _fit(t, ATTN_STRIP, 16)
    scale = HEAD_DIM ** -0.5
    qb, kb, vb = qcol // LANES, (qcol + w) // LANES, (qcol + 2 * w) // LANES
    ex, ex_arrays = ride if ride is not None else (None, [])
    nt_dims = (((1,), (1,)), ((), ()))

    def kern(*refs):
        own_in, (o_ref, l_ref), comm, scratch = _ride_split(ex, refs, 5, 2)
        q_ref, k_ref, v_ref, cxq_ref, cxk_ref = own_in
        ka_ref, vat_ref, s0_ref, s1_ref, p_ref, m_ref, acc_ref = scratch
        s_refs = (s0_ref, s1_ref)
        i = pl.program_id(1)
        if ex is not None:
            @pl.when((pl.program_id(0) == 0) & (i == 0))
            def _():
                ex.start(*comm)

            @pl.when((pl.program_id(0) == nhp - 1) & (i == 0))
            def _():
                ex.forward(*comm)

        msks = _head_masks(t)

        @pl.when(i == 0)
        def _():
            def build(c, carry):
                rows = pl.ds(pl.multiple_of(c * t, LANES), t)
                k2, v2, cx = k_ref[rows, :], v_ref[rows, :], cxk_ref[rows, :]
                for e in range(2):
                    ka_ref[e, rows, :] = _augment(k2, -cx, e, bias_slot=1, ones_slot=0)
                    vat_ref[e, :, rows] = jnp.where(msks[e], v2, 1.0).T.astype(BF)
                return carry
            lax.fori_loop(0, nq, build, 0)

        q2 = q_ref[...] * scale
        qa = [_augment(q2, cxq_ref[...], e, bias_slot=0, ones_slot=1) for e in range(2)]
        m_ref[...] = jnp.full(m_ref.shape, NEG, F32)
        acc_ref[...] = jnp.zeros(acc_ref.shape, F32)
        slabs = strip // SUBLANES

        def scores(j, slot):
            rows_k = pl.ds(pl.multiple_of(j * t, LANES), t)
            for e in range(2):
                st = lax.dot_general(ka_ref[e, rows_k, :], qa[e], nt_dims, preferred_element_type=F32)
                s_refs[slot][e] = st.reshape(t // SUBLANES, SUBLANES, t)

        def tile(j, slot, diagonal):
            rows_k = pl.ds(pl.multiple_of(j * t, LANES), t)
            s_ref = s_refs[slot]
            for e in range(2):
                mx = jnp.full((SUBLANES, t), NEG, F32)
                for r in range(t // strip):
                    sl = slice(r * slabs, (r + 1) * slabs)
                    sv = s_ref[e,sl]
                    if diagonal:
                        shape = (slabs, SUBLANES, t)
                        key = (r * strip + lax.broadcasted_iota(jnp.int32, shape, 0) * SUBLANES
                               + lax.broadcasted_iota(jnp.int32, shape, 1))
                        sv = jnp.where(key <= lax.broadcasted_iota(jnp.int32, shape, 2), sv, NEG)
                        s_ref[e,sl] = sv
                    mx = jnp.maximum(mx, jnp.max(sv, axis=0))
                for sh in (4, 2, 1):
                    mx = jnp.maximum(mx, pltpu.roll(mx, sh, 0))
                m_old = m_ref[e]
                m_new = jnp.maximum(m_old, mx)
                alpha = jnp.exp(m_old - m_new)
                m_ref[e] = m_new
                for r in range(t // strip):
                    p = jnp.exp(s_ref[e,r * slabs:(r + 1) * slabs] - m_new[None])
                    p_ref[e, r * strip:(r + 1) * strip, :] = p.reshape(strip, t).astype(BF)
                acc = acc_ref[e].reshape(LANES // SUBLANES, SUBLANES, t) * alpha[None]
                acc_ref[e] = acc.reshape(LANES, t) + jnp.dot(vat_ref[e, :, rows_k], p_ref[e],
                                                             preferred_element_type=F32)

        _two_slot_pipeline(i, scores, tile)

        outs, lses = [], []
        for e in range(2):
            acc = acc_ref[e]
            other = HEAD_DIM * (1 - e)
            den = acc[other:other + 1, :]
            outs.append(acc / den)
            lses.append(jnp.broadcast_to(m_ref[e][0:1, :] + jnp.log(den), (LANES, t)))
        upper = lax.broadcasted_iota(jnp.int32, (LANES, t), 0) < HEAD_DIM
        o_ref[...] = jnp.where(upper, outs[0], outs[1]).T.astype(BF)
        l_ref[...] = jnp.where(upper, lses[0], lses[1]).T
        if ex is not None:
            @pl.when((pl.program_id(0) == nhp - 1) & (i == nq - 1))
            def _():
                ex.wait(*comm)

    own_scratch = [pltpu.VMEM((2, s, LANES), BF), pltpu.VMEM((2, LANES, s), BF),
                   pltpu.VMEM((2, t // SUBLANES, SUBLANES, t), F32),
                   pltpu.VMEM((2, t // SUBLANES, SUBLANES, t), F32), pltpu.VMEM((2, t, t), BF),
                   pltpu.VMEM((2, SUBLANES, t), F32), pltpu.VMEM((2, LANES, t), F32)]
    return pl.pallas_call(
        kern, name=name,
        out_shape=[jax.ShapeDtypeStruct((s, w), BF), jax.ShapeDtypeStruct((nhp, s, LANES), F32)]
        + (ex.out_shape if ex else []),
        grid=(nhp, nq),
        in_specs=[pl.BlockSpec((t, LANES), lambda h, i: (i, qb + h)),
                  pl.BlockSpec((s, LANES), lambda h, i: (0, kb + h)),
                  pl.BlockSpec((s, LANES), lambda h, i: (0, vb + h)),
                  pl.BlockSpec((t, LANES), lambda h, i: (i, h)),
                  pl.BlockSpec((s, LANES), lambda h, i: (0, h))] + (ex.specs if ex else []),
        out_specs=[pl.BlockSpec((t, LANES), lambda h, i: (i, h)),
                   pl.BlockSpec((None, t, LANES), lambda h, i: (h, i, 0))] + (ex.specs if ex else []),
        scratch_shapes=(ex.scratch if ex else []) + own_scratch,
        compiler_params=_params(dimension_semantics=("arbitrary", "arbitrary"),
                                has_side_effects=ex is not None),
    )(proj, proj, proj, cumx, cumx, *ex_arrays)


def _attn_bwd(proj, do, o, lse, cumx, qcol, *, name, ride=None):
    s = proj.shape[0]
    w = cumx.shape[1]
    nhp = w // LANES
    t = _fit(s, 256, LANES)
    nq = s // t
    strip = _fit(t, ATTN_STRIP, 16)
    scale = HEAD_DIM ** -0.5
    qb, kb, vb = qcol // LANES, (qcol + w) // LANES, (qcol + 2 * w) // LANES
    tn_dims = (((0,), (0,)), ((), ()))
    nt_dims = (((1,), (1,)), ((), ()))
    ex, ex_arrays = ride if ride is not None else (None, [])

    def kern(*refs):
        own_in, own_out, comm, scratch = _ride_split(ex, refs, 7, 5)
        q_ref, k_ref, v_ref, do_ref, o_ref, l_ref, cx_ref = own_in
        dq_ref, dk_ref, dv_ref, dkc_ref, dqc_ref = own_out
        qa_ref, da_ref, dqa_ref, dka_ref, dva_ref, st0_ref, st1_ref, dpt0_ref, dpt1_ref, pt_ref, dst_ref = scratch
        st_refs, dpt_refs = (st0_ref, st1_ref), (dpt0_ref, dpt1_ref)
        j = pl.program_id(1)
        if ex is not None:
            @pl.when((pl.program_id(0) == 0) & (j == 0))
            def _():
                ex.start(*comm)

        msks = _head_masks(t)

        @pl.when(j == 0)
        def _():
            def build(c, carry):
                rows = pl.ds(pl.multiple_of(c * t, LANES), t)
                q2 = q_ref[rows, :] * scale
                do2 = do_ref[rows, :]
                dd = do2 * o_ref[rows, :].astype(F32)
                delta = jnp.where(msks[0], jnp.sum(jnp.where(msks[0], dd, 0.0), axis=1, keepdims=True),
                                  jnp.sum(jnp.where(msks[1], dd, 0.0), axis=1, keepdims=True))
                bias = cx_ref[rows, :] - l_ref[rows, :]
                for e in range(2):
                    qa_ref[e, rows, :] = _augment(q2, bias, e, bias_slot=0, ones_slot=1)
                    da_ref[e, rows, :] = _augment(do2, -delta, e, bias_slot=0, ones_slot=None)
                return carry
            lax.fori_loop(0, nq, build, 0)
            dqa_ref[...] = jnp.zeros(dqa_ref.shape, F32)

        rows_k = pl.ds(pl.multiple_of(j * t, LANES), t)
        k2, v2 = k_ref[...], v_ref[...]
        ka = [_augment(k2, -cx_ref[rows_k, :], e, bias_slot=1, ones_slot=0) for e in range(2)]
        va = [_augment(v2, None, e, bias_slot=None, ones_slot=0) for e in range(2)]
        dka_ref[...] = jnp.zeros(dka_ref.shape, F32)
        dva_ref[...] = jnp.zeros(dva_ref.shape, F32)

        def scores(k, slot):
            rows_q = pl.ds(pl.multiple_of((nq - 1 - k) * t, LANES), t)
            for e in range(2):
                st_refs[slot][e] = lax.dot_general(ka[e], qa_ref[e, rows_q, :], nt_dims,
                                                   preferred_element_type=F32)
                dpt_refs[slot][e] = lax.dot_general(va[e], da_ref[e, rows_q, :], nt_dims,
                                                    preferred_element_type=F32)

        def tile(k, slot, diagonal):
            rows_q = pl.ds(pl.multiple_of((nq - 1 - k) * t, LANES), t)
            st_ref, dpt_ref = st_refs[slot], dpt_refs[slot]
            for e in range(2):
                for r in range(t // strip):
                    rows = slice(r * strip, (r + 1) * strip)
                    sv = st_ref[e, rows, :]
                    if diagonal:
                        key = r * strip + lax.broadcasted_iota(jnp.int32, (strip, t), 0)
                        qry = lax.broadcasted_iota(jnp.int32, (strip, t), 1)
                        sv = jnp.where(key <= qry, sv, NEG)
                    p = jnp.exp(sv)
                    pt_ref[e, rows, :] = p.astype(BF)
                    dst_ref[e, rows, :] = (p * dpt_ref[e, rows, :]).astype(BF)
            for e in range(2):
                dva_ref[e] += jnp.dot(pt_ref[e], da_ref[e, rows_q, :], preferred_element_type=F32)
                dka_ref[e] += jnp.dot(dst_ref[e], qa_ref[e, rows_q, :], preferred_element_type=F32)
                dqa_ref[e, rows_q, :] += lax.dot_general(dst_ref[e], ka[e], tn_dims, preferred_element_type=F32)

        _two_slot_pipeline(nq - 1 - j, scores, tile)

        dk_ref[...] = jnp.where(msks[0], dka_ref[0], dka_ref[1])
        dv_ref[...] = jnp.where(msks[0], dva_ref[0], dva_ref[1])
        dkc_ref[...] = jnp.where(msks[0], pltpu.roll(dka_ref[0], HEAD_DIM, 1), pltpu.roll(dka_ref[1], HEAD_DIM, 1))

        @pl.when(j == nq - 1)
        def _():
            def flush(c, carry):
                rows = pl.ds(pl.multiple_of(c * t, LANES), t)
                a0, a1 = dqa_ref[0, rows, :], dqa_ref[1, rows, :]
                dq_ref[rows, :] = jnp.where(msks[0], a0, a1) * scale
                dqc_ref[rows, :] = jnp.where(msks[0], pltpu.roll(a0, HEAD_DIM, 1), pltpu.roll(a1, HEAD_DIM, 1))
                return carry
            lax.fori_loop(0, nq, flush, 0)

        if ex is not None:
            @pl.when((pl.program_id(0) == nhp - 1) & (j == nq - 1))
            def _():
                ex.wait(*comm)

    full = lambda cb: pl.BlockSpec((s, LANES), lambda h, j: (0, cb + h))
    blk = lambda cb: pl.BlockSpec((t, LANES), lambda h, j: (j, cb + h))
    own_scratch = [pltpu.VMEM((2, s, LANES), BF), pltpu.VMEM((2, s, LANES), BF), pltpu.VMEM((2, s, LANES), F32),
                   pltpu.VMEM((2, t, LANES), F32), pltpu.VMEM((2, t, LANES), F32),
                   pltpu.VMEM((2, t, t), F32), pltpu.VMEM((2, t, t), F32),
                   pltpu.VMEM((2, t, t), F32), pltpu.VMEM((2, t, t), F32),
                   pltpu.VMEM((2, t, t), BF), pltpu.VMEM((2, t, t), BF)]
    return pl.pallas_call(
        kern, name=name,
        out_shape=[jax.ShapeDtypeStruct((s, w), F32)] * 5 + (ex.out_shape if ex else []),
        grid=(nhp, nq),
        in_specs=[full(qb), blk(kb), blk(vb), full(0), full(0),
                  pl.BlockSpec((None, s, LANES), lambda h, j: (h, 0, 0)), full(0)] + (ex.specs if ex else []),
        out_specs=[full(0), blk(0), blk(0), blk(0), full(0)] + (ex.specs if ex else []),
        scratch_shapes=(ex.scratch if ex else []) + own_scratch,
        compiler_params=_params(dimension_semantics=("arbitrary", "arbitrary"),
                                has_side_effects=ex is not None),
    )(proj, proj, proj, do, o, lse, cumx, *ex_arrays)


S5_STATES = 256
S5_ROWS = 512


def _cmul(ar, ai, br, bi):
    return ar * br - ai * bi, ar * bi + ai * br


def _scan_tables(lr, li, reverse):
    w = lr.shape[1]
    row = lax.broadcasted_iota(jnp.int32, (SUBLANES, w), 0)
    if reverse:
        row = SUBLANES - 1 - row
    lr1, li1 = jnp.broadcast_to(lr, (SUBLANES, w)), jnp.broadcast_to(li, (SUBLANES, w))
    lr2, li2 = _cmul(lr1, li1, lr1, li1)
    lr4, li4 = _cmul(lr2, li2, lr2, li2)
    steps = []
    for d, (pr, pi) in zip((1, 2, 4), ((lr1, li1), (lr2, li2), (lr4, li4))):
        keep = row >= d
        steps.append((jnp.where(keep, pr, 0.0), jnp.where(keep, pi, 0.0)))
    cr, ci = lr1, li1
    for bit, (pr, pi) in zip((1, 2, 4), ((lr1, li1), (lr2, li2), (lr4, li4))):
        nr, ni = _cmul(cr, ci, pr, pi)
        has = (row & bit) != 0
        cr, ci = jnp.where(has, nr, cr), jnp.where(has, ni, ci)
    return steps, (cr, ci)


def _scan_block(xr, xi, car_r, car_i, steps, carry_pow, reverse):
    for d, (pr, pi) in zip((1, 2, 4), steps):
        sh = (SUBLANES - d) if reverse else d
        sr, si = pltpu.roll(xr, sh, 0), pltpu.roll(xi, sh, 0)
        xr, xi = xr + (pr * sr - pi * si), xi + (pr * si + pi * sr)
    cr, ci = carry_pow
    xr, xi = xr + (cr * car_r - ci * car_i), xi + (cr * car_i + ci * car_r)
    return xr, xi


def _s5_specs(s, ncb):
    u_spec = pl.BlockSpec((s, LANES), lambda cb, hf: (0, cb))
    wb_spec = pl.BlockSpec((LANES, S5_STATES), lambda cb, hf: (cb, 2 * cb + hf))
    wc_spec = pl.BlockSpec((S5_STATES, LANES), lambda cb, hf: (2 * cb + hf, cb))
    lam_spec = pl.BlockSpec((1, S5_STATES), lambda cb, hf: (0, 2 * cb + hf))
    d_spec = pl.BlockSpec((1, LANES), lambda cb, hf: (0, cb))
    return u_spec, wb_spec, wc_spec, lam_spec, d_spec


def _s5_project_and_scan(u_ref, wbr_ref, wbi_ref, lr_ref, li_ref, xr_ref, xi_ref, s, rows):
    wbr, wbi = wbr_ref[...], wbi_ref[...]
    for r in range(s // rows):
        sl = pl.ds(r * rows, rows)
        ub = u_ref[sl, :].astype(BF)
        xr_ref[sl, :] = jnp.dot(ub, wbr, preferred_element_type=F32)
        xi_ref[sl, :] = jnp.dot(ub, wbi, preferred_element_type=F32)
    steps, cpow = _scan_tables(lr_ref[...], li_ref[...], False)

    def body(b, carry):
        car_r, car_i = carry
        sl = pl.ds(pl.multiple_of(b * SUBLANES, SUBLANES), SUBLANES)
        xr, xi = _scan_block(xr_ref[sl, :], xi_ref[sl, :], car_r, car_i, steps, cpow, False)
        xr_ref[sl, :] = xr
        xi_ref[sl, :] = xi
        return xr[SUBLANES - 1:SUBLANES, :], xi[SUBLANES - 1:SUBLANES, :]

    zero = jnp.zeros((1, S5_STATES), F32)
    lax.fori_loop(0, s // SUBLANES, body, (zero, zero), unroll=4)


def _s5_fwd(proj, wb_re, wb_im, wc_re, wc_im, lam_re, lam_im, dskip, *, name):
    s = proj.shape[0]
    w = dskip.shape[1]
    ncb = w // LANES
    rows = _fit(s, S5_ROWS, SUBLANES)

    def kern(u_ref, wbr_ref, wbi_ref, wcr_ref, wci_ref, lr_ref, li_ref, d_ref, y_ref, xr_ref, xi_ref):
        hf = pl.program_id(1)
        _s5_project_and_scan(u_ref, wbr_ref, wbi_ref, lr_ref, li_ref, xr_ref, xi_ref, s, rows)
        wcr, wci = wcr_ref[...], wci_ref[...]
        for r in range(s // rows):
            sl = pl.ds(r * rows, rows)
            y = (jnp.dot(xr_ref[sl, :].astype(BF), wcr, preferred_element_type=F32)
                 - jnp.dot(xi_ref[sl, :].astype(BF), wci, preferred_element_type=F32))

            @pl.when(hf == 0)
            def _(y=y, sl=sl):
                y_ref[sl, :] = y + d_ref[...] * u_ref[sl, :]

            @pl.when(hf == 1)
            def _(y=y, sl=sl):
                y_ref[sl, :] += y

    u_spec, wb_spec, wc_spec, lam_spec, d_spec = _s5_specs(s, ncb)
    return pl.pallas_call(
        kern, name=name, out_shape=jax.ShapeDtypeStruct((s, w), F32), grid=(ncb, 2),
        in_specs=[u_spec, wb_spec, wb_spec, wc_spec, wc_spec, lam_spec, lam_spec, d_spec],
        out_specs=u_spec,
        scratch_shapes=[pltpu.VMEM((s, S5_STATES), F32), pltpu.VMEM((s, S5_STATES), F32)],
        compiler_params=_params(dimension_semantics=("parallel", "arbitrary")),
    )(proj, wb_re, wb_im, wc_re, wc_im, lam_re, lam_im, dskip)


def _s5_bwd(proj, dy, wb_re, wb_im, wc_re, wc_im, lam_re, lam_im, dskip, *, name):
    s = proj.shape[0]
    w = dskip.shape[1]
    ncb = w // LANES
    rows = _fit(s, S5_ROWS, SUBLANES)
    tn_dims = (((0,), (0,)), ((), ()))
    nt_dims = (((1,), (1,)), ((), ()))

    def kern(u_ref, dy_ref, wbr_ref, wbi_ref, wcr_ref, wci_ref, lr_ref, li_ref, d_ref,
             du_ref, dwbr_ref, dwbi_ref, dwcr_ref, dwci_ref, dlr_ref, dli_ref, dd_ref,
             xr_ref, xi_ref, gr_ref, gi_ref):
        hf = pl.program_id(1)
        _s5_project_and_scan(u_ref, wbr_ref, wbi_ref, lr_ref, li_ref, xr_ref, xi_ref, s, rows)

        wcr, wci = wcr_ref[...], wci_ref[...]
        dwcr = jnp.zeros((S5_STATES, LANES), F32)
        dwci = jnp.zeros((S5_STATES, LANES), F32)
        ddsk = jnp.zeros((1, LANES), F32)
        for r in range(s // rows):
            sl = pl.ds(r * rows, rows)
            dyf = dy_ref[sl, :]
            dyb = dyf.astype(BF)
            gr_ref[sl, :] = lax.dot_general(dyb, wcr, nt_dims, preferred_element_type=F32)
            gi_ref[sl, :] = -lax.dot_general(dyb, wci, nt_dims, preferred_element_type=F32)
            dwcr = dwcr + lax.dot_general(xr_ref[sl, :].astype(BF), dyb, tn_dims, preferred_element_type=F32)
            dwci = dwci - lax.dot_general(xi_ref[sl, :].astype(BF), dyb, tn_dims, preferred_element_type=F32)
            ddsk = ddsk + jnp.sum(dyf * u_ref[sl, :], axis=0, keepdims=True)
        dwcr_ref[...] = dwcr
        dwci_ref[...] = dwci

        @pl.when(hf == 0)
        def _():
            dd_ref[...] = ddsk

        steps, cpow = _scan_tables(lr_ref[...], -li_ref[...], True)
        row = lax.broadcasted_iota(jnp.int32, (SUBLANES, S5_STATES), 0)
        nblk = s // SUBLANES

        def body(k, carry):
            car_r, car_i, ar, ai = carry
            b = nblk - 1 - k
            sl = pl.ds(pl.multiple_of(b * SUBLANES, SUBLANES), SUBLANES)
            g_r, g_i = _scan_block(gr_ref[sl, :], gi_ref[sl, :], car_r, car_i, steps, cpow, True)
            gr_ref[sl, :] = g_r
            gi_ref[sl, :] = g_i
            nr = jnp.where(row == SUBLANES - 1, car_r, pltpu.roll(g_r, SUBLANES - 1, 0))
            ni = jnp.where(row == SUBLANES - 1, car_i, pltpu.roll(g_i, SUBLANES - 1, 0))
            xr, xi = xr_ref[sl, :], xi_ref[sl, :]
            ar = ar + (xr * nr + xi * ni)
            ai = ai + (xr * ni - xi * nr)
            return g_r[0:1, :], g_i[0:1, :], ar, ai

        zero = jnp.zeros((1, S5_STATES), F32)
        zacc = jnp.zeros((SUBLANES, S5_STATES), F32)
        _, _, ar, ai = lax.fori_loop(0, nblk, body, (zero, zero, zacc, zacc), unroll=4)
        dlr_ref[...] = jnp.sum(ar, axis=0, keepdims=True)
        dli_ref[...] = jnp.sum(ai, axis=0, keepdims=True)

        wbr, wbi = wbr_ref[...], wbi_ref[...]
        dwbr = jnp.zeros((LANES, S5_STATES), F32)
        dwbi = jnp.zeros((LANES, S5_STATES), F32)
        for r in range(s // rows):
            sl = pl.ds(r * rows, rows)
            grb, gib = gr_ref[sl, :].astype(BF), gi_ref[sl, :].astype(BF)
            ub = u_ref[sl, :].astype(BF)
            dwbr = dwbr + lax.dot_general(ub, grb, tn_dims, preferred_element_type=F32)
            dwbi = dwbi + lax.dot_general(ub, gib, tn_dims, preferred_element_type=F32)
            du = (lax.dot_general(grb, wbr, nt_dims, preferred_element_type=F32)
                  + lax.dot_general(gib, wbi, nt_dims, preferred_element_type=F32))

            @pl.when(hf == 0)
            def _(du=du, sl=sl):
                du_ref[sl, :] = du + d_ref[...] * dy_ref[sl, :]

            @pl.when(hf == 1)
            def _(du=du, sl=sl):
                du_ref[sl, :] += du
        dwbr_ref[...] = dwbr
        dwbi_ref[...] = dwbi

    u_spec, wb_spec, wc_spec, lam_spec, d_spec = _s5_specs(s, ncb)
    dwb_spec = pl.BlockSpec((None, None, LANES, S5_STATES), lambda cb, hf: (cb, hf, 0, 0))
    dwc_spec = pl.BlockSpec((None, None, S5_STATES, LANES), lambda cb, hf: (cb, hf, 0, 0))
    state = pltpu.VMEM((s, S5_STATES), F32)
    return pl.pallas_call(
        kern, name=name,
        out_shape=[jax.ShapeDtypeStruct((s, w), F32),
                   jax.ShapeDtypeStruct((ncb, 2, LANES, S5_STATES), F32), jax.ShapeDtypeStruct((ncb, 2, LANES, S5_STATES), F32),
                   jax.ShapeDtypeStruct((ncb, 2, S5_STATES, LANES), F32), jax.ShapeDtypeStruct((ncb, 2, S5_STATES, LANES), F32),
                   jax.ShapeDtypeStruct((1, 4 * w), F32), jax.ShapeDtypeStruct((1, 4 * w), F32),
                   jax.ShapeDtypeStruct((1, w), F32)],
        grid=(ncb, 2),
        in_specs=[u_spec, u_spec, wb_spec, wb_spec, wc_spec, wc_spec, lam_spec, lam_spec, d_spec],
        out_specs=[u_spec, dwb_spec, dwb_spec, dwc_spec, dwc_spec, lam_spec, lam_spec, d_spec],
        scratch_shapes=[state, state, state, state],
        compiler_params=_params(dimension_semantics=("parallel", "arbitrary")),
    )(proj, dy, wb_re, wb_im, wc_re, wc_im, lam_re, lam_im, dskip)


def _s5_discretise(lam_re, lam_im, log_dt, b_re, b_im):
    lr = jnp.minimum(lam_re, -EIG_CLIP)
    li = lam_im
    dt = jnp.exp(log_dt)[:, None]
    mag = jnp.exp(lr * dt)
    lbr, lbi = mag * jnp.cos(li * dt), mag * jnp.sin(li * dt)
    den = lr * lr + li * li
    fr = ((lbr - 1.0) * lr + lbi * li) / den
    fi = (lbi * lr - (lbr - 1.0) * li) / den
    bbr = fr[..., None] * b_re - fi[..., None] * b_im
    bbi = fr[..., None] * b_im + fi[..., None] * b_re
    return lbr, lbi, bbr, bbi


def _block_diag(blocks):
    g, a, b = blocks.shape
    eye = jnp.eye(g, dtype=blocks.dtype)
    return (blocks[:, :, None, :] * eye[:, None, :, None]).reshape(g * a, g * b)


def _s5_block_grads(dwb, a, b, transpose):
    ncb = dwb.shape[0]
    gl = LANES // 2 // (a if not transpose else b)
    if not transpose:
        d = dwb.reshape(ncb, 2, 2, gl, a, gl, b)
        parts = [[d[:, hf, hf, g, :, g, :] for g in range(gl)] for hf in range(2)]
    else:
        d = dwb.reshape(ncb, 2, gl, a, 2, gl, b)
        parts = [[d[:, hf, g, :, hf, g, :] for g in range(gl)] for hf in range(2)]
    st = jnp.stack([jnp.stack(p, axis=1) for p in parts], axis=1)
    return st.reshape(ncb * 2 * gl, a, b)


def _adamw(parts, w, m, v, *, name):
    depth, r, c = w.shape
    assert len(parts) == depth
    npart = parts[0].shape[0]
    row_bytes = 4 * (-(-c // LANES) * LANES)
    align = 16 if parts[0].dtype == BF else SUBLANES
    budget = VMEM_LIMIT // 2 // (2 * (depth * npart + 7) * row_bytes)
    tr = _fit(r, max(align, budget // align * align), align)
    nr = r // tr
    c1 = 1.0 / (1.0 - ADAM_B1 ** ADAM_STEP)
    c2 = 1.0 / (1.0 - ADAM_B2 ** ADAM_STEP)

    def kern(*refs):
        p_refs = refs[:depth]
        w_ref, m_ref, v_ref, g_ref, d_ref, nm_ref, nv_ref = refs[depth:]
        layer = pl.program_id(0)
        for l in range(depth):
            @pl.when(layer == l)
            def _(p_ref=p_refs[l]):
                g = p_ref[0].astype(F32)
                for q in range(1, npart):
                    g = g + p_ref[q].astype(F32)
                m2 = ADAM_B1 * m_ref[...] + (1.0 - ADAM_B1) * g
                v2 = ADAM_B2 * v_ref[...] + (1.0 - ADAM_B2) * (g * g)
                upd = (m2 * c1) / (jnp.sqrt(v2 * c2) + ADAM_EPS) + ADAM_WD * w_ref[...]
                g_ref[...] = g
                d_ref[...] = -ADAM_LR * upd
                nm_ref[...] = m2
                nv_ref[...] = v2

    def part_spec(l):
        return pl.BlockSpec((npart, tr, c),
                            lambda ly, i: (0, jnp.where(ly == l, i, jnp.where(ly < l, 0, nr - 1)), 0))

    spec = pl.BlockSpec((None, tr, c), lambda ly, i: (ly, i, 0))
    return pl.pallas_call(
        kern, name=name, out_shape=[jax.ShapeDtypeStruct((depth, r, c), F32)] * 4, grid=(depth, nr),
        in_specs=[part_spec(l) for l in range(depth)] + [spec, spec, spec],
        out_specs=[spec] * 4,
        compiler_params=_params(dimension_semantics=("arbitrary", "arbitrary")),
    )(*parts, w, m, v)


def _sum_parts(parts, *, name):
    npart, r, c = parts.shape

    def kern(p_ref, o_ref):
        g = p_ref[0]
        for q in range(1, npart):
            g = g + p_ref[q]
        o_ref[...] = g

    return pl.pallas_call(kern, name=name, out_shape=jax.ShapeDtypeStruct((r, c), F32), compiler_params=_params())(parts)


class _Exchange:
    def __init__(self, arrays, gather):
        self.n = len(arrays)
        self.gather = gather
        self.out_shape = [jax.ShapeDtypeStruct(((NDEV,) + a.shape) if gather else a.shape, a.dtype) for a in arrays]
        self.scratch = [pltpu.SemaphoreType.DMA((self.n, NDEV - 1)), pltpu.SemaphoreType.DMA((self.n, NDEV - 1)),
                        pltpu.SemaphoreType.DMA((self.n,))]
        self.specs = [pl.BlockSpec(memory_space=pl.ANY)] * self.n

    def _copies(self, srcs, dsts, sems):
        send_sems, recv_sems, local_sems = sems
        x, y, c = lax.axis_index("x"), lax.axis_index("y"), lax.axis_index("c")
        me = 4 * x + 2 * y + c
        local = [pltpu.make_async_copy(srcs[a] if self.gather else srcs[a].at[me], dsts[a].at[me], local_sems.at[a])
                 for a in range(self.n)]
        remote = []
        for k in (1, 2, 4, 3, 5, 6, 7):
            px, py, pc = x ^ ((k >> 2) & 1), y ^ ((k >> 1) & 1), c ^ (k & 1)
            peer = 4 * px + 2 * py + pc
            for a in range(self.n):
                src = srcs[a] if self.gather else srcs[a].at[peer]
                mk = functools.partial(
                    pltpu.make_async_remote_copy, src_ref=src,
                    send_sem=send_sems.at[a, k - 1], recv_sem=recv_sems.at[a, k - 1],
                    device_id=(px, py, pc), device_id_type=pl.DeviceIdType.MESH)
                remote.append((mk(dst_ref=dsts[a].at[me]), mk(dst_ref=dsts[a].at[peer])))
        return local, remote

    def _gather_copies(self, srcs, dsts, sems):
        send_sems, recv_sems, local_sems = sems
        x, y, c = lax.axis_index("x"), lax.axis_index("y"), lax.axis_index("c")
        block = lambda px, py, pc: 4 * px + 2 * py + pc
        me = block(x, y, c)
        chips = [(1 - x, y), (x, 1 - y), (1 - x, 1 - y)]
        local = [pltpu.make_async_copy(srcs[a], dsts[a].at[me], local_sems.at[a]) for a in range(self.n)]
        own, passed = [], []
        for a in range(self.n):
            def copy(k, blk, to, src=None, a=a):
                return pltpu.make_async_remote_copy(
                    src_ref=dsts[a].at[blk] if src is None else src, dst_ref=dsts[a].at[blk],
                    send_sem=send_sems.at[a, k], recv_sem=recv_sems.at[a, k],
                    device_id=to, device_id_type=pl.DeviceIdType.MESH)
            sib = (x, y, 1 - c)
            own.append((copy(0, me, sib, srcs[a]), copy(0, block(x, y, 1 - c), sib)))
            for j, (px, py) in enumerate(chips):
                own.append((copy(1 + j, me, (px, py, c), srcs[a]), copy(1 + j, block(px, py, c), (px, py, c))))
            for j, (px, py) in enumerate(chips):
                passed.append((copy(4 + j, block(px, py, c), sib), copy(4 + j, block(px, py, 1 - c), sib)))
        return local, own, passed

    def start(self, srcs, dsts, sems):
        if self.gather:
            local, own, _ = self._gather_copies(srcs, dsts, sems)
            for cp in local:
                cp.start()
            for send, _ in own:
                send.start()
            return
        local, remote = self._copies(srcs, dsts, sems)
        for cp in local:
            cp.start()
        for send, _ in remote:
            send.start()

    def forward(self, srcs, dsts, sems):
        if not self.gather:
            return
        _, own, passed = self._gather_copies(srcs, dsts, sems)
        for a in range(self.n):
            for j in range(3):
                own[4 * a + 1 + j][1].wait_recv()
                passed[3 * a + j][0].start()

    def wait(self, srcs, dsts, sems):
        if self.gather:
            local, own, passed = self._gather_copies(srcs, dsts, sems)
            for a in range(self.n):
                own[4 * a][1].wait_recv()
            for _, arrival in passed:
                arrival.wait_recv()
            for send, _ in own + passed:
                send.wait_send()
            for cp in local:
                cp.wait()
            return
        local, remote = self._copies(srcs, dsts, sems)
        for send, arrival in remote:
            send.wait_send()
            arrival.wait_recv()
        for cp in local:
            cp.wait()


def _exchange(arrays, gather, *, name):
    ex = _Exchange(arrays, gather)
    n = ex.n

    def kern(*refs):
        srcs, dsts, sems = refs[:n], refs[n:2 * n], refs[2 * n:]
        ex.start(srcs, dsts, sems)
        ex.forward(srcs, dsts, sems)
        ex.wait(srcs, dsts, sems)

    return pl.pallas_call(
        kern, name=name, out_shape=ex.out_shape, in_specs=ex.specs, out_specs=ex.specs, scratch_shapes=ex.scratch,
        compiler_params=pltpu.CompilerParams(has_side_effects=True),
    )(*arrays)


def _pack(arrays):
    flat = jnp.concatenate([a.reshape(-1).astype(F32) for a in arrays])
    pad = (-flat.shape[0]) % (SUBLANES * LANES)
    return jnp.pad(flat, (0, pad)).reshape(-1, LANES)


def _unpack(buf, like):
    flat = buf.reshape(-1)
    out, off = [], 0
    for a in like:
        sz = math.prod(a.shape)
        out.append(flat[off:off + sz].reshape(a.shape))
        off += sz
    return out


def _row(v):
    return v.reshape(1, -1)


def _layer_fwd(x, mod, p, l, ride=None, on_receive=None):
    s, d = x.shape
    sw = d // 2
    nh = d // LANES
    shift_m, scale_m, gate_m, shift_f, scale_f, gate_f = mod
    n = lambda tag: f"{tag}{l}"
    sv = {}

    h1, = _rowwise(lambda xv, g, sc, sh: (xv * _rms(xv) * g) * (1.0 + sc) + sh,
                   [x], [p['g_pre_mix'], scale_m, shift_m], [(d, BF, 'tile')], name=n("pre_mix"))
    proj_a = _mm(h1, p['w_in_a'], name=n("proj_a"))
    flog = _mm(h1, p['w_in_f'], name=n("proj_f"))
    gates = _mm(h1, p['w_in_g'], name=n("proj_g"))

    y_s5 = _s5_fwd(proj_a, p['wb_re'], p['wb_im'], p['wc_re'], p['wc_im'], p['lamb_re'], p['lamb_im'], p['d_skip'],
                   name=n("s5_fwd"))
    z, = _rowwise(_gelu, [y_s5], [], [(sw, BF, 'tile')], name=n("gelu"))
    tglu = _mm(z, p['w_glu'], name=n("glu_mm"))
    ys, = _rowwise(lambda yv, tv, b: _gelu(yv) * _sigmoid(tv + b), [y_s5, tglu], [p['b_glu']], [(sw, BF, 'tile')],
                   name=n("glu"))

    cumx = _cum_fwd(flog, p['b_f_row'], nh, name=n("cum_fwd"))
    ya, lse, *received = _attn_fwd(proj_a, cumx, sw, name=n("attn_fwd"), ride=ride)
    if on_receive is not None:
        on_receive(received)

    am = _mm(ys, p['w_pa'], name=n("pa_mm"))
    bm = _mm(ya, p['w_pb'], name=n("pb_mm"))
    merged, = _rowwise(lambda a, b, ga, gb: _sigmoid(ga) * a + _sigmoid(gb) * b,
                       [am, bm, (gates, d, 0), (gates, d, 1)], [], [(d, BF, 'tile')], name=n("merge"))
    ym = _mm(merged, p['w_o'], name=n("o_mm"))
    x2, = _rowwise(lambda xv, yv, g, gt: xv + gt * (yv * _rms(yv) * g),
                   [x, ym], [p['g_post_mix'], gate_m], [(d, F32, 'tile')], name=n("post_mix"))

    h2, = _rowwise(lambda xv, g, sc, sh: (xv * _rms(xv) * g) * (1.0 + sc) + sh,
                   [x2], [p['g_pre_ffn'], scale_f, shift_f], [(d, BF, 'tile')], name=n("pre_ffn"))
    gt = _mm(h2, p['w_ffn_gate'], name=n("gate_mm"))
    up = _mm(h2, p['w_ffn_up'], name=n("up_mm"))
    dff = gt.shape[1]
    act, = _rowwise(lambda g, u: _silu(g) * u, [gt, up], [], [(dff, BF, 'tile')], name=n("swiglu"))
    yf = _mm(act, p['w_ffn_down'], name=n("down_mm"))
    x3, = _rowwise(lambda xv, yv, g, gt_: xv + gt_ * (yv * _rms(yv) * g),
                   [x2, yf], [p['g_post_ffn'], gate_f], [(d, F32, 'tile')], name=n("post_ffn"))

    sv.update(x=x, h1=h1, proj_a=proj_a, flog=flog, gates=gates, y_s5=y_s5, z=z, tglu=tglu, ys=ys, cumx=cumx,
              ya=ya, lse=lse, am=am, bm=bm, merged=merged, ym=ym, x2=x2, h2=h2, gt=gt, up=up,
              act=act, yf=yf)
    return x3, sv


def _layer_bwd(dx3, sv, mod, p, l, make_ride=None, on_receive=None):
    x, x2 = sv['x'], sv['x2']
    s, d = x.shape
    sw = d // 2
    nh = d // LANES
    shift_m, scale_m, gate_m, shift_f, scale_f, gate_f = mod
    n = lambda tag: f"{tag}{l}"
    gw, gs = {}, {}

    def post_bwd(dxo, yv, g, gate):
        r = _rms(yv)
        nf = yv * r
        dn = dxo * gate * g
        return _norm_bwd(dn, nf, r), dxo * (nf * g), dxo * gate * nf

    def pre_bwd(dh, dres, xv, g, sc):
        r = _rms(xv)
        xh = xv * r
        n3 = xh * g
        dn3 = dh * (1.0 + sc)
        return dres + _norm_bwd(dn3 * g, xh, r), dh, dh * n3, dn3 * xh

    dyf, dgate_f, gs['g_post_ffn'] = _rowwise(
        post_bwd, [dx3, sv['yf']], [p['g_post_ffn'], gate_f],
        [(d, BF, 'tile'), (d, F32, 'sum'), (d, F32, 'sum')], name=n("post_ffn_bwd"))
    dff = sv['gt'].shape[1]
    dact = _mm(dyf, p['w_ffn_down'], tb=True, name=n("down_bwd_x"))
    gw['w_ffn_down'] = _mm(sv['act'], dyf, ta=True, out_dtype=BF, name=n("down_bwd_w"))

    def swiglu_bwd(da, g, u):
        sg = _sigmoid(g)
        return da * u * (sg * (1.0 + g * (1.0 - sg))), da * (g * sg)

    dgt, dup = _rowwise(swiglu_bwd, [dact, sv['gt'], sv['up']], [], [(dff, BF, 'tile'), (dff, BF, 'tile')],
                        name=n("swiglu_bwd"))
    dh2a = _mm(dgt, p['w_ffn_gate'], tb=True, name=n("gate_bwd_x"))
    dh2b = _mm(dup, p['w_ffn_up'], tb=True, name=n("up_bwd_x"))
    gw['w_ffn_gate'] = _mm(sv['h2'], dgt, ta=True, out_dtype=BF, name=n("gate_bwd_w"))
    gw['w_ffn_up'] = _mm(sv['h2'], dup, ta=True, out_dtype=BF, name=n("up_bwd_w"))
    dx2, dshift_f, dscale_f, gs['g_pre_ffn'] = _rowwise(
        lambda da, db, dres, xv, g, sc: pre_bwd(da + db, dres, xv, g, sc),
        [dh2a, dh2b, dx3, x2], [p['g_pre_ffn'], scale_f],
        [(d, F32, 'tile'), (d, F32, 'sum'), (d, F32, 'sum'), (d, F32, 'sum')], name=n("pre_ffn_bwd"))

    dym, dgate_m, gs['g_post_mix'] = _rowwise(
        post_bwd, [dx2, sv['ym']], [p['g_post_mix'], gate_m],
        [(d, BF, 'tile'), (d, F32, 'sum'), (d, F32, 'sum')], name=n("post_mix_bwd"))
    dmerged = _mm(dym, p['w_o'], tb=True, name=n("o_bwd_x"))
    gw['w_o'] = _mm(sv['merged'], dym, ta=True, out_dtype=BF, name=n("o_bwd_w"))

    def merge_bwd(dm, a, b, ga, gb):
        sa, sb = _sigmoid(ga), _sigmoid(gb)
        return dm * sa, dm * sb, dm * a * sa * (1.0 - sa), dm * b * sb * (1.0 - sb)

    da_, db_, dga, dgb = _rowwise(
        merge_bwd, [dmerged, sv['am'], sv['bm'], (sv['gates'], d, 0), (sv['gates'], d, 1)], [],
        [(d, BF, 'tile')] * 4, name=n("merge_bwd"))
    dys = _mm(da_, p['w_pa'], tb=True, name=n("pa_bwd_x"))
    gw['w_pa'] = _mm(sv['ys'], da_, ta=True, out_dtype=BF, name=n("pa_bwd_w"))
    dya = _mm(db_, p['w_pb'], tb=True, name=n("pb_bwd_x"))
    gw['w_pb'] = _mm(sv['ya'], db_, ta=True, out_dtype=BF, name=n("pb_bwd_w"))

    sent = list(gw)
    dq, dk, dv, dkc, dqc, *received = _attn_bwd(
        sv['proj_a'], dya, sv['ya'], sv['lse'], sv['cumx'], sw, name=n("attn_bwd"),
        ride=make_ride({k: gw[k] for k in sent}) if make_ride is not None else None)
    if on_receive is not None:
        on_receive(sent, received)
    frow = sv['flog'][:, :nh].T
    dcum = jnp.stack([-dkc[:, BIAS_LANES::HEAD_DIM].T, dqc[:, ::HEAD_DIM].T])
    dfrow, dbf = _cum_bwd(dcum, frow, p['b_f_col'], name=n("cum_bwd"))
    gs['b_f'] = dbf.reshape(nh)
    dflog = jnp.pad(dfrow.T, ((0, 0), (0, LANES - nh))).astype(BF)

    def glu_bwd(dy_, yv, tv, b):
        zv = _gelu(yv)
        sg = _sigmoid(tv + b)
        dt = dy_ * zv * sg * (1.0 - sg)
        return dt, dy_ * sg, dt

    dt, dz1, gs['b_glu'] = _rowwise(glu_bwd, [dys, sv['y_s5'], sv['tglu']], [p['b_glu']],
                                    [(sw, BF, 'tile'), (sw, F32, 'tile'), (sw, F32, 'sum')], name=n("glu_bwd"))
    dz2 = _mm(dt, p['w_glu'], tb=True, name=n("glu_bwd_x"))
    gw['w_glu'] = _mm(sv['z'], dt, ta=True, out_dtype=BF, name=n("glu_bwd_w"))
    dy_s5, = _rowwise(lambda a, b, yv: (a + b) * _gelu_grad(yv), [dz1, dz2, sv['y_s5']], [], [(sw, F32, 'tile')],
                      name=n("gelu_bwd"))
    du, dwbr, dwbi, dwcr, dwci, dlr, dli, gs['d_skip'] = _s5_bwd(
        sv['proj_a'], dy_s5, p['wb_re'], p['wb_im'], p['wc_re'], p['wc_im'], p['lamb_re'], p['lamb_im'], p['d_skip'],
        name=n("s5_bwd"))
    g_ = sw // SSM_H
    pst = p['lamb_re'].shape[1] // g_
    gs['lamb_re'], gs['lamb_im'] = dlr.reshape(g_, pst), dli.reshape(g_, pst)
    gs['bbar_re'] = _s5_block_grads(dwbr, SSM_H, pst, False).transpose(0, 2, 1)
    gs['bbar_im'] = _s5_block_grads(dwbi, SSM_H, pst, False).transpose(0, 2, 1)
    gs['c_re'] = _s5_block_grads(dwcr, pst, SSM_H, True).transpose(0, 2, 1)
    gs['c_im'] = _s5_block_grads(dwci, pst, SSM_H, True).transpose(0, 2, 1)

    dproj = jnp.concatenate([du.astype(BF), dq.astype(BF), dk.astype(BF), dv.astype(BF), dflog, dga, dgb], axis=1)
    gw['w_in'] = _mm(sv['h1'], dproj, ta=True, out_dtype=BF, name=n("proj_bwd_w"))
    if make_ride is not None:
        tail = [k for k in gw if k not in sent]
        dh1, received = _mm(dproj, p['w_in_all'], tb=True, tk=1408, name=n("proj_bwd_x"),
                            ride=make_ride({k: gw[k] for k in tail}))
        on_receive(tail, received)
        gw = {}
    else:
        dh1 = _mm(dproj, p['w_in_all'], tb=True, tk=1408, name=n("proj_bwd_x"))
    dx, dshift_m, dscale_m, gs['g_pre_mix'] = _rowwise(
        pre_bwd, [dh1, dx2, x], [p['g_pre_mix'], scale_m],
        [(d, F32, 'tile'), (d, F32, 'sum'), (d, F32, 'sum'), (d, F32, 'sum')], name=n("pre_mix_bwd"))
    dmod = [dshift_m, dscale_m, dgate_m, dshift_f, dscale_f, dgate_f]
    return dx, gw, dmod, gs


def _unshard(k, blocks):
    if k in COL_SHARDED:
        return blocks.transpose(1, 0, 2).reshape(blocks.shape[1], NDEV * blocks.shape[2])
    return blocks.reshape(NDEV * blocks.shape[1], blocks.shape[2])


def _to_slabs(k, g):
    if k == 'w_in':
        d = g.shape[0]
        nh = d // LANES
        g = jnp.concatenate([g[:, :2 * d + nh], g[:, 2 * d + LANES:]], axis=1)
    if k in COL_SHARDED:
        return g.reshape(g.shape[0], NDEV, g.shape[1] // NDEV).transpose(1, 0, 2)
    return g.reshape(NDEV, g.shape[0] // NDEV, g.shape[1])


def _prep_w_in(w_in):
    d = w_in.shape[0]
    nh = d // LANES
    fcol = 2 * d
    p = {}
    p['w_in_a'] = w_in[:, :fcol]
    p['w_in_f'] = jnp.pad(w_in[:, fcol:fcol + nh], ((0, 0), (0, LANES - nh)))
    p['w_in_g'] = w_in[:, fcol + nh:]
    p['w_in_all'] = jnp.concatenate([p['w_in_a'], p['w_in_f'], p['w_in_g']], axis=1)
    return p


def _prep_small(small):
    nh = small['b_f'].shape[0]
    p = {}
    for k in ('g_pre_mix', 'g_post_mix', 'g_pre_ffn', 'g_post_ffn', 'd_skip', 'b_glu'):
        p[k] = _row(small[k])
    p['b_f_row'] = jnp.pad(_row(small['b_f']), ((0, 0), (0, LANES - nh)))
    p['b_f_col'] = small['b_f'].reshape(nh, 1)
    lbr, lbi, bbr, bbi = _s5_discretise(small['lam_re'], small['lam_im'], small['log_dt'], small['b_re'], small['b_im'])
    p['lamb_re'], p['lamb_im'] = _row(lbr), _row(lbi)
    p['wb_re'] = _block_diag(bbr.transpose(0, 2, 1)).astype(BF)
    p['wb_im'] = _block_diag(bbi.transpose(0, 2, 1)).astype(BF)
    p['wc_re'] = _block_diag(small['c_re'].transpose(0, 2, 1)).astype(BF)
    p['wc_im'] = _block_diag(small['c_im'].transpose(0, 2, 1)).astype(BF)
    return p


def _local_step(x, target, mods, ps, small, hooks=None):
    depth = len(ps)
    s, d = x.shape
    hooks = hooks or {}
    saved = []
    h = x
    for l in range(depth):
        h, sv = _layer_fwd(h, mods[l], ps[l], l, ride=hooks['fwd_ride'](l) if hooks else None,
                           on_receive=functools.partial(hooks['fwd_recv'], l) if hooks else None)
        saved.append(sv)

    def loss_fn(yv, tv):
        e = yv - tv
        return e * (1.0 / d), jnp.sum(e * e, axis=1, keepdims=True) * (0.5 / d)

    dy, loss = _rowwise(loss_fn, [h, target], [], [(d, F32, 'tile'), (1, F32, 'sum')], name="loss")
    dmods, gss = [None] * depth, [None] * depth
    unsent = {}
    for l in range(depth - 1, -1, -1):
        def on_receive(names, results, l=l):
            hooks['bwd_recv']([(k, l) for k in names], results)

        dy, gw, dmods[l], gs = _layer_bwd(dy, saved[l], mods[l], ps[l], l,
                                          make_ride=hooks['bwd_ride'] if hooks else None,
                                          on_receive=on_receive if hooks else None)
        unsent.update({(k, l): g for k, g in gw.items()})
        sm = small[l]
        _, vjp = jax.vjp(_s5_discretise, sm['lam_re'], sm['lam_im'], sm['log_dt'], sm['b_re'], sm['b_im'])
        gs['lam_re'], gs['lam_im'], gs['log_dt'], gs['b_re'], gs['b_im'] = vjp(
            (gs.pop('lamb_re'), gs.pop('lamb_im'), gs.pop('bbar_re'), gs.pop('bbar_im')))
        gss[l] = gs
    return loss, dy, unsent, dmods, gss


SMALL_LOCAL = ['g_pre_mix', 'g_post_mix', 'g_pre_ffn', 'g_post_ffn', 'lam_re', 'lam_im', 'log_dt', 'b_re', 'b_im',
               'c_re', 'c_im', 'd_skip', 'b_glu', 'b_f']


def kernel(x, c, w_ada, b_ada, g_pre_mix, g_post_mix, g_pre_ffn, g_post_ffn, w_in, lam_re, lam_im, log_dt, b_re, b_im, c_re, c_im, d_skip, w_glu, b_glu, b_f, w_pa, w_pb, w_o, w_ffn_gate, w_ffn_up, w_ffn_down, loss_target, m_w_ada, m_b_ada, m_g_pre_mix, m_g_post_mix, m_g_pre_ffn, m_g_post_ffn, m_w_in, m_lam_re, m_lam_im, m_log_dt, m_b_re, m_b_im, m_c_re, m_c_im, m_d_skip, m_w_glu, m_b_glu, m_b_f, m_w_pa, m_w_pb, m_w_o, m_w_ffn_gate, m_w_ffn_up, m_w_ffn_down, v_w_ada, v_b_ada, v_g_pre_mix, v_g_post_mix, v_g_pre_ffn, v_g_post_ffn, v_w_in, v_lam_re, v_lam_im, v_log_dt, v_b_re, v_b_im, v_c_re, v_c_im, v_d_skip, v_w_glu, v_b_glu, v_b_f, v_w_pa, v_w_pb, v_w_o, v_w_ffn_gate, v_w_ffn_up, v_w_ffn_down):
    args = dict(locals())
    W = {k: args[k] for k in WEIGHTS}
    M = {k: args['m_' + k] for k in WEIGHTS}
    V = {k: args['v_' + k] for k in WEIGHTS}
    depth, d, ncol = w_ada.shape
    s = x.shape[1]
    me = 4 * lax.axis_index("x") + 2 * lax.axis_index("y") + lax.axis_index("c")

    c_all, = _exchange([jnp.pad(c, ((0, SUBLANES - 1), (0, 0)))], True, name="gather_c")
    c_all = c_all[:, 0, :]
    cond, = _rowwise(_silu, [c_all], [], [(d, F32, 'tile')], name="cond")
    mod_part = jnp.stack([_mm(cond, w_ada[l], name=f"ada_mm{l}") for l in range(depth)], axis=1)
    mod_recv, = _exchange([mod_part.reshape(NDEV, depth, 1, ncol)], False, name="scatter_mod")
    mod_cat = mod_recv.reshape(NDEV, depth, ncol).transpose(1, 0, 2).reshape(depth, NDEV * ncol)
    mod, = _rowwise(lambda a, b: a + b, [mod_cat, b_ada], [], [(NDEV * ncol, F32, 'tile')], name="mod_bias")
    mods = [[mod[l:l + 1, i * d:(i + 1) * d] for i in range(6)] for l in range(depth)]

    small = [{k: W[k][l] for k in SMALL_LOCAL} for l in range(depth)]
    ps = [_prep_small(small[l]) for l in range(depth)]
    first = ['w_in', 'w_glu']
    rest = [k for k in BIG if k not in first]
    riding = [[(k, l) for k in rest] + [(k, l + 1) for k in first if l + 1 < depth] for l in range(depth)]

    def take_weights(keys, results):
        for (k, l), blocks in zip(keys, results):
            full = _unshard(k, blocks)
            ps[l].update(_prep_w_in(full) if k == 'w_in' else {k: full})

    take_weights([(k, 0) for k in first],
                 _exchange([W[k][0].astype(BF) for k in first], True, name="gather_w_first"))

    def fwd_ride(l):
        blocks = [W[k][ll].astype(BF) for k, ll in riding[l]]
        return _Exchange(blocks, True), blocks

    grad_parts = {}

    def bwd_ride(grads):
        slabs = [_to_slabs(k, g) for k, g in grads.items()]
        return _Exchange(slabs, False), slabs

    hooks = dict(fwd_ride=fwd_ride, fwd_recv=lambda l, results: take_weights(riding[l], results),
                 bwd_ride=bwd_ride, bwd_recv=lambda keys, results: grad_parts.update(zip(keys, results)))

    loss, dx, unsent, dmods, gss = _local_step(x[0], loss_target[0], mods, ps, small, hooks)
    assert not unsent
    loss = lax.psum(loss[0, 0], ("x", "y", "c"))
    out = {}
    for k in BIG:
        out[k] = _adamw([grad_parts[(k, l)] for l in range(depth)], W[k], M[k], V[k], name=f"adamw_{k}")

    dmod_mine = jnp.stack([jnp.concatenate(dmods[l], axis=1)[0] for l in range(depth)])
    small_mine = [dmod_mine] + [jnp.stack([gss[l][k] for l in range(depth)]) for k in SMALL_LOCAL]
    parts, = _exchange([_pack(small_mine)], True, name="gather_small")
    summed = _sum_parts(parts, name="sum_small")
    names = ['b_ada'] + SMALL_LOCAL
    grads = _unpack(summed, [W[k] for k in names])
    res = _adamw([_pack(grads)[None]], _pack([W[k] for k in names])[None], _pack([M[k] for k in names])[None],
                 _pack([V[k] for k in names])[None], name="adamw_small")
    unpacked = [_unpack(a, [W[k] for k in names]) for a in res]
    for i, k in enumerate(names):
        out[k] = [unpacked[j][i] for j in range(4)]

    dmod_all = parts.reshape(NDEV, -1)[:, :depth * 6 * d].reshape(NDEV, depth, 6 * d)
    dmod_cols = lax.dynamic_slice_in_dim(dmod_all, me * ncol, ncol, axis=2)
    g_ada = [_mm(cond, dmod_cols[:, l], ta=True, precision=HI, name=f"ada_bwd{l}")[None] for l in range(depth)]
    out['w_ada'] = _adamw(g_ada, w_ada, m_w_ada, v_w_ada, name="adamw_w_ada")

    return (loss, dx[None], *[out[k][0] for k in WEIGHTS], *[out[k][1] for k in WEIGHTS],
            *[out[k][2] for k in WEIGHTS], *[out[k][3] for k in WEIGHTS])
```

```python
import functools
import math

import jax
import jax.numpy as jnp
from jax import lax
from jax.experimental import pallas as pl
from jax.experimental.pallas import tpu as pltpu

F32 = jnp.float32
BF = jnp.bfloat16
NDEV = 8
LANES = 128
SUBLANES = 8
VMEM_LIMIT = 48 * 1024 * 1024

SSM_H = 16
HEAD_DIM = 64
RMS_EPS = 1e-6
EIG_CLIP = 1e-4
ADAM_LR = 0.001
ADAM_B1 = 0.9
ADAM_B2 = 0.999
ADAM_EPS = 1e-08
ADAM_WD = 0.01
ADAM_STEP = 10
NEG = -1e30
HI = lax.Precision.HIGHEST

WEIGHTS = ['w_ada', 'b_ada', 'g_pre_mix', 'g_post_mix', 'g_pre_ffn', 'g_post_ffn', 'w_in', 'lam_re', 'lam_im',
           'log_dt', 'b_re', 'b_im', 'c_re', 'c_im', 'd_skip', 'w_glu', 'b_glu', 'b_f', 'w_pa', 'w_pb', 'w_o',
           'w_ffn_gate', 'w_ffn_up', 'w_ffn_down']
COL_SHARDED = ['w_in', 'w_pa', 'w_pb', 'w_ffn_gate', 'w_ffn_up']
ROW_SHARDED = ['w_glu', 'w_o', 'w_ffn_down']
BIG = COL_SHARDED + ROW_SHARDED
SMALL = ['b_ada', 'g_pre_mix', 'g_post_mix', 'g_pre_ffn', 'g_post_ffn', 'lam_re', 'lam_im', 'log_dt', 'b_re',
         'b_im', 'c_re', 'c_im', 'd_skip', 'b_glu', 'b_f']


def _fit(dim, target, align):
    if dim <= target:
        return dim
    t = (target // align) * align
    while t >= align:
        if dim % t == 0:
            return t
        t -= align
    return dim


def _params(**kw):
    return pltpu.CompilerParams(vmem_limit_bytes=VMEM_LIMIT, **kw)


def _mm(a, b, *, ta=False, tb=False, out_dtype=F32, tm=512, tn=512, tk=2048, precision=None, name, ride=None):
    m, k = (a.shape[1], a.shape[0]) if ta else a.shape
    n = b.shape[0] if tb else b.shape[1]
    assert (b.shape[1] if tb else b.shape[0]) == k
    tm = _fit(m, tm, LANES if ta else 16)
    tn = _fit(n, tn, LANES)
    tk = _fit(k, tk, LANES)
    nk = k // tk
    grid = (m // tm, n // tn, nk)
    dims = (((0 if ta else 1,), (1 if tb else 0,)), ((), ()))
    ex, ex_arrays = ride if ride is not None else (None, [])

    def kern(*refs):
        (a_ref, b_ref), (o_ref,), comm, scratch = _ride_split(ex, refs, 2, 1)
        step = (pl.program_id(0) * grid[1] + pl.program_id(1)) * grid[2] + pl.program_id(2)
        if ex is not None:
            @pl.when(step == 0)
            def _():
                ex.start(*comm)

            @pl.when(step == (grid[0] * grid[1] * grid[2]) // 2)
            def _():
                ex.forward(*comm)

        av, bv = a_ref[...], b_ref[...]
        if precision is None:
            av, bv = av.astype(BF), bv.astype(BF)
        p = lax.dot_general(av, bv, dims, preferred_element_type=F32, precision=precision)
        if nk == 1:
            o_ref[...] = p.astype(out_dtype)
        else:
            acc_ref, = scratch
            kk = pl.program_id(2)

            @pl.when(kk == 0)
            def _():
                acc_ref[...] = p

            @pl.when(kk > 0)
            def _():
                acc_ref[...] += p

            @pl.when(kk == nk - 1)
            def _():
                o_ref[...] = acc_ref[...].astype(out_dtype)

        if ex is not None:
            @pl.when(step == grid[0] * grid[1] * grid[2] - 1)
            def _():
                ex.wait(*comm)

    a_spec = pl.BlockSpec((tk, tm), lambda i, j, kk: (kk, i)) if ta else pl.BlockSpec((tm, tk), lambda i, j, kk: (i, kk))
    b_spec = pl.BlockSpec((tn, tk), lambda i, j, kk: (j, kk)) if tb else pl.BlockSpec((tk, tn), lambda i, j, kk: (kk, j))
    res = pl.pallas_call(
        kern, name=name,
        out_shape=[jax.ShapeDtypeStruct((m, n), out_dtype)] + (ex.out_shape if ex else []),
        grid=grid,
        in_specs=[a_spec, b_spec] + (ex.specs if ex else []),
        out_specs=[pl.BlockSpec((tm, tn), lambda i, j, kk: (i, j))] + (ex.specs if ex else []),
        scratch_shapes=(ex.scratch if ex else []) + ([] if nk == 1 else [pltpu.VMEM((tm, tn), F32)]),
        compiler_params=_params(dimension_semantics=("arbitrary",) * 3 if ex else ("parallel", "parallel", "arbitrary"),
                                has_side_effects=ex is not None),
    )(a, b, *ex_arrays)
    return (res[0], res[1:]) if ex else res[0]


def _rowwise(fn, tiles, params, outs, *, tr=256, name):
    tiles = [t if isinstance(t, tuple) else (t, t.shape[1], 0) for t in tiles]
    s = tiles[0][0].shape[0]
    tr = _fit(s, tr, 16)
    nt, npar = len(tiles), len(params)

    def kern(*refs):
        i = pl.program_id(0)
        res = fn(*[r[...] for r in refs[:nt + npar]])
        if not isinstance(res, (tuple, list)):
            res = (res,)
        for (w, dt, kind), o_ref, r in zip(outs, refs[nt + npar:], res):
            if kind == 'tile':
                o_ref[...] = r.astype(dt)
            else:
                part = jnp.sum(r.astype(F32), axis=0, keepdims=True)

                @pl.when(i == 0)
                def _(o_ref=o_ref, part=part):
                    o_ref[...] = part

                @pl.when(i > 0)
                def _(o_ref=o_ref, part=part):
                    o_ref[...] += part

    def tile_spec(w, cb):
        return pl.BlockSpec((tr, w), lambda i: (i, cb))

    in_specs = [tile_spec(w, cb) for _, w, cb in tiles]
    in_specs += [pl.BlockSpec(p.shape, lambda i, nd=p.ndim: (0,) * nd) for p in params]
    out_shape, out_specs = [], []
    for w, dt, kind in outs:
        if kind == 'tile':
            out_shape.append(jax.ShapeDtypeStruct((s, w), dt))
            out_specs.append(pl.BlockSpec((tr, w), lambda i: (i, 0)))
        else:
            out_shape.append(jax.ShapeDtypeStruct((1, w), F32))
            out_specs.append(pl.BlockSpec((1, w), lambda i: (0, 0)))
    res = pl.pallas_call(
        kern, name=name, out_shape=out_shape, grid=(s // tr,), in_specs=in_specs, out_specs=out_specs,
        compiler_params=_params(dimension_semantics=("arbitrary",)),
    )(*[t[0] for t in tiles], *params)
    return res


def _sigmoid(z):
    return 1.0 / (1.0 + jnp.exp(-z))


def _silu(z):
    return z * _sigmoid(z)


_GELU_K = math.sqrt(2.0 / math.pi)


def _gelu(y):
    return 0.5 * y * (1.0 + jnp.tanh(_GELU_K * (y + 0.044715 * y * y * y)))


def _gelu_grad(y):
    th = jnp.tanh(_GELU_K * (y + 0.044715 * y * y * y))
    return 0.5 * (1.0 + th) + 0.5 * y * (1.0 - th * th) * _GELU_K * (1.0 + 3.0 * 0.044715 * y * y)


def _rms(x):
    return lax.rsqrt(jnp.mean(x * x, axis=-1, keepdims=True) + RMS_EPS)


def _norm_bwd(dn, xhat, r):
    return r * (dn - xhat * jnp.mean(dn * xhat, axis=-1, keepdims=True))


def _cum_fwd(flog, bf_row, nh, *, name):
    s = flog.shape[0]
    w = nh * HEAD_DIM
    t = _fit(s, 256, SUBLANES)

    def kern(f_ref, b_ref, o_ref, carry_ref):
        i = pl.program_id(0)

        @pl.when(i == 0)
        def _():
            carry_ref[...] = jnp.zeros_like(carry_ref)

        z = f_ref[...] + b_ref[...]
        logf = jnp.minimum(z, 0.0) - jnp.log(1.0 + jnp.exp(-jnp.abs(z)))
        hh = lax.broadcasted_iota(jnp.int32, (LANES, w), 0)
        cc = lax.broadcasted_iota(jnp.int32, (LANES, w), 1)
        expand = (cc // HEAD_DIM == hh).astype(F32)
        lx = jnp.dot(logf, expand, preferred_element_type=F32, precision=HI)
        rr = lax.broadcasted_iota(jnp.int32, (t, t), 0)
        kk = lax.broadcasted_iota(jnp.int32, (t, t), 1)
        tri = (kk <= rr).astype(F32)
        cum = jnp.dot(tri, lx, preferred_element_type=F32, precision=HI) + carry_ref[...]
        o_ref[...] = cum
        carry_ref[...] = cum[t - 1:t, :]

    return pl.pallas_call(
        kern, name=name, out_shape=jax.ShapeDtypeStruct((s, w), F32), grid=(s // t,),
        in_specs=[pl.BlockSpec((t, LANES), lambda i: (i, 0)), pl.BlockSpec((1, LANES), lambda i: (0, 0))],
        out_specs=pl.BlockSpec((t, w), lambda i: (i, 0)),
        scratch_shapes=[pltpu.VMEM((1, w), F32)],
        compiler_params=_params(dimension_semantics=("arbitrary",)),
    )(flog, bf_row)


def _cum_bwd(dcrow, frow, bf_col, *, name):
    _, nh, s = dcrow.shape
    t = _fit(s, 512, LANES)
    nb = s // t

    def kern(d_ref, f_ref, b_ref, df_ref, db_ref):
        rr = lax.broadcasted_iota(jnp.int32, (t, t), 0)
        kk = lax.broadcasted_iota(jnp.int32, (t, t), 1)
        upper = (rr >= kk).astype(F32)
        carry = jnp.zeros((nh, 1), F32)
        db = jnp.zeros((nh, 1), F32)
        for blk in range(nb - 1, -1, -1):
            sl = slice(blk * t, (blk + 1) * t)
            rc = jnp.dot(d_ref[0, :, sl] + d_ref[1, :, sl], upper, preferred_element_type=F32, precision=HI) + carry
            carry = rc[:, 0:1]
            df = rc * _sigmoid(-(f_ref[:, sl] + b_ref[...]))
            df_ref[:, sl] = df
            db = db + jnp.sum(df, axis=1, keepdims=True)
        db_ref[...] = db

    return pl.pallas_call(
        kern, name=name,
        out_shape=[jax.ShapeDtypeStruct((nh, s), F32), jax.ShapeDtypeStruct((nh, 1), F32)],
        compiler_params=_params(),
    )(dcrow, frow, bf_col)


def _ride_split(ex, refs, n_in, n_out):
    n = ex.n if ex is not None else 0
    own_in, srcs = refs[:n_in], refs[n_in:n_in + n]
    own_out, dsts = refs[n_in + n:n_in + n + n_out], refs[n_in + n + n_out:n_in + 2 * n + n_out]
    sems = refs[n_in + 2 * n + n_out:n_in + 2 * n + n_out + 3] if n else ()
    rest = refs[n_in + 2 * n + n_out + (3 if n else 0):]
    return own_in, own_out, (srcs, dsts, sems), rest


ATTN_STRIP = 32
BIAS_LANES = 3


def _head_masks(rows):
    lane = lax.broadcasted_iota(jnp.int32, (rows, LANES), 1)
    return [(lane >= HEAD_DIM * e) & (lane < HEAD_DIM * (e + 1)) for e in range(2)]


def _augment(feat, bias, e, *, bias_slot, ones_slot):
    rows = feat.shape[0]
    lane = lax.broadcasted_iota(jnp.int32, (rows, LANES), 1)
    own = (lane >= HEAD_DIM * e) & (lane < HEAD_DIM * (e + 1))
    off = lane - HEAD_DIM * (1 - e)
    out = jnp.where(own, feat, 0.0)
    if ones_slot is not None:
        out = jnp.where((off >= ones_slot * BIAS_LANES) & (off < (ones_slot + 1) * BIAS_LANES), 1.0, out)
    if bias is not None:
        rest = pltpu.roll(bias, HEAD_DIM, 1)
        for term in range(BIAS_LANES):
            part = rest.astype(BF).astype(F32)
            out = jnp.where(off == bias_slot * BIAS_LANES + term, part, out)
            rest = rest - part
    return out.astype(BF)


def _two_slot_pipeline(m, scores, tile):
    scores(0, 0)

    def pair(n, carry):
        k = 2 * n
        scores(k + 1, 1)
        tile(k, 0, False)
        scores(k + 2, 0)
        tile(k + 1, 1, False)
        return carry

    lax.fori_loop(0, m // 2, pair, 0)

    @pl.when(m % 2 == 0)
    def _():
        tile(m, 0, True)

    @pl.when(m % 2 == 1)
    def _():
        scores(m, 1)
        tile(m - 1, 0, False)
        tile(m, 1, True)


def _attn_fwd(proj, cumx, qcol, *, name, ride=None):
    s = proj.shape[0]
    w = cumx.shape[1]
    nhp = w // LANES
    t = _fit(s, 256, LANES)
    nq = s // t
    strip = _fit(t, ATTN_STRIP, 16)
    scale = HEAD_DIM ** -0.5
    qb, kb, vb = qcol // LANES, (qcol + w) // LANES, (qcol + 2 * w) // LANES
    ex, ex_arrays = ride if ride is not None else (None, [])
    nt_dims = (((1,), (1,)), ((), ()))

    def kern(*refs):
        own_in, (o_ref, l_ref), comm, scratch = _ride_split(ex, refs, 5, 2)
        q_ref, k_ref, v_ref, cxq_ref, cxk_ref = own_in
        ka_ref, vat_ref, s0_ref, s1_ref, p_ref, m_ref, acc_ref = scratch
        s_refs = (s0_ref, s1_ref)
        i = pl.program_id(1)
        if ex is not None:
            @pl.when((pl.program_id(0) == 0) & (i == 0))
            def _():
                ex.start(*comm)

            @pl.when((pl.program_id(0) == nhp - 1) & (i == 0))
            def _():
                ex.forward(*comm)

        msks = _head_masks(t)

        @pl.when(i == 0)
        def _():
            def build(c, carry):
                rows = pl.ds(pl.multiple_of(c * t, LANES), t)
                k2, v2, cx = k_ref[rows, :], v_ref[rows, :], cxk_ref[rows, :]
                for e in range(2):
                    ka_ref[e, rows, :] = _augment(k2, -cx, e, bias_slot=1, ones_slot=0)
                    vat_ref[e, :, rows] = jnp.where(msks[e], v2, 1.0).T.astype(BF)
                return carry
            lax.fori_loop(0, nq, build, 0)

        q2 = q_ref[...] * scale
        qa = [_augment(q2, cxq_ref[...], e, bias_slot=0, ones_slot=1) for e in range(2)]
        m_ref[...] = jnp.full(m_ref.shape, NEG, F32)
        acc_ref[...] = jnp.zeros(acc_ref.shape, F32)
        slabs = strip // SUBLANES

        def scores(j, slot):
            rows_k = pl.ds(pl.multiple_of(j * t, LANES), t)
            for e in range(2):
                st = lax.dot_general(ka_ref[e, rows_k, :], qa[e], nt_dims, preferred_element_type=F32)
                s_refs[slot][e] = st.reshape(t // SUBLANES, SUBLANES, t)

        def tile(j, slot, diagonal):
            rows_k = pl.ds(pl.multiple_of(j * t, LANES), t)
            s_ref = s_refs[slot]
            for e in range(2):
                mx = jnp.full((SUBLANES, t), NEG, F32)
                for r in range(t // strip):
                    sl = slice(r * slabs, (r + 1) * slabs)
                    sv = s_ref[e,sl]
                    if diagonal:
                        shape = (slabs, SUBLANES, t)
                        key = (r * strip + lax.broadcasted_iota(jnp.int32, shape, 0) * SUBLANES
                               + lax.broadcasted_iota(jnp.int32, shape, 1))
                        sv = jnp.where(key <= lax.broadcasted_iota(jnp.int32, shape, 2), sv, NEG)
                        s_ref[e,sl] = sv
                    mx = jnp.maximum(mx, jnp.max(sv, axis=0))
                for sh in (4, 2, 1):
                    mx = jnp.maximum(mx, pltpu.roll(mx, sh, 0))
                m_old = m_ref[e]
                m_new = jnp.maximum(m_old, mx)
                alpha = jnp.exp(m_old - m_new)
                m_ref[e] = m_new
                for r in range(t // strip):
                    p = jnp.exp(s_ref[e,r * slabs:(r + 1) * slabs] - m_new[None])
                    p_ref[e, r * strip:(r + 1) * strip, :] = p.reshape(strip, t).astype(BF)
                acc = acc_ref[e].reshape(LANES // SUBLANES, SUBLANES, t) * alpha[None]
                acc_ref[e] = acc.reshape(LANES, t) + jnp.dot(vat_ref[e, :, rows_k], p_ref[e],
                                                             preferred_element_type=F32)

        _two_slot_pipeline(i, scores, tile)

        outs, lses = [], []
        for e in range(2):
            acc = acc_ref[e]
            other = HEAD_DIM * (1 - e)
            den = acc[other:other + 1, :]
            outs.append(acc / den)
            lses.append(jnp.broadcast_to(m_ref[e][0:1, :] + jnp.log(den), (LANES, t)))
        upper = lax.broadcasted_iota(jnp.int32, (LANES, t), 0) < HEAD_DIM
        o_ref[...] = jnp.where(upper, outs[0], outs[1]).T.astype(BF)
        l_ref[...] = jnp.where(upper, lses[0], lses[1]).T
        if ex is not None:
            @pl.when((pl.program_id(0) == nhp - 1) & (i == nq - 1))
            def _():
                ex.wait(*comm)

    own_scratch = [pltpu.VMEM((2, s, LANES), BF), pltpu.VMEM((2, LANES, s), BF),
                   pltpu.VMEM((2, t // SUBLANES, SUBLANES, t), F32),
                   pltpu.VMEM((2, t // SUBLANES, SUBLANES, t), F32), pltpu.VMEM((2, t, t), BF),
                   pltpu.VMEM((2, SUBLANES, t), F32), pltpu.VMEM((2, LANES, t), F32)]
    return pl.pallas_call(
        kern, name=name,
        out_shape=[jax.ShapeDtypeStruct((s, w), BF), jax.ShapeDtypeStruct((nhp, s, LANES), F32)]
        + (ex.out_shape if ex else []),
        grid=(nhp, nq),
        in_specs=[pl.BlockSpec((t, LANES), lambda h, i: (i, qb + h)),
                  pl.BlockSpec((s, LANES), lambda h, i: (0, kb + h)),
                  pl.BlockSpec((s, LANES), lambda h, i: (0, vb + h)),
                  pl.BlockSpec((t, LANES), lambda h, i: (i, h)),
                  pl.BlockSpec((s, LANES), lambda h, i: (0, h))] + (ex.specs if ex else []),
        out_specs=[pl.BlockSpec((t, LANES), lambda h, i: (i, h)),
                   pl.BlockSpec((None, t, LANES), lambda h, i: (h, i, 0))] + (ex.specs if ex else []),
        scratch_shapes=(ex.scratch if ex else []) + own_scratch,
        compiler_params=_params(dimension_semantics=("arbitrary", "arbitrary"),
                                has_side_effects=ex is not None),
    )(proj, proj, proj, cumx, cumx, *ex_arrays)


def _attn_bwd(proj, do, o, lse, cumx, qcol, *, name, ride=None):
    s = proj.shape[0]
    w = cumx.shape[1]
    nhp = w // LANES
    t = _fit(s, 256, LANES)
    nq = s // t
    strip = _fit(t, ATTN_STRIP, 16)
    scale = HEAD_DIM ** -0.5
    qb, kb, vb = qcol // LANES, (qcol + w) // LANES, (qcol + 2 * w) // LANES
    tn_dims = (((0,), (0,)), ((), ()))
    nt_dims = (((1,), (1,)), ((), ()))
    ex, ex_arrays = ride if ride is not None else (None, [])

    def kern(*refs):
        own_in, own_out, comm, scratch = _ride_split(ex, refs, 7, 5)
        q_ref, k_ref, v_ref, do_ref, o_ref, l_ref, cx_ref = own_in
        dq_ref, dk_ref, dv_ref, dkc_ref, dqc_ref = own_out
        qa_ref, da_ref, dqa_ref, dka_ref, dva_ref, st0_ref, st1_ref, dpt0_ref, dpt1_ref, pt_ref, dst_ref = scratch
        st_refs, dpt_refs = (st0_ref, st1_ref), (dpt0_ref, dpt1_ref)
        j = pl.program_id(1)
        if ex is not None:
            @pl.when((pl.program_id(0) == 0) & (j == 0))
            def _():
                ex.start(*comm)

        msks = _head_masks(t)

        @pl.when(j == 0)
        def _():
            def build(c, carry):
                rows = pl.ds(pl.multiple_of(c * t, LANES), t)
                q2 = q_ref[rows, :] * scale
                do2 = do_ref[rows, :]
                dd = do2 * o_ref[rows, :].astype(F32)
                delta = jnp.where(msks[0], jnp.sum(jnp.where(msks[0], dd, 0.0), axis=1, keepdims=True),
                                  jnp.sum(jnp.where(msks[1], dd, 0.0), axis=1, keepdims=True))
                bias = cx_ref[rows, :] - l_ref[rows, :]
                for e in range(2):
                    qa_ref[e, rows, :] = _augment(q2, bias, e, bias_slot=0, ones_slot=1)
                    da_ref[e, rows, :] = _augment(do2, -delta, e, bias_slot=0, ones_slot=None)
                return carry
            lax.fori_loop(0, nq, build, 0)
            dqa_ref[...] = jnp.zeros(dqa_ref.shape, F32)

        rows_k = pl.ds(pl.multiple_of(j * t, LANES), t)
        k2, v2 = k_ref[...], v_ref[...]
        ka = [_augment(k2, -cx_ref[rows_k, :], e, bias_slot=1, ones_slot=0) for e in range(2)]
        va = [_augment(v2, None, e, bias_slot=None, ones_slot=0) for e in range(2)]
        dka_ref[...] = jnp.zeros(dka_ref.shape, F32)
        dva_ref[...] = jnp.zeros(dva_ref.shape, F32)

        def scores(k, slot):
            rows_q = pl.ds(pl.multiple_of((nq - 1 - k) * t, LANES), t)
            for e in range(2):
                st_refs[slot][e] = lax.dot_general(ka[e], qa_ref[e, rows_q, :], nt_dims,
                                                   preferred_element_type=F32)
                dpt_refs[slot][e] = lax.dot_general(va[e], da_ref[e, rows_q, :], nt_dims,
                                                    preferred_element_type=F32)

        def tile(k, slot, diagonal):
            rows_q = pl.ds(pl.multiple_of((nq - 1 - k) * t, LANES), t)
            st_ref, dpt_ref = st_refs[slot], dpt_refs[slot]
            for e in range(2):
                for r in range(t // strip):
                    rows = slice(r * strip, (r + 1) * strip)
                    sv = st_ref[e, rows, :]
                    if diagonal:
                        key = r * strip + lax.broadcasted_iota(jnp.int32, (strip, t), 0)
                        qry = lax.broadcasted_iota(jnp.int32, (strip, t), 1)
                        sv = jnp.where(key <= qry, sv, NEG)
                    p = jnp.exp(sv)
                    pt_ref[e, rows, :] = p.astype(BF)
                    dst_ref[e, rows, :] = (p * dpt_ref[e, rows, :]).astype(BF)
            for e in range(2):
                dva_ref[e] += jnp.dot(pt_ref[e], da_ref[e, rows_q, :], preferred_element_type=F32)
                dka_ref[e] += jnp.dot(dst_ref[e], qa_ref[e, rows_q, :], preferred_element_type=F32)
                dqa_ref[e, rows_q, :] += lax.dot_general(dst_ref[e], ka[e], tn_dims, preferred_element_type=F32)

        _two_slot_pipeline(nq - 1 - j, scores, tile)

        dk_ref[...] = jnp.where(msks[0], dka_ref[0], dka_ref[1])
        dv_ref[...] = jnp.where(msks[0], dva_ref[0], dva_ref[1])
        sums = jnp.where(msks[1], dka_ref[0], dka_ref[1]).T
        dkc_ref[0:1, :] = sums[HEAD_DIM + BIAS_LANES:HEAD_DIM + BIAS_LANES + 1, :]
        dkc_ref[1:2, :] = sums[BIAS_LANES:BIAS_LANES + 1, :]

        @pl.when(j == nq - 1)
        def _():
            def flush(c, carry):
                rows = pl.ds(pl.multiple_of(c * t, LANES), t)
                a0, a1 = dqa_ref[0, rows, :], dqa_ref[1, rows, :]
                dq_ref[rows, :] = jnp.where(msks[0], a0, a1) * scale
                sums = jnp.where(msks[1], a0, a1).T
                dqc_ref[0:1, rows] = sums[HEAD_DIM:HEAD_DIM + 1, :]
                dqc_ref[1:2, rows] = sums[0:1, :]
                return carry
            lax.fori_loop(0, nq, flush, 0)

        if ex is not None:
            @pl.when((pl.program_id(0) == nhp - 1) & (j == nq - 1))
            def _():
                ex.wait(*comm)

    full = lambda cb: pl.BlockSpec((s, LANES), lambda h, j: (0, cb + h))
    blk = lambda cb: pl.BlockSpec((t, LANES), lambda h, j: (j, cb + h))
    own_scratch = [pltpu.VMEM((2, s, LANES), BF), pltpu.VMEM((2, s, LANES), BF), pltpu.VMEM((2, s, LANES), F32),
                   pltpu.VMEM((2, t, LANES), F32), pltpu.VMEM((2, t, LANES), F32),
                   pltpu.VMEM((2, t, t), F32), pltpu.VMEM((2, t, t), F32),
                   pltpu.VMEM((2, t, t), F32), pltpu.VMEM((2, t, t), F32),
                   pltpu.VMEM((2, t, t), BF), pltpu.VMEM((2, t, t), BF)]
    return pl.pallas_call(
        kern, name=name,
        out_shape=[jax.ShapeDtypeStruct((s, w), F32)] * 3 + [jax.ShapeDtypeStruct((nhp, 2, s), F32)] * 2
        + (ex.out_shape if ex else []),
        grid=(nhp, nq),
        in_specs=[full(qb), blk(kb), blk(vb), full(0), full(0),
                  pl.BlockSpec((None, s, LANES), lambda h, j: (h, 0, 0)), full(0)] + (ex.specs if ex else []),
        out_specs=[full(0), blk(0), blk(0), pl.BlockSpec((None, 2, t), lambda h, j: (h, 0, j)),
                   pl.BlockSpec((None, 2, s), lambda h, j: (h, 0, 0))] + (ex.specs if ex else []),
        scratch_shapes=(ex.scratch if ex else []) + own_scratch,
        compiler_params=_params(dimension_semantics=("arbitrary", "arbitrary"),
                                has_side_effects=ex is not None),
    )(proj, proj, proj, do, o, lse, cumx, *ex_arrays)


S5_STATES = 256
S5_ROWS = 512


def _cmul(ar, ai, br, bi):
    return ar * br - ai * bi, ar * bi + ai * br


def _scan_tables(lr, li, reverse):
    w = lr.shape[1]
    row = lax.broadcasted_iota(jnp.int32, (SUBLANES, w), 0)
    if reverse:
        row = SUBLANES - 1 - row
    lr1, li1 = jnp.broadcast_to(lr, (SUBLANES, w)), jnp.broadcast_to(li, (SUBLANES, w))
    lr2, li2 = _cmul(lr1, li1, lr1, li1)
    lr4, li4 = _cmul(lr2, li2, lr2, li2)
    steps = []
    for d, (pr, pi) in zip((1, 2, 4), ((lr1, li1), (lr2, li2), (lr4, li4))):
        keep = row >= d
        steps.append((jnp.where(keep, pr, 0.0), jnp.where(keep, pi, 0.0)))
    cr, ci = lr1, li1
    for bit, (pr, pi) in zip((1, 2, 4), ((lr1, li1), (lr2, li2), (lr4, li4))):
        nr, ni = _cmul(cr, ci, pr, pi)
        has = (row & bit) != 0
        cr, ci = jnp.where(has, nr, cr), jnp.where(has, ni, ci)
    return steps, (cr, ci)


def _scan_local(xr, xi, steps, reverse):
    for d, (pr, pi) in zip((1, 2, 4), steps):
        sh = (SUBLANES - d) if reverse else d
        sr, si = pltpu.roll(xr, sh, 0), pltpu.roll(xi, sh, 0)
        xr, xi = xr + (pr * sr - pi * si), xi + (pr * si + pi * sr)
    return xr, xi


def _scan_carry(xr, xi, car_r, car_i, carry_pow):
    cr, ci = carry_pow
    return xr + (cr * car_r - ci * car_i), xi + (cr * car_i + ci * car_r)


SCAN_UNROLL = 4


def _s5_specs(s, ncb):
    u_spec = pl.BlockSpec((s, LANES), lambda cb, hf: (0, cb))
    wb_spec = pl.BlockSpec((None, None, LANES, S5_STATES), lambda cb, hf: (cb, hf, 0, 0))
    wc_spec = pl.BlockSpec((None, None, S5_STATES, LANES), lambda cb, hf: (cb, hf, 0, 0))
    lam_spec = pl.BlockSpec((1, S5_STATES), lambda cb, hf: (0, 2 * cb + hf))
    d_spec = pl.BlockSpec((1, LANES), lambda cb, hf: (0, cb))
    return u_spec, wb_spec, wc_spec, lam_spec, d_spec


def _s5_project_and_scan(u_ref, wbr_ref, wbi_ref, lr_ref, li_ref, xr_ref, xi_ref, s, rows):
    wbr, wbi = wbr_ref[...], wbi_ref[...]
    for r in range(s // rows):
        sl = pl.ds(r * rows, rows)
        ub = u_ref[sl, :].astype(BF)
        xr_ref[sl, :] = jnp.dot(ub, wbr, preferred_element_type=F32)
        xi_ref[sl, :] = jnp.dot(ub, wbi, preferred_element_type=F32)
    steps, cpow = _scan_tables(lr_ref[...], li_ref[...], False)

    unroll = _fit(s // SUBLANES, SCAN_UNROLL, 1)

    def body(b, carry):
        car_r, car_i = carry
        sls = [pl.ds(pl.multiple_of((b * unroll + q) * SUBLANES, SUBLANES), SUBLANES) for q in range(unroll)]
        blocks = [_scan_local(xr_ref[sl, :], xi_ref[sl, :], steps, False) for sl in sls]
        for sl, (xr, xi) in zip(sls, blocks):
            xr, xi = _scan_carry(xr, xi, car_r, car_i, cpow)
            xr_ref[sl, :] = xr
            xi_ref[sl, :] = xi
            car_r, car_i = xr[SUBLANES - 1:SUBLANES, :], xi[SUBLANES - 1:SUBLANES, :]
        return car_r, car_i

    zero = jnp.zeros((1, S5_STATES), F32)
    lax.fori_loop(0, s // SUBLANES // unroll, body, (zero, zero))


def _s5_fwd(proj, wb_re, wb_im, wc_re, wc_im, lam_re, lam_im, dskip, *, name):
    s = proj.shape[0]
    w = dskip.shape[1]
    ncb = w // LANES
    rows = _fit(s, S5_ROWS, SUBLANES)

    def kern(u_ref, wbr_ref, wbi_ref, wcr_ref, wci_ref, lr_ref, li_ref, d_ref, y_ref, xr_ref, xi_ref):
        hf = pl.program_id(1)
        _s5_project_and_scan(u_ref, wbr_ref, wbi_ref, lr_ref, li_ref, xr_ref, xi_ref, s, rows)
        wcr, wci = wcr_ref[...], wci_ref[...]
        for r in range(s // rows):
            sl = pl.ds(r * rows, rows)
            y = (jnp.dot(xr_ref[sl, :].astype(BF), wcr, preferred_element_type=F32)
                 - jnp.dot(xi_ref[sl, :].astype(BF), wci, preferred_element_type=F32))

            @pl.when(hf == 0)
            def _(y=y, sl=sl):
                y_ref[sl, :] = y + d_ref[...] * u_ref[sl, :]

            @pl.when(hf == 1)
            def _(y=y, sl=sl):
                y_ref[sl, :] += y

    u_spec, wb_spec, wc_spec, lam_spec, d_spec = _s5_specs(s, ncb)
    return pl.pallas_call(
        kern, name=name, out_shape=jax.ShapeDtypeStruct((s, w), F32), grid=(ncb, 2),
        in_specs=[u_spec, wb_spec, wb_spec, wc_spec, wc_spec, lam_spec, lam_spec, d_spec],
        out_specs=u_spec,
        scratch_shapes=[pltpu.VMEM((s, S5_STATES), F32), pltpu.VMEM((s, S5_STATES), F32)],
        compiler_params=_params(dimension_semantics=("parallel", "arbitrary")),
    )(proj, wb_re, wb_im, wc_re, wc_im, lam_re, lam_im, dskip)


def _s5_bwd(proj, dy, wb_re, wb_im, wc_re, wc_im, lam_re, lam_im, dskip, *, name):
    s = proj.shape[0]
    w = dskip.shape[1]
    ncb = w // LANES
    rows = _fit(s, S5_ROWS, SUBLANES)
    tn_dims = (((0,), (0,)), ((), ()))
    nt_dims = (((1,), (1,)), ((), ()))

    def kern(u_ref, dy_ref, wbr_ref, wbi_ref, wcr_ref, wci_ref, lr_ref, li_ref, d_ref,
             du_ref, dwbr_ref, dwbi_ref, dwcr_ref, dwci_ref, dlr_ref, dli_ref, dd_ref,
             xr_ref, xi_ref, gr_ref, gi_ref):
        hf = pl.program_id(1)
        _s5_project_and_scan(u_ref, wbr_ref, wbi_ref, lr_ref, li_ref, xr_ref, xi_ref, s, rows)

        wcr, wci = wcr_ref[...], wci_ref[...]
        dwcr = jnp.zeros((S5_STATES, LANES), F32)
        dwci = jnp.zeros((S5_STATES, LANES), F32)
        ddsk = jnp.zeros((1, LANES), F32)
        for r in range(s // rows):
            sl = pl.ds(r * rows, rows)
            dyf = dy_ref[sl, :]
            dyb = dyf.astype(BF)
            gr_ref[sl, :] = lax.dot_general(dyb, wcr, nt_dims, preferred_element_type=F32)
            gi_ref[sl, :] = -lax.dot_general(dyb, wci, nt_dims, preferred_element_type=F32)
            dwcr = dwcr + lax.dot_general(xr_ref[sl, :].astype(BF), dyb, tn_dims, preferred_element_type=F32)
            dwci = dwci - lax.dot_general(xi_ref[sl, :].astype(BF), dyb, tn_dims, preferred_element_type=F32)
            ddsk = ddsk + jnp.sum(dyf * u_ref[sl, :], axis=0, keepdims=True)
        dwcr_ref[...] = dwcr
        dwci_ref[...] = dwci

        @pl.when(hf == 0)
        def _():
            dd_ref[...] = ddsk

        steps, cpow = _scan_tables(lr_ref[...], -li_ref[...], True)
        row = lax.broadcasted_iota(jnp.int32, (SUBLANES, S5_STATES), 0)
        nblk = s // SUBLANES

        unroll = _fit(nblk, SCAN_UNROLL, 1)

        def body(k, carry):
            car_r, car_i, ar, ai = carry
            sls = [pl.ds(pl.multiple_of((nblk - 1 - k * unroll - q) * SUBLANES, SUBLANES), SUBLANES)
                   for q in range(unroll)]
            blocks = [_scan_local(gr_ref[sl, :], gi_ref[sl, :], steps, True) for sl in sls]
            for sl, (g_r, g_i) in zip(sls, blocks):
                g_r, g_i = _scan_carry(g_r, g_i, car_r, car_i, cpow)
                gr_ref[sl, :] = g_r
                gi_ref[sl, :] = g_i
                nr = jnp.where(row == SUBLANES - 1, car_r, pltpu.roll(g_r, SUBLANES - 1, 0))
                ni = jnp.where(row == SUBLANES - 1, car_i, pltpu.roll(g_i, SUBLANES - 1, 0))
                xr, xi = xr_ref[sl, :], xi_ref[sl, :]
                ar = ar + (xr * nr + xi * ni)
                ai = ai + (xr * ni - xi * nr)
                car_r, car_i = g_r[0:1, :], g_i[0:1, :]
            return car_r, car_i, ar, ai

        zero = jnp.zeros((1, S5_STATES), F32)
        zacc = jnp.zeros((SUBLANES, S5_STATES), F32)
        _, _, ar, ai = lax.fori_loop(0, nblk // unroll, body, (zero, zero, zacc, zacc))
        dlr_ref[...] = jnp.sum(ar, axis=0, keepdims=True)
        dli_ref[...] = jnp.sum(ai, axis=0, keepdims=True)

        wbr, wbi = wbr_ref[...], wbi_ref[...]
        dwbr = jnp.zeros((LANES, S5_STATES), F32)
        dwbi = jnp.zeros((LANES, S5_STATES), F32)
        for r in range(s // rows):
            sl = pl.ds(r * rows, rows)
            grb, gib = gr_ref[sl, :].astype(BF), gi_ref[sl, :].astype(BF)
            ub = u_ref[sl, :].astype(BF)
            dwbr = dwbr + lax.dot_general(ub, grb, tn_dims, preferred_element_type=F32)
            dwbi = dwbi + lax.dot_general(ub, gib, tn_dims, preferred_element_type=F32)
            du = (lax.dot_general(grb, wbr, nt_dims, preferred_element_type=F32)
                  + lax.dot_general(gib, wbi, nt_dims, preferred_element_type=F32))

            @pl.when(hf == 0)
            def _(du=du, sl=sl):
                du_ref[sl, :] = du + d_ref[...] * dy_ref[sl, :]

            @pl.when(hf == 1)
            def _(du=du, sl=sl):
                du_ref[sl, :] += du
        dwbr_ref[...] = dwbr
        dwbi_ref[...] = dwbi

    u_spec, wb_spec, wc_spec, lam_spec, d_spec = _s5_specs(s, ncb)
    dwb_spec = pl.BlockSpec((None, None, LANES, S5_STATES), lambda cb, hf: (cb, hf, 0, 0))
    dwc_spec = pl.BlockSpec((None, None, S5_STATES, LANES), lambda cb, hf: (cb, hf, 0, 0))
    state = pltpu.VMEM((s, S5_STATES), F32)
    return pl.pallas_call(
        kern, name=name,
        out_shape=[jax.ShapeDtypeStruct((s, w), F32),
                   jax.ShapeDtypeStruct((ncb, 2, LANES, S5_STATES), F32), jax.ShapeDtypeStruct((ncb, 2, LANES, S5_STATES), F32),
                   jax.ShapeDtypeStruct((ncb, 2, S5_STATES, LANES), F32), jax.ShapeDtypeStruct((ncb, 2, S5_STATES, LANES), F32),
                   jax.ShapeDtypeStruct((1, 4 * w), F32), jax.ShapeDtypeStruct((1, 4 * w), F32),
                   jax.ShapeDtypeStruct((1, w), F32)],
        grid=(ncb, 2),
        in_specs=[u_spec, u_spec, wb_spec, wb_spec, wc_spec, wc_spec, lam_spec, lam_spec, d_spec],
        out_specs=[u_spec, dwb_spec, dwb_spec, dwc_spec, dwc_spec, lam_spec, lam_spec, d_spec],
        scratch_shapes=[state, state, state, state],
        compiler_params=_params(dimension_semantics=("parallel", "arbitrary")),
    )(proj, dy, wb_re, wb_im, wc_re, wc_im, lam_re, lam_im, dskip)


def _s5_discretise(lam_re, lam_im, log_dt, b_re, b_im):
    lr = jnp.minimum(lam_re, -EIG_CLIP)
    li = lam_im
    dt = jnp.exp(log_dt)[:, None]
    mag = jnp.exp(lr * dt)
    lbr, lbi = mag * jnp.cos(li * dt), mag * jnp.sin(li * dt)
    den = lr * lr + li * li
    fr = ((lbr - 1.0) * lr + lbi * li) / den
    fi = (lbi * lr - (lbr - 1.0) * li) / den
    bbr = fr[..., None] * b_re - fi[..., None] * b_im
    bbi = fr[..., None] * b_im + fi[..., None] * b_re
    return lbr, lbi, bbr, bbi


def _s5_operand(mats, channels_first):
    g, a, b = mats.shape
    gl = LANES // 2 // SSM_H
    ncb = g // (2 * gl)
    m = mats.reshape(ncb, 2, gl, a, b)
    eye = jnp.eye(gl, dtype=mats.dtype)
    inner = (m[:, :, :, :, None, :] * eye[None, None, :, None, :, None]).reshape(ncb, 2, gl * a, gl * b)
    zeros = jnp.zeros_like(inner[:, 0])
    axis = 1 if channels_first else 2
    return jnp.stack([jnp.concatenate([inner[:, 0], zeros], axis=axis),
                      jnp.concatenate([zeros, inner[:, 1]], axis=axis)], axis=1)


def _s5_block_grads(dwb, a, b, transpose):
    ncb = dwb.shape[0]
    gl = LANES // 2 // (a if not transpose else b)
    if not transpose:
        d = dwb.reshape(ncb, 2, 2, gl, a, gl, b)
        parts = [[d[:, hf, hf, g, :, g, :] for g in range(gl)] for hf in range(2)]
    else:
        d = dwb.reshape(ncb, 2, gl, a, 2, gl, b)
        parts = [[d[:, hf, g, :, hf, g, :] for g in range(gl)] for hf in range(2)]
    st = jnp.stack([jnp.stack(p, axis=1) for p in parts], axis=1)
    return st.reshape(ncb * 2 * gl, a, b)


def _adamw(parts, w, m, v, *, name):
    depth, r, c = w.shape
    assert len(parts) == depth
    npart = parts[0].shape[0]
    row_bytes = 4 * (-(-c // LANES) * LANES)
    align = 16 if parts[0].dtype == BF else SUBLANES
    budget = VMEM_LIMIT // 2 // (2 * (depth * npart + 7) * row_bytes)
    tr = _fit(r, max(align, budget // align * align), align)
    nr = r // tr
    c1 = 1.0 / (1.0 - ADAM_B1 ** ADAM_STEP)
    c2 = 1.0 / (1.0 - ADAM_B2 ** ADAM_STEP)

    def kern(*refs):
        p_refs = refs[:depth]
        w_ref, m_ref, v_ref, g_ref, d_ref, nm_ref, nv_ref = refs[depth:]
        layer = pl.program_id(0)
        for l in range(depth):
            @pl.when(layer == l)
            def _(p_ref=p_refs[l]):
                g = p_ref[0].astype(F32)
                for q in range(1, npart):
                    g = g + p_ref[q].astype(F32)
                m2 = ADAM_B1 * m_ref[...] + (1.0 - ADAM_B1) * g
                v2 = ADAM_B2 * v_ref[...] + (1.0 - ADAM_B2) * (g * g)
                upd = (m2 * c1) / (jnp.sqrt(v2 * c2) + ADAM_EPS) + ADAM_WD * w_ref[...]
                g_ref[...] = g
                d_ref[...] = -ADAM_LR * upd
                nm_ref[...] = m2
                nv_ref[...] = v2

    def part_spec(l):
        return pl.BlockSpec((npart, tr, c),
                            lambda ly, i: (0, jnp.where(ly == l, i, jnp.where(ly < l, 0, nr - 1)), 0))

    spec = pl.BlockSpec((None, tr, c), lambda ly, i: (ly, i, 0))
    return pl.pallas_call(
        kern, name=name, out_shape=[jax.ShapeDtypeStruct((depth, r, c), F32)] * 4, grid=(depth, nr),
        in_specs=[part_spec(l) for l in range(depth)] + [spec, spec, spec],
        out_specs=[spec] * 4,
        compiler_params=_params(dimension_semantics=("arbitrary", "arbitrary")),
    )(*parts, w, m, v)


def _sum_parts(parts, *, name):
    npart, r, c = parts.shape

    def kern(p_ref, o_ref):
        g = p_ref[0]
        for q in range(1, npart):
            g = g + p_ref[q]
        o_ref[...] = g

    return pl.pallas_call(kern, name=name, out_shape=jax.ShapeDtypeStruct((r, c), F32), compiler_params=_params())(parts)


class _Exchange:
    def __init__(self, arrays, gather):
        self.n = len(arrays)
        self.gather = gather
        self.out_shape = [jax.ShapeDtypeStruct(((NDEV,) + a.shape) if gather else a.shape, a.dtype) for a in arrays]
        self.scratch = [pltpu.SemaphoreType.DMA((self.n, NDEV - 1)), pltpu.SemaphoreType.DMA((self.n, NDEV - 1)),
                        pltpu.SemaphoreType.DMA((self.n,))]
        self.specs = [pl.BlockSpec(memory_space=pl.ANY)] * self.n

    def _copies(self, srcs, dsts, sems):
        send_sems, recv_sems, local_sems = sems
        x, y, c = lax.axis_index("x"), lax.axis_index("y"), lax.axis_index("c")
        me = 4 * x + 2 * y + c
        local = [pltpu.make_async_copy(srcs[a] if self.gather else srcs[a].at[me], dsts[a].at[me], local_sems.at[a])
                 for a in range(self.n)]
        remote = []
        for k in (1, 2, 4, 3, 5, 6, 7):
            px, py, pc = x ^ ((k >> 2) & 1), y ^ ((k >> 1) & 1), c ^ (k & 1)
            peer = 4 * px + 2 * py + pc
            for a in range(self.n):
                src = srcs[a] if self.gather else srcs[a].at[peer]
                mk = functools.partial(
                    pltpu.make_async_remote_copy, src_ref=src,
                    send_sem=send_sems.at[a, k - 1], recv_sem=recv_sems.at[a, k - 1],
                    device_id=(px, py, pc), device_id_type=pl.DeviceIdType.MESH)
                remote.append((mk(dst_ref=dsts[a].at[me]), mk(dst_ref=dsts[a].at[peer])))
        return local, remote

    def _gather_copies(self, srcs, dsts, sems):
        send_sems, recv_sems, local_sems = sems
        x, y, c = lax.axis_index("x"), lax.axis_index("y"), lax.axis_index("c")
        block = lambda px, py, pc: 4 * px + 2 * py + pc
        me = block(x, y, c)
        chips = [(1 - x, y), (x, 1 - y), (1 - x, 1 - y)]
        local = [pltpu.make_async_copy(srcs[a], dsts[a].at[me], local_sems.at[a]) for a in range(self.n)]
        own, passed = [], []
        for a in range(self.n):
            def copy(k, blk, to, src=None, a=a):
                return pltpu.make_async_remote_copy(
                    src_ref=dsts[a].at[blk] if src is None else src, dst_ref=dsts[a].at[blk],
                    send_sem=send_sems.at[a, k], recv_sem=recv_sems.at[a, k],
                    device_id=to, device_id_type=pl.DeviceIdType.MESH)
            sib = (x, y, 1 - c)
            own.append((copy(0, me, sib, srcs[a]), copy(0, block(x, y, 1 - c), sib)))
            for j, (px, py) in enumerate(chips):
                own.append((copy(1 + j, me, (px, py, c), srcs[a]), copy(1 + j, block(px, py, c), (px, py, c))))
            for j, (px, py) in enumerate(chips):
                passed.append((copy(4 + j, block(px, py, c), sib), copy(4 + j, block(px, py, 1 - c), sib)))
        return local, own, passed

    def start(self, srcs, dsts, sems):
        if self.gather:
            local, own, _ = self._gather_copies(srcs, dsts, sems)
            for cp in local:
                cp.start()
            for send, _ in own:
                send.start()
            return
        local, remote = self._copies(srcs, dsts, sems)
        for cp in local:
            cp.start()
        for send, _ in remote:
            send.start()

    def forward(self, srcs, dsts, sems):
        if not self.gather:
            return
        _, own, passed = self._gather_copies(srcs, dsts, sems)
        for a in range(self.n):
            for j in range(3):
                own[4 * a + 1 + j][1].wait_recv()
                passed[3 * a + j][0].start()

    def wait(self, srcs, dsts, sems):
        if self.gather:
            local, own, passed = self._gather_copies(srcs, dsts, sems)
            for a in range(self.n):
                own[4 * a][1].wait_recv()
            for _, arrival in passed:
                arrival.wait_recv()
            for send, _ in own + passed:
                send.wait_send()
            for cp in local:
                cp.wait()
            return
        local, remote = self._copies(srcs, dsts, sems)
        for send, arrival in remote:
            send.wait_send()
            arrival.wait_recv()
        for cp in local:
            cp.wait()


def _exchange(arrays, gather, *, name):
    ex = _Exchange(arrays, gather)
    n = ex.n

    def kern(*refs):
        srcs, dsts, sems = refs[:n], refs[n:2 * n], refs[2 * n:]
        ex.start(srcs, dsts, sems)
        ex.forward(srcs, dsts, sems)
        ex.wait(srcs, dsts, sems)

    return pl.pallas_call(
        kern, name=name, out_shape=ex.out_shape, in_specs=ex.specs, out_specs=ex.specs, scratch_shapes=ex.scratch,
        compiler_params=pltpu.CompilerParams(has_side_effects=True),
    )(*arrays)


def _pack(arrays):
    flat = jnp.concatenate([a.reshape(-1).astype(F32) for a in arrays])
    pad = (-flat.shape[0]) % (SUBLANES * LANES)
    return jnp.pad(flat, (0, pad)).reshape(-1, LANES)


def _unpack(buf, like):
    flat = buf.reshape(-1)
    out, off = [], 0
    for a in like:
        sz = math.prod(a.shape)
        out.append(flat[off:off + sz].reshape(a.shape))
        off += sz
    return out


def _row(v):
    return v.reshape(1, -1)


def _layer_fwd(x, mod, p, l, ride=None, on_receive=None):
    s, d = x.shape
    sw = d // 2
    nh = d // LANES
    shift_m, scale_m, gate_m, shift_f, scale_f, gate_f = mod
    n = lambda tag: f"{tag}{l}"
    sv = {}

    h1, = _rowwise(lambda xv, g, sc, sh: (xv * _rms(xv) * g) * (1.0 + sc) + sh,
                   [x], [p['g_pre_mix'], scale_m, shift_m], [(d, BF, 'tile')], name=n("pre_mix"))
    proj_a = _mm(h1, p['w_in_a'], name=n("proj_a"))
    flog = _mm(h1, p['w_in_f'], name=n("proj_f"))
    gates = _mm(h1, p['w_in_g'], name=n("proj_g"))

    y_s5 = _s5_fwd(proj_a, p['wb_re'], p['wb_im'], p['wc_re'], p['wc_im'], p['lamb_re'], p['lamb_im'], p['d_skip'],
                   name=n("s5_fwd"))
    z, = _rowwise(_gelu, [y_s5], [], [(sw, BF, 'tile')], name=n("gelu"))
    tglu = _mm(z, p['w_glu'], name=n("glu_mm"))
    ys, = _rowwise(lambda yv, tv, b: _gelu(yv) * _sigmoid(tv + b), [y_s5, tglu], [p['b_glu']], [(sw, BF, 'tile')],
                   name=n("glu"))

    cumx = _cum_fwd(flog, p['b_f_row'], nh, name=n("cum_fwd"))
    ya, lse, *received = _attn_fwd(proj_a, cumx, sw, name=n("attn_fwd"), ride=ride)
    if on_receive is not None:
        on_receive(received)

    am = _mm(ys, p['w_pa'], name=n("pa_mm"))
    bm = _mm(ya, p['w_pb'], name=n("pb_mm"))
    merged, = _rowwise(lambda a, b, ga, gb: _sigmoid(ga) * a + _sigmoid(gb) * b,
                       [am, bm, (gates, d, 0), (gates, d, 1)], [], [(d, BF, 'tile')], name=n("merge"))
    ym = _mm(merged, p['w_o'], name=n("o_mm"))
    x2, = _rowwise(lambda xv, yv, g, gt: xv + gt * (yv * _rms(yv) * g),
                   [x, ym], [p['g_post_mix'], gate_m], [(d, F32, 'tile')], name=n("post_mix"))

    h2, = _rowwise(lambda xv, g, sc, sh: (xv * _rms(xv) * g) * (1.0 + sc) + sh,
                   [x2], [p['g_pre_ffn'], scale_f, shift_f], [(d, BF, 'tile')], name=n("pre_ffn"))
    gt = _mm(h2, p['w_ffn_gate'], name=n("gate_mm"))
    up = _mm(h2, p['w_ffn_up'], name=n("up_mm"))
    dff = gt.shape[1]
    act, = _rowwise(lambda g, u: _silu(g) * u, [gt, up], [], [(dff, BF, 'tile')], name=n("swiglu"))
    yf = _mm(act, p['w_ffn_down'], name=n("down_mm"))
    x3, = _rowwise(lambda xv, yv, g, gt_: xv + gt_ * (yv * _rms(yv) * g),
                   [x2, yf], [p['g_post_ffn'], gate_f], [(d, F32, 'tile')], name=n("post_ffn"))

    sv.update(x=x, h1=h1, proj_a=proj_a, flog=flog, gates=gates, y_s5=y_s5, z=z, tglu=tglu, ys=ys, cumx=cumx,
              ya=ya, lse=lse, am=am, bm=bm, merged=merged, ym=ym, x2=x2, h2=h2, gt=gt, up=up,
              act=act, yf=yf)
    return x3, sv


def _layer_bwd(dx3, sv, mod, p, l, make_ride=None, on_receive=None):
    x, x2 = sv['x'], sv['x2']
    s, d = x.shape
    sw = d // 2
    nh = d // LANES
    shift_m, scale_m, gate_m, shift_f, scale_f, gate_f = mod
    n = lambda tag: f"{tag}{l}"
    gw, gs = {}, {}

    def post_bwd(dxo, yv, g, gate):
        r = _rms(yv)
        nf = yv * r
        dn = dxo * gate * g
        return _norm_bwd(dn, nf, r), dxo * (nf * g), dxo * gate * nf

    def pre_bwd(dh, dres, xv, g, sc):
        r = _rms(xv)
        xh = xv * r
        n3 = xh * g
        dn3 = dh * (1.0 + sc)
        return dres + _norm_bwd(dn3 * g, xh, r), dh, dh * n3, dn3 * xh

    dyf, dgate_f, gs['g_post_ffn'] = _rowwise(
        post_bwd, [dx3, sv['yf']], [p['g_post_ffn'], gate_f],
        [(d, BF, 'tile'), (d, F32, 'sum'), (d, F32, 'sum')], name=n("post_ffn_bwd"))
    dff = sv['gt'].shape[1]
    dact = _mm(dyf, p['w_ffn_down'], tb=True, name=n("down_bwd_x"))
    gw['w_ffn_down'] = _mm(sv['act'], dyf, ta=True, out_dtype=BF, name=n("down_bwd_w"))

    def swiglu_bwd(da, g, u):
        sg = _sigmoid(g)
        return da * u * (sg * (1.0 + g * (1.0 - sg))), da * (g * sg)

    dgt, dup = _rowwise(swiglu_bwd, [dact, sv['gt'], sv['up']], [], [(dff, BF, 'tile'), (dff, BF, 'tile')],
                        name=n("swiglu_bwd"))
    dh2a = _mm(dgt, p['w_ffn_gate'], tb=True, name=n("gate_bwd_x"))
    dh2b = _mm(dup, p['w_ffn_up'], tb=True, name=n("up_bwd_x"))
    gw['w_ffn_gate'] = _mm(sv['h2'], dgt, ta=True, out_dtype=BF, name=n("gate_bwd_w"))
    gw['w_ffn_up'] = _mm(sv['h2'], dup, ta=True, out_dtype=BF, name=n("up_bwd_w"))
    dx2, dshift_f, dscale_f, gs['g_pre_ffn'] = _rowwise(
        lambda da, db, dres, xv, g, sc: pre_bwd(da + db, dres, xv, g, sc),
        [dh2a, dh2b, dx3, x2], [p['g_pre_ffn'], scale_f],
        [(d, F32, 'tile'), (d, F32, 'sum'), (d, F32, 'sum'), (d, F32, 'sum')], name=n("pre_ffn_bwd"))

    dym, dgate_m, gs['g_post_mix'] = _rowwise(
        post_bwd, [dx2, sv['ym']], [p['g_post_mix'], gate_m],
        [(d, BF, 'tile'), (d, F32, 'sum'), (d, F32, 'sum')], name=n("post_mix_bwd"))
    dmerged = _mm(dym, p['w_o'], tb=True, name=n("o_bwd_x"))
    gw['w_o'] = _mm(sv['merged'], dym, ta=True, out_dtype=BF, name=n("o_bwd_w"))

    def merge_bwd(dm, a, b, ga, gb):
        sa, sb = _sigmoid(ga), _sigmoid(gb)
        return dm * sa, dm * sb, dm * a * sa * (1.0 - sa), dm * b * sb * (1.0 - sb)

    da_, db_, dga, dgb = _rowwise(
        merge_bwd, [dmerged, sv['am'], sv['bm'], (sv['gates'], d, 0), (sv['gates'], d, 1)], [],
        [(d, BF, 'tile')] * 4, name=n("merge_bwd"))
    dys = _mm(da_, p['w_pa'], tb=True, name=n("pa_bwd_x"))
    gw['w_pa'] = _mm(sv['ys'], da_, ta=True, out_dtype=BF, name=n("pa_bwd_w"))
    dya = _mm(db_, p['w_pb'], tb=True, name=n("pb_bwd_x"))
    gw['w_pb'] = _mm(sv['ya'], db_, ta=True, out_dtype=BF, name=n("pb_bwd_w"))

    sent = list(gw)
    dq, dk, dv, dkc, dqc, *received = _attn_bwd(
        sv['proj_a'], dya, sv['ya'], sv['lse'], sv['cumx'], sw, name=n("attn_bwd"),
        ride=make_ride({k: gw[k] for k in sent}) if make_ride is not None else None)
    if on_receive is not None:
        on_receive(sent, received)
    frow = sv['flog'][:, :nh].T
    dcum = jnp.stack([-dkc.reshape(nh, s), dqc.reshape(nh, s)])
    dfrow, dbf = _cum_bwd(dcum, frow, p['b_f_col'], name=n("cum_bwd"))
    gs['b_f'] = dbf.reshape(nh)
    dflog = jnp.pad(dfrow.T, ((0, 0), (0, LANES - nh))).astype(BF)

    def glu_bwd(dy_, yv, tv, b):
        zv = _gelu(yv)
        sg = _sigmoid(tv + b)
        dt = dy_ * zv * sg * (1.0 - sg)
        return dt, dy_ * sg, dt

    dt, dz1, gs['b_glu'] = _rowwise(glu_bwd, [dys, sv['y_s5'], sv['tglu']], [p['b_glu']],
                                    [(sw, BF, 'tile'), (sw, F32, 'tile'), (sw, F32, 'sum')], name=n("glu_bwd"))
    dz2 = _mm(dt, p['w_glu'], tb=True, name=n("glu_bwd_x"))
    gw['w_glu'] = _mm(sv['z'], dt, ta=True, out_dtype=BF, name=n("glu_bwd_w"))
    dy_s5, = _rowwise(lambda a, b, yv: (a + b) * _gelu_grad(yv), [dz1, dz2, sv['y_s5']], [], [(sw, F32, 'tile')],
                      name=n("gelu_bwd"))
    du, dwbr, dwbi, dwcr, dwci, dlr, dli, gs['d_skip'] = _s5_bwd(
        sv['proj_a'], dy_s5, p['wb_re'], p['wb_im'], p['wc_re'], p['wc_im'], p['lamb_re'], p['lamb_im'], p['d_skip'],
        name=n("s5_bwd"))
    g_ = sw // SSM_H
    pst = p['lamb_re'].shape[1] // g_
    gs['lamb_re'], gs['lamb_im'] = dlr.reshape(g_, pst), dli.reshape(g_, pst)
    gs['bbar_re'] = _s5_block_grads(dwbr, SSM_H, pst, False).transpose(0, 2, 1)
    gs['bbar_im'] = _s5_block_grads(dwbi, SSM_H, pst, False).transpose(0, 2, 1)
    gs['c_re'] = _s5_block_grads(dwcr, pst, SSM_H, True).transpose(0, 2, 1)
    gs['c_im'] = _s5_block_grads(dwci, pst, SSM_H, True).transpose(0, 2, 1)

    dproj = jnp.concatenate([du.astype(BF), dq.astype(BF), dk.astype(BF), dv.astype(BF), dflog, dga, dgb], axis=1)
    gw['w_in'] = _mm(sv['h1'], dproj, ta=True, out_dtype=BF, name=n("proj_bwd_w"))
    if make_ride is not None:
        tail = [k for k in gw if k not in sent]
        dh1, received = _mm(dproj, p['w_in_all'], tb=True, tk=1408, name=n("proj_bwd_x"),
                            ride=make_ride({k: gw[k] for k in tail}))
        on_receive(tail, received)
        gw = {}
    else:
        dh1 = _mm(dproj, p['w_in_all'], tb=True, tk=1408, name=n("proj_bwd_x"))
    dx, dshift_m, dscale_m, gs['g_pre_mix'] = _rowwise(
        pre_bwd, [dh1, dx2, x], [p['g_pre_mix'], scale_m],
        [(d, F32, 'tile'), (d, F32, 'sum'), (d, F32, 'sum'), (d, F32, 'sum')], name=n("pre_mix_bwd"))
    dmod = [dshift_m, dscale_m, dgate_m, dshift_f, dscale_f, dgate_f]
    return dx, gw, dmod, gs


def _unshard(k, blocks):
    if k in COL_SHARDED:
        return blocks.transpose(1, 0, 2).reshape(blocks.shape[1], NDEV * blocks.shape[2])
    return blocks.reshape(NDEV * blocks.shape[1], blocks.shape[2])


def _to_slabs(k, g):
    if k == 'w_in':
        d = g.shape[0]
        nh = d // LANES
        g = jnp.concatenate([g[:, :2 * d + nh], g[:, 2 * d + LANES:]], axis=1)
    if k in COL_SHARDED:
        return g.reshape(g.shape[0], NDEV, g.shape[1] // NDEV).transpose(1, 0, 2)
    return g.reshape(NDEV, g.shape[0] // NDEV, g.shape[1])


def _prep_w_in(w_in):
    d = w_in.shape[0]
    nh = d // LANES
    fcol = 2 * d
    p = {}
    p['w_in_a'] = w_in[:, :fcol]
    p['w_in_f'] = jnp.pad(w_in[:, fcol:fcol + nh], ((0, 0), (0, LANES - nh)))
    p['w_in_g'] = w_in[:, fcol + nh:]
    p['w_in_all'] = jnp.concatenate([p['w_in_a'], p['w_in_f'], p['w_in_g']], axis=1)
    return p


def _prep_small(small):
    nh = small['b_f'].shape[0]
    p = {}
    for k in ('g_pre_mix', 'g_post_mix', 'g_pre_ffn', 'g_post_ffn', 'd_skip', 'b_glu'):
        p[k] = _row(small[k])
    p['b_f_row'] = jnp.pad(_row(small['b_f']), ((0, 0), (0, LANES - nh)))
    p['b_f_col'] = small['b_f'].reshape(nh, 1)
    lbr, lbi, bbr, bbi = _s5_discretise(small['lam_re'], small['lam_im'], small['log_dt'], small['b_re'], small['b_im'])
    p['lamb_re'], p['lamb_im'] = _row(lbr), _row(lbi)
    p['wb_re'] = _s5_operand(bbr.transpose(0, 2, 1), True).astype(BF)
    p['wb_im'] = _s5_operand(bbi.transpose(0, 2, 1), True).astype(BF)
    p['wc_re'] = _s5_operand(small['c_re'].transpose(0, 2, 1), False).astype(BF)
    p['wc_im'] = _s5_operand(small['c_im'].transpose(0, 2, 1), False).astype(BF)
    return p


def _local_step(x, target, mods, ps, small, hooks=None):
    depth = len(ps)
    s, d = x.shape
    hooks = hooks or {}
    saved = []
    h = x
    for l in range(depth):
        h, sv = _layer_fwd(h, mods[l], ps[l], l, ride=hooks['fwd_ride'](l) if hooks else None,
                           on_receive=functools.partial(hooks['fwd_recv'], l) if hooks else None)
        saved.append(sv)

    def loss_fn(yv, tv):
        e = yv - tv
        return e * (1.0 / d), jnp.sum(e * e, axis=1, keepdims=True) * (0.5 / d)

    dy, loss = _rowwise(loss_fn, [h, target], [], [(d, F32, 'tile'), (1, F32, 'sum')], name="loss")
    dmods, gss = [None] * depth, [None] * depth
    unsent = {}
    for l in range(depth - 1, -1, -1):
        def on_receive(names, results, l=l):
            hooks['bwd_recv']([(k, l) for k in names], results)

        dy, gw, dmods[l], gs = _layer_bwd(dy, saved[l], mods[l], ps[l], l,
                                          make_ride=hooks['bwd_ride'] if hooks else None,
                                          on_receive=on_receive if hooks else None)
        unsent.update({(k, l): g for k, g in gw.items()})
        sm = small[l]
        _, vjp = jax.vjp(_s5_discretise, sm['lam_re'], sm['lam_im'], sm['log_dt'], sm['b_re'], sm['b_im'])
        gs['lam_re'], gs['lam_im'], gs['log_dt'], gs['b_re'], gs['b_im'] = vjp(
            (gs.pop('lamb_re'), gs.pop('lamb_im'), gs.pop('bbar_re'), gs.pop('bbar_im')))
        gss[l] = gs
    return loss, dy, unsent, dmods, gss


SMALL_LOCAL = ['g_pre_mix', 'g_post_mix', 'g_pre_ffn', 'g_post_ffn', 'lam_re', 'lam_im', 'log_dt', 'b_re', 'b_im',
               'c_re', 'c_im', 'd_skip', 'b_glu', 'b_f']


def kernel(x, c, w_ada, b_ada, g_pre_mix, g_post_mix, g_pre_ffn, g_post_ffn, w_in, lam_re, lam_im, log_dt, b_re, b_im, c_re, c_im, d_skip, w_glu, b_glu, b_f, w_pa, w_pb, w_o, w_ffn_gate, w_ffn_up, w_ffn_down, loss_target, m_w_ada, m_b_ada, m_g_pre_mix, m_g_post_mix, m_g_pre_ffn, m_g_post_ffn, m_w_in, m_lam_re, m_lam_im, m_log_dt, m_b_re, m_b_im, m_c_re, m_c_im, m_d_skip, m_w_glu, m_b_glu, m_b_f, m_w_pa, m_w_pb, m_w_o, m_w_ffn_gate, m_w_ffn_up, m_w_ffn_down, v_w_ada, v_b_ada, v_g_pre_mix, v_g_post_mix, v_g_pre_ffn, v_g_post_ffn, v_w_in, v_lam_re, v_lam_im, v_log_dt, v_b_re, v_b_im, v_c_re, v_c_im, v_d_skip, v_w_glu, v_b_glu, v_b_f, v_w_pa, v_w_pb, v_w_o, v_w_ffn_gate, v_w_ffn_up, v_w_ffn_down):
    args = dict(locals())
    W = {k: args[k] for k in WEIGHTS}
    M = {k: args['m_' + k] for k in WEIGHTS}
    V = {k: args['v_' + k] for k in WEIGHTS}
    depth, d, ncol = w_ada.shape
    s = x.shape[1]
    me = 4 * lax.axis_index("x") + 2 * lax.axis_index("y") + lax.axis_index("c")

    c_all, = _exchange([jnp.pad(c, ((0, SUBLANES - 1), (0, 0)))], True, name="gather_c")
    c_all = c_all[:, 0, :]
    cond, = _rowwise(_silu, [c_all], [], [(d, F32, 'tile')], name="cond")
    mod_part = jnp.stack([_mm(cond, w_ada[l], name=f"ada_mm{l}") for l in range(depth)], axis=1)
    mod_recv, = _exchange([mod_part.reshape(NDEV, depth, 1, ncol)], False, name="scatter_mod")
    mod_cat = mod_recv.reshape(NDEV, depth, ncol).transpose(1, 0, 2).reshape(depth, NDEV * ncol)
    mod, = _rowwise(lambda a, b: a + b, [mod_cat, b_ada], [], [(NDEV * ncol, F32, 'tile')], name="mod_bias")
    mods = [[mod[l:l + 1, i * d:(i + 1) * d] for i in range(6)] for l in range(depth)]

    small = [{k: W[k][l] for k in SMALL_LOCAL} for l in range(depth)]
    ps = [_prep_small(small[l]) for l in range(depth)]
    first = ['w_in', 'w_glu']
    rest = [k for k in BIG if k not in first]
    riding = [[(k, l) for k in rest] + [(k, l + 1) for k in first if l + 1 < depth] for l in range(depth)]

    def take_weights(keys, results):
        for (k, l), blocks in zip(keys, results):
            full = _unshard(k, blocks)
            ps[l].update(_prep_w_in(full) if k == 'w_in' else {k: full})

    take_weights([(k, 0) for k in first],
                 _exchange([W[k][0].astype(BF) for k in first], True, name="gather_w_first"))

    def fwd_ride(l):
        blocks = [W[k][ll].astype(BF) for k, ll in riding[l]]
        return _Exchange(blocks, True), blocks

    grad_parts = {}

    def bwd_ride(grads):
        slabs = [_to_slabs(k, g) for k, g in grads.items()]
        return _Exchange(slabs, False), slabs

    hooks = dict(fwd_ride=fwd_ride, fwd_recv=lambda l, results: take_weights(riding[l], results),
                 bwd_ride=bwd_ride, bwd_recv=lambda keys, results: grad_parts.update(zip(keys, results)))

    loss, dx, unsent, dmods, gss = _local_step(x[0], loss_target[0], mods, ps, small, hooks)
    assert not unsent
    loss = lax.psum(loss[0, 0], ("x", "y", "c"))
    out = {}
    for k in BIG:
        out[k] = _adamw([grad_parts[(k, l)] for l in range(depth)], W[k], M[k], V[k], name=f"adamw_{k}")

    dmod_mine = jnp.stack([jnp.concatenate(dmods[l], axis=1)[0] for l in range(depth)])
    small_mine = [dmod_mine] + [jnp.stack([gss[l][k] for l in range(depth)]) for k in SMALL_LOCAL]
    parts, = _exchange([_pack(small_mine)], True, name="gather_small")
    summed = _sum_parts(parts, name="sum_small")
    names = ['b_ada'] + SMALL_LOCAL
    grads = _unpack(summed, [W[k] for k in names])
    res = _adamw([_pack(grads)[None]], _pack([W[k] for k in names])[None], _pack([M[k] for k in names])[None],
                 _pack([V[k] for k in names])[None], name="adamw_small")
    unpacked = [_unpack(a, [W[k] for k in names]) for a in res]
    for i, k in enumerate(names):
        out[k] = [unpacked[j][i] for j in range(4)]

    dmod_all = parts.reshape(NDEV, -1)[:, :depth * 6 * d].reshape(NDEV, depth, 6 * d)
    dmod_cols = lax.dynamic_slice_in_dim(dmod_all, me * ncol, ncol, axis=2)
    g_ada = [_mm(cond, dmod_cols[:, l], ta=True, precision=HI, name=f"ada_bwd{l}")[None] for l in range(depth)]
    out['w_ada'] = _adamw(g_ada, w_ada, m_w_ada, v_w_ada, name="adamw_w_ada")

    return (loss, dx[None], *[out[k][0] for k in WEIGHTS], *[out[k][1] for k in WEIGHTS],
            *[out[k][2] for k in WEIGHTS], *[out[k][3] for k in WEIGHTS])
```

```python
import functools
import math

import jax
import jax.numpy as jnp
from jax import lax
from jax.experimental import pallas as pl
from jax.experimental.pallas import tpu as pltpu

F32 = jnp.float32
BF = jnp.bfloat16
NDEV = 8
LANES = 128
SUBLANES = 8
VMEM_LIMIT = 48 * 1024 * 1024

SSM_H = 16
HEAD_DIM = 64
RMS_EPS = 1e-6
EIG_CLIP = 1e-4
ADAM_LR = 0.001
ADAM_B1 = 0.9
ADAM_B2 = 0.999
ADAM_EPS = 1e-08
ADAM_WD = 0.01
ADAM_STEP = 10
NEG = -1e30
HI = lax.Precision.HIGHEST

WEIGHTS = ['w_ada', 'b_ada', 'g_pre_mix', 'g_post_mix', 'g_pre_ffn', 'g_post_ffn', 'w_in', 'lam_re', 'lam_im',
           'log_dt', 'b_re', 'b_im', 'c_re', 'c_im', 'd_skip', 'w_glu', 'b_glu', 'b_f', 'w_pa', 'w_pb', 'w_o',
           'w_ffn_gate', 'w_ffn_up', 'w_ffn_down']
COL_SHARDED = ['w_in', 'w_pa', 'w_pb', 'w_ffn_gate', 'w_ffn_up']
ROW_SHARDED = ['w_glu', 'w_o', 'w_ffn_down']
BIG = COL_SHARDED + ROW_SHARDED
SMALL = ['b_ada', 'g_pre_mix', 'g_post_mix', 'g_pre_ffn', 'g_post_ffn', 'lam_re', 'lam_im', 'log_dt', 'b_re',
         'b_im', 'c_re', 'c_im', 'd_skip', 'b_glu', 'b_f']


def _fit(dim, target, align):
    if dim <= target:
        return dim
    t = (target // align) * align
    while t >= align:
        if dim % t == 0:
            return t
        t -= align
    return dim


def _params(**kw):
    return pltpu.CompilerParams(vmem_limit_bytes=VMEM_LIMIT, **kw)


def _mm(a, b, *, ta=False, tb=False, out_dtype=F32, tm=None, tn=512, tk=2048, precision=None, name, ride=None):
    m, k = (a.shape[1], a.shape[0]) if ta else a.shape
    n = b.shape[0] if tb else b.shape[1]
    assert (b.shape[1] if tb else b.shape[0]) == k
    tm = _fit(m, tm or (1024 if ta else 2048), LANES if ta else 16)
    tn = _fit(n, tn, LANES)
    tk = _fit(k, tk, LANES)
    nk = k // tk
    grid = (m // tm, n // tn, nk)
    dims = (((0 if ta else 1,), (1 if tb else 0,)), ((), ()))
    ex, ex_arrays = ride if ride is not None else (None, [])

    def kern(*refs):
        (a_ref, b_ref), (o_ref,), comm, scratch = _ride_split(ex, refs, 2, 1)
        step = (pl.program_id(0) * grid[1] + pl.program_id(1)) * grid[2] + pl.program_id(2)
        if ex is not None:
            @pl.when(step == 0)
            def _():
                ex.start(*comm)

            @pl.when(step == (grid[0] * grid[1] * grid[2]) // 2)
            def _():
                ex.forward(*comm)

        av, bv = a_ref[...], b_ref[...]
        if precision is None:
            av, bv = av.astype(BF), bv.astype(BF)
        p = lax.dot_general(av, bv, dims, preferred_element_type=F32, precision=precision)
        if nk == 1:
            o_ref[...] = p.astype(out_dtype)
        else:
            acc_ref, = scratch
            kk = pl.program_id(2)

            @pl.when(kk == 0)
            def _():
                acc_ref[...] = p

            @pl.when(kk > 0)
            def _():
                acc_ref[...] += p

            @pl.when(kk == nk - 1)
            def _():
                o_ref[...] = acc_ref[...].astype(out_dtype)

        if ex is not None:
            @pl.when(step == grid[0] * grid[1] * grid[2] - 1)
            def _():
                ex.wait(*comm)

    a_spec = pl.BlockSpec((tk, tm), lambda i, j, kk: (kk, i)) if ta else pl.BlockSpec((tm, tk), lambda i, j, kk: (i, kk))
    b_spec = pl.BlockSpec((tn, tk), lambda i, j, kk: (j, kk)) if tb else pl.BlockSpec((tk, tn), lambda i, j, kk: (kk, j))
    res = pl.pallas_call(
        kern, name=name,
        out_shape=[jax.ShapeDtypeStruct((m, n), out_dtype)] + (ex.out_shape if ex else []),
        grid=grid,
        in_specs=[a_spec, b_spec] + (ex.specs if ex else []),
        out_specs=[pl.BlockSpec((tm, tn), lambda i, j, kk: (i, j))] + (ex.specs if ex else []),
        scratch_shapes=(ex.scratch if ex else []) + ([] if nk == 1 else [pltpu.VMEM((tm, tn), F32)]),
        compiler_params=_params(dimension_semantics=("arbitrary",) * 3 if ex else ("parallel", "parallel", "arbitrary"),
                                has_side_effects=ex is not None),
    )(a, b, *ex_arrays)
    return (res[0], res[1:]) if ex else res[0]


def _rowwise(fn, tiles, params, outs, *, tr=256, name):
    tiles = [t if isinstance(t, tuple) else (t, t.shape[1], 0) for t in tiles]
    s = tiles[0][0].shape[0]
    tr = _fit(s, tr, 16)
    nt, npar = len(tiles), len(params)

    def kern(*refs):
        i = pl.program_id(0)
        res = fn(*[r[...] for r in refs[:nt + npar]])
        if not isinstance(res, (tuple, list)):
            res = (res,)
        for (w, dt, kind), o_ref, r in zip(outs, refs[nt + npar:], res):
            if kind == 'tile':
                o_ref[...] = r.astype(dt)
            else:
                part = jnp.sum(r.astype(F32), axis=0, keepdims=True)

                @pl.when(i == 0)
                def _(o_ref=o_ref, part=part):
                    o_ref[...] = part

                @pl.when(i > 0)
                def _(o_ref=o_ref, part=part):
                    o_ref[...] += part

    def tile_spec(w, cb):
        return pl.BlockSpec((tr, w), lambda i: (i, cb))

    in_specs = [tile_spec(w, cb) for _, w, cb in tiles]
    in_specs += [pl.BlockSpec(p.shape, lambda i, nd=p.ndim: (0,) * nd) for p in params]
    out_shape, out_specs = [], []
    for w, dt, kind in outs:
        if kind == 'tile':
            out_shape.append(jax.ShapeDtypeStruct((s, w), dt))
            out_specs.append(pl.BlockSpec((tr, w), lambda i: (i, 0)))
        else:
            out_shape.append(jax.ShapeDtypeStruct((1, w), F32))
            out_specs.append(pl.BlockSpec((1, w), lambda i: (0, 0)))
    res = pl.pallas_call(
        kern, name=name, out_shape=out_shape, grid=(s // tr,), in_specs=in_specs, out_specs=out_specs,
        compiler_params=_params(dimension_semantics=("arbitrary",)),
    )(*[t[0] for t in tiles], *params)
    return res


def _sigmoid(z):
    return 1.0 / (1.0 + jnp.exp(-z))


def _silu(z):
    return z * _sigmoid(z)


_GELU_K = math.sqrt(2.0 / math.pi)


def _gelu(y):
    return 0.5 * y * (1.0 + jnp.tanh(_GELU_K * (y + 0.044715 * y * y * y)))


def _gelu_grad(y):
    th = jnp.tanh(_GELU_K * (y + 0.044715 * y * y * y))
    return 0.5 * (1.0 + th) + 0.5 * y * (1.0 - th * th) * _GELU_K * (1.0 + 3.0 * 0.044715 * y * y)


def _rms(x):
    return lax.rsqrt(jnp.mean(x * x, axis=-1, keepdims=True) + RMS_EPS)


def _norm_bwd(dn, xhat, r):
    return r * (dn - xhat * jnp.mean(dn * xhat, axis=-1, keepdims=True))


def _cum_fwd(flog, bf_row, nh, *, name):
    s = flog.shape[0]
    w = nh * HEAD_DIM
    t = _fit(s, 256, SUBLANES)

    def kern(f_ref, b_ref, o_ref, carry_ref):
        i = pl.program_id(0)

        @pl.when(i == 0)
        def _():
            carry_ref[...] = jnp.zeros_like(carry_ref)

        z = f_ref[...] + b_ref[...]
        logf = jnp.minimum(z, 0.0) - jnp.log(1.0 + jnp.exp(-jnp.abs(z)))
        hh = lax.broadcasted_iota(jnp.int32, (LANES, w), 0)
        cc = lax.broadcasted_iota(jnp.int32, (LANES, w), 1)
        expand = (cc // HEAD_DIM == hh).astype(F32)
        lx = jnp.dot(logf, expand, preferred_element_type=F32, precision=HI)
        rr = lax.broadcasted_iota(jnp.int32, (t, t), 0)
        kk = lax.broadcasted_iota(jnp.int32, (t, t), 1)
        tri = (kk <= rr).astype(F32)
        cum = jnp.dot(tri, lx, preferred_element_type=F32, precision=HI) + carry_ref[...]
        o_ref[...] = cum
        carry_ref[...] = cum[t - 1:t, :]

    return pl.pallas_call(
        kern, name=name, out_shape=jax.ShapeDtypeStruct((s, w), F32), grid=(s // t,),
        in_specs=[pl.BlockSpec((t, LANES), lambda i: (i, 0)), pl.BlockSpec((1, LANES), lambda i: (0, 0))],
        out_specs=pl.BlockSpec((t, w), lambda i: (i, 0)),
        scratch_shapes=[pltpu.VMEM((1, w), F32)],
        compiler_params=_params(dimension_semantics=("arbitrary",)),
    )(flog, bf_row)


def _cum_bwd(dcrow, frow, bf_col, *, name):
    _, nh, s = dcrow.shape
    t = _fit(s, 512, LANES)
    nb = s // t

    def kern(d_ref, f_ref, b_ref, df_ref, db_ref):
        rr = lax.broadcasted_iota(jnp.int32, (t, t), 0)
        kk = lax.broadcasted_iota(jnp.int32, (t, t), 1)
        upper = (rr >= kk).astype(F32)
        carry = jnp.zeros((nh, 1), F32)
        db = jnp.zeros((nh, 1), F32)
        for blk in range(nb - 1, -1, -1):
            sl = slice(blk * t, (blk + 1) * t)
            rc = jnp.dot(d_ref[0, :, sl] + d_ref[1, :, sl], upper, preferred_element_type=F32, precision=HI) + carry
            carry = rc[:, 0:1]
            df = rc * _sigmoid(-(f_ref[:, sl] + b_ref[...]))
            df_ref[:, sl] = df
            db = db + jnp.sum(df, axis=1, keepdims=True)
        db_ref[...] = db

    return pl.pallas_call(
        kern, name=name,
        out_shape=[jax.ShapeDtypeStruct((nh, s), F32), jax.ShapeDtypeStruct((nh, 1), F32)],
        compiler_params=_params(),
    )(dcrow, frow, bf_col)


def _ride_split(ex, refs, n_in, n_out):
    n = ex.n if ex is not None else 0
    own_in, srcs = refs[:n_in], refs[n_in:n_in + n]
    own_out, dsts = refs[n_in + n:n_in + n + n_out], refs[n_in + n + n_out:n_in + 2 * n + n_out]
    sems = refs[n_in + 2 * n + n_out:n_in + 2 * n + n_out + 3] if n else ()
    rest = refs[n_in + 2 * n + n_out + (3 if n else 0):]
    return own_in, own_out, (srcs, dsts, sems), rest


ATTN_STRIP = 32
BIAS_LANES = 3


def _head_masks(rows):
    lane = lax.broadcasted_iota(jnp.int32, (rows, LANES), 1)
    return [(lane >= HEAD_DIM * e) & (lane < HEAD_DIM * (e + 1)) for e in range(2)]


def _augment(feat, bias, e, *, bias_slot, ones_slot):
    rows = feat.shape[0]
    lane = lax.broadcasted_iota(jnp.int32, (rows, LANES), 1)
    own = (lane >= HEAD_DIM * e) & (lane < HEAD_DIM * (e + 1))
    off = lane - HEAD_DIM * (1 - e)
    out = jnp.where(own, feat, 0.0)
    if ones_slot is not None:
        out = jnp.where((off >= ones_slot * BIAS_LANES) & (off < (ones_slot + 1) * BIAS_LANES), 1.0, out)
    if bias is not None:
        rest = pltpu.roll(bias, HEAD_DIM, 1)
        for term in range(BIAS_LANES):
            part = rest.astype(BF).astype(F32)
            out = jnp.where(off == bias_slot * BIAS_LANES + term, part, out)
            rest = rest - part
    return out.astype(BF)


def _two_slot_pipeline(m, scores, tile):
    scores(0, 0)

    def pair(n, carry):
        k = 2 * n
        scores(k + 1, 1)
        tile(k, 0, False)
        scores(k + 2, 0)
        tile(k + 1, 1, False)
        return carry

    lax.fori_loop(0, m // 2, pair, 0)

    @pl.when(m % 2 == 0)
    def _():
        tile(m, 0, True)

    @pl.when(m % 2 == 1)
    def _():
        scores(m, 1)
        tile(m - 1, 0, False)
        tile(m, 1, True)


def _attn_fwd(proj, cumx, qcol, *, name, ride=None):
    s = proj.shape[0]
    w = cumx.shape[1]
    nhp = w // LANES
    t = _fit(s, 256, LANES)
    nq = s // t
    strip = _fit(t, ATTN_STRIP, 16)
    scale = HEAD_DIM ** -0.5
    qb, kb, vb = qcol // LANES, (qcol + w) // LANES, (qcol + 2 * w) // LANES
    ex, ex_arrays = ride if ride is not None else (None, [])
    nt_dims = (((1,), (1,)), ((), ()))

    def kern(*refs):
        own_in, (o_ref, l_ref), comm, scratch = _ride_split(ex, refs, 5, 2)
        q_ref, k_ref, v_ref, cxq_ref, cxk_ref = own_in
        ka_ref, vat_ref, s0_ref, s1_ref, p_ref, m_ref, acc_ref = scratch
        s_refs = (s0_ref, s1_ref)
        i = pl.program_id(1)
        if ex is not None:
            @pl.when((pl.program_id(0) == 0) & (i == 0))
            def _():
                ex.start(*comm)

            @pl.when((pl.program_id(0) == nhp - 1) & (i == 0))
            def _():
                ex.forward(*comm)

        msks = _head_masks(t)

        @pl.when(i == 0)
        def _():
            def build(c, carry):
                rows = pl.ds(pl.multiple_of(c * t, LANES), t)
                k2, v2, cx = k_ref[rows, :], v_ref[rows, :], cxk_ref[rows, :]
                for e in range(2):
                    ka_ref[e, rows, :] = _augment(k2, -cx, e, bias_slot=1, ones_slot=0)
                    vat_ref[e, :, rows] = jnp.where(msks[e], v2, 1.0).T.astype(BF)
                return carry
            lax.fori_loop(0, nq, build, 0)

        q2 = q_ref[...] * scale
        qa = [_augment(q2, cxq_ref[...], e, bias_slot=0, ones_slot=1) for e in range(2)]
        m_ref[...] = jnp.full(m_ref.shape, NEG, F32)
        acc_ref[...] = jnp.zeros(acc_ref.shape, F32)
        slabs = strip // SUBLANES

        def scores(j, slot):
            rows_k = pl.ds(pl.multiple_of(j * t, LANES), t)
            for e in range(2):
                st = lax.dot_general(ka_ref[e, rows_k, :], qa[e], nt_dims, preferred_element_type=F32)
                s_refs[slot][e] = st.reshape(t // SUBLANES, SUBLANES, t)

        def tile(j, slot, diagonal):
            rows_k = pl.ds(pl.multiple_of(j * t, LANES), t)
            s_ref = s_refs[slot]
            for e in range(2):
                mx = jnp.full((SUBLANES, t), NEG, F32)
                for r in range(t // strip):
                    sl = slice(r * slabs, (r + 1) * slabs)
                    sv = s_ref[e,sl]
                    if diagonal:
                        shape = (slabs, SUBLANES, t)
                        key = (r * strip + lax.broadcasted_iota(jnp.int32, shape, 0) * SUBLANES
                               + lax.broadcasted_iota(jnp.int32, shape, 1))
                        sv = jnp.where(key <= lax.broadcasted_iota(jnp.int32, shape, 2), sv, NEG)
                        s_ref[e,sl] = sv
                    mx = jnp.maximum(mx, jnp.max(sv, axis=0))
                for sh in (4, 2, 1):
                    mx = jnp.maximum(mx, pltpu.roll(mx, sh, 0))
                m_old = m_ref[e]
                m_new = jnp.maximum(m_old, mx)
                alpha = jnp.exp(m_old - m_new)
                m_ref[e] = m_new
                for r in range(t // strip):
                    p = jnp.exp(s_ref[e,r * slabs:(r + 1) * slabs] - m_new[None])
                    p_ref[e, r * strip:(r + 1) * strip, :] = p.reshape(strip, t).astype(BF)
                acc = acc_ref[e].reshape(LANES // SUBLANES, SUBLANES, t) * alpha[None]
                acc_ref[e] = acc.reshape(LANES, t) + jnp.dot(vat_ref[e, :, rows_k], p_ref[e],
                                                             preferred_element_type=F32)

        _two_slot_pipeline(i, scores, tile)

        outs, lses = [], []
        for e in range(2):
            acc = acc_ref[e]
            other = HEAD_DIM * (1 - e)
            den = acc[other:other + 1, :]
            outs.append(acc / den)
            lses.append(jnp.broadcast_to(m_ref[e][0:1, :] + jnp.log(den), (LANES, t)))
        upper = lax.broadcasted_iota(jnp.int32, (LANES, t), 0) < HEAD_DIM
        o_ref[...] = jnp.where(upper, outs[0], outs[1]).T.astype(BF)
        l_ref[...] = jnp.where(upper, lses[0], lses[1]).T
        if ex is not None:
            @pl.when((pl.program_id(0) == nhp - 1) & (i == nq - 1))
            def _():
                ex.wait(*comm)

    own_scratch = [pltpu.VMEM((2, s, LANES), BF), pltpu.VMEM((2, LANES, s), BF),
                   pltpu.VMEM((2, t // SUBLANES, SUBLANES, t), F32),
                   pltpu.VMEM((2, t // SUBLANES, SUBLANES, t), F32), pltpu.VMEM((2, t, t), BF),
                   pltpu.VMEM((2, SUBLANES, t), F32), pltpu.VMEM((2, LANES, t), F32)]
    return pl.pallas_call(
        kern, name=name,
        out_shape=[jax.ShapeDtypeStruct((s, w), BF), jax.ShapeDtypeStruct((nhp, s, LANES), F32)]
        + (ex.out_shape if ex else []),
        grid=(nhp, nq),
        in_specs=[pl.BlockSpec((t, LANES), lambda h, i: (i, qb + h)),
                  pl.BlockSpec((s, LANES), lambda h, i: (0, kb + h)),
                  pl.BlockSpec((s, LANES), lambda h, i: (0, vb + h)),
                  pl.BlockSpec((t, LANES), lambda h, i: (i, h)),
                  pl.BlockSpec((s, LANES), lambda h, i: (0, h))] + (ex.specs if ex else []),
        out_specs=[pl.BlockSpec((t, LANES), lambda h, i: (i, h)),
                   pl.BlockSpec((None, t, LANES), lambda h, i: (h, i, 0))] + (ex.specs if ex else []),
        scratch_shapes=(ex.scratch if ex else []) + own_scratch,
        compiler_params=_params(dimension_semantics=("arbitrary", "arbitrary"),
                                has_side_effects=ex is not None),
    )(proj, proj, proj, cumx, cumx, *ex_arrays)


def _attn_bwd(proj, do, o, lse, cumx, qcol, *, name, ride=None):
    s = proj.shape[0]
    w = cumx.shape[1]
    nhp = w // LANES
    t = _fit(s, 256, LANES)
    nq = s // t
    strip = _fit(t, ATTN_STRIP, 16)
    scale = HEAD_DIM ** -0.5
    qb, kb, vb = qcol // LANES, (qcol + w) // LANES, (qcol + 2 * w) // LANES
    tn_dims = (((0,), (0,)), ((), ()))
    nt_dims = (((1,), (1,)), ((), ()))
    ex, ex_arrays = ride if ride is not None else (None, [])

    def kern(*refs):
        own_in, own_out, comm, scratch = _ride_split(ex, refs, 7, 5)
        q_ref, k_ref, v_ref, do_ref, o_ref, l_ref, cx_ref = own_in
        dq_ref, dk_ref, dv_ref, dkc_ref, dqc_ref = own_out
        qa_ref, da_ref, dqa_ref, dka_ref, dva_ref, st0_ref, st1_ref, dpt0_ref, dpt1_ref, pt_ref, dst_ref = scratch
        st_refs, dpt_refs = (st0_ref, st1_ref), (dpt0_ref, dpt1_ref)
        j = pl.program_id(1)
        if ex is not None:
            @pl.when((pl.program_id(0) == 0) & (j == 0))
            def _():
                ex.start(*comm)

        msks = _head_masks(t)

        @pl.when(j == 0)
        def _():
            def build(c, carry):
                rows = pl.ds(pl.multiple_of(c * t, LANES), t)
                q2 = q_ref[rows, :] * scale
                do2 = do_ref[rows, :]
                dd = do2 * o_ref[rows, :].astype(F32)
                delta = jnp.where(msks[0], jnp.sum(jnp.where(msks[0], dd, 0.0), axis=1, keepdims=True),
                                  jnp.sum(jnp.where(msks[1], dd, 0.0), axis=1, keepdims=True))
                bias = cx_ref[rows, :] - l_ref[rows, :]
                for e in range(2):
                    qa_ref[e, rows, :] = _augment(q2, bias, e, bias_slot=0, ones_slot=1)
                    da_ref[e, rows, :] = _augment(do2, -delta, e, bias_slot=0, ones_slot=None)
                return carry
            lax.fori_loop(0, nq, build, 0)
            dqa_ref[...] = jnp.zeros(dqa_ref.shape, F32)

        rows_k = pl.ds(pl.multiple_of(j * t, LANES), t)
        k2, v2 = k_ref[...], v_ref[...]
        ka = [_augment(k2, -cx_ref[rows_k, :], e, bias_slot=1, ones_slot=0) for e in range(2)]
        va = [_augment(v2, None, e, bias_slot=None, ones_slot=0) for e in range(2)]
        dka_ref[...] = jnp.zeros(dka_ref.shape, F32)
        dva_ref[...] = jnp.zeros(dva_ref.shape, F32)

        def scores(k, slot):
            rows_q = pl.ds(pl.multiple_of((nq - 1 - k) * t, LANES), t)
            for e in range(2):
                st_refs[slot][e] = lax.dot_general(ka[e], qa_ref[e, rows_q, :], nt_dims,
                                                   preferred_element_type=F32)
                dpt_refs[slot][e] = lax.dot_general(va[e], da_ref[e, rows_q, :], nt_dims,
                                                    preferred_element_type=F32)

        def tile(k, slot, diagonal):
            rows_q = pl.ds(pl.multiple_of((nq - 1 - k) * t, LANES), t)
            st_ref, dpt_ref = st_refs[slot], dpt_refs[slot]
            for e in range(2):
                for r in range(t // strip):
                    rows = slice(r * strip, (r + 1) * strip)
                    sv = st_ref[e, rows, :]
                    if diagonal:
                        key = r * strip + lax.broadcasted_iota(jnp.int32, (strip, t), 0)
                        qry = lax.broadcasted_iota(jnp.int32, (strip, t), 1)
                        sv = jnp.where(key <= qry, sv, NEG)
                    p = jnp.exp(sv)
                    pt_ref[e, rows, :] = p.astype(BF)
                    dst_ref[e, rows, :] = (p * dpt_ref[e, rows, :]).astype(BF)
            for e in range(2):
                dva_ref[e] += jnp.dot(pt_ref[e], da_ref[e, rows_q, :], preferred_element_type=F32)
                dka_ref[e] += jnp.dot(dst_ref[e], qa_ref[e, rows_q, :], preferred_element_type=F32)
                dqa_ref[e, rows_q, :] += lax.dot_general(dst_ref[e], ka[e], tn_dims, preferred_element_type=F32)

        _two_slot_pipeline(nq - 1 - j, scores, tile)

        dk_ref[...] = jnp.where(msks[0], dka_ref[0], dka_ref[1])
        dv_ref[...] = jnp.where(msks[0], dva_ref[0], dva_ref[1])
        sums = jnp.where(msks[1], dka_ref[0], dka_ref[1]).T
        dkc_ref[0:1, :] = sums[HEAD_DIM + BIAS_LANES:HEAD_DIM + BIAS_LANES + 1, :]
        dkc_ref[1:2, :] = sums[BIAS_LANES:BIAS_LANES + 1, :]

        @pl.when(j == nq - 1)
        def _():
            def flush(c, carry):
                rows = pl.ds(pl.multiple_of(c * t, LANES), t)
                a0, a1 = dqa_ref[0, rows, :], dqa_ref[1, rows, :]
                dq_ref[rows, :] = jnp.where(msks[0], a0, a1) * scale
                sums = jnp.where(msks[1], a0, a1).T
                dqc_ref[0:1, rows] = sums[HEAD_DIM:HEAD_DIM + 1, :]
                dqc_ref[1:2, rows] = sums[0:1, :]
                return carry
            lax.fori_loop(0, nq, flush, 0)

        if ex is not None:
            @pl.when((pl.program_id(0) == nhp - 1) & (j == nq - 1))
            def _():
                ex.wait(*comm)

    full = lambda cb: pl.BlockSpec((s, LANES), lambda h, j: (0, cb + h))
    blk = lambda cb: pl.BlockSpec((t, LANES), lambda h, j: (j, cb + h))
    own_scratch = [pltpu.VMEM((2, s, LANES), BF), pltpu.VMEM((2, s, LANES), BF), pltpu.VMEM((2, s, LANES), F32),
                   pltpu.VMEM((2, t, LANES), F32), pltpu.VMEM((2, t, LANES), F32),
                   pltpu.VMEM((2, t, t), F32), pltpu.VMEM((2, t, t), F32),
                   pltpu.VMEM((2, t, t), F32), pltpu.VMEM((2, t, t), F32),
                   pltpu.VMEM((2, t, t), BF), pltpu.VMEM((2, t, t), BF)]
    return pl.pallas_call(
        kern, name=name,
        out_shape=[jax.ShapeDtypeStruct((s, w), F32)] * 3 + [jax.ShapeDtypeStruct((nhp, 2, s), F32)] * 2
        + (ex.out_shape if ex else []),
        grid=(nhp, nq),
        in_specs=[full(qb), blk(kb), blk(vb), full(0), full(0),
                  pl.BlockSpec((None, s, LANES), lambda h, j: (h, 0, 0)), full(0)] + (ex.specs if ex else []),
        out_specs=[full(0), blk(0), blk(0), pl.BlockSpec((None, 2, t), lambda h, j: (h, 0, j)),
                   pl.BlockSpec((None, 2, s), lambda h, j: (h, 0, 0))] + (ex.specs if ex else []),
        scratch_shapes=(ex.scratch if ex else []) + own_scratch,
        compiler_params=_params(dimension_semantics=("arbitrary", "arbitrary"),
                                has_side_effects=ex is not None),
    )(proj, proj, proj, do, o, lse, cumx, *ex_arrays)


S5_STATES = 256
S5_ROWS = 512


def _cmul(ar, ai, br, bi):
    return ar * br - ai * bi, ar * bi + ai * br


def _scan_tables(lr, li, reverse):
    w = lr.shape[1]
    row = lax.broadcasted_iota(jnp.int32, (SUBLANES, w), 0)
    if reverse:
        row = SUBLANES - 1 - row
    lr1, li1 = jnp.broadcast_to(lr, (SUBLANES, w)), jnp.broadcast_to(li, (SUBLANES, w))
    lr2, li2 = _cmul(lr1, li1, lr1, li1)
    lr4, li4 = _cmul(lr2, li2, lr2, li2)
    steps = []
    for d, (pr, pi) in zip((1, 2, 4), ((lr1, li1), (lr2, li2), (lr4, li4))):
        keep = row >= d
        steps.append((jnp.where(keep, pr, 0.0), jnp.where(keep, pi, 0.0)))
    cr, ci = lr1, li1
    for bit, (pr, pi) in zip((1, 2, 4), ((lr1, li1), (lr2, li2), (lr4, li4))):
        nr, ni = _cmul(cr, ci, pr, pi)
        has = (row & bit) != 0
        cr, ci = jnp.where(has, nr, cr), jnp.where(has, ni, ci)
    return steps, (cr, ci)


def _scan_local(xr, xi, steps, reverse):
    for d, (pr, pi) in zip((1, 2, 4), steps):
        sh = (SUBLANES - d) if reverse else d
        sr, si = pltpu.roll(xr, sh, 0), pltpu.roll(xi, sh, 0)
        xr, xi = xr + (pr * sr - pi * si), xi + (pr * si + pi * sr)
    return xr, xi


def _scan_carry(xr, xi, car_r, car_i, carry_pow):
    cr, ci = carry_pow
    return xr + (cr * car_r - ci * car_i), xi + (cr * car_i + ci * car_r)


SCAN_UNROLL = 4


def _s5_specs(s, ncb):
    u_spec = pl.BlockSpec((s, LANES), lambda cb, hf: (0, cb))
    wb_spec = pl.BlockSpec((None, None, LANES, S5_STATES), lambda cb, hf: (cb, hf, 0, 0))
    wc_spec = pl.BlockSpec((None, None, S5_STATES, LANES), lambda cb, hf: (cb, hf, 0, 0))
    lam_spec = pl.BlockSpec((1, S5_STATES), lambda cb, hf: (0, 2 * cb + hf))
    d_spec = pl.BlockSpec((1, LANES), lambda cb, hf: (0, cb))
    return u_spec, wb_spec, wc_spec, lam_spec, d_spec


def _s5_project_and_scan(u_ref, wbr_ref, wbi_ref, lr_ref, li_ref, xr_ref, xi_ref, s, rows):
    wbr, wbi = wbr_ref[...], wbi_ref[...]
    for r in range(s // rows):
        sl = pl.ds(r * rows, rows)
        ub = u_ref[sl, :].astype(BF)
        xr_ref[sl, :] = jnp.dot(ub, wbr, preferred_element_type=F32)
        xi_ref[sl, :] = jnp.dot(ub, wbi, preferred_element_type=F32)
    steps, cpow = _scan_tables(lr_ref[...], li_ref[...], False)

    unroll = _fit(s // SUBLANES, SCAN_UNROLL, 1)

    def body(b, carry):
        car_r, car_i = carry
        sls = [pl.ds(pl.multiple_of((b * unroll + q) * SUBLANES, SUBLANES), SUBLANES) for q in range(unroll)]
        blocks = [_scan_local(xr_ref[sl, :], xi_ref[sl, :], steps, False) for sl in sls]
        for sl, (xr, xi) in zip(sls, blocks):
            xr, xi = _scan_carry(xr, xi, car_r, car_i, cpow)
            xr_ref[sl, :] = xr
            xi_ref[sl, :] = xi
            car_r, car_i = xr[SUBLANES - 1:SUBLANES, :], xi[SUBLANES - 1:SUBLANES, :]
        return car_r, car_i

    zero = jnp.zeros((1, S5_STATES), F32)
    lax.fori_loop(0, s // SUBLANES // unroll, body, (zero, zero))


def _s5_fwd(proj, wb_re, wb_im, wc_re, wc_im, lam_re, lam_im, dskip, *, name):
    s = proj.shape[0]
    w = dskip.shape[1]
    ncb = w // LANES
    rows = _fit(s, S5_ROWS, SUBLANES)

    def kern(u_ref, wbr_ref, wbi_ref, wcr_ref, wci_ref, lr_ref, li_ref, d_ref, y_ref, xr_ref, xi_ref):
        hf = pl.program_id(1)
        _s5_project_and_scan(u_ref, wbr_ref, wbi_ref, lr_ref, li_ref, xr_ref, xi_ref, s, rows)
        wcr, wci = wcr_ref[...], wci_ref[...]
        for r in range(s // rows):
            sl = pl.ds(r * rows, rows)
            y = (jnp.dot(xr_ref[sl, :].astype(BF), wcr, preferred_element_type=F32)
                 - jnp.dot(xi_ref[sl, :].astype(BF), wci, preferred_element_type=F32))

            @pl.when(hf == 0)
            def _(y=y, sl=sl):
                y_ref[sl, :] = y + d_ref[...] * u_ref[sl, :]

            @pl.when(hf == 1)
            def _(y=y, sl=sl):
                y_ref[sl, :] += y

    u_spec, wb_spec, wc_spec, lam_spec, d_spec = _s5_specs(s, ncb)
    return pl.pallas_call(
        kern, name=name, out_shape=jax.ShapeDtypeStruct((s, w), F32), grid=(ncb, 2),
        in_specs=[u_spec, wb_spec, wb_spec, wc_spec, wc_spec, lam_spec, lam_spec, d_spec],
        out_specs=u_spec,
        scratch_shapes=[pltpu.VMEM((s, S5_STATES), F32), pltpu.VMEM((s, S5_STATES), F32)],
        compiler_params=_params(dimension_semantics=("parallel", "arbitrary")),
    )(proj, wb_re, wb_im, wc_re, wc_im, lam_re, lam_im, dskip)


def _s5_bwd(proj, dy, wb_re, wb_im, wc_re, wc_im, lam_re, lam_im, dskip, *, name):
    s = proj.shape[0]
    w = dskip.shape[1]
    ncb = w // LANES
    rows = _fit(s, S5_ROWS, SUBLANES)
    tn_dims = (((0,), (0,)), ((), ()))
    nt_dims = (((1,), (1,)), ((), ()))

    def kern(u_ref, dy_ref, wbr_ref, wbi_ref, wcr_ref, wci_ref, lr_ref, li_ref, d_ref,
             du_ref, dwbr_ref, dwbi_ref, dwcr_ref, dwci_ref, dlr_ref, dli_ref, dd_ref,
             xr_ref, xi_ref, gr_ref, gi_ref):
        hf = pl.program_id(1)
        _s5_project_and_scan(u_ref, wbr_ref, wbi_ref, lr_ref, li_ref, xr_ref, xi_ref, s, rows)

        wcr, wci = wcr_ref[...], wci_ref[...]
        dwcr = jnp.zeros((S5_STATES, LANES), F32)
        dwci = jnp.zeros((S5_STATES, LANES), F32)
        ddsk = jnp.zeros((1, LANES), F32)
        for r in range(s // rows):
            sl = pl.ds(r * rows, rows)
            dyf = dy_ref[sl, :]
            dyb = dyf.astype(BF)
            gr_ref[sl, :] = lax.dot_general(dyb, wcr, nt_dims, preferred_element_type=F32)
            gi_ref[sl, :] = -lax.dot_general(dyb, wci, nt_dims, preferred_element_type=F32)
            dwcr = dwcr + lax.dot_general(xr_ref[sl, :].astype(BF), dyb, tn_dims, preferred_element_type=F32)
            dwci = dwci - lax.dot_general(xi_ref[sl, :].astype(BF), dyb, tn_dims, preferred_element_type=F32)
            ddsk = ddsk + jnp.sum(dyf * u_ref[sl, :], axis=0, keepdims=True)
        dwcr_ref[...] = dwcr
        dwci_ref[...] = dwci

        @pl.when(hf == 0)
        def _():
            dd_ref[...] = ddsk

        steps, cpow = _scan_tables(lr_ref[...], -li_ref[...], True)
        row = lax.broadcasted_iota(jnp.int32, (SUBLANES, S5_STATES), 0)
        nblk = s // SUBLANES

        unroll = _fit(nblk, SCAN_UNROLL, 1)

        def body(k, carry):
            car_r, car_i, ar, ai = carry
            sls = [pl.ds(pl.multiple_of((nblk - 1 - k * unroll - q) * SUBLANES, SUBLANES), SUBLANES)
                   for q in range(unroll)]
            blocks = [_scan_local(gr_ref[sl, :], gi_ref[sl, :], steps, True) for sl in sls]
            for sl, (g_r, g_i) in zip(sls, blocks):
                g_r, g_i = _scan_carry(g_r, g_i, car_r, car_i, cpow)
                gr_ref[sl, :] = g_r
                gi_ref[sl, :] = g_i
                nr = jnp.where(row == SUBLANES - 1, car_r, pltpu.roll(g_r, SUBLANES - 1, 0))
                ni = jnp.where(row == SUBLANES - 1, car_i, pltpu.roll(g_i, SUBLANES - 1, 0))
                xr, xi = xr_ref[sl, :], xi_ref[sl, :]
                ar = ar + (xr * nr + xi * ni)
                ai = ai + (xr * ni - xi * nr)
                car_r, car_i = g_r[0:1, :], g_i[0:1, :]
            return car_r, car_i, ar, ai

        zero = jnp.zeros((1, S5_STATES), F32)
        zacc = jnp.zeros((SUBLANES, S5_STATES), F32)
        _, _, ar, ai = lax.fori_loop(0, nblk // unroll, body, (zero, zero, zacc, zacc))
        dlr_ref[...] = jnp.sum(ar, axis=0, keepdims=True)
        dli_ref[...] = jnp.sum(ai, axis=0, keepdims=True)

        wbr, wbi = wbr_ref[...], wbi_ref[...]
        dwbr = jnp.zeros((LANES, S5_STATES), F32)
        dwbi = jnp.zeros((LANES, S5_STATES), F32)
        for r in range(s // rows):
            sl = pl.ds(r * rows, rows)
            grb, gib = gr_ref[sl, :].astype(BF), gi_ref[sl, :].astype(BF)
            ub = u_ref[sl, :].astype(BF)
            dwbr = dwbr + lax.dot_general(ub, grb, tn_dims, preferred_element_type=F32)
            dwbi = dwbi + lax.dot_general(ub, gib, tn_dims, preferred_element_type=F32)
            du = (lax.dot_general(grb, wbr, nt_dims, preferred_element_type=F32)
                  + lax.dot_general(gib, wbi, nt_dims, preferred_element_type=F32))

            @pl.when(hf == 0)
            def _(du=du, sl=sl):
                du_ref[sl, :] = du + d_ref[...] * dy_ref[sl, :]

            @pl.when(hf == 1)
            def _(du=du, sl=sl):
                du_ref[sl, :] += du
        dwbr_ref[...] = dwbr
        dwbi_ref[...] = dwbi

    u_spec, wb_spec, wc_spec, lam_spec, d_spec = _s5_specs(s, ncb)
    dwb_spec = pl.BlockSpec((None, None, LANES, S5_STATES), lambda cb, hf: (cb, hf, 0, 0))
    dwc_spec = pl.BlockSpec((None, None, S5_STATES, LANES), lambda cb, hf: (cb, hf, 0, 0))
    state = pltpu.VMEM((s, S5_STATES), F32)
    return pl.pallas_call(
        kern, name=name,
        out_shape=[jax.ShapeDtypeStruct((s, w), F32),
                   jax.ShapeDtypeStruct((ncb, 2, LANES, S5_STATES), F32), jax.ShapeDtypeStruct((ncb, 2, LANES, S5_STATES), F32),
                   jax.ShapeDtypeStruct((ncb, 2, S5_STATES, LANES), F32), jax.ShapeDtypeStruct((ncb, 2, S5_STATES, LANES), F32),
                   jax.ShapeDtypeStruct((1, 4 * w), F32), jax.ShapeDtypeStruct((1, 4 * w), F32),
                   jax.ShapeDtypeStruct((1, w), F32)],
        grid=(ncb, 2),
        in_specs=[u_spec, u_spec, wb_spec, wb_spec, wc_spec, wc_spec, lam_spec, lam_spec, d_spec],
        out_specs=[u_spec, dwb_spec, dwb_spec, dwc_spec, dwc_spec, lam_spec, lam_spec, d_spec],
        scratch_shapes=[state, state, state, state],
        compiler_params=_params(dimension_semantics=("parallel", "arbitrary")),
    )(proj, dy, wb_re, wb_im, wc_re, wc_im, lam_re, lam_im, dskip)


def _s5_discretise(lam_re, lam_im, log_dt, b_re, b_im):
    lr = jnp.minimum(lam_re, -EIG_CLIP)
    li = lam_im
    dt = jnp.exp(log_dt)[:, None]
    mag = jnp.exp(lr * dt)
    lbr, lbi = mag * jnp.cos(li * dt), mag * jnp.sin(li * dt)
    den = lr * lr + li * li
    fr = ((lbr - 1.0) * lr + lbi * li) / den
    fi = (lbi * lr - (lbr - 1.0) * li) / den
    bbr = fr[..., None] * b_re - fi[..., None] * b_im
    bbi = fr[..., None] * b_im + fi[..., None] * b_re
    return lbr, lbi, bbr, bbi


def _s5_operand(mats, channels_first):
    g, a, b = mats.shape
    gl = LANES // 2 // SSM_H
    ncb = g // (2 * gl)
    m = mats.reshape(ncb, 2, gl, a, b)
    eye = jnp.eye(gl, dtype=mats.dtype)
    inner = (m[:, :, :, :, None, :] * eye[None, None, :, None, :, None]).reshape(ncb, 2, gl * a, gl * b)
    zeros = jnp.zeros_like(inner[:, 0])
    axis = 1 if channels_first else 2
    return jnp.stack([jnp.concatenate([inner[:, 0], zeros], axis=axis),
                      jnp.concatenate([zeros, inner[:, 1]], axis=axis)], axis=1)


def _s5_block_grads(dwb, a, b, transpose):
    ncb = dwb.shape[0]
    gl = LANES // 2 // (a if not transpose else b)
    if not transpose:
        d = dwb.reshape(ncb, 2, 2, gl, a, gl, b)
        parts = [[d[:, hf, hf, g, :, g, :] for g in range(gl)] for hf in range(2)]
    else:
        d = dwb.reshape(ncb, 2, gl, a, 2, gl, b)
        parts = [[d[:, hf, g, :, hf, g, :] for g in range(gl)] for hf in range(2)]
    st = jnp.stack([jnp.stack(p, axis=1) for p in parts], axis=1)
    return st.reshape(ncb * 2 * gl, a, b)


def _adamw(parts, w, m, v, *, name):
    depth, r, c = w.shape
    assert len(parts) == depth
    npart = parts[0].shape[0]
    row_bytes = 4 * (-(-c // LANES) * LANES)
    align = 16 if parts[0].dtype == BF else SUBLANES
    budget = VMEM_LIMIT // 2 // (2 * (depth * npart + 7) * row_bytes)
    tr = _fit(r, max(align, budget // align * align), align)
    nr = r // tr
    c1 = 1.0 / (1.0 - ADAM_B1 ** ADAM_STEP)
    c2 = 1.0 / (1.0 - ADAM_B2 ** ADAM_STEP)

    def kern(*refs):
        p_refs = refs[:depth]
        w_ref, m_ref, v_ref, g_ref, d_ref, nm_ref, nv_ref = refs[depth:]
        layer = pl.program_id(0)
        for l in range(depth):
            @pl.when(layer == l)
            def _(p_ref=p_refs[l]):
                g = p_ref[0].astype(F32)
                for q in range(1, npart):
                    g = g + p_ref[q].astype(F32)
                m2 = ADAM_B1 * m_ref[...] + (1.0 - ADAM_B1) * g
                v2 = ADAM_B2 * v_ref[...] + (1.0 - ADAM_B2) * (g * g)
                upd = (m2 * c1) / (jnp.sqrt(v2 * c2) + ADAM_EPS) + ADAM_WD * w_ref[...]
                g_ref[...] = g
                d_ref[...] = -ADAM_LR * upd
                nm_ref[...] = m2
                nv_ref[...] = v2

    def part_spec(l):
        return pl.BlockSpec((npart, tr, c),
                            lambda ly, i: (0, jnp.where(ly == l, i, jnp.where(ly < l, 0, nr - 1)), 0))

    spec = pl.BlockSpec((None, tr, c), lambda ly, i: (ly, i, 0))
    return pl.pallas_call(
        kern, name=name, out_shape=[jax.ShapeDtypeStruct((depth, r, c), F32)] * 4, grid=(depth, nr),
        in_specs=[part_spec(l) for l in range(depth)] + [spec, spec, spec],
        out_specs=[spec] * 4,
        compiler_params=_params(dimension_semantics=("arbitrary", "arbitrary")),
    )(*parts, w, m, v)


def _sum_parts(parts, *, name):
    npart, r, c = parts.shape

    def kern(p_ref, o_ref):
        g = p_ref[0]
        for q in range(1, npart):
            g = g + p_ref[q]
        o_ref[...] = g

    return pl.pallas_call(kern, name=name, out_shape=jax.ShapeDtypeStruct((r, c), F32), compiler_params=_params())(parts)


class _Exchange:
    def __init__(self, arrays, gather):
        self.n = len(arrays)
        self.gather = gather
        self.out_shape = [jax.ShapeDtypeStruct(((NDEV,) + a.shape) if gather else a.shape, a.dtype) for a in arrays]
        self.scratch = [pltpu.SemaphoreType.DMA((self.n, NDEV - 1)), pltpu.SemaphoreType.DMA((self.n, NDEV - 1)),
                        pltpu.SemaphoreType.DMA((self.n,))]
        self.specs = [pl.BlockSpec(memory_space=pl.ANY)] * self.n

    def _copies(self, srcs, dsts, sems):
        send_sems, recv_sems, local_sems = sems
        x, y, c = lax.axis_index("x"), lax.axis_index("y"), lax.axis_index("c")
        me = 4 * x + 2 * y + c
        local = [pltpu.make_async_copy(srcs[a] if self.gather else srcs[a].at[me], dsts[a].at[me], local_sems.at[a])
                 for a in range(self.n)]
        remote = []
        for k in (1, 2, 4, 3, 5, 6, 7):
            px, py, pc = x ^ ((k >> 2) & 1), y ^ ((k >> 1) & 1), c ^ (k & 1)
            peer = 4 * px + 2 * py + pc
            for a in range(self.n):
                src = srcs[a] if self.gather else srcs[a].at[peer]
                mk = functools.partial(
                    pltpu.make_async_remote_copy, src_ref=src,
                    send_sem=send_sems.at[a, k - 1], recv_sem=recv_sems.at[a, k - 1],
                    device_id=(px, py, pc), device_id_type=pl.DeviceIdType.MESH)
                remote.append((mk(dst_ref=dsts[a].at[me]), mk(dst_ref=dsts[a].at[peer])))
        return local, remote

    def _gather_copies(self, srcs, dsts, sems):
        send_sems, recv_sems, local_sems = sems
        x, y, c = lax.axis_index("x"), lax.axis_index("y"), lax.axis_index("c")
        block = lambda px, py, pc: 4 * px + 2 * py + pc
        me = block(x, y, c)
        chips = [(1 - x, y), (x, 1 - y), (1 - x, 1 - y)]
        local = [pltpu.make_async_copy(srcs[a], dsts[a].at[me], local_sems.at[a]) for a in range(self.n)]
        own, passed = [], []
        for a in range(self.n):
            def copy(k, blk, to, src=None, a=a):
                return pltpu.make_async_remote_copy(
                    src_ref=dsts[a].at[blk] if src is None else src, dst_ref=dsts[a].at[blk],
                    send_sem=send_sems.at[a, k], recv_sem=recv_sems.at[a, k],
                    device_id=to, device_id_type=pl.DeviceIdType.MESH)
            sib = (x, y, 1 - c)
            own.append((copy(0, me, sib, srcs[a]), copy(0, block(x, y, 1 - c), sib)))
            for j, (px, py) in enumerate(chips):
                own.append((copy(1 + j, me, (px, py, c), srcs[a]), copy(1 + j, block(px, py, c), (px, py, c))))
            for j, (px, py) in enumerate(chips):
                passed.append((copy(4 + j, block(px, py, c), sib), copy(4 + j, block(px, py, 1 - c), sib)))
        return local, own, passed

    def start(self, srcs, dsts, sems):
        if self.gather:
            local, own, _ = self._gather_copies(srcs, dsts, sems)
            for cp in local:
                cp.start()
            for send, _ in own:
                send.start()
            return
        local, remote = self._copies(srcs, dsts, sems)
        for cp in local:
            cp.start()
        for send, _ in remote:
            send.start()

    def forward(self, srcs, dsts, sems):
        if not self.gather:
            return
        _, own, passed = self._gather_copies(srcs, dsts, sems)
        for a in range(self.n):
            for j in range(3):
                own[4 * a + 1 + j][1].wait_recv()
                passed[3 * a + j][0].start()

    def wait(self, srcs, dsts, sems):
        if self.gather:
            local, own, passed = self._gather_copies(srcs, dsts, sems)
            for a in range(self.n):
                own[4 * a][1].wait_recv()
            for _, arrival in passed:
                arrival.wait_recv()
            for send, _ in own + passed:
                send.wait_send()
            for cp in local:
                cp.wait()
            return
        local, remote = self._copies(srcs, dsts, sems)
        for send, arrival in remote:
            send.wait_send()
            arrival.wait_recv()
        for cp in local:
            cp.wait()


def _exchange(arrays, gather, *, name):
    ex = _Exchange(arrays, gather)
    n = ex.n

    def kern(*refs):
        srcs, dsts, sems = refs[:n], refs[n:2 * n], refs[2 * n:]
        ex.start(srcs, dsts, sems)
        ex.forward(srcs, dsts, sems)
        ex.wait(srcs, dsts, sems)

    return pl.pallas_call(
        kern, name=name, out_shape=ex.out_shape, in_specs=ex.specs, out_specs=ex.specs, scratch_shapes=ex.scratch,
        compiler_params=pltpu.CompilerParams(has_side_effects=True),
    )(*arrays)


def _pack(arrays):
    flat = jnp.concatenate([a.reshape(-1).astype(F32) for a in arrays])
    pad = (-flat.shape[0]) % (SUBLANES * LANES)
    return jnp.pad(flat, (0, pad)).reshape(-1, LANES)


def _unpack(buf, like):
    flat = buf.reshape(-1)
    out, off = [], 0
    for a in like:
        sz = math.prod(a.shape)
        out.append(flat[off:off + sz].reshape(a.shape))
        off += sz
    return out


def _row(v):
    return v.reshape(1, -1)


def _layer_fwd(x, mod, p, l, ride=None, on_receive=None):
    s, d = x.shape
    sw = d // 2
    nh = d // LANES
    shift_m, scale_m, gate_m, shift_f, scale_f, gate_f = mod
    n = lambda tag: f"{tag}{l}"
    sv = {}

    h1, = _rowwise(lambda xv, g, sc, sh: (xv * _rms(xv) * g) * (1.0 + sc) + sh,
                   [x], [p['g_pre_mix'], scale_m, shift_m], [(d, BF, 'tile')], name=n("pre_mix"))
    proj_a = _mm(h1, p['w_in_a'], name=n("proj_a"))
    flog = _mm(h1, p['w_in_f'], name=n("proj_f"))
    gates = _mm(h1, p['w_in_g'], name=n("proj_g"))

    y_s5 = _s5_fwd(proj_a, p['wb_re'], p['wb_im'], p['wc_re'], p['wc_im'], p['lamb_re'], p['lamb_im'], p['d_skip'],
                   name=n("s5_fwd"))
    z, = _rowwise(_gelu, [y_s5], [], [(sw, BF, 'tile')], name=n("gelu"))
    tglu = _mm(z, p['w_glu'], name=n("glu_mm"))
    ys, = _rowwise(lambda yv, tv, b: _gelu(yv) * _sigmoid(tv + b), [y_s5, tglu], [p['b_glu']], [(sw, BF, 'tile')],
                   name=n("glu"))

    cumx = _cum_fwd(flog, p['b_f_row'], nh, name=n("cum_fwd"))
    ya, lse, *received = _attn_fwd(proj_a, cumx, sw, name=n("attn_fwd"), ride=ride)
    if on_receive is not None:
        on_receive(received)

    am = _mm(ys, p['w_pa'], name=n("pa_mm"))
    bm = _mm(ya, p['w_pb'], name=n("pb_mm"))
    merged, = _rowwise(lambda a, b, ga, gb: _sigmoid(ga) * a + _sigmoid(gb) * b,
                       [am, bm, (gates, d, 0), (gates, d, 1)], [], [(d, BF, 'tile')], name=n("merge"))
    ym = _mm(merged, p['w_o'], name=n("o_mm"))
    x2, = _rowwise(lambda xv, yv, g, gt: xv + gt * (yv * _rms(yv) * g),
                   [x, ym], [p['g_post_mix'], gate_m], [(d, F32, 'tile')], name=n("post_mix"))

    h2, = _rowwise(lambda xv, g, sc, sh: (xv * _rms(xv) * g) * (1.0 + sc) + sh,
                   [x2], [p['g_pre_ffn'], scale_f, shift_f], [(d, BF, 'tile')], name=n("pre_ffn"))
    gt = _mm(h2, p['w_ffn_gate'], name=n("gate_mm"))
    up = _mm(h2, p['w_ffn_up'], name=n("up_mm"))
    dff = gt.shape[1]
    act, = _rowwise(lambda g, u: _silu(g) * u, [gt, up], [], [(dff, BF, 'tile')], name=n("swiglu"))
    yf = _mm(act, p['w_ffn_down'], name=n("down_mm"))
    x3, = _rowwise(lambda xv, yv, g, gt_: xv + gt_ * (yv * _rms(yv) * g),
                   [x2, yf], [p['g_post_ffn'], gate_f], [(d, F32, 'tile')], name=n("post_ffn"))

    sv.update(x=x, h1=h1, proj_a=proj_a, flog=flog, gates=gates, y_s5=y_s5, z=z, tglu=tglu, ys=ys, cumx=cumx,
              ya=ya, lse=lse, am=am, bm=bm, merged=merged, ym=ym, x2=x2, h2=h2, gt=gt, up=up,
              act=act, yf=yf)
    return x3, sv


def _layer_bwd(dx3, sv, mod, p, l, make_ride=None, on_receive=None):
    x, x2 = sv['x'], sv['x2']
    s, d = x.shape
    sw = d // 2
    nh = d // LANES
    shift_m, scale_m, gate_m, shift_f, scale_f, gate_f = mod
    n = lambda tag: f"{tag}{l}"
    gw, gs = {}, {}

    def post_bwd(dxo, yv, g, gate):
        r = _rms(yv)
        nf = yv * r
        dn = dxo * gate * g
        return _norm_bwd(dn, nf, r), dxo * (nf * g), dxo * gate * nf

    def pre_bwd(dh, dres, xv, g, sc):
        r = _rms(xv)
        xh = xv * r
        n3 = xh * g
        dn3 = dh * (1.0 + sc)
        return dres + _norm_bwd(dn3 * g, xh, r), dh, dh * n3, dn3 * xh

    dyf, dgate_f, gs['g_post_ffn'] = _rowwise(
        post_bwd, [dx3, sv['yf']], [p['g_post_ffn'], gate_f],
        [(d, BF, 'tile'), (d, F32, 'sum'), (d, F32, 'sum')], name=n("post_ffn_bwd"))
    dff = sv['gt'].shape[1]
    dact = _mm(dyf, p['w_ffn_down'], tb=True, name=n("down_bwd_x"))
    gw['w_ffn_down'] = _mm(sv['act'], dyf, ta=True, out_dtype=BF, name=n("down_bwd_w"))

    def swiglu_bwd(da, g, u):
        sg = _sigmoid(g)
        return da * u * (sg * (1.0 + g * (1.0 - sg))), da * (g * sg)

    dgt, dup = _rowwise(swiglu_bwd, [dact, sv['gt'], sv['up']], [], [(dff, BF, 'tile'), (dff, BF, 'tile')],
                        name=n("swiglu_bwd"))
    dh2a = _mm(dgt, p['w_ffn_gate'], tb=True, name=n("gate_bwd_x"))
    dh2b = _mm(dup, p['w_ffn_up'], tb=True, name=n("up_bwd_x"))
    gw['w_ffn_gate'] = _mm(sv['h2'], dgt, ta=True, out_dtype=BF, name=n("gate_bwd_w"))
    gw['w_ffn_up'] = _mm(sv['h2'], dup, ta=True, out_dtype=BF, name=n("up_bwd_w"))
    dx2, dshift_f, dscale_f, gs['g_pre_ffn'] = _rowwise(
        lambda da, db, dres, xv, g, sc: pre_bwd(da + db, dres, xv, g, sc),
        [dh2a, dh2b, dx3, x2], [p['g_pre_ffn'], scale_f],
        [(d, F32, 'tile'), (d, F32, 'sum'), (d, F32, 'sum'), (d, F32, 'sum')], name=n("pre_ffn_bwd"))

    dym, dgate_m, gs['g_post_mix'] = _rowwise(
        post_bwd, [dx2, sv['ym']], [p['g_post_mix'], gate_m],
        [(d, BF, 'tile'), (d, F32, 'sum'), (d, F32, 'sum')], name=n("post_mix_bwd"))
    dmerged = _mm(dym, p['w_o'], tb=True, name=n("o_bwd_x"))
    gw['w_o'] = _mm(sv['merged'], dym, ta=True, out_dtype=BF, name=n("o_bwd_w"))

    def merge_bwd(dm, a, b, ga, gb):
        sa, sb = _sigmoid(ga), _sigmoid(gb)
        return dm * sa, dm * sb, dm * a * sa * (1.0 - sa), dm * b * sb * (1.0 - sb)

    da_, db_, dga, dgb = _rowwise(
        merge_bwd, [dmerged, sv['am'], sv['bm'], (sv['gates'], d, 0), (sv['gates'], d, 1)], [],
        [(d, BF, 'tile')] * 4, name=n("merge_bwd"))
    dys = _mm(da_, p['w_pa'], tb=True, name=n("pa_bwd_x"))
    gw['w_pa'] = _mm(sv['ys'], da_, ta=True, out_dtype=BF, name=n("pa_bwd_w"))
    dya = _mm(db_, p['w_pb'], tb=True, name=n("pb_bwd_x"))
    gw['w_pb'] = _mm(sv['ya'], db_, ta=True, out_dtype=BF, name=n("pb_bwd_w"))

    sent = list(gw)
    dq, dk, dv, dkc, dqc, *received = _attn_bwd(
        sv['proj_a'], dya, sv['ya'], sv['lse'], sv['cumx'], sw, name=n("attn_bwd"),
        ride=make_ride({k: gw[k] for k in sent}) if make_ride is not None else None)
    if on_receive is not None:
        on_receive(sent, received)
    frow = sv['flog'][:, :nh].T
    dcum = jnp.stack([-dkc.reshape(nh, s), dqc.reshape(nh, s)])
    dfrow, dbf = _cum_bwd(dcum, frow, p['b_f_col'], name=n("cum_bwd"))
    gs['b_f'] = dbf.reshape(nh)
    dflog = jnp.pad(dfrow.T, ((0, 0), (0, LANES - nh))).astype(BF)

    def glu_bwd(dy_, yv, tv, b):
        zv = _gelu(yv)
        sg = _sigmoid(tv + b)
        dt = dy_ * zv * sg * (1.0 - sg)
        return dt, dy_ * sg, dt

    dt, dz1, gs['b_glu'] = _rowwise(glu_bwd, [dys, sv['y_s5'], sv['tglu']], [p['b_glu']],
                                    [(sw, BF, 'tile'), (sw, F32, 'tile'), (sw, F32, 'sum')], name=n("glu_bwd"))
    dz2 = _mm(dt, p['w_glu'], tb=True, name=n("glu_bwd_x"))
    gw['w_glu'] = _mm(sv['z'], dt, ta=True, out_dtype=BF, name=n("glu_bwd_w"))
    dy_s5, = _rowwise(lambda a, b, yv: (a + b) * _gelu_grad(yv), [dz1, dz2, sv['y_s5']], [], [(sw, F32, 'tile')],
                      name=n("gelu_bwd"))
    du, dwbr, dwbi, dwcr, dwci, dlr, dli, gs['d_skip'] = _s5_bwd(
        sv['proj_a'], dy_s5, p['wb_re'], p['wb_im'], p['wc_re'], p['wc_im'], p['lamb_re'], p['lamb_im'], p['d_skip'],
        name=n("s5_bwd"))
    g_ = sw // SSM_H
    pst = p['lamb_re'].shape[1] // g_
    gs['lamb_re'], gs['lamb_im'] = dlr.reshape(g_, pst), dli.reshape(g_, pst)
    gs['bbar_re'] = _s5_block_grads(dwbr, SSM_H, pst, False).transpose(0, 2, 1)
    gs['bbar_im'] = _s5_block_grads(dwbi, SSM_H, pst, False).transpose(0, 2, 1)
    gs['c_re'] = _s5_block_grads(dwcr, pst, SSM_H, True).transpose(0, 2, 1)
    gs['c_im'] = _s5_block_grads(dwci, pst, SSM_H, True).transpose(0, 2, 1)

    dproj = jnp.concatenate([du.astype(BF), dq.astype(BF), dk.astype(BF), dv.astype(BF), dflog, dga, dgb], axis=1)
    gw['w_in'] = _mm(sv['h1'], dproj, ta=True, out_dtype=BF, tn=1408, name=n("proj_bwd_w"))
    if make_ride is not None:
        tail = [k for k in gw if k not in sent]
        dh1, received = _mm(dproj, p['w_in_all'], tb=True, tk=1408, name=n("proj_bwd_x"),
                            ride=make_ride({k: gw[k] for k in tail}))
        on_receive(tail, received)
        gw = {}
    else:
        dh1 = _mm(dproj, p['w_in_all'], tb=True, tk=1408, name=n("proj_bwd_x"))
    dx, dshift_m, dscale_m, gs['g_pre_mix'] = _rowwise(
        pre_bwd, [dh1, dx2, x], [p['g_pre_mix'], scale_m],
        [(d, F32, 'tile'), (d, F32, 'sum'), (d, F32, 'sum'), (d, F32, 'sum')], name=n("pre_mix_bwd"))
    dmod = [dshift_m, dscale_m, dgate_m, dshift_f, dscale_f, dgate_f]
    return dx, gw, dmod, gs


def _unshard(k, blocks):
    if k in COL_SHARDED:
        return blocks.transpose(1, 0, 2).reshape(blocks.shape[1], NDEV * blocks.shape[2])
    return blocks.reshape(NDEV * blocks.shape[1], blocks.shape[2])


def _to_slabs(k, g):
    if k == 'w_in':
        d = g.shape[0]
        nh = d // LANES
        g = jnp.concatenate([g[:, :2 * d + nh], g[:, 2 * d + LANES:]], axis=1)
    if k in COL_SHARDED:
        return g.reshape(g.shape[0], NDEV, g.shape[1] // NDEV).transpose(1, 0, 2)
    return g.reshape(NDEV, g.shape[0] // NDEV, g.shape[1])


def _prep_w_in(w_in):
    d = w_in.shape[0]
    nh = d // LANES
    fcol = 2 * d
    p = {}
    p['w_in_a'] = w_in[:, :fcol]
    p['w_in_f'] = jnp.pad(w_in[:, fcol:fcol + nh], ((0, 0), (0, LANES - nh)))
    p['w_in_g'] = w_in[:, fcol + nh:]
    p['w_in_all'] = jnp.concatenate([p['w_in_a'], p['w_in_f'], p['w_in_g']], axis=1)
    return p


def _prep_small(small):
    nh = small['b_f'].shape[0]
    p = {}
    for k in ('g_pre_mix', 'g_post_mix', 'g_pre_ffn', 'g_post_ffn', 'd_skip', 'b_glu'):
        p[k] = _row(small[k])
    p['b_f_row'] = jnp.pad(_row(small['b_f']), ((0, 0), (0, LANES - nh)))
    p['b_f_col'] = small['b_f'].reshape(nh, 1)
    lbr, lbi, bbr, bbi = _s5_discretise(small['lam_re'], small['lam_im'], small['log_dt'], small['b_re'], small['b_im'])
    p['lamb_re'], p['lamb_im'] = _row(lbr), _row(lbi)
    p['wb_re'] = _s5_operand(bbr.transpose(0, 2, 1), True).astype(BF)
    p['wb_im'] = _s5_operand(bbi.transpose(0, 2, 1), True).astype(BF)
    p['wc_re'] = _s5_operand(small['c_re'].transpose(0, 2, 1), False).astype(BF)
    p['wc_im'] = _s5_operand(small['c_im'].transpose(0, 2, 1), False).astype(BF)
    return p


def _local_step(x, target, mods, ps, small, hooks=None):
    depth = len(ps)
    s, d = x.shape
    hooks = hooks or {}
    saved = []
    h = x
    for l in range(depth):
        h, sv = _layer_fwd(h, mods[l], ps[l], l, ride=hooks['fwd_ride'](l) if hooks else None,
                           on_receive=functools.partial(hooks['fwd_recv'], l) if hooks else None)
        saved.append(sv)

    def loss_fn(yv, tv):
        e = yv - tv
        return e * (1.0 / d), jnp.sum(e * e, axis=1, keepdims=True) * (0.5 / d)

    dy, loss = _rowwise(loss_fn, [h, target], [], [(d, F32, 'tile'), (1, F32, 'sum')], name="loss")
    dmods, gss = [None] * depth, [None] * depth
    unsent = {}
    for l in range(depth - 1, -1, -1):
        def on_receive(names, results, l=l):
            hooks['bwd_recv']([(k, l) for k in names], results)

        dy, gw, dmods[l], gs = _layer_bwd(dy, saved[l], mods[l], ps[l], l,
                                          make_ride=hooks['bwd_ride'] if hooks else None,
                                          on_receive=on_receive if hooks else None)
        unsent.update({(k, l): g for k, g in gw.items()})
        sm = small[l]
        _, vjp = jax.vjp(_s5_discretise, sm['lam_re'], sm['lam_im'], sm['log_dt'], sm['b_re'], sm['b_im'])
        gs['lam_re'], gs['lam_im'], gs['log_dt'], gs['b_re'], gs['b_im'] = vjp(
            (gs.pop('lamb_re'), gs.pop('lamb_im'), gs.pop('bbar_re'), gs.pop('bbar_im')))
        gss[l] = gs
    return loss, dy, unsent, dmods, gss


SMALL_LOCAL = ['g_pre_mix', 'g_post_mix', 'g_pre_ffn', 'g_post_ffn', 'lam_re', 'lam_im', 'log_dt', 'b_re', 'b_im',
               'c_re', 'c_im', 'd_skip', 'b_glu', 'b_f']


def kernel(x, c, w_ada, b_ada, g_pre_mix, g_post_mix, g_pre_ffn, g_post_ffn, w_in, lam_re, lam_im, log_dt, b_re, b_im, c_re, c_im, d_skip, w_glu, b_glu, b_f, w_pa, w_pb, w_o, w_ffn_gate, w_ffn_up, w_ffn_down, loss_target, m_w_ada, m_b_ada, m_g_pre_mix, m_g_post_mix, m_g_pre_ffn, m_g_post_ffn, m_w_in, m_lam_re, m_lam_im, m_log_dt, m_b_re, m_b_im, m_c_re, m_c_im, m_d_skip, m_w_glu, m_b_glu, m_b_f, m_w_pa, m_w_pb, m_w_o, m_w_ffn_gate, m_w_ffn_up, m_w_ffn_down, v_w_ada, v_b_ada, v_g_pre_mix, v_g_post_mix, v_g_pre_ffn, v_g_post_ffn, v_w_in, v_lam_re, v_lam_im, v_log_dt, v_b_re, v_b_im, v_c_re, v_c_im, v_d_skip, v_w_glu, v_b_glu, v_b_f, v_w_pa, v_w_pb, v_w_o, v_w_ffn_gate, v_w_ffn_up, v_w_ffn_down):
    args = dict(locals())
    W = {k: args[k] for k in WEIGHTS}
    M = {k: args['m_' + k] for k in WEIGHTS}
    V = {k: args['v_' + k] for k in WEIGHTS}
    depth, d, ncol = w_ada.shape
    s = x.shape[1]
    me = 4 * lax.axis_index("x") + 2 * lax.axis_index("y") + lax.axis_index("c")

    c_all, = _exchange([jnp.pad(c, ((0, SUBLANES - 1), (0, 0)))], True, name="gather_c")
    c_all = c_all[:, 0, :]
    cond, = _rowwise(_silu, [c_all], [], [(d, F32, 'tile')], name="cond")
    mod_part = jnp.stack([_mm(cond, w_ada[l], name=f"ada_mm{l}") for l in range(depth)], axis=1)
    mod_recv, = _exchange([mod_part.reshape(NDEV, depth, 1, ncol)], False, name="scatter_mod")
    mod_cat = mod_recv.reshape(NDEV, depth, ncol).transpose(1, 0, 2).reshape(depth, NDEV * ncol)
    mod, = _rowwise(lambda a, b: a + b, [mod_cat, b_ada], [], [(NDEV * ncol, F32, 'tile')], name="mod_bias")
    mods = [[mod[l:l + 1, i * d:(i + 1) * d] for i in range(6)] for l in range(depth)]

    small = [{k: W[k][l] for k in SMALL_LOCAL} for l in range(depth)]
    ps = [_prep_small(small[l]) for l in range(depth)]
    first = ['w_in', 'w_glu']
    rest = [k for k in BIG if k not in first]
    riding = [[(k, l) for k in rest] + [(k, l + 1) for k in first if l + 1 < depth] for l in range(depth)]

    def take_weights(keys, results):
        for (k, l), blocks in zip(keys, results):
            full = _unshard(k, blocks)
            ps[l].update(_prep_w_in(full) if k == 'w_in' else {k: full})

    take_weights([(k, 0) for k in first],
                 _exchange([W[k][0].astype(BF) for k in first], True, name="gather_w_first"))

    def fwd_ride(l):
        blocks = [W[k][ll].astype(BF) for k, ll in riding[l]]
        return _Exchange(blocks, True), blocks

    grad_parts = {}

    def bwd_ride(grads):
        slabs = [_to_slabs(k, g) for k, g in grads.items()]
        return _Exchange(slabs, False), slabs

    hooks = dict(fwd_ride=fwd_ride, fwd_recv=lambda l, results: take_weights(riding[l], results),
                 bwd_ride=bwd_ride, bwd_recv=lambda keys, results: grad_parts.update(zip(keys, results)))

    loss, dx, unsent, dmods, gss = _local_step(x[0], loss_target[0], mods, ps, small, hooks)
    assert not unsent
    loss = lax.psum(loss[0, 0], ("x", "y", "c"))
    out = {}
    for k in BIG:
        out[k] = _adamw([grad_parts[(k, l)] for l in range(depth)], W[k], M[k], V[k], name=f"adamw_{k}")

    dmod_mine = jnp.stack([jnp.concatenate(dmods[l], axis=1)[0] for l in range(depth)])
    small_mine = [dmod_mine] + [jnp.stack([gss[l][k] for l in range(depth)]) for k in SMALL_LOCAL]
    parts, = _exchange([_pack(small_mine)], True, name="gather_small")
    summed = _sum_parts(parts, name="sum_small")
    names = ['b_ada'] + SMALL_LOCAL
    for k, g in zip(names, _unpack(summed, [W[k] for k in names])):
        shp = W[k].shape
        rows = lambda a: a.reshape(depth, -1, shp[-1])
        res = _adamw([rows(g)[l][None] for l in range(depth)], rows(W[k]), rows(M[k]), rows(V[k]), name=f"adamw_{k}")
        out[k] = [a.reshape(shp) for a in res]

    dmod_all = parts.reshape(NDEV, -1)[:, :depth * 6 * d].reshape(NDEV, depth, 6 * d)
    dmod_cols = lax.dynamic_slice_in_dim(dmod_all, me * ncol, ncol, axis=2)
    g_ada = [_mm(cond, dmod_cols[:, l], ta=True, precision=HI, name=f"ada_bwd{l}")[None] for l in range(depth)]
    out['w_ada'] = _adamw(g_ada, w_ada, m_w_ada, v_w_ada, name="adamw_w_ada")

    return (loss, dx[None], *[out[k][0] for k in WEIGHTS], *[out[k][1] for k in WEIGHTS],
            *[out[k][2] for k in WEIGHTS], *[out[k][3] for k in WEIGHTS])
```

```python
import functools
import math

import jax
import jax.numpy as jnp
from jax import lax
from jax.experimental import pallas as pl
from jax.experimental.pallas import tpu as pltpu

F32 = jnp.float32
BF = jnp.bfloat16
NDEV = 8
LANES = 128
SUBLANES = 8
VMEM_LIMIT = 48 * 1024 * 1024

SSM_H = 16
HEAD_DIM = 64
RMS_EPS = 1e-6
EIG_CLIP = 1e-4
ADAM_LR = 0.001
ADAM_B1 = 0.9
ADAM_B2 = 0.999
ADAM_EPS = 1e-08
ADAM_WD = 0.01
ADAM_STEP = 10
NEG = -1e30
HI = lax.Precision.HIGHEST

WEIGHTS = ['w_ada', 'b_ada', 'g_pre_mix', 'g_post_mix', 'g_pre_ffn', 'g_post_ffn', 'w_in', 'lam_re', 'lam_im',
           'log_dt', 'b_re', 'b_im', 'c_re', 'c_im', 'd_skip', 'w_glu', 'b_glu', 'b_f', 'w_pa', 'w_pb', 'w_o',
           'w_ffn_gate', 'w_ffn_up', 'w_ffn_down']
COL_SHARDED = ['w_in', 'w_pa', 'w_pb', 'w_ffn_gate', 'w_ffn_up']
ROW_SHARDED = ['w_glu', 'w_o', 'w_ffn_down']
BIG = COL_SHARDED + ROW_SHARDED
SMALL = ['b_ada', 'g_pre_mix', 'g_post_mix', 'g_pre_ffn', 'g_post_ffn', 'lam_re', 'lam_im', 'log_dt', 'b_re',
         'b_im', 'c_re', 'c_im', 'd_skip', 'b_glu', 'b_f']


def _fit(dim, target, align):
    if dim <= target:
        return dim
    t = (target // align) * align
    while t >= align:
        if dim % t == 0:
            return t
        t -= align
    return dim


def _params(**kw):
    return pltpu.CompilerParams(vmem_limit_bytes=VMEM_LIMIT, **kw)


def _mm(a, b, *, ta=False, tb=False, out_dtype=F32, tm=None, tn=512, tk=2048, precision=None, name, ride=None):
    m, k = (a.shape[1], a.shape[0]) if ta else a.shape
    n = b.shape[0] if tb else b.shape[1]
    assert (b.shape[1] if tb else b.shape[0]) == k
    tm = _fit(m, tm or (1024 if ta else 2048), LANES if ta else 16)
    tn = _fit(n, tn, LANES)
    tk = _fit(k, tk, LANES)
    nk = k // tk
    grid = (m // tm, n // tn, nk)
    dims = (((0 if ta else 1,), (1 if tb else 0,)), ((), ()))
    ex, ex_arrays = ride if ride is not None else (None, [])

    def kern(*refs):
        (a_ref, b_ref), (o_ref,), comm, scratch = _ride_split(ex, refs, 2, 1)
        step = (pl.program_id(0) * grid[1] + pl.program_id(1)) * grid[2] + pl.program_id(2)
        if ex is not None:
            @pl.when(step == 0)
            def _():
                ex.start(*comm)

            @pl.when(step == (grid[0] * grid[1] * grid[2]) // 2)
            def _():
                ex.forward(*comm)

        av, bv = a_ref[...], b_ref[...]
        if precision is None:
            av, bv = av.astype(BF), bv.astype(BF)
        p = lax.dot_general(av, bv, dims, preferred_element_type=F32, precision=precision)
        if nk == 1:
            o_ref[...] = p.astype(out_dtype)
        else:
            acc_ref, = scratch
            kk = pl.program_id(2)

            @pl.when(kk == 0)
            def _():
                acc_ref[...] = p

            @pl.when(kk > 0)
            def _():
                acc_ref[...] += p

            @pl.when(kk == nk - 1)
            def _():
                o_ref[...] = acc_ref[...].astype(out_dtype)

        if ex is not None:
            @pl.when(step == grid[0] * grid[1] * grid[2] - 1)
            def _():
                ex.wait(*comm)

    a_spec = pl.BlockSpec((tk, tm), lambda i, j, kk: (kk, i)) if ta else pl.BlockSpec((tm, tk), lambda i, j, kk: (i, kk))
    b_spec = pl.BlockSpec((tn, tk), lambda i, j, kk: (j, kk)) if tb else pl.BlockSpec((tk, tn), lambda i, j, kk: (kk, j))
    res = pl.pallas_call(
        kern, name=name,
        out_shape=[jax.ShapeDtypeStruct((m, n), out_dtype)] + (ex.out_shape if ex else []),
        grid=grid,
        in_specs=[a_spec, b_spec] + (ex.specs if ex else []),
        out_specs=[pl.BlockSpec((tm, tn), lambda i, j, kk: (i, j))] + (ex.specs if ex else []),
        scratch_shapes=(ex.scratch if ex else []) + ([] if nk == 1 else [pltpu.VMEM((tm, tn), F32)]),
        compiler_params=_params(dimension_semantics=("arbitrary",) * 3 if ex else ("parallel", "parallel", "arbitrary"),
                                has_side_effects=ex is not None),
    )(a, b, *ex_arrays)
    return (res[0], res[1:]) if ex else res[0]


def _mm_fused(a, bs, extras, fn, out_dtypes, *, tb=False, tm=2048, tn=256, name):
    m, k = a.shape
    n = bs[0].shape[0] if tb else bs[0].shape[1]
    tm = _fit(m, tm, 16)
    tn = _fit(n, tn, LANES)
    extras = [e if isinstance(e, tuple) else (e, 0) for e in extras]
    nb, ne = len(bs), len(extras)
    dims = (((1,), (1 if tb else 0,)), ((), ()))

    def kern(*refs):
        av = refs[0][...].astype(BF)
        prods = [lax.dot_general(av, r[...].astype(BF), dims, preferred_element_type=F32) for r in refs[1:1 + nb]]
        res = fn(*prods, *[r[...] for r in refs[1 + nb:1 + nb + ne]])
        for o_ref, r, dt in zip(refs[1 + nb + ne:], res, out_dtypes):
            o_ref[...] = r.astype(dt)

    tile = pl.BlockSpec((tm, tn), lambda i, j: (i, j))
    b_spec = pl.BlockSpec((tn, k), lambda i, j: (j, 0)) if tb else pl.BlockSpec((k, tn), lambda i, j: (0, j))
    return pl.pallas_call(
        kern, name=name, out_shape=[jax.ShapeDtypeStruct((m, n), dt) for dt in out_dtypes],
        grid=(m // tm, n // tn),
        in_specs=[pl.BlockSpec((tm, k), lambda i, j: (i, 0))] + [b_spec] * nb
        + [pl.BlockSpec((tm, tn), lambda i, j, c=c: (i, j + c * (n // tn))) for _, c in extras],
        out_specs=[tile] * len(out_dtypes),
        compiler_params=_params(dimension_semantics=("parallel", "parallel")),
    )(a, *bs, *[e for e, _ in extras])


def _rowwise(fn, tiles, params, outs, *, tr=256, name):
    tiles = [t if isinstance(t, tuple) else (t, t.shape[1], 0) for t in tiles]
    s = tiles[0][0].shape[0]
    tr = _fit(s, tr, 16)
    nt, npar = len(tiles), len(params)

    def kern(*refs):
        i = pl.program_id(0)
        res = fn(*[r[...] for r in refs[:nt + npar]])
        if not isinstance(res, (tuple, list)):
            res = (res,)
        for (w, dt, kind), o_ref, r in zip(outs, refs[nt + npar:], res):
            if kind == 'tile':
                o_ref[...] = r.astype(dt)
            else:
                part = jnp.sum(r.astype(F32), axis=0, keepdims=True)

                @pl.when(i == 0)
                def _(o_ref=o_ref, part=part):
                    o_ref[...] = part

                @pl.when(i > 0)
                def _(o_ref=o_ref, part=part):
                    o_ref[...] += part

    def tile_spec(w, cb):
        return pl.BlockSpec((tr, w), lambda i: (i, cb))

    in_specs = [tile_spec(w, cb) for _, w, cb in tiles]
    in_specs += [pl.BlockSpec(p.shape, lambda i, nd=p.ndim: (0,) * nd) for p in params]
    out_shape, out_specs = [], []
    for w, dt, kind in outs:
        if kind == 'tile':
            out_shape.append(jax.ShapeDtypeStruct((s, w), dt))
            out_specs.append(pl.BlockSpec((tr, w), lambda i: (i, 0)))
        else:
            out_shape.append(jax.ShapeDtypeStruct((1, w), F32))
            out_specs.append(pl.BlockSpec((1, w), lambda i: (0, 0)))
    res = pl.pallas_call(
        kern, name=name, out_shape=out_shape, grid=(s // tr,), in_specs=in_specs, out_specs=out_specs,
        compiler_params=_params(dimension_semantics=("arbitrary",)),
    )(*[t[0] for t in tiles], *params)
    return res


def _sigmoid(z):
    return 1.0 / (1.0 + jnp.exp(-z))


def _silu(z):
    return z * _sigmoid(z)


_GELU_K = math.sqrt(2.0 / math.pi)


def _gelu(y):
    return 0.5 * y * (1.0 + jnp.tanh(_GELU_K * (y + 0.044715 * y * y * y)))


def _gelu_grad(y):
    th = jnp.tanh(_GELU_K * (y + 0.044715 * y * y * y))
    return 0.5 * (1.0 + th) + 0.5 * y * (1.0 - th * th) * _GELU_K * (1.0 + 3.0 * 0.044715 * y * y)


def _rms(x):
    return lax.rsqrt(jnp.mean(x * x, axis=-1, keepdims=True) + RMS_EPS)


def _norm_bwd(dn, xhat, r):
    return r * (dn - xhat * jnp.mean(dn * xhat, axis=-1, keepdims=True))


def _cum_fwd(flog, bf_row, nh, *, name):
    s = flog.shape[0]
    w = nh * HEAD_DIM
    t = _fit(s, 256, SUBLANES)

    def kern(f_ref, b_ref, o_ref, carry_ref):
        i = pl.program_id(0)

        @pl.when(i == 0)
        def _():
            carry_ref[...] = jnp.zeros_like(carry_ref)

        z = f_ref[...] + b_ref[...]
        logf = jnp.minimum(z, 0.0) - jnp.log(1.0 + jnp.exp(-jnp.abs(z)))
        hh = lax.broadcasted_iota(jnp.int32, (LANES, w), 0)
        cc = lax.broadcasted_iota(jnp.int32, (LANES, w), 1)
        expand = (cc // HEAD_DIM == hh).astype(F32)
        lx = jnp.dot(logf, expand, preferred_element_type=F32, precision=HI)
        rr = lax.broadcasted_iota(jnp.int32, (t, t), 0)
        kk = lax.broadcasted_iota(jnp.int32, (t, t), 1)
        tri = (kk <= rr).astype(F32)
        cum = jnp.dot(tri, lx, preferred_element_type=F32, precision=HI) + carry_ref[...]
        o_ref[...] = cum
        carry_ref[...] = cum[t - 1:t, :]

    return pl.pallas_call(
        kern, name=name, out_shape=jax.ShapeDtypeStruct((s, w), F32), grid=(s // t,),
        in_specs=[pl.BlockSpec((t, LANES), lambda i: (i, 0)), pl.BlockSpec((1, LANES), lambda i: (0, 0))],
        out_specs=pl.BlockSpec((t, w), lambda i: (i, 0)),
        scratch_shapes=[pltpu.VMEM((1, w), F32)],
        compiler_params=_params(dimension_semantics=("arbitrary",)),
    )(flog, bf_row)


def _cum_bwd(dcrow, frow, bf_col, *, name):
    _, nh, s = dcrow.shape
    t = _fit(s, 512, LANES)
    nb = s // t

    def kern(d_ref, f_ref, b_ref, df_ref, db_ref):
        rr = lax.broadcasted_iota(jnp.int32, (t, t), 0)
        kk = lax.broadcasted_iota(jnp.int32, (t, t), 1)
        upper = (rr >= kk).astype(F32)
        carry = jnp.zeros((nh, 1), F32)
        db = jnp.zeros((nh, 1), F32)
        for blk in range(nb - 1, -1, -1):
            sl = slice(blk * t, (blk + 1) * t)
            rc = jnp.dot(d_ref[0, :, sl] + d_ref[1, :, sl], upper, preferred_element_type=F32, precision=HI) + carry
            carry = rc[:, 0:1]
            df = rc * _sigmoid(-(f_ref[:, sl] + b_ref[...]))
            df_ref[:, sl] = df
            db = db + jnp.sum(df, axis=1, keepdims=True)
        db_ref[...] = db

    return pl.pallas_call(
        kern, name=name,
        out_shape=[jax.ShapeDtypeStruct((nh, s), F32), jax.ShapeDtypeStruct((nh, 1), F32)],
        compiler_params=_params(),
    )(dcrow, frow, bf_col)


def _ride_split(ex, refs, n_in, n_out):
    n = ex.n if ex is not None else 0
    own_in, srcs = refs[:n_in], refs[n_in:n_in + n]
    own_out, dsts = refs[n_in + n:n_in + n + n_out], refs[n_in + n + n_out:n_in + 2 * n + n_out]
    sems = refs[n_in + 2 * n + n_out:n_in + 2 * n + n_out + 3] if n else ()
    rest = refs[n_in + 2 * n + n_out + (3 if n else 0):]
    return own_in, own_out, (srcs, dsts, sems), rest


ATTN_STRIP = 32
BIAS_LANES = 3


def _head_masks(rows):
    lane = lax.broadcasted_iota(jnp.int32, (rows, LANES), 1)
    return [(lane >= HEAD_DIM * e) & (lane < HEAD_DIM * (e + 1)) for e in range(2)]


def _augment(feat, bias, e, *, bias_slot, ones_slot):
    rows = feat.shape[0]
    lane = lax.broadcasted_iota(jnp.int32, (rows, LANES), 1)
    own = (lane >= HEAD_DIM * e) & (lane < HEAD_DIM * (e + 1))
    off = lane - HEAD_DIM * (1 - e)
    out = jnp.where(own, feat, 0.0)
    if ones_slot is not None:
        out = jnp.where((off >= ones_slot * BIAS_LANES) & (off < (ones_slot + 1) * BIAS_LANES), 1.0, out)
    if bias is not None:
        rest = pltpu.roll(bias, HEAD_DIM, 1)
        for term in range(BIAS_LANES):
            part = rest.astype(BF).astype(F32)
            out = jnp.where(off == bias_slot * BIAS_LANES + term, part, out)
            rest = rest - part
    return out.astype(BF)


def _two_slot_pipeline(m, scores, tile):
    scores(0, 0)

    def pair(n, carry):
        k = 2 * n
        scores(k + 1, 1)
        tile(k, 0, False)
        scores(k + 2, 0)
        tile(k + 1, 1, False)
        return carry

    lax.fori_loop(0, m // 2, pair, 0)

    @pl.when(m % 2 == 0)
    def _():
        tile(m, 0, True)

    @pl.when(m % 2 == 1)
    def _():
        scores(m, 1)
        tile(m - 1, 0, False)
        tile(m, 1, True)


def _attn_fwd(proj, cumx, qcol, *, name, ride=None):
    s = proj.shape[0]
    w = cumx.shape[1]
    nhp = w // LANES
    t = _fit(s, 256, LANES)
    nq = s // t
    strip = _fit(t, ATTN_STRIP, 16)
    scale = HEAD_DIM ** -0.5
    qb, kb, vb = qcol // LANES, (qcol + w) // LANES, (qcol + 2 * w) // LANES
    ex, ex_arrays = ride if ride is not None else (None, [])
    nt_dims = (((1,), (1,)), ((), ()))

    def kern(*refs):
        own_in, (o_ref, l_ref), comm, scratch = _ride_split(ex, refs, 5, 2)
        q_ref, k_ref, v_ref, cxq_ref, cxk_ref = own_in
        ka_ref, vat_ref, s0_ref, s1_ref, p_ref, m_ref, acc_ref = scratch
        s_refs = (s0_ref, s1_ref)
        i = pl.program_id(1)
        if ex is not None:
            @pl.when((pl.program_id(0) == 0) & (i == 0))
            def _():
                ex.start(*comm)

            @pl.when((pl.program_id(0) == nhp - 1) & (i == 0))
            def _():
                ex.forward(*comm)

        msks = _head_masks(t)

        @pl.when(i == 0)
        def _():
            def build(c, carry):
                rows = pl.ds(pl.multiple_of(c * t, LANES), t)
                k2, v2, cx = k_ref[rows, :], v_ref[rows, :], cxk_ref[rows, :]
                for e in range(2):
                    ka_ref[e, rows, :] = _augment(k2, -cx, e, bias_slot=1, ones_slot=0)
                    vat_ref[e, :, rows] = jnp.where(msks[e], v2, 1.0).T.astype(BF)
                return carry
            lax.fori_loop(0, nq, build, 0)

        q2 = q_ref[...] * scale
        qa = [_augment(q2, cxq_ref[...], e, bias_slot=0, ones_slot=1) for e in range(2)]
        m_ref[...] = jnp.full(m_ref.shape, NEG, F32)
        acc_ref[...] = jnp.zeros(acc_ref.shape, F32)
        slabs = strip // SUBLANES

        def scores(j, slot):
            rows_k = pl.ds(pl.multiple_of(j * t, LANES), t)
            for e in range(2):
                st = lax.dot_general(ka_ref[e, rows_k, :], qa[e], nt_dims, preferred_element_type=F32)
                s_refs[slot][e] = st.reshape(t // SUBLANES, SUBLANES, t)

        def tile(j, slot, diagonal):
            rows_k = pl.ds(pl.multiple_of(j * t, LANES), t)
            s_ref = s_refs[slot]
            for e in range(2):
                mx = jnp.full((SUBLANES, t), NEG, F32)
                for r in range(t // strip):
                    sl = slice(r * slabs, (r + 1) * slabs)
                    sv = s_ref[e,sl]
                    if diagonal:
                        shape = (slabs, SUBLANES, t)
                        key = (r * strip + lax.broadcasted_iota(jnp.int32, shape, 0) * SUBLANES
                               + lax.broadcasted_iota(jnp.int32, shape, 1))
                        sv = jnp.where(key <= lax.broadcasted_iota(jnp.int32, shape, 2), sv, NEG)
                        s_ref[e,sl] = sv
                    mx = jnp.maximum(mx, jnp.max(sv, axis=0))
                for sh in (4, 2, 1):
                    mx = jnp.maximum(mx, pltpu.roll(mx, sh, 0))
                m_old = m_ref[e]
                m_new = jnp.maximum(m_old, mx)
                alpha = jnp.exp(m_old - m_new)
                m_ref[e] = m_new
                for r in range(t // strip):
                    p = jnp.exp(s_ref[e,r * slabs:(r + 1) * slabs] - m_new[None])
                    p_ref[e, r * strip:(r + 1) * strip, :] = p.reshape(strip, t).astype(BF)
                acc = acc_ref[e].reshape(LANES // SUBLANES, SUBLANES, t) * alpha[None]
                acc_ref[e] = acc.reshape(LANES, t) + jnp.dot(vat_ref[e, :, rows_k], p_ref[e],
                                                             preferred_element_type=F32)

        _two_slot_pipeline(i, scores, tile)

        outs, lses = [], []
        for e in range(2):
            acc = acc_ref[e]
            other = HEAD_DIM * (1 - e)
            den = acc[other:other + 1, :]
            outs.append(acc / den)
            lses.append(jnp.broadcast_to(m_ref[e][0:1, :] + jnp.log(den), (LANES, t)))
        upper = lax.broadcasted_iota(jnp.int32, (LANES, t), 0) < HEAD_DIM
        o_ref[...] = jnp.where(upper, outs[0], outs[1]).T.astype(BF)
        l_ref[...] = jnp.where(upper, lses[0], lses[1]).T
        if ex is not None:
            @pl.when((pl.program_id(0) == nhp - 1) & (i == nq - 1))
            def _():
                ex.wait(*comm)

    own_scratch = [pltpu.VMEM((2, s, LANES), BF), pltpu.VMEM((2, LANES, s), BF),
                   pltpu.VMEM((2, t // SUBLANES, SUBLANES, t), F32),
                   pltpu.VMEM((2, t // SUBLANES, SUBLANES, t), F32), pltpu.VMEM((2, t, t), BF),
                   pltpu.VMEM((2, SUBLANES, t), F32), pltpu.VMEM((2, LANES, t), F32)]
    return pl.pallas_call(
        kern, name=name,
        out_shape=[jax.ShapeDtypeStruct((s, w), BF), jax.ShapeDtypeStruct((nhp, s, LANES), F32)]
        + (ex.out_shape if ex else []),
        grid=(nhp, nq),
        in_specs=[pl.BlockSpec((t, LANES), lambda h, i: (i, qb + h)),
                  pl.BlockSpec((s, LANES), lambda h, i: (0, kb + h)),
                  pl.BlockSpec((s, LANES), lambda h, i: (0, vb + h)),
                  pl.BlockSpec((t, LANES), lambda h, i: (i, h)),
                  pl.BlockSpec((s, LANES), lambda h, i: (0, h))] + (ex.specs if ex else []),
        out_specs=[pl.BlockSpec((t, LANES), lambda h, i: (i, h)),
                   pl.BlockSpec((None, t, LANES), lambda h, i: (h, i, 0))] + (ex.specs if ex else []),
        scratch_shapes=(ex.scratch if ex else []) + own_scratch,
        compiler_params=_params(dimension_semantics=("arbitrary", "arbitrary"),
                                has_side_effects=ex is not None),
    )(proj, proj, proj, cumx, cumx, *ex_arrays)


def _attn_bwd(proj, do, o, lse, cumx, qcol, *, name, ride=None):
    s = proj.shape[0]
    w = cumx.shape[1]
    nhp = w // LANES
    t = _fit(s, 256, LANES)
    nq = s // t
    strip = _fit(t, ATTN_STRIP, 16)
    scale = HEAD_DIM ** -0.5
    qb, kb, vb = qcol // LANES, (qcol + w) // LANES, (qcol + 2 * w) // LANES
    tn_dims = (((0,), (0,)), ((), ()))
    nt_dims = (((1,), (1,)), ((), ()))
    ex, ex_arrays = ride if ride is not None else (None, [])

    def kern(*refs):
        own_in, own_out, comm, scratch = _ride_split(ex, refs, 7, 5)
        q_ref, k_ref, v_ref, do_ref, o_ref, l_ref, cx_ref = own_in
        dq_ref, dk_ref, dv_ref, dkc_ref, dqc_ref = own_out
        qa_ref, da_ref, dqa_ref, dka_ref, dva_ref, st0_ref, st1_ref, dpt0_ref, dpt1_ref, pt_ref, dst_ref = scratch
        st_refs, dpt_refs = (st0_ref, st1_ref), (dpt0_ref, dpt1_ref)
        j = pl.program_id(1)
        if ex is not None:
            @pl.when((pl.program_id(0) == 0) & (j == 0))
            def _():
                ex.start(*comm)

        msks = _head_masks(t)

        @pl.when(j == 0)
        def _():
            def build(c, carry):
                rows = pl.ds(pl.multiple_of(c * t, LANES), t)
                q2 = q_ref[rows, :] * scale
                do2 = do_ref[rows, :]
                dd = do2 * o_ref[rows, :].astype(F32)
                delta = jnp.where(msks[0], jnp.sum(jnp.where(msks[0], dd, 0.0), axis=1, keepdims=True),
                                  jnp.sum(jnp.where(msks[1], dd, 0.0), axis=1, keepdims=True))
                bias = cx_ref[rows, :] - l_ref[rows, :]
                for e in range(2):
                    qa_ref[e, rows, :] = _augment(q2, bias, e, bias_slot=0, ones_slot=1)
                    da_ref[e, rows, :] = _augment(do2, -delta, e, bias_slot=0, ones_slot=None)
                return carry
            lax.fori_loop(0, nq, build, 0)
            dqa_ref[...] = jnp.zeros(dqa_ref.shape, F32)

        rows_k = pl.ds(pl.multiple_of(j * t, LANES), t)
        k2, v2 = k_ref[...], v_ref[...]
        ka = [_augment(k2, -cx_ref[rows_k, :], e, bias_slot=1, ones_slot=0) for e in range(2)]
        va = [_augment(v2, None, e, bias_slot=None, ones_slot=0) for e in range(2)]
        dka_ref[...] = jnp.zeros(dka_ref.shape, F32)
        dva_ref[...] = jnp.zeros(dva_ref.shape, F32)

        def scores(k, slot):
            rows_q = pl.ds(pl.multiple_of((nq - 1 - k) * t, LANES), t)
            for e in range(2):
                st_refs[slot][e] = lax.dot_general(ka[e], qa_ref[e, rows_q, :], nt_dims,
                                                   preferred_element_type=F32)
                dpt_refs[slot][e] = lax.dot_general(va[e], da_ref[e, rows_q, :], nt_dims,
                                                    preferred_element_type=F32)

        def tile(k, slot, diagonal):
            rows_q = pl.ds(pl.multiple_of((nq - 1 - k) * t, LANES), t)
            st_ref, dpt_ref = st_refs[slot], dpt_refs[slot]
            for e in range(2):
                for r in range(t // strip):
                    rows = slice(r * strip, (r + 1) * strip)
                    sv = st_ref[e, rows, :]
                    if diagonal:
                        key = r * strip + lax.broadcasted_iota(jnp.int32, (strip, t), 0)
                        qry = lax.broadcasted_iota(jnp.int32, (strip, t), 1)
                        sv = jnp.where(key <= qry, sv, NEG)
                    p = jnp.exp(sv)
                    pt_ref[e, rows, :] = p.astype(BF)
                    dst_ref[e, rows, :] = (p * dpt_ref[e, rows, :]).astype(BF)
            for e in range(2):
                dva_ref[e] += jnp.dot(pt_ref[e], da_ref[e, rows_q, :], preferred_element_type=F32)
                dka_ref[e] += jnp.dot(dst_ref[e], qa_ref[e, rows_q, :], preferred_element_type=F32)
                dqa_ref[e, rows_q, :] += lax.dot_general(dst_ref[e], ka[e], tn_dims, preferred_element_type=F32)

        _two_slot_pipeline(nq - 1 - j, scores, tile)

        dk_ref[...] = jnp.where(msks[0], dka_ref[0], dka_ref[1])
        dv_ref[...] = jnp.where(msks[0], dva_ref[0], dva_ref[1])
        sums = jnp.where(msks[1], dka_ref[0], dka_ref[1]).T
        dkc_ref[0:1, :] = sums[HEAD_DIM + BIAS_LANES:HEAD_DIM + BIAS_LANES + 1, :]
        dkc_ref[1:2, :] = sums[BIAS_LANES:BIAS_LANES + 1, :]

        @pl.when(j == nq - 1)
        def _():
            def flush(c, carry):
                rows = pl.ds(pl.multiple_of(c * t, LANES), t)
                a0, a1 = dqa_ref[0, rows, :], dqa_ref[1, rows, :]
                dq_ref[rows, :] = jnp.where(msks[0], a0, a1) * scale
                sums = jnp.where(msks[1], a0, a1).T
                dqc_ref[0:1, rows] = sums[HEAD_DIM:HEAD_DIM + 1, :]
                dqc_ref[1:2, rows] = sums[0:1, :]
                return carry
            lax.fori_loop(0, nq, flush, 0)

        if ex is not None:
            @pl.when((pl.program_id(0) == nhp - 1) & (j == nq - 1))
            def _():
                ex.wait(*comm)

    full = lambda cb: pl.BlockSpec((s, LANES), lambda h, j: (0, cb + h))
    blk = lambda cb: pl.BlockSpec((t, LANES), lambda h, j: (j, cb + h))
    own_scratch = [pltpu.VMEM((2, s, LANES), BF), pltpu.VMEM((2, s, LANES), BF), pltpu.VMEM((2, s, LANES), F32),
                   pltpu.VMEM((2, t, LANES), F32), pltpu.VMEM((2, t, LANES), F32),
                   pltpu.VMEM((2, t, t), F32), pltpu.VMEM((2, t, t), F32),
                   pltpu.VMEM((2, t, t), F32), pltpu.VMEM((2, t, t), F32),
                   pltpu.VMEM((2, t, t), BF), pltpu.VMEM((2, t, t), BF)]
    return pl.pallas_call(
        kern, name=name,
        out_shape=[jax.ShapeDtypeStruct((s, w), F32)] * 3 + [jax.ShapeDtypeStruct((nhp, 2, s), F32)] * 2
        + (ex.out_shape if ex else []),
        grid=(nhp, nq),
        in_specs=[full(qb), blk(kb), blk(vb), full(0), full(0),
                  pl.BlockSpec((None, s, LANES), lambda h, j: (h, 0, 0)), full(0)] + (ex.specs if ex else []),
        out_specs=[full(0), blk(0), blk(0), pl.BlockSpec((None, 2, t), lambda h, j: (h, 0, j)),
                   pl.BlockSpec((None, 2, s), lambda h, j: (h, 0, 0))] + (ex.specs if ex else []),
        scratch_shapes=(ex.scratch if ex else []) + own_scratch,
        compiler_params=_params(dimension_semantics=("arbitrary", "arbitrary"),
                                has_side_effects=ex is not None),
    )(proj, proj, proj, do, o, lse, cumx, *ex_arrays)


S5_STATES = 256
S5_ROWS = 512


def _cmul(ar, ai, br, bi):
    return ar * br - ai * bi, ar * bi + ai * br


def _scan_tables(lr, li, reverse):
    w = lr.shape[1]
    row = lax.broadcasted_iota(jnp.int32, (SUBLANES, w), 0)
    if reverse:
        row = SUBLANES - 1 - row
    lr1, li1 = jnp.broadcast_to(lr, (SUBLANES, w)), jnp.broadcast_to(li, (SUBLANES, w))
    lr2, li2 = _cmul(lr1, li1, lr1, li1)
    lr4, li4 = _cmul(lr2, li2, lr2, li2)
    steps = []
    for d, (pr, pi) in zip((1, 2, 4), ((lr1, li1), (lr2, li2), (lr4, li4))):
        keep = row >= d
        steps.append((jnp.where(keep, pr, 0.0), jnp.where(keep, pi, 0.0)))
    cr, ci = lr1, li1
    for bit, (pr, pi) in zip((1, 2, 4), ((lr1, li1), (lr2, li2), (lr4, li4))):
        nr, ni = _cmul(cr, ci, pr, pi)
        has = (row & bit) != 0
        cr, ci = jnp.where(has, nr, cr), jnp.where(has, ni, ci)
    return steps, (cr, ci)


def _scan_local(xr, xi, steps, reverse):
    for d, (pr, pi) in zip((1, 2, 4), steps):
        sh = (SUBLANES - d) if reverse else d
        sr, si = pltpu.roll(xr, sh, 0), pltpu.roll(xi, sh, 0)
        xr, xi = xr + (pr * sr - pi * si), xi + (pr * si + pi * sr)
    return xr, xi


def _scan_carry(xr, xi, car_r, car_i, carry_pow):
    cr, ci = carry_pow
    return xr + (cr * car_r - ci * car_i), xi + (cr * car_i + ci * car_r)


SCAN_UNROLL = 4


def _s5_specs(s, ncb):
    u_spec = pl.BlockSpec((s, LANES), lambda cb, hf: (0, cb))
    wb_spec = pl.BlockSpec((None, None, LANES, S5_STATES), lambda cb, hf: (cb, hf, 0, 0))
    wc_spec = pl.BlockSpec((None, None, S5_STATES, LANES), lambda cb, hf: (cb, hf, 0, 0))
    lam_spec = pl.BlockSpec((1, S5_STATES), lambda cb, hf: (0, 2 * cb + hf))
    d_spec = pl.BlockSpec((1, LANES), lambda cb, hf: (0, cb))
    return u_spec, wb_spec, wc_spec, lam_spec, d_spec


def _s5_project_and_scan(u_ref, wbr_ref, wbi_ref, lr_ref, li_ref, xr_ref, xi_ref, s, rows):
    wbr, wbi = wbr_ref[...], wbi_ref[...]
    for r in range(s // rows):
        sl = pl.ds(r * rows, rows)
        ub = u_ref[sl, :].astype(BF)
        xr_ref[sl, :] = jnp.dot(ub, wbr, preferred_element_type=F32)
        xi_ref[sl, :] = jnp.dot(ub, wbi, preferred_element_type=F32)
    steps, cpow = _scan_tables(lr_ref[...], li_ref[...], False)

    unroll = _fit(s // SUBLANES, SCAN_UNROLL, 1)

    def body(b, carry):
        car_r, car_i = carry
        sls = [pl.ds(pl.multiple_of((b * unroll + q) * SUBLANES, SUBLANES), SUBLANES) for q in range(unroll)]
        blocks = [_scan_local(xr_ref[sl, :], xi_ref[sl, :], steps, False) for sl in sls]
        for sl, (xr, xi) in zip(sls, blocks):
            xr, xi = _scan_carry(xr, xi, car_r, car_i, cpow)
            xr_ref[sl, :] = xr
            xi_ref[sl, :] = xi
            car_r, car_i = xr[SUBLANES - 1:SUBLANES, :], xi[SUBLANES - 1:SUBLANES, :]
        return car_r, car_i

    zero = jnp.zeros((1, S5_STATES), F32)
    lax.fori_loop(0, s // SUBLANES // unroll, body, (zero, zero))


def _s5_fwd(proj, wb_re, wb_im, wc_re, wc_im, lam_re, lam_im, dskip, *, name):
    s = proj.shape[0]
    w = dskip.shape[1]
    ncb = w // LANES
    rows = _fit(s, S5_ROWS, SUBLANES)

    def kern(u_ref, wbr_ref, wbi_ref, wcr_ref, wci_ref, lr_ref, li_ref, d_ref, y_ref, xr_ref, xi_ref):
        hf = pl.program_id(1)
        _s5_project_and_scan(u_ref, wbr_ref, wbi_ref, lr_ref, li_ref, xr_ref, xi_ref, s, rows)
        wcr, wci = wcr_ref[...], wci_ref[...]
        for r in range(s // rows):
            sl = pl.ds(r * rows, rows)
            y = (jnp.dot(xr_ref[sl, :].astype(BF), wcr, preferred_element_type=F32)
                 - jnp.dot(xi_ref[sl, :].astype(BF), wci, preferred_element_type=F32))

            @pl.when(hf == 0)
            def _(y=y, sl=sl):
                y_ref[sl, :] = y + d_ref[...] * u_ref[sl, :]

            @pl.when(hf == 1)
            def _(y=y, sl=sl):
                y_ref[sl, :] += y

    u_spec, wb_spec, wc_spec, lam_spec, d_spec = _s5_specs(s, ncb)
    return pl.pallas_call(
        kern, name=name, out_shape=jax.ShapeDtypeStruct((s, w), F32), grid=(ncb, 2),
        in_specs=[u_spec, wb_spec, wb_spec, wc_spec, wc_spec, lam_spec, lam_spec, d_spec],
        out_specs=u_spec,
        scratch_shapes=[pltpu.VMEM((s, S5_STATES), F32), pltpu.VMEM((s, S5_STATES), F32)],
        compiler_params=_params(dimension_semantics=("parallel", "arbitrary")),
    )(proj, wb_re, wb_im, wc_re, wc_im, lam_re, lam_im, dskip)


def _s5_bwd(proj, dy, wb_re, wb_im, wc_re, wc_im, lam_re, lam_im, dskip, *, name, ride=None):
    s = proj.shape[0]
    w = dskip.shape[1]
    ncb = w // LANES
    rows = _fit(s, S5_ROWS, SUBLANES)
    tn_dims = (((0,), (0,)), ((), ()))
    nt_dims = (((1,), (1,)), ((), ()))
    ex, ex_arrays = ride if ride is not None else (None, [])

    def kern(*refs):
        own_in, own_out, comm, scratch = _ride_split(ex, refs, 9, 8)
        u_ref, dy_ref, wbr_ref, wbi_ref, wcr_ref, wci_ref, lr_ref, li_ref, d_ref = own_in
        du_ref, dwbr_ref, dwbi_ref, dwcr_ref, dwci_ref, dlr_ref, dli_ref, dd_ref = own_out
        xr_ref, xi_ref, gr_ref, gi_ref = scratch
        hf = pl.program_id(1)
        if ex is not None:
            @pl.when((pl.program_id(0) == 0) & (hf == 0))
            def _():
                ex.start(*comm)

        _s5_project_and_scan(u_ref, wbr_ref, wbi_ref, lr_ref, li_ref, xr_ref, xi_ref, s, rows)

        wcr, wci = wcr_ref[...], wci_ref[...]
        dwcr = jnp.zeros((S5_STATES, LANES), F32)
        dwci = jnp.zeros((S5_STATES, LANES), F32)
        ddsk = jnp.zeros((1, LANES), F32)
        for r in range(s // rows):
            sl = pl.ds(r * rows, rows)
            dyf = dy_ref[sl, :]
            dyb = dyf.astype(BF)
            gr_ref[sl, :] = lax.dot_general(dyb, wcr, nt_dims, preferred_element_type=F32)
            gi_ref[sl, :] = -lax.dot_general(dyb, wci, nt_dims, preferred_element_type=F32)
            dwcr = dwcr + lax.dot_general(xr_ref[sl, :].astype(BF), dyb, tn_dims, preferred_element_type=F32)
            dwci = dwci - lax.dot_general(xi_ref[sl, :].astype(BF), dyb, tn_dims, preferred_element_type=F32)
            ddsk = ddsk + jnp.sum(dyf * u_ref[sl, :], axis=0, keepdims=True)
        dwcr_ref[...] = dwcr
        dwci_ref[...] = dwci

        @pl.when(hf == 0)
        def _():
            dd_ref[...] = ddsk

        steps, cpow = _scan_tables(lr_ref[...], -li_ref[...], True)
        row = lax.broadcasted_iota(jnp.int32, (SUBLANES, S5_STATES), 0)
        nblk = s // SUBLANES

        unroll = _fit(nblk, SCAN_UNROLL, 1)

        def body(k, carry):
            car_r, car_i, ar, ai = carry
            sls = [pl.ds(pl.multiple_of((nblk - 1 - k * unroll - q) * SUBLANES, SUBLANES), SUBLANES)
                   for q in range(unroll)]
            blocks = [_scan_local(gr_ref[sl, :], gi_ref[sl, :], steps, True) for sl in sls]
            for sl, (g_r, g_i) in zip(sls, blocks):
                g_r, g_i = _scan_carry(g_r, g_i, car_r, car_i, cpow)
                gr_ref[sl, :] = g_r
                gi_ref[sl, :] = g_i
                nr = jnp.where(row == SUBLANES - 1, car_r, pltpu.roll(g_r, SUBLANES - 1, 0))
                ni = jnp.where(row == SUBLANES - 1, car_i, pltpu.roll(g_i, SUBLANES - 1, 0))
                xr, xi = xr_ref[sl, :], xi_ref[sl, :]
                ar = ar + (xr * nr + xi * ni)
                ai = ai + (xr * ni - xi * nr)
                car_r, car_i = g_r[0:1, :], g_i[0:1, :]
            return car_r, car_i, ar, ai

        zero = jnp.zeros((1, S5_STATES), F32)
        zacc = jnp.zeros((SUBLANES, S5_STATES), F32)
        _, _, ar, ai = lax.fori_loop(0, nblk // unroll, body, (zero, zero, zacc, zacc))
        dlr_ref[...] = jnp.sum(ar, axis=0, keepdims=True)
        dli_ref[...] = jnp.sum(ai, axis=0, keepdims=True)

        wbr, wbi = wbr_ref[...], wbi_ref[...]
        dwbr = jnp.zeros((LANES, S5_STATES), F32)
        dwbi = jnp.zeros((LANES, S5_STATES), F32)
        for r in range(s // rows):
            sl = pl.ds(r * rows, rows)
            grb, gib = gr_ref[sl, :].astype(BF), gi_ref[sl, :].astype(BF)
            ub = u_ref[sl, :].astype(BF)
            dwbr = dwbr + lax.dot_general(ub, grb, tn_dims, preferred_element_type=F32)
            dwbi = dwbi + lax.dot_general(ub, gib, tn_dims, preferred_element_type=F32)
            du = (lax.dot_general(grb, wbr, nt_dims, preferred_element_type=F32)
                  + lax.dot_general(gib, wbi, nt_dims, preferred_element_type=F32))

            @pl.when(hf == 0)
            def _(du=du, sl=sl):
                du_ref[sl, :] = du + d_ref[...] * dy_ref[sl, :]

            @pl.when(hf == 1)
            def _(du=du, sl=sl):
                du_ref[sl, :] += du
        dwbr_ref[...] = dwbr
        dwbi_ref[...] = dwbi
        if ex is not None:
            @pl.when((pl.program_id(0) == ncb - 1) & (hf == 1))
            def _():
                ex.wait(*comm)

    u_spec, wb_spec, wc_spec, lam_spec, d_spec = _s5_specs(s, ncb)
    dwb_spec = pl.BlockSpec((None, None, LANES, S5_STATES), lambda cb, hf: (cb, hf, 0, 0))
    dwc_spec = pl.BlockSpec((None, None, S5_STATES, LANES), lambda cb, hf: (cb, hf, 0, 0))
    state = pltpu.VMEM((s, S5_STATES), F32)
    return pl.pallas_call(
        kern, name=name,
        out_shape=[jax.ShapeDtypeStruct((s, w), F32),
                   jax.ShapeDtypeStruct((ncb, 2, LANES, S5_STATES), F32), jax.ShapeDtypeStruct((ncb, 2, LANES, S5_STATES), F32),
                   jax.ShapeDtypeStruct((ncb, 2, S5_STATES, LANES), F32), jax.ShapeDtypeStruct((ncb, 2, S5_STATES, LANES), F32),
                   jax.ShapeDtypeStruct((1, 4 * w), F32), jax.ShapeDtypeStruct((1, 4 * w), F32),
                   jax.ShapeDtypeStruct((1, w), F32)] + (ex.out_shape if ex else []),
        grid=(ncb, 2),
        in_specs=[u_spec, u_spec, wb_spec, wb_spec, wc_spec, wc_spec, lam_spec, lam_spec, d_spec]
        + (ex.specs if ex else []),
        out_specs=[u_spec, dwb_spec, dwb_spec, dwc_spec, dwc_spec, lam_spec, lam_spec, d_spec]
        + (ex.specs if ex else []),
        scratch_shapes=(ex.scratch if ex else []) + [state, state, state, state],
        compiler_params=_params(dimension_semantics=("arbitrary", "arbitrary"), has_side_effects=ex is not None),
    )(proj, dy, wb_re, wb_im, wc_re, wc_im, lam_re, lam_im, dskip, *ex_arrays)


def _s5_discretise(lam_re, lam_im, log_dt, b_re, b_im):
    lr = jnp.minimum(lam_re, -EIG_CLIP)
    li = lam_im
    dt = jnp.exp(log_dt)[:, None]
    mag = jnp.exp(lr * dt)
    lbr, lbi = mag * jnp.cos(li * dt), mag * jnp.sin(li * dt)
    den = lr * lr + li * li
    fr = ((lbr - 1.0) * lr + lbi * li) / den
    fi = (lbi * lr - (lbr - 1.0) * li) / den
    bbr = fr[..., None] * b_re - fi[..., None] * b_im
    bbi = fr[..., None] * b_im + fi[..., None] * b_re
    return lbr, lbi, bbr, bbi


def _s5_operand(mats, channels_first):
    g, a, b = mats.shape
    gl = LANES // 2 // SSM_H
    ncb = g // (2 * gl)
    m = mats.reshape(ncb, 2, gl, a, b)
    eye = jnp.eye(gl, dtype=mats.dtype)
    inner = (m[:, :, :, :, None, :] * eye[None, None, :, None, :, None]).reshape(ncb, 2, gl * a, gl * b)
    zeros = jnp.zeros_like(inner[:, 0])
    axis = 1 if channels_first else 2
    return jnp.stack([jnp.concatenate([inner[:, 0], zeros], axis=axis),
                      jnp.concatenate([zeros, inner[:, 1]], axis=axis)], axis=1)


def _s5_block_grads(dwb, a, b, transpose):
    ncb = dwb.shape[0]
    gl = LANES // 2 // (a if not transpose else b)
    if not transpose:
        d = dwb.reshape(ncb, 2, 2, gl, a, gl, b)
        parts = [[d[:, hf, hf, g, :, g, :] for g in range(gl)] for hf in range(2)]
    else:
        d = dwb.reshape(ncb, 2, gl, a, 2, gl, b)
        parts = [[d[:, hf, g, :, hf, g, :] for g in range(gl)] for hf in range(2)]
    st = jnp.stack([jnp.stack(p, axis=1) for p in parts], axis=1)
    return st.reshape(ncb * 2 * gl, a, b)


def _adamw(parts, w, m, v, *, name):
    depth, r, c = w.shape
    assert len(parts) == depth
    npart = parts[0].shape[0]
    row_bytes = 4 * (-(-c // LANES) * LANES)
    align = 16 if parts[0].dtype == BF else SUBLANES
    budget = VMEM_LIMIT // 2 // (2 * (depth * npart + 7) * row_bytes)
    tr = _fit(r, max(align, budget // align * align), align)
    nr = r // tr
    c1 = 1.0 / (1.0 - ADAM_B1 ** ADAM_STEP)
    c2 = 1.0 / (1.0 - ADAM_B2 ** ADAM_STEP)

    def kern(*refs):
        p_refs = refs[:depth]
        w_ref, m_ref, v_ref, g_ref, d_ref, nm_ref, nv_ref = refs[depth:]
        layer = pl.program_id(0)
        for l in range(depth):
            @pl.when(layer == l)
            def _(p_ref=p_refs[l]):
                g = p_ref[0].astype(F32)
                for q in range(1, npart):
                    g = g + p_ref[q].astype(F32)
                m2 = ADAM_B1 * m_ref[...] + (1.0 - ADAM_B1) * g
                v2 = ADAM_B2 * v_ref[...] + (1.0 - ADAM_B2) * (g * g)
                upd = (m2 * c1) / (jnp.sqrt(v2 * c2) + ADAM_EPS) + ADAM_WD * w_ref[...]
                g_ref[...] = g
                d_ref[...] = -ADAM_LR * upd
                nm_ref[...] = m2
                nv_ref[...] = v2

    def part_spec(l):
        return pl.BlockSpec((npart, tr, c),
                            lambda ly, i: (0, jnp.where(ly == l, i, jnp.where(ly < l, 0, nr - 1)), 0))

    spec = pl.BlockSpec((None, tr, c), lambda ly, i: (ly, i, 0))
    return pl.pallas_call(
        kern, name=name, out_shape=[jax.ShapeDtypeStruct((depth, r, c), F32)] * 4, grid=(depth, nr),
        in_specs=[part_spec(l) for l in range(depth)] + [spec, spec, spec],
        out_specs=[spec] * 4,
        compiler_params=_params(dimension_semantics=("arbitrary", "arbitrary")),
    )(*parts, w, m, v)


def _sum_parts(parts, *, name):
    npart, r, c = parts.shape

    def kern(p_ref, o_ref):
        g = p_ref[0]
        for q in range(1, npart):
            g = g + p_ref[q]
        o_ref[...] = g

    return pl.pallas_call(kern, name=name, out_shape=jax.ShapeDtypeStruct((r, c), F32), compiler_params=_params())(parts)


class _Exchange:
    def __init__(self, arrays, gather):
        self.n = len(arrays)
        self.gather = gather
        self.out_shape = [jax.ShapeDtypeStruct(((NDEV,) + a.shape) if gather else a.shape, a.dtype) for a in arrays]
        self.scratch = [pltpu.SemaphoreType.DMA((self.n, NDEV - 1)), pltpu.SemaphoreType.DMA((self.n, NDEV - 1)),
                        pltpu.SemaphoreType.DMA((self.n,))]
        self.specs = [pl.BlockSpec(memory_space=pl.ANY)] * self.n

    def _copies(self, srcs, dsts, sems):
        send_sems, recv_sems, local_sems = sems
        x, y, c = lax.axis_index("x"), lax.axis_index("y"), lax.axis_index("c")
        me = 4 * x + 2 * y + c
        local = [pltpu.make_async_copy(srcs[a] if self.gather else srcs[a].at[me], dsts[a].at[me], local_sems.at[a])
                 for a in range(self.n)]
        remote = []
        for k in (1, 2, 4, 3, 5, 6, 7):
            px, py, pc = x ^ ((k >> 2) & 1), y ^ ((k >> 1) & 1), c ^ (k & 1)
            peer = 4 * px + 2 * py + pc
            for a in range(self.n):
                src = srcs[a] if self.gather else srcs[a].at[peer]
                mk = functools.partial(
                    pltpu.make_async_remote_copy, src_ref=src,
                    send_sem=send_sems.at[a, k - 1], recv_sem=recv_sems.at[a, k - 1],
                    device_id=(px, py, pc), device_id_type=pl.DeviceIdType.MESH)
                remote.append((mk(dst_ref=dsts[a].at[me]), mk(dst_ref=dsts[a].at[peer])))
        return local, remote

    def _gather_copies(self, srcs, dsts, sems):
        send_sems, recv_sems, local_sems = sems
        x, y, c = lax.axis_index("x"), lax.axis_index("y"), lax.axis_index("c")
        block = lambda px, py, pc: 4 * px + 2 * py + pc
        me = block(x, y, c)
        chips = [(1 - x, y), (x, 1 - y), (1 - x, 1 - y)]
        local = [pltpu.make_async_copy(srcs[a], dsts[a].at[me], local_sems.at[a]) for a in range(self.n)]
        own, passed = [], []
        for a in range(self.n):
            def copy(k, blk, to, src=None, a=a):
                return pltpu.make_async_remote_copy(
                    src_ref=dsts[a].at[blk] if src is None else src, dst_ref=dsts[a].at[blk],
                    send_sem=send_sems.at[a, k], recv_sem=recv_sems.at[a, k],
                    device_id=to, device_id_type=pl.DeviceIdType.MESH)
            sib = (x, y, 1 - c)
            own.append((copy(0, me, sib, srcs[a]), copy(0, block(x, y, 1 - c), sib)))
            for j, (px, py) in enumerate(chips):
                own.append((copy(1 + j, me, (px, py, c), srcs[a]), copy(1 + j, block(px, py, c), (px, py, c))))
            for j, (px, py) in enumerate(chips):
                passed.append((copy(4 + j, block(px, py, c), sib), copy(4 + j, block(px, py, 1 - c), sib)))
        return local, own, passed

    def start(self, srcs, dsts, sems):
        if self.gather:
            local, own, _ = self._gather_copies(srcs, dsts, sems)
            for cp in local:
                cp.start()
            for send, _ in own:
                send.start()
            return
        local, remote = self._copies(srcs, dsts, sems)
        for cp in local:
            cp.start()
        for send, _ in remote:
            send.start()

    def forward(self, srcs, dsts, sems):
        if not self.gather:
            return
        _, own, passed = self._gather_copies(srcs, dsts, sems)
        for a in range(self.n):
            for j in range(3):
                own[4 * a + 1 + j][1].wait_recv()
                passed[3 * a + j][0].start()

    def wait(self, srcs, dsts, sems):
        if self.gather:
            local, own, passed = self._gather_copies(srcs, dsts, sems)
            for a in range(self.n):
                own[4 * a][1].wait_recv()
            for _, arrival in passed:
                arrival.wait_recv()
            for send, _ in own + passed:
                send.wait_send()
            for cp in local:
                cp.wait()
            return
        local, remote = self._copies(srcs, dsts, sems)
        for send, arrival in remote:
            send.wait_send()
            arrival.wait_recv()
        for cp in local:
            cp.wait()


def _exchange(arrays, gather, *, name):
    ex = _Exchange(arrays, gather)
    n = ex.n

    def kern(*refs):
        srcs, dsts, sems = refs[:n], refs[n:2 * n], refs[2 * n:]
        ex.start(srcs, dsts, sems)
        ex.forward(srcs, dsts, sems)
        ex.wait(srcs, dsts, sems)

    return pl.pallas_call(
        kern, name=name, out_shape=ex.out_shape, in_specs=ex.specs, out_specs=ex.specs, scratch_shapes=ex.scratch,
        compiler_params=pltpu.CompilerParams(has_side_effects=True),
    )(*arrays)


def _pack(arrays):
    flat = jnp.concatenate([a.reshape(-1).astype(F32) for a in arrays])
    pad = (-flat.shape[0]) % (SUBLANES * LANES)
    return jnp.pad(flat, (0, pad)).reshape(-1, LANES)


def _unpack(buf, like):
    flat = buf.reshape(-1)
    out, off = [], 0
    for a in like:
        sz = math.prod(a.shape)
        out.append(flat[off:off + sz].reshape(a.shape))
        off += sz
    return out


def _row(v):
    return v.reshape(1, -1)


def _layer_fwd(x, mod, p, l, ride=None, on_receive=None):
    s, d = x.shape
    sw = d // 2
    nh = d // LANES
    shift_m, scale_m, gate_m, shift_f, scale_f, gate_f = mod
    n = lambda tag: f"{tag}{l}"
    sv = {}

    h1, = _rowwise(lambda xv, g, sc, sh: (xv * _rms(xv) * g) * (1.0 + sc) + sh,
                   [x], [p['g_pre_mix'], scale_m, shift_m], [(d, BF, 'tile')], name=n("pre_mix"))
    proj_a = _mm(h1, p['w_in_a'], name=n("proj_a"))
    flog = _mm(h1, p['w_in_f'], name=n("proj_f"))
    gates = _mm(h1, p['w_in_g'], name=n("proj_g"))

    y_s5 = _s5_fwd(proj_a, p['wb_re'], p['wb_im'], p['wc_re'], p['wc_im'], p['lamb_re'], p['lamb_im'], p['d_skip'],
                   name=n("s5_fwd"))
    z, = _rowwise(_gelu, [y_s5], [], [(sw, BF, 'tile')], name=n("gelu"))
    tglu = _mm(z, p['w_glu'], name=n("glu_mm"))
    ys, = _rowwise(lambda yv, tv, b: _gelu(yv) * _sigmoid(tv + b), [y_s5, tglu], [p['b_glu']], [(sw, BF, 'tile')],
                   name=n("glu"))

    cumx = _cum_fwd(flog, p['b_f_row'], nh, name=n("cum_fwd"))
    ya, lse, *received = _attn_fwd(proj_a, cumx, sw, name=n("attn_fwd"), ride=ride)
    if on_receive is not None:
        on_receive(received)

    am = _mm(ys, p['w_pa'], name=n("pa_mm"))
    bm, merged = _mm_fused(ya, [p['w_pb']], [am, (gates, 0), (gates, 1)],
                           lambda b, a, ga, gb: (b, _sigmoid(ga) * a + _sigmoid(gb) * b), [F32, BF],
                           name=n("pb_mm"))
    ym = _mm(merged, p['w_o'], name=n("o_mm"))
    x2, = _rowwise(lambda xv, yv, g, gt: xv + gt * (yv * _rms(yv) * g),
                   [x, ym], [p['g_post_mix'], gate_m], [(d, F32, 'tile')], name=n("post_mix"))

    h2, = _rowwise(lambda xv, g, sc, sh: (xv * _rms(xv) * g) * (1.0 + sc) + sh,
                   [x2], [p['g_pre_ffn'], scale_f, shift_f], [(d, BF, 'tile')], name=n("pre_ffn"))
    gt, up, act = _mm_fused(h2, [p['w_ffn_gate'], p['w_ffn_up']], [], lambda g, u: (g, u, _silu(g) * u),
                            [F32, F32, BF], name=n("gate_up_mm"))
    yf = _mm(act, p['w_ffn_down'], name=n("down_mm"))
    x3, = _rowwise(lambda xv, yv, g, gt_: xv + gt_ * (yv * _rms(yv) * g),
                   [x2, yf], [p['g_post_ffn'], gate_f], [(d, F32, 'tile')], name=n("post_ffn"))

    sv.update(x=x, h1=h1, proj_a=proj_a, flog=flog, gates=gates, y_s5=y_s5, z=z, tglu=tglu, ys=ys, cumx=cumx,
              ya=ya, lse=lse, am=am, bm=bm, merged=merged, ym=ym, x2=x2, h2=h2, gt=gt, up=up,
              act=act, yf=yf)
    return x3, sv


def _layer_bwd(dx3, sv, mod, p, l, make_ride=None, on_receive=None, carried=None, defer_tail=False):
    x, x2 = sv['x'], sv['x2']
    s, d = x.shape
    sw = d // 2
    nh = d // LANES
    shift_m, scale_m, gate_m, shift_f, scale_f, gate_f = mod
    n = lambda tag: f"{tag}{l}"
    gw, gs = {}, {}

    def post_bwd(dxo, yv, g, gate):
        r = _rms(yv)
        nf = yv * r
        dn = dxo * gate * g
        return _norm_bwd(dn, nf, r), dxo * (nf * g), dxo * gate * nf

    def pre_bwd(dh, dres, xv, g, sc):
        r = _rms(xv)
        xh = xv * r
        n3 = xh * g
        dn3 = dh * (1.0 + sc)
        return dres + _norm_bwd(dn3 * g, xh, r), dh, dh * n3, dn3 * xh

    dyf, dgate_f, gs['g_post_ffn'] = _rowwise(
        post_bwd, [dx3, sv['yf']], [p['g_post_ffn'], gate_f],
        [(d, BF, 'tile'), (d, F32, 'sum'), (d, F32, 'sum')], name=n("post_ffn_bwd"))
    gw['w_ffn_down'] = _mm(sv['act'], dyf, ta=True, out_dtype=BF, name=n("down_bwd_w"))

    def swiglu_bwd(da, g, u):
        sg = _sigmoid(g)
        return da * u * (sg * (1.0 + g * (1.0 - sg))), da * (g * sg)

    dgt, dup = _mm_fused(dyf, [p['w_ffn_down']], [sv['gt'], sv['up']], swiglu_bwd, [BF, BF], tb=True,
                         name=n("down_bwd_x"))
    dh2a = _mm(dgt, p['w_ffn_gate'], tb=True, name=n("gate_bwd_x"))
    dh2b = _mm(dup, p['w_ffn_up'], tb=True, name=n("up_bwd_x"))
    gw['w_ffn_gate'] = _mm(sv['h2'], dgt, ta=True, out_dtype=BF, name=n("gate_bwd_w"))
    gw['w_ffn_up'] = _mm(sv['h2'], dup, ta=True, out_dtype=BF, name=n("up_bwd_w"))
    dx2, dshift_f, dscale_f, gs['g_pre_ffn'] = _rowwise(
        lambda da, db, dres, xv, g, sc: pre_bwd(da + db, dres, xv, g, sc),
        [dh2a, dh2b, dx3, x2], [p['g_pre_ffn'], scale_f],
        [(d, F32, 'tile'), (d, F32, 'sum'), (d, F32, 'sum'), (d, F32, 'sum')], name=n("pre_ffn_bwd"))

    dym, dgate_m, gs['g_post_mix'] = _rowwise(
        post_bwd, [dx2, sv['ym']], [p['g_post_mix'], gate_m],
        [(d, BF, 'tile'), (d, F32, 'sum'), (d, F32, 'sum')], name=n("post_mix_bwd"))
    gw['w_o'] = _mm(sv['merged'], dym, ta=True, out_dtype=BF, name=n("o_bwd_w"))

    def merge_bwd(dm, a, b, ga, gb):
        sa, sb = _sigmoid(ga), _sigmoid(gb)
        return dm * sa, dm * sb, dm * a * sa * (1.0 - sa), dm * b * sb * (1.0 - sb)

    da_, db_, dga, dgb = _mm_fused(dym, [p['w_o']], [sv['am'], sv['bm'], (sv['gates'], 0), (sv['gates'], 1)],
                                   merge_bwd, [BF] * 4, tb=True, name=n("o_bwd_x"))
    dys = _mm(da_, p['w_pa'], tb=True, name=n("pa_bwd_x"))
    gw['w_pa'] = _mm(sv['ys'], da_, ta=True, out_dtype=BF, name=n("pa_bwd_w"))
    dya = _mm(db_, p['w_pb'], tb=True, name=n("pb_bwd_x"))
    gw['w_pb'] = _mm(sv['ya'], db_, ta=True, out_dtype=BF, name=n("pb_bwd_w"))

    sent = list(gw)
    dq, dk, dv, dkc, dqc, *received = _attn_bwd(
        sv['proj_a'], dya, sv['ya'], sv['lse'], sv['cumx'], sw, name=n("attn_bwd"),
        ride=make_ride({k: gw[k] for k in sent}) if make_ride is not None else None)
    if on_receive is not None:
        on_receive(sent, received)
    frow = sv['flog'][:, :nh].T
    dcum = jnp.stack([-dkc.reshape(nh, s), dqc.reshape(nh, s)])
    dfrow, dbf = _cum_bwd(dcum, frow, p['b_f_col'], name=n("cum_bwd"))
    gs['b_f'] = dbf.reshape(nh)
    dflog = jnp.pad(dfrow.T, ((0, 0), (0, LANES - nh))).astype(BF)

    def glu_bwd(dy_, yv, tv, b):
        zv = _gelu(yv)
        sg = _sigmoid(tv + b)
        dt = dy_ * zv * sg * (1.0 - sg)
        return dt, dy_ * sg, dt

    dt, dz1, gs['b_glu'] = _rowwise(glu_bwd, [dys, sv['y_s5'], sv['tglu']], [p['b_glu']],
                                    [(sw, BF, 'tile'), (sw, F32, 'tile'), (sw, F32, 'sum')], name=n("glu_bwd"))
    dz2 = _mm(dt, p['w_glu'], tb=True, name=n("glu_bwd_x"))
    gw['w_glu'] = _mm(sv['z'], dt, ta=True, out_dtype=BF, name=n("glu_bwd_w"))
    dy_s5, = _rowwise(lambda a, b, yv: (a + b) * _gelu_grad(yv), [dz1, dz2, sv['y_s5']], [], [(sw, F32, 'tile')],
                      name=n("gelu_bwd"))
    du, dwbr, dwbi, dwcr, dwci, dlr, dli, gs['d_skip'], *received = _s5_bwd(
        sv['proj_a'], dy_s5, p['wb_re'], p['wb_im'], p['wc_re'], p['wc_im'], p['lamb_re'], p['lamb_im'], p['d_skip'],
        name=n("s5_bwd"), ride=make_ride(carried[1]) if carried else None)
    if carried:
        carried[0](list(carried[1]), received)
    g_ = sw // SSM_H
    pst = p['lamb_re'].shape[1] // g_
    gs['lamb_re'], gs['lamb_im'] = dlr.reshape(g_, pst), dli.reshape(g_, pst)
    gs['bbar_re'] = _s5_block_grads(dwbr, SSM_H, pst, False).transpose(0, 2, 1)
    gs['bbar_im'] = _s5_block_grads(dwbi, SSM_H, pst, False).transpose(0, 2, 1)
    gs['c_re'] = _s5_block_grads(dwcr, pst, SSM_H, True).transpose(0, 2, 1)
    gs['c_im'] = _s5_block_grads(dwci, pst, SSM_H, True).transpose(0, 2, 1)

    dproj = jnp.concatenate([du.astype(BF), dq.astype(BF), dk.astype(BF), dv.astype(BF), dflog, dga, dgb], axis=1)
    gw['w_in'] = _mm(sv['h1'], dproj, ta=True, out_dtype=BF, tn=1408, name=n("proj_bwd_w"))
    if make_ride is not None:
        gw = {k: g for k, g in gw.items() if k not in sent}
    if make_ride is not None and not defer_tail:
        dh1, received = _mm(dproj, p['w_in_all'], tb=True, tk=1408, name=n("proj_bwd_x"), ride=make_ride(gw))
        on_receive(list(gw), received)
        gw = {}
    else:
        dh1 = _mm(dproj, p['w_in_all'], tb=True, tk=1408, name=n("proj_bwd_x"))
    dx, dshift_m, dscale_m, gs['g_pre_mix'] = _rowwise(
        pre_bwd, [dh1, dx2, x], [p['g_pre_mix'], scale_m],
        [(d, F32, 'tile'), (d, F32, 'sum'), (d, F32, 'sum'), (d, F32, 'sum')], name=n("pre_mix_bwd"))
    dmod = [dshift_m, dscale_m, dgate_m, dshift_f, dscale_f, dgate_f]
    return dx, gw, dmod, gs


def _unshard(k, blocks):
    if k in COL_SHARDED:
        return blocks.transpose(1, 0, 2).reshape(blocks.shape[1], NDEV * blocks.shape[2])
    return blocks.reshape(NDEV * blocks.shape[1], blocks.shape[2])


def _to_slabs(k, g):
    if k == 'w_in':
        d = g.shape[0]
        nh = d // LANES
        g = jnp.concatenate([g[:, :2 * d + nh], g[:, 2 * d + LANES:]], axis=1)
    if k in COL_SHARDED:
        return g.reshape(g.shape[0], NDEV, g.shape[1] // NDEV).transpose(1, 0, 2)
    return g.reshape(NDEV, g.shape[0] // NDEV, g.shape[1])


def _prep_w_in(w_in):
    d = w_in.shape[0]
    nh = d // LANES
    fcol = 2 * d
    p = {}
    p['w_in_a'] = w_in[:, :fcol]
    p['w_in_f'] = jnp.pad(w_in[:, fcol:fcol + nh], ((0, 0), (0, LANES - nh)))
    p['w_in_g'] = w_in[:, fcol + nh:]
    p['w_in_all'] = jnp.concatenate([p['w_in_a'], p['w_in_f'], p['w_in_g']], axis=1)
    return p


def _prep_small(small):
    nh = small['b_f'].shape[0]
    p = {}
    for k in ('g_pre_mix', 'g_post_mix', 'g_pre_ffn', 'g_post_ffn', 'd_skip', 'b_glu'):
        p[k] = _row(small[k])
    p['b_f_row'] = jnp.pad(_row(small['b_f']), ((0, 0), (0, LANES - nh)))
    p['b_f_col'] = small['b_f'].reshape(nh, 1)
    lbr, lbi, bbr, bbi = _s5_discretise(small['lam_re'], small['lam_im'], small['log_dt'], small['b_re'], small['b_im'])
    p['lamb_re'], p['lamb_im'] = _row(lbr), _row(lbi)
    p['wb_re'] = _s5_operand(bbr.transpose(0, 2, 1), True).astype(BF)
    p['wb_im'] = _s5_operand(bbi.transpose(0, 2, 1), True).astype(BF)
    p['wc_re'] = _s5_operand(small['c_re'].transpose(0, 2, 1), False).astype(BF)
    p['wc_im'] = _s5_operand(small['c_im'].transpose(0, 2, 1), False).astype(BF)
    return p


def _local_step(x, target, mods, ps, small, hooks=None):
    depth = len(ps)
    s, d = x.shape
    hooks = hooks or {}
    saved = []
    h = x
    for l in range(depth):
        h, sv = _layer_fwd(h, mods[l], ps[l], l, ride=hooks['fwd_ride'](l) if hooks else None,
                           on_receive=functools.partial(hooks['fwd_recv'], l) if hooks else None)
        saved.append(sv)

    def loss_fn(yv, tv):
        e = yv - tv
        return e * (1.0 / d), jnp.sum(e * e, axis=1, keepdims=True) * (0.5 / d)

    dy, loss = _rowwise(loss_fn, [h, target], [], [(d, F32, 'tile'), (1, F32, 'sum')], name="loss")
    dmods, gss = [None] * depth, [None] * depth
    unsent = {}
    carried = None
    for l in range(depth - 1, -1, -1):
        def on_receive(names, results, l=l):
            hooks['bwd_recv']([(k, l) for k in names], results)

        dy, gw, dmods[l], gs = _layer_bwd(dy, saved[l], mods[l], ps[l], l,
                                          make_ride=hooks['bwd_ride'] if hooks else None,
                                          on_receive=on_receive if hooks else None,
                                          carried=carried, defer_tail=bool(hooks) and l > 0)
        if hooks and l > 0:
            carried = (on_receive, gw)
        else:
            unsent.update({(k, l): g for k, g in gw.items()})
        sm = small[l]
        _, vjp = jax.vjp(_s5_discretise, sm['lam_re'], sm['lam_im'], sm['log_dt'], sm['b_re'], sm['b_im'])
        gs['lam_re'], gs['lam_im'], gs['log_dt'], gs['b_re'], gs['b_im'] = vjp(
            (gs.pop('lamb_re'), gs.pop('lamb_im'), gs.pop('bbar_re'), gs.pop('bbar_im')))
        gss[l] = gs
    return loss, dy, unsent, dmods, gss


SMALL_LOCAL = ['g_pre_mix', 'g_post_mix', 'g_pre_ffn', 'g_post_ffn', 'lam_re', 'lam_im', 'log_dt', 'b_re', 'b_im',
               'c_re', 'c_im', 'd_skip', 'b_glu', 'b_f']


def kernel(x, c, w_ada, b_ada, g_pre_mix, g_post_mix, g_pre_ffn, g_post_ffn, w_in, lam_re, lam_im, log_dt, b_re, b_im, c_re, c_im, d_skip, w_glu, b_glu, b_f, w_pa, w_pb, w_o, w_ffn_gate, w_ffn_up, w_ffn_down, loss_target, m_w_ada, m_b_ada, m_g_pre_mix, m_g_post_mix, m_g_pre_ffn, m_g_post_ffn, m_w_in, m_lam_re, m_lam_im, m_log_dt, m_b_re, m_b_im, m_c_re, m_c_im, m_d_skip, m_w_glu, m_b_glu, m_b_f, m_w_pa, m_w_pb, m_w_o, m_w_ffn_gate, m_w_ffn_up, m_w_ffn_down, v_w_ada, v_b_ada, v_g_pre_mix, v_g_post_mix, v_g_pre_ffn, v_g_post_ffn, v_w_in, v_lam_re, v_lam_im, v_log_dt, v_b_re, v_b_im, v_c_re, v_c_im, v_d_skip, v_w_glu, v_b_glu, v_b_f, v_w_pa, v_w_pb, v_w_o, v_w_ffn_gate, v_w_ffn_up, v_w_ffn_down):
    args = dict(locals())
    W = {k: args[k] for k in WEIGHTS}
    M = {k: args['m_' + k] for k in WEIGHTS}
    V = {k: args['v_' + k] for k in WEIGHTS}
    depth, d, ncol = w_ada.shape
    s = x.shape[1]
    me = 4 * lax.axis_index("x") + 2 * lax.axis_index("y") + lax.axis_index("c")

    c_all, = _exchange([jnp.pad(c, ((0, SUBLANES - 1), (0, 0)))], True, name="gather_c")
    c_all = c_all[:, 0, :]
    cond, = _rowwise(_silu, [c_all], [], [(d, F32, 'tile')], name="cond")
    mod_part = jnp.stack([_mm(cond, w_ada[l], name=f"ada_mm{l}") for l in range(depth)], axis=1)
    mod_recv, = _exchange([mod_part.reshape(NDEV, depth, 1, ncol)], False, name="scatter_mod")
    mod_cat = mod_recv.reshape(NDEV, depth, ncol).transpose(1, 0, 2).reshape(depth, NDEV * ncol)
    mod, = _rowwise(lambda a, b: a + b, [mod_cat, b_ada], [], [(NDEV * ncol, F32, 'tile')], name="mod_bias")
    mods = [[mod[l:l + 1, i * d:(i + 1) * d] for i in range(6)] for l in range(depth)]

    small = [{k: W[k][l] for k in SMALL_LOCAL} for l in range(depth)]
    ps = [_prep_small(small[l]) for l in range(depth)]
    first = ['w_in', 'w_glu']
    rest = [k for k in BIG if k not in first]
    riding = [[(k, l) for k in rest] + [(k, l + 1) for k in first if l + 1 < depth] for l in range(depth)]

    def take_weights(keys, results):
        for (k, l), blocks in zip(keys, results):
            full = _unshard(k, blocks)
            ps[l].update(_prep_w_in(full) if k == 'w_in' else {k: full})

    take_weights([(k, 0) for k in first],
                 _exchange([W[k][0].astype(BF) for k in first], True, name="gather_w_first"))

    def fwd_ride(l):
        blocks = [W[k][ll].astype(BF) for k, ll in riding[l]]
        return _Exchange(blocks, True), blocks

    grad_parts = {}

    def bwd_ride(grads):
        slabs = [_to_slabs(k, g) for k, g in grads.items()]
        return _Exchange(slabs, False), slabs

    hooks = dict(fwd_ride=fwd_ride, fwd_recv=lambda l, results: take_weights(riding[l], results),
                 bwd_ride=bwd_ride, bwd_recv=lambda keys, results: grad_parts.update(zip(keys, results)))

    loss, dx, unsent, dmods, gss = _local_step(x[0], loss_target[0], mods, ps, small, hooks)
    assert not unsent
    loss = lax.psum(loss[0, 0], ("x", "y", "c"))
    out = {}
    for k in BIG:
        out[k] = _adamw([grad_parts[(k, l)] for l in range(depth)], W[k], M[k], V[k], name=f"adamw_{k}")

    dmod_mine = jnp.stack([jnp.concatenate(dmods[l], axis=1)[0] for l in range(depth)])
    small_mine = [dmod_mine] + [jnp.stack([gss[l][k] for l in range(depth)]) for k in SMALL_LOCAL]
    parts, = _exchange([_pack(small_mine)], True, name="gather_small")
    summed = _sum_parts(parts, name="sum_small")
    names = ['b_ada'] + SMALL_LOCAL
    for k, g in zip(names, _unpack(summed, [W[k] for k in names])):
        shp = W[k].shape
        rows = lambda a: a.reshape(depth, -1, shp[-1])
        res = _adamw([rows(g)[l][None] for l in range(depth)], rows(W[k]), rows(M[k]), rows(V[k]), name=f"adamw_{k}")
        out[k] = [a.reshape(shp) for a in res]

    dmod_all = parts.reshape(NDEV, -1)[:, :depth * 6 * d].reshape(NDEV, depth, 6 * d)
    dmod_cols = lax.dynamic_slice_in_dim(dmod_all, me * ncol, ncol, axis=2)
    g_ada = [_mm(cond, dmod_cols[:, l], ta=True, precision=HI, name=f"ada_bwd{l}")[None] for l in range(depth)]
    out['w_ada'] = _adamw(g_ada, w_ada, m_w_ada, v_w_ada, name="adamw_w_ada")

    return (loss, dx[None], *[out[k][0] for k in WEIGHTS], *[out[k][1] for k in WEIGHTS],
            *[out[k][2] for k in WEIGHTS], *[out[k][3] for k in WEIGHTS])
```

```python
import functools
import math

import jax
import jax.numpy as jnp
from jax import lax
from jax.experimental import pallas as pl
from jax.experimental.pallas import tpu as pltpu

F32 = jnp.float32
BF = jnp.bfloat16
NDEV = 8
LANES = 128
SUBLANES = 8
VMEM_LIMIT = 48 * 1024 * 1024

SSM_H = 16
HEAD_DIM = 64
RMS_EPS = 1e-6
EIG_CLIP = 1e-4
ADAM_LR = 0.001
ADAM_B1 = 0.9
ADAM_B2 = 0.999
ADAM_EPS = 1e-08
ADAM_WD = 0.01
ADAM_STEP = 10
NEG = -1e30
HI = lax.Precision.HIGHEST

WEIGHTS = ['w_ada', 'b_ada', 'g_pre_mix', 'g_post_mix', 'g_pre_ffn', 'g_post_ffn', 'w_in', 'lam_re', 'lam_im',
           'log_dt', 'b_re', 'b_im', 'c_re', 'c_im', 'd_skip', 'w_glu', 'b_glu', 'b_f', 'w_pa', 'w_pb', 'w_o',
           'w_ffn_gate', 'w_ffn_up', 'w_ffn_down']
COL_SHARDED = ['w_in', 'w_pa', 'w_pb', 'w_ffn_gate', 'w_ffn_up']
ROW_SHARDED = ['w_glu', 'w_o', 'w_ffn_down']
BIG = COL_SHARDED + ROW_SHARDED
SMALL = ['b_ada', 'g_pre_mix', 'g_post_mix', 'g_pre_ffn', 'g_post_ffn', 'lam_re', 'lam_im', 'log_dt', 'b_re',
         'b_im', 'c_re', 'c_im', 'd_skip', 'b_glu', 'b_f']


def _fit(dim, target, align):
    if dim <= target:
        return dim
    t = (target // align) * align
    while t >= align:
        if dim % t == 0:
            return t
        t -= align
    return dim


def _params(**kw):
    return pltpu.CompilerParams(vmem_limit_bytes=VMEM_LIMIT, **kw)


def _mm(a, b, *, ta=False, tb=False, out_dtype=F32, tm=None, tn=512, tk=2048, precision=None, name, ride=None,
        second=None):
    m, k = (a.shape[1], a.shape[0]) if ta else a.shape
    n = b.shape[0] if tb else b.shape[1]
    assert (b.shape[1] if tb else b.shape[0]) == k
    tm = _fit(m, tm or (1024 if ta else 2048), LANES if ta else 16)
    tn = _fit(n, tn, LANES)
    tk = _fit(k, tk, LANES)
    nk = k // tk
    grid = (m // tm, n // tn, nk)
    dims = (((0 if ta else 1,), (1 if tb else 0,)), ((), ()))
    ex, ex_arrays = ride if ride is not None else (None, [])

    pairs = [(a, b)] + ([second] if second is not None else [])

    def kern(*refs):
        ab_refs, (o_ref,), comm, scratch = _ride_split(ex, refs, 2 * len(pairs), 1)
        step = (pl.program_id(0) * grid[1] + pl.program_id(1)) * grid[2] + pl.program_id(2)
        if ex is not None:
            @pl.when(step == 0)
            def _():
                ex.start(*comm)

            @pl.when(step == (grid[0] * grid[1] * grid[2]) // 2)
            def _():
                ex.forward(*comm)

        p = None
        for a_ref, b_ref in zip(ab_refs[::2], ab_refs[1::2]):
            av, bv = a_ref[...], b_ref[...]
            if precision is None:
                av, bv = av.astype(BF), bv.astype(BF)
            q = lax.dot_general(av, bv, dims, preferred_element_type=F32, precision=precision)
            p = q if p is None else p + q
        if nk == 1:
            o_ref[...] = p.astype(out_dtype)
        else:
            acc_ref, = scratch
            kk = pl.program_id(2)

            @pl.when(kk == 0)
            def _():
                acc_ref[...] = p

            @pl.when(kk > 0)
            def _():
                acc_ref[...] += p

            @pl.when(kk == nk - 1)
            def _():
                o_ref[...] = acc_ref[...].astype(out_dtype)

        if ex is not None:
            @pl.when(step == grid[0] * grid[1] * grid[2] - 1)
            def _():
                ex.wait(*comm)

    a_spec = pl.BlockSpec((tk, tm), lambda i, j, kk: (kk, i)) if ta else pl.BlockSpec((tm, tk), lambda i, j, kk: (i, kk))
    b_spec = pl.BlockSpec((tn, tk), lambda i, j, kk: (j, kk)) if tb else pl.BlockSpec((tk, tn), lambda i, j, kk: (kk, j))
    res = pl.pallas_call(
        kern, name=name,
        out_shape=[jax.ShapeDtypeStruct((m, n), out_dtype)] + (ex.out_shape if ex else []),
        grid=grid,
        in_specs=[a_spec, b_spec] * len(pairs) + (ex.specs if ex else []),
        out_specs=[pl.BlockSpec((tm, tn), lambda i, j, kk: (i, j))] + (ex.specs if ex else []),
        scratch_shapes=(ex.scratch if ex else []) + ([] if nk == 1 else [pltpu.VMEM((tm, tn), F32)]),
        compiler_params=_params(dimension_semantics=("arbitrary",) * 3 if ex else ("parallel", "parallel", "arbitrary"),
                                has_side_effects=ex is not None),
    )(*[x for pair in pairs for x in pair], *ex_arrays)
    return (res[0], res[1:]) if ex else res[0]


def _mm_fused(a, bs, extras, fn, out_dtypes, *, tb=False, tm=2048, tn=256, name):
    m, k = a.shape
    n = bs[0].shape[0] if tb else bs[0].shape[1]
    tm = _fit(m, tm, 16)
    tn = _fit(n, tn, LANES)
    extras = [e if isinstance(e, tuple) else (e, 0) for e in extras]
    nb, ne = len(bs), len(extras)
    dims = (((1,), (1 if tb else 0,)), ((), ()))

    def kern(*refs):
        av = refs[0][...].astype(BF)
        prods = [lax.dot_general(av, r[...].astype(BF), dims, preferred_element_type=F32) for r in refs[1:1 + nb]]
        res = fn(*prods, *[r[...] for r in refs[1 + nb:1 + nb + ne]])
        for o_ref, r, dt in zip(refs[1 + nb + ne:], res, out_dtypes):
            o_ref[...] = r.astype(dt)

    tile = pl.BlockSpec((tm, tn), lambda i, j: (i, j))
    b_spec = pl.BlockSpec((tn, k), lambda i, j: (j, 0)) if tb else pl.BlockSpec((k, tn), lambda i, j: (0, j))
    return pl.pallas_call(
        kern, name=name, out_shape=[jax.ShapeDtypeStruct((m, n), dt) for dt in out_dtypes],
        grid=(m // tm, n // tn),
        in_specs=[pl.BlockSpec((tm, k), lambda i, j: (i, 0))] + [b_spec] * nb
        + [pl.BlockSpec((tm, tn), lambda i, j, c=c: (i, j + c * (n // tn))) for _, c in extras],
        out_specs=[tile] * len(out_dtypes),
        compiler_params=_params(dimension_semantics=("parallel", "parallel")),
    )(a, *bs, *[e for e, _ in extras])


def _rowwise(fn, tiles, params, outs, *, tr=256, name):
    tiles = [t if isinstance(t, tuple) else (t, t.shape[1], 0) for t in tiles]
    s = tiles[0][0].shape[0]
    tr = _fit(s, tr, 16)
    nt, npar = len(tiles), len(params)

    def kern(*refs):
        i = pl.program_id(0)
        res = fn(*[r[...] for r in refs[:nt + npar]])
        if not isinstance(res, (tuple, list)):
            res = (res,)
        for (w, dt, kind), o_ref, r in zip(outs, refs[nt + npar:], res):
            if kind == 'tile':
                o_ref[...] = r.astype(dt)
            else:
                part = jnp.sum(r.astype(F32), axis=0, keepdims=True)

                @pl.when(i == 0)
                def _(o_ref=o_ref, part=part):
                    o_ref[...] = part

                @pl.when(i > 0)
                def _(o_ref=o_ref, part=part):
                    o_ref[...] += part

    def tile_spec(w, cb):
        return pl.BlockSpec((tr, w), lambda i: (i, cb))

    in_specs = [tile_spec(w, cb) for _, w, cb in tiles]
    in_specs += [pl.BlockSpec(p.shape, lambda i, nd=p.ndim: (0,) * nd) for p in params]
    out_shape, out_specs = [], []
    for w, dt, kind in outs:
        if kind == 'tile':
            out_shape.append(jax.ShapeDtypeStruct((s, w), dt))
            out_specs.append(pl.BlockSpec((tr, w), lambda i: (i, 0)))
        else:
            out_shape.append(jax.ShapeDtypeStruct((1, w), F32))
            out_specs.append(pl.BlockSpec((1, w), lambda i: (0, 0)))
    res = pl.pallas_call(
        kern, name=name, out_shape=out_shape, grid=(s // tr,), in_specs=in_specs, out_specs=out_specs,
        compiler_params=_params(dimension_semantics=("arbitrary",)),
    )(*[t[0] for t in tiles], *params)
    return res


def _sigmoid(z):
    return 1.0 / (1.0 + jnp.exp(-z))


def _silu(z):
    return z * _sigmoid(z)


_GELU_K = math.sqrt(2.0 / math.pi)


def _gelu(y):
    return 0.5 * y * (1.0 + jnp.tanh(_GELU_K * (y + 0.044715 * y * y * y)))


def _gelu_grad(y):
    th = jnp.tanh(_GELU_K * (y + 0.044715 * y * y * y))
    return 0.5 * (1.0 + th) + 0.5 * y * (1.0 - th * th) * _GELU_K * (1.0 + 3.0 * 0.044715 * y * y)


def _rms(x):
    return lax.rsqrt(jnp.mean(x * x, axis=-1, keepdims=True) + RMS_EPS)


def _norm_bwd(dn, xhat, r):
    return r * (dn - xhat * jnp.mean(dn * xhat, axis=-1, keepdims=True))


def _cum_fwd(flog, bf_row, nh, *, name):
    s = flog.shape[0]
    w = nh * HEAD_DIM
    t = _fit(s, 256, SUBLANES)

    def kern(f_ref, b_ref, o_ref, carry_ref):
        i = pl.program_id(0)

        @pl.when(i == 0)
        def _():
            carry_ref[...] = jnp.zeros_like(carry_ref)

        z = f_ref[...] + b_ref[...]
        logf = jnp.minimum(z, 0.0) - jnp.log(1.0 + jnp.exp(-jnp.abs(z)))
        hh = lax.broadcasted_iota(jnp.int32, (LANES, w), 0)
        cc = lax.broadcasted_iota(jnp.int32, (LANES, w), 1)
        expand = (cc // HEAD_DIM == hh).astype(F32)
        lx = jnp.dot(logf, expand, preferred_element_type=F32, precision=HI)
        rr = lax.broadcasted_iota(jnp.int32, (t, t), 0)
        kk = lax.broadcasted_iota(jnp.int32, (t, t), 1)
        tri = (kk <= rr).astype(F32)
        cum = jnp.dot(tri, lx, preferred_element_type=F32, precision=HI) + carry_ref[...]
        o_ref[...] = cum
        carry_ref[...] = cum[t - 1:t, :]

    return pl.pallas_call(
        kern, name=name, out_shape=jax.ShapeDtypeStruct((s, w), F32), grid=(s // t,),
        in_specs=[pl.BlockSpec((t, LANES), lambda i: (i, 0)), pl.BlockSpec((1, LANES), lambda i: (0, 0))],
        out_specs=pl.BlockSpec((t, w), lambda i: (i, 0)),
        scratch_shapes=[pltpu.VMEM((1, w), F32)],
        compiler_params=_params(dimension_semantics=("arbitrary",)),
    )(flog, bf_row)


def _cum_bwd(dcrow, frow, bf_col, *, name):
    _, nh, s = dcrow.shape
    t = _fit(s, 512, LANES)
    nb = s // t

    def kern(d_ref, f_ref, b_ref, df_ref, db_ref):
        rr = lax.broadcasted_iota(jnp.int32, (t, t), 0)
        kk = lax.broadcasted_iota(jnp.int32, (t, t), 1)
        upper = (rr >= kk).astype(F32)
        carry = jnp.zeros((nh, 1), F32)
        db = jnp.zeros((nh, 1), F32)
        for blk in range(nb - 1, -1, -1):
            sl = slice(blk * t, (blk + 1) * t)
            rc = jnp.dot(d_ref[0, :, sl] + d_ref[1, :, sl], upper, preferred_element_type=F32, precision=HI) + carry
            carry = rc[:, 0:1]
            df = rc * _sigmoid(-(f_ref[:, sl] + b_ref[...]))
            df_ref[:, sl] = df
            db = db + jnp.sum(df, axis=1, keepdims=True)
        db_ref[...] = db

    return pl.pallas_call(
        kern, name=name,
        out_shape=[jax.ShapeDtypeStruct((nh, s), F32), jax.ShapeDtypeStruct((nh, 1), F32)],
        compiler_params=_params(),
    )(dcrow, frow, bf_col)


def _ride_split(ex, refs, n_in, n_out):
    n = ex.n if ex is not None else 0
    own_in, srcs = refs[:n_in], refs[n_in:n_in + n]
    own_out, dsts = refs[n_in + n:n_in + n + n_out], refs[n_in + n + n_out:n_in + 2 * n + n_out]
    sems = refs[n_in + 2 * n + n_out:n_in + 2 * n + n_out + 3] if n else ()
    rest = refs[n_in + 2 * n + n_out + (3 if n else 0):]
    return own_in, own_out, (srcs, dsts, sems), rest


ATTN_STRIP = 32
BIAS_LANES = 3


def _head_masks(rows):
    lane = lax.broadcasted_iota(jnp.int32, (rows, LANES), 1)
    return [(lane >= HEAD_DIM * e) & (lane < HEAD_DIM * (e + 1)) for e in range(2)]


def _augment(feat, bias, e, *, bias_slot, ones_slot):
    rows = feat.shape[0]
    lane = lax.broadcasted_iota(jnp.int32, (rows, LANES), 1)
    own = (lane >= HEAD_DIM * e) & (lane < HEAD_DIM * (e + 1))
    off = lane - HEAD_DIM * (1 - e)
    out = jnp.where(own, feat, 0.0)
    if ones_slot is not None:
        out = jnp.where((off >= ones_slot * BIAS_LANES) & (off < (ones_slot + 1) * BIAS_LANES), 1.0, out)
    if bias is not None:
        rest = pltpu.roll(bias, HEAD_DIM, 1)
        for term in range(BIAS_LANES):
            part = rest.astype(BF).astype(F32)
            out = jnp.where(off == bias_slot * BIAS_LANES + term, part, out)
            rest = rest - part
    return out.astype(BF)


def _two_slot_pipeline(m, scores, tile):
    scores(0, 0)

    def pair(n, carry):
        k = 2 * n
        scores(k + 1, 1)
        tile(k, 0, False)
        scores(k + 2, 0)
        tile(k + 1, 1, False)
        return carry

    lax.fori_loop(0, m // 2, pair, 0)

    @pl.when(m % 2 == 0)
    def _():
        tile(m, 0, True)

    @pl.when(m % 2 == 1)
    def _():
        scores(m, 1)
        tile(m - 1, 0, False)
        tile(m, 1, True)


def _attn_fwd(proj, cumx, qcol, *, name, ride=None):
    s = proj.shape[0]
    w = cumx.shape[1]
    nhp = w // LANES
    t = _fit(s, 256, LANES)
    nq = s // t
    strip = _fit(t, ATTN_STRIP, 16)
    scale = HEAD_DIM ** -0.5
    qb, kb, vb = qcol // LANES, (qcol + w) // LANES, (qcol + 2 * w) // LANES
    ex, ex_arrays = ride if ride is not None else (None, [])
    nt_dims = (((1,), (1,)), ((), ()))

    def kern(*refs):
        own_in, (o_ref, l_ref), comm, scratch = _ride_split(ex, refs, 5, 2)
        q_ref, k_ref, v_ref, cxq_ref, cxk_ref = own_in
        ka_ref, vat_ref, s0_ref, s1_ref, p_ref, m_ref, acc_ref = scratch
        s_refs = (s0_ref, s1_ref)
        i = pl.program_id(1)
        if ex is not None:
            @pl.when((pl.program_id(0) == 0) & (i == 0))
            def _():
                ex.start(*comm)

            @pl.when((pl.program_id(0) == nhp - 1) & (i == 0))
            def _():
                ex.forward(*comm)

        msks = _head_masks(t)

        @pl.when(i == 0)
        def _():
            def build(c, carry):
                rows = pl.ds(pl.multiple_of(c * t, LANES), t)
                k2, v2, cx = k_ref[rows, :], v_ref[rows, :], cxk_ref[rows, :]
                for e in range(2):
                    ka_ref[e, rows, :] = _augment(k2, -cx, e, bias_slot=1, ones_slot=0)
                    vat_ref[e, :, rows] = jnp.where(msks[e], v2, 1.0).T.astype(BF)
                return carry
            lax.fori_loop(0, nq, build, 0)

        q2 = q_ref[...] * scale
        qa = [_augment(q2, cxq_ref[...], e, bias_slot=0, ones_slot=1) for e in range(2)]
        m_ref[...] = jnp.full(m_ref.shape, NEG, F32)
        acc_ref[...] = jnp.zeros(acc_ref.shape, F32)
        slabs = strip // SUBLANES

        def scores(j, slot):
            rows_k = pl.ds(pl.multiple_of(j * t, LANES), t)
            for e in range(2):
                st = lax.dot_general(ka_ref[e, rows_k, :], qa[e], nt_dims, preferred_element_type=F32)
                s_refs[slot][e] = st.reshape(t // SUBLANES, SUBLANES, t)

        def tile(j, slot, diagonal):
            rows_k = pl.ds(pl.multiple_of(j * t, LANES), t)
            s_ref = s_refs[slot]
            for e in range(2):
                mx = jnp.full((SUBLANES, t), NEG, F32)
                for r in range(t // strip):
                    sl = slice(r * slabs, (r + 1) * slabs)
                    sv = s_ref[e,sl]
                    if diagonal:
                        shape = (slabs, SUBLANES, t)
                        key = (r * strip + lax.broadcasted_iota(jnp.int32, shape, 0) * SUBLANES
                               + lax.broadcasted_iota(jnp.int32, shape, 1))
                        sv = jnp.where(key <= lax.broadcasted_iota(jnp.int32, shape, 2), sv, NEG)
                        s_ref[e,sl] = sv
                    mx = jnp.maximum(mx, jnp.max(sv, axis=0))
                for sh in (4, 2, 1):
                    mx = jnp.maximum(mx, pltpu.roll(mx, sh, 0))
                m_old = m_ref[e]
                m_new = jnp.maximum(m_old, mx)
                alpha = jnp.exp(m_old - m_new)
                m_ref[e] = m_new
                for r in range(t // strip):
                    p = jnp.exp(s_ref[e,r * slabs:(r + 1) * slabs] - m_new[None])
                    p_ref[e, r * strip:(r + 1) * strip, :] = p.reshape(strip, t).astype(BF)
                acc = acc_ref[e].reshape(LANES // SUBLANES, SUBLANES, t) * alpha[None]
                acc_ref[e] = acc.reshape(LANES, t) + jnp.dot(vat_ref[e, :, rows_k], p_ref[e],
                                                             preferred_element_type=F32)

        _two_slot_pipeline(i, scores, tile)

        outs, lses = [], []
        for e in range(2):
            acc = acc_ref[e]
            other = HEAD_DIM * (1 - e)
            den = acc[other:other + 1, :]
            outs.append(acc / den)
            lses.append(jnp.broadcast_to(m_ref[e][0:1, :] + jnp.log(den), (LANES, t)))
        upper = lax.broadcasted_iota(jnp.int32, (LANES, t), 0) < HEAD_DIM
        o_ref[...] = jnp.where(upper, outs[0], outs[1]).T.astype(BF)
        l_ref[...] = jnp.where(upper, lses[0], lses[1]).T
        if ex is not None:
            @pl.when((pl.program_id(0) == nhp - 1) & (i == nq - 1))
            def _():
                ex.wait(*comm)

    own_scratch = [pltpu.VMEM((2, s, LANES), BF), pltpu.VMEM((2, LANES, s), BF),
                   pltpu.VMEM((2, t // SUBLANES, SUBLANES, t), F32),
                   pltpu.VMEM((2, t // SUBLANES, SUBLANES, t), F32), pltpu.VMEM((2, t, t), BF),
                   pltpu.VMEM((2, SUBLANES, t), F32), pltpu.VMEM((2, LANES, t), F32)]
    return pl.pallas_call(
        kern, name=name,
        out_shape=[jax.ShapeDtypeStruct((s, w), BF), jax.ShapeDtypeStruct((nhp, s, LANES), F32)]
        + (ex.out_shape if ex else []),
        grid=(nhp, nq),
        in_specs=[pl.BlockSpec((t, LANES), lambda h, i: (i, qb + h)),
                  pl.BlockSpec((s, LANES), lambda h, i: (0, kb + h)),
                  pl.BlockSpec((s, LANES), lambda h, i: (0, vb + h)),
                  pl.BlockSpec((t, LANES), lambda h, i: (i, h)),
                  pl.BlockSpec((s, LANES), lambda h, i: (0, h))] + (ex.specs if ex else []),
        out_specs=[pl.BlockSpec((t, LANES), lambda h, i: (i, h)),
                   pl.BlockSpec((None, t, LANES), lambda h, i: (h, i, 0))] + (ex.specs if ex else []),
        scratch_shapes=(ex.scratch if ex else []) + own_scratch,
        compiler_params=_params(dimension_semantics=("arbitrary", "arbitrary"),
                                has_side_effects=ex is not None),
    )(proj, proj, proj, cumx, cumx, *ex_arrays)


def _attn_bwd(proj, do, o, lse, cumx, qcol, *, name, ride=None):
    s = proj.shape[0]
    w = cumx.shape[1]
    nhp = w // LANES
    t = _fit(s, 256, LANES)
    nq = s // t
    strip = _fit(t, ATTN_STRIP, 16)
    scale = HEAD_DIM ** -0.5
    qb, kb, vb = qcol // LANES, (qcol + w) // LANES, (qcol + 2 * w) // LANES
    tn_dims = (((0,), (0,)), ((), ()))
    nt_dims = (((1,), (1,)), ((), ()))
    ex, ex_arrays = ride if ride is not None else (None, [])

    def kern(*refs):
        own_in, own_out, comm, scratch = _ride_split(ex, refs, 7, 5)
        q_ref, k_ref, v_ref, do_ref, o_ref, l_ref, cx_ref = own_in
        dq_ref, dk_ref, dv_ref, dkc_ref, dqc_ref = own_out
        qa_ref, da_ref, dqa_ref, dka_ref, dva_ref, st0_ref, st1_ref, dpt0_ref, dpt1_ref, pt_ref, dst_ref = scratch
        st_refs, dpt_refs = (st0_ref, st1_ref), (dpt0_ref, dpt1_ref)
        j = pl.program_id(1)
        if ex is not None:
            @pl.when((pl.program_id(0) == 0) & (j == 0))
            def _():
                ex.start(*comm)

        msks = _head_masks(t)

        @pl.when(j == 0)
        def _():
            def build(c, carry):
                rows = pl.ds(pl.multiple_of(c * t, LANES), t)
                q2 = q_ref[rows, :] * scale
                do2 = do_ref[rows, :]
                dd = do2 * o_ref[rows, :].astype(F32)
                delta = jnp.where(msks[0], jnp.sum(jnp.where(msks[0], dd, 0.0), axis=1, keepdims=True),
                                  jnp.sum(jnp.where(msks[1], dd, 0.0), axis=1, keepdims=True))
                bias = cx_ref[rows, :] - l_ref[rows, :]
                for e in range(2):
                    qa_ref[e, rows, :] = _augment(q2, bias, e, bias_slot=0, ones_slot=1)
                    da_ref[e, rows, :] = _augment(do2, -delta, e, bias_slot=0, ones_slot=None)
                return carry
            lax.fori_loop(0, nq, build, 0)
            dqa_ref[...] = jnp.zeros(dqa_ref.shape, F32)

        rows_k = pl.ds(pl.multiple_of(j * t, LANES), t)
        k2, v2 = k_ref[...], v_ref[...]
        ka = [_augment(k2, -cx_ref[rows_k, :], e, bias_slot=1, ones_slot=0) for e in range(2)]
        va = [_augment(v2, None, e, bias_slot=None, ones_slot=0) for e in range(2)]
        dka_ref[...] = jnp.zeros(dka_ref.shape, F32)
        dva_ref[...] = jnp.zeros(dva_ref.shape, F32)

        def scores(k, slot):
            rows_q = pl.ds(pl.multiple_of((nq - 1 - k) * t, LANES), t)
            for e in range(2):
                st_refs[slot][e] = lax.dot_general(ka[e], qa_ref[e, rows_q, :], nt_dims,
                                                   preferred_element_type=F32)
                dpt_refs[slot][e] = lax.dot_general(va[e], da_ref[e, rows_q, :], nt_dims,
                                                    preferred_element_type=F32)

        def tile(k, slot, diagonal):
            rows_q = pl.ds(pl.multiple_of((nq - 1 - k) * t, LANES), t)
            st_ref, dpt_ref = st_refs[slot], dpt_refs[slot]
            for e in range(2):
                for r in range(t // strip):
                    rows = slice(r * strip, (r + 1) * strip)
                    sv = st_ref[e, rows, :]
                    if diagonal:
                        key = r * strip + lax.broadcasted_iota(jnp.int32, (strip, t), 0)
                        qry = lax.broadcasted_iota(jnp.int32, (strip, t), 1)
                        sv = jnp.where(key <= qry, sv, NEG)
                    p = jnp.exp(sv)
                    pt_ref[e, rows, :] = p.astype(BF)
                    dst_ref[e, rows, :] = (p * dpt_ref[e, rows, :]).astype(BF)
            for e in range(2):
                dva_ref[e] += jnp.dot(pt_ref[e], da_ref[e, rows_q, :], preferred_element_type=F32)
                dka_ref[e] += jnp.dot(dst_ref[e], qa_ref[e, rows_q, :], preferred_element_type=F32)
                dqa_ref[e, rows_q, :] += lax.dot_general(dst_ref[e], ka[e], tn_dims, preferred_element_type=F32)

        _two_slot_pipeline(nq - 1 - j, scores, tile)

        dk_ref[...] = jnp.where(msks[0], dka_ref[0], dka_ref[1])
        dv_ref[...] = jnp.where(msks[0], dva_ref[0], dva_ref[1])
        sums = jnp.where(msks[1], dka_ref[0], dka_ref[1]).T
        dkc_ref[0:1, :] = sums[HEAD_DIM + BIAS_LANES:HEAD_DIM + BIAS_LANES + 1, :]
        dkc_ref[1:2, :] = sums[BIAS_LANES:BIAS_LANES + 1, :]

        @pl.when(j == nq - 1)
        def _():
            def flush(c, carry):
                rows = pl.ds(pl.multiple_of(c * t, LANES), t)
                a0, a1 = dqa_ref[0, rows, :], dqa_ref[1, rows, :]
                dq_ref[rows, :] = jnp.where(msks[0], a0, a1) * scale
                sums = jnp.where(msks[1], a0, a1).T
                dqc_ref[0:1, rows] = sums[HEAD_DIM:HEAD_DIM + 1, :]
                dqc_ref[1:2, rows] = sums[0:1, :]
                return carry
            lax.fori_loop(0, nq, flush, 0)

        if ex is not None:
            @pl.when((pl.program_id(0) == nhp - 1) & (j == nq - 1))
            def _():
                ex.wait(*comm)

    full = lambda cb: pl.BlockSpec((s, LANES), lambda h, j: (0, cb + h))
    blk = lambda cb: pl.BlockSpec((t, LANES), lambda h, j: (j, cb + h))
    own_scratch = [pltpu.VMEM((2, s, LANES), BF), pltpu.VMEM((2, s, LANES), BF), pltpu.VMEM((2, s, LANES), F32),
                   pltpu.VMEM((2, t, LANES), F32), pltpu.VMEM((2, t, LANES), F32),
                   pltpu.VMEM((2, t, t), F32), pltpu.VMEM((2, t, t), F32),
                   pltpu.VMEM((2, t, t), F32), pltpu.VMEM((2, t, t), F32),
                   pltpu.VMEM((2, t, t), BF), pltpu.VMEM((2, t, t), BF)]
    return pl.pallas_call(
        kern, name=name,
        out_shape=[jax.ShapeDtypeStruct((s, w), F32)] * 3 + [jax.ShapeDtypeStruct((nhp, 2, s), F32)] * 2
        + (ex.out_shape if ex else []),
        grid=(nhp, nq),
        in_specs=[full(qb), blk(kb), blk(vb), full(0), full(0),
                  pl.BlockSpec((None, s, LANES), lambda h, j: (h, 0, 0)), full(0)] + (ex.specs if ex else []),
        out_specs=[full(0), blk(0), blk(0), pl.BlockSpec((None, 2, t), lambda h, j: (h, 0, j)),
                   pl.BlockSpec((None, 2, s), lambda h, j: (h, 0, 0))] + (ex.specs if ex else []),
        scratch_shapes=(ex.scratch if ex else []) + own_scratch,
        compiler_params=_params(dimension_semantics=("arbitrary", "arbitrary"),
                                has_side_effects=ex is not None),
    )(proj, proj, proj, do, o, lse, cumx, *ex_arrays)


S5_STATES = 256
S5_ROWS = 512


def _cmul(ar, ai, br, bi):
    return ar * br - ai * bi, ar * bi + ai * br


def _scan_tables(lr, li, reverse):
    w = lr.shape[1]
    row = lax.broadcasted_iota(jnp.int32, (SUBLANES, w), 0)
    if reverse:
        row = SUBLANES - 1 - row
    lr1, li1 = jnp.broadcast_to(lr, (SUBLANES, w)), jnp.broadcast_to(li, (SUBLANES, w))
    lr2, li2 = _cmul(lr1, li1, lr1, li1)
    lr4, li4 = _cmul(lr2, li2, lr2, li2)
    steps = []
    for d, (pr, pi) in zip((1, 2, 4), ((lr1, li1), (lr2, li2), (lr4, li4))):
        keep = row >= d
        steps.append((jnp.where(keep, pr, 0.0), jnp.where(keep, pi, 0.0)))
    cr, ci = lr1, li1
    for bit, (pr, pi) in zip((1, 2, 4), ((lr1, li1), (lr2, li2), (lr4, li4))):
        nr, ni = _cmul(cr, ci, pr, pi)
        has = (row & bit) != 0
        cr, ci = jnp.where(has, nr, cr), jnp.where(has, ni, ci)
    return steps, (cr, ci)


def _scan_local(xr, xi, steps, reverse):
    for d, (pr, pi) in zip((1, 2, 4), steps):
        sh = (SUBLANES - d) if reverse else d
        sr, si = pltpu.roll(xr, sh, 0), pltpu.roll(xi, sh, 0)
        xr, xi = xr + (pr * sr - pi * si), xi + (pr * si + pi * sr)
    return xr, xi


def _scan_carry(xr, xi, car_r, car_i, carry_pow):
    cr, ci = carry_pow
    return xr + (cr * car_r - ci * car_i), xi + (cr * car_i + ci * car_r)


SCAN_UNROLL = 4


def _s5_specs(s, ncb):
    u_spec = pl.BlockSpec((s, LANES), lambda cb, hf: (0, cb))
    wb_spec = pl.BlockSpec((None, None, LANES, S5_STATES), lambda cb, hf: (cb, hf, 0, 0))
    wc_spec = pl.BlockSpec((None, None, S5_STATES, LANES), lambda cb, hf: (cb, hf, 0, 0))
    lam_spec = pl.BlockSpec((1, S5_STATES), lambda cb, hf: (0, 2 * cb + hf))
    d_spec = pl.BlockSpec((1, LANES), lambda cb, hf: (0, cb))
    return u_spec, wb_spec, wc_spec, lam_spec, d_spec


def _s5_project_and_scan(u_ref, wbr_ref, wbi_ref, lr_ref, li_ref, xr_ref, xi_ref, s, rows):
    wbr, wbi = wbr_ref[...], wbi_ref[...]
    for r in range(s // rows):
        sl = pl.ds(r * rows, rows)
        ub = u_ref[sl, :].astype(BF)
        xr_ref[sl, :] = jnp.dot(ub, wbr, preferred_element_type=F32)
        xi_ref[sl, :] = jnp.dot(ub, wbi, preferred_element_type=F32)
    steps, cpow = _scan_tables(lr_ref[...], li_ref[...], False)

    unroll = _fit(s // SUBLANES, SCAN_UNROLL, 1)

    def body(b, carry):
        car_r, car_i = carry
        sls = [pl.ds(pl.multiple_of((b * unroll + q) * SUBLANES, SUBLANES), SUBLANES) for q in range(unroll)]
        blocks = [_scan_local(xr_ref[sl, :], xi_ref[sl, :], steps, False) for sl in sls]
        for sl, (xr, xi) in zip(sls, blocks):
            xr, xi = _scan_carry(xr, xi, car_r, car_i, cpow)
            xr_ref[sl, :] = xr
            xi_ref[sl, :] = xi
            car_r, car_i = xr[SUBLANES - 1:SUBLANES, :], xi[SUBLANES - 1:SUBLANES, :]
        return car_r, car_i

    zero = jnp.zeros((1, S5_STATES), F32)
    lax.fori_loop(0, s // SUBLANES // unroll, body, (zero, zero))


def _s5_fwd(proj, wb_re, wb_im, wc_re, wc_im, lam_re, lam_im, dskip, *, name):
    s = proj.shape[0]
    w = dskip.shape[1]
    ncb = w // LANES
    rows = _fit(s, S5_ROWS, SUBLANES)

    def kern(u_ref, wbr_ref, wbi_ref, wcr_ref, wci_ref, lr_ref, li_ref, d_ref, y_ref, xr_ref, xi_ref):
        hf = pl.program_id(1)
        _s5_project_and_scan(u_ref, wbr_ref, wbi_ref, lr_ref, li_ref, xr_ref, xi_ref, s, rows)
        wcr, wci = wcr_ref[...], wci_ref[...]
        for r in range(s // rows):
            sl = pl.ds(r * rows, rows)
            y = (jnp.dot(xr_ref[sl, :].astype(BF), wcr, preferred_element_type=F32)
                 - jnp.dot(xi_ref[sl, :].astype(BF), wci, preferred_element_type=F32))

            @pl.when(hf == 0)
            def _(y=y, sl=sl):
                y_ref[sl, :] = y + d_ref[...] * u_ref[sl, :]

            @pl.when(hf == 1)
            def _(y=y, sl=sl):
                y_ref[sl, :] += y

    u_spec, wb_spec, wc_spec, lam_spec, d_spec = _s5_specs(s, ncb)
    return pl.pallas_call(
        kern, name=name, out_shape=jax.ShapeDtypeStruct((s, w), F32), grid=(ncb, 2),
        in_specs=[u_spec, wb_spec, wb_spec, wc_spec, wc_spec, lam_spec, lam_spec, d_spec],
        out_specs=u_spec,
        scratch_shapes=[pltpu.VMEM((s, S5_STATES), F32), pltpu.VMEM((s, S5_STATES), F32)],
        compiler_params=_params(dimension_semantics=("parallel", "arbitrary")),
    )(proj, wb_re, wb_im, wc_re, wc_im, lam_re, lam_im, dskip)


def _s5_bwd(proj, dy, wb_re, wb_im, wc_re, wc_im, lam_re, lam_im, dskip, *, name, ride=None):
    s = proj.shape[0]
    w = dskip.shape[1]
    ncb = w // LANES
    rows = _fit(s, S5_ROWS, SUBLANES)
    tn_dims = (((0,), (0,)), ((), ()))
    nt_dims = (((1,), (1,)), ((), ()))
    ex, ex_arrays = ride if ride is not None else (None, [])

    def kern(*refs):
        own_in, own_out, comm, scratch = _ride_split(ex, refs, 9, 8)
        u_ref, dy_ref, wbr_ref, wbi_ref, wcr_ref, wci_ref, lr_ref, li_ref, d_ref = own_in
        du_ref, dwbr_ref, dwbi_ref, dwcr_ref, dwci_ref, dlr_ref, dli_ref, dd_ref = own_out
        xr_ref, xi_ref, gr_ref, gi_ref = scratch
        hf = pl.program_id(1)
        if ex is not None:
            @pl.when((pl.program_id(0) == 0) & (hf == 0))
            def _():
                ex.start(*comm)

        _s5_project_and_scan(u_ref, wbr_ref, wbi_ref, lr_ref, li_ref, xr_ref, xi_ref, s, rows)

        wcr, wci = wcr_ref[...], wci_ref[...]
        dwcr = jnp.zeros((S5_STATES, LANES), F32)
        dwci = jnp.zeros((S5_STATES, LANES), F32)
        ddsk = jnp.zeros((1, LANES), F32)
        for r in range(s // rows):
            sl = pl.ds(r * rows, rows)
            dyf = dy_ref[sl, :]
            dyb = dyf.astype(BF)
            gr_ref[sl, :] = lax.dot_general(dyb, wcr, nt_dims, preferred_element_type=F32)
            gi_ref[sl, :] = -lax.dot_general(dyb, wci, nt_dims, preferred_element_type=F32)
            dwcr = dwcr + lax.dot_general(xr_ref[sl, :].astype(BF), dyb, tn_dims, preferred_element_type=F32)
            dwci = dwci - lax.dot_general(xi_ref[sl, :].astype(BF), dyb, tn_dims, preferred_element_type=F32)
            ddsk = ddsk + jnp.sum(dyf * u_ref[sl, :], axis=0, keepdims=True)
        dwcr_ref[...] = dwcr
        dwci_ref[...] = dwci

        @pl.when(hf == 0)
        def _():
            dd_ref[...] = ddsk

        steps, cpow = _scan_tables(lr_ref[...], -li_ref[...], True)
        row = lax.broadcasted_iota(jnp.int32, (SUBLANES, S5_STATES), 0)
        nblk = s // SUBLANES

        unroll = _fit(nblk, SCAN_UNROLL, 1)

        def body(k, carry):
            car_r, car_i, ar, ai = carry
            sls = [pl.ds(pl.multiple_of((nblk - 1 - k * unroll - q) * SUBLANES, SUBLANES), SUBLANES)
                   for q in range(unroll)]
            blocks = [_scan_local(gr_ref[sl, :], gi_ref[sl, :], steps, True) for sl in sls]
            for sl, (g_r, g_i) in zip(sls, blocks):
                g_r, g_i = _scan_carry(g_r, g_i, car_r, car_i, cpow)
                gr_ref[sl, :] = g_r
                gi_ref[sl, :] = g_i
                nr = jnp.where(row == SUBLANES - 1, car_r, pltpu.roll(g_r, SUBLANES - 1, 0))
                ni = jnp.where(row == SUBLANES - 1, car_i, pltpu.roll(g_i, SUBLANES - 1, 0))
                xr, xi = xr_ref[sl, :], xi_ref[sl, :]
                ar = ar + (xr * nr + xi * ni)
                ai = ai + (xr * ni - xi * nr)
                car_r, car_i = g_r[0:1, :], g_i[0:1, :]
            return car_r, car_i, ar, ai

        zero = jnp.zeros((1, S5_STATES), F32)
        zacc = jnp.zeros((SUBLANES, S5_STATES), F32)
        _, _, ar, ai = lax.fori_loop(0, nblk // unroll, body, (zero, zero, zacc, zacc))
        dlr_ref[...] = jnp.sum(ar, axis=0, keepdims=True)
        dli_ref[...] = jnp.sum(ai, axis=0, keepdims=True)

        wbr, wbi = wbr_ref[...], wbi_ref[...]
        dwbr = jnp.zeros((LANES, S5_STATES), F32)
        dwbi = jnp.zeros((LANES, S5_STATES), F32)
        for r in range(s // rows):
            sl = pl.ds(r * rows, rows)
            grb, gib = gr_ref[sl, :].astype(BF), gi_ref[sl, :].astype(BF)
            ub = u_ref[sl, :].astype(BF)
            dwbr = dwbr + lax.dot_general(ub, grb, tn_dims, preferred_element_type=F32)
            dwbi = dwbi + lax.dot_general(ub, gib, tn_dims, preferred_element_type=F32)
            du = (lax.dot_general(grb, wbr, nt_dims, preferred_element_type=F32)
                  + lax.dot_general(gib, wbi, nt_dims, preferred_element_type=F32))

            @pl.when(hf == 0)
            def _(du=du, sl=sl):
                du_ref[sl, :] = du + d_ref[...] * dy_ref[sl, :]

            @pl.when(hf == 1)
            def _(du=du, sl=sl):
                du_ref[sl, :] += du
        dwbr_ref[...] = dwbr
        dwbi_ref[...] = dwbi
        if ex is not None:
            @pl.when((pl.program_id(0) == ncb - 1) & (hf == 1))
            def _():
                ex.wait(*comm)

    u_spec, wb_spec, wc_spec, lam_spec, d_spec = _s5_specs(s, ncb)
    dwb_spec = pl.BlockSpec((None, None, LANES, S5_STATES), lambda cb, hf: (cb, hf, 0, 0))
    dwc_spec = pl.BlockSpec((None, None, S5_STATES, LANES), lambda cb, hf: (cb, hf, 0, 0))
    state = pltpu.VMEM((s, S5_STATES), F32)
    return pl.pallas_call(
        kern, name=name,
        out_shape=[jax.ShapeDtypeStruct((s, w), F32),
                   jax.ShapeDtypeStruct((ncb, 2, LANES, S5_STATES), F32), jax.ShapeDtypeStruct((ncb, 2, LANES, S5_STATES), F32),
                   jax.ShapeDtypeStruct((ncb, 2, S5_STATES, LANES), F32), jax.ShapeDtypeStruct((ncb, 2, S5_STATES, LANES), F32),
                   jax.ShapeDtypeStruct((1, 4 * w), F32), jax.ShapeDtypeStruct((1, 4 * w), F32),
                   jax.ShapeDtypeStruct((1, w), F32)] + (ex.out_shape if ex else []),
        grid=(ncb, 2),
        in_specs=[u_spec, u_spec, wb_spec, wb_spec, wc_spec, wc_spec, lam_spec, lam_spec, d_spec]
        + (ex.specs if ex else []),
        out_specs=[u_spec, dwb_spec, dwb_spec, dwc_spec, dwc_spec, lam_spec, lam_spec, d_spec]
        + (ex.specs if ex else []),
        scratch_shapes=(ex.scratch if ex else []) + [state, state, state, state],
        compiler_params=_params(dimension_semantics=("arbitrary", "arbitrary"), has_side_effects=ex is not None),
    )(proj, dy, wb_re, wb_im, wc_re, wc_im, lam_re, lam_im, dskip, *ex_arrays)


def _s5_discretise(lam_re, lam_im, log_dt, b_re, b_im):
    lr = jnp.minimum(lam_re, -EIG_CLIP)
    li = lam_im
    dt = jnp.exp(log_dt)[:, None]
    mag = jnp.exp(lr * dt)
    lbr, lbi = mag * jnp.cos(li * dt), mag * jnp.sin(li * dt)
    den = lr * lr + li * li
    fr = ((lbr - 1.0) * lr + lbi * li) / den
    fi = (lbi * lr - (lbr - 1.0) * li) / den
    bbr = fr[..., None] * b_re - fi[..., None] * b_im
    bbi = fr[..., None] * b_im + fi[..., None] * b_re
    return lbr, lbi, bbr, bbi


def _s5_operand(mats, channels_first):
    g, a, b = mats.shape
    gl = LANES // 2 // SSM_H
    ncb = g // (2 * gl)
    m = mats.reshape(ncb, 2, gl, a, b)
    eye = jnp.eye(gl, dtype=mats.dtype)
    inner = (m[:, :, :, :, None, :] * eye[None, None, :, None, :, None]).reshape(ncb, 2, gl * a, gl * b)
    zeros = jnp.zeros_like(inner[:, 0])
    axis = 1 if channels_first else 2
    return jnp.stack([jnp.concatenate([inner[:, 0], zeros], axis=axis),
                      jnp.concatenate([zeros, inner[:, 1]], axis=axis)], axis=1)


def _s5_block_grads(dwb, a, b, transpose):
    ncb = dwb.shape[0]
    gl = LANES // 2 // (a if not transpose else b)
    if not transpose:
        d = dwb.reshape(ncb, 2, 2, gl, a, gl, b)
        parts = [[d[:, hf, hf, g, :, g, :] for g in range(gl)] for hf in range(2)]
    else:
        d = dwb.reshape(ncb, 2, gl, a, 2, gl, b)
        parts = [[d[:, hf, g, :, hf, g, :] for g in range(gl)] for hf in range(2)]
    st = jnp.stack([jnp.stack(p, axis=1) for p in parts], axis=1)
    return st.reshape(ncb * 2 * gl, a, b)


def _adamw(parts, w, m, v, *, name):
    depth, r, c = w.shape
    assert len(parts) == depth
    npart = parts[0].shape[0]
    row_bytes = 4 * (-(-c // LANES) * LANES)
    align = 16 if parts[0].dtype == BF else SUBLANES
    budget = VMEM_LIMIT // 2 // (2 * (depth * npart + 7) * row_bytes)
    tr = _fit(r, max(align, budget // align * align), align)
    nr = r // tr
    c1 = 1.0 / (1.0 - ADAM_B1 ** ADAM_STEP)
    c2 = 1.0 / (1.0 - ADAM_B2 ** ADAM_STEP)

    def kern(*refs):
        p_refs = refs[:depth]
        w_ref, m_ref, v_ref, g_ref, d_ref, nm_ref, nv_ref = refs[depth:]
        layer = pl.program_id(0)
        for l in range(depth):
            @pl.when(layer == l)
            def _(p_ref=p_refs[l]):
                g = p_ref[0].astype(F32)
                for q in range(1, npart):
                    g = g + p_ref[q].astype(F32)
                m2 = ADAM_B1 * m_ref[...] + (1.0 - ADAM_B1) * g
                v2 = ADAM_B2 * v_ref[...] + (1.0 - ADAM_B2) * (g * g)
                upd = (m2 * c1) / (jnp.sqrt(v2 * c2) + ADAM_EPS) + ADAM_WD * w_ref[...]
                g_ref[...] = g
                d_ref[...] = -ADAM_LR * upd
                nm_ref[...] = m2
                nv_ref[...] = v2

    def part_spec(l):
        return pl.BlockSpec((npart, tr, c),
                            lambda ly, i: (0, jnp.where(ly == l, i, jnp.where(ly < l, 0, nr - 1)), 0))

    spec = pl.BlockSpec((None, tr, c), lambda ly, i: (ly, i, 0))
    return pl.pallas_call(
        kern, name=name, out_shape=[jax.ShapeDtypeStruct((depth, r, c), F32)] * 4, grid=(depth, nr),
        in_specs=[part_spec(l) for l in range(depth)] + [spec, spec, spec],
        out_specs=[spec] * 4,
        compiler_params=_params(dimension_semantics=("arbitrary", "arbitrary")),
    )(*parts, w, m, v)


def _sum_parts(parts, *, name):
    npart, r, c = parts.shape

    def kern(p_ref, o_ref):
        g = p_ref[0]
        for q in range(1, npart):
            g = g + p_ref[q]
        o_ref[...] = g

    return pl.pallas_call(kern, name=name, out_shape=jax.ShapeDtypeStruct((r, c), F32), compiler_params=_params())(parts)


class _Exchange:
    def __init__(self, arrays, gather):
        self.n = len(arrays)
        self.gather = gather
        self.out_shape = [jax.ShapeDtypeStruct(((NDEV,) + a.shape) if gather else a.shape, a.dtype) for a in arrays]
        self.scratch = [pltpu.SemaphoreType.DMA((self.n, NDEV - 1)), pltpu.SemaphoreType.DMA((self.n, NDEV - 1)),
                        pltpu.SemaphoreType.DMA((self.n,))]
        self.specs = [pl.BlockSpec(memory_space=pl.ANY)] * self.n

    def _copies(self, srcs, dsts, sems):
        send_sems, recv_sems, local_sems = sems
        x, y, c = lax.axis_index("x"), lax.axis_index("y"), lax.axis_index("c")
        me = 4 * x + 2 * y + c
        local = [pltpu.make_async_copy(srcs[a] if self.gather else srcs[a].at[me], dsts[a].at[me], local_sems.at[a])
                 for a in range(self.n)]
        remote = []
        for k in (1, 2, 4, 3, 5, 6, 7):
            px, py, pc = x ^ ((k >> 2) & 1), y ^ ((k >> 1) & 1), c ^ (k & 1)
            peer = 4 * px + 2 * py + pc
            for a in range(self.n):
                src = srcs[a] if self.gather else srcs[a].at[peer]
                mk = functools.partial(
                    pltpu.make_async_remote_copy, src_ref=src,
                    send_sem=send_sems.at[a, k - 1], recv_sem=recv_sems.at[a, k - 1],
                    device_id=(px, py, pc), device_id_type=pl.DeviceIdType.MESH)
                remote.append((mk(dst_ref=dsts[a].at[me]), mk(dst_ref=dsts[a].at[peer])))
        return local, remote

    def _gather_copies(self, srcs, dsts, sems):
        send_sems, recv_sems, local_sems = sems
        x, y, c = lax.axis_index("x"), lax.axis_index("y"), lax.axis_index("c")
        block = lambda px, py, pc: 4 * px + 2 * py + pc
        me = block(x, y, c)
        chips = [(1 - x, y), (x, 1 - y), (1 - x, 1 - y)]
        local = [pltpu.make_async_copy(srcs[a], dsts[a].at[me], local_sems.at[a]) for a in range(self.n)]
        own, passed = [], []
        for a in range(self.n):
            def copy(k, blk, to, src=None, a=a):
                return pltpu.make_async_remote_copy(
                    src_ref=dsts[a].at[blk] if src is None else src, dst_ref=dsts[a].at[blk],
                    send_sem=send_sems.at[a, k], recv_sem=recv_sems.at[a, k],
                    device_id=to, device_id_type=pl.DeviceIdType.MESH)
            sib = (x, y, 1 - c)
            own.append((copy(0, me, sib, srcs[a]), copy(0, block(x, y, 1 - c), sib)))
            for j, (px, py) in enumerate(chips):
                own.append((copy(1 + j, me, (px, py, c), srcs[a]), copy(1 + j, block(px, py, c), (px, py, c))))
            for j, (px, py) in enumerate(chips):
                passed.append((copy(4 + j, block(px, py, c), sib), copy(4 + j, block(px, py, 1 - c), sib)))
        return local, own, passed

    def start(self, srcs, dsts, sems):
        if self.gather:
            local, own, _ = self._gather_copies(srcs, dsts, sems)
            for cp in local:
                cp.start()
            for send, _ in own:
                send.start()
            return
        local, remote = self._copies(srcs, dsts, sems)
        for cp in local:
            cp.start()
        for send, _ in remote:
            send.start()

    def forward(self, srcs, dsts, sems):
        if not self.gather:
            return
        _, own, passed = self._gather_copies(srcs, dsts, sems)
        for a in range(self.n):
            for j in range(3):
                own[4 * a + 1 + j][1].wait_recv()
                passed[3 * a + j][0].start()

    def wait(self, srcs, dsts, sems):
        if self.gather:
            local, own, passed = self._gather_copies(srcs, dsts, sems)
            for a in range(self.n):
                own[4 * a][1].wait_recv()
            for _, arrival in passed:
                arrival.wait_recv()
            for send, _ in own + passed:
                send.wait_send()
            for cp in local:
                cp.wait()
            return
        local, remote = self._copies(srcs, dsts, sems)
        for send, arrival in remote:
            send.wait_send()
            arrival.wait_recv()
        for cp in local:
            cp.wait()


def _exchange(arrays, gather, *, name):
    ex = _Exchange(arrays, gather)
    n = ex.n

    def kern(*refs):
        srcs, dsts, sems = refs[:n], refs[n:2 * n], refs[2 * n:]
        ex.start(srcs, dsts, sems)
        ex.forward(srcs, dsts, sems)
        ex.wait(srcs, dsts, sems)

    return pl.pallas_call(
        kern, name=name, out_shape=ex.out_shape, in_specs=ex.specs, out_specs=ex.specs, scratch_shapes=ex.scratch,
        compiler_params=pltpu.CompilerParams(has_side_effects=True),
    )(*arrays)


def _pack(arrays):
    flat = jnp.concatenate([a.reshape(-1).astype(F32) for a in arrays])
    pad = (-flat.shape[0]) % (SUBLANES * LANES)
    return jnp.pad(flat, (0, pad)).reshape(-1, LANES)


def _unpack(buf, like):
    flat = buf.reshape(-1)
    out, off = [], 0
    for a in like:
        sz = math.prod(a.shape)
        out.append(flat[off:off + sz].reshape(a.shape))
        off += sz
    return out


def _row(v):
    return v.reshape(1, -1)


def _layer_fwd(x, mod, p, l, ride=None, on_receive=None):
    s, d = x.shape
    sw = d // 2
    nh = d // LANES
    shift_m, scale_m, gate_m, shift_f, scale_f, gate_f = mod
    n = lambda tag: f"{tag}{l}"
    sv = {}

    h1, = _rowwise(lambda xv, g, sc, sh: (xv * _rms(xv) * g) * (1.0 + sc) + sh,
                   [x], [p['g_pre_mix'], scale_m, shift_m], [(d, BF, 'tile')], name=n("pre_mix"))
    proj_a = _mm(h1, p['w_in_a'], name=n("proj_a"))
    flog = _mm(h1, p['w_in_f'], name=n("proj_f"))
    gates = _mm(h1, p['w_in_g'], name=n("proj_g"))

    y_s5 = _s5_fwd(proj_a, p['wb_re'], p['wb_im'], p['wc_re'], p['wc_im'], p['lamb_re'], p['lamb_im'], p['d_skip'],
                   name=n("s5_fwd"))
    z, = _rowwise(_gelu, [y_s5], [], [(sw, BF, 'tile')], name=n("gelu"))
    tglu = _mm(z, p['w_glu'], name=n("glu_mm"))
    ys, = _rowwise(lambda yv, tv, b: _gelu(yv) * _sigmoid(tv + b), [y_s5, tglu], [p['b_glu']], [(sw, BF, 'tile')],
                   name=n("glu"))

    cumx = _cum_fwd(flog, p['b_f_row'], nh, name=n("cum_fwd"))
    ya, lse, *received = _attn_fwd(proj_a, cumx, sw, name=n("attn_fwd"), ride=ride)
    if on_receive is not None:
        on_receive(received)

    am = _mm(ys, p['w_pa'], name=n("pa_mm"))
    bm, merged = _mm_fused(ya, [p['w_pb']], [am, (gates, 0), (gates, 1)],
                           lambda b, a, ga, gb: (b, _sigmoid(ga) * a + _sigmoid(gb) * b), [F32, BF],
                           name=n("pb_mm"))
    ym = _mm(merged, p['w_o'], name=n("o_mm"))
    def post_mix_pre_ffn(xv, yv, g, gt, g2, sc, sh):
        x2v = xv + gt * (yv * _rms(yv) * g)
        return x2v, (x2v * _rms(x2v) * g2) * (1.0 + sc) + sh

    x2, h2 = _rowwise(post_mix_pre_ffn, [x, ym], [p['g_post_mix'], gate_m, p['g_pre_ffn'], scale_f, shift_f],
                      [(d, F32, 'tile'), (d, BF, 'tile')], name=n("post_mix_pre_ffn"))
    gt, up, act = _mm_fused(h2, [p['w_ffn_gate'], p['w_ffn_up']], [], lambda g, u: (g, u, _silu(g) * u),
                            [F32, F32, BF], name=n("gate_up_mm"))
    yf = _mm(act, p['w_ffn_down'], name=n("down_mm"))
    x3, = _rowwise(lambda xv, yv, g, gt_: xv + gt_ * (yv * _rms(yv) * g),
                   [x2, yf], [p['g_post_ffn'], gate_f], [(d, F32, 'tile')], name=n("post_ffn"))

    sv.update(x=x, h1=h1, proj_a=proj_a, flog=flog, gates=gates, y_s5=y_s5, z=z, tglu=tglu, ys=ys, cumx=cumx,
              ya=ya, lse=lse, am=am, bm=bm, merged=merged, ym=ym, x2=x2, h2=h2, gt=gt, up=up,
              act=act, yf=yf)
    return x3, sv


def _layer_bwd(dx3, sv, mod, p, l, make_ride=None, on_receive=None, carried=None, defer_tail=False):
    x, x2 = sv['x'], sv['x2']
    s, d = x.shape
    sw = d // 2
    nh = d // LANES
    shift_m, scale_m, gate_m, shift_f, scale_f, gate_f = mod
    n = lambda tag: f"{tag}{l}"
    gw, gs = {}, {}

    def post_bwd(dxo, yv, g, gate):
        r = _rms(yv)
        nf = yv * r
        dn = dxo * gate * g
        return _norm_bwd(dn, nf, r), dxo * (nf * g), dxo * gate * nf

    def pre_bwd(dh, dres, xv, g, sc):
        r = _rms(xv)
        xh = xv * r
        n3 = xh * g
        dn3 = dh * (1.0 + sc)
        return dres + _norm_bwd(dn3 * g, xh, r), dh, dh * n3, dn3 * xh

    dyf, dgate_f, gs['g_post_ffn'] = _rowwise(
        post_bwd, [dx3, sv['yf']], [p['g_post_ffn'], gate_f],
        [(d, BF, 'tile'), (d, F32, 'sum'), (d, F32, 'sum')], name=n("post_ffn_bwd"))
    gw['w_ffn_down'] = _mm(sv['act'], dyf, ta=True, out_dtype=BF, tm=1408, name=n("down_bwd_w"))

    def swiglu_bwd(da, g, u):
        sg = _sigmoid(g)
        return da * u * (sg * (1.0 + g * (1.0 - sg))), da * (g * sg)

    dgt, dup = _mm_fused(dyf, [p['w_ffn_down']], [sv['gt'], sv['up']], swiglu_bwd, [BF, BF], tb=True,
                         name=n("down_bwd_x"))
    dh2 = _mm(dgt, p['w_ffn_gate'], tb=True, tm=1024, second=(dup, p['w_ffn_up']), name=n("gate_up_bwd_x"))
    gw['w_ffn_gate'] = _mm(sv['h2'], dgt, ta=True, out_dtype=BF, tn=1408, name=n("gate_bwd_w"))
    gw['w_ffn_up'] = _mm(sv['h2'], dup, ta=True, out_dtype=BF, tn=1408, name=n("up_bwd_w"))
    def pre_ffn_post_mix_bwd(dh, dres, xv, yv, g, sc, g2, gate):
        dx2v, dsh, dsc, dg = pre_bwd(dh, dres, xv, g, sc)
        return (dx2v, dsh, dsc, dg) + post_bwd(dx2v, yv, g2, gate)

    dx2, dshift_f, dscale_f, gs['g_pre_ffn'], dym, dgate_m, gs['g_post_mix'] = _rowwise(
        pre_ffn_post_mix_bwd, [dh2, dx3, x2, sv['ym']], [p['g_pre_ffn'], scale_f, p['g_post_mix'], gate_m],
        [(d, F32, 'tile'), (d, F32, 'sum'), (d, F32, 'sum'), (d, F32, 'sum'),
         (d, BF, 'tile'), (d, F32, 'sum'), (d, F32, 'sum')], name=n("pre_ffn_post_mix_bwd"))
    gw['w_o'] = _mm(sv['merged'], dym, ta=True, out_dtype=BF, name=n("o_bwd_w"))

    def merge_bwd(dm, a, b, ga, gb):
        sa, sb = _sigmoid(ga), _sigmoid(gb)
        return dm * sa, dm * sb, dm * a * sa * (1.0 - sa), dm * b * sb * (1.0 - sb)

    da_, db_, dga, dgb = _mm_fused(dym, [p['w_o']], [sv['am'], sv['bm'], (sv['gates'], 0), (sv['gates'], 1)],
                                   merge_bwd, [BF] * 4, tb=True, name=n("o_bwd_x"))
    dys = _mm(da_, p['w_pa'], tb=True, name=n("pa_bwd_x"))
    gw['w_pa'] = _mm(sv['ys'], da_, ta=True, out_dtype=BF, name=n("pa_bwd_w"))
    dya = _mm(db_, p['w_pb'], tb=True, name=n("pb_bwd_x"))
    gw['w_pb'] = _mm(sv['ya'], db_, ta=True, out_dtype=BF, name=n("pb_bwd_w"))

    sent = list(gw)
    dq, dk, dv, dkc, dqc, *received = _attn_bwd(
        sv['proj_a'], dya, sv['ya'], sv['lse'], sv['cumx'], sw, name=n("attn_bwd"),
        ride=make_ride({k: gw[k] for k in sent}) if make_ride is not None else None)
    if on_receive is not None:
        on_receive(sent, received)
    frow = sv['flog'][:, :nh].T
    dcum = jnp.stack([-dkc.reshape(nh, s), dqc.reshape(nh, s)])
    dfrow, dbf = _cum_bwd(dcum, frow, p['b_f_col'], name=n("cum_bwd"))
    gs['b_f'] = dbf.reshape(nh)
    dflog = jnp.pad(dfrow.T, ((0, 0), (0, LANES - nh))).astype(BF)

    def glu_bwd(dy_, yv, tv, b):
        zv = _gelu(yv)
        sg = _sigmoid(tv + b)
        dt = dy_ * zv * sg * (1.0 - sg)
        return dt, dy_ * sg, dt

    dt, dz1, gs['b_glu'] = _rowwise(glu_bwd, [dys, sv['y_s5'], sv['tglu']], [p['b_glu']],
                                    [(sw, BF, 'tile'), (sw, F32, 'tile'), (sw, F32, 'sum')], name=n("glu_bwd"))
    dz2 = _mm(dt, p['w_glu'], tb=True, name=n("glu_bwd_x"))
    gw['w_glu'] = _mm(sv['z'], dt, ta=True, out_dtype=BF, name=n("glu_bwd_w"))
    dy_s5, = _rowwise(lambda a, b, yv: (a + b) * _gelu_grad(yv), [dz1, dz2, sv['y_s5']], [], [(sw, F32, 'tile')],
                      name=n("gelu_bwd"))
    du, dwbr, dwbi, dwcr, dwci, dlr, dli, gs['d_skip'], *received = _s5_bwd(
        sv['proj_a'], dy_s5, p['wb_re'], p['wb_im'], p['wc_re'], p['wc_im'], p['lamb_re'], p['lamb_im'], p['d_skip'],
        name=n("s5_bwd"), ride=make_ride(carried[1]) if carried else None)
    if carried:
        carried[0](list(carried[1]), received)
    g_ = sw // SSM_H
    pst = p['lamb_re'].shape[1] // g_
    gs['lamb_re'], gs['lamb_im'] = dlr.reshape(g_, pst), dli.reshape(g_, pst)
    gs['bbar_re'] = _s5_block_grads(dwbr, SSM_H, pst, False).transpose(0, 2, 1)
    gs['bbar_im'] = _s5_block_grads(dwbi, SSM_H, pst, False).transpose(0, 2, 1)
    gs['c_re'] = _s5_block_grads(dwcr, pst, SSM_H, True).transpose(0, 2, 1)
    gs['c_im'] = _s5_block_grads(dwci, pst, SSM_H, True).transpose(0, 2, 1)

    dproj = jnp.concatenate([du.astype(BF), dq.astype(BF), dk.astype(BF), dv.astype(BF), dflog, dga, dgb], axis=1)
    gw['w_in'] = _mm(sv['h1'], dproj, ta=True, out_dtype=BF, tn=1408, name=n("proj_bwd_w"))
    if make_ride is not None:
        gw = {k: g for k, g in gw.items() if k not in sent}
    if make_ride is not None and not defer_tail:
        dh1, received = _mm(dproj, p['w_in_all'], tb=True, tk=1408, name=n("proj_bwd_x"), ride=make_ride(gw))
        on_receive(list(gw), received)
        gw = {}
    else:
        dh1 = _mm(dproj, p['w_in_all'], tb=True, tk=1408, name=n("proj_bwd_x"))
    dx, dshift_m, dscale_m, gs['g_pre_mix'] = _rowwise(
        pre_bwd, [dh1, dx2, x], [p['g_pre_mix'], scale_m],
        [(d, F32, 'tile'), (d, F32, 'sum'), (d, F32, 'sum'), (d, F32, 'sum')], name=n("pre_mix_bwd"))
    dmod = [dshift_m, dscale_m, dgate_m, dshift_f, dscale_f, dgate_f]
    return dx, gw, dmod, gs


def _unshard(k, blocks):
    if k in COL_SHARDED:
        return blocks.transpose(1, 0, 2).reshape(blocks.shape[1], NDEV * blocks.shape[2])
    return blocks.reshape(NDEV * blocks.shape[1], blocks.shape[2])


def _to_slabs(k, g):
    if k == 'w_in':
        d = g.shape[0]
        nh = d // LANES
        g = jnp.concatenate([g[:, :2 * d + nh], g[:, 2 * d + LANES:]], axis=1)
    if k in COL_SHARDED:
        return g.reshape(g.shape[0], NDEV, g.shape[1] // NDEV).transpose(1, 0, 2)
    return g.reshape(NDEV, g.shape[0] // NDEV, g.shape[1])


def _prep_w_in(w_in):
    d = w_in.shape[0]
    nh = d // LANES
    fcol = 2 * d
    p = {}
    p['w_in_a'] = w_in[:, :fcol]
    p['w_in_f'] = jnp.pad(w_in[:, fcol:fcol + nh], ((0, 0), (0, LANES - nh)))
    p['w_in_g'] = w_in[:, fcol + nh:]
    p['w_in_all'] = jnp.concatenate([p['w_in_a'], p['w_in_f'], p['w_in_g']], axis=1)
    return p


def _prep_small(small):
    nh = small['b_f'].shape[0]
    p = {}
    for k in ('g_pre_mix', 'g_post_mix', 'g_pre_ffn', 'g_post_ffn', 'd_skip', 'b_glu'):
        p[k] = _row(small[k])
    p['b_f_row'] = jnp.pad(_row(small['b_f']), ((0, 0), (0, LANES - nh)))
    p['b_f_col'] = small['b_f'].reshape(nh, 1)
    lbr, lbi, bbr, bbi = _s5_discretise(small['lam_re'], small['lam_im'], small['log_dt'], small['b_re'], small['b_im'])
    p['lamb_re'], p['lamb_im'] = _row(lbr), _row(lbi)
    p['wb_re'] = _s5_operand(bbr.transpose(0, 2, 1), True).astype(BF)
    p['wb_im'] = _s5_operand(bbi.transpose(0, 2, 1), True).astype(BF)
    p['wc_re'] = _s5_operand(small['c_re'].transpose(0, 2, 1), False).astype(BF)
    p['wc_im'] = _s5_operand(small['c_im'].transpose(0, 2, 1), False).astype(BF)
    return p


def _local_step(x, target, mods, ps, small, hooks=None):
    depth = len(ps)
    s, d = x.shape
    hooks = hooks or {}
    saved = []
    h = x
    for l in range(depth):
        h, sv = _layer_fwd(h, mods[l], ps[l], l, ride=hooks['fwd_ride'](l) if hooks else None,
                           on_receive=functools.partial(hooks['fwd_recv'], l) if hooks else None)
        saved.append(sv)

    def loss_fn(yv, tv):
        e = yv - tv
        return e * (1.0 / d), jnp.sum(e * e, axis=1, keepdims=True) * (0.5 / d)

    dy, loss = _rowwise(loss_fn, [h, target], [], [(d, F32, 'tile'), (1, F32, 'sum')], name="loss")
    dmods, gss = [None] * depth, [None] * depth
    unsent = {}
    carried = None
    for l in range(depth - 1, -1, -1):
        def on_receive(names, results, l=l):
            hooks['bwd_recv']([(k, l) for k in names], results)

        dy, gw, dmods[l], gs = _layer_bwd(dy, saved[l], mods[l], ps[l], l,
                                          make_ride=hooks['bwd_ride'] if hooks else None,
                                          on_receive=on_receive if hooks else None,
                                          carried=carried, defer_tail=bool(hooks) and l > 0)
        if hooks and l > 0:
            carried = (on_receive, gw)
        else:
            unsent.update({(k, l): g for k, g in gw.items()})
        sm = small[l]
        _, vjp = jax.vjp(_s5_discretise, sm['lam_re'], sm['lam_im'], sm['log_dt'], sm['b_re'], sm['b_im'])
        gs['lam_re'], gs['lam_im'], gs['log_dt'], gs['b_re'], gs['b_im'] = vjp(
            (gs.pop('lamb_re'), gs.pop('lamb_im'), gs.pop('bbar_re'), gs.pop('bbar_im')))
        gss[l] = gs
    return loss, dy, unsent, dmods, gss


SMALL_LOCAL = ['g_pre_mix', 'g_post_mix', 'g_pre_ffn', 'g_post_ffn', 'lam_re', 'lam_im', 'log_dt', 'b_re', 'b_im',
               'c_re', 'c_im', 'd_skip', 'b_glu', 'b_f']


def kernel(x, c, w_ada, b_ada, g_pre_mix, g_post_mix, g_pre_ffn, g_post_ffn, w_in, lam_re, lam_im, log_dt, b_re, b_im, c_re, c_im, d_skip, w_glu, b_glu, b_f, w_pa, w_pb, w_o, w_ffn_gate, w_ffn_up, w_ffn_down, loss_target, m_w_ada, m_b_ada, m_g_pre_mix, m_g_post_mix, m_g_pre_ffn, m_g_post_ffn, m_w_in, m_lam_re, m_lam_im, m_log_dt, m_b_re, m_b_im, m_c_re, m_c_im, m_d_skip, m_w_glu, m_b_glu, m_b_f, m_w_pa, m_w_pb, m_w_o, m_w_ffn_gate, m_w_ffn_up, m_w_ffn_down, v_w_ada, v_b_ada, v_g_pre_mix, v_g_post_mix, v_g_pre_ffn, v_g_post_ffn, v_w_in, v_lam_re, v_lam_im, v_log_dt, v_b_re, v_b_im, v_c_re, v_c_im, v_d_skip, v_w_glu, v_b_glu, v_b_f, v_w_pa, v_w_pb, v_w_o, v_w_ffn_gate, v_w_ffn_up, v_w_ffn_down):
    args = dict(locals())
    W = {k: args[k] for k in WEIGHTS}
    M = {k: args['m_' + k] for k in WEIGHTS}
    V = {k: args['v_' + k] for k in WEIGHTS}
    depth, d, ncol = w_ada.shape
    s = x.shape[1]
    me = 4 * lax.axis_index("x") + 2 * lax.axis_index("y") + lax.axis_index("c")

    c_all, = _exchange([jnp.pad(c, ((0, SUBLANES - 1), (0, 0)))], True, name="gather_c")
    c_all = c_all[:, 0, :]
    cond, = _rowwise(_silu, [c_all], [], [(d, F32, 'tile')], name="cond")
    mod_part = jnp.stack([_mm(cond, w_ada[l], name=f"ada_mm{l}") for l in range(depth)], axis=1)
    mod_recv, = _exchange([mod_part.reshape(NDEV, depth, 1, ncol)], False, name="scatter_mod")
    mod_cat = mod_recv.reshape(NDEV, depth, ncol).transpose(1, 0, 2).reshape(depth, NDEV * ncol)
    mod, = _rowwise(lambda a, b: a + b, [mod_cat, b_ada], [], [(NDEV * ncol, F32, 'tile')], name="mod_bias")
    mods = [[mod[l:l + 1, i * d:(i + 1) * d] for i in range(6)] for l in range(depth)]

    small = [{k: W[k][l] for k in SMALL_LOCAL} for l in range(depth)]
    ps = [_prep_small(small[l]) for l in range(depth)]
    first = ['w_in', 'w_glu']
    rest = [k for k in BIG if k not in first]
    riding = [[(k, l) for k in rest] + [(k, l + 1) for k in first if l + 1 < depth] for l in range(depth)]

    def take_weights(keys, results):
        for (k, l), blocks in zip(keys, results):
            full = _unshard(k, blocks)
            ps[l].update(_prep_w_in(full) if k == 'w_in' else {k: full})

    take_weights([(k, 0) for k in first],
                 _exchange([W[k][0].astype(BF) for k in first], True, name="gather_w_first"))

    def fwd_ride(l):
        blocks = [W[k][ll].astype(BF) for k, ll in riding[l]]
        return _Exchange(blocks, True), blocks

    grad_parts = {}

    def bwd_ride(grads):
        slabs = [_to_slabs(k, g) for k, g in grads.items()]
        return _Exchange(slabs, False), slabs

    hooks = dict(fwd_ride=fwd_ride, fwd_recv=lambda l, results: take_weights(riding[l], results),
                 bwd_ride=bwd_ride, bwd_recv=lambda keys, results: grad_parts.update(zip(keys, results)))

    loss, dx, unsent, dmods, gss = _local_step(x[0], loss_target[0], mods, ps, small, hooks)
    assert not unsent
    loss = lax.psum(loss[0, 0], ("x", "y", "c"))
    out = {}
    for k in BIG:
        out[k] = _adamw([grad_parts[(k, l)] for l in range(depth)], W[k], M[k], V[k], name=f"adamw_{k}")

    dmod_mine = jnp.stack([jnp.concatenate(dmods[l], axis=1)[0] for l in range(depth)])
    small_mine = [dmod_mine] + [jnp.stack([gss[l][k] for l in range(depth)]) for k in SMALL_LOCAL]
    parts, = _exchange([_pack(small_mine)], True, name="gather_small")
    summed = _sum_parts(parts, name="sum_small")
    names = ['b_ada'] + SMALL_LOCAL
    for k, g in zip(names, _unpack(summed, [W[k] for k in names])):
        shp = W[k].shape
        rows = lambda a: a.reshape(depth, -1, shp[-1])
        res = _adamw([rows(g)[l][None] for l in range(depth)], rows(W[k]), rows(M[k]), rows(V[k]), name=f"adamw_{k}")
        out[k] = [a.reshape(shp) for a in res]

    dmod_all = parts.reshape(NDEV, -1)[:, :depth * 6 * d].reshape(NDEV, depth, 6 * d)
    dmod_cols = lax.dynamic_slice_in_dim(dmod_all, me * ncol, ncol, axis=2)
    g_ada = [_mm(cond, dmod_cols[:, l], ta=True, precision=HI, name=f"ada_bwd{l}")[None] for l in range(depth)]
    out['w_ada'] = _adamw(g_ada, w_ada, m_w_ada, v_w_ada, name="adamw_w_ada")

    return (loss, dx[None], *[out[k][0] for k in WEIGHTS], *[out[k][1] for k in WEIGHTS],
            *[out[k][2] for k in WEIGHTS], *[out[k][3] for k in WEIGHTS])
```

```python
import functools
import math

import jax
import jax.numpy as jnp
from jax import lax
from jax.experimental import pallas as pl
from jax.experimental.pallas import tpu as pltpu

F32 = jnp.float32
BF = jnp.bfloat16
NDEV = 8
LANES = 128
SUBLANES = 8
VMEM_LIMIT = 48 * 1024 * 1024

SSM_H = 16
HEAD_DIM = 64
RMS_EPS = 1e-6
EIG_CLIP = 1e-4
ADAM_LR = 0.001
ADAM_B1 = 0.9
ADAM_B2 = 0.999
ADAM_EPS = 1e-08
ADAM_WD = 0.01
ADAM_STEP = 10
NEG = -1e30
HI = lax.Precision.HIGHEST

WEIGHTS = ['w_ada', 'b_ada', 'g_pre_mix', 'g_post_mix', 'g_pre_ffn', 'g_post_ffn', 'w_in', 'lam_re', 'lam_im',
           'log_dt', 'b_re', 'b_im', 'c_re', 'c_im', 'd_skip', 'w_glu', 'b_glu', 'b_f', 'w_pa', 'w_pb', 'w_o',
           'w_ffn_gate', 'w_ffn_up', 'w_ffn_down']
TRANSPOSED = ['w_ffn_gate', 'w_ffn_up', 'b_re', 'b_im']
COL_SHARDED = ['w_in', 'w_pa', 'w_pb']
ROW_SHARDED = ['w_glu', 'w_o', 'w_ffn_down', 'w_ffn_gate', 'w_ffn_up']
BIG = COL_SHARDED + ROW_SHARDED
SMALL = ['b_ada', 'g_pre_mix', 'g_post_mix', 'g_pre_ffn', 'g_post_ffn', 'lam_re', 'lam_im', 'log_dt', 'b_re',
         'b_im', 'c_re', 'c_im', 'd_skip', 'b_glu', 'b_f']


def _fit(dim, target, align):
    if dim <= target:
        return dim
    t = (target // align) * align
    while t >= align:
        if dim % t == 0:
            return t
        t -= align
    return dim


def _params(**kw):
    return pltpu.CompilerParams(vmem_limit_bytes=VMEM_LIMIT, **kw)


def _mm(a, b, *, ta=False, tb=False, out_dtype=F32, tm=None, tn=512, tk=2048, precision=None, name, ride=None,
        second=None):
    m, k = (a.shape[1], a.shape[0]) if ta else a.shape
    n = b.shape[0] if tb else b.shape[1]
    assert (b.shape[1] if tb else b.shape[0]) == k
    tm = _fit(m, tm or (1024 if ta else 2048), LANES if ta else 16)
    tn = _fit(n, tn, LANES)
    tk = _fit(k, tk, LANES)
    nk = k // tk
    grid = (m // tm, n // tn, nk)
    dims = (((0 if ta else 1,), (1 if tb else 0,)), ((), ()))
    ex, ex_arrays = ride if ride is not None else (None, [])

    pairs = [(a, b)] + ([second] if second is not None else [])

    def kern(*refs):
        ab_refs, (o_ref,), comm, scratch = _ride_split(ex, refs, 2 * len(pairs), 1)
        step = (pl.program_id(0) * grid[1] + pl.program_id(1)) * grid[2] + pl.program_id(2)
        if ex is not None:
            @pl.when(step == 0)
            def _():
                ex.start(*comm)

            @pl.when(step == (grid[0] * grid[1] * grid[2]) // 2)
            def _():
                ex.forward(*comm)

        p = None
        for a_ref, b_ref in zip(ab_refs[::2], ab_refs[1::2]):
            av, bv = a_ref[...], b_ref[...]
            if precision is None:
                av, bv = av.astype(BF), bv.astype(BF)
            q = lax.dot_general(av, bv, dims, preferred_element_type=F32, precision=precision)
            p = q if p is None else p + q
        if nk == 1:
            o_ref[...] = p.astype(out_dtype)
        else:
            acc_ref, = scratch
            kk = pl.program_id(2)

            @pl.when(kk == 0)
            def _():
                acc_ref[...] = p

            @pl.when(kk > 0)
            def _():
                acc_ref[...] += p

            @pl.when(kk == nk - 1)
            def _():
                o_ref[...] = acc_ref[...].astype(out_dtype)

        if ex is not None:
            @pl.when(step == grid[0] * grid[1] * grid[2] - 1)
            def _():
                ex.wait(*comm)

    a_spec = pl.BlockSpec((tk, tm), lambda i, j, kk: (kk, i)) if ta else pl.BlockSpec((tm, tk), lambda i, j, kk: (i, kk))
    b_spec = pl.BlockSpec((tn, tk), lambda i, j, kk: (j, kk)) if tb else pl.BlockSpec((tk, tn), lambda i, j, kk: (kk, j))
    res = pl.pallas_call(
        kern, name=name,
        out_shape=[jax.ShapeDtypeStruct((m, n), out_dtype)] + (ex.out_shape if ex else []),
        grid=grid,
        in_specs=[a_spec, b_spec] * len(pairs) + (ex.specs if ex else []),
        out_specs=[pl.BlockSpec((tm, tn), lambda i, j, kk: (i, j))] + (ex.specs if ex else []),
        scratch_shapes=(ex.scratch if ex else []) + ([] if nk == 1 else [pltpu.VMEM((tm, tn), F32)]),
        compiler_params=_params(dimension_semantics=("arbitrary",) * 3 if ex else ("parallel", "parallel", "arbitrary"),
                                has_side_effects=ex is not None),
    )(*[x for pair in pairs for x in pair], *ex_arrays)
    return (res[0], res[1:]) if ex else res[0]


def _mm_fused(a, bs, extras, fn, out_dtypes, *, tb=False, tm=2048, tn=256, name):
    m, k = a.shape
    n = bs[0].shape[0] if tb else bs[0].shape[1]
    tm = _fit(m, tm, 16)
    tn = _fit(n, tn, LANES)
    extras = [e if isinstance(e, tuple) else (e, 0) for e in extras]
    nb, ne = len(bs), len(extras)
    dims = (((1,), (1 if tb else 0,)), ((), ()))

    def kern(*refs):
        av = refs[0][...].astype(BF)
        prods = [lax.dot_general(av, r[...].astype(BF), dims, preferred_element_type=F32) for r in refs[1:1 + nb]]
        res = fn(*prods, *[r[...] for r in refs[1 + nb:1 + nb + ne]])
        for o_ref, r, dt in zip(refs[1 + nb + ne:], res, out_dtypes):
            o_ref[...] = r.astype(dt)

    tile = pl.BlockSpec((tm, tn), lambda i, j: (i, j))
    b_spec = pl.BlockSpec((tn, k), lambda i, j: (j, 0)) if tb else pl.BlockSpec((k, tn), lambda i, j: (0, j))
    return pl.pallas_call(
        kern, name=name, out_shape=[jax.ShapeDtypeStruct((m, n), dt) for dt in out_dtypes],
        grid=(m // tm, n // tn),
        in_specs=[pl.BlockSpec((tm, k), lambda i, j: (i, 0))] + [b_spec] * nb
        + [pl.BlockSpec((tm, tn), lambda i, j, c=c: (i, j + c * (n // tn))) for _, c in extras],
        out_specs=[tile] * len(out_dtypes),
        compiler_params=_params(dimension_semantics=("parallel", "parallel")),
    )(a, *bs, *[e for e, _ in extras])


def _rowwise(fn, tiles, params, outs, *, tr=256, name):
    tiles = [t if isinstance(t, tuple) else (t, t.shape[1], 0) for t in tiles]
    s = tiles[0][0].shape[0]
    tr = _fit(s, tr, 16)
    nt, npar = len(tiles), len(params)

    def kern(*refs):
        i = pl.program_id(0)
        res = fn(*[r[...] for r in refs[:nt + npar]])
        if not isinstance(res, (tuple, list)):
            res = (res,)
        for (w, dt, kind), o_ref, r in zip(outs, refs[nt + npar:], res):
            if kind == 'tile':
                o_ref[...] = r.astype(dt)
            else:
                part = jnp.sum(r.astype(F32), axis=0, keepdims=True)

                @pl.when(i == 0)
                def _(o_ref=o_ref, part=part):
                    o_ref[...] = part

                @pl.when(i > 0)
                def _(o_ref=o_ref, part=part):
                    o_ref[...] += part

    def tile_spec(w, cb):
        return pl.BlockSpec((tr, w), lambda i: (i, cb))

    in_specs = [tile_spec(w, cb) for _, w, cb in tiles]
    in_specs += [pl.BlockSpec(p.shape, lambda i, nd=p.ndim: (0,) * nd) for p in params]
    out_shape, out_specs = [], []
    for w, dt, kind in outs:
        if kind == 'tile':
            out_shape.append(jax.ShapeDtypeStruct((s, w), dt))
            out_specs.append(pl.BlockSpec((tr, w), lambda i: (i, 0)))
        else:
            out_shape.append(jax.ShapeDtypeStruct((1, w), F32))
            out_specs.append(pl.BlockSpec((1, w), lambda i: (0, 0)))
    res = pl.pallas_call(
        kern, name=name, out_shape=out_shape, grid=(s // tr,), in_specs=in_specs, out_specs=out_specs,
        compiler_params=_params(dimension_semantics=("arbitrary",)),
    )(*[t[0] for t in tiles], *params)
    return res


def _sigmoid(z):
    return 1.0 / (1.0 + jnp.exp(-z))


def _silu(z):
    return z * _sigmoid(z)


_GELU_K = math.sqrt(2.0 / math.pi)


def _gelu(y):
    return 0.5 * y * (1.0 + jnp.tanh(_GELU_K * (y + 0.044715 * y * y * y)))


def _gelu_grad(y):
    th = jnp.tanh(_GELU_K * (y + 0.044715 * y * y * y))
    return 0.5 * (1.0 + th) + 0.5 * y * (1.0 - th * th) * _GELU_K * (1.0 + 3.0 * 0.044715 * y * y)


def _rms(x):
    return lax.rsqrt(jnp.mean(x * x, axis=-1, keepdims=True) + RMS_EPS)


def _norm_bwd(dn, xhat, r):
    return r * (dn - xhat * jnp.mean(dn * xhat, axis=-1, keepdims=True))


def _cum_fwd(flog, bf_row, nh, *, name):
    s = flog.shape[0]
    w = nh * HEAD_DIM
    t = _fit(s, 256, SUBLANES)

    def kern(f_ref, b_ref, o_ref, carry_ref):
        i = pl.program_id(0)

        @pl.when(i == 0)
        def _():
            carry_ref[...] = jnp.zeros_like(carry_ref)

        z = f_ref[...] + b_ref[...]
        logf = jnp.minimum(z, 0.0) - jnp.log(1.0 + jnp.exp(-jnp.abs(z)))
        hh = lax.broadcasted_iota(jnp.int32, (LANES, w), 0)
        cc = lax.broadcasted_iota(jnp.int32, (LANES, w), 1)
        expand = (cc // HEAD_DIM == hh).astype(F32)
        lx = jnp.dot(logf, expand, preferred_element_type=F32, precision=HI)
        rr = lax.broadcasted_iota(jnp.int32, (t, t), 0)
        kk = lax.broadcasted_iota(jnp.int32, (t, t), 1)
        tri = (kk <= rr).astype(F32)
        cum = jnp.dot(tri, lx, preferred_element_type=F32, precision=HI) + carry_ref[...]
        o_ref[...] = cum
        carry_ref[...] = cum[t - 1:t, :]

    return pl.pallas_call(
        kern, name=name, out_shape=jax.ShapeDtypeStruct((s, w), F32), grid=(s // t,),
        in_specs=[pl.BlockSpec((t, LANES), lambda i: (i, 0)), pl.BlockSpec((1, LANES), lambda i: (0, 0))],
        out_specs=pl.BlockSpec((t, w), lambda i: (i, 0)),
        scratch_shapes=[pltpu.VMEM((1, w), F32)],
        compiler_params=_params(dimension_semantics=("arbitrary",)),
    )(flog, bf_row)


def _cum_bwd(dcrow, frow, bf_col, *, name):
    _, nh, s = dcrow.shape
    t = _fit(s, 512, LANES)
    nb = s // t

    def kern(d_ref, f_ref, b_ref, df_ref, db_ref):
        rr = lax.broadcasted_iota(jnp.int32, (t, t), 0)
        kk = lax.broadcasted_iota(jnp.int32, (t, t), 1)
        upper = (rr >= kk).astype(F32)
        carry = jnp.zeros((nh, 1), F32)
        db = jnp.zeros((nh, 1), F32)
        for blk in range(nb - 1, -1, -1):
            sl = slice(blk * t, (blk + 1) * t)
            rc = jnp.dot(d_ref[0, :, sl] + d_ref[1, :, sl], upper, preferred_element_type=F32, precision=HI) + carry
            carry = rc[:, 0:1]
            df = rc * _sigmoid(-(f_ref[:, sl] + b_ref[...]))
            df_ref[:, sl] = df
            db = db + jnp.sum(df, axis=1, keepdims=True)
        db_ref[...] = db

    return pl.pallas_call(
        kern, name=name,
        out_shape=[jax.ShapeDtypeStruct((nh, s), F32), jax.ShapeDtypeStruct((nh, 1), F32)],
        compiler_params=_params(),
    )(dcrow, frow, bf_col)


def _ride_split(ex, refs, n_in, n_out):
    n = ex.n if ex is not None else 0
    own_in, srcs = refs[:n_in], refs[n_in:n_in + n]
    own_out, dsts = refs[n_in + n:n_in + n + n_out], refs[n_in + n + n_out:n_in + 2 * n + n_out]
    sems = refs[n_in + 2 * n + n_out:n_in + 2 * n + n_out + 3] if n else ()
    rest = refs[n_in + 2 * n + n_out + (3 if n else 0):]
    return own_in, own_out, (srcs, dsts, sems), rest


ATTN_STRIP = 32
BIAS_LANES = 3


def _head_masks(rows):
    lane = lax.broadcasted_iota(jnp.int32, (rows, LANES), 1)
    return [(lane >= HEAD_DIM * e) & (lane < HEAD_DIM * (e + 1)) for e in range(2)]


def _augment(feat, bias, e, *, bias_slot, ones_slot):
    rows = feat.shape[0]
    lane = lax.broadcasted_iota(jnp.int32, (rows, LANES), 1)
    own = (lane >= HEAD_DIM * e) & (lane < HEAD_DIM * (e + 1))
    off = lane - HEAD_DIM * (1 - e)
    out = jnp.where(own, feat, 0.0)
    if ones_slot is not None:
        out = jnp.where((off >= ones_slot * BIAS_LANES) & (off < (ones_slot + 1) * BIAS_LANES), 1.0, out)
    if bias is not None:
        rest = pltpu.roll(bias, HEAD_DIM, 1)
        for term in range(BIAS_LANES):
            part = rest.astype(BF).astype(F32)
            out = jnp.where(off == bias_slot * BIAS_LANES + term, part, out)
            rest = rest - part
    return out.astype(BF)


def _two_slot_pipeline(m, scores, tile):
    scores(0, 0)

    def pair(n, carry):
        k = 2 * n
        scores(k + 1, 1)
        tile(k, 0, False)
        scores(k + 2, 0)
        tile(k + 1, 1, False)
        return carry

    lax.fori_loop(0, m // 2, pair, 0)

    @pl.when(m % 2 == 0)
    def _():
        tile(m, 0, True)

    @pl.when(m % 2 == 1)
    def _():
        scores(m, 1)
        tile(m - 1, 0, False)
        tile(m, 1, True)


def _attn_fwd(proj, cumx, qcol, *, name, ride=None):
    s = proj.shape[0]
    w = cumx.shape[1]
    nhp = w // LANES
    t = _fit(s, 256, LANES)
    nq = s // t
    strip = _fit(t, ATTN_STRIP, 16)
    scale = HEAD_DIM ** -0.5
    qb, kb, vb = qcol // LANES, (qcol + w) // LANES, (qcol + 2 * w) // LANES
    ex, ex_arrays = ride if ride is not None else (None, [])
    nt_dims = (((1,), (1,)), ((), ()))

    def kern(*refs):
        own_in, (o_ref, l_ref), comm, scratch = _ride_split(ex, refs, 5, 2)
        q_ref, k_ref, v_ref, cxq_ref, cxk_ref = own_in
        ka_ref, vat_ref, s0_ref, s1_ref, p_ref, m_ref, acc_ref = scratch
        s_refs = (s0_ref, s1_ref)
        i = pl.program_id(1)
        if ex is not None:
            @pl.when((pl.program_id(0) == 0) & (i == 0))
            def _():
                ex.start(*comm)

            @pl.when((pl.program_id(0) == nhp - 1) & (i == 0))
            def _():
                ex.forward(*comm)

        msks = _head_masks(t)

        @pl.when(i == 0)
        def _():
            def build(c, carry):
                rows = pl.ds(pl.multiple_of(c * t, LANES), t)
                k2, v2, cx = k_ref[rows, :], v_ref[rows, :], cxk_ref[rows, :]
                for e in range(2):
                    ka_ref[e, rows, :] = _augment(k2, -cx, e, bias_slot=1, ones_slot=0)
                    vat_ref[e, :, rows] = jnp.where(msks[e], v2, 1.0).T.astype(BF)
                return carry
            lax.fori_loop(0, nq, build, 0)

        q2 = q_ref[...] * scale
        qa = [_augment(q2, cxq_ref[...], e, bias_slot=0, ones_slot=1) for e in range(2)]
        m_ref[...] = jnp.full(m_ref.shape, NEG, F32)
        acc_ref[...] = jnp.zeros(acc_ref.shape, F32)
        slabs = strip // SUBLANES

        def scores(j, slot):
            rows_k = pl.ds(pl.multiple_of(j * t, LANES), t)
            for e in range(2):
                st = lax.dot_general(ka_ref[e, rows_k, :], qa[e], nt_dims, preferred_element_type=F32)
                s_refs[slot][e] = st.reshape(t // SUBLANES, SUBLANES, t)

        def tile(j, slot, diagonal):
            rows_k = pl.ds(pl.multiple_of(j * t, LANES), t)
            s_ref = s_refs[slot]
            for e in range(2):
                mx = jnp.full((SUBLANES, t), NEG, F32)
                for r in range(t // strip):
                    sl = slice(r * slabs, (r + 1) * slabs)
                    sv = s_ref[e,sl]
                    if diagonal:
                        shape = (slabs, SUBLANES, t)
                        key = (r * strip + lax.broadcasted_iota(jnp.int32, shape, 0) * SUBLANES
                               + lax.broadcasted_iota(jnp.int32, shape, 1))
                        sv = jnp.where(key <= lax.broadcasted_iota(jnp.int32, shape, 2), sv, NEG)
                        s_ref[e,sl] = sv
                    mx = jnp.maximum(mx, jnp.max(sv, axis=0))
                for sh in (4, 2, 1):
                    mx = jnp.maximum(mx, pltpu.roll(mx, sh, 0))
                m_old = m_ref[e]
                m_new = jnp.maximum(m_old, mx)
                alpha = jnp.exp(m_old - m_new)
                m_ref[e] = m_new
                for r in range(t // strip):
                    p = jnp.exp(s_ref[e,r * slabs:(r + 1) * slabs] - m_new[None])
                    p_ref[e, r * strip:(r + 1) * strip, :] = p.reshape(strip, t).astype(BF)
                acc = acc_ref[e].reshape(LANES // SUBLANES, SUBLANES, t) * alpha[None]
                acc_ref[e] = acc.reshape(LANES, t) + jnp.dot(vat_ref[e, :, rows_k], p_ref[e],
                                                             preferred_element_type=F32)

        _two_slot_pipeline(i, scores, tile)

        outs, lses = [], []
        for e in range(2):
            acc = acc_ref[e]
            other = HEAD_DIM * (1 - e)
            den = acc[other:other + 1, :]
            outs.append(acc / den)
            lses.append(jnp.broadcast_to(m_ref[e][0:1, :] + jnp.log(den), (LANES, t)))
        upper = lax.broadcasted_iota(jnp.int32, (LANES, t), 0) < HEAD_DIM
        o_ref[...] = jnp.where(upper, outs[0], outs[1]).T.astype(BF)
        l_ref[...] = jnp.where(upper, lses[0], lses[1]).T
        if ex is not None:
            @pl.when((pl.program_id(0) == nhp - 1) & (i == nq - 1))
            def _():
                ex.wait(*comm)

    own_scratch = [pltpu.VMEM((2, s, LANES), BF), pltpu.VMEM((2, LANES, s), BF),
                   pltpu.VMEM((2, t // SUBLANES, SUBLANES, t), F32),
                   pltpu.VMEM((2, t // SUBLANES, SUBLANES, t), F32), pltpu.VMEM((2, t, t), BF),
                   pltpu.VMEM((2, SUBLANES, t), F32), pltpu.VMEM((2, LANES, t), F32)]
    return pl.pallas_call(
        kern, name=name,
        out_shape=[jax.ShapeDtypeStruct((s, w), BF), jax.ShapeDtypeStruct((nhp, s, LANES), F32)]
        + (ex.out_shape if ex else []),
        grid=(nhp, nq),
        in_specs=[pl.BlockSpec((t, LANES), lambda h, i: (i, qb + h)),
                  pl.BlockSpec((s, LANES), lambda h, i: (0, kb + h)),
                  pl.BlockSpec((s, LANES), lambda h, i: (0, vb + h)),
                  pl.BlockSpec((t, LANES), lambda h, i: (i, h)),
                  pl.BlockSpec((s, LANES), lambda h, i: (0, h))] + (ex.specs if ex else []),
        out_specs=[pl.BlockSpec((t, LANES), lambda h, i: (i, h)),
                   pl.BlockSpec((None, t, LANES), lambda h, i: (h, i, 0))] + (ex.specs if ex else []),
        scratch_shapes=(ex.scratch if ex else []) + own_scratch,
        compiler_params=_params(dimension_semantics=("arbitrary", "arbitrary"),
                                has_side_effects=ex is not None),
    )(proj, proj, proj, cumx, cumx, *ex_arrays)


def _attn_bwd(proj, do, o, lse, cumx, qcol, *, name, ride=None):
    s = proj.shape[0]
    w = cumx.shape[1]
    nhp = w // LANES
    t = _fit(s, 256, LANES)
    nq = s // t
    strip = _fit(t, ATTN_STRIP, 16)
    scale = HEAD_DIM ** -0.5
    qb, kb, vb = qcol // LANES, (qcol + w) // LANES, (qcol + 2 * w) // LANES
    tn_dims = (((0,), (0,)), ((), ()))
    nt_dims = (((1,), (1,)), ((), ()))
    ex, ex_arrays = ride if ride is not None else (None, [])

    def kern(*refs):
        own_in, own_out, comm, scratch = _ride_split(ex, refs, 7, 5)
        q_ref, k_ref, v_ref, do_ref, o_ref, l_ref, cx_ref = own_in
        dq_ref, dk_ref, dv_ref, dkc_ref, dqc_ref = own_out
        qa_ref, da_ref, dqa_ref, dka_ref, dva_ref, st0_ref, st1_ref, dpt0_ref, dpt1_ref, pt_ref, dst_ref = scratch
        st_refs, dpt_refs = (st0_ref, st1_ref), (dpt0_ref, dpt1_ref)
        j = pl.program_id(1)
        if ex is not None:
            @pl.when((pl.program_id(0) == 0) & (j == 0))
            def _():
                ex.start(*comm)

        msks = _head_masks(t)

        @pl.when(j == 0)
        def _():
            def build(c, carry):
                rows = pl.ds(pl.multiple_of(c * t, LANES), t)
                q2 = q_ref[rows, :] * scale
                do2 = do_ref[rows, :]
                dd = do2 * o_ref[rows, :].astype(F32)
                delta = jnp.where(msks[0], jnp.sum(jnp.where(msks[0], dd, 0.0), axis=1, keepdims=True),
                                  jnp.sum(jnp.where(msks[1], dd, 0.0), axis=1, keepdims=True))
                bias = cx_ref[rows, :] - l_ref[rows, :]
                for e in range(2):
                    qa_ref[e, rows, :] = _augment(q2, bias, e, bias_slot=0, ones_slot=1)
                    da_ref[e, rows, :] = _augment(do2, -delta, e, bias_slot=0, ones_slot=None)
                return carry
            lax.fori_loop(0, nq, build, 0)
            dqa_ref[...] = jnp.zeros(dqa_ref.shape, F32)

        rows_k = pl.ds(pl.multiple_of(j * t, LANES), t)
        k2, v2 = k_ref[...], v_ref[...]
        ka = [_augment(k2, -cx_ref[rows_k, :], e, bias_slot=1, ones_slot=0) for e in range(2)]
        va = [_augment(v2, None, e, bias_slot=None, ones_slot=0) for e in range(2)]
        dka_ref[...] = jnp.zeros(dka_ref.shape, F32)
        dva_ref[...] = jnp.zeros(dva_ref.shape, F32)

        def scores(k, slot):
            rows_q = pl.ds(pl.multiple_of((nq - 1 - k) * t, LANES), t)
            for e in range(2):
                st_refs[slot][e] = lax.dot_general(ka[e], qa_ref[e, rows_q, :], nt_dims,
                                                   preferred_element_type=F32)
                dpt_refs[slot][e] = lax.dot_general(va[e], da_ref[e, rows_q, :], nt_dims,
                                                    preferred_element_type=F32)

        def tile(k, slot, diagonal):
            rows_q = pl.ds(pl.multiple_of((nq - 1 - k) * t, LANES), t)
            st_ref, dpt_ref = st_refs[slot], dpt_refs[slot]
            for e in range(2):
                for r in range(t // strip):
                    rows = slice(r * strip, (r + 1) * strip)
                    sv = st_ref[e, rows, :]
                    if diagonal:
                        key = r * strip + lax.broadcasted_iota(jnp.int32, (strip, t), 0)
                        qry = lax.broadcasted_iota(jnp.int32, (strip, t), 1)
                        sv = jnp.where(key <= qry, sv, NEG)
                    p = jnp.exp(sv)
                    pt_ref[e, rows, :] = p.astype(BF)
                    dst_ref[e, rows, :] = (p * dpt_ref[e, rows, :]).astype(BF)
            for e in range(2):
                dva_ref[e] += jnp.dot(pt_ref[e], da_ref[e, rows_q, :], preferred_element_type=F32)
                dka_ref[e] += jnp.dot(dst_ref[e], qa_ref[e, rows_q, :], preferred_element_type=F32)
                dqa_ref[e, rows_q, :] += lax.dot_general(dst_ref[e], ka[e], tn_dims, preferred_element_type=F32)

        _two_slot_pipeline(nq - 1 - j, scores, tile)

        dk_ref[...] = jnp.where(msks[0], dka_ref[0], dka_ref[1])
        dv_ref[...] = jnp.where(msks[0], dva_ref[0], dva_ref[1])
        sums = jnp.where(msks[1], dka_ref[0], dka_ref[1]).T
        dkc_ref[0:1, :] = sums[HEAD_DIM + BIAS_LANES:HEAD_DIM + BIAS_LANES + 1, :]
        dkc_ref[1:2, :] = sums[BIAS_LANES:BIAS_LANES + 1, :]

        @pl.when(j == nq - 1)
        def _():
            def flush(c, carry):
                rows = pl.ds(pl.multiple_of(c * t, LANES), t)
                a0, a1 = dqa_ref[0, rows, :], dqa_ref[1, rows, :]
                dq_ref[rows, :] = jnp.where(msks[0], a0, a1) * scale
                sums = jnp.where(msks[1], a0, a1).T
                dqc_ref[0:1, rows] = sums[HEAD_DIM:HEAD_DIM + 1, :]
                dqc_ref[1:2, rows] = sums[0:1, :]
                return carry
            lax.fori_loop(0, nq, flush, 0)

        if ex is not None:
            @pl.when((pl.program_id(0) == nhp - 1) & (j == nq - 1))
            def _():
                ex.wait(*comm)

    full = lambda cb: pl.BlockSpec((s, LANES), lambda h, j: (0, cb + h))
    blk = lambda cb: pl.BlockSpec((t, LANES), lambda h, j: (j, cb + h))
    own_scratch = [pltpu.VMEM((2, s, LANES), BF), pltpu.VMEM((2, s, LANES), BF), pltpu.VMEM((2, s, LANES), F32),
                   pltpu.VMEM((2, t, LANES), F32), pltpu.VMEM((2, t, LANES), F32),
                   pltpu.VMEM((2, t, t), F32), pltpu.VMEM((2, t, t), F32),
                   pltpu.VMEM((2, t, t), F32), pltpu.VMEM((2, t, t), F32),
                   pltpu.VMEM((2, t, t), BF), pltpu.VMEM((2, t, t), BF)]
    return pl.pallas_call(
        kern, name=name,
        out_shape=[jax.ShapeDtypeStruct((s, w), F32)] * 3 + [jax.ShapeDtypeStruct((nhp, 2, s), F32)] * 2
        + (ex.out_shape if ex else []),
        grid=(nhp, nq),
        in_specs=[full(qb), blk(kb), blk(vb), full(0), full(0),
                  pl.BlockSpec((None, s, LANES), lambda h, j: (h, 0, 0)), full(0)] + (ex.specs if ex else []),
        out_specs=[full(0), blk(0), blk(0), pl.BlockSpec((None, 2, t), lambda h, j: (h, 0, j)),
                   pl.BlockSpec((None, 2, s), lambda h, j: (h, 0, 0))] + (ex.specs if ex else []),
        scratch_shapes=(ex.scratch if ex else []) + own_scratch,
        compiler_params=_params(dimension_semantics=("arbitrary", "arbitrary"),
                                has_side_effects=ex is not None),
    )(proj, proj, proj, do, o, lse, cumx, *ex_arrays)


S5_STATES = 256
S5_ROWS = 512


def _cmul(ar, ai, br, bi):
    return ar * br - ai * bi, ar * bi + ai * br


def _scan_tables(lr, li, reverse):
    w = lr.shape[1]
    row = lax.broadcasted_iota(jnp.int32, (SUBLANES, w), 0)
    if reverse:
        row = SUBLANES - 1 - row
    lr1, li1 = jnp.broadcast_to(lr, (SUBLANES, w)), jnp.broadcast_to(li, (SUBLANES, w))
    lr2, li2 = _cmul(lr1, li1, lr1, li1)
    lr4, li4 = _cmul(lr2, li2, lr2, li2)
    steps = []
    for d, (pr, pi) in zip((1, 2, 4), ((lr1, li1), (lr2, li2), (lr4, li4))):
        keep = row >= d
        steps.append((jnp.where(keep, pr, 0.0), jnp.where(keep, pi, 0.0)))
    cr, ci = lr1, li1
    for bit, (pr, pi) in zip((1, 2, 4), ((lr1, li1), (lr2, li2), (lr4, li4))):
        nr, ni = _cmul(cr, ci, pr, pi)
        has = (row & bit) != 0
        cr, ci = jnp.where(has, nr, cr), jnp.where(has, ni, ci)
    return steps, (cr, ci)


def _scan_local(xr, xi, steps, reverse):
    for d, (pr, pi) in zip((1, 2, 4), steps):
        sh = (SUBLANES - d) if reverse else d
        sr, si = pltpu.roll(xr, sh, 0), pltpu.roll(xi, sh, 0)
        xr, xi = xr + (pr * sr - pi * si), xi + (pr * si + pi * sr)
    return xr, xi


def _scan_carry(xr, xi, car_r, car_i, carry_pow):
    cr, ci = carry_pow
    return xr + (cr * car_r - ci * car_i), xi + (cr * car_i + ci * car_r)


SCAN_UNROLL = 4


def _s5_specs(s, ncb):
    u_spec = pl.BlockSpec((s, LANES), lambda cb, hf: (0, cb))
    wb_spec = pl.BlockSpec((None, None, LANES, S5_STATES), lambda cb, hf: (cb, hf, 0, 0))
    wc_spec = pl.BlockSpec((None, None, S5_STATES, LANES), lambda cb, hf: (cb, hf, 0, 0))
    lam_spec = pl.BlockSpec((1, S5_STATES), lambda cb, hf: (0, 2 * cb + hf))
    d_spec = pl.BlockSpec((1, LANES), lambda cb, hf: (0, cb))
    return u_spec, wb_spec, wc_spec, lam_spec, d_spec


def _s5_project_and_scan(u_ref, wbr_ref, wbi_ref, lr_ref, li_ref, xr_ref, xi_ref, s, rows):
    wbr, wbi = wbr_ref[...], wbi_ref[...]
    for r in range(s // rows):
        sl = pl.ds(r * rows, rows)
        ub = u_ref[sl, :].astype(BF)
        xr_ref[sl, :] = jnp.dot(ub, wbr, preferred_element_type=F32)
        xi_ref[sl, :] = jnp.dot(ub, wbi, preferred_element_type=F32)
    steps, cpow = _scan_tables(lr_ref[...], li_ref[...], False)

    unroll = _fit(s // SUBLANES, SCAN_UNROLL, 1)

    def body(b, carry):
        car_r, car_i = carry
        sls = [pl.ds(pl.multiple_of((b * unroll + q) * SUBLANES, SUBLANES), SUBLANES) for q in range(unroll)]
        blocks = [_scan_local(xr_ref[sl, :], xi_ref[sl, :], steps, False) for sl in sls]
        for sl, (xr, xi) in zip(sls, blocks):
            xr, xi = _scan_carry(xr, xi, car_r, car_i, cpow)
            xr_ref[sl, :] = xr
            xi_ref[sl, :] = xi
            car_r, car_i = xr[SUBLANES - 1:SUBLANES, :], xi[SUBLANES - 1:SUBLANES, :]
        return car_r, car_i

    zero = jnp.zeros((1, S5_STATES), F32)
    lax.fori_loop(0, s // SUBLANES // unroll, body, (zero, zero))


def _s5_fwd(proj, wb_re, wb_im, wc_re, wc_im, lam_re, lam_im, dskip, *, name):
    s = proj.shape[0]
    w = dskip.shape[1]
    ncb = w // LANES
    rows = _fit(s, S5_ROWS, SUBLANES)

    def kern(u_ref, wbr_ref, wbi_ref, wcr_ref, wci_ref, lr_ref, li_ref, d_ref, y_ref, xr_ref, xi_ref):
        hf = pl.program_id(1)
        _s5_project_and_scan(u_ref, wbr_ref, wbi_ref, lr_ref, li_ref, xr_ref, xi_ref, s, rows)
        wcr, wci = wcr_ref[...], wci_ref[...]
        for r in range(s // rows):
            sl = pl.ds(r * rows, rows)
            y = (jnp.dot(xr_ref[sl, :].astype(BF), wcr, preferred_element_type=F32)
                 - jnp.dot(xi_ref[sl, :].astype(BF), wci, preferred_element_type=F32))

            @pl.when(hf == 0)
            def _(y=y, sl=sl):
                y_ref[sl, :] = y + d_ref[...] * u_ref[sl, :]

            @pl.when(hf == 1)
            def _(y=y, sl=sl):
                y_ref[sl, :] += y

    u_spec, wb_spec, wc_spec, lam_spec, d_spec = _s5_specs(s, ncb)
    return pl.pallas_call(
        kern, name=name, out_shape=jax.ShapeDtypeStruct((s, w), F32), grid=(ncb, 2),
        in_specs=[u_spec, wb_spec, wb_spec, wc_spec, wc_spec, lam_spec, lam_spec, d_spec],
        out_specs=u_spec,
        scratch_shapes=[pltpu.VMEM((s, S5_STATES), F32), pltpu.VMEM((s, S5_STATES), F32)],
        compiler_params=_params(dimension_semantics=("parallel", "arbitrary")),
    )(proj, wb_re, wb_im, wc_re, wc_im, lam_re, lam_im, dskip)


def _s5_bwd(proj, dy, wb_re, wb_im, wc_re, wc_im, lam_re, lam_im, dskip, *, name, ride=None):
    s = proj.shape[0]
    w = dskip.shape[1]
    ncb = w // LANES
    rows = _fit(s, S5_ROWS, SUBLANES)
    tn_dims = (((0,), (0,)), ((), ()))
    nt_dims = (((1,), (1,)), ((), ()))
    ex, ex_arrays = ride if ride is not None else (None, [])

    def kern(*refs):
        own_in, own_out, comm, scratch = _ride_split(ex, refs, 9, 8)
        u_ref, dy_ref, wbr_ref, wbi_ref, wcr_ref, wci_ref, lr_ref, li_ref, d_ref = own_in
        du_ref, dwbr_ref, dwbi_ref, dwcr_ref, dwci_ref, dlr_ref, dli_ref, dd_ref = own_out
        xr_ref, xi_ref, gr_ref, gi_ref = scratch
        hf = pl.program_id(1)
        if ex is not None:
            @pl.when((pl.program_id(0) == 0) & (hf == 0))
            def _():
                ex.start(*comm)

        _s5_project_and_scan(u_ref, wbr_ref, wbi_ref, lr_ref, li_ref, xr_ref, xi_ref, s, rows)

        wcr, wci = wcr_ref[...], wci_ref[...]
        dwcr = jnp.zeros((S5_STATES, LANES), F32)
        dwci = jnp.zeros((S5_STATES, LANES), F32)
        ddsk = jnp.zeros((1, LANES), F32)
        for r in range(s // rows):
            sl = pl.ds(r * rows, rows)
            dyf = dy_ref[sl, :]
            dyb = dyf.astype(BF)
            gr_ref[sl, :] = lax.dot_general(dyb, wcr, nt_dims, preferred_element_type=F32)
            gi_ref[sl, :] = -lax.dot_general(dyb, wci, nt_dims, preferred_element_type=F32)
            dwcr = dwcr + lax.dot_general(xr_ref[sl, :].astype(BF), dyb, tn_dims, preferred_element_type=F32)
            dwci = dwci - lax.dot_general(xi_ref[sl, :].astype(BF), dyb, tn_dims, preferred_element_type=F32)
            ddsk = ddsk + jnp.sum(dyf * u_ref[sl, :], axis=0, keepdims=True)
        dwcr_ref[...] = dwcr
        dwci_ref[...] = dwci

        @pl.when(hf == 0)
        def _():
            dd_ref[...] = ddsk

        steps, cpow = _scan_tables(lr_ref[...], -li_ref[...], True)
        row = lax.broadcasted_iota(jnp.int32, (SUBLANES, S5_STATES), 0)
        nblk = s // SUBLANES

        unroll = _fit(nblk, SCAN_UNROLL, 1)

        def body(k, carry):
            car_r, car_i, ar, ai = carry
            sls = [pl.ds(pl.multiple_of((nblk - 1 - k * unroll - q) * SUBLANES, SUBLANES), SUBLANES)
                   for q in range(unroll)]
            blocks = [_scan_local(gr_ref[sl, :], gi_ref[sl, :], steps, True) for sl in sls]
            for sl, (g_r, g_i) in zip(sls, blocks):
                g_r, g_i = _scan_carry(g_r, g_i, car_r, car_i, cpow)
                gr_ref[sl, :] = g_r
                gi_ref[sl, :] = g_i
                nr = jnp.where(row == SUBLANES - 1, car_r, pltpu.roll(g_r, SUBLANES - 1, 0))
                ni = jnp.where(row == SUBLANES - 1, car_i, pltpu.roll(g_i, SUBLANES - 1, 0))
                xr, xi = xr_ref[sl, :], xi_ref[sl, :]
                ar = ar + (xr * nr + xi * ni)
                ai = ai + (xr * ni - xi * nr)
                car_r, car_i = g_r[0:1, :], g_i[0:1, :]
            return car_r, car_i, ar, ai

        zero = jnp.zeros((1, S5_STATES), F32)
        zacc = jnp.zeros((SUBLANES, S5_STATES), F32)
        _, _, ar, ai = lax.fori_loop(0, nblk // unroll, body, (zero, zero, zacc, zacc))
        dlr_ref[...] = jnp.sum(ar, axis=0, keepdims=True)
        dli_ref[...] = jnp.sum(ai, axis=0, keepdims=True)

        wbr, wbi = wbr_ref[...], wbi_ref[...]
        dwbr = jnp.zeros((LANES, S5_STATES), F32)
        dwbi = jnp.zeros((LANES, S5_STATES), F32)
        for r in range(s // rows):
            sl = pl.ds(r * rows, rows)
            grb, gib = gr_ref[sl, :].astype(BF), gi_ref[sl, :].astype(BF)
            ub = u_ref[sl, :].astype(BF)
            dwbr = dwbr + lax.dot_general(ub, grb, tn_dims, preferred_element_type=F32)
            dwbi = dwbi + lax.dot_general(ub, gib, tn_dims, preferred_element_type=F32)
            du = (lax.dot_general(grb, wbr, nt_dims, preferred_element_type=F32)
                  + lax.dot_general(gib, wbi, nt_dims, preferred_element_type=F32))

            @pl.when(hf == 0)
            def _(du=du, sl=sl):
                du_ref[sl, :] = du + d_ref[...] * dy_ref[sl, :]

            @pl.when(hf == 1)
            def _(du=du, sl=sl):
                du_ref[sl, :] += du
        dwbr_ref[...] = dwbr
        dwbi_ref[...] = dwbi
        if ex is not None:
            @pl.when((pl.program_id(0) == ncb - 1) & (hf == 1))
            def _():
                ex.wait(*comm)

    u_spec, wb_spec, wc_spec, lam_spec, d_spec = _s5_specs(s, ncb)
    dwb_spec = pl.BlockSpec((None, None, LANES, S5_STATES), lambda cb, hf: (cb, hf, 0, 0))
    dwc_spec = pl.BlockSpec((None, None, S5_STATES, LANES), lambda cb, hf: (cb, hf, 0, 0))
    state = pltpu.VMEM((s, S5_STATES), F32)
    return pl.pallas_call(
        kern, name=name,
        out_shape=[jax.ShapeDtypeStruct((s, w), F32),
                   jax.ShapeDtypeStruct((ncb, 2, LANES, S5_STATES), F32), jax.ShapeDtypeStruct((ncb, 2, LANES, S5_STATES), F32),
                   jax.ShapeDtypeStruct((ncb, 2, S5_STATES, LANES), F32), jax.ShapeDtypeStruct((ncb, 2, S5_STATES, LANES), F32),
                   jax.ShapeDtypeStruct((1, 4 * w), F32), jax.ShapeDtypeStruct((1, 4 * w), F32),
                   jax.ShapeDtypeStruct((1, w), F32)] + (ex.out_shape if ex else []),
        grid=(ncb, 2),
        in_specs=[u_spec, u_spec, wb_spec, wb_spec, wc_spec, wc_spec, lam_spec, lam_spec, d_spec]
        + (ex.specs if ex else []),
        out_specs=[u_spec, dwb_spec, dwb_spec, dwc_spec, dwc_spec, lam_spec, lam_spec, d_spec]
        + (ex.specs if ex else []),
        scratch_shapes=(ex.scratch if ex else []) + [state, state, state, state],
        compiler_params=_params(dimension_semantics=("arbitrary", "arbitrary"), has_side_effects=ex is not None),
    )(proj, dy, wb_re, wb_im, wc_re, wc_im, lam_re, lam_im, dskip, *ex_arrays)


def _s5_discretise(lam_re, lam_im, log_dt, b_re, b_im):
    lr = jnp.minimum(lam_re, -EIG_CLIP)
    li = lam_im
    dt = jnp.exp(log_dt)[:, None]
    mag = jnp.exp(lr * dt)
    lbr, lbi = mag * jnp.cos(li * dt), mag * jnp.sin(li * dt)
    den = lr * lr + li * li
    fr = ((lbr - 1.0) * lr + lbi * li) / den
    fi = (lbi * lr - (lbr - 1.0) * li) / den
    bbr = fr[:, None, :] * b_re - fi[:, None, :] * b_im
    bbi = fr[:, None, :] * b_im + fi[:, None, :] * b_re
    return lbr, lbi, bbr, bbi


def _s5_operand(mats, channels_first):
    g, a, b = mats.shape
    gl = LANES // 2 // SSM_H
    ncb = g // (2 * gl)
    m = mats.reshape(ncb, 2, gl, a, b)
    eye = jnp.eye(gl, dtype=mats.dtype)
    inner = (m[:, :, :, :, None, :] * eye[None, None, :, None, :, None]).reshape(ncb, 2, gl * a, gl * b)
    zeros = jnp.zeros_like(inner[:, 0])
    axis = 1 if channels_first else 2
    return jnp.stack([jnp.concatenate([inner[:, 0], zeros], axis=axis),
                      jnp.concatenate([zeros, inner[:, 1]], axis=axis)], axis=1)


def _s5_block_grads(dwb, a, b, transpose):
    ncb = dwb.shape[0]
    gl = LANES // 2 // (a if not transpose else b)
    if not transpose:
        d = dwb.reshape(ncb, 2, 2, gl, a, gl, b)
        parts = [[d[:, hf, hf, g, :, g, :] for g in range(gl)] for hf in range(2)]
    else:
        d = dwb.reshape(ncb, 2, gl, a, 2, gl, b)
        parts = [[d[:, hf, g, :, hf, g, :] for g in range(gl)] for hf in range(2)]
    st = jnp.stack([jnp.stack(p, axis=1) for p in parts], axis=1)
    return st.reshape(ncb * 2 * gl, a, b)


def _adamw(parts, w, m, v, *, name):
    depth, r, c = w.shape
    assert len(parts) == depth
    npart = parts[0].shape[0]
    row_bytes = 4 * (-(-c // LANES) * LANES)
    align = 16 if parts[0].dtype == BF else SUBLANES
    budget = VMEM_LIMIT // 2 // (2 * (depth * npart + 7) * row_bytes)
    tr = _fit(r, max(align, budget // align * align), align)
    nr = r // tr
    c1 = 1.0 / (1.0 - ADAM_B1 ** ADAM_STEP)
    c2 = 1.0 / (1.0 - ADAM_B2 ** ADAM_STEP)

    def kern(*refs):
        p_refs = refs[:depth]
        w_ref, m_ref, v_ref, g_ref, d_ref, nm_ref, nv_ref = refs[depth:]
        layer = pl.program_id(0)
        for l in range(depth):
            @pl.when(layer == l)
            def _(p_ref=p_refs[l]):
                g = p_ref[0].astype(F32)
                for q in range(1, npart):
                    g = g + p_ref[q].astype(F32)
                m2 = ADAM_B1 * m_ref[...] + (1.0 - ADAM_B1) * g
                v2 = ADAM_B2 * v_ref[...] + (1.0 - ADAM_B2) * (g * g)
                upd = (m2 * c1) / (jnp.sqrt(v2 * c2) + ADAM_EPS) + ADAM_WD * w_ref[...]
                g_ref[...] = g
                d_ref[...] = -ADAM_LR * upd
                nm_ref[...] = m2
                nv_ref[...] = v2

    def part_spec(l):
        return pl.BlockSpec((npart, tr, c),
                            lambda ly, i: (0, jnp.where(ly == l, i, jnp.where(ly < l, 0, nr - 1)), 0))

    spec = pl.BlockSpec((None, tr, c), lambda ly, i: (ly, i, 0))
    return pl.pallas_call(
        kern, name=name, out_shape=[jax.ShapeDtypeStruct((depth, r, c), F32)] * 4, grid=(depth, nr),
        in_specs=[part_spec(l) for l in range(depth)] + [spec, spec, spec],
        out_specs=[spec] * 4,
        compiler_params=_params(dimension_semantics=("arbitrary", "arbitrary")),
    )(*parts, w, m, v)


def _sum_parts(parts, *, name):
    npart, r, c = parts.shape

    def kern(p_ref, o_ref):
        g = p_ref[0]
        for q in range(1, npart):
            g = g + p_ref[q]
        o_ref[...] = g

    return pl.pallas_call(kern, name=name, out_shape=jax.ShapeDtypeStruct((r, c), F32), compiler_params=_params())(parts)


class _Exchange:
    def __init__(self, arrays, gather):
        self.n = len(arrays)
        self.gather = gather
        self.out_shape = [jax.ShapeDtypeStruct(((NDEV,) + a.shape) if gather else a.shape, a.dtype) for a in arrays]
        self.scratch = [pltpu.SemaphoreType.DMA((self.n, NDEV - 1)), pltpu.SemaphoreType.DMA((self.n, NDEV - 1)),
                        pltpu.SemaphoreType.DMA((self.n,))]
        self.specs = [pl.BlockSpec(memory_space=pl.ANY)] * self.n

    def _copies(self, srcs, dsts, sems):
        send_sems, recv_sems, local_sems = sems
        x, y, c = lax.axis_index("x"), lax.axis_index("y"), lax.axis_index("c")
        me = 4 * x + 2 * y + c
        local = [pltpu.make_async_copy(srcs[a] if self.gather else srcs[a].at[me], dsts[a].at[me], local_sems.at[a])
                 for a in range(self.n)]
        remote = []
        for k in (1, 2, 4, 3, 5, 6, 7):
            px, py, pc = x ^ ((k >> 2) & 1), y ^ ((k >> 1) & 1), c ^ (k & 1)
            peer = 4 * px + 2 * py + pc
            for a in range(self.n):
                src = srcs[a] if self.gather else srcs[a].at[peer]
                mk = functools.partial(
                    pltpu.make_async_remote_copy, src_ref=src,
                    send_sem=send_sems.at[a, k - 1], recv_sem=recv_sems.at[a, k - 1],
                    device_id=(px, py, pc), device_id_type=pl.DeviceIdType.MESH)
                remote.append((mk(dst_ref=dsts[a].at[me]), mk(dst_ref=dsts[a].at[peer])))
        return local, remote

    def _gather_copies(self, srcs, dsts, sems):
        send_sems, recv_sems, local_sems = sems
        x, y, c = lax.axis_index("x"), lax.axis_index("y"), lax.axis_index("c")
        block = lambda px, py, pc: 4 * px + 2 * py + pc
        me = block(x, y, c)
        chips = [(1 - x, y), (x, 1 - y), (1 - x, 1 - y)]
        local = [pltpu.make_async_copy(srcs[a], dsts[a].at[me], local_sems.at[a]) for a in range(self.n)]
        own, passed = [], []
        for a in range(self.n):
            def copy(k, blk, to, src=None, a=a):
                return pltpu.make_async_remote_copy(
                    src_ref=dsts[a].at[blk] if src is None else src, dst_ref=dsts[a].at[blk],
                    send_sem=send_sems.at[a, k], recv_sem=recv_sems.at[a, k],
                    device_id=to, device_id_type=pl.DeviceIdType.MESH)
            sib = (x, y, 1 - c)
            own.append((copy(0, me, sib, srcs[a]), copy(0, block(x, y, 1 - c), sib)))
            for j, (px, py) in enumerate(chips):
                own.append((copy(1 + j, me, (px, py, c), srcs[a]), copy(1 + j, block(px, py, c), (px, py, c))))
            for j, (px, py) in enumerate(chips):
                passed.append((copy(4 + j, block(px, py, c), sib), copy(4 + j, block(px, py, 1 - c), sib)))
        return local, own, passed

    def start(self, srcs, dsts, sems):
        if self.gather:
            local, own, _ = self._gather_copies(srcs, dsts, sems)
            for cp in local:
                cp.start()
            for send, _ in own:
                send.start()
            return
        local, remote = self._copies(srcs, dsts, sems)
        for cp in local:
            cp.start()
        for send, _ in remote:
            send.start()

    def forward(self, srcs, dsts, sems):
        if not self.gather:
            return
        _, own, passed = self._gather_copies(srcs, dsts, sems)
        for a in range(self.n):
            for j in range(3):
                own[4 * a + 1 + j][1].wait_recv()
                passed[3 * a + j][0].start()

    def wait(self, srcs, dsts, sems):
        if self.gather:
            local, own, passed = self._gather_copies(srcs, dsts, sems)
            for a in range(self.n):
                own[4 * a][1].wait_recv()
            for _, arrival in passed:
                arrival.wait_recv()
            for send, _ in own + passed:
                send.wait_send()
            for cp in local:
                cp.wait()
            return
        local, remote = self._copies(srcs, dsts, sems)
        for send, arrival in remote:
            send.wait_send()
            arrival.wait_recv()
        for cp in local:
            cp.wait()


def _exchange(arrays, gather, *, name):
    ex = _Exchange(arrays, gather)
    n = ex.n

    def kern(*refs):
        srcs, dsts, sems = refs[:n], refs[n:2 * n], refs[2 * n:]
        ex.start(srcs, dsts, sems)
        ex.forward(srcs, dsts, sems)
        ex.wait(srcs, dsts, sems)

    return pl.pallas_call(
        kern, name=name, out_shape=ex.out_shape, in_specs=ex.specs, out_specs=ex.specs, scratch_shapes=ex.scratch,
        compiler_params=pltpu.CompilerParams(has_side_effects=True),
    )(*arrays)


def _pack(arrays):
    flat = jnp.concatenate([a.reshape(-1).astype(F32) for a in arrays])
    pad = (-flat.shape[0]) % (SUBLANES * LANES)
    return jnp.pad(flat, (0, pad)).reshape(-1, LANES)


def _unpack(buf, like):
    flat = buf.reshape(-1)
    out, off = [], 0
    for a in like:
        sz = math.prod(a.shape)
        out.append(flat[off:off + sz].reshape(a.shape))
        off += sz
    return out


def _row(v):
    return v.reshape(1, -1)


def _layer_fwd(x, mod, p, l, ride=None, on_receive=None):
    s, d = x.shape
    sw = d // 2
    nh = d // LANES
    shift_m, scale_m, gate_m, shift_f, scale_f, gate_f = mod
    n = lambda tag: f"{tag}{l}"
    sv = {}

    h1, = _rowwise(lambda xv, g, sc, sh: (xv * _rms(xv) * g) * (1.0 + sc) + sh,
                   [x], [p['g_pre_mix'], scale_m, shift_m], [(d, BF, 'tile')], name=n("pre_mix"))
    proj_a = _mm(h1, p['w_in_a'], name=n("proj_a"))
    flog = _mm(h1, p['w_in_f'], name=n("proj_f"))
    gates = _mm(h1, p['w_in_g'], name=n("proj_g"))

    y_s5 = _s5_fwd(proj_a, p['wb_re'], p['wb_im'], p['wc_re'], p['wc_im'], p['lamb_re'], p['lamb_im'], p['d_skip'],
                   name=n("s5_fwd"))
    z, = _rowwise(_gelu, [y_s5], [], [(sw, BF, 'tile')], name=n("gelu"))
    tglu = _mm(z, p['w_glu'], name=n("glu_mm"))
    ys, = _rowwise(lambda yv, tv, b: _gelu(yv) * _sigmoid(tv + b), [y_s5, tglu], [p['b_glu']], [(sw, BF, 'tile')],
                   name=n("glu"))

    cumx = _cum_fwd(flog, p['b_f_row'], nh, name=n("cum_fwd"))
    ya, lse, *received = _attn_fwd(proj_a, cumx, sw, name=n("attn_fwd"), ride=ride)
    if on_receive is not None:
        on_receive(received)

    am = _mm(ys, p['w_pa'], name=n("pa_mm"))
    bm, merged = _mm_fused(ya, [p['w_pb']], [am, (gates, 0), (gates, 1)],
                           lambda b, a, ga, gb: (b, _sigmoid(ga) * a + _sigmoid(gb) * b), [F32, BF],
                           name=n("pb_mm"))
    ym = _mm(merged, p['w_o'], name=n("o_mm"))
    def post_mix_pre_ffn(xv, yv, g, gt, g2, sc, sh):
        x2v = xv + gt * (yv * _rms(yv) * g)
        return x2v, (x2v * _rms(x2v) * g2) * (1.0 + sc) + sh

    x2, h2 = _rowwise(post_mix_pre_ffn, [x, ym], [p['g_post_mix'], gate_m, p['g_pre_ffn'], scale_f, shift_f],
                      [(d, F32, 'tile'), (d, BF, 'tile')], name=n("post_mix_pre_ffn"))
    gt, up, act = _mm_fused(h2, [p['w_ffn_gate'], p['w_ffn_up']], [], lambda g, u: (g, u, _silu(g) * u),
                            [F32, F32, BF], tb=True, name=n("gate_up_mm"))
    yf = _mm(act, p['w_ffn_down'], name=n("down_mm"))
    x3, = _rowwise(lambda xv, yv, g, gt_: xv + gt_ * (yv * _rms(yv) * g),
                   [x2, yf], [p['g_post_ffn'], gate_f], [(d, F32, 'tile')], name=n("post_ffn"))

    sv.update(x=x, h1=h1, proj_a=proj_a, flog=flog, gates=gates, y_s5=y_s5, z=z, tglu=tglu, ys=ys, cumx=cumx,
              ya=ya, lse=lse, am=am, bm=bm, merged=merged, ym=ym, x2=x2, h2=h2, gt=gt, up=up,
              act=act, yf=yf)
    return x3, sv


def _layer_bwd(dx3, sv, mod, p, l, make_ride=None, on_receive=None, carried=None, defer_tail=False):
    x, x2 = sv['x'], sv['x2']
    s, d = x.shape
    sw = d // 2
    nh = d // LANES
    shift_m, scale_m, gate_m, shift_f, scale_f, gate_f = mod
    n = lambda tag: f"{tag}{l}"
    gw, gs = {}, {}

    def post_bwd(dxo, yv, g, gate):
        r = _rms(yv)
        nf = yv * r
        dn = dxo * gate * g
        return _norm_bwd(dn, nf, r), dxo * (nf * g), dxo * gate * nf

    def pre_bwd(dh, dres, xv, g, sc):
        r = _rms(xv)
        xh = xv * r
        n3 = xh * g
        dn3 = dh * (1.0 + sc)
        return dres + _norm_bwd(dn3 * g, xh, r), dh, dh * n3, dn3 * xh

    dyf, dgate_f, gs['g_post_ffn'] = _rowwise(
        post_bwd, [dx3, sv['yf']], [p['g_post_ffn'], gate_f],
        [(d, BF, 'tile'), (d, F32, 'sum'), (d, F32, 'sum')], name=n("post_ffn_bwd"))
    gw['w_ffn_down'] = _mm(sv['act'], dyf, ta=True, out_dtype=BF, tm=1408, name=n("down_bwd_w"))

    def swiglu_bwd(da, g, u):
        sg = _sigmoid(g)
        return da * u * (sg * (1.0 + g * (1.0 - sg))), da * (g * sg)

    dgt, dup = _mm_fused(dyf, [p['w_ffn_down']], [sv['gt'], sv['up']], swiglu_bwd, [BF, BF], tb=True,
                         name=n("down_bwd_x"))
    dh2 = _mm(dgt, p['w_ffn_gate'], tm=1024, second=(dup, p['w_ffn_up']), name=n("gate_up_bwd_x"))
    gw['w_ffn_gate'] = _mm(dgt, sv['h2'], ta=True, out_dtype=BF, tm=1408, name=n("gate_bwd_w"))
    gw['w_ffn_up'] = _mm(dup, sv['h2'], ta=True, out_dtype=BF, tm=1408, name=n("up_bwd_w"))
    def pre_ffn_post_mix_bwd(dh, dres, xv, yv, g, sc, g2, gate):
        dx2v, dsh, dsc, dg = pre_bwd(dh, dres, xv, g, sc)
        return (dx2v, dsh, dsc, dg) + post_bwd(dx2v, yv, g2, gate)

    dx2, dshift_f, dscale_f, gs['g_pre_ffn'], dym, dgate_m, gs['g_post_mix'] = _rowwise(
        pre_ffn_post_mix_bwd, [dh2, dx3, x2, sv['ym']], [p['g_pre_ffn'], scale_f, p['g_post_mix'], gate_m],
        [(d, F32, 'tile'), (d, F32, 'sum'), (d, F32, 'sum'), (d, F32, 'sum'),
         (d, BF, 'tile'), (d, F32, 'sum'), (d, F32, 'sum')], name=n("pre_ffn_post_mix_bwd"))
    gw['w_o'] = _mm(sv['merged'], dym, ta=True, out_dtype=BF, name=n("o_bwd_w"))

    def merge_bwd(dm, a, b, ga, gb):
        sa, sb = _sigmoid(ga), _sigmoid(gb)
        return dm * sa, dm * sb, dm * a * sa * (1.0 - sa), dm * b * sb * (1.0 - sb)

    da_, db_, dga, dgb = _mm_fused(dym, [p['w_o']], [sv['am'], sv['bm'], (sv['gates'], 0), (sv['gates'], 1)],
                                   merge_bwd, [BF] * 4, tb=True, name=n("o_bwd_x"))
    dys = _mm(da_, p['w_pa'], tb=True, name=n("pa_bwd_x"))
    gw['w_pa'] = _mm(sv['ys'], da_, ta=True, out_dtype=BF, name=n("pa_bwd_w"))
    dya = _mm(db_, p['w_pb'], tb=True, name=n("pb_bwd_x"))
    gw['w_pb'] = _mm(sv['ya'], db_, ta=True, out_dtype=BF, name=n("pb_bwd_w"))

    sent = list(gw)
    dq, dk, dv, dkc, dqc, *received = _attn_bwd(
        sv['proj_a'], dya, sv['ya'], sv['lse'], sv['cumx'], sw, name=n("attn_bwd"),
        ride=make_ride({k: gw[k] for k in sent}) if make_ride is not None else None)
    if on_receive is not None:
        on_receive(sent, received)
    frow = sv['flog'][:, :nh].T
    dcum = jnp.stack([-dkc.reshape(nh, s), dqc.reshape(nh, s)])
    dfrow, dbf = _cum_bwd(dcum, frow, p['b_f_col'], name=n("cum_bwd"))
    gs['b_f'] = dbf.reshape(nh)
    dflog = jnp.pad(dfrow.T, ((0, 0), (0, LANES - nh))).astype(BF)

    def glu_bwd(dy_, yv, tv, b):
        zv = _gelu(yv)
        sg = _sigmoid(tv + b)
        dt = dy_ * zv * sg * (1.0 - sg)
        return dt, dy_ * sg, dt

    dt, dz1, gs['b_glu'] = _rowwise(glu_bwd, [dys, sv['y_s5'], sv['tglu']], [p['b_glu']],
                                    [(sw, BF, 'tile'), (sw, F32, 'tile'), (sw, F32, 'sum')], name=n("glu_bwd"))
    dz2 = _mm(dt, p['w_glu'], tb=True, name=n("glu_bwd_x"))
    gw['w_glu'] = _mm(sv['z'], dt, ta=True, out_dtype=BF, name=n("glu_bwd_w"))
    dy_s5, = _rowwise(lambda a, b, yv: (a + b) * _gelu_grad(yv), [dz1, dz2, sv['y_s5']], [], [(sw, F32, 'tile')],
                      name=n("gelu_bwd"))
    du, dwbr, dwbi, dwcr, dwci, dlr, dli, gs['d_skip'], *received = _s5_bwd(
        sv['proj_a'], dy_s5, p['wb_re'], p['wb_im'], p['wc_re'], p['wc_im'], p['lamb_re'], p['lamb_im'], p['d_skip'],
        name=n("s5_bwd"), ride=make_ride(carried[1]) if carried else None)
    if carried:
        carried[0](list(carried[1]), received)
    g_ = sw // SSM_H
    pst = p['lamb_re'].shape[1] // g_
    gs['lamb_re'], gs['lamb_im'] = dlr.reshape(g_, pst), dli.reshape(g_, pst)
    gs['bbar_re'] = _s5_block_grads(dwbr, SSM_H, pst, False)
    gs['bbar_im'] = _s5_block_grads(dwbi, SSM_H, pst, False)
    gs['c_re'] = _s5_block_grads(dwcr, pst, SSM_H, True).transpose(0, 2, 1)
    gs['c_im'] = _s5_block_grads(dwci, pst, SSM_H, True).transpose(0, 2, 1)

    dproj = jnp.concatenate([du.astype(BF), dq.astype(BF), dk.astype(BF), dv.astype(BF), dflog, dga, dgb], axis=1)
    gw['w_in'] = _mm(sv['h1'], dproj, ta=True, out_dtype=BF, tn=1408, name=n("proj_bwd_w"))
    if make_ride is not None:
        gw = {k: g for k, g in gw.items() if k not in sent}
    if make_ride is not None and not defer_tail:
        dh1, received = _mm(dproj, p['w_in_all'], tb=True, tk=1408, name=n("proj_bwd_x"), ride=make_ride(gw))
        on_receive(list(gw), received)
        gw = {}
    else:
        dh1 = _mm(dproj, p['w_in_all'], tb=True, tk=1408, name=n("proj_bwd_x"))
    dx, dshift_m, dscale_m, gs['g_pre_mix'] = _rowwise(
        pre_bwd, [dh1, dx2, x], [p['g_pre_mix'], scale_m],
        [(d, F32, 'tile'), (d, F32, 'sum'), (d, F32, 'sum'), (d, F32, 'sum')], name=n("pre_mix_bwd"))
    dmod = [dshift_m, dscale_m, dgate_m, dshift_f, dscale_f, dgate_f]
    return dx, gw, dmod, gs


def _unshard(k, blocks):
    if k in COL_SHARDED:
        return blocks.transpose(1, 0, 2).reshape(blocks.shape[1], NDEV * blocks.shape[2])
    return blocks.reshape(NDEV * blocks.shape[1], blocks.shape[2])


def _to_slabs(k, g):
    if k == 'w_in':
        d = g.shape[0]
        nh = d // LANES
        g = jnp.concatenate([g[:, :2 * d + nh], g[:, 2 * d + LANES:]], axis=1)
    if k in COL_SHARDED:
        return g.reshape(g.shape[0], NDEV, g.shape[1] // NDEV).transpose(1, 0, 2)
    return g.reshape(NDEV, g.shape[0] // NDEV, g.shape[1])


def _prep_w_in(w_in):
    d = w_in.shape[0]
    nh = d // LANES
    fcol = 2 * d
    p = {}
    p['w_in_a'] = w_in[:, :fcol]
    p['w_in_f'] = jnp.pad(w_in[:, fcol:fcol + nh], ((0, 0), (0, LANES - nh)))
    p['w_in_g'] = w_in[:, fcol + nh:]
    p['w_in_all'] = jnp.concatenate([p['w_in_a'], p['w_in_f'], p['w_in_g']], axis=1)
    return p


def _prep_small(small):
    nh = small['b_f'].shape[0]
    p = {}
    for k in ('g_pre_mix', 'g_post_mix', 'g_pre_ffn', 'g_post_ffn', 'd_skip', 'b_glu'):
        p[k] = _row(small[k])
    p['b_f_row'] = jnp.pad(_row(small['b_f']), ((0, 0), (0, LANES - nh)))
    p['b_f_col'] = small['b_f'].reshape(nh, 1)
    lbr, lbi, bbr, bbi = _s5_discretise(small['lam_re'], small['lam_im'], small['log_dt'], small['b_re'], small['b_im'])
    p['lamb_re'], p['lamb_im'] = _row(lbr), _row(lbi)
    p['wb_re'] = _s5_operand(bbr, True).astype(BF)
    p['wb_im'] = _s5_operand(bbi, True).astype(BF)
    p['wc_re'] = _s5_operand(small['c_re'].transpose(0, 2, 1), False).astype(BF)
    p['wc_im'] = _s5_operand(small['c_im'].transpose(0, 2, 1), False).astype(BF)
    return p


def _local_step(x, target, mods, ps, small, hooks=None):
    depth = len(ps)
    s, d = x.shape
    hooks = hooks or {}
    saved = []
    h = x
    for l in range(depth):
        h, sv = _layer_fwd(h, mods[l], ps[l], l, ride=hooks['fwd_ride'](l) if hooks else None,
                           on_receive=functools.partial(hooks['fwd_recv'], l) if hooks else None)
        saved.append(sv)

    def loss_fn(yv, tv):
        e = yv - tv
        return e * (1.0 / d), jnp.sum(e * e, axis=1, keepdims=True) * (0.5 / d)

    dy, loss = _rowwise(loss_fn, [h, target], [], [(d, F32, 'tile'), (1, F32, 'sum')], name="loss")
    dmods, gss = [None] * depth, [None] * depth
    unsent = {}
    carried = None
    for l in range(depth - 1, -1, -1):
        def on_receive(names, results, l=l):
            hooks['bwd_recv']([(k, l) for k in names], results)

        dy, gw, dmods[l], gs = _layer_bwd(dy, saved[l], mods[l], ps[l], l,
                                          make_ride=hooks['bwd_ride'] if hooks else None,
                                          on_receive=on_receive if hooks else None,
                                          carried=carried, defer_tail=bool(hooks) and l > 0)
        if hooks and l > 0:
            carried = (on_receive, gw)
        else:
            unsent.update({(k, l): g for k, g in gw.items()})
        sm = small[l]
        _, vjp = jax.vjp(_s5_discretise, sm['lam_re'], sm['lam_im'], sm['log_dt'], sm['b_re'], sm['b_im'])
        gs['lam_re'], gs['lam_im'], gs['log_dt'], gs['b_re'], gs['b_im'] = vjp(
            (gs.pop('lamb_re'), gs.pop('lamb_im'), gs.pop('bbar_re'), gs.pop('bbar_im')))
        gss[l] = gs
    return loss, dy, unsent, dmods, gss


SMALL_LOCAL = ['g_pre_mix', 'g_post_mix', 'g_pre_ffn', 'g_post_ffn', 'lam_re', 'lam_im', 'log_dt', 'b_re', 'b_im',
               'c_re', 'c_im', 'd_skip', 'b_glu', 'b_f']


def kernel(x, c, w_ada, b_ada, g_pre_mix, g_post_mix, g_pre_ffn, g_post_ffn, w_in, lam_re, lam_im, log_dt, b_re, b_im, c_re, c_im, d_skip, w_glu, b_glu, b_f, w_pa, w_pb, w_o, w_ffn_gate, w_ffn_up, w_ffn_down, loss_target, m_w_ada, m_b_ada, m_g_pre_mix, m_g_post_mix, m_g_pre_ffn, m_g_post_ffn, m_w_in, m_lam_re, m_lam_im, m_log_dt, m_b_re, m_b_im, m_c_re, m_c_im, m_d_skip, m_w_glu, m_b_glu, m_b_f, m_w_pa, m_w_pb, m_w_o, m_w_ffn_gate, m_w_ffn_up, m_w_ffn_down, v_w_ada, v_b_ada, v_g_pre_mix, v_g_post_mix, v_g_pre_ffn, v_g_post_ffn, v_w_in, v_lam_re, v_lam_im, v_log_dt, v_b_re, v_b_im, v_c_re, v_c_im, v_d_skip, v_w_glu, v_b_glu, v_b_f, v_w_pa, v_w_pb, v_w_o, v_w_ffn_gate, v_w_ffn_up, v_w_ffn_down):
    args = dict(locals())
    view = lambda k, a: jnp.swapaxes(a, -1, -2) if k in TRANSPOSED else a
    W = {k: view(k, args[k]) for k in WEIGHTS}
    M = {k: view(k, args['m_' + k]) for k in WEIGHTS}
    V = {k: view(k, args['v_' + k]) for k in WEIGHTS}
    depth, d, ncol = w_ada.shape
    s = x.shape[1]
    me = 4 * lax.axis_index("x") + 2 * lax.axis_index("y") + lax.axis_index("c")

    c_all, = _exchange([jnp.pad(c, ((0, SUBLANES - 1), (0, 0)))], True, name="gather_c")
    c_all = c_all[:, 0, :]
    cond, = _rowwise(_silu, [c_all], [], [(d, F32, 'tile')], name="cond")
    mod_part = jnp.stack([_mm(cond, w_ada[l], name=f"ada_mm{l}") for l in range(depth)], axis=1)
    mod_recv, = _exchange([mod_part.reshape(NDEV, depth, 1, ncol)], False, name="scatter_mod")
    mod_cat = mod_recv.reshape(NDEV, depth, ncol).transpose(1, 0, 2).reshape(depth, NDEV * ncol)
    mod, = _rowwise(lambda a, b: a + b, [mod_cat, b_ada], [], [(NDEV * ncol, F32, 'tile')], name="mod_bias")
    mods = [[mod[l:l + 1, i * d:(i + 1) * d] for i in range(6)] for l in range(depth)]

    small = [{k: W[k][l] for k in SMALL_LOCAL} for l in range(depth)]
    ps = [_prep_small(small[l]) for l in range(depth)]
    first = ['w_in', 'w_glu']
    rest = [k for k in BIG if k not in first]
    riding = [[(k, l) for k in rest] + [(k, l + 1) for k in first if l + 1 < depth] for l in range(depth)]

    def take_weights(keys, results):
        for (k, l), blocks in zip(keys, results):
            full = _unshard(k, blocks)
            ps[l].update(_prep_w_in(full) if k == 'w_in' else {k: full})

    take_weights([(k, 0) for k in first],
                 _exchange([W[k][0].astype(BF) for k in first], True, name="gather_w_first"))

    def fwd_ride(l):
        blocks = [W[k][ll].astype(BF) for k, ll in riding[l]]
        return _Exchange(blocks, True), blocks

    grad_parts = {}

    def bwd_ride(grads):
        slabs = [_to_slabs(k, g) for k, g in grads.items()]
        return _Exchange(slabs, False), slabs

    hooks = dict(fwd_ride=fwd_ride, fwd_recv=lambda l, results: take_weights(riding[l], results),
                 bwd_ride=bwd_ride, bwd_recv=lambda keys, results: grad_parts.update(zip(keys, results)))

    loss, dx, unsent, dmods, gss = _local_step(x[0], loss_target[0], mods, ps, small, hooks)
    assert not unsent
    loss = lax.psum(loss[0, 0], ("x", "y", "c"))
    out = {}
    for k in BIG:
        out[k] = _adamw([grad_parts[(k, l)] for l in range(depth)], W[k], M[k], V[k], name=f"adamw_{k}")

    dmod_mine = jnp.stack([jnp.concatenate(dmods[l], axis=1)[0] for l in range(depth)])
    small_mine = [dmod_mine] + [jnp.stack([gss[l][k] for l in range(depth)]) for k in SMALL_LOCAL]
    parts, = _exchange([_pack(small_mine)], True, name="gather_small")
    summed = _sum_parts(parts, name="sum_small")
    names = ['b_ada'] + SMALL_LOCAL
    for k, g in zip(names, _unpack(summed, [W[k] for k in names])):
        shp = W[k].shape
        rows = lambda a: a.reshape(depth, -1, shp[-1])
        res = _adamw([rows(g)[l][None] for l in range(depth)], rows(W[k]), rows(M[k]), rows(V[k]), name=f"adamw_{k}")
        out[k] = [a.reshape(shp) for a in res]

    dmod_all = parts.reshape(NDEV, -1)[:, :depth * 6 * d].reshape(NDEV, depth, 6 * d)
    dmod_cols = lax.dynamic_slice_in_dim(dmod_all, me * ncol, ncol, axis=2)
    g_ada = [_mm(cond, dmod_cols[:, l], ta=True, precision=HI, name=f"ada_bwd{l}")[None] for l in range(depth)]
    out['w_ada'] = _adamw(g_ada, w_ada, m_w_ada, v_w_ada, name="adamw_w_ada")

    return (loss, dx[None], *[view(k, out[k][i]) for i in range(4) for k in WEIGHTS])
```

```python
import functools
import math

import jax
import jax.numpy as jnp
from jax import lax
from jax.experimental import pallas as pl
from jax.experimental.pallas import tpu as pltpu

F32 = jnp.float32
BF = jnp.bfloat16
NDEV = 8
LANES = 128
SUBLANES = 8
VMEM_LIMIT = 48 * 1024 * 1024

SSM_H = 16
HEAD_DIM = 64
RMS_EPS = 1e-6
EIG_CLIP = 1e-4
ADAM_LR = 0.001
ADAM_B1 = 0.9
ADAM_B2 = 0.999
ADAM_EPS = 1e-08
ADAM_WD = 0.01
ADAM_STEP = 10
NEG = -1e30
HI = lax.Precision.HIGHEST

WEIGHTS = ['w_ada', 'b_ada', 'g_pre_mix', 'g_post_mix', 'g_pre_ffn', 'g_post_ffn', 'w_in', 'lam_re', 'lam_im',
           'log_dt', 'b_re', 'b_im', 'c_re', 'c_im', 'd_skip', 'w_glu', 'b_glu', 'b_f', 'w_pa', 'w_pb', 'w_o',
           'w_ffn_gate', 'w_ffn_up', 'w_ffn_down']
TRANSPOSED = ['w_ffn_gate', 'w_ffn_up', 'b_re', 'b_im']
COL_SHARDED = ['w_in', 'w_pa', 'w_pb']
ROW_SHARDED = ['w_glu', 'w_o', 'w_ffn_down', 'w_ffn_gate', 'w_ffn_up']
BIG = COL_SHARDED + ROW_SHARDED
SMALL = ['b_ada', 'g_pre_mix', 'g_post_mix', 'g_pre_ffn', 'g_post_ffn', 'lam_re', 'lam_im', 'log_dt', 'b_re',
         'b_im', 'c_re', 'c_im', 'd_skip', 'b_glu', 'b_f']


def _fit(dim, target, align):
    if dim <= target:
        return dim
    t = (target // align) * align
    while t >= align:
        if dim % t == 0:
            return t
        t -= align
    return dim


def _params(**kw):
    return pltpu.CompilerParams(vmem_limit_bytes=VMEM_LIMIT, **kw)


def _mm(a, b, *, ta=False, tb=False, out_dtype=F32, tm=None, tn=512, tk=2048, precision=None, name, ride=None,
        second=None):
    m, k = (a.shape[1], a.shape[0]) if ta else a.shape
    n = b.shape[0] if tb else b.shape[1]
    assert (b.shape[1] if tb else b.shape[0]) == k
    tm = _fit(m, tm or (1024 if ta else 2048), LANES if ta else 16)
    tn = _fit(n, tn, LANES)
    tk = _fit(k, tk, LANES)
    nk = k // tk
    grid = (m // tm, n // tn, nk)
    dims = (((0 if ta else 1,), (1 if tb else 0,)), ((), ()))
    ex, ex_arrays = ride if ride is not None else (None, [])

    pairs = [(a, b)] + ([second] if second is not None else [])

    def kern(*refs):
        ab_refs, (o_ref,), comm, scratch = _ride_split(ex, refs, 2 * len(pairs), 1)
        step = (pl.program_id(0) * grid[1] + pl.program_id(1)) * grid[2] + pl.program_id(2)
        if ex is not None:
            @pl.when(step == 0)
            def _():
                ex.start(*comm)

            @pl.when(step == (grid[0] * grid[1] * grid[2]) // 2)
            def _():
                ex.forward(*comm)

        p = None
        for a_ref, b_ref in zip(ab_refs[::2], ab_refs[1::2]):
            av, bv = a_ref[...], b_ref[...]
            if precision is None:
                av, bv = av.astype(BF), bv.astype(BF)
            q = lax.dot_general(av, bv, dims, preferred_element_type=F32, precision=precision)
            p = q if p is None else p + q
        if nk == 1:
            o_ref[...] = p.astype(out_dtype)
        else:
            acc_ref, = scratch
            kk = pl.program_id(2)

            @pl.when(kk == 0)
            def _():
                acc_ref[...] = p

            @pl.when(kk > 0)
            def _():
                acc_ref[...] += p

            @pl.when(kk == nk - 1)
            def _():
                o_ref[...] = acc_ref[...].astype(out_dtype)

        if ex is not None:
            @pl.when(step == grid[0] * grid[1] * grid[2] - 1)
            def _():
                ex.wait(*comm)

    a_spec = pl.BlockSpec((tk, tm), lambda i, j, kk: (kk, i)) if ta else pl.BlockSpec((tm, tk), lambda i, j, kk: (i, kk))
    b_spec = pl.BlockSpec((tn, tk), lambda i, j, kk: (j, kk)) if tb else pl.BlockSpec((tk, tn), lambda i, j, kk: (kk, j))
    res = pl.pallas_call(
        kern, name=name,
        out_shape=[jax.ShapeDtypeStruct((m, n), out_dtype)] + (ex.out_shape if ex else []),
        grid=grid,
        in_specs=[a_spec, b_spec] * len(pairs) + (ex.specs if ex else []),
        out_specs=[pl.BlockSpec((tm, tn), lambda i, j, kk: (i, j))] + (ex.specs if ex else []),
        scratch_shapes=(ex.scratch if ex else []) + ([] if nk == 1 else [pltpu.VMEM((tm, tn), F32)]),
        compiler_params=_params(dimension_semantics=("arbitrary",) * 3 if ex else ("parallel", "parallel", "arbitrary"),
                                has_side_effects=ex is not None),
    )(*[x for pair in pairs for x in pair], *ex_arrays)
    return (res[0], res[1:]) if ex else res[0]


def _mm_fused(a, bs, extras, fn, out_dtypes, *, rows=(), tb=False, tm=2048, tn=256, name):
    m, k = a.shape
    n = bs[0].shape[0] if tb else bs[0].shape[1]
    tm = _fit(m, tm, 16)
    tn = _fit(n, tn, LANES)
    extras = [e if isinstance(e, tuple) else (e, 0) for e in extras]
    nb, ne, nr = len(bs), len(extras), len(rows)
    dims = (((1,), (1 if tb else 0,)), ((), ()))

    def kern(*refs):
        av = refs[0][...].astype(BF)
        prods = [lax.dot_general(av, r[...].astype(BF), dims, preferred_element_type=F32) for r in refs[1:1 + nb]]
        res = fn(*prods, *[r[...] for r in refs[1 + nb:1 + nb + ne + nr]])
        for o_ref, r, dt in zip(refs[1 + nb + ne + nr:], res, out_dtypes):
            o_ref[...] = r.astype(dt)

    tile = pl.BlockSpec((tm, tn), lambda i, j: (i, j))
    b_spec = pl.BlockSpec((tn, k), lambda i, j: (j, 0)) if tb else pl.BlockSpec((k, tn), lambda i, j: (0, j))
    return pl.pallas_call(
        kern, name=name, out_shape=[jax.ShapeDtypeStruct((m, n), dt) for dt in out_dtypes],
        grid=(m // tm, n // tn),
        in_specs=[pl.BlockSpec((tm, k), lambda i, j: (i, 0))] + [b_spec] * nb
        + [pl.BlockSpec((tm, tn), lambda i, j, c=c: (i, j + c * (n // tn))) for _, c in extras]
        + [pl.BlockSpec((1, tn), lambda i, j: (0, j))] * nr,
        out_specs=[tile] * len(out_dtypes),
        compiler_params=_params(dimension_semantics=("parallel", "parallel")),
    )(a, *bs, *[e for e, _ in extras], *rows)


def _rowwise(fn, tiles, params, outs, *, tr=256, name):
    tiles = [t if isinstance(t, tuple) else (t, t.shape[1], 0) for t in tiles]
    s = tiles[0][0].shape[0]
    tr = _fit(s, tr, 16)
    nt, npar = len(tiles), len(params)

    def kern(*refs):
        i = pl.program_id(0)
        res = fn(*[r[...] for r in refs[:nt + npar]])
        if not isinstance(res, (tuple, list)):
            res = (res,)
        for (w, dt, kind), o_ref, r in zip(outs, refs[nt + npar:], res):
            if kind == 'tile':
                o_ref[...] = r.astype(dt)
            else:
                part = jnp.sum(r.astype(F32), axis=0, keepdims=True)

                @pl.when(i == 0)
                def _(o_ref=o_ref, part=part):
                    o_ref[...] = part

                @pl.when(i > 0)
                def _(o_ref=o_ref, part=part):
                    o_ref[...] += part

    def tile_spec(w, cb):
        return pl.BlockSpec((tr, w), lambda i: (i, cb))

    in_specs = [tile_spec(w, cb) for _, w, cb in tiles]
    in_specs += [pl.BlockSpec(p.shape, lambda i, nd=p.ndim: (0,) * nd) for p in params]
    out_shape, out_specs = [], []
    for w, dt, kind in outs:
        if kind == 'tile':
            out_shape.append(jax.ShapeDtypeStruct((s, w), dt))
            out_specs.append(pl.BlockSpec((tr, w), lambda i: (i, 0)))
        else:
            out_shape.append(jax.ShapeDtypeStruct((1, w), F32))
            out_specs.append(pl.BlockSpec((1, w), lambda i: (0, 0)))
    res = pl.pallas_call(
        kern, name=name, out_shape=out_shape, grid=(s // tr,), in_specs=in_specs, out_specs=out_specs,
        compiler_params=_params(dimension_semantics=("arbitrary",)),
    )(*[t[0] for t in tiles], *params)
    return res


def _sigmoid(z):
    return 1.0 / (1.0 + jnp.exp(-z))


def _silu(z):
    return z * _sigmoid(z)


_GELU_K = math.sqrt(2.0 / math.pi)


def _gelu(y):
    return 0.5 * y * (1.0 + jnp.tanh(_GELU_K * (y + 0.044715 * y * y * y)))


def _gelu_grad(y):
    th = jnp.tanh(_GELU_K * (y + 0.044715 * y * y * y))
    return 0.5 * (1.0 + th) + 0.5 * y * (1.0 - th * th) * _GELU_K * (1.0 + 3.0 * 0.044715 * y * y)


def _rms(x):
    return lax.rsqrt(jnp.mean(x * x, axis=-1, keepdims=True) + RMS_EPS)


def _norm_bwd(dn, xhat, r):
    return r * (dn - xhat * jnp.mean(dn * xhat, axis=-1, keepdims=True))


def _cum_fwd(flog, bf_row, nh, *, name):
    s = flog.shape[0]
    w = nh * HEAD_DIM
    t = _fit(s, 256, SUBLANES)

    def kern(f_ref, b_ref, o_ref, carry_ref):
        i = pl.program_id(0)

        @pl.when(i == 0)
        def _():
            carry_ref[...] = jnp.zeros_like(carry_ref)

        z = f_ref[...] + b_ref[...]
        logf = jnp.minimum(z, 0.0) - jnp.log(1.0 + jnp.exp(-jnp.abs(z)))
        hh = lax.broadcasted_iota(jnp.int32, (LANES, w), 0)
        cc = lax.broadcasted_iota(jnp.int32, (LANES, w), 1)
        expand = (cc // HEAD_DIM == hh).astype(F32)
        lx = jnp.dot(logf, expand, preferred_element_type=F32, precision=HI)
        rr = lax.broadcasted_iota(jnp.int32, (t, t), 0)
        kk = lax.broadcasted_iota(jnp.int32, (t, t), 1)
        tri = (kk <= rr).astype(F32)
        cum = jnp.dot(tri, lx, preferred_element_type=F32, precision=HI) + carry_ref[...]
        o_ref[...] = cum
        carry_ref[...] = cum[t - 1:t, :]

    return pl.pallas_call(
        kern, name=name, out_shape=jax.ShapeDtypeStruct((s, w), F32), grid=(s // t,),
        in_specs=[pl.BlockSpec((t, LANES), lambda i: (i, 0)), pl.BlockSpec((1, LANES), lambda i: (0, 0))],
        out_specs=pl.BlockSpec((t, w), lambda i: (i, 0)),
        scratch_shapes=[pltpu.VMEM((1, w), F32)],
        compiler_params=_params(dimension_semantics=("arbitrary",)),
    )(flog, bf_row)


def _cum_bwd(dcrow, flog, bf_col, *, name):
    _, nh, s = dcrow.shape
    t = _fit(s, 512, LANES)
    nb = s // t

    def kern(d_ref, f_ref, b_ref, df_ref, db_ref):
        rr = lax.broadcasted_iota(jnp.int32, (t, t), 0)
        kk = lax.broadcasted_iota(jnp.int32, (t, t), 1)
        upper = (rr >= kk).astype(F32)
        pick = (lax.broadcasted_iota(jnp.int32, (nh, LANES), 0)
                == lax.broadcasted_iota(jnp.int32, (nh, LANES), 1)).astype(F32)
        carry = jnp.zeros((nh, 1), F32)
        db = jnp.zeros((nh, 1), F32)
        for blk in range(nb - 1, -1, -1):
            sl = slice(blk * t, (blk + 1) * t)
            rc = jnp.dot(d_ref[0, :, sl] + d_ref[1, :, sl], upper, preferred_element_type=F32, precision=HI) + carry
            carry = rc[:, 0:1]
            frow = lax.dot_general(pick, f_ref[sl, :], (((1,), (1,)), ((), ())), preferred_element_type=F32,
                                   precision=HI)
            df = rc * _sigmoid(-(frow + b_ref[...]))
            df_ref[sl, :] = lax.dot_general(df, pick, (((0,), (0,)), ((), ())), preferred_element_type=F32,
                                            precision=HI).astype(BF)
            db = db + jnp.sum(df, axis=1, keepdims=True)
        db_ref[...] = db

    return pl.pallas_call(
        kern, name=name,
        out_shape=[jax.ShapeDtypeStruct((s, LANES), BF), jax.ShapeDtypeStruct((nh, 1), F32)],
        compiler_params=_params(),
    )(dcrow, flog, bf_col)


def _ride_split(ex, refs, n_in, n_out):
    n = ex.n if ex is not None else 0
    own_in, srcs = refs[:n_in], refs[n_in:n_in + n]
    own_out, dsts = refs[n_in + n:n_in + n + n_out], refs[n_in + n + n_out:n_in + 2 * n + n_out]
    sems = refs[n_in + 2 * n + n_out:n_in + 2 * n + n_out + 3] if n else ()
    rest = refs[n_in + 2 * n + n_out + (3 if n else 0):]
    return own_in, own_out, (srcs, dsts, sems), rest


ATTN_STRIP = 32
BIAS_LANES = 3


def _head_masks(rows):
    lane = lax.broadcasted_iota(jnp.int32, (rows, LANES), 1)
    return [(lane >= HEAD_DIM * e) & (lane < HEAD_DIM * (e + 1)) for e in range(2)]


def _augment(feat, bias, e, *, bias_slot, ones_slot):
    rows = feat.shape[0]
    lane = lax.broadcasted_iota(jnp.int32, (rows, LANES), 1)
    own = (lane >= HEAD_DIM * e) & (lane < HEAD_DIM * (e + 1))
    off = lane - HEAD_DIM * (1 - e)
    out = jnp.where(own, feat, 0.0)
    if ones_slot is not None:
        out = jnp.where((off >= ones_slot * BIAS_LANES) & (off < (ones_slot + 1) * BIAS_LANES), 1.0, out)
    if bias is not None:
        rest = pltpu.roll(bias, HEAD_DIM, 1)
        for term in range(BIAS_LANES):
            part = rest.astype(BF).astype(F32)
            out = jnp.where(off == bias_slot * BIAS_LANES + term, part, out)
            rest = rest - part
    return out.astype(BF)


def _two_slot_pipeline(m, scores, tile):
    scores(0, 0)

    def pair(n, carry):
        k = 2 * n
        scores(k + 1, 1)
        tile(k, 0, False)
        scores(k + 2, 0)
        tile(k + 1, 1, False)
        return carry

    lax.fori_loop(0, m // 2, pair, 0)

    @pl.when(m % 2 == 0)
    def _():
        tile(m, 0, True)

    @pl.when(m % 2 == 1)
    def _():
        scores(m, 1)
        tile(m - 1, 0, False)
        tile(m, 1, True)


def _attn_fwd(proj, cumx, qcol, *, name, ride=None):
    s = proj.shape[0]
    w = cumx.shape[1]
    nhp = w // LANES
    t = _fit(s, 256, LANES)
    nq = s // t
    strip = _fit(t, ATTN_STRIP, 16)
    scale = HEAD_DIM ** -0.5
    qb, kb, vb = qcol // LANES, (qcol + w) // LANES, (qcol + 2 * w) // LANES
    ex, ex_arrays = ride if ride is not None else (None, [])
    nt_dims = (((1,), (1,)), ((), ()))

    def kern(*refs):
        own_in, (o_ref, l_ref), comm, scratch = _ride_split(ex, refs, 5, 2)
        q_ref, k_ref, v_ref, cxq_ref, cxk_ref = own_in
        ka_ref, vat_ref, s0_ref, s1_ref, p_ref, m_ref, acc_ref = scratch
        s_refs = (s0_ref, s1_ref)
        i = pl.program_id(1)
        if ex is not None:
            @pl.when((pl.program_id(0) == 0) & (i == 0))
            def _():
                ex.start(*comm)

            @pl.when((pl.program_id(0) == nhp - 1) & (i == 0))
            def _():
                ex.forward(*comm)

        msks = _head_masks(t)

        @pl.when(i == 0)
        def _():
            def build(c, carry):
                rows = pl.ds(pl.multiple_of(c * t, LANES), t)
                k2, v2, cx = k_ref[rows, :], v_ref[rows, :], cxk_ref[rows, :]
                for e in range(2):
                    ka_ref[e, rows, :] = _augment(k2, -cx, e, bias_slot=1, ones_slot=0)
                    vat_ref[e, :, rows] = jnp.where(msks[e], v2, 1.0).T.astype(BF)
                return carry
            lax.fori_loop(0, nq, build, 0)

        q2 = q_ref[...] * scale
        qa = [_augment(q2, cxq_ref[...], e, bias_slot=0, ones_slot=1) for e in range(2)]
        m_ref[...] = jnp.full(m_ref.shape, NEG, F32)
        acc_ref[...] = jnp.zeros(acc_ref.shape, F32)
        slabs = strip // SUBLANES

        def scores(j, slot):
            rows_k = pl.ds(pl.multiple_of(j * t, LANES), t)
            for e in range(2):
                st = lax.dot_general(ka_ref[e, rows_k, :], qa[e], nt_dims, preferred_element_type=F32)
                s_refs[slot][e] = st.reshape(t // SUBLANES, SUBLANES, t)

        def tile(j, slot, diagonal):
            rows_k = pl.ds(pl.multiple_of(j * t, LANES), t)
            s_ref = s_refs[slot]
            for e in range(2):
                mx = jnp.full((SUBLANES, t), NEG, F32)
                for r in range(t // strip):
                    sl = slice(r * slabs, (r + 1) * slabs)
                    sv = s_ref[e,sl]
                    if diagonal:
                        shape = (slabs, SUBLANES, t)
                        key = (r * strip + lax.broadcasted_iota(jnp.int32, shape, 0) * SUBLANES
                               + lax.broadcasted_iota(jnp.int32, shape, 1))
                        sv = jnp.where(key <= lax.broadcasted_iota(jnp.int32, shape, 2), sv, NEG)
                        s_ref[e,sl] = sv
                    mx = jnp.maximum(mx, jnp.max(sv, axis=0))
                for sh in (4, 2, 1):
                    mx = jnp.maximum(mx, pltpu.roll(mx, sh, 0))
                m_old = m_ref[e]
                m_new = jnp.maximum(m_old, mx)
                alpha = jnp.exp(m_old - m_new)
                m_ref[e] = m_new
                for r in range(t // strip):
                    p = jnp.exp(s_ref[e,r * slabs:(r + 1) * slabs] - m_new[None])
                    p_ref[e, r * strip:(r + 1) * strip, :] = p.reshape(strip, t).astype(BF)
                acc = acc_ref[e].reshape(LANES // SUBLANES, SUBLANES, t) * alpha[None]
                acc_ref[e] = acc.reshape(LANES, t) + jnp.dot(vat_ref[e, :, rows_k], p_ref[e],
                                                             preferred_element_type=F32)

        _two_slot_pipeline(i, scores, tile)

        outs, lses = [], []
        for e in range(2):
            acc = acc_ref[e]
            other = HEAD_DIM * (1 - e)
            den = acc[other:other + 1, :]
            outs.append(acc / den)
            lses.append(jnp.broadcast_to(m_ref[e][0:1, :] + jnp.log(den), (LANES, t)))
        upper = lax.broadcasted_iota(jnp.int32, (LANES, t), 0) < HEAD_DIM
        o_ref[...] = jnp.where(upper, outs[0], outs[1]).T.astype(BF)
        l_ref[...] = jnp.where(upper, lses[0], lses[1]).T
        if ex is not None:
            @pl.when((pl.program_id(0) == nhp - 1) & (i == nq - 1))
            def _():
                ex.wait(*comm)

    own_scratch = [pltpu.VMEM((2, s, LANES), BF), pltpu.VMEM((2, LANES, s), BF),
                   pltpu.VMEM((2, t // SUBLANES, SUBLANES, t), F32),
                   pltpu.VMEM((2, t // SUBLANES, SUBLANES, t), F32), pltpu.VMEM((2, t, t), BF),
                   pltpu.VMEM((2, SUBLANES, t), F32), pltpu.VMEM((2, LANES, t), F32)]
    return pl.pallas_call(
        kern, name=name,
        out_shape=[jax.ShapeDtypeStruct((s, w), BF), jax.ShapeDtypeStruct((nhp, s, LANES), F32)]
        + (ex.out_shape if ex else []),
        grid=(nhp, nq),
        in_specs=[pl.BlockSpec((t, LANES), lambda h, i: (i, qb + h)),
                  pl.BlockSpec((s, LANES), lambda h, i: (0, kb + h)),
                  pl.BlockSpec((s, LANES), lambda h, i: (0, vb + h)),
                  pl.BlockSpec((t, LANES), lambda h, i: (i, h)),
                  pl.BlockSpec((s, LANES), lambda h, i: (0, h))] + (ex.specs if ex else []),
        out_specs=[pl.BlockSpec((t, LANES), lambda h, i: (i, h)),
                   pl.BlockSpec((None, t, LANES), lambda h, i: (h, i, 0))] + (ex.specs if ex else []),
        scratch_shapes=(ex.scratch if ex else []) + own_scratch,
        compiler_params=_params(dimension_semantics=("arbitrary", "arbitrary"),
                                has_side_effects=ex is not None),
    )(proj, proj, proj, cumx, cumx, *ex_arrays)


def _attn_bwd(proj, do, o, lse, cumx, qcol, *, name, ride=None):
    s = proj.shape[0]
    w = cumx.shape[1]
    nhp = w // LANES
    t = _fit(s, 256, LANES)
    nq = s // t
    strip = _fit(t, ATTN_STRIP, 16)
    scale = HEAD_DIM ** -0.5
    qb, kb, vb = qcol // LANES, (qcol + w) // LANES, (qcol + 2 * w) // LANES
    tn_dims = (((0,), (0,)), ((), ()))
    nt_dims = (((1,), (1,)), ((), ()))
    ex, ex_arrays = ride if ride is not None else (None, [])

    def kern(*refs):
        own_in, own_out, comm, scratch = _ride_split(ex, refs, 7, 5)
        q_ref, k_ref, v_ref, do_ref, o_ref, l_ref, cx_ref = own_in
        dq_ref, dk_ref, dv_ref, dkc_ref, dqc_ref = own_out
        qa_ref, da_ref, dqa_ref, dka_ref, dva_ref, st0_ref, st1_ref, dpt0_ref, dpt1_ref, pt_ref, dst_ref = scratch
        st_refs, dpt_refs = (st0_ref, st1_ref), (dpt0_ref, dpt1_ref)
        j = pl.program_id(1)
        if ex is not None:
            @pl.when((pl.program_id(0) == 0) & (j == 0))
            def _():
                ex.start(*comm)

        msks = _head_masks(t)

        @pl.when(j == 0)
        def _():
            def build(c, carry):
                rows = pl.ds(pl.multiple_of(c * t, LANES), t)
                q2 = q_ref[rows, :] * scale
                do2 = do_ref[rows, :]
                dd = do2 * o_ref[rows, :].astype(F32)
                delta = jnp.where(msks[0], jnp.sum(jnp.where(msks[0], dd, 0.0), axis=1, keepdims=True),
                                  jnp.sum(jnp.where(msks[1], dd, 0.0), axis=1, keepdims=True))
                bias = cx_ref[rows, :] - l_ref[rows, :]
                for e in range(2):
                    qa_ref[e, rows, :] = _augment(q2, bias, e, bias_slot=0, ones_slot=1)
                    da_ref[e, rows, :] = _augment(do2, -delta, e, bias_slot=0, ones_slot=None)
                return carry
            lax.fori_loop(0, nq, build, 0)
            dqa_ref[...] = jnp.zeros(dqa_ref.shape, F32)

        rows_k = pl.ds(pl.multiple_of(j * t, LANES), t)
        k2, v2 = k_ref[...], v_ref[...]
        ka = [_augment(k2, -cx_ref[rows_k, :], e, bias_slot=1, ones_slot=0) for e in range(2)]
        va = [_augment(v2, None, e, bias_slot=None, ones_slot=0) for e in range(2)]
        dka_ref[...] = jnp.zeros(dka_ref.shape, F32)
        dva_ref[...] = jnp.zeros(dva_ref.shape, F32)

        def scores(k, slot):
            rows_q = pl.ds(pl.multiple_of((nq - 1 - k) * t, LANES), t)
            for e in range(2):
                st_refs[slot][e] = lax.dot_general(ka[e], qa_ref[e, rows_q, :], nt_dims,
                                                   preferred_element_type=F32)
                dpt_refs[slot][e] = lax.dot_general(va[e], da_ref[e, rows_q, :], nt_dims,
                                                    preferred_element_type=F32)

        def tile(k, slot, diagonal):
            rows_q = pl.ds(pl.multiple_of((nq - 1 - k) * t, LANES), t)
            st_ref, dpt_ref = st_refs[slot], dpt_refs[slot]
            for e in range(2):
                for r in range(t // strip):
                    rows = slice(r * strip, (r + 1) * strip)
                    sv = st_ref[e, rows, :]
                    if diagonal:
                        key = r * strip + lax.broadcasted_iota(jnp.int32, (strip, t), 0)
                        qry = lax.broadcasted_iota(jnp.int32, (strip, t), 1)
                        sv = jnp.where(key <= qry, sv, NEG)
                    p = jnp.exp(sv)
                    pt_ref[e, rows, :] = p.astype(BF)
                    dst_ref[e, rows, :] = (p * dpt_ref[e, rows, :]).astype(BF)
            for e in range(2):
                dva_ref[e] += jnp.dot(pt_ref[e], da_ref[e, rows_q, :], preferred_element_type=F32)
                dka_ref[e] += jnp.dot(dst_ref[e], qa_ref[e, rows_q, :], preferred_element_type=F32)
                dqa_ref[e, rows_q, :] += lax.dot_general(dst_ref[e], ka[e], tn_dims, preferred_element_type=F32)

        _two_slot_pipeline(nq - 1 - j, scores, tile)

        dk_ref[...] = jnp.where(msks[0], dka_ref[0], dka_ref[1])
        dv_ref[...] = jnp.where(msks[0], dva_ref[0], dva_ref[1])
        sums = jnp.where(msks[1], dka_ref[0], dka_ref[1]).T
        dkc_ref[0:1, :] = sums[HEAD_DIM + BIAS_LANES:HEAD_DIM + BIAS_LANES + 1, :]
        dkc_ref[1:2, :] = sums[BIAS_LANES:BIAS_LANES + 1, :]

        @pl.when(j == nq - 1)
        def _():
            def flush(c, carry):
                rows = pl.ds(pl.multiple_of(c * t, LANES), t)
                a0, a1 = dqa_ref[0, rows, :], dqa_ref[1, rows, :]
                dq_ref[rows, :] = jnp.where(msks[0], a0, a1) * scale
                sums = jnp.where(msks[1], a0, a1).T
                dqc_ref[0:1, rows] = sums[HEAD_DIM:HEAD_DIM + 1, :]
                dqc_ref[1:2, rows] = sums[0:1, :]
                return carry
            lax.fori_loop(0, nq, flush, 0)

        if ex is not None:
            @pl.when((pl.program_id(0) == nhp - 1) & (j == nq - 1))
            def _():
                ex.wait(*comm)

    full = lambda cb: pl.BlockSpec((s, LANES), lambda h, j: (0, cb + h))
    blk = lambda cb: pl.BlockSpec((t, LANES), lambda h, j: (j, cb + h))
    own_scratch = [pltpu.VMEM((2, s, LANES), BF), pltpu.VMEM((2, s, LANES), BF), pltpu.VMEM((2, s, LANES), F32),
                   pltpu.VMEM((2, t, LANES), F32), pltpu.VMEM((2, t, LANES), F32),
                   pltpu.VMEM((2, t, t), F32), pltpu.VMEM((2, t, t), F32),
                   pltpu.VMEM((2, t, t), F32), pltpu.VMEM((2, t, t), F32),
                   pltpu.VMEM((2, t, t), BF), pltpu.VMEM((2, t, t), BF)]
    return pl.pallas_call(
        kern, name=name,
        out_shape=[jax.ShapeDtypeStruct((s, w), F32)] * 3 + [jax.ShapeDtypeStruct((nhp, 2, s), F32)] * 2
        + (ex.out_shape if ex else []),
        grid=(nhp, nq),
        in_specs=[full(qb), blk(kb), blk(vb), full(0), full(0),
                  pl.BlockSpec((None, s, LANES), lambda h, j: (h, 0, 0)), full(0)] + (ex.specs if ex else []),
        out_specs=[full(0), blk(0), blk(0), pl.BlockSpec((None, 2, t), lambda h, j: (h, 0, j)),
                   pl.BlockSpec((None, 2, s), lambda h, j: (h, 0, 0))] + (ex.specs if ex else []),
        scratch_shapes=(ex.scratch if ex else []) + own_scratch,
        compiler_params=_params(dimension_semantics=("arbitrary", "arbitrary"),
                                has_side_effects=ex is not None),
    )(proj, proj, proj, do, o, lse, cumx, *ex_arrays)


S5_STATES = 256
S5_ROWS = 512


def _cmul(ar, ai, br, bi):
    return ar * br - ai * bi, ar * bi + ai * br


def _scan_tables(lr, li, reverse):
    w = lr.shape[1]
    row = lax.broadcasted_iota(jnp.int32, (SUBLANES, w), 0)
    if reverse:
        row = SUBLANES - 1 - row
    lr1, li1 = jnp.broadcast_to(lr, (SUBLANES, w)), jnp.broadcast_to(li, (SUBLANES, w))
    lr2, li2 = _cmul(lr1, li1, lr1, li1)
    lr4, li4 = _cmul(lr2, li2, lr2, li2)
    steps = []
    for d, (pr, pi) in zip((1, 2, 4), ((lr1, li1), (lr2, li2), (lr4, li4))):
        keep = row >= d
        steps.append((jnp.where(keep, pr, 0.0), jnp.where(keep, pi, 0.0)))
    cr, ci = lr1, li1
    for bit, (pr, pi) in zip((1, 2, 4), ((lr1, li1), (lr2, li2), (lr4, li4))):
        nr, ni = _cmul(cr, ci, pr, pi)
        has = (row & bit) != 0
        cr, ci = jnp.where(has, nr, cr), jnp.where(has, ni, ci)
    return steps, (cr, ci)


def _scan_local(xr, xi, steps, reverse):
    for d, (pr, pi) in zip((1, 2, 4), steps):
        sh = (SUBLANES - d) if reverse else d
        sr, si = pltpu.roll(xr, sh, 0), pltpu.roll(xi, sh, 0)
        xr, xi = xr + (pr * sr - pi * si), xi + (pr * si + pi * sr)
    return xr, xi


def _scan_carry(xr, xi, car_r, car_i, carry_pow):
    cr, ci = carry_pow
    return xr + (cr * car_r - ci * car_i), xi + (cr * car_i + ci * car_r)


SCAN_UNROLL = 4


def _s5_specs(s, ncb):
    u_spec = pl.BlockSpec((s, LANES), lambda cb, hf: (0, cb))
    wb_spec = pl.BlockSpec((None, None, LANES, S5_STATES), lambda cb, hf: (cb, hf, 0, 0))
    wc_spec = pl.BlockSpec((None, None, S5_STATES, LANES), lambda cb, hf: (cb, hf, 0, 0))
    lam_spec = pl.BlockSpec((1, S5_STATES), lambda cb, hf: (0, 2 * cb + hf))
    d_spec = pl.BlockSpec((1, LANES), lambda cb, hf: (0, cb))
    return u_spec, wb_spec, wc_spec, lam_spec, d_spec


def _s5_project_and_scan(u_ref, wbr_ref, wbi_ref, lr_ref, li_ref, xr_ref, xi_ref, s, rows):
    wbr, wbi = wbr_ref[...], wbi_ref[...]
    for r in range(s // rows):
        sl = pl.ds(r * rows, rows)
        ub = u_ref[sl, :].astype(BF)
        xr_ref[sl, :] = jnp.dot(ub, wbr, preferred_element_type=F32)
        xi_ref[sl, :] = jnp.dot(ub, wbi, preferred_element_type=F32)
    steps, cpow = _scan_tables(lr_ref[...], li_ref[...], False)

    unroll = _fit(s // SUBLANES, SCAN_UNROLL, 1)

    def body(b, carry):
        car_r, car_i = carry
        sls = [pl.ds(pl.multiple_of((b * unroll + q) * SUBLANES, SUBLANES), SUBLANES) for q in range(unroll)]
        blocks = [_scan_local(xr_ref[sl, :], xi_ref[sl, :], steps, False) for sl in sls]
        for sl, (xr, xi) in zip(sls, blocks):
            xr, xi = _scan_carry(xr, xi, car_r, car_i, cpow)
            xr_ref[sl, :] = xr
            xi_ref[sl, :] = xi
            car_r, car_i = xr[SUBLANES - 1:SUBLANES, :], xi[SUBLANES - 1:SUBLANES, :]
        return car_r, car_i

    zero = jnp.zeros((1, S5_STATES), F32)
    lax.fori_loop(0, s // SUBLANES // unroll, body, (zero, zero))


def _s5_fwd(proj, wb_re, wb_im, wc_re, wc_im, lam_re, lam_im, dskip, *, name):
    s = proj.shape[0]
    w = dskip.shape[1]
    ncb = w // LANES
    rows = _fit(s, S5_ROWS, SUBLANES)

    def kern(u_ref, wbr_ref, wbi_ref, wcr_ref, wci_ref, lr_ref, li_ref, d_ref, y_ref, xr_ref, xi_ref):
        hf = pl.program_id(1)
        _s5_project_and_scan(u_ref, wbr_ref, wbi_ref, lr_ref, li_ref, xr_ref, xi_ref, s, rows)
        wcr, wci = wcr_ref[...], wci_ref[...]
        for r in range(s // rows):
            sl = pl.ds(r * rows, rows)
            y = (jnp.dot(xr_ref[sl, :].astype(BF), wcr, preferred_element_type=F32)
                 - jnp.dot(xi_ref[sl, :].astype(BF), wci, preferred_element_type=F32))

            @pl.when(hf == 0)
            def _(y=y, sl=sl):
                y_ref[sl, :] = y + d_ref[...] * u_ref[sl, :]

            @pl.when(hf == 1)
            def _(y=y, sl=sl):
                y_ref[sl, :] += y

    u_spec, wb_spec, wc_spec, lam_spec, d_spec = _s5_specs(s, ncb)
    return pl.pallas_call(
        kern, name=name, out_shape=jax.ShapeDtypeStruct((s, w), F32), grid=(ncb, 2),
        in_specs=[u_spec, wb_spec, wb_spec, wc_spec, wc_spec, lam_spec, lam_spec, d_spec],
        out_specs=u_spec,
        scratch_shapes=[pltpu.VMEM((s, S5_STATES), F32), pltpu.VMEM((s, S5_STATES), F32)],
        compiler_params=_params(dimension_semantics=("parallel", "arbitrary")),
    )(proj, wb_re, wb_im, wc_re, wc_im, lam_re, lam_im, dskip)


def _s5_bwd(proj, dy, wb_re, wb_im, wc_re, wc_im, lam_re, lam_im, dskip, *, name, ride=None):
    s = proj.shape[0]
    w = dskip.shape[1]
    ncb = w // LANES
    rows = _fit(s, S5_ROWS, SUBLANES)
    tn_dims = (((0,), (0,)), ((), ()))
    nt_dims = (((1,), (1,)), ((), ()))
    ex, ex_arrays = ride if ride is not None else (None, [])

    def kern(*refs):
        own_in, own_out, comm, scratch = _ride_split(ex, refs, 9, 8)
        u_ref, dy_ref, wbr_ref, wbi_ref, wcr_ref, wci_ref, lr_ref, li_ref, d_ref = own_in
        du_ref, dwbr_ref, dwbi_ref, dwcr_ref, dwci_ref, dlr_ref, dli_ref, dd_ref = own_out
        xr_ref, xi_ref, gr_ref, gi_ref = scratch
        hf = pl.program_id(1)
        if ex is not None:
            @pl.when((pl.program_id(0) == 0) & (hf == 0))
            def _():
                ex.start(*comm)

        _s5_project_and_scan(u_ref, wbr_ref, wbi_ref, lr_ref, li_ref, xr_ref, xi_ref, s, rows)

        wcr, wci = wcr_ref[...], wci_ref[...]
        dwcr = jnp.zeros((S5_STATES, LANES), F32)
        dwci = jnp.zeros((S5_STATES, LANES), F32)
        ddsk = jnp.zeros((1, LANES), F32)
        for r in range(s // rows):
            sl = pl.ds(r * rows, rows)
            dyf = dy_ref[sl, :]
            dyb = dyf.astype(BF)
            gr_ref[sl, :] = lax.dot_general(dyb, wcr, nt_dims, preferred_element_type=F32)
            gi_ref[sl, :] = -lax.dot_general(dyb, wci, nt_dims, preferred_element_type=F32)
            dwcr = dwcr + lax.dot_general(xr_ref[sl, :].astype(BF), dyb, tn_dims, preferred_element_type=F32)
            dwci = dwci - lax.dot_general(xi_ref[sl, :].astype(BF), dyb, tn_dims, preferred_element_type=F32)
            ddsk = ddsk + jnp.sum(dyf * u_ref[sl, :], axis=0, keepdims=True)
        dwcr_ref[...] = dwcr
        dwci_ref[...] = dwci

        @pl.when(hf == 0)
        def _():
            dd_ref[...] = ddsk

        steps, cpow = _scan_tables(lr_ref[...], -li_ref[...], True)
        row = lax.broadcasted_iota(jnp.int32, (SUBLANES, S5_STATES), 0)
        nblk = s // SUBLANES

        unroll = _fit(nblk, SCAN_UNROLL, 1)

        def body(k, carry):
            car_r, car_i, ar, ai = carry
            sls = [pl.ds(pl.multiple_of((nblk - 1 - k * unroll - q) * SUBLANES, SUBLANES), SUBLANES)
                   for q in range(unroll)]
            blocks = [_scan_local(gr_ref[sl, :], gi_ref[sl, :], steps, True) for sl in sls]
            for sl, (g_r, g_i) in zip(sls, blocks):
                g_r, g_i = _scan_carry(g_r, g_i, car_r, car_i, cpow)
                gr_ref[sl, :] = g_r
                gi_ref[sl, :] = g_i
                nr = jnp.where(row == SUBLANES - 1, car_r, pltpu.roll(g_r, SUBLANES - 1, 0))
                ni = jnp.where(row == SUBLANES - 1, car_i, pltpu.roll(g_i, SUBLANES - 1, 0))
                xr, xi = xr_ref[sl, :], xi_ref[sl, :]
                ar = ar + (xr * nr + xi * ni)
                ai = ai + (xr * ni - xi * nr)
                car_r, car_i = g_r[0:1, :], g_i[0:1, :]
            return car_r, car_i, ar, ai

        zero = jnp.zeros((1, S5_STATES), F32)
        zacc = jnp.zeros((SUBLANES, S5_STATES), F32)
        _, _, ar, ai = lax.fori_loop(0, nblk // unroll, body, (zero, zero, zacc, zacc))
        dlr_ref[...] = jnp.sum(ar, axis=0, keepdims=True)
        dli_ref[...] = jnp.sum(ai, axis=0, keepdims=True)

        wbr, wbi = wbr_ref[...], wbi_ref[...]
        dwbr = jnp.zeros((LANES, S5_STATES), F32)
        dwbi = jnp.zeros((LANES, S5_STATES), F32)
        for r in range(s // rows):
            sl = pl.ds(r * rows, rows)
            grb, gib = gr_ref[sl, :].astype(BF), gi_ref[sl, :].astype(BF)
            ub = u_ref[sl, :].astype(BF)
            dwbr = dwbr + lax.dot_general(ub, grb, tn_dims, preferred_element_type=F32)
            dwbi = dwbi + lax.dot_general(ub, gib, tn_dims, preferred_element_type=F32)
            du = (lax.dot_general(grb, wbr, nt_dims, preferred_element_type=F32)
                  + lax.dot_general(gib, wbi, nt_dims, preferred_element_type=F32))

            @pl.when(hf == 0)
            def _(du=du, sl=sl):
                du_ref[sl, :] = du + d_ref[...] * dy_ref[sl, :]

            @pl.when(hf == 1)
            def _(du=du, sl=sl):
                du_ref[sl, :] += du
        dwbr_ref[...] = dwbr
        dwbi_ref[...] = dwbi
        if ex is not None:
            @pl.when((pl.program_id(0) == ncb - 1) & (hf == 1))
            def _():
                ex.wait(*comm)

    u_spec, wb_spec, wc_spec, lam_spec, d_spec = _s5_specs(s, ncb)
    dwb_spec = pl.BlockSpec((None, None, LANES, S5_STATES), lambda cb, hf: (cb, hf, 0, 0))
    dwc_spec = pl.BlockSpec((None, None, S5_STATES, LANES), lambda cb, hf: (cb, hf, 0, 0))
    state = pltpu.VMEM((s, S5_STATES), F32)
    return pl.pallas_call(
        kern, name=name,
        out_shape=[jax.ShapeDtypeStruct((s, w), F32),
                   jax.ShapeDtypeStruct((ncb, 2, LANES, S5_STATES), F32), jax.ShapeDtypeStruct((ncb, 2, LANES, S5_STATES), F32),
                   jax.ShapeDtypeStruct((ncb, 2, S5_STATES, LANES), F32), jax.ShapeDtypeStruct((ncb, 2, S5_STATES, LANES), F32),
                   jax.ShapeDtypeStruct((1, 4 * w), F32), jax.ShapeDtypeStruct((1, 4 * w), F32),
                   jax.ShapeDtypeStruct((1, w), F32)] + (ex.out_shape if ex else []),
        grid=(ncb, 2),
        in_specs=[u_spec, u_spec, wb_spec, wb_spec, wc_spec, wc_spec, lam_spec, lam_spec, d_spec]
        + (ex.specs if ex else []),
        out_specs=[u_spec, dwb_spec, dwb_spec, dwc_spec, dwc_spec, lam_spec, lam_spec, d_spec]
        + (ex.specs if ex else []),
        scratch_shapes=(ex.scratch if ex else []) + [state, state, state, state],
        compiler_params=_params(dimension_semantics=("arbitrary", "arbitrary"), has_side_effects=ex is not None),
    )(proj, dy, wb_re, wb_im, wc_re, wc_im, lam_re, lam_im, dskip, *ex_arrays)


def _s5_discretise(lam_re, lam_im, log_dt, b_re, b_im):
    lr = jnp.minimum(lam_re, -EIG_CLIP)
    li = lam_im
    dt = jnp.exp(log_dt)[:, None]
    mag = jnp.exp(lr * dt)
    lbr, lbi = mag * jnp.cos(li * dt), mag * jnp.sin(li * dt)
    den = lr * lr + li * li
    fr = ((lbr - 1.0) * lr + lbi * li) / den
    fi = (lbi * lr - (lbr - 1.0) * li) / den
    bbr = fr[:, None, :] * b_re - fi[:, None, :] * b_im
    bbi = fr[:, None, :] * b_im + fi[:, None, :] * b_re
    return lbr, lbi, bbr, bbi


def _s5_operand(mats, channels_first):
    g, a, b = mats.shape
    gl = LANES // 2 // SSM_H
    ncb = g // (2 * gl)
    m = mats.reshape(ncb, 2, gl, a, b)
    eye = jnp.eye(gl, dtype=mats.dtype)
    inner = (m[:, :, :, :, None, :] * eye[None, None, :, None, :, None]).reshape(ncb, 2, gl * a, gl * b)
    zeros = jnp.zeros_like(inner[:, 0])
    axis = 1 if channels_first else 2
    return jnp.stack([jnp.concatenate([inner[:, 0], zeros], axis=axis),
                      jnp.concatenate([zeros, inner[:, 1]], axis=axis)], axis=1)


def _s5_block_grads(dwb, a, b, transpose):
    ncb = dwb.shape[0]
    gl = LANES // 2 // (a if not transpose else b)
    if not transpose:
        d = dwb.reshape(ncb, 2, 2, gl, a, gl, b)
        parts = [[d[:, hf, hf, g, :, g, :] for g in range(gl)] for hf in range(2)]
    else:
        d = dwb.reshape(ncb, 2, gl, a, 2, gl, b)
        parts = [[d[:, hf, g, :, hf, g, :] for g in range(gl)] for hf in range(2)]
    st = jnp.stack([jnp.stack(p, axis=1) for p in parts], axis=1)
    return st.reshape(ncb * 2 * gl, a, b)


def _adamw(parts, w, m, v, *, name):
    depth, r, c = w.shape
    assert len(parts) == depth
    npart = parts[0].shape[0]
    row_bytes = 4 * (-(-c // LANES) * LANES)
    align = 16 if parts[0].dtype == BF else SUBLANES
    budget = VMEM_LIMIT // 2 // (2 * (depth * npart + 7) * row_bytes)
    tr = _fit(r, max(align, budget // align * align), align)
    nr = r // tr
    c1 = 1.0 / (1.0 - ADAM_B1 ** ADAM_STEP)
    c2 = 1.0 / (1.0 - ADAM_B2 ** ADAM_STEP)

    def kern(*refs):
        p_refs = refs[:depth]
        w_ref, m_ref, v_ref, g_ref, d_ref, nm_ref, nv_ref = refs[depth:]
        layer = pl.program_id(0)
        for l in range(depth):
            @pl.when(layer == l)
            def _(p_ref=p_refs[l]):
                g = p_ref[0].astype(F32)
                for q in range(1, npart):
                    g = g + p_ref[q].astype(F32)
                m2 = ADAM_B1 * m_ref[...] + (1.0 - ADAM_B1) * g
                v2 = ADAM_B2 * v_ref[...] + (1.0 - ADAM_B2) * (g * g)
                upd = (m2 * c1) / (jnp.sqrt(v2 * c2) + ADAM_EPS) + ADAM_WD * w_ref[...]
                g_ref[...] = g
                d_ref[...] = -ADAM_LR * upd
                nm_ref[...] = m2
                nv_ref[...] = v2

    def part_spec(l):
        return pl.BlockSpec((npart, tr, c),
                            lambda ly, i: (0, jnp.where(ly == l, i, jnp.where(ly < l, 0, nr - 1)), 0))

    spec = pl.BlockSpec((None, tr, c), lambda ly, i: (ly, i, 0))
    return pl.pallas_call(
        kern, name=name, out_shape=[jax.ShapeDtypeStruct((depth, r, c), F32)] * 4, grid=(depth, nr),
        in_specs=[part_spec(l) for l in range(depth)] + [spec, spec, spec],
        out_specs=[spec] * 4,
        compiler_params=_params(dimension_semantics=("arbitrary", "arbitrary")),
    )(*parts, w, m, v)


def _sum_parts(parts, *, name):
    npart, r, c = parts.shape

    def kern(p_ref, o_ref):
        g = p_ref[0]
        for q in range(1, npart):
            g = g + p_ref[q]
        o_ref[...] = g

    return pl.pallas_call(kern, name=name, out_shape=jax.ShapeDtypeStruct((r, c), F32), compiler_params=_params())(parts)


class _Exchange:
    def __init__(self, arrays, gather):
        self.n = len(arrays)
        self.gather = gather
        self.out_shape = [jax.ShapeDtypeStruct(((NDEV,) + a.shape) if gather else a.shape, a.dtype) for a in arrays]
        self.scratch = [pltpu.SemaphoreType.DMA((self.n, NDEV - 1)), pltpu.SemaphoreType.DMA((self.n, NDEV - 1)),
                        pltpu.SemaphoreType.DMA((self.n,))]
        self.specs = [pl.BlockSpec(memory_space=pl.ANY)] * self.n

    def _copies(self, srcs, dsts, sems):
        send_sems, recv_sems, local_sems = sems
        x, y, c = lax.axis_index("x"), lax.axis_index("y"), lax.axis_index("c")
        me = 4 * x + 2 * y + c
        local = [pltpu.make_async_copy(srcs[a] if self.gather else srcs[a].at[me], dsts[a].at[me], local_sems.at[a])
                 for a in range(self.n)]
        remote = []
        for k in (1, 2, 4, 3, 5, 6, 7):
            px, py, pc = x ^ ((k >> 2) & 1), y ^ ((k >> 1) & 1), c ^ (k & 1)
            peer = 4 * px + 2 * py + pc
            for a in range(self.n):
                src = srcs[a] if self.gather else srcs[a].at[peer]
                mk = functools.partial(
                    pltpu.make_async_remote_copy, src_ref=src,
                    send_sem=send_sems.at[a, k - 1], recv_sem=recv_sems.at[a, k - 1],
                    device_id=(px, py, pc), device_id_type=pl.DeviceIdType.MESH)
                remote.append((mk(dst_ref=dsts[a].at[me]), mk(dst_ref=dsts[a].at[peer])))
        return local, remote

    def _gather_copies(self, srcs, dsts, sems):
        send_sems, recv_sems, local_sems = sems
        x, y, c = lax.axis_index("x"), lax.axis_index("y"), lax.axis_index("c")
        block = lambda px, py, pc: 4 * px + 2 * py + pc
        me = block(x, y, c)
        chips = [(1 - x, y), (x, 1 - y), (1 - x, 1 - y)]
        local = [pltpu.make_async_copy(srcs[a], dsts[a].at[me], local_sems.at[a]) for a in range(self.n)]
        own, passed = [], []
        for a in range(self.n):
            def copy(k, blk, to, src=None, a=a):
                return pltpu.make_async_remote_copy(
                    src_ref=dsts[a].at[blk] if src is None else src, dst_ref=dsts[a].at[blk],
                    send_sem=send_sems.at[a, k], recv_sem=recv_sems.at[a, k],
                    device_id=to, device_id_type=pl.DeviceIdType.MESH)
            sib = (x, y, 1 - c)
            own.append((copy(0, me, sib, srcs[a]), copy(0, block(x, y, 1 - c), sib)))
            for j, (px, py) in enumerate(chips):
                own.append((copy(1 + j, me, (px, py, c), srcs[a]), copy(1 + j, block(px, py, c), (px, py, c))))
            for j, (px, py) in enumerate(chips):
                passed.append((copy(4 + j, block(px, py, c), sib), copy(4 + j, block(px, py, 1 - c), sib)))
        return local, own, passed

    def start(self, srcs, dsts, sems):
        if self.gather:
            local, own, _ = self._gather_copies(srcs, dsts, sems)
            for cp in local:
                cp.start()
            for send, _ in own:
                send.start()
            return
        local, remote = self._copies(srcs, dsts, sems)
        for cp in local:
            cp.start()
        for send, _ in remote:
            send.start()

    def forward(self, srcs, dsts, sems):
        if not self.gather:
            return
        _, own, passed = self._gather_copies(srcs, dsts, sems)
        for a in range(self.n):
            for j in range(3):
                own[4 * a + 1 + j][1].wait_recv()
                passed[3 * a + j][0].start()

    def wait(self, srcs, dsts, sems):
        if self.gather:
            local, own, passed = self._gather_copies(srcs, dsts, sems)
            for a in range(self.n):
                own[4 * a][1].wait_recv()
            for _, arrival in passed:
                arrival.wait_recv()
            for send, _ in own + passed:
                send.wait_send()
            for cp in local:
                cp.wait()
            return
        local, remote = self._copies(srcs, dsts, sems)
        for send, arrival in remote:
            send.wait_send()
            arrival.wait_recv()
        for cp in local:
            cp.wait()


def _exchange(arrays, gather, *, name):
    ex = _Exchange(arrays, gather)
    n = ex.n

    def kern(*refs):
        srcs, dsts, sems = refs[:n], refs[n:2 * n], refs[2 * n:]
        ex.start(srcs, dsts, sems)
        ex.forward(srcs, dsts, sems)
        ex.wait(srcs, dsts, sems)

    return pl.pallas_call(
        kern, name=name, out_shape=ex.out_shape, in_specs=ex.specs, out_specs=ex.specs, scratch_shapes=ex.scratch,
        compiler_params=pltpu.CompilerParams(has_side_effects=True),
    )(*arrays)


def _pack(arrays):
    flat = jnp.concatenate([a.reshape(-1).astype(F32) for a in arrays])
    pad = (-flat.shape[0]) % (SUBLANES * LANES)
    return jnp.pad(flat, (0, pad)).reshape(-1, LANES)


def _unpack(buf, like):
    flat = buf.reshape(-1)
    out, off = [], 0
    for a in like:
        sz = math.prod(a.shape)
        out.append(flat[off:off + sz].reshape(a.shape))
        off += sz
    return out


def _row(v):
    return v.reshape(1, -1)


def _layer_fwd(x, mod, p, l, ride=None, on_receive=None):
    s, d = x.shape
    sw = d // 2
    nh = d // LANES
    shift_m, scale_m, gate_m, shift_f, scale_f, gate_f = mod
    n = lambda tag: f"{tag}{l}"
    sv = {}

    h1, = _rowwise(lambda xv, g, sc, sh: (xv * _rms(xv) * g) * (1.0 + sc) + sh,
                   [x], [p['g_pre_mix'], scale_m, shift_m], [(d, BF, 'tile')], name=n("pre_mix"))
    proj_a = _mm(h1, p['w_in_a'], name=n("proj_a"))
    flog = _mm(h1, p['w_in_f'], name=n("proj_f"))
    gates = _mm(h1, p['w_in_g'], name=n("proj_g"))

    y_s5 = _s5_fwd(proj_a, p['wb_re'], p['wb_im'], p['wc_re'], p['wc_im'], p['lamb_re'], p['lamb_im'], p['d_skip'],
                   name=n("s5_fwd"))
    z, = _rowwise(_gelu, [y_s5], [], [(sw, BF, 'tile')], name=n("gelu"))
    tglu, ys = _mm_fused(z, [p['w_glu']], [y_s5], lambda tv, yv, b: (tv, _gelu(yv) * _sigmoid(tv + b)),
                         [F32, BF], rows=[p['b_glu']], name=n("glu_mm"))

    cumx = _cum_fwd(flog, p['b_f_row'], nh, name=n("cum_fwd"))
    ya, lse, *received = _attn_fwd(proj_a, cumx, sw, name=n("attn_fwd"), ride=ride)
    if on_receive is not None:
        on_receive(received)

    am = _mm(ys, p['w_pa'], name=n("pa_mm"))
    bm, merged = _mm_fused(ya, [p['w_pb']], [am, (gates, 0), (gates, 1)],
                           lambda b, a, ga, gb: (b, _sigmoid(ga) * a + _sigmoid(gb) * b), [F32, BF],
                           name=n("pb_mm"))
    ym = _mm(merged, p['w_o'], name=n("o_mm"))
    def post_mix_pre_ffn(xv, yv, g, gt, g2, sc, sh):
        x2v = xv + gt * (yv * _rms(yv) * g)
        return x2v, (x2v * _rms(x2v) * g2) * (1.0 + sc) + sh

    x2, h2 = _rowwise(post_mix_pre_ffn, [x, ym], [p['g_post_mix'], gate_m, p['g_pre_ffn'], scale_f, shift_f],
                      [(d, F32, 'tile'), (d, BF, 'tile')], name=n("post_mix_pre_ffn"))
    gt, up, act = _mm_fused(h2, [p['w_ffn_gate'], p['w_ffn_up']], [], lambda g, u: (g, u, _silu(g) * u),
                            [F32, F32, BF], tb=True, name=n("gate_up_mm"))
    yf = _mm(act, p['w_ffn_down'], name=n("down_mm"))
    x3, = _rowwise(lambda xv, yv, g, gt_: xv + gt_ * (yv * _rms(yv) * g),
                   [x2, yf], [p['g_post_ffn'], gate_f], [(d, F32, 'tile')], name=n("post_ffn"))

    sv.update(x=x, h1=h1, proj_a=proj_a, flog=flog, gates=gates, y_s5=y_s5, z=z, tglu=tglu, ys=ys, cumx=cumx,
              ya=ya, lse=lse, am=am, bm=bm, merged=merged, ym=ym, x2=x2, h2=h2, gt=gt, up=up,
              act=act, yf=yf)
    return x3, sv


def _layer_bwd(dx3, sv, mod, p, l, make_ride=None, on_receive=None, carried=None, defer_tail=False):
    x, x2 = sv['x'], sv['x2']
    s, d = x.shape
    sw = d // 2
    nh = d // LANES
    shift_m, scale_m, gate_m, shift_f, scale_f, gate_f = mod
    n = lambda tag: f"{tag}{l}"
    gw, gs = {}, {}

    def post_bwd(dxo, yv, g, gate):
        r = _rms(yv)
        nf = yv * r
        dn = dxo * gate * g
        return _norm_bwd(dn, nf, r), dxo * (nf * g), dxo * gate * nf

    def pre_bwd(dh, dres, xv, g, sc):
        r = _rms(xv)
        xh = xv * r
        n3 = xh * g
        dn3 = dh * (1.0 + sc)
        return dres + _norm_bwd(dn3 * g, xh, r), dh, dh * n3, dn3 * xh

    dyf, dgate_f, gs['g_post_ffn'] = _rowwise(
        post_bwd, [dx3, sv['yf']], [p['g_post_ffn'], gate_f],
        [(d, BF, 'tile'), (d, F32, 'sum'), (d, F32, 'sum')], name=n("post_ffn_bwd"))
    gw['w_ffn_down'] = _mm(sv['act'], dyf, ta=True, out_dtype=BF, tm=1408, name=n("down_bwd_w"))

    def swiglu_bwd(da, g, u):
        sg = _sigmoid(g)
        return da * u * (sg * (1.0 + g * (1.0 - sg))), da * (g * sg)

    dgt, dup = _mm_fused(dyf, [p['w_ffn_down']], [sv['gt'], sv['up']], swiglu_bwd, [BF, BF], tb=True,
                         name=n("down_bwd_x"))
    dh2 = _mm(dgt, p['w_ffn_gate'], tm=1024, second=(dup, p['w_ffn_up']), name=n("gate_up_bwd_x"))
    gw['w_ffn_gate'] = _mm(dgt, sv['h2'], ta=True, out_dtype=BF, tm=1408, name=n("gate_bwd_w"))
    gw['w_ffn_up'] = _mm(dup, sv['h2'], ta=True, out_dtype=BF, tm=1408, name=n("up_bwd_w"))
    def pre_ffn_post_mix_bwd(dh, dres, xv, yv, g, sc, g2, gate):
        dx2v, dsh, dsc, dg = pre_bwd(dh, dres, xv, g, sc)
        return (dx2v, dsh, dsc, dg) + post_bwd(dx2v, yv, g2, gate)

    dx2, dshift_f, dscale_f, gs['g_pre_ffn'], dym, dgate_m, gs['g_post_mix'] = _rowwise(
        pre_ffn_post_mix_bwd, [dh2, dx3, x2, sv['ym']], [p['g_pre_ffn'], scale_f, p['g_post_mix'], gate_m],
        [(d, F32, 'tile'), (d, F32, 'sum'), (d, F32, 'sum'), (d, F32, 'sum'),
         (d, BF, 'tile'), (d, F32, 'sum'), (d, F32, 'sum')], name=n("pre_ffn_post_mix_bwd"))
    gw['w_o'] = _mm(sv['merged'], dym, ta=True, out_dtype=BF, name=n("o_bwd_w"))

    def merge_bwd(dm, a, b, ga, gb):
        sa, sb = _sigmoid(ga), _sigmoid(gb)
        return dm * sa, dm * sb, dm * a * sa * (1.0 - sa), dm * b * sb * (1.0 - sb)

    da_, db_, dga, dgb = _mm_fused(dym, [p['w_o']], [sv['am'], sv['bm'], (sv['gates'], 0), (sv['gates'], 1)],
                                   merge_bwd, [BF] * 4, tb=True, name=n("o_bwd_x"))
    dys = _mm(da_, p['w_pa'], tb=True, name=n("pa_bwd_x"))
    gw['w_pa'] = _mm(sv['ys'], da_, ta=True, out_dtype=BF, name=n("pa_bwd_w"))
    dya = _mm(db_, p['w_pb'], tb=True, name=n("pb_bwd_x"))
    gw['w_pb'] = _mm(sv['ya'], db_, ta=True, out_dtype=BF, name=n("pb_bwd_w"))

    sent = list(gw)
    dq, dk, dv, dkc, dqc, *received = _attn_bwd(
        sv['proj_a'], dya, sv['ya'], sv['lse'], sv['cumx'], sw, name=n("attn_bwd"),
        ride=make_ride({k: gw[k] for k in sent}) if make_ride is not None else None)
    if on_receive is not None:
        on_receive(sent, received)
    dcum = jnp.stack([-dkc.reshape(nh, s), dqc.reshape(nh, s)])
    dflog, dbf = _cum_bwd(dcum, sv['flog'], p['b_f_col'], name=n("cum_bwd"))
    gs['b_f'] = dbf.reshape(nh)

    def glu_bwd(dy_, yv, tv, b):
        zv = _gelu(yv)
        sg = _sigmoid(tv + b)
        dt = dy_ * zv * sg * (1.0 - sg)
        return dt, dy_ * sg, dt

    dt, dz1, gs['b_glu'] = _rowwise(glu_bwd, [dys, sv['y_s5'], sv['tglu']], [p['b_glu']],
                                    [(sw, BF, 'tile'), (sw, F32, 'tile'), (sw, F32, 'sum')], name=n("glu_bwd"))
    dy_s5, = _mm_fused(dt, [p['w_glu']], [dz1, sv['y_s5']], lambda dz2, a, yv: ((a + dz2) * _gelu_grad(yv),),
                       [F32], tb=True, name=n("glu_bwd_x"))
    gw['w_glu'] = _mm(sv['z'], dt, ta=True, out_dtype=BF, name=n("glu_bwd_w"))
    du, dwbr, dwbi, dwcr, dwci, dlr, dli, gs['d_skip'], *received = _s5_bwd(
        sv['proj_a'], dy_s5, p['wb_re'], p['wb_im'], p['wc_re'], p['wc_im'], p['lamb_re'], p['lamb_im'], p['d_skip'],
        name=n("s5_bwd"), ride=make_ride(carried[1]) if carried else None)
    if carried:
        carried[0](list(carried[1]), received)
    g_ = sw // SSM_H
    pst = p['lamb_re'].shape[1] // g_
    gs['lamb_re'], gs['lamb_im'] = dlr.reshape(g_, pst), dli.reshape(g_, pst)
    gs['bbar_re'] = _s5_block_grads(dwbr, SSM_H, pst, False)
    gs['bbar_im'] = _s5_block_grads(dwbi, SSM_H, pst, False)
    gs['c_re'] = _s5_block_grads(dwcr, pst, SSM_H, True).transpose(0, 2, 1)
    gs['c_im'] = _s5_block_grads(dwci, pst, SSM_H, True).transpose(0, 2, 1)

    dproj = jnp.concatenate([du.astype(BF), dq.astype(BF), dk.astype(BF), dv.astype(BF), dflog, dga, dgb], axis=1)
    gw['w_in'] = _mm(sv['h1'], dproj, ta=True, out_dtype=BF, tn=1408, name=n("proj_bwd_w"))
    if make_ride is not None:
        gw = {k: g for k, g in gw.items() if k not in sent}
    if make_ride is not None and not defer_tail:
        dh1, received = _mm(dproj, p['w_in_all'], tb=True, tk=1408, name=n("proj_bwd_x"), ride=make_ride(gw))
        on_receive(list(gw), received)
        gw = {}
    else:
        dh1 = _mm(dproj, p['w_in_all'], tb=True, tk=1408, name=n("proj_bwd_x"))
    dx, dshift_m, dscale_m, gs['g_pre_mix'] = _rowwise(
        pre_bwd, [dh1, dx2, x], [p['g_pre_mix'], scale_m],
        [(d, F32, 'tile'), (d, F32, 'sum'), (d, F32, 'sum'), (d, F32, 'sum')], name=n("pre_mix_bwd"))
    dmod = [dshift_m, dscale_m, dgate_m, dshift_f, dscale_f, dgate_f]
    return dx, gw, dmod, gs


def _unshard(k, blocks):
    if k in COL_SHARDED:
        return blocks.transpose(1, 0, 2).reshape(blocks.shape[1], NDEV * blocks.shape[2])
    return blocks.reshape(NDEV * blocks.shape[1], blocks.shape[2])


def _to_slabs(k, g):
    if k == 'w_in':
        d = g.shape[0]
        nh = d // LANES
        g = jnp.concatenate([g[:, :2 * d + nh], g[:, 2 * d + LANES:]], axis=1)
    if k in COL_SHARDED:
        return g.reshape(g.shape[0], NDEV, g.shape[1] // NDEV).transpose(1, 0, 2)
    return g.reshape(NDEV, g.shape[0] // NDEV, g.shape[1])


def _prep_w_in(w_in):
    d = w_in.shape[0]
    nh = d // LANES
    fcol = 2 * d
    p = {}
    p['w_in_a'] = w_in[:, :fcol]
    p['w_in_f'] = jnp.pad(w_in[:, fcol:fcol + nh], ((0, 0), (0, LANES - nh)))
    p['w_in_g'] = w_in[:, fcol + nh:]
    p['w_in_all'] = jnp.concatenate([p['w_in_a'], p['w_in_f'], p['w_in_g']], axis=1)
    return p


def _prep_small(small):
    nh = small['b_f'].shape[0]
    p = {}
    for k in ('g_pre_mix', 'g_post_mix', 'g_pre_ffn', 'g_post_ffn', 'd_skip', 'b_glu'):
        p[k] = _row(small[k])
    p['b_f_row'] = jnp.pad(_row(small['b_f']), ((0, 0), (0, LANES - nh)))
    p['b_f_col'] = small['b_f'].reshape(nh, 1)
    lbr, lbi, bbr, bbi = _s5_discretise(small['lam_re'], small['lam_im'], small['log_dt'], small['b_re'], small['b_im'])
    p['lamb_re'], p['lamb_im'] = _row(lbr), _row(lbi)
    p['wb_re'] = _s5_operand(bbr, True).astype(BF)
    p['wb_im'] = _s5_operand(bbi, True).astype(BF)
    p['wc_re'] = _s5_operand(small['c_re'].transpose(0, 2, 1), False).astype(BF)
    p['wc_im'] = _s5_operand(small['c_im'].transpose(0, 2, 1), False).astype(BF)
    return p


def _local_step(x, target, mods, ps, small, hooks=None):
    depth = len(ps)
    s, d = x.shape
    hooks = hooks or {}
    saved = []
    h = x
    for l in range(depth):
        h, sv = _layer_fwd(h, mods[l], ps[l], l, ride=hooks['fwd_ride'](l) if hooks else None,
                           on_receive=functools.partial(hooks['fwd_recv'], l) if hooks else None)
        saved.append(sv)

    def loss_fn(yv, tv):
        e = yv - tv
        return e * (1.0 / d), jnp.sum(e * e, axis=1, keepdims=True) * (0.5 / d)

    dy, loss = _rowwise(loss_fn, [h, target], [], [(d, F32, 'tile'), (1, F32, 'sum')], name="loss")
    dmods, gss = [None] * depth, [None] * depth
    unsent = {}
    carried = None
    for l in range(depth - 1, -1, -1):
        def on_receive(names, results, l=l):
            hooks['bwd_recv']([(k, l) for k in names], results)

        dy, gw, dmods[l], gs = _layer_bwd(dy, saved[l], mods[l], ps[l], l,
                                          make_ride=hooks['bwd_ride'] if hooks else None,
                                          on_receive=on_receive if hooks else None,
                                          carried=carried, defer_tail=bool(hooks) and l > 0)
        if hooks and l > 0:
            carried = (on_receive, gw)
        else:
            unsent.update({(k, l): g for k, g in gw.items()})
        sm = small[l]
        _, vjp = jax.vjp(_s5_discretise, sm['lam_re'], sm['lam_im'], sm['log_dt'], sm['b_re'], sm['b_im'])
        gs['lam_re'], gs['lam_im'], gs['log_dt'], gs['b_re'], gs['b_im'] = vjp(
            (gs.pop('lamb_re'), gs.pop('lamb_im'), gs.pop('bbar_re'), gs.pop('bbar_im')))
        gss[l] = gs
    return loss, dy, unsent, dmods, gss


SMALL_LOCAL = ['g_pre_mix', 'g_post_mix', 'g_pre_ffn', 'g_post_ffn', 'lam_re', 'lam_im', 'log_dt', 'b_re', 'b_im',
               'c_re', 'c_im', 'd_skip', 'b_glu', 'b_f']


def kernel(x, c, w_ada, b_ada, g_pre_mix, g_post_mix, g_pre_ffn, g_post_ffn, w_in, lam_re, lam_im, log_dt, b_re, b_im, c_re, c_im, d_skip, w_glu, b_glu, b_f, w_pa, w_pb, w_o, w_ffn_gate, w_ffn_up, w_ffn_down, loss_target, m_w_ada, m_b_ada, m_g_pre_mix, m_g_post_mix, m_g_pre_ffn, m_g_post_ffn, m_w_in, m_lam_re, m_lam_im, m_log_dt, m_b_re, m_b_im, m_c_re, m_c_im, m_d_skip, m_w_glu, m_b_glu, m_b_f, m_w_pa, m_w_pb, m_w_o, m_w_ffn_gate, m_w_ffn_up, m_w_ffn_down, v_w_ada, v_b_ada, v_g_pre_mix, v_g_post_mix, v_g_pre_ffn, v_g_post_ffn, v_w_in, v_lam_re, v_lam_im, v_log_dt, v_b_re, v_b_im, v_c_re, v_c_im, v_d_skip, v_w_glu, v_b_glu, v_b_f, v_w_pa, v_w_pb, v_w_o, v_w_ffn_gate, v_w_ffn_up, v_w_ffn_down):
    args = dict(locals())
    view = lambda k, a: jnp.swapaxes(a, -1, -2) if k in TRANSPOSED else a
    W = {k: view(k, args[k]) for k in WEIGHTS}
    M = {k: view(k, args['m_' + k]) for k in WEIGHTS}
    V = {k: view(k, args['v_' + k]) for k in WEIGHTS}
    depth, d, ncol = w_ada.shape
    s = x.shape[1]
    me = 4 * lax.axis_index("x") + 2 * lax.axis_index("y") + lax.axis_index("c")

    first = ['w_in', 'w_glu']
    c_all, *first_blocks = _exchange([jnp.pad(c, ((0, SUBLANES - 1), (0, 0)))] + [W[k][0].astype(BF) for k in first],
                                     True, name="gather_first")
    c_all = c_all[:, 0, :]

    cond, = _rowwise(_silu, [c_all], [], [(d, F32, 'tile')], name="cond")
    mod_part = jnp.stack([_mm(cond, w_ada[l], name=f"ada_mm{l}") for l in range(depth)], axis=1)
    mod_recv, = _exchange([mod_part.reshape(NDEV, depth, 1, ncol)], False, name="scatter_mod")
    mod_cat = mod_recv.reshape(NDEV, depth, ncol).transpose(1, 0, 2).reshape(depth, NDEV * ncol)
    mod, = _rowwise(lambda a, b: a + b, [mod_cat, b_ada], [], [(NDEV * ncol, F32, 'tile')], name="mod_bias")
    mods = [[mod[l:l + 1, i * d:(i + 1) * d] for i in range(6)] for l in range(depth)]

    small = [{k: W[k][l] for k in SMALL_LOCAL} for l in range(depth)]
    ps = [_prep_small(small[l]) for l in range(depth)]
    rest = [k for k in BIG if k not in first]
    riding = [[(k, l) for k in rest] + [(k, l + 1) for k in first if l + 1 < depth] for l in range(depth)]

    def take_weights(keys, results):
        for (k, l), blocks in zip(keys, results):
            full = _unshard(k, blocks)
            ps[l].update(_prep_w_in(full) if k == 'w_in' else {k: full})

    take_weights([(k, 0) for k in first], first_blocks)

    def fwd_ride(l):
        blocks = [W[k][ll].astype(BF) for k, ll in riding[l]]
        return _Exchange(blocks, True), blocks

    grad_parts = {}

    def bwd_ride(grads):
        slabs = [_to_slabs(k, g) for k, g in grads.items()]
        return _Exchange(slabs, False), slabs

    hooks = dict(fwd_ride=fwd_ride, fwd_recv=lambda l, results: take_weights(riding[l], results),
                 bwd_ride=bwd_ride, bwd_recv=lambda keys, results: grad_parts.update(zip(keys, results)))

    loss, dx, unsent, dmods, gss = _local_step(x[0], loss_target[0], mods, ps, small, hooks)
    assert not unsent
    out = {}
    for k in BIG:
        out[k] = _adamw([grad_parts[(k, l)] for l in range(depth)], W[k], M[k], V[k], name=f"adamw_{k}")

    dmod_mine = jnp.stack([jnp.concatenate(dmods[l], axis=1)[0] for l in range(depth)])
    small_mine = [dmod_mine] + [jnp.stack([gss[l][k] for l in range(depth)]) for k in SMALL_LOCAL] + [loss]
    parts, = _exchange([_pack(small_mine)], True, name="gather_small")
    summed = _sum_parts(parts, name="sum_small")
    names = ['b_ada'] + SMALL_LOCAL
    *small_grads, loss = _unpack(summed, [W[k] for k in names] + [loss])
    loss = loss[0, 0]
    for k, g in zip(names, small_grads):
        shp = W[k].shape
        rows = lambda a: a.reshape(depth, -1, shp[-1])
        res = _adamw([rows(g)[l][None] for l in range(depth)], rows(W[k]), rows(M[k]), rows(V[k]), name=f"adamw_{k}")
        out[k] = [a.reshape(shp) for a in res]

    dmod_all = parts.reshape(NDEV, -1)[:, :depth * 6 * d].reshape(NDEV, depth, 6 * d)
    dmod_cols = lax.dynamic_slice_in_dim(dmod_all, me * ncol, ncol, axis=2)
    g_ada = [_mm(cond, dmod_cols[:, l], ta=True, precision=HI, name=f"ada_bwd{l}")[None] for l in range(depth)]
    out['w_ada'] = _adamw(g_ada, w_ada, m_w_ada, v_w_ada, name="adamw_w_ada")

    return (loss, dx[None], *[view(k, out[k][i]) for i in range(4) for k in WEIGHTS])
```

```python
import functools
import math

import jax
import jax.numpy as jnp
from jax import lax
from jax.experimental import pallas as pl
from jax.experimental.pallas import tpu as pltpu

F32 = jnp.float32
BF = jnp.bfloat16
NDEV = 8
LANES = 128
SUBLANES = 8
VMEM_LIMIT = 48 * 1024 * 1024

SSM_H = 16
HEAD_DIM = 64
RMS_EPS = 1e-6
EIG_CLIP = 1e-4
ADAM_LR = 0.001
ADAM_B1 = 0.9
ADAM_B2 = 0.999
ADAM_EPS = 1e-08
ADAM_WD = 0.01
ADAM_STEP = 10
NEG = -1e30
HI = lax.Precision.HIGHEST

WEIGHTS = ['w_ada', 'b_ada', 'g_pre_mix', 'g_post_mix', 'g_pre_ffn', 'g_post_ffn', 'w_in', 'lam_re', 'lam_im',
           'log_dt', 'b_re', 'b_im', 'c_re', 'c_im', 'd_skip', 'w_glu', 'b_glu', 'b_f', 'w_pa', 'w_pb', 'w_o',
           'w_ffn_gate', 'w_ffn_up', 'w_ffn_down']
TRANSPOSED = ['w_ffn_gate', 'w_ffn_up', 'b_re', 'b_im']
COL_SHARDED = ['w_in', 'w_pa', 'w_pb']
ROW_SHARDED = ['w_glu', 'w_o', 'w_ffn_down', 'w_ffn_gate', 'w_ffn_up']
BIG = COL_SHARDED + ROW_SHARDED
SMALL = ['b_ada', 'g_pre_mix', 'g_post_mix', 'g_pre_ffn', 'g_post_ffn', 'lam_re', 'lam_im', 'log_dt', 'b_re',
         'b_im', 'c_re', 'c_im', 'd_skip', 'b_glu', 'b_f']


def _fit(dim, target, align):
    if dim <= target:
        return dim
    t = (target // align) * align
    while t >= align:
        if dim % t == 0:
            return t
        t -= align
    return dim


def _params(**kw):
    return pltpu.CompilerParams(vmem_limit_bytes=VMEM_LIMIT, **kw)


def _mm(a, b, *, ta=False, tb=False, out_dtype=F32, tm=None, tn=512, tk=2048, precision=None, name, ride=None,
        second=None):
    m, k = (a.shape[1], a.shape[0]) if ta else a.shape
    n = b.shape[0] if tb else b.shape[1]
    assert (b.shape[1] if tb else b.shape[0]) == k
    tm = _fit(m, tm or (1024 if ta else 2048), LANES if ta else 16)
    tn = _fit(n, tn, LANES)
    tk = _fit(k, tk, LANES)
    nk = k // tk
    grid = (m // tm, n // tn, nk)
    dims = (((0 if ta else 1,), (1 if tb else 0,)), ((), ()))
    ex, ex_arrays = ride if ride is not None else (None, [])

    pairs = [(a, b)] + ([second] if second is not None else [])

    def kern(*refs):
        ab_refs, (o_ref,), comm, scratch = _ride_split(ex, refs, 2 * len(pairs), 1)
        step = (pl.program_id(0) * grid[1] + pl.program_id(1)) * grid[2] + pl.program_id(2)
        if ex is not None:
            @pl.when(step == 0)
            def _():
                ex.start(*comm)

            @pl.when(step == (grid[0] * grid[1] * grid[2]) // 2)
            def _():
                ex.forward(*comm)

        p = None
        for a_ref, b_ref in zip(ab_refs[::2], ab_refs[1::2]):
            av, bv = a_ref[...], b_ref[...]
            if precision is None:
                av, bv = av.astype(BF), bv.astype(BF)
            q = lax.dot_general(av, bv, dims, preferred_element_type=F32, precision=precision)
            p = q if p is None else p + q
        if nk == 1:
            o_ref[...] = p.astype(out_dtype)
        else:
            acc_ref, = scratch
            kk = pl.program_id(2)

            @pl.when(kk == 0)
            def _():
                acc_ref[...] = p

            @pl.when(kk > 0)
            def _():
                acc_ref[...] += p

            @pl.when(kk == nk - 1)
            def _():
                o_ref[...] = acc_ref[...].astype(out_dtype)

        if ex is not None:
            @pl.when(step == grid[0] * grid[1] * grid[2] - 1)
            def _():
                ex.wait(*comm)

    a_spec = pl.BlockSpec((tk, tm), lambda i, j, kk: (kk, i)) if ta else pl.BlockSpec((tm, tk), lambda i, j, kk: (i, kk))
    b_spec = pl.BlockSpec((tn, tk), lambda i, j, kk: (j, kk)) if tb else pl.BlockSpec((tk, tn), lambda i, j, kk: (kk, j))
    res = pl.pallas_call(
        kern, name=name,
        out_shape=[jax.ShapeDtypeStruct((m, n), out_dtype)] + (ex.out_shape if ex else []),
        grid=grid,
        in_specs=[a_spec, b_spec] * len(pairs) + (ex.specs if ex else []),
        out_specs=[pl.BlockSpec((tm, tn), lambda i, j, kk: (i, j))] + (ex.specs if ex else []),
        scratch_shapes=(ex.scratch if ex else []) + ([] if nk == 1 else [pltpu.VMEM((tm, tn), F32)]),
        compiler_params=_params(dimension_semantics=("arbitrary",) * 3 if ex else ("parallel", "parallel", "arbitrary"),
                                has_side_effects=ex is not None),
    )(*[x for pair in pairs for x in pair], *ex_arrays)
    return (res[0], res[1:]) if ex else res[0]


def _mm_fused(a, bs, extras, fn, out_dtypes, *, rows=(), tb=False, tm=2048, tn=256, name):
    m, k = a.shape
    n = bs[0].shape[0] if tb else bs[0].shape[1]
    tm = _fit(m, tm, 16)
    tn = _fit(n, tn, LANES)
    extras = [e if isinstance(e, tuple) else (e, 0) for e in extras]
    nb, ne, nr = len(bs), len(extras), len(rows)
    dims = (((1,), (1 if tb else 0,)), ((), ()))

    def kern(*refs):
        av = refs[0][...].astype(BF)
        prods = [lax.dot_general(av, r[...].astype(BF), dims, preferred_element_type=F32) for r in refs[1:1 + nb]]
        res = fn(*prods, *[r[...] for r in refs[1 + nb:1 + nb + ne + nr]])
        for o_ref, r, dt in zip(refs[1 + nb + ne + nr:], res, out_dtypes):
            o_ref[...] = r.astype(dt)

    tile = pl.BlockSpec((tm, tn), lambda i, j: (i, j))
    b_spec = pl.BlockSpec((tn, k), lambda i, j: (j, 0)) if tb else pl.BlockSpec((k, tn), lambda i, j: (0, j))
    return pl.pallas_call(
        kern, name=name, out_shape=[jax.ShapeDtypeStruct((m, n), dt) for dt in out_dtypes],
        grid=(m // tm, n // tn),
        in_specs=[pl.BlockSpec((tm, k), lambda i, j: (i, 0))] + [b_spec] * nb
        + [pl.BlockSpec((tm, tn), lambda i, j, c=c: (i, j + c * (n // tn))) for _, c in extras]
        + [pl.BlockSpec((1, tn), lambda i, j: (0, j))] * nr,
        out_specs=[tile] * len(out_dtypes),
        compiler_params=_params(dimension_semantics=("parallel", "parallel")),
    )(a, *bs, *[e for e, _ in extras], *rows)


def _rowwise(fn, tiles, params, outs, *, tr=256, name):
    tiles = [t if isinstance(t, tuple) else (t, t.shape[1], 0) for t in tiles]
    s = tiles[0][0].shape[0]
    tr = _fit(s, tr, 16)
    nt, npar = len(tiles), len(params)

    def kern(*refs):
        i = pl.program_id(0)
        res = fn(*[r[...] for r in refs[:nt + npar]])
        if not isinstance(res, (tuple, list)):
            res = (res,)
        for (w, dt, kind), o_ref, r in zip(outs, refs[nt + npar:], res):
            if kind == 'tile':
                o_ref[...] = r.astype(dt)
            else:
                part = jnp.sum(r.astype(F32), axis=0, keepdims=True)

                @pl.when(i == 0)
                def _(o_ref=o_ref, part=part):
                    o_ref[...] = part

                @pl.when(i > 0)
                def _(o_ref=o_ref, part=part):
                    o_ref[...] += part

    def tile_spec(w, cb):
        return pl.BlockSpec((tr, w), lambda i: (i, cb))

    in_specs = [tile_spec(w, cb) for _, w, cb in tiles]
    in_specs += [pl.BlockSpec(p.shape, lambda i, nd=p.ndim: (0,) * nd) for p in params]
    out_shape, out_specs = [], []
    for w, dt, kind in outs:
        if kind == 'tile':
            out_shape.append(jax.ShapeDtypeStruct((s, w), dt))
            out_specs.append(pl.BlockSpec((tr, w), lambda i: (i, 0)))
        else:
            out_shape.append(jax.ShapeDtypeStruct((1, w), F32))
            out_specs.append(pl.BlockSpec((1, w), lambda i: (0, 0)))
    res = pl.pallas_call(
        kern, name=name, out_shape=out_shape, grid=(s // tr,), in_specs=in_specs, out_specs=out_specs,
        compiler_params=_params(dimension_semantics=("arbitrary",)),
    )(*[t[0] for t in tiles], *params)
    return res


def _sigmoid(z):
    return 1.0 / (1.0 + jnp.exp(-z))


def _silu(z):
    return z * _sigmoid(z)


_GELU_K = math.sqrt(2.0 / math.pi)


def _gelu(y):
    return 0.5 * y * (1.0 + jnp.tanh(_GELU_K * (y + 0.044715 * y * y * y)))


def _gelu_grad(y):
    th = jnp.tanh(_GELU_K * (y + 0.044715 * y * y * y))
    return 0.5 * (1.0 + th) + 0.5 * y * (1.0 - th * th) * _GELU_K * (1.0 + 3.0 * 0.044715 * y * y)


def _rms(x):
    return lax.rsqrt(jnp.mean(x * x, axis=-1, keepdims=True) + RMS_EPS)


def _norm_bwd(dn, xhat, r):
    return r * (dn - xhat * jnp.mean(dn * xhat, axis=-1, keepdims=True))


def _cum_fwd(flog, bf_row, nh, *, name):
    s = flog.shape[0]
    w = nh * HEAD_DIM
    t = _fit(s, 256, SUBLANES)

    def kern(f_ref, b_ref, o_ref, carry_ref):
        i = pl.program_id(0)

        @pl.when(i == 0)
        def _():
            carry_ref[...] = jnp.zeros_like(carry_ref)

        z = f_ref[...] + b_ref[...]
        logf = jnp.minimum(z, 0.0) - jnp.log(1.0 + jnp.exp(-jnp.abs(z)))
        hh = lax.broadcasted_iota(jnp.int32, (LANES, w), 0)
        cc = lax.broadcasted_iota(jnp.int32, (LANES, w), 1)
        expand = (cc // HEAD_DIM == hh).astype(F32)
        lx = jnp.dot(logf, expand, preferred_element_type=F32, precision=HI)
        rr = lax.broadcasted_iota(jnp.int32, (t, t), 0)
        kk = lax.broadcasted_iota(jnp.int32, (t, t), 1)
        tri = (kk <= rr).astype(F32)
        cum = jnp.dot(tri, lx, preferred_element_type=F32, precision=HI) + carry_ref[...]
        o_ref[...] = cum
        carry_ref[...] = cum[t - 1:t, :]

    return pl.pallas_call(
        kern, name=name, out_shape=jax.ShapeDtypeStruct((s, w), F32), grid=(s // t,),
        in_specs=[pl.BlockSpec((t, LANES), lambda i: (i, 0)), pl.BlockSpec((1, LANES), lambda i: (0, 0))],
        out_specs=pl.BlockSpec((t, w), lambda i: (i, 0)),
        scratch_shapes=[pltpu.VMEM((1, w), F32)],
        compiler_params=_params(dimension_semantics=("arbitrary",)),
    )(flog, bf_row)


def _cum_bwd(dcrow, flog, bf_col, *, name):
    _, nh, s = dcrow.shape
    t = _fit(s, 512, LANES)
    nb = s // t

    def kern(d_ref, f_ref, b_ref, df_ref, db_ref):
        rr = lax.broadcasted_iota(jnp.int32, (t, t), 0)
        kk = lax.broadcasted_iota(jnp.int32, (t, t), 1)
        upper = (rr >= kk).astype(F32)
        pick = (lax.broadcasted_iota(jnp.int32, (nh, LANES), 0)
                == lax.broadcasted_iota(jnp.int32, (nh, LANES), 1)).astype(F32)
        carry = jnp.zeros((nh, 1), F32)
        db = jnp.zeros((nh, 1), F32)
        for blk in range(nb - 1, -1, -1):
            sl = slice(blk * t, (blk + 1) * t)
            rc = jnp.dot(d_ref[0, :, sl] + d_ref[1, :, sl], upper, preferred_element_type=F32, precision=HI) + carry
            carry = rc[:, 0:1]
            frow = lax.dot_general(pick, f_ref[sl, :], (((1,), (1,)), ((), ())), preferred_element_type=F32,
                                   precision=HI)
            df = rc * _sigmoid(-(frow + b_ref[...]))
            df_ref[sl, :] = lax.dot_general(df, pick, (((0,), (0,)), ((), ())), preferred_element_type=F32,
                                            precision=HI).astype(BF)
            db = db + jnp.sum(df, axis=1, keepdims=True)
        db_ref[...] = db

    return pl.pallas_call(
        kern, name=name,
        out_shape=[jax.ShapeDtypeStruct((s, LANES), BF), jax.ShapeDtypeStruct((nh, 1), F32)],
        compiler_params=_params(),
    )(dcrow, flog, bf_col)


def _ride_split(ex, refs, n_in, n_out):
    n = ex.n if ex is not None else 0
    own_in, srcs = refs[:n_in], refs[n_in:n_in + n]
    own_out, dsts = refs[n_in + n:n_in + n + n_out], refs[n_in + n + n_out:n_in + 2 * n + n_out]
    sems = refs[n_in + 2 * n + n_out:n_in + 2 * n + n_out + 3] if n else ()
    rest = refs[n_in + 2 * n + n_out + (3 if n else 0):]
    return own_in, own_out, (srcs, dsts, sems), rest


ATTN_TILE = 512
ATTN_TILE_BWD = 256
ATTN_STRIP = 32
BIAS_LANES = 3


def _head_masks(rows):
    lane = lax.broadcasted_iota(jnp.int32, (rows, LANES), 1)
    return [(lane >= HEAD_DIM * e) & (lane < HEAD_DIM * (e + 1)) for e in range(2)]


def _augment(feat, bias, e, *, bias_slot, ones_slot):
    rows = feat.shape[0]
    lane = lax.broadcasted_iota(jnp.int32, (rows, LANES), 1)
    own = (lane >= HEAD_DIM * e) & (lane < HEAD_DIM * (e + 1))
    off = lane - HEAD_DIM * (1 - e)
    out = jnp.where(own, feat, 0.0)
    if ones_slot is not None:
        out = jnp.where((off >= ones_slot * BIAS_LANES) & (off < (ones_slot + 1) * BIAS_LANES), 1.0, out)
    if bias is not None:
        rest = pltpu.roll(bias, HEAD_DIM, 1)
        for term in range(BIAS_LANES):
            part = rest.astype(BF).astype(F32)
            out = jnp.where(off == bias_slot * BIAS_LANES + term, part, out)
            rest = rest - part
    return out.astype(BF)


def _two_slot_pipeline(m, scores, tile):
    scores(0, 0)

    def pair(n, carry):
        k = 2 * n
        scores(k + 1, 1)
        tile(k, 0, False)
        scores(k + 2, 0)
        tile(k + 1, 1, False)
        return carry

    lax.fori_loop(0, m // 2, pair, 0)

    @pl.when(m % 2 == 0)
    def _():
        tile(m, 0, True)

    @pl.when(m % 2 == 1)
    def _():
        scores(m, 1)
        tile(m - 1, 0, False)
        tile(m, 1, True)


def _attn_fwd(proj, cumx, qcol, *, name, ride=None):
    s = proj.shape[0]
    w = cumx.shape[1]
    nhp = w // LANES
    t = _fit(s, ATTN_TILE, LANES)
    nq = s // t
    strip = _fit(t, ATTN_STRIP, 16)
    scale = HEAD_DIM ** -0.5
    qb, kb, vb = qcol // LANES, (qcol + w) // LANES, (qcol + 2 * w) // LANES
    ex, ex_arrays = ride if ride is not None else (None, [])
    nt_dims = (((1,), (1,)), ((), ()))

    def kern(*refs):
        own_in, (o_ref, l_ref), comm, scratch = _ride_split(ex, refs, 5, 2)
        q_ref, k_ref, v_ref, cxq_ref, cxk_ref = own_in
        ka_ref, vat_ref, s0_ref, s1_ref, p_ref, m_ref, acc_ref = scratch
        s_refs = (s0_ref, s1_ref)
        i = pl.program_id(1)
        if ex is not None:
            @pl.when((pl.program_id(0) == 0) & (i == 0))
            def _():
                ex.start(*comm)

            @pl.when((pl.program_id(0) == nhp - 1) & (i == 0))
            def _():
                ex.forward(*comm)

        msks = _head_masks(t)

        @pl.when(i == 0)
        def _():
            def build(c, carry):
                rows = pl.ds(pl.multiple_of(c * t, LANES), t)
                k2, v2, cx = k_ref[rows, :], v_ref[rows, :], cxk_ref[rows, :]
                for e in range(2):
                    ka_ref[e, rows, :] = _augment(k2, -cx, e, bias_slot=1, ones_slot=0)
                    vat_ref[e, :, rows] = jnp.where(msks[e], v2, 1.0).T.astype(BF)
                return carry
            lax.fori_loop(0, nq, build, 0)

        q2 = q_ref[...] * scale
        qa = [_augment(q2, cxq_ref[...], e, bias_slot=0, ones_slot=1) for e in range(2)]
        m_ref[...] = jnp.full(m_ref.shape, NEG, F32)
        acc_ref[...] = jnp.zeros(acc_ref.shape, F32)
        slabs = strip // SUBLANES

        def scores(j, slot):
            rows_k = pl.ds(pl.multiple_of(j * t, LANES), t)
            for e in range(2):
                st = lax.dot_general(ka_ref[e, rows_k, :], qa[e], nt_dims, preferred_element_type=F32)
                s_refs[slot][e] = st.reshape(t // SUBLANES, SUBLANES, t)

        def tile(j, slot, diagonal):
            rows_k = pl.ds(pl.multiple_of(j * t, LANES), t)
            s_ref = s_refs[slot]
            for e in range(2):
                mx = jnp.full((SUBLANES, t), NEG, F32)
                for r in range(t // strip):
                    sl = slice(r * slabs, (r + 1) * slabs)
                    sv = s_ref[e,sl]
                    if diagonal:
                        shape = (slabs, SUBLANES, t)
                        key = (r * strip + lax.broadcasted_iota(jnp.int32, shape, 0) * SUBLANES
                               + lax.broadcasted_iota(jnp.int32, shape, 1))
                        sv = jnp.where(key <= lax.broadcasted_iota(jnp.int32, shape, 2), sv, NEG)
                        s_ref[e,sl] = sv
                    mx = jnp.maximum(mx, jnp.max(sv, axis=0))
                for sh in (4, 2, 1):
                    mx = jnp.maximum(mx, pltpu.roll(mx, sh, 0))
                m_old = m_ref[e]
                m_new = jnp.maximum(m_old, mx)
                alpha = jnp.exp(m_old - m_new)
                m_ref[e] = m_new
                for r in range(t // strip):
                    p = jnp.exp(s_ref[e,r * slabs:(r + 1) * slabs] - m_new[None])
                    p_ref[e, r * strip:(r + 1) * strip, :] = p.reshape(strip, t).astype(BF)
                acc = acc_ref[e].reshape(LANES // SUBLANES, SUBLANES, t) * alpha[None]
                acc_ref[e] = acc.reshape(LANES, t) + jnp.dot(vat_ref[e, :, rows_k], p_ref[e],
                                                             preferred_element_type=F32)

        _two_slot_pipeline(i, scores, tile)

        outs, lses = [], []
        for e in range(2):
            acc = acc_ref[e]
            other = HEAD_DIM * (1 - e)
            den = acc[other:other + 1, :]
            outs.append(acc / den)
            lses.append(jnp.broadcast_to(m_ref[e][0:1, :] + jnp.log(den), (LANES, t)))
        upper = lax.broadcasted_iota(jnp.int32, (LANES, t), 0) < HEAD_DIM
        o_ref[...] = jnp.where(upper, outs[0], outs[1]).T.astype(BF)
        l_ref[...] = jnp.where(upper, lses[0], lses[1]).T
        if ex is not None:
            @pl.when((pl.program_id(0) == nhp - 1) & (i == nq - 1))
            def _():
                ex.wait(*comm)

    own_scratch = [pltpu.VMEM((2, s, LANES), BF), pltpu.VMEM((2, LANES, s), BF),
                   pltpu.VMEM((2, t // SUBLANES, SUBLANES, t), F32),
                   pltpu.VMEM((2, t // SUBLANES, SUBLANES, t), F32), pltpu.VMEM((2, t, t), BF),
                   pltpu.VMEM((2, SUBLANES, t), F32), pltpu.VMEM((2, LANES, t), F32)]
    return pl.pallas_call(
        kern, name=name,
        out_shape=[jax.ShapeDtypeStruct((s, w), BF), jax.ShapeDtypeStruct((nhp, s, LANES), F32)]
        + (ex.out_shape if ex else []),
        grid=(nhp, nq),
        in_specs=[pl.BlockSpec((t, LANES), lambda h, i: (i, qb + h)),
                  pl.BlockSpec((s, LANES), lambda h, i: (0, kb + h)),
                  pl.BlockSpec((s, LANES), lambda h, i: (0, vb + h)),
                  pl.BlockSpec((t, LANES), lambda h, i: (i, h)),
                  pl.BlockSpec((s, LANES), lambda h, i: (0, h))] + (ex.specs if ex else []),
        out_specs=[pl.BlockSpec((t, LANES), lambda h, i: (i, h)),
                   pl.BlockSpec((None, t, LANES), lambda h, i: (h, i, 0))] + (ex.specs if ex else []),
        scratch_shapes=(ex.scratch if ex else []) + own_scratch,
        compiler_params=_params(dimension_semantics=("arbitrary", "arbitrary"),
                                has_side_effects=ex is not None),
    )(proj, proj, proj, cumx, cumx, *ex_arrays)


def _attn_bwd(proj, do, o, lse, cumx, qcol, *, name, ride=None):
    s = proj.shape[0]
    w = cumx.shape[1]
    nhp = w // LANES
    t = _fit(s, ATTN_TILE_BWD, LANES)
    nq = s // t
    strip = _fit(t, ATTN_STRIP, 16)
    scale = HEAD_DIM ** -0.5
    qb, kb, vb = qcol // LANES, (qcol + w) // LANES, (qcol + 2 * w) // LANES
    tn_dims = (((0,), (0,)), ((), ()))
    nt_dims = (((1,), (1,)), ((), ()))
    ex, ex_arrays = ride if ride is not None else (None, [])

    def kern(*refs):
        own_in, own_out, comm, scratch = _ride_split(ex, refs, 7, 5)
        q_ref, k_ref, v_ref, do_ref, o_ref, l_ref, cx_ref = own_in
        dq_ref, dk_ref, dv_ref, dkc_ref, dqc_ref = own_out
        qa_ref, da_ref, dqa_ref, dka_ref, dva_ref, st0_ref, st1_ref, dpt0_ref, dpt1_ref, pt_ref, dst_ref = scratch
        st_refs, dpt_refs = (st0_ref, st1_ref), (dpt0_ref, dpt1_ref)
        j = pl.program_id(1)
        if ex is not None:
            @pl.when((pl.program_id(0) == 0) & (j == 0))
            def _():
                ex.start(*comm)

        msks = _head_masks(t)

        @pl.when(j == 0)
        def _():
            def build(c, carry):
                rows = pl.ds(pl.multiple_of(c * t, LANES), t)
                q2 = q_ref[rows, :] * scale
                do2 = do_ref[rows, :]
                dd = do2 * o_ref[rows, :].astype(F32)
                delta = jnp.where(msks[0], jnp.sum(jnp.where(msks[0], dd, 0.0), axis=1, keepdims=True),
                                  jnp.sum(jnp.where(msks[1], dd, 0.0), axis=1, keepdims=True))
                bias = cx_ref[rows, :] - l_ref[rows, :]
                for e in range(2):
                    qa_ref[e, rows, :] = _augment(q2, bias, e, bias_slot=0, ones_slot=1)
                    da_ref[e, rows, :] = _augment(do2, -delta, e, bias_slot=0, ones_slot=None)
                return carry
            lax.fori_loop(0, nq, build, 0)
            dqa_ref[...] = jnp.zeros(dqa_ref.shape, F32)

        rows_k = pl.ds(pl.multiple_of(j * t, LANES), t)
        k2, v2 = k_ref[...], v_ref[...]
        ka = [_augment(k2, -cx_ref[rows_k, :], e, bias_slot=1, ones_slot=0) for e in range(2)]
        va = [_augment(v2, None, e, bias_slot=None, ones_slot=0) for e in range(2)]
        dka_ref[...] = jnp.zeros(dka_ref.shape, F32)
        dva_ref[...] = jnp.zeros(dva_ref.shape, F32)

        def scores(k, slot):
            rows_q = pl.ds(pl.multiple_of((nq - 1 - k) * t, LANES), t)
            for e in range(2):
                st_refs[slot][e] = lax.dot_general(ka[e], qa_ref[e, rows_q, :], nt_dims,
                                                   preferred_element_type=F32)
                dpt_refs[slot][e] = lax.dot_general(va[e], da_ref[e, rows_q, :], nt_dims,
                                                    preferred_element_type=F32)

        def tile(k, slot, diagonal):
            rows_q = pl.ds(pl.multiple_of((nq - 1 - k) * t, LANES), t)
            st_ref, dpt_ref = st_refs[slot], dpt_refs[slot]
            for e in range(2):
                for r in range(t // strip):
                    rows = slice(r * strip, (r + 1) * strip)
                    sv = st_ref[e, rows, :]
                    if diagonal:
                        key = r * strip + lax.broadcasted_iota(jnp.int32, (strip, t), 0)
                        qry = lax.broadcasted_iota(jnp.int32, (strip, t), 1)
                        sv = jnp.where(key <= qry, sv, NEG)
                    p = jnp.exp(sv)
                    pt_ref[e, rows, :] = p.astype(BF)
                    dst_ref[e, rows, :] = (p * dpt_ref[e, rows, :]).astype(BF)
            for e in range(2):
                dva_ref[e] += jnp.dot(pt_ref[e], da_ref[e, rows_q, :], preferred_element_type=F32)
                dka_ref[e] += jnp.dot(dst_ref[e], qa_ref[e, rows_q, :], preferred_element_type=F32)
                dqa_ref[e, rows_q, :] += lax.dot_general(dst_ref[e], ka[e], tn_dims, preferred_element_type=F32)

        _two_slot_pipeline(nq - 1 - j, scores, tile)

        dk_ref[...] = jnp.where(msks[0], dka_ref[0], dka_ref[1])
        dv_ref[...] = jnp.where(msks[0], dva_ref[0], dva_ref[1])
        sums = jnp.where(msks[1], dka_ref[0], dka_ref[1]).T
        dkc_ref[0:1, :] = sums[HEAD_DIM + BIAS_LANES:HEAD_DIM + BIAS_LANES + 1, :]
        dkc_ref[1:2, :] = sums[BIAS_LANES:BIAS_LANES + 1, :]

        @pl.when(j == nq - 1)
        def _():
            def flush(c, carry):
                rows = pl.ds(pl.multiple_of(c * t, LANES), t)
                a0, a1 = dqa_ref[0, rows, :], dqa_ref[1, rows, :]
                dq_ref[rows, :] = jnp.where(msks[0], a0, a1) * scale
                sums = jnp.where(msks[1], a0, a1).T
                dqc_ref[0:1, rows] = sums[HEAD_DIM:HEAD_DIM + 1, :]
                dqc_ref[1:2, rows] = sums[0:1, :]
                return carry
            lax.fori_loop(0, nq, flush, 0)

        if ex is not None:
            @pl.when((pl.program_id(0) == nhp - 1) & (j == nq - 1))
            def _():
                ex.wait(*comm)

    full = lambda cb: pl.BlockSpec((s, LANES), lambda h, j: (0, cb + h))
    blk = lambda cb: pl.BlockSpec((t, LANES), lambda h, j: (j, cb + h))
    own_scratch = [pltpu.VMEM((2, s, LANES), BF), pltpu.VMEM((2, s, LANES), BF), pltpu.VMEM((2, s, LANES), F32),
                   pltpu.VMEM((2, t, LANES), F32), pltpu.VMEM((2, t, LANES), F32),
                   pltpu.VMEM((2, t, t), F32), pltpu.VMEM((2, t, t), F32),
                   pltpu.VMEM((2, t, t), F32), pltpu.VMEM((2, t, t), F32),
                   pltpu.VMEM((2, t, t), BF), pltpu.VMEM((2, t, t), BF)]
    return pl.pallas_call(
        kern, name=name,
        out_shape=[jax.ShapeDtypeStruct((s, w), F32)] * 3 + [jax.ShapeDtypeStruct((nhp, 2, s), F32)] * 2
        + (ex.out_shape if ex else []),
        grid=(nhp, nq),
        in_specs=[full(qb), blk(kb), blk(vb), full(0), full(0),
                  pl.BlockSpec((None, s, LANES), lambda h, j: (h, 0, 0)), full(0)] + (ex.specs if ex else []),
        out_specs=[full(0), blk(0), blk(0), pl.BlockSpec((None, 2, t), lambda h, j: (h, 0, j)),
                   pl.BlockSpec((None, 2, s), lambda h, j: (h, 0, 0))] + (ex.specs if ex else []),
        scratch_shapes=(ex.scratch if ex else []) + own_scratch,
        compiler_params=_params(dimension_semantics=("arbitrary", "arbitrary"),
                                has_side_effects=ex is not None),
    )(proj, proj, proj, do, o, lse, cumx, *ex_arrays)


S5_STATES = 256
S5_ROWS = 512


def _cmul(ar, ai, br, bi):
    return ar * br - ai * bi, ar * bi + ai * br


def _scan_tables(lr, li, reverse):
    w = lr.shape[1]
    row = lax.broadcasted_iota(jnp.int32, (SUBLANES, w), 0)
    if reverse:
        row = SUBLANES - 1 - row
    lr1, li1 = jnp.broadcast_to(lr, (SUBLANES, w)), jnp.broadcast_to(li, (SUBLANES, w))
    lr2, li2 = _cmul(lr1, li1, lr1, li1)
    lr4, li4 = _cmul(lr2, li2, lr2, li2)
    steps = []
    for d, (pr, pi) in zip((1, 2, 4), ((lr1, li1), (lr2, li2), (lr4, li4))):
        keep = row >= d
        steps.append((jnp.where(keep, pr, 0.0), jnp.where(keep, pi, 0.0)))
    cr, ci = lr1, li1
    for bit, (pr, pi) in zip((1, 2, 4), ((lr1, li1), (lr2, li2), (lr4, li4))):
        nr, ni = _cmul(cr, ci, pr, pi)
        has = (row & bit) != 0
        cr, ci = jnp.where(has, nr, cr), jnp.where(has, ni, ci)
    return steps, (cr, ci)


def _scan_local(xr, xi, steps, reverse):
    for d, (pr, pi) in zip((1, 2, 4), steps):
        sh = (SUBLANES - d) if reverse else d
        sr, si = pltpu.roll(xr, sh, 0), pltpu.roll(xi, sh, 0)
        xr, xi = xr + (pr * sr - pi * si), xi + (pr * si + pi * sr)
    return xr, xi


def _scan_carry(xr, xi, car_r, car_i, carry_pow):
    cr, ci = carry_pow
    return xr + (cr * car_r - ci * car_i), xi + (cr * car_i + ci * car_r)


SCAN_UNROLL = 4


def _s5_specs(s, ncb):
    u_spec = pl.BlockSpec((s, LANES), lambda cb, hf: (0, cb))
    wb_spec = pl.BlockSpec((None, None, LANES, S5_STATES), lambda cb, hf: (cb, hf, 0, 0))
    wc_spec = pl.BlockSpec((None, None, S5_STATES, LANES), lambda cb, hf: (cb, hf, 0, 0))
    lam_spec = pl.BlockSpec((1, S5_STATES), lambda cb, hf: (0, 2 * cb + hf))
    d_spec = pl.BlockSpec((1, LANES), lambda cb, hf: (0, cb))
    return u_spec, wb_spec, wc_spec, lam_spec, d_spec


def _s5_project_and_scan(u_ref, wbr_ref, wbi_ref, lr_ref, li_ref, xr_ref, xi_ref, s, rows):
    wbr, wbi = wbr_ref[...], wbi_ref[...]
    for r in range(s // rows):
        sl = pl.ds(r * rows, rows)
        ub = u_ref[sl, :].astype(BF)
        xr_ref[sl, :] = jnp.dot(ub, wbr, preferred_element_type=F32)
        xi_ref[sl, :] = jnp.dot(ub, wbi, preferred_element_type=F32)
    steps, cpow = _scan_tables(lr_ref[...], li_ref[...], False)

    unroll = _fit(s // SUBLANES, SCAN_UNROLL, 1)

    def body(b, carry):
        car_r, car_i = carry
        sls = [pl.ds(pl.multiple_of((b * unroll + q) * SUBLANES, SUBLANES), SUBLANES) for q in range(unroll)]
        blocks = [_scan_local(xr_ref[sl, :], xi_ref[sl, :], steps, False) for sl in sls]
        for sl, (xr, xi) in zip(sls, blocks):
            xr, xi = _scan_carry(xr, xi, car_r, car_i, cpow)
            xr_ref[sl, :] = xr
            xi_ref[sl, :] = xi
            car_r, car_i = xr[SUBLANES - 1:SUBLANES, :], xi[SUBLANES - 1:SUBLANES, :]
        return car_r, car_i

    zero = jnp.zeros((1, S5_STATES), F32)
    lax.fori_loop(0, s // SUBLANES // unroll, body, (zero, zero))


def _s5_fwd(proj, wb_re, wb_im, wc_re, wc_im, lam_re, lam_im, dskip, *, name):
    s = proj.shape[0]
    w = dskip.shape[1]
    ncb = w // LANES
    rows = _fit(s, S5_ROWS, SUBLANES)

    def kern(u_ref, wbr_ref, wbi_ref, wcr_ref, wci_ref, lr_ref, li_ref, d_ref, y_ref, xr_ref, xi_ref):
        hf = pl.program_id(1)
        _s5_project_and_scan(u_ref, wbr_ref, wbi_ref, lr_ref, li_ref, xr_ref, xi_ref, s, rows)
        wcr, wci = wcr_ref[...], wci_ref[...]
        for r in range(s // rows):
            sl = pl.ds(r * rows, rows)
            y = (jnp.dot(xr_ref[sl, :].astype(BF), wcr, preferred_element_type=F32)
                 - jnp.dot(xi_ref[sl, :].astype(BF), wci, preferred_element_type=F32))

            @pl.when(hf == 0)
            def _(y=y, sl=sl):
                y_ref[sl, :] = y + d_ref[...] * u_ref[sl, :]

            @pl.when(hf == 1)
            def _(y=y, sl=sl):
                y_ref[sl, :] += y

    u_spec, wb_spec, wc_spec, lam_spec, d_spec = _s5_specs(s, ncb)
    return pl.pallas_call(
        kern, name=name, out_shape=jax.ShapeDtypeStruct((s, w), F32), grid=(ncb, 2),
        in_specs=[u_spec, wb_spec, wb_spec, wc_spec, wc_spec, lam_spec, lam_spec, d_spec],
        out_specs=u_spec,
        scratch_shapes=[pltpu.VMEM((s, S5_STATES), F32), pltpu.VMEM((s, S5_STATES), F32)],
        compiler_params=_params(dimension_semantics=("parallel", "arbitrary")),
    )(proj, wb_re, wb_im, wc_re, wc_im, lam_re, lam_im, dskip)


def _s5_bwd(proj, dy, wb_re, wb_im, wc_re, wc_im, lam_re, lam_im, dskip, *, name, ride=None):
    s = proj.shape[0]
    w = dskip.shape[1]
    ncb = w // LANES
    rows = _fit(s, S5_ROWS, SUBLANES)
    tn_dims = (((0,), (0,)), ((), ()))
    nt_dims = (((1,), (1,)), ((), ()))
    ex, ex_arrays = ride if ride is not None else (None, [])

    def kern(*refs):
        own_in, own_out, comm, scratch = _ride_split(ex, refs, 9, 8)
        u_ref, dy_ref, wbr_ref, wbi_ref, wcr_ref, wci_ref, lr_ref, li_ref, d_ref = own_in
        du_ref, dwbr_ref, dwbi_ref, dwcr_ref, dwci_ref, dlr_ref, dli_ref, dd_ref = own_out
        xr_ref, xi_ref, gr_ref, gi_ref = scratch
        hf = pl.program_id(1)
        if ex is not None:
            @pl.when((pl.program_id(0) == 0) & (hf == 0))
            def _():
                ex.start(*comm)

        _s5_project_and_scan(u_ref, wbr_ref, wbi_ref, lr_ref, li_ref, xr_ref, xi_ref, s, rows)

        wcr, wci = wcr_ref[...], wci_ref[...]
        dwcr = jnp.zeros((S5_STATES, LANES), F32)
        dwci = jnp.zeros((S5_STATES, LANES), F32)
        ddsk = jnp.zeros((1, LANES), F32)
        for r in range(s // rows):
            sl = pl.ds(r * rows, rows)
            dyf = dy_ref[sl, :]
            dyb = dyf.astype(BF)
            gr_ref[sl, :] = lax.dot_general(dyb, wcr, nt_dims, preferred_element_type=F32)
            gi_ref[sl, :] = -lax.dot_general(dyb, wci, nt_dims, preferred_element_type=F32)
            dwcr = dwcr + lax.dot_general(xr_ref[sl, :].astype(BF), dyb, tn_dims, preferred_element_type=F32)
            dwci = dwci - lax.dot_general(xi_ref[sl, :].astype(BF), dyb, tn_dims, preferred_element_type=F32)
            ddsk = ddsk + jnp.sum(dyf * u_ref[sl, :], axis=0, keepdims=True)
        dwcr_ref[...] = dwcr
        dwci_ref[...] = dwci

        @pl.when(hf == 0)
        def _():
            dd_ref[...] = ddsk

        steps, cpow = _scan_tables(lr_ref[...], -li_ref[...], True)
        row = lax.broadcasted_iota(jnp.int32, (SUBLANES, S5_STATES), 0)
        nblk = s // SUBLANES

        unroll = _fit(nblk, SCAN_UNROLL, 1)

        def body(k, carry):
            car_r, car_i, ar, ai = carry
            sls = [pl.ds(pl.multiple_of((nblk - 1 - k * unroll - q) * SUBLANES, SUBLANES), SUBLANES)
                   for q in range(unroll)]
            blocks = [_scan_local(gr_ref[sl, :], gi_ref[sl, :], steps, True) for sl in sls]
            for sl, (g_r, g_i) in zip(sls, blocks):
                g_r, g_i = _scan_carry(g_r, g_i, car_r, car_i, cpow)
                gr_ref[sl, :] = g_r
                gi_ref[sl, :] = g_i
                nr = jnp.where(row == SUBLANES - 1, car_r, pltpu.roll(g_r, SUBLANES - 1, 0))
                ni = jnp.where(row == SUBLANES - 1, car_i, pltpu.roll(g_i, SUBLANES - 1, 0))
                xr, xi = xr_ref[sl, :], xi_ref[sl, :]
                ar = ar + (xr * nr + xi * ni)
                ai = ai + (xr * ni - xi * nr)
                car_r, car_i = g_r[0:1, :], g_i[0:1, :]
            return car_r, car_i, ar, ai

        zero = jnp.zeros((1, S5_STATES), F32)
        zacc = jnp.zeros((SUBLANES, S5_STATES), F32)
        _, _, ar, ai = lax.fori_loop(0, nblk // unroll, body, (zero, zero, zacc, zacc))
        dlr_ref[...] = jnp.sum(ar, axis=0, keepdims=True)
        dli_ref[...] = jnp.sum(ai, axis=0, keepdims=True)

        wbr, wbi = wbr_ref[...], wbi_ref[...]
        dwbr = jnp.zeros((LANES, S5_STATES), F32)
        dwbi = jnp.zeros((LANES, S5_STATES), F32)
        for r in range(s // rows):
            sl = pl.ds(r * rows, rows)
            grb, gib = gr_ref[sl, :].astype(BF), gi_ref[sl, :].astype(BF)
            ub = u_ref[sl, :].astype(BF)
            dwbr = dwbr + lax.dot_general(ub, grb, tn_dims, preferred_element_type=F32)
            dwbi = dwbi + lax.dot_general(ub, gib, tn_dims, preferred_element_type=F32)
            du = (lax.dot_general(grb, wbr, nt_dims, preferred_element_type=F32)
                  + lax.dot_general(gib, wbi, nt_dims, preferred_element_type=F32))

            @pl.when(hf == 0)
            def _(du=du, sl=sl):
                du_ref[sl, :] = du + d_ref[...] * dy_ref[sl, :]

            @pl.when(hf == 1)
            def _(du=du, sl=sl):
                du_ref[sl, :] += du
        dwbr_ref[...] = dwbr
        dwbi_ref[...] = dwbi
        if ex is not None:
            @pl.when((pl.program_id(0) == ncb - 1) & (hf == 1))
            def _():
                ex.wait(*comm)

    u_spec, wb_spec, wc_spec, lam_spec, d_spec = _s5_specs(s, ncb)
    dwb_spec = pl.BlockSpec((None, None, LANES, S5_STATES), lambda cb, hf: (cb, hf, 0, 0))
    dwc_spec = pl.BlockSpec((None, None, S5_STATES, LANES), lambda cb, hf: (cb, hf, 0, 0))
    state = pltpu.VMEM((s, S5_STATES), F32)
    return pl.pallas_call(
        kern, name=name,
        out_shape=[jax.ShapeDtypeStruct((s, w), F32),
                   jax.ShapeDtypeStruct((ncb, 2, LANES, S5_STATES), F32), jax.ShapeDtypeStruct((ncb, 2, LANES, S5_STATES), F32),
                   jax.ShapeDtypeStruct((ncb, 2, S5_STATES, LANES), F32), jax.ShapeDtypeStruct((ncb, 2, S5_STATES, LANES), F32),
                   jax.ShapeDtypeStruct((1, 4 * w), F32), jax.ShapeDtypeStruct((1, 4 * w), F32),
                   jax.ShapeDtypeStruct((1, w), F32)] + (ex.out_shape if ex else []),
        grid=(ncb, 2),
        in_specs=[u_spec, u_spec, wb_spec, wb_spec, wc_spec, wc_spec, lam_spec, lam_spec, d_spec]
        + (ex.specs if ex else []),
        out_specs=[u_spec, dwb_spec, dwb_spec, dwc_spec, dwc_spec, lam_spec, lam_spec, d_spec]
        + (ex.specs if ex else []),
        scratch_shapes=(ex.scratch if ex else []) + [state, state, state, state],
        compiler_params=_params(dimension_semantics=("arbitrary", "arbitrary"), has_side_effects=ex is not None),
    )(proj, dy, wb_re, wb_im, wc_re, wc_im, lam_re, lam_im, dskip, *ex_arrays)


def _s5_discretise(lam_re, lam_im, log_dt, b_re, b_im):
    lr = jnp.minimum(lam_re, -EIG_CLIP)
    li = lam_im
    dt = jnp.exp(log_dt)[:, None]
    mag = jnp.exp(lr * dt)
    lbr, lbi = mag * jnp.cos(li * dt), mag * jnp.sin(li * dt)
    den = lr * lr + li * li
    fr = ((lbr - 1.0) * lr + lbi * li) / den
    fi = (lbi * lr - (lbr - 1.0) * li) / den
    bbr = fr[:, None, :] * b_re - fi[:, None, :] * b_im
    bbi = fr[:, None, :] * b_im + fi[:, None, :] * b_re
    return lbr, lbi, bbr, bbi


def _s5_operand(mats, channels_first):
    g, a, b = mats.shape
    gl = LANES // 2 // SSM_H
    ncb = g // (2 * gl)
    m = mats.reshape(ncb, 2, gl, a, b)
    eye = jnp.eye(gl, dtype=mats.dtype)
    inner = (m[:, :, :, :, None, :] * eye[None, None, :, None, :, None]).reshape(ncb, 2, gl * a, gl * b)
    zeros = jnp.zeros_like(inner[:, 0])
    axis = 1 if channels_first else 2
    return jnp.stack([jnp.concatenate([inner[:, 0], zeros], axis=axis),
                      jnp.concatenate([zeros, inner[:, 1]], axis=axis)], axis=1)


def _s5_block_grads(dwb, a, b, transpose):
    ncb = dwb.shape[0]
    gl = LANES // 2 // (a if not transpose else b)
    if not transpose:
        d = dwb.reshape(ncb, 2, 2, gl, a, gl, b)
        parts = [[d[:, hf, hf, g, :, g, :] for g in range(gl)] for hf in range(2)]
    else:
        d = dwb.reshape(ncb, 2, gl, a, 2, gl, b)
        parts = [[d[:, hf, g, :, hf, g, :] for g in range(gl)] for hf in range(2)]
    st = jnp.stack([jnp.stack(p, axis=1) for p in parts], axis=1)
    return st.reshape(ncb * 2 * gl, a, b)


def _adamw(parts, w, m, v, *, name):
    depth, r, c = w.shape
    assert len(parts) == depth
    npart = parts[0].shape[0]
    row_bytes = 4 * (-(-c // LANES) * LANES)
    align = 16 if parts[0].dtype == BF else SUBLANES
    budget = VMEM_LIMIT // 2 // (2 * (depth * npart + 7) * row_bytes)
    tr = _fit(r, max(align, budget // align * align), align)
    nr = r // tr
    c1 = 1.0 / (1.0 - ADAM_B1 ** ADAM_STEP)
    c2 = 1.0 / (1.0 - ADAM_B2 ** ADAM_STEP)

    def kern(*refs):
        p_refs = refs[:depth]
        w_ref, m_ref, v_ref, g_ref, d_ref, nm_ref, nv_ref = refs[depth:]
        layer = pl.program_id(0)
        for l in range(depth):
            @pl.when(layer == l)
            def _(p_ref=p_refs[l]):
                g = p_ref[0].astype(F32)
                for q in range(1, npart):
                    g = g + p_ref[q].astype(F32)
                m2 = ADAM_B1 * m_ref[...] + (1.0 - ADAM_B1) * g
                v2 = ADAM_B2 * v_ref[...] + (1.0 - ADAM_B2) * (g * g)
                upd = (m2 * c1) / (jnp.sqrt(v2 * c2) + ADAM_EPS) + ADAM_WD * w_ref[...]
                g_ref[...] = g
                d_ref[...] = -ADAM_LR * upd
                nm_ref[...] = m2
                nv_ref[...] = v2

    def part_spec(l):
        return pl.BlockSpec((npart, tr, c),
                            lambda ly, i: (0, jnp.where(ly == l, i, jnp.where(ly < l, 0, nr - 1)), 0))

    spec = pl.BlockSpec((None, tr, c), lambda ly, i: (ly, i, 0))
    return pl.pallas_call(
        kern, name=name, out_shape=[jax.ShapeDtypeStruct((depth, r, c), F32)] * 4, grid=(depth, nr),
        in_specs=[part_spec(l) for l in range(depth)] + [spec, spec, spec],
        out_specs=[spec] * 4,
        compiler_params=_params(dimension_semantics=("arbitrary", "arbitrary")),
    )(*parts, w, m, v)


def _sum_parts(parts, *, name):
    npart, r, c = parts.shape

    def kern(p_ref, o_ref):
        g = p_ref[0]
        for q in range(1, npart):
            g = g + p_ref[q]
        o_ref[...] = g

    return pl.pallas_call(kern, name=name, out_shape=jax.ShapeDtypeStruct((r, c), F32), compiler_params=_params())(parts)


class _Exchange:
    def __init__(self, arrays, gather):
        self.n = len(arrays)
        self.gather = gather
        self.out_shape = [jax.ShapeDtypeStruct(((NDEV,) + a.shape) if gather else a.shape, a.dtype) for a in arrays]
        self.scratch = [pltpu.SemaphoreType.DMA((self.n, NDEV - 1)), pltpu.SemaphoreType.DMA((self.n, NDEV - 1)),
                        pltpu.SemaphoreType.DMA((self.n,))]
        self.specs = [pl.BlockSpec(memory_space=pl.ANY)] * self.n

    def _copies(self, srcs, dsts, sems):
        send_sems, recv_sems, local_sems = sems
        x, y, c = lax.axis_index("x"), lax.axis_index("y"), lax.axis_index("c")
        me = 4 * x + 2 * y + c
        local = [pltpu.make_async_copy(srcs[a] if self.gather else srcs[a].at[me], dsts[a].at[me], local_sems.at[a])
                 for a in range(self.n)]
        remote = []
        for k in (1, 2, 4, 3, 5, 6, 7):
            px, py, pc = x ^ ((k >> 2) & 1), y ^ ((k >> 1) & 1), c ^ (k & 1)
            peer = 4 * px + 2 * py + pc
            for a in range(self.n):
                src = srcs[a] if self.gather else srcs[a].at[peer]
                mk = functools.partial(
                    pltpu.make_async_remote_copy, src_ref=src,
                    send_sem=send_sems.at[a, k - 1], recv_sem=recv_sems.at[a, k - 1],
                    device_id=(px, py, pc), device_id_type=pl.DeviceIdType.MESH)
                remote.append((mk(dst_ref=dsts[a].at[me]), mk(dst_ref=dsts[a].at[peer])))
        return local, remote

    def _gather_copies(self, srcs, dsts, sems):
        send_sems, recv_sems, local_sems = sems
        x, y, c = lax.axis_index("x"), lax.axis_index("y"), lax.axis_index("c")
        block = lambda px, py, pc: 4 * px + 2 * py + pc
        me = block(x, y, c)
        chips = [(1 - x, y), (x, 1 - y), (1 - x, 1 - y)]
        local = [pltpu.make_async_copy(srcs[a], dsts[a].at[me], local_sems.at[a]) for a in range(self.n)]
        own, passed = [], []
        for a in range(self.n):
            def copy(k, blk, to, src=None, a=a):
                return pltpu.make_async_remote_copy(
                    src_ref=dsts[a].at[blk] if src is None else src, dst_ref=dsts[a].at[blk],
                    send_sem=send_sems.at[a, k], recv_sem=recv_sems.at[a, k],
                    device_id=to, device_id_type=pl.DeviceIdType.MESH)
            sib = (x, y, 1 - c)
            own.append((copy(0, me, sib, srcs[a]), copy(0, block(x, y, 1 - c), sib)))
            for j, (px, py) in enumerate(chips):
                own.append((copy(1 + j, me, (px, py, c), srcs[a]), copy(1 + j, block(px, py, c), (px, py, c))))
            for j, (px, py) in enumerate(chips):
                passed.append((copy(4 + j, block(px, py, c), sib), copy(4 + j, block(px, py, 1 - c), sib)))
        return local, own, passed

    def start(self, srcs, dsts, sems):
        if self.gather:
            local, own, _ = self._gather_copies(srcs, dsts, sems)
            for cp in local:
                cp.start()
            for send, _ in own:
                send.start()
            return
        local, remote = self._copies(srcs, dsts, sems)
        for cp in local:
            cp.start()
        for send, _ in remote:
            send.start()

    def forward(self, srcs, dsts, sems):
        if not self.gather:
            return
        _, own, passed = self._gather_copies(srcs, dsts, sems)
        for a in range(self.n):
            for j in range(3):
                own[4 * a + 1 + j][1].wait_recv()
                passed[3 * a + j][0].start()

    def wait(self, srcs, dsts, sems):
        if self.gather:
            local, own, passed = self._gather_copies(srcs, dsts, sems)
            for a in range(self.n):
                own[4 * a][1].wait_recv()
            for _, arrival in passed:
                arrival.wait_recv()
            for send, _ in own + passed:
                send.wait_send()
            for cp in local:
                cp.wait()
            return
        local, remote = self._copies(srcs, dsts, sems)
        for send, arrival in remote:
            send.wait_send()
            arrival.wait_recv()
        for cp in local:
            cp.wait()


def _exchange(arrays, gather, *, name):
    ex = _Exchange(arrays, gather)
    n = ex.n

    def kern(*refs):
        srcs, dsts, sems = refs[:n], refs[n:2 * n], refs[2 * n:]
        ex.start(srcs, dsts, sems)
        ex.forward(srcs, dsts, sems)
        ex.wait(srcs, dsts, sems)

    return pl.pallas_call(
        kern, name=name, out_shape=ex.out_shape, in_specs=ex.specs, out_specs=ex.specs, scratch_shapes=ex.scratch,
        compiler_params=pltpu.CompilerParams(has_side_effects=True),
    )(*arrays)


def _pack(arrays):
    flat = jnp.concatenate([a.reshape(-1).astype(F32) for a in arrays])
    pad = (-flat.shape[0]) % (SUBLANES * LANES)
    return jnp.pad(flat, (0, pad)).reshape(-1, LANES)


def _unpack(buf, like):
    flat = buf.reshape(-1)
    out, off = [], 0
    for a in like:
        sz = math.prod(a.shape)
        out.append(flat[off:off + sz].reshape(a.shape))
        off += sz
    return out


def _row(v):
    return v.reshape(1, -1)


def _layer_fwd(x, mod, p, l, ride=None, on_receive=None):
    s, d = x.shape
    sw = d // 2
    nh = d // LANES
    shift_m, scale_m, gate_m, shift_f, scale_f, gate_f = mod
    n = lambda tag: f"{tag}{l}"
    sv = {}

    h1, = _rowwise(lambda xv, g, sc, sh: (xv * _rms(xv) * g) * (1.0 + sc) + sh,
                   [x], [p['g_pre_mix'], scale_m, shift_m], [(d, BF, 'tile')], name=n("pre_mix"))
    proj_a = _mm(h1, p['w_in_a'], name=n("proj_a"))
    flog = _mm(h1, p['w_in_f'], name=n("proj_f"))
    gates = _mm(h1, p['w_in_g'], name=n("proj_g"))

    y_s5 = _s5_fwd(proj_a, p['wb_re'], p['wb_im'], p['wc_re'], p['wc_im'], p['lamb_re'], p['lamb_im'], p['d_skip'],
                   name=n("s5_fwd"))
    z, = _rowwise(_gelu, [y_s5], [], [(sw, BF, 'tile')], name=n("gelu"))
    tglu, ys = _mm_fused(z, [p['w_glu']], [y_s5], lambda tv, yv, b: (tv, _gelu(yv) * _sigmoid(tv + b)),
                         [F32, BF], rows=[p['b_glu']], name=n("glu_mm"))

    cumx = _cum_fwd(flog, p['b_f_row'], nh, name=n("cum_fwd"))
    ya, lse, *received = _attn_fwd(proj_a, cumx, sw, name=n("attn_fwd"), ride=ride)
    if on_receive is not None:
        on_receive(received)

    am = _mm(ys, p['w_pa'], name=n("pa_mm"))
    bm, merged = _mm_fused(ya, [p['w_pb']], [am, (gates, 0), (gates, 1)],
                           lambda b, a, ga, gb: (b, _sigmoid(ga) * a + _sigmoid(gb) * b), [F32, BF],
                           name=n("pb_mm"))
    ym = _mm(merged, p['w_o'], name=n("o_mm"))
    def post_mix_pre_ffn(xv, yv, g, gt, g2, sc, sh):
        x2v = xv + gt * (yv * _rms(yv) * g)
        return x2v, (x2v * _rms(x2v) * g2) * (1.0 + sc) + sh

    x2, h2 = _rowwise(post_mix_pre_ffn, [x, ym], [p['g_post_mix'], gate_m, p['g_pre_ffn'], scale_f, shift_f],
                      [(d, F32, 'tile'), (d, BF, 'tile')], name=n("post_mix_pre_ffn"))
    gt, up, act = _mm_fused(h2, [p['w_ffn_gate'], p['w_ffn_up']], [], lambda g, u: (g, u, _silu(g) * u),
                            [F32, F32, BF], tb=True, name=n("gate_up_mm"))
    yf = _mm(act, p['w_ffn_down'], name=n("down_mm"))
    x3, = _rowwise(lambda xv, yv, g, gt_: xv + gt_ * (yv * _rms(yv) * g),
                   [x2, yf], [p['g_post_ffn'], gate_f], [(d, F32, 'tile')], name=n("post_ffn"))

    sv.update(x=x, h1=h1, proj_a=proj_a, flog=flog, gates=gates, y_s5=y_s5, z=z, tglu=tglu, ys=ys, cumx=cumx,
              ya=ya, lse=lse, am=am, bm=bm, merged=merged, ym=ym, x2=x2, h2=h2, gt=gt, up=up,
              act=act, yf=yf)
    return x3, sv


def _layer_bwd(dx3, sv, mod, p, l, make_ride=None, on_receive=None, carried=None, defer_tail=False):
    x, x2 = sv['x'], sv['x2']
    s, d = x.shape
    sw = d // 2
    nh = d // LANES
    shift_m, scale_m, gate_m, shift_f, scale_f, gate_f = mod
    n = lambda tag: f"{tag}{l}"
    gw, gs = {}, {}

    def post_bwd(dxo, yv, g, gate):
        r = _rms(yv)
        nf = yv * r
        dn = dxo * gate * g
        return _norm_bwd(dn, nf, r), dxo * (nf * g), dxo * gate * nf

    def pre_bwd(dh, dres, xv, g, sc):
        r = _rms(xv)
        xh = xv * r
        n3 = xh * g
        dn3 = dh * (1.0 + sc)
        return dres + _norm_bwd(dn3 * g, xh, r), dh, dh * n3, dn3 * xh

    dyf, dgate_f, gs['g_post_ffn'] = _rowwise(
        post_bwd, [dx3, sv['yf']], [p['g_post_ffn'], gate_f],
        [(d, BF, 'tile'), (d, F32, 'sum'), (d, F32, 'sum')], name=n("post_ffn_bwd"))
    gw['w_ffn_down'] = _mm(sv['act'], dyf, ta=True, out_dtype=BF, tm=1408, name=n("down_bwd_w"))

    def swiglu_bwd(da, g, u):
        sg = _sigmoid(g)
        return da * u * (sg * (1.0 + g * (1.0 - sg))), da * (g * sg)

    dgt, dup = _mm_fused(dyf, [p['w_ffn_down']], [sv['gt'], sv['up']], swiglu_bwd, [BF, BF], tb=True,
                         name=n("down_bwd_x"))
    dh2 = _mm(dgt, p['w_ffn_gate'], tm=1024, second=(dup, p['w_ffn_up']), name=n("gate_up_bwd_x"))
    gw['w_ffn_gate'] = _mm(dgt, sv['h2'], ta=True, out_dtype=BF, tm=1408, name=n("gate_bwd_w"))
    gw['w_ffn_up'] = _mm(dup, sv['h2'], ta=True, out_dtype=BF, tm=1408, name=n("up_bwd_w"))
    def pre_ffn_post_mix_bwd(dh, dres, xv, yv, g, sc, g2, gate):
        dx2v, dsh, dsc, dg = pre_bwd(dh, dres, xv, g, sc)
        return (dx2v, dsh, dsc, dg) + post_bwd(dx2v, yv, g2, gate)

    dx2, dshift_f, dscale_f, gs['g_pre_ffn'], dym, dgate_m, gs['g_post_mix'] = _rowwise(
        pre_ffn_post_mix_bwd, [dh2, dx3, x2, sv['ym']], [p['g_pre_ffn'], scale_f, p['g_post_mix'], gate_m],
        [(d, F32, 'tile'), (d, F32, 'sum'), (d, F32, 'sum'), (d, F32, 'sum'),
         (d, BF, 'tile'), (d, F32, 'sum'), (d, F32, 'sum')], name=n("pre_ffn_post_mix_bwd"))
    gw['w_o'] = _mm(sv['merged'], dym, ta=True, out_dtype=BF, name=n("o_bwd_w"))

    def merge_bwd(dm, a, b, ga, gb):
        sa, sb = _sigmoid(ga), _sigmoid(gb)
        return dm * sa, dm * sb, dm * a * sa * (1.0 - sa), dm * b * sb * (1.0 - sb)

    da_, db_, dga, dgb = _mm_fused(dym, [p['w_o']], [sv['am'], sv['bm'], (sv['gates'], 0), (sv['gates'], 1)],
                                   merge_bwd, [BF] * 4, tb=True, name=n("o_bwd_x"))
    dys = _mm(da_, p['w_pa'], tb=True, name=n("pa_bwd_x"))
    gw['w_pa'] = _mm(sv['ys'], da_, ta=True, out_dtype=BF, name=n("pa_bwd_w"))
    dya = _mm(db_, p['w_pb'], tb=True, name=n("pb_bwd_x"))
    gw['w_pb'] = _mm(sv['ya'], db_, ta=True, out_dtype=BF, name=n("pb_bwd_w"))

    sent = list(gw)
    dq, dk, dv, dkc, dqc, *received = _attn_bwd(
        sv['proj_a'], dya, sv['ya'], sv['lse'], sv['cumx'], sw, name=n("attn_bwd"),
        ride=make_ride({k: gw[k] for k in sent}) if make_ride is not None else None)
    if on_receive is not None:
        on_receive(sent, received)
    dcum = jnp.stack([-dkc.reshape(nh, s), dqc.reshape(nh, s)])
    dflog, dbf = _cum_bwd(dcum, sv['flog'], p['b_f_col'], name=n("cum_bwd"))
    gs['b_f'] = dbf.reshape(nh)

    def glu_bwd(dy_, yv, tv, b):
        zv = _gelu(yv)
        sg = _sigmoid(tv + b)
        dt = dy_ * zv * sg * (1.0 - sg)
        return dt, dy_ * sg, dt

    dt, dz1, gs['b_glu'] = _rowwise(glu_bwd, [dys, sv['y_s5'], sv['tglu']], [p['b_glu']],
                                    [(sw, BF, 'tile'), (sw, F32, 'tile'), (sw, F32, 'sum')], name=n("glu_bwd"))
    dy_s5, = _mm_fused(dt, [p['w_glu']], [dz1, sv['y_s5']], lambda dz2, a, yv: ((a + dz2) * _gelu_grad(yv),),
                       [F32], tb=True, name=n("glu_bwd_x"))
    gw['w_glu'] = _mm(sv['z'], dt, ta=True, out_dtype=BF, name=n("glu_bwd_w"))
    du, dwbr, dwbi, dwcr, dwci, dlr, dli, gs['d_skip'], *received = _s5_bwd(
        sv['proj_a'], dy_s5, p['wb_re'], p['wb_im'], p['wc_re'], p['wc_im'], p['lamb_re'], p['lamb_im'], p['d_skip'],
        name=n("s5_bwd"), ride=make_ride(carried[1]) if carried else None)
    if carried:
        carried[0](list(carried[1]), received)
    g_ = sw // SSM_H
    pst = p['lamb_re'].shape[1] // g_
    gs['lamb_re'], gs['lamb_im'] = dlr.reshape(g_, pst), dli.reshape(g_, pst)
    gs['bbar_re'] = _s5_block_grads(dwbr, SSM_H, pst, False)
    gs['bbar_im'] = _s5_block_grads(dwbi, SSM_H, pst, False)
    gs['c_re'] = _s5_block_grads(dwcr, pst, SSM_H, True).transpose(0, 2, 1)
    gs['c_im'] = _s5_block_grads(dwci, pst, SSM_H, True).transpose(0, 2, 1)

    dproj = jnp.concatenate([du.astype(BF), dq.astype(BF), dk.astype(BF), dv.astype(BF), dflog, dga, dgb], axis=1)
    gw['w_in'] = _mm(sv['h1'], dproj, ta=True, out_dtype=BF, tn=1408, name=n("proj_bwd_w"))
    if make_ride is not None:
        gw = {k: g for k, g in gw.items() if k not in sent}
    if make_ride is not None and not defer_tail:
        dh1, received = _mm(dproj, p['w_in_all'], tb=True, tk=1408, name=n("proj_bwd_x"), ride=make_ride(gw))
        on_receive(list(gw), received)
        gw = {}
    else:
        dh1 = _mm(dproj, p['w_in_all'], tb=True, tk=1408, name=n("proj_bwd_x"))
    dx, dshift_m, dscale_m, gs['g_pre_mix'] = _rowwise(
        pre_bwd, [dh1, dx2, x], [p['g_pre_mix'], scale_m],
        [(d, F32, 'tile'), (d, F32, 'sum'), (d, F32, 'sum'), (d, F32, 'sum')], name=n("pre_mix_bwd"))
    dmod = [dshift_m, dscale_m, dgate_m, dshift_f, dscale_f, dgate_f]
    return dx, gw, dmod, gs


def _unshard(k, blocks):
    if k in COL_SHARDED:
        return blocks.transpose(1, 0, 2).reshape(blocks.shape[1], NDEV * blocks.shape[2])
    return blocks.reshape(NDEV * blocks.shape[1], blocks.shape[2])


def _to_slabs(k, g):
    if k == 'w_in':
        d = g.shape[0]
        nh = d // LANES
        g = jnp.concatenate([g[:, :2 * d + nh], g[:, 2 * d + LANES:]], axis=1)
    if k in COL_SHARDED:
        return g.reshape(g.shape[0], NDEV, g.shape[1] // NDEV).transpose(1, 0, 2)
    return g.reshape(NDEV, g.shape[0] // NDEV, g.shape[1])


def _prep_w_in(w_in):
    d = w_in.shape[0]
    nh = d // LANES
    fcol = 2 * d
    p = {}
    p['w_in_a'] = w_in[:, :fcol]
    p['w_in_f'] = jnp.pad(w_in[:, fcol:fcol + nh], ((0, 0), (0, LANES - nh)))
    p['w_in_g'] = w_in[:, fcol + nh:]
    p['w_in_all'] = jnp.concatenate([p['w_in_a'], p['w_in_f'], p['w_in_g']], axis=1)
    return p


def _prep_small(small):
    nh = small['b_f'].shape[0]
    p = {}
    for k in ('g_pre_mix', 'g_post_mix', 'g_pre_ffn', 'g_post_ffn', 'd_skip', 'b_glu'):
        p[k] = _row(small[k])
    p['b_f_row'] = jnp.pad(_row(small['b_f']), ((0, 0), (0, LANES - nh)))
    p['b_f_col'] = small['b_f'].reshape(nh, 1)
    lbr, lbi, bbr, bbi = _s5_discretise(small['lam_re'], small['lam_im'], small['log_dt'], small['b_re'], small['b_im'])
    p['lamb_re'], p['lamb_im'] = _row(lbr), _row(lbi)
    p['wb_re'] = _s5_operand(bbr, True).astype(BF)
    p['wb_im'] = _s5_operand(bbi, True).astype(BF)
    p['wc_re'] = _s5_operand(small['c_re'].transpose(0, 2, 1), False).astype(BF)
    p['wc_im'] = _s5_operand(small['c_im'].transpose(0, 2, 1), False).astype(BF)
    return p


def _local_step(x, target, mods, ps, small, hooks=None):
    depth = len(ps)
    s, d = x.shape
    hooks = hooks or {}
    saved = []
    h = x
    for l in range(depth):
        h, sv = _layer_fwd(h, mods[l], ps[l], l, ride=hooks['fwd_ride'](l) if hooks else None,
                           on_receive=functools.partial(hooks['fwd_recv'], l) if hooks else None)
        saved.append(sv)

    def loss_fn(yv, tv):
        e = yv - tv
        return e * (1.0 / d), jnp.sum(e * e, axis=1, keepdims=True) * (0.5 / d)

    dy, loss = _rowwise(loss_fn, [h, target], [], [(d, F32, 'tile'), (1, F32, 'sum')], name="loss")
    dmods, gss = [None] * depth, [None] * depth
    unsent = {}
    carried = None
    for l in range(depth - 1, -1, -1):
        def on_receive(names, results, l=l):
            hooks['bwd_recv']([(k, l) for k in names], results)

        dy, gw, dmods[l], gs = _layer_bwd(dy, saved[l], mods[l], ps[l], l,
                                          make_ride=hooks['bwd_ride'] if hooks else None,
                                          on_receive=on_receive if hooks else None,
                                          carried=carried, defer_tail=bool(hooks) and l > 0)
        if hooks and l > 0:
            carried = (on_receive, gw)
        else:
            unsent.update({(k, l): g for k, g in gw.items()})
        sm = small[l]
        _, vjp = jax.vjp(_s5_discretise, sm['lam_re'], sm['lam_im'], sm['log_dt'], sm['b_re'], sm['b_im'])
        gs['lam_re'], gs['lam_im'], gs['log_dt'], gs['b_re'], gs['b_im'] = vjp(
            (gs.pop('lamb_re'), gs.pop('lamb_im'), gs.pop('bbar_re'), gs.pop('bbar_im')))
        gss[l] = gs
    return loss, dy, unsent, dmods, gss


SMALL_LOCAL = ['g_pre_mix', 'g_post_mix', 'g_pre_ffn', 'g_post_ffn', 'lam_re', 'lam_im', 'log_dt', 'b_re', 'b_im',
               'c_re', 'c_im', 'd_skip', 'b_glu', 'b_f']


def kernel(x, c, w_ada, b_ada, g_pre_mix, g_post_mix, g_pre_ffn, g_post_ffn, w_in, lam_re, lam_im, log_dt, b_re, b_im, c_re, c_im, d_skip, w_glu, b_glu, b_f, w_pa, w_pb, w_o, w_ffn_gate, w_ffn_up, w_ffn_down, loss_target, m_w_ada, m_b_ada, m_g_pre_mix, m_g_post_mix, m_g_pre_ffn, m_g_post_ffn, m_w_in, m_lam_re, m_lam_im, m_log_dt, m_b_re, m_b_im, m_c_re, m_c_im, m_d_skip, m_w_glu, m_b_glu, m_b_f, m_w_pa, m_w_pb, m_w_o, m_w_ffn_gate, m_w_ffn_up, m_w_ffn_down, v_w_ada, v_b_ada, v_g_pre_mix, v_g_post_mix, v_g_pre_ffn, v_g_post_ffn, v_w_in, v_lam_re, v_lam_im, v_log_dt, v_b_re, v_b_im, v_c_re, v_c_im, v_d_skip, v_w_glu, v_b_glu, v_b_f, v_w_pa, v_w_pb, v_w_o, v_w_ffn_gate, v_w_ffn_up, v_w_ffn_down):
    args = dict(locals())
    view = lambda k, a: jnp.swapaxes(a, -1, -2) if k in TRANSPOSED else a
    W = {k: view(k, args[k]) for k in WEIGHTS}
    M = {k: view(k, args['m_' + k]) for k in WEIGHTS}
    V = {k: view(k, args['v_' + k]) for k in WEIGHTS}
    depth, d, ncol = w_ada.shape
    s = x.shape[1]
    me = 4 * lax.axis_index("x") + 2 * lax.axis_index("y") + lax.axis_index("c")

    first = ['w_in', 'w_glu']
    c_all, *first_blocks = _exchange([jnp.pad(c, ((0, SUBLANES - 1), (0, 0)))] + [W[k][0].astype(BF) for k in first],
                                     True, name="gather_first")
    c_all = c_all[:, 0, :]

    cond, = _rowwise(_silu, [c_all], [], [(d, F32, 'tile')], name="cond")
    mod_part = jnp.stack([_mm(cond, w_ada[l], name=f"ada_mm{l}") for l in range(depth)], axis=1)
    mod_recv, = _exchange([mod_part.reshape(NDEV, depth, 1, ncol)], False, name="scatter_mod")
    mod_cat = mod_recv.reshape(NDEV, depth, ncol).transpose(1, 0, 2).reshape(depth, NDEV * ncol)
    mod, = _rowwise(lambda a, b: a + b, [mod_cat, b_ada], [], [(NDEV * ncol, F32, 'tile')], name="mod_bias")
    mods = [[mod[l:l + 1, i * d:(i + 1) * d] for i in range(6)] for l in range(depth)]

    small = [{k: W[k][l] for k in SMALL_LOCAL} for l in range(depth)]
    ps = [_prep_small(small[l]) for l in range(depth)]
    rest = [k for k in BIG if k not in first]
    riding = [[(k, l) for k in rest] + [(k, l + 1) for k in first if l + 1 < depth] for l in range(depth)]

    def take_weights(keys, results):
        for (k, l), blocks in zip(keys, results):
            full = _unshard(k, blocks)
            ps[l].update(_prep_w_in(full) if k == 'w_in' else {k: full})

    take_weights([(k, 0) for k in first], first_blocks)

    def fwd_ride(l):
        blocks = [W[k][ll].astype(BF) for k, ll in riding[l]]
        return _Exchange(blocks, True), blocks

    grad_parts = {}

    def bwd_ride(grads):
        slabs = [_to_slabs(k, g) for k, g in grads.items()]
        return _Exchange(slabs, False), slabs

    hooks = dict(fwd_ride=fwd_ride, fwd_recv=lambda l, results: take_weights(riding[l], results),
                 bwd_ride=bwd_ride, bwd_recv=lambda keys, results: grad_parts.update(zip(keys, results)))

    loss, dx, unsent, dmods, gss = _local_step(x[0], loss_target[0], mods, ps, small, hooks)
    assert not unsent
    out = {}
    for k in BIG:
        out[k] = _adamw([grad_parts[(k, l)] for l in range(depth)], W[k], M[k], V[k], name=f"adamw_{k}")

    dmod_mine = jnp.stack([jnp.concatenate(dmods[l], axis=1)[0] for l in range(depth)])
    small_mine = [dmod_mine] + [jnp.stack([gss[l][k] for l in range(depth)]) for k in SMALL_LOCAL] + [loss]
    parts, = _exchange([_pack(small_mine)], True, name="gather_small")
    summed = _sum_parts(parts, name="sum_small")
    names = ['b_ada'] + SMALL_LOCAL
    *small_grads, loss = _unpack(summed, [W[k] for k in names] + [loss])
    loss = loss[0, 0]
    for k, g in zip(names, small_grads):
        shp = W[k].shape
        rows = lambda a: a.reshape(depth, -1, shp[-1])
        res = _adamw([rows(g)[l][None] for l in range(depth)], rows(W[k]), rows(M[k]), rows(V[k]), name=f"adamw_{k}")
        out[k] = [a.reshape(shp) for a in res]

    dmod_all = parts.reshape(NDEV, -1)[:, :depth * 6 * d].reshape(NDEV, depth, 6 * d)
    dmod_cols = lax.dynamic_slice_in_dim(dmod_all, me * ncol, ncol, axis=2)
    g_ada = [_mm(cond, dmod_cols[:, l], ta=True, precision=HI, name=f"ada_bwd{l}")[None] for l in range(depth)]
    out['w_ada'] = _adamw(g_ada, w_ada, m_w_ada, v_w_ada, name="adamw_w_ada")

    return (loss, dx[None], *[view(k, out[k][i]) for i in range(4) for k in WEIGHTS])
```

```python
import functools
import math

import jax
import jax.numpy as jnp
from jax import lax
from jax.experimental import pallas as pl
from jax.experimental.pallas import tpu as pltpu

F32 = jnp.float32
BF = jnp.bfloat16
NDEV = 8
LANES = 128
SUBLANES = 8
VMEM_LIMIT = 48 * 1024 * 1024

SSM_H = 16
HEAD_DIM = 64
RMS_EPS = 1e-6
EIG_CLIP = 1e-4
ADAM_LR = 0.001
ADAM_B1 = 0.9
ADAM_B2 = 0.999
ADAM_EPS = 1e-08
ADAM_WD = 0.01
ADAM_STEP = 10
NEG = -1e30
HI = lax.Precision.HIGHEST

WEIGHTS = ['w_ada', 'b_ada', 'g_pre_mix', 'g_post_mix', 'g_pre_ffn', 'g_post_ffn', 'w_in', 'lam_re', 'lam_im',
           'log_dt', 'b_re', 'b_im', 'c_re', 'c_im', 'd_skip', 'w_glu', 'b_glu', 'b_f', 'w_pa', 'w_pb', 'w_o',
           'w_ffn_gate', 'w_ffn_up', 'w_ffn_down']
TRANSPOSED = ['w_ffn_gate', 'w_ffn_up', 'b_re', 'b_im']
COL_SHARDED = ['w_in', 'w_pa', 'w_pb']
ROW_SHARDED = ['w_glu', 'w_o', 'w_ffn_down', 'w_ffn_gate', 'w_ffn_up']
BIG = COL_SHARDED + ROW_SHARDED
SMALL = ['b_ada', 'g_pre_mix', 'g_post_mix', 'g_pre_ffn', 'g_post_ffn', 'lam_re', 'lam_im', 'log_dt', 'b_re',
         'b_im', 'c_re', 'c_im', 'd_skip', 'b_glu', 'b_f']


def _fit(dim, target, align):
    if dim <= target:
        return dim
    t = (target // align) * align
    while t >= align:
        if dim % t == 0:
            return t
        t -= align
    return dim


def _params(**kw):
    return pltpu.CompilerParams(vmem_limit_bytes=VMEM_LIMIT, **kw)


def _mm(a, b, *, ta=False, tb=False, out_dtype=F32, tm=None, tn=512, tk=2048, precision=None, name, ride=None,
        second=None):
    m, k = (a.shape[1], a.shape[0]) if ta else a.shape
    n = b.shape[0] if tb else b.shape[1]
    assert (b.shape[1] if tb else b.shape[0]) == k
    tm = _fit(m, tm or (1024 if ta else 2048), LANES if ta else 16)
    tn = _fit(n, tn, LANES)
    tk = _fit(k, tk, LANES)
    nk = k // tk
    grid = (m // tm, n // tn, nk)
    dims = (((0 if ta else 1,), (1 if tb else 0,)), ((), ()))
    ex, ex_arrays = ride if ride is not None else (None, [])

    pairs = [(a, b)] + ([second] if second is not None else [])

    def kern(*refs):
        ab_refs, (o_ref,), comm, scratch = _ride_split(ex, refs, 2 * len(pairs), 1)
        step = (pl.program_id(0) * grid[1] + pl.program_id(1)) * grid[2] + pl.program_id(2)
        if ex is not None:
            @pl.when(step == 0)
            def _():
                ex.start(*comm)

            @pl.when(step == (grid[0] * grid[1] * grid[2]) // 2)
            def _():
                ex.forward(*comm)

        p = None
        for a_ref, b_ref in zip(ab_refs[::2], ab_refs[1::2]):
            av, bv = a_ref[...], b_ref[...]
            if precision is None:
                av, bv = av.astype(BF), bv.astype(BF)
            q = lax.dot_general(av, bv, dims, preferred_element_type=F32, precision=precision)
            p = q if p is None else p + q
        if nk == 1:
            o_ref[...] = p.astype(out_dtype)
        else:
            acc_ref, = scratch
            kk = pl.program_id(2)

            @pl.when(kk == 0)
            def _():
                acc_ref[...] = p

            @pl.when(kk > 0)
            def _():
                acc_ref[...] += p

            @pl.when(kk == nk - 1)
            def _():
                o_ref[...] = acc_ref[...].astype(out_dtype)

        if ex is not None:
            @pl.when(step == grid[0] * grid[1] * grid[2] - 1)
            def _():
                ex.wait(*comm)

    a_spec = pl.BlockSpec((tk, tm), lambda i, j, kk: (kk, i)) if ta else pl.BlockSpec((tm, tk), lambda i, j, kk: (i, kk))
    b_spec = pl.BlockSpec((tn, tk), lambda i, j, kk: (j, kk)) if tb else pl.BlockSpec((tk, tn), lambda i, j, kk: (kk, j))
    res = pl.pallas_call(
        kern, name=name,
        out_shape=[jax.ShapeDtypeStruct((m, n), out_dtype)] + (ex.out_shape if ex else []),
        grid=grid,
        in_specs=[a_spec, b_spec] * len(pairs) + (ex.specs if ex else []),
        out_specs=[pl.BlockSpec((tm, tn), lambda i, j, kk: (i, j))] + (ex.specs if ex else []),
        scratch_shapes=(ex.scratch if ex else []) + ([] if nk == 1 else [pltpu.VMEM((tm, tn), F32)]),
        compiler_params=_params(dimension_semantics=("arbitrary",) * 3 if ex else ("parallel", "parallel", "arbitrary"),
                                has_side_effects=ex is not None),
    )(*[x for pair in pairs for x in pair], *ex_arrays)
    return (res[0], res[1:]) if ex else res[0]


def _mm_fused(a, bs, extras, fn, out_dtypes, *, rows=(), tb=False, tm=2048, tn=256, name):
    m, k = a.shape
    n = bs[0].shape[0] if tb else bs[0].shape[1]
    tm = _fit(m, tm, 16)
    tn = _fit(n, tn, LANES)
    extras = [e if isinstance(e, tuple) else (e, 0) for e in extras]
    nb, ne, nr = len(bs), len(extras), len(rows)
    dims = (((1,), (1 if tb else 0,)), ((), ()))

    def kern(*refs):
        av = refs[0][...].astype(BF)
        prods = [lax.dot_general(av, r[...].astype(BF), dims, preferred_element_type=F32) for r in refs[1:1 + nb]]
        res = fn(*prods, *[r[...] for r in refs[1 + nb:1 + nb + ne + nr]])
        for o_ref, r, dt in zip(refs[1 + nb + ne + nr:], res, out_dtypes):
            o_ref[...] = r.astype(dt)

    tile = pl.BlockSpec((tm, tn), lambda i, j: (i, j))
    b_spec = pl.BlockSpec((tn, k), lambda i, j: (j, 0)) if tb else pl.BlockSpec((k, tn), lambda i, j: (0, j))
    return pl.pallas_call(
        kern, name=name, out_shape=[jax.ShapeDtypeStruct((m, n), dt) for dt in out_dtypes],
        grid=(m // tm, n // tn),
        in_specs=[pl.BlockSpec((tm, k), lambda i, j: (i, 0))] + [b_spec] * nb
        + [pl.BlockSpec((tm, tn), lambda i, j, c=c: (i, j + c * (n // tn))) for _, c in extras]
        + [pl.BlockSpec((1, tn), lambda i, j: (0, j))] * nr,
        out_specs=[tile] * len(out_dtypes),
        compiler_params=_params(dimension_semantics=("parallel", "parallel")),
    )(a, *bs, *[e for e, _ in extras], *rows)


def _rowwise(fn, tiles, params, outs, *, tr=256, name):
    tiles = [t if isinstance(t, tuple) else (t, t.shape[1], 0) for t in tiles]
    s = tiles[0][0].shape[0]
    tr = _fit(s, tr, 16)
    nt, npar = len(tiles), len(params)

    def kern(*refs):
        i = pl.program_id(0)
        res = fn(*[r[...] for r in refs[:nt + npar]])
        if not isinstance(res, (tuple, list)):
            res = (res,)
        for (w, dt, kind), o_ref, r in zip(outs, refs[nt + npar:], res):
            if kind == 'tile':
                o_ref[...] = r.astype(dt)
            else:
                part = jnp.sum(r.astype(F32), axis=0, keepdims=True)

                @pl.when(i == 0)
                def _(o_ref=o_ref, part=part):
                    o_ref[...] = part

                @pl.when(i > 0)
                def _(o_ref=o_ref, part=part):
                    o_ref[...] += part

    def tile_spec(w, cb):
        return pl.BlockSpec((tr, w), lambda i: (i, cb))

    in_specs = [tile_spec(w, cb) for _, w, cb in tiles]
    in_specs += [pl.BlockSpec(p.shape, lambda i, nd=p.ndim: (0,) * nd) for p in params]
    out_shape, out_specs = [], []
    for w, dt, kind in outs:
        if kind == 'tile':
            out_shape.append(jax.ShapeDtypeStruct((s, w), dt))
            out_specs.append(pl.BlockSpec((tr, w), lambda i: (i, 0)))
        else:
            out_shape.append(jax.ShapeDtypeStruct((1, w), F32))
            out_specs.append(pl.BlockSpec((1, w), lambda i: (0, 0)))
    res = pl.pallas_call(
        kern, name=name, out_shape=out_shape, grid=(s // tr,), in_specs=in_specs, out_specs=out_specs,
        compiler_params=_params(dimension_semantics=("arbitrary",)),
    )(*[t[0] for t in tiles], *params)
    return res


def _sigmoid(z):
    return 1.0 / (1.0 + jnp.exp(-z))


def _silu(z):
    return z * _sigmoid(z)


_GELU_K = math.sqrt(2.0 / math.pi)


def _gelu(y):
    return 0.5 * y * (1.0 + jnp.tanh(_GELU_K * (y + 0.044715 * y * y * y)))


def _gelu_grad(y):
    th = jnp.tanh(_GELU_K * (y + 0.044715 * y * y * y))
    return 0.5 * (1.0 + th) + 0.5 * y * (1.0 - th * th) * _GELU_K * (1.0 + 3.0 * 0.044715 * y * y)


def _rms(x):
    return lax.rsqrt(jnp.mean(x * x, axis=-1, keepdims=True) + RMS_EPS)


def _norm_bwd(dn, xhat, r):
    return r * (dn - xhat * jnp.mean(dn * xhat, axis=-1, keepdims=True))


def _cum_fwd(flog, bf_row, nh, *, name):
    s = flog.shape[0]
    w = nh * HEAD_DIM
    t = _fit(s, 256, SUBLANES)

    def kern(f_ref, b_ref, o_ref, carry_ref):
        i = pl.program_id(0)

        @pl.when(i == 0)
        def _():
            carry_ref[...] = jnp.zeros_like(carry_ref)

        z = f_ref[...] + b_ref[...]
        logf = jnp.minimum(z, 0.0) - jnp.log(1.0 + jnp.exp(-jnp.abs(z)))
        hh = lax.broadcasted_iota(jnp.int32, (LANES, w), 0)
        cc = lax.broadcasted_iota(jnp.int32, (LANES, w), 1)
        expand = (cc // HEAD_DIM == hh).astype(F32)
        lx = jnp.dot(logf, expand, preferred_element_type=F32, precision=HI)
        rr = lax.broadcasted_iota(jnp.int32, (t, t), 0)
        kk = lax.broadcasted_iota(jnp.int32, (t, t), 1)
        tri = (kk <= rr).astype(F32)
        cum = jnp.dot(tri, lx, preferred_element_type=F32, precision=HI) + carry_ref[...]
        o_ref[...] = cum
        carry_ref[...] = cum[t - 1:t, :]

    return pl.pallas_call(
        kern, name=name, out_shape=jax.ShapeDtypeStruct((s, w), F32), grid=(s // t,),
        in_specs=[pl.BlockSpec((t, LANES), lambda i: (i, 0)), pl.BlockSpec((1, LANES), lambda i: (0, 0))],
        out_specs=pl.BlockSpec((t, w), lambda i: (i, 0)),
        scratch_shapes=[pltpu.VMEM((1, w), F32)],
        compiler_params=_params(dimension_semantics=("arbitrary",)),
    )(flog, bf_row)


def _cum_bwd(dcrow, flog, bf_col, *, name):
    _, nh, s = dcrow.shape
    t = _fit(s, 512, LANES)
    nb = s // t

    def kern(d_ref, f_ref, b_ref, df_ref, db_ref):
        rr = lax.broadcasted_iota(jnp.int32, (t, t), 0)
        kk = lax.broadcasted_iota(jnp.int32, (t, t), 1)
        upper = (rr >= kk).astype(F32)
        pick = (lax.broadcasted_iota(jnp.int32, (nh, LANES), 0)
                == lax.broadcasted_iota(jnp.int32, (nh, LANES), 1)).astype(F32)
        carry = jnp.zeros((nh, 1), F32)
        db = jnp.zeros((nh, 1), F32)
        for blk in range(nb - 1, -1, -1):
            sl = slice(blk * t, (blk + 1) * t)
            rc = jnp.dot(d_ref[0, :, sl] + d_ref[1, :, sl], upper, preferred_element_type=F32, precision=HI) + carry
            carry = rc[:, 0:1]
            frow = lax.dot_general(pick, f_ref[sl, :], (((1,), (1,)), ((), ())), preferred_element_type=F32,
                                   precision=HI)
            df = rc * _sigmoid(-(frow + b_ref[...]))
            df_ref[sl, :] = lax.dot_general(df, pick, (((0,), (0,)), ((), ())), preferred_element_type=F32,
                                            precision=HI).astype(BF)
            db = db + jnp.sum(df, axis=1, keepdims=True)
        db_ref[...] = db

    return pl.pallas_call(
        kern, name=name,
        out_shape=[jax.ShapeDtypeStruct((s, LANES), BF), jax.ShapeDtypeStruct((nh, 1), F32)],
        compiler_params=_params(),
    )(dcrow, flog, bf_col)


def _ride_split(ex, refs, n_in, n_out):
    n = ex.n if ex is not None else 0
    own_in, srcs = refs[:n_in], refs[n_in:n_in + n]
    own_out, dsts = refs[n_in + n:n_in + n + n_out], refs[n_in + n + n_out:n_in + 2 * n + n_out]
    sems = refs[n_in + 2 * n + n_out:n_in + 2 * n + n_out + 3] if n else ()
    rest = refs[n_in + 2 * n + n_out + (3 if n else 0):]
    return own_in, own_out, (srcs, dsts, sems), rest


ATTN_TILE = 512
ATTN_TILE_BWD = 256
ATTN_STRIP = 32
BIAS_LANES = 3


def _head_masks(rows):
    lane = lax.broadcasted_iota(jnp.int32, (rows, LANES), 1)
    return [(lane >= HEAD_DIM * e) & (lane < HEAD_DIM * (e + 1)) for e in range(2)]


def _augment(feat, bias, e, *, bias_slot, ones_slot):
    rows = feat.shape[0]
    lane = lax.broadcasted_iota(jnp.int32, (rows, LANES), 1)
    own = (lane >= HEAD_DIM * e) & (lane < HEAD_DIM * (e + 1))
    off = lane - HEAD_DIM * (1 - e)
    out = jnp.where(own, feat, 0.0)
    if ones_slot is not None:
        out = jnp.where((off >= ones_slot * BIAS_LANES) & (off < (ones_slot + 1) * BIAS_LANES), 1.0, out)
    if bias is not None:
        rest = pltpu.roll(bias, HEAD_DIM, 1)
        for term in range(BIAS_LANES):
            part = rest.astype(BF).astype(F32)
            out = jnp.where(off == bias_slot * BIAS_LANES + term, part, out)
            rest = rest - part
    return out.astype(BF)


def _two_slot_pipeline(m, scores, tile):
    scores(0, 0)

    def pair(n, carry):
        k = 2 * n
        scores(k + 1, 1)
        tile(k, 0, False)
        scores(k + 2, 0)
        tile(k + 1, 1, False)
        return carry

    lax.fori_loop(0, m // 2, pair, 0)

    @pl.when(m % 2 == 0)
    def _():
        tile(m, 0, True)

    @pl.when(m % 2 == 1)
    def _():
        scores(m, 1)
        tile(m - 1, 0, False)
        tile(m, 1, True)


def _attn_fwd(proj, cumx, qcol, *, name, ride=None):
    s = proj.shape[0]
    w = cumx.shape[1]
    nhp = w // LANES
    t = _fit(s, ATTN_TILE, LANES)
    nq = s // t
    strip = _fit(t, ATTN_STRIP, 16)
    scale = HEAD_DIM ** -0.5
    qb, kb, vb = qcol // LANES, (qcol + w) // LANES, (qcol + 2 * w) // LANES
    ex, ex_arrays = ride if ride is not None else (None, [])
    nt_dims = (((1,), (1,)), ((), ()))

    def kern(*refs):
        own_in, (o_ref, l_ref), comm, scratch = _ride_split(ex, refs, 5, 2)
        q_ref, k_ref, v_ref, cxq_ref, cxk_ref = own_in
        ka_ref, vat_ref, s0_ref, s1_ref, p_ref, m_ref, acc_ref = scratch
        s_refs = (s0_ref, s1_ref)
        i = pl.program_id(1)
        if ex is not None:
            @pl.when((pl.program_id(0) == 0) & (i == 0))
            def _():
                ex.start(*comm)

            @pl.when((pl.program_id(0) == nhp - 1) & (i == 0))
            def _():
                ex.forward(*comm)

        msks = _head_masks(t)

        @pl.when(i == 0)
        def _():
            def build(c, carry):
                rows = pl.ds(pl.multiple_of(c * t, LANES), t)
                k2, v2, cx = k_ref[rows, :], v_ref[rows, :], cxk_ref[rows, :]
                for e in range(2):
                    ka_ref[e, rows, :] = _augment(k2, -cx, e, bias_slot=1, ones_slot=0)
                    vat_ref[e, :, rows] = jnp.where(msks[e], v2, 1.0).T.astype(BF)
                return carry
            lax.fori_loop(0, nq, build, 0)

        q2 = q_ref[...] * scale
        qa = [_augment(q2, cxq_ref[...], e, bias_slot=0, ones_slot=1) for e in range(2)]
        m_ref[...] = jnp.full(m_ref.shape, NEG, F32)
        acc_ref[...] = jnp.zeros(acc_ref.shape, F32)
        slabs = strip // SUBLANES

        def scores(j, slot):
            rows_k = pl.ds(pl.multiple_of(j * t, LANES), t)
            for e in range(2):
                st = lax.dot_general(ka_ref[e, rows_k, :], qa[e], nt_dims, preferred_element_type=F32)
                s_refs[slot][e] = st.reshape(t // SUBLANES, SUBLANES, t)

        def tile(j, slot, diagonal):
            rows_k = pl.ds(pl.multiple_of(j * t, LANES), t)
            s_ref = s_refs[slot]
            for e in range(2):
                mx = jnp.full((SUBLANES, t), NEG, F32)
                for r in range(t // strip):
                    sl = slice(r * slabs, (r + 1) * slabs)
                    sv = s_ref[e,sl]
                    if diagonal:
                        shape = (slabs, SUBLANES, t)
                        key = (r * strip + lax.broadcasted_iota(jnp.int32, shape, 0) * SUBLANES
                               + lax.broadcasted_iota(jnp.int32, shape, 1))
                        sv = jnp.where(key <= lax.broadcasted_iota(jnp.int32, shape, 2), sv, NEG)
                        s_ref[e,sl] = sv
                    mx = jnp.maximum(mx, jnp.max(sv, axis=0))
                for sh in (4, 2, 1):
                    mx = jnp.maximum(mx, pltpu.roll(mx, sh, 0))
                m_old = m_ref[e]
                m_new = jnp.maximum(m_old, mx)
                alpha = jnp.exp(m_old - m_new)
                m_ref[e] = m_new
                for r in range(t // strip):
                    p = jnp.exp(s_ref[e,r * slabs:(r + 1) * slabs] - m_new[None])
                    p_ref[e, r * strip:(r + 1) * strip, :] = p.reshape(strip, t).astype(BF)
                acc = acc_ref[e].reshape(LANES // SUBLANES, SUBLANES, t) * alpha[None]
                acc_ref[e] = acc.reshape(LANES, t) + jnp.dot(vat_ref[e, :, rows_k], p_ref[e],
                                                             preferred_element_type=F32)

        _two_slot_pipeline(i, scores, tile)

        outs, lses = [], []
        for e in range(2):
            acc = acc_ref[e]
            other = HEAD_DIM * (1 - e)
            den = acc[other:other + 1, :]
            outs.append(acc / den)
            lses.append(jnp.broadcast_to(m_ref[e][0:1, :] + jnp.log(den), (LANES, t)))
        upper = lax.broadcasted_iota(jnp.int32, (LANES, t), 0) < HEAD_DIM
        o_ref[...] = jnp.where(upper, outs[0], outs[1]).T.astype(BF)
        l_ref[...] = jnp.where(upper, lses[0], lses[1]).T
        if ex is not None:
            @pl.when((pl.program_id(0) == nhp - 1) & (i == nq - 1))
            def _():
                ex.wait(*comm)

    own_scratch = [pltpu.VMEM((2, s, LANES), BF), pltpu.VMEM((2, LANES, s), BF),
                   pltpu.VMEM((2, t // SUBLANES, SUBLANES, t), F32),
                   pltpu.VMEM((2, t // SUBLANES, SUBLANES, t), F32), pltpu.VMEM((2, t, t), BF),
                   pltpu.VMEM((2, SUBLANES, t), F32), pltpu.VMEM((2, LANES, t), F32)]
    return pl.pallas_call(
        kern, name=name,
        out_shape=[jax.ShapeDtypeStruct((s, w), BF), jax.ShapeDtypeStruct((nhp, s, LANES), F32)]
        + (ex.out_shape if ex else []),
        grid=(nhp, nq),
        in_specs=[pl.BlockSpec((t, LANES), lambda h, i: (i, qb + h)),
                  pl.BlockSpec((s, LANES), lambda h, i: (0, kb + h)),
                  pl.BlockSpec((s, LANES), lambda h, i: (0, vb + h)),
                  pl.BlockSpec((t, LANES), lambda h, i: (i, h)),
                  pl.BlockSpec((s, LANES), lambda h, i: (0, h))] + (ex.specs if ex else []),
        out_specs=[pl.BlockSpec((t, LANES), lambda h, i: (i, h)),
                   pl.BlockSpec((None, t, LANES), lambda h, i: (h, i, 0))] + (ex.specs if ex else []),
        scratch_shapes=(ex.scratch if ex else []) + own_scratch,
        compiler_params=_params(dimension_semantics=("arbitrary", "arbitrary"),
                                has_side_effects=ex is not None),
    )(proj, proj, proj, cumx, cumx, *ex_arrays)


def _attn_bwd(proj, do, o, lse, cumx, qcol, *, name, ride=None):
    s = proj.shape[0]
    w = cumx.shape[1]
    nhp = w // LANES
    t = _fit(s, ATTN_TILE_BWD, LANES)
    nq = s // t
    strip = _fit(t, ATTN_STRIP, 16)
    scale = HEAD_DIM ** -0.5
    qb, kb, vb = qcol // LANES, (qcol + w) // LANES, (qcol + 2 * w) // LANES
    tn_dims = (((0,), (0,)), ((), ()))
    nt_dims = (((1,), (1,)), ((), ()))
    ex, ex_arrays = ride if ride is not None else (None, [])

    def kern(*refs):
        own_in, own_out, comm, scratch = _ride_split(ex, refs, 7, 5)
        q_ref, k_ref, v_ref, do_ref, o_ref, l_ref, cx_ref = own_in
        dq_ref, dk_ref, dv_ref, dkc_ref, dqc_ref = own_out
        qa_ref, da_ref, dqa_ref, dka_ref, dva_ref, st0_ref, st1_ref, dpt0_ref, dpt1_ref, pt_ref, dst_ref = scratch
        st_refs, dpt_refs = (st0_ref, st1_ref), (dpt0_ref, dpt1_ref)
        j = pl.program_id(1)
        if ex is not None:
            @pl.when((pl.program_id(0) == 0) & (j == 0))
            def _():
                ex.start(*comm)

        msks = _head_masks(t)

        @pl.when(j == 0)
        def _():
            def build(c, carry):
                rows = pl.ds(pl.multiple_of(c * t, LANES), t)
                q2 = q_ref[rows, :] * scale
                do2 = do_ref[rows, :]
                dd = do2 * o_ref[rows, :].astype(F32)
                delta = jnp.where(msks[0], jnp.sum(jnp.where(msks[0], dd, 0.0), axis=1, keepdims=True),
                                  jnp.sum(jnp.where(msks[1], dd, 0.0), axis=1, keepdims=True))
                bias = cx_ref[rows, :] - l_ref[rows, :]
                for e in range(2):
                    qa_ref[e, rows, :] = _augment(q2, bias, e, bias_slot=0, ones_slot=1)
                    da_ref[e, rows, :] = _augment(do2, -delta, e, bias_slot=0, ones_slot=None)
                return carry
            lax.fori_loop(0, nq, build, 0)
            dqa_ref[...] = jnp.zeros(dqa_ref.shape, F32)

        rows_k = pl.ds(pl.multiple_of(j * t, LANES), t)
        k2, v2 = k_ref[...], v_ref[...]
        ka = [_augment(k2, -cx_ref[rows_k, :], e, bias_slot=1, ones_slot=0) for e in range(2)]
        va = [_augment(v2, None, e, bias_slot=None, ones_slot=0) for e in range(2)]
        dka_ref[...] = jnp.zeros(dka_ref.shape, F32)
        dva_ref[...] = jnp.zeros(dva_ref.shape, F32)

        def scores(k, slot):
            rows_q = pl.ds(pl.multiple_of((nq - 1 - k) * t, LANES), t)
            for e in range(2):
                st_refs[slot][e] = lax.dot_general(ka[e], qa_ref[e, rows_q, :], nt_dims,
                                                   preferred_element_type=F32)
                dpt_refs[slot][e] = lax.dot_general(va[e], da_ref[e, rows_q, :], nt_dims,
                                                    preferred_element_type=F32)

        def tile(k, slot, diagonal):
            rows_q = pl.ds(pl.multiple_of((nq - 1 - k) * t, LANES), t)
            st_ref, dpt_ref = st_refs[slot], dpt_refs[slot]
            for e in range(2):
                for r in range(t // strip):
                    rows = slice(r * strip, (r + 1) * strip)
                    sv = st_ref[e, rows, :]
                    if diagonal:
                        key = r * strip + lax.broadcasted_iota(jnp.int32, (strip, t), 0)
                        qry = lax.broadcasted_iota(jnp.int32, (strip, t), 1)
                        sv = jnp.where(key <= qry, sv, NEG)
                    p = jnp.exp(sv)
                    pt_ref[e, rows, :] = p.astype(BF)
                    dst_ref[e, rows, :] = (p * dpt_ref[e, rows, :]).astype(BF)
            for e in range(2):
                dva_ref[e] += jnp.dot(pt_ref[e], da_ref[e, rows_q, :], preferred_element_type=F32)
                dka_ref[e] += jnp.dot(dst_ref[e], qa_ref[e, rows_q, :], preferred_element_type=F32)
                dqa_ref[e, rows_q, :] += lax.dot_general(dst_ref[e], ka[e], tn_dims, preferred_element_type=F32)

        _two_slot_pipeline(nq - 1 - j, scores, tile)

        dk_ref[...] = jnp.where(msks[0], dka_ref[0], dka_ref[1])
        dv_ref[...] = jnp.where(msks[0], dva_ref[0], dva_ref[1])
        sums = jnp.where(msks[1], dka_ref[0], dka_ref[1]).T
        dkc_ref[0:1, :] = sums[HEAD_DIM + BIAS_LANES:HEAD_DIM + BIAS_LANES + 1, :]
        dkc_ref[1:2, :] = sums[BIAS_LANES:BIAS_LANES + 1, :]

        @pl.when(j == nq - 1)
        def _():
            def flush(c, carry):
                rows = pl.ds(pl.multiple_of(c * t, LANES), t)
                a0, a1 = dqa_ref[0, rows, :], dqa_ref[1, rows, :]
                dq_ref[rows, :] = jnp.where(msks[0], a0, a1) * scale
                sums = jnp.where(msks[1], a0, a1).T
                dqc_ref[0:1, rows] = sums[HEAD_DIM:HEAD_DIM + 1, :]
                dqc_ref[1:2, rows] = sums[0:1, :]
                return carry
            lax.fori_loop(0, nq, flush, 0)

        if ex is not None:
            @pl.when((pl.program_id(0) == nhp - 1) & (j == nq - 1))
            def _():
                ex.wait(*comm)

    full = lambda cb: pl.BlockSpec((s, LANES), lambda h, j: (0, cb + h))
    blk = lambda cb: pl.BlockSpec((t, LANES), lambda h, j: (j, cb + h))
    own_scratch = [pltpu.VMEM((2, s, LANES), BF), pltpu.VMEM((2, s, LANES), BF), pltpu.VMEM((2, s, LANES), F32),
                   pltpu.VMEM((2, t, LANES), F32), pltpu.VMEM((2, t, LANES), F32),
                   pltpu.VMEM((2, t, t), F32), pltpu.VMEM((2, t, t), F32),
                   pltpu.VMEM((2, t, t), F32), pltpu.VMEM((2, t, t), F32),
                   pltpu.VMEM((2, t, t), BF), pltpu.VMEM((2, t, t), BF)]
    return pl.pallas_call(
        kern, name=name,
        out_shape=[jax.ShapeDtypeStruct((s, w), F32)] * 3 + [jax.ShapeDtypeStruct((nhp, 2, s), F32)] * 2
        + (ex.out_shape if ex else []),
        grid=(nhp, nq),
        in_specs=[full(qb), blk(kb), blk(vb), full(0), full(0),
                  pl.BlockSpec((None, s, LANES), lambda h, j: (h, 0, 0)), full(0)] + (ex.specs if ex else []),
        out_specs=[full(0), blk(0), blk(0), pl.BlockSpec((None, 2, t), lambda h, j: (h, 0, j)),
                   pl.BlockSpec((None, 2, s), lambda h, j: (h, 0, 0))] + (ex.specs if ex else []),
        scratch_shapes=(ex.scratch if ex else []) + own_scratch,
        compiler_params=_params(dimension_semantics=("arbitrary", "arbitrary"),
                                has_side_effects=ex is not None),
    )(proj, proj, proj, do, o, lse, cumx, *ex_arrays)


S5_STATES = 256
S5_ROWS = 512


def _cmul(ar, ai, br, bi):
    return ar * br - ai * bi, ar * bi + ai * br


def _scan_tables(lr, li, reverse):
    w = lr.shape[1]
    row = lax.broadcasted_iota(jnp.int32, (SUBLANES, w), 0)
    if reverse:
        row = SUBLANES - 1 - row
    lr1, li1 = jnp.broadcast_to(lr, (SUBLANES, w)), jnp.broadcast_to(li, (SUBLANES, w))
    lr2, li2 = _cmul(lr1, li1, lr1, li1)
    lr4, li4 = _cmul(lr2, li2, lr2, li2)
    steps = []
    for d, (pr, pi) in zip((1, 2, 4), ((lr1, li1), (lr2, li2), (lr4, li4))):
        keep = row >= d
        steps.append((jnp.where(keep, pr, 0.0), jnp.where(keep, pi, 0.0)))
    cr, ci = lr1, li1
    for bit, (pr, pi) in zip((1, 2, 4), ((lr1, li1), (lr2, li2), (lr4, li4))):
        nr, ni = _cmul(cr, ci, pr, pi)
        has = (row & bit) != 0
        cr, ci = jnp.where(has, nr, cr), jnp.where(has, ni, ci)
    return steps, (cr, ci)


def _scan_local(xr, xi, steps, reverse):
    for d, (pr, pi) in zip((1, 2, 4), steps):
        sh = (SUBLANES - d) if reverse else d
        sr, si = pltpu.roll(xr, sh, 0), pltpu.roll(xi, sh, 0)
        xr, xi = xr + (pr * sr - pi * si), xi + (pr * si + pi * sr)
    return xr, xi


def _scan_carry(xr, xi, car_r, car_i, carry_pow):
    cr, ci = carry_pow
    return xr + (cr * car_r - ci * car_i), xi + (cr * car_i + ci * car_r)


SCAN_UNROLL = 4


def _s5_specs(s, ncb):
    u_spec = pl.BlockSpec((s, LANES), lambda cb, hf: (0, cb))
    wb_spec = pl.BlockSpec((None, None, LANES, S5_STATES), lambda cb, hf: (cb, hf, 0, 0))
    wc_spec = pl.BlockSpec((None, None, S5_STATES, LANES), lambda cb, hf: (cb, hf, 0, 0))
    lam_spec = pl.BlockSpec((1, S5_STATES), lambda cb, hf: (0, 2 * cb + hf))
    d_spec = pl.BlockSpec((1, LANES), lambda cb, hf: (0, cb))
    return u_spec, wb_spec, wc_spec, lam_spec, d_spec


def _s5_project_and_scan(u_ref, wbr_ref, wbi_ref, lr_ref, li_ref, xr_ref, xi_ref, s, rows):
    wbr, wbi = wbr_ref[...], wbi_ref[...]
    for r in range(s // rows):
        sl = pl.ds(r * rows, rows)
        ub = u_ref[sl, :].astype(BF)
        xr_ref[sl, :] = jnp.dot(ub, wbr, preferred_element_type=F32)
        xi_ref[sl, :] = jnp.dot(ub, wbi, preferred_element_type=F32)
    steps, cpow = _scan_tables(lr_ref[...], li_ref[...], False)

    unroll = _fit(s // SUBLANES, SCAN_UNROLL, 1)

    def body(b, carry):
        car_r, car_i = carry
        sls = [pl.ds(pl.multiple_of((b * unroll + q) * SUBLANES, SUBLANES), SUBLANES) for q in range(unroll)]
        blocks = [_scan_local(xr_ref[sl, :], xi_ref[sl, :], steps, False) for sl in sls]
        for sl, (xr, xi) in zip(sls, blocks):
            xr, xi = _scan_carry(xr, xi, car_r, car_i, cpow)
            xr_ref[sl, :] = xr
            xi_ref[sl, :] = xi
            car_r, car_i = xr[SUBLANES - 1:SUBLANES, :], xi[SUBLANES - 1:SUBLANES, :]
        return car_r, car_i

    zero = jnp.zeros((1, S5_STATES), F32)
    lax.fori_loop(0, s // SUBLANES // unroll, body, (zero, zero))


def _s5_fwd(proj, wb_re, wb_im, wc_re, wc_im, lam_re, lam_im, dskip, *, name, ride=None):
    s = proj.shape[0]
    w = dskip.shape[1]
    ncb = w // LANES
    rows = _fit(s, S5_ROWS, SUBLANES)
    ex, ex_arrays = ride if ride is not None else (None, [])

    def kern(*refs):
        own_in, (y_ref,), comm, (xr_ref, xi_ref) = _ride_split(ex, refs, 8, 1)
        u_ref, wbr_ref, wbi_ref, wcr_ref, wci_ref, lr_ref, li_ref, d_ref = own_in
        hf = pl.program_id(1)
        if ex is not None:
            @pl.when((pl.program_id(0) == 0) & (hf == 0))
            def _():
                ex.start(*comm)

            @pl.when((pl.program_id(0) == ncb // 2) & (hf == 0))
            def _():
                ex.forward(*comm)

        _s5_project_and_scan(u_ref, wbr_ref, wbi_ref, lr_ref, li_ref, xr_ref, xi_ref, s, rows)
        wcr, wci = wcr_ref[...], wci_ref[...]
        for r in range(s // rows):
            sl = pl.ds(r * rows, rows)
            y = (jnp.dot(xr_ref[sl, :].astype(BF), wcr, preferred_element_type=F32)
                 - jnp.dot(xi_ref[sl, :].astype(BF), wci, preferred_element_type=F32))

            @pl.when(hf == 0)
            def _(y=y, sl=sl):
                y_ref[sl, :] = y + d_ref[...] * u_ref[sl, :]

            @pl.when(hf == 1)
            def _(y=y, sl=sl):
                y_ref[sl, :] += y

        if ex is not None:
            @pl.when((pl.program_id(0) == ncb - 1) & (hf == 1))
            def _():
                ex.wait(*comm)

    u_spec, wb_spec, wc_spec, lam_spec, d_spec = _s5_specs(s, ncb)
    res = pl.pallas_call(
        kern, name=name, out_shape=[jax.ShapeDtypeStruct((s, w), F32)] + (ex.out_shape if ex else []),
        grid=(ncb, 2),
        in_specs=[u_spec, wb_spec, wb_spec, wc_spec, wc_spec, lam_spec, lam_spec, d_spec] + (ex.specs if ex else []),
        out_specs=[u_spec] + (ex.specs if ex else []),
        scratch_shapes=(ex.scratch if ex else []) + [pltpu.VMEM((s, S5_STATES), F32), pltpu.VMEM((s, S5_STATES), F32)],
        compiler_params=_params(dimension_semantics=("arbitrary", "arbitrary"), has_side_effects=ex is not None),
    )(proj, wb_re, wb_im, wc_re, wc_im, lam_re, lam_im, dskip, *ex_arrays)
    return res[0], res[1:]


def _s5_bwd(proj, dy, wb_re, wb_im, wc_re, wc_im, lam_re, lam_im, dskip, *, name, ride=None):
    s = proj.shape[0]
    w = dskip.shape[1]
    ncb = w // LANES
    rows = _fit(s, S5_ROWS, SUBLANES)
    tn_dims = (((0,), (0,)), ((), ()))
    nt_dims = (((1,), (1,)), ((), ()))
    ex, ex_arrays = ride if ride is not None else (None, [])

    def kern(*refs):
        own_in, own_out, comm, scratch = _ride_split(ex, refs, 9, 8)
        u_ref, dy_ref, wbr_ref, wbi_ref, wcr_ref, wci_ref, lr_ref, li_ref, d_ref = own_in
        du_ref, dwbr_ref, dwbi_ref, dwcr_ref, dwci_ref, dlr_ref, dli_ref, dd_ref = own_out
        xr_ref, xi_ref, gr_ref, gi_ref = scratch
        hf = pl.program_id(1)
        if ex is not None:
            @pl.when((pl.program_id(0) == 0) & (hf == 0))
            def _():
                ex.start(*comm)

        _s5_project_and_scan(u_ref, wbr_ref, wbi_ref, lr_ref, li_ref, xr_ref, xi_ref, s, rows)

        wcr, wci = wcr_ref[...], wci_ref[...]
        dwcr = jnp.zeros((S5_STATES, LANES), F32)
        dwci = jnp.zeros((S5_STATES, LANES), F32)
        ddsk = jnp.zeros((1, LANES), F32)
        for r in range(s // rows):
            sl = pl.ds(r * rows, rows)
            dyf = dy_ref[sl, :]
            dyb = dyf.astype(BF)
            gr_ref[sl, :] = lax.dot_general(dyb, wcr, nt_dims, preferred_element_type=F32)
            gi_ref[sl, :] = -lax.dot_general(dyb, wci, nt_dims, preferred_element_type=F32)
            dwcr = dwcr + lax.dot_general(xr_ref[sl, :].astype(BF), dyb, tn_dims, preferred_element_type=F32)
            dwci = dwci - lax.dot_general(xi_ref[sl, :].astype(BF), dyb, tn_dims, preferred_element_type=F32)
            ddsk = ddsk + jnp.sum(dyf * u_ref[sl, :], axis=0, keepdims=True)
        dwcr_ref[...] = dwcr
        dwci_ref[...] = dwci

        @pl.when(hf == 0)
        def _():
            dd_ref[...] = ddsk

        steps, cpow = _scan_tables(lr_ref[...], -li_ref[...], True)
        row = lax.broadcasted_iota(jnp.int32, (SUBLANES, S5_STATES), 0)
        nblk = s // SUBLANES

        unroll = _fit(nblk, SCAN_UNROLL, 1)

        def body(k, carry):
            car_r, car_i, ar, ai = carry
            sls = [pl.ds(pl.multiple_of((nblk - 1 - k * unroll - q) * SUBLANES, SUBLANES), SUBLANES)
                   for q in range(unroll)]
            blocks = [_scan_local(gr_ref[sl, :], gi_ref[sl, :], steps, True) for sl in sls]
            for sl, (g_r, g_i) in zip(sls, blocks):
                g_r, g_i = _scan_carry(g_r, g_i, car_r, car_i, cpow)
                gr_ref[sl, :] = g_r
                gi_ref[sl, :] = g_i
                nr = jnp.where(row == SUBLANES - 1, car_r, pltpu.roll(g_r, SUBLANES - 1, 0))
                ni = jnp.where(row == SUBLANES - 1, car_i, pltpu.roll(g_i, SUBLANES - 1, 0))
                xr, xi = xr_ref[sl, :], xi_ref[sl, :]
                ar = ar + (xr * nr + xi * ni)
                ai = ai + (xr * ni - xi * nr)
                car_r, car_i = g_r[0:1, :], g_i[0:1, :]
            return car_r, car_i, ar, ai

        zero = jnp.zeros((1, S5_STATES), F32)
        zacc = jnp.zeros((SUBLANES, S5_STATES), F32)
        _, _, ar, ai = lax.fori_loop(0, nblk // unroll, body, (zero, zero, zacc, zacc))
        dlr_ref[...] = jnp.sum(ar, axis=0, keepdims=True)
        dli_ref[...] = jnp.sum(ai, axis=0, keepdims=True)

        wbr, wbi = wbr_ref[...], wbi_ref[...]
        dwbr = jnp.zeros((LANES, S5_STATES), F32)
        dwbi = jnp.zeros((LANES, S5_STATES), F32)
        for r in range(s // rows):
            sl = pl.ds(r * rows, rows)
            grb, gib = gr_ref[sl, :].astype(BF), gi_ref[sl, :].astype(BF)
            ub = u_ref[sl, :].astype(BF)
            dwbr = dwbr + lax.dot_general(ub, grb, tn_dims, preferred_element_type=F32)
            dwbi = dwbi + lax.dot_general(ub, gib, tn_dims, preferred_element_type=F32)
            du = (lax.dot_general(grb, wbr, nt_dims, preferred_element_type=F32)
                  + lax.dot_general(gib, wbi, nt_dims, preferred_element_type=F32))

            @pl.when(hf == 0)
            def _(du=du, sl=sl):
                du_ref[sl, :] = du + d_ref[...] * dy_ref[sl, :]

            @pl.when(hf == 1)
            def _(du=du, sl=sl):
                du_ref[sl, :] += du
        dwbr_ref[...] = dwbr
        dwbi_ref[...] = dwbi
        if ex is not None:
            @pl.when((pl.program_id(0) == ncb - 1) & (hf == 1))
            def _():
                ex.wait(*comm)

    u_spec, wb_spec, wc_spec, lam_spec, d_spec = _s5_specs(s, ncb)
    dwb_spec = pl.BlockSpec((None, None, LANES, S5_STATES), lambda cb, hf: (cb, hf, 0, 0))
    dwc_spec = pl.BlockSpec((None, None, S5_STATES, LANES), lambda cb, hf: (cb, hf, 0, 0))
    state = pltpu.VMEM((s, S5_STATES), F32)
    return pl.pallas_call(
        kern, name=name,
        out_shape=[jax.ShapeDtypeStruct((s, w), F32),
                   jax.ShapeDtypeStruct((ncb, 2, LANES, S5_STATES), F32), jax.ShapeDtypeStruct((ncb, 2, LANES, S5_STATES), F32),
                   jax.ShapeDtypeStruct((ncb, 2, S5_STATES, LANES), F32), jax.ShapeDtypeStruct((ncb, 2, S5_STATES, LANES), F32),
                   jax.ShapeDtypeStruct((1, 4 * w), F32), jax.ShapeDtypeStruct((1, 4 * w), F32),
                   jax.ShapeDtypeStruct((1, w), F32)] + (ex.out_shape if ex else []),
        grid=(ncb, 2),
        in_specs=[u_spec, u_spec, wb_spec, wb_spec, wc_spec, wc_spec, lam_spec, lam_spec, d_spec]
        + (ex.specs if ex else []),
        out_specs=[u_spec, dwb_spec, dwb_spec, dwc_spec, dwc_spec, lam_spec, lam_spec, d_spec]
        + (ex.specs if ex else []),
        scratch_shapes=(ex.scratch if ex else []) + [state, state, state, state],
        compiler_params=_params(dimension_semantics=("arbitrary", "arbitrary"), has_side_effects=ex is not None),
    )(proj, dy, wb_re, wb_im, wc_re, wc_im, lam_re, lam_im, dskip, *ex_arrays)


def _s5_discretise(lam_re, lam_im, log_dt, b_re, b_im):
    lr = jnp.minimum(lam_re, -EIG_CLIP)
    li = lam_im
    dt = jnp.exp(log_dt)[:, None]
    mag = jnp.exp(lr * dt)
    lbr, lbi = mag * jnp.cos(li * dt), mag * jnp.sin(li * dt)
    den = lr * lr + li * li
    fr = ((lbr - 1.0) * lr + lbi * li) / den
    fi = (lbi * lr - (lbr - 1.0) * li) / den
    bbr = fr[:, None, :] * b_re - fi[:, None, :] * b_im
    bbi = fr[:, None, :] * b_im + fi[:, None, :] * b_re
    return lbr, lbi, bbr, bbi


def _s5_operand(mats, channels_first):
    g, a, b = mats.shape
    gl = LANES // 2 // SSM_H
    ncb = g // (2 * gl)
    m = mats.reshape(ncb, 2, gl, a, b)
    eye = jnp.eye(gl, dtype=mats.dtype)
    inner = (m[:, :, :, :, None, :] * eye[None, None, :, None, :, None]).reshape(ncb, 2, gl * a, gl * b)
    zeros = jnp.zeros_like(inner[:, 0])
    axis = 1 if channels_first else 2
    return jnp.stack([jnp.concatenate([inner[:, 0], zeros], axis=axis),
                      jnp.concatenate([zeros, inner[:, 1]], axis=axis)], axis=1)


def _s5_block_grads(dwb, a, b, transpose):
    ncb = dwb.shape[0]
    gl = LANES // 2 // (a if not transpose else b)
    if not transpose:
        d = dwb.reshape(ncb, 2, 2, gl, a, gl, b)
        parts = [[d[:, hf, hf, g, :, g, :] for g in range(gl)] for hf in range(2)]
    else:
        d = dwb.reshape(ncb, 2, gl, a, 2, gl, b)
        parts = [[d[:, hf, g, :, hf, g, :] for g in range(gl)] for hf in range(2)]
    st = jnp.stack([jnp.stack(p, axis=1) for p in parts], axis=1)
    return st.reshape(ncb * 2 * gl, a, b)


def _adamw(parts, w, m, v, *, name):
    depth, r, c = w.shape
    assert len(parts) == depth
    npart = parts[0].shape[0]
    row_bytes = 4 * (-(-c // LANES) * LANES)
    align = 16 if parts[0].dtype == BF else SUBLANES
    budget = VMEM_LIMIT // 2 // (2 * (depth * npart + 7) * row_bytes)
    tr = _fit(r, max(align, budget // align * align), align)
    nr = r // tr
    c1 = 1.0 / (1.0 - ADAM_B1 ** ADAM_STEP)
    c2 = 1.0 / (1.0 - ADAM_B2 ** ADAM_STEP)

    def kern(*refs):
        p_refs = refs[:depth]
        w_ref, m_ref, v_ref, g_ref, d_ref, nm_ref, nv_ref = refs[depth:]
        layer = pl.program_id(0)
        for l in range(depth):
            @pl.when(layer == l)
            def _(p_ref=p_refs[l]):
                g = p_ref[0].astype(F32)
                for q in range(1, npart):
                    g = g + p_ref[q].astype(F32)
                m2 = ADAM_B1 * m_ref[...] + (1.0 - ADAM_B1) * g
                v2 = ADAM_B2 * v_ref[...] + (1.0 - ADAM_B2) * (g * g)
                upd = (m2 * c1) / (jnp.sqrt(v2 * c2) + ADAM_EPS) + ADAM_WD * w_ref[...]
                g_ref[...] = g
                d_ref[...] = -ADAM_LR * upd
                nm_ref[...] = m2
                nv_ref[...] = v2

    def part_spec(l):
        return pl.BlockSpec((npart, tr, c),
                            lambda ly, i: (0, jnp.where(ly == l, i, jnp.where(ly < l, 0, nr - 1)), 0))

    spec = pl.BlockSpec((None, tr, c), lambda ly, i: (ly, i, 0))
    return pl.pallas_call(
        kern, name=name, out_shape=[jax.ShapeDtypeStruct((depth, r, c), F32)] * 4, grid=(depth, nr),
        in_specs=[part_spec(l) for l in range(depth)] + [spec, spec, spec],
        out_specs=[spec] * 4,
        compiler_params=_params(dimension_semantics=("arbitrary", "arbitrary")),
    )(*parts, w, m, v)


def _sum_parts(parts, *, name):
    npart, r, c = parts.shape

    def kern(p_ref, o_ref):
        g = p_ref[0]
        for q in range(1, npart):
            g = g + p_ref[q]
        o_ref[...] = g

    return pl.pallas_call(kern, name=name, out_shape=jax.ShapeDtypeStruct((r, c), F32), compiler_params=_params())(parts)


class _Exchange:
    def __init__(self, arrays, gather):
        self.n = len(arrays)
        self.gather = gather
        self.out_shape = [jax.ShapeDtypeStruct(((NDEV,) + a.shape) if gather else a.shape, a.dtype) for a in arrays]
        self.scratch = [pltpu.SemaphoreType.DMA((self.n, NDEV - 1)), pltpu.SemaphoreType.DMA((self.n, NDEV - 1)),
                        pltpu.SemaphoreType.DMA((self.n,))]
        self.specs = [pl.BlockSpec(memory_space=pl.ANY)] * self.n

    def _copies(self, srcs, dsts, sems):
        send_sems, recv_sems, local_sems = sems
        x, y, c = lax.axis_index("x"), lax.axis_index("y"), lax.axis_index("c")
        me = 4 * x + 2 * y + c
        local = [pltpu.make_async_copy(srcs[a] if self.gather else srcs[a].at[me], dsts[a].at[me], local_sems.at[a])
                 for a in range(self.n)]
        remote = []
        for k in (1, 2, 4, 3, 5, 6, 7):
            px, py, pc = x ^ ((k >> 2) & 1), y ^ ((k >> 1) & 1), c ^ (k & 1)
            peer = 4 * px + 2 * py + pc
            for a in range(self.n):
                src = srcs[a] if self.gather else srcs[a].at[peer]
                mk = functools.partial(
                    pltpu.make_async_remote_copy, src_ref=src,
                    send_sem=send_sems.at[a, k - 1], recv_sem=recv_sems.at[a, k - 1],
                    device_id=(px, py, pc), device_id_type=pl.DeviceIdType.MESH)
                remote.append((mk(dst_ref=dsts[a].at[me]), mk(dst_ref=dsts[a].at[peer])))
        return local, remote

    def _gather_copies(self, srcs, dsts, sems):
        send_sems, recv_sems, local_sems = sems
        x, y, c = lax.axis_index("x"), lax.axis_index("y"), lax.axis_index("c")
        block = lambda px, py, pc: 4 * px + 2 * py + pc
        me = block(x, y, c)
        chips = [(1 - x, y), (x, 1 - y), (1 - x, 1 - y)]
        local = [pltpu.make_async_copy(srcs[a], dsts[a].at[me], local_sems.at[a]) for a in range(self.n)]
        own, passed = [], []
        for a in range(self.n):
            def copy(k, blk, to, src=None, a=a):
                return pltpu.make_async_remote_copy(
                    src_ref=dsts[a].at[blk] if src is None else src, dst_ref=dsts[a].at[blk],
                    send_sem=send_sems.at[a, k], recv_sem=recv_sems.at[a, k],
                    device_id=to, device_id_type=pl.DeviceIdType.MESH)
            sib = (x, y, 1 - c)
            own.append((copy(0, me, sib, srcs[a]), copy(0, block(x, y, 1 - c), sib)))
            for j, (px, py) in enumerate(chips):
                own.append((copy(1 + j, me, (px, py, c), srcs[a]), copy(1 + j, block(px, py, c), (px, py, c))))
            for j, (px, py) in enumerate(chips):
                passed.append((copy(4 + j, block(px, py, c), sib), copy(4 + j, block(px, py, 1 - c), sib)))
        return local, own, passed

    def start(self, srcs, dsts, sems):
        if self.gather:
            local, own, _ = self._gather_copies(srcs, dsts, sems)
            for cp in local:
                cp.start()
            for send, _ in own:
                send.start()
            return
        local, remote = self._copies(srcs, dsts, sems)
        for cp in local:
            cp.start()
        for send, _ in remote:
            send.start()

    def forward(self, srcs, dsts, sems):
        if not self.gather:
            return
        _, own, passed = self._gather_copies(srcs, dsts, sems)
        for a in range(self.n):
            for j in range(3):
                own[4 * a + 1 + j][1].wait_recv()
                passed[3 * a + j][0].start()

    def wait(self, srcs, dsts, sems):
        if self.gather:
            local, own, passed = self._gather_copies(srcs, dsts, sems)
            for a in range(self.n):
                own[4 * a][1].wait_recv()
            for _, arrival in passed:
                arrival.wait_recv()
            for send, _ in own + passed:
                send.wait_send()
            for cp in local:
                cp.wait()
            return
        local, remote = self._copies(srcs, dsts, sems)
        for send, arrival in remote:
            send.wait_send()
            arrival.wait_recv()
        for cp in local:
            cp.wait()


def _exchange(arrays, gather, *, name):
    ex = _Exchange(arrays, gather)
    n = ex.n

    def kern(*refs):
        srcs, dsts, sems = refs[:n], refs[n:2 * n], refs[2 * n:]
        ex.start(srcs, dsts, sems)
        ex.forward(srcs, dsts, sems)
        ex.wait(srcs, dsts, sems)

    return pl.pallas_call(
        kern, name=name, out_shape=ex.out_shape, in_specs=ex.specs, out_specs=ex.specs, scratch_shapes=ex.scratch,
        compiler_params=pltpu.CompilerParams(has_side_effects=True),
    )(*arrays)


def _pack(arrays):
    flat = jnp.concatenate([a.reshape(-1).astype(F32) for a in arrays])
    pad = (-flat.shape[0]) % (SUBLANES * LANES)
    return jnp.pad(flat, (0, pad)).reshape(-1, LANES)


def _unpack(buf, like):
    flat = buf.reshape(-1)
    out, off = [], 0
    for a in like:
        sz = math.prod(a.shape)
        out.append(flat[off:off + sz].reshape(a.shape))
        off += sz
    return out


def _row(v):
    return v.reshape(1, -1)


def _layer_fwd(x, mod, p, l, ride=None, on_receive=None):
    s, d = x.shape
    sw = d // 2
    nh = d // LANES
    shift_m, scale_m, gate_m, shift_f, scale_f, gate_f = mod
    n = lambda tag: f"{tag}{l}"
    sv = {}

    h1, = _rowwise(lambda xv, g, sc, sh: (xv * _rms(xv) * g) * (1.0 + sc) + sh,
                   [x], [p['g_pre_mix'], scale_m, shift_m], [(d, BF, 'tile')], name=n("pre_mix"))
    proj_a = _mm(h1, p['w_in_a'], name=n("proj_a"))
    flog = _mm(h1, p['w_in_f'], name=n("proj_f"))
    gates = _mm(h1, p['w_in_g'], name=n("proj_g"))

    y_s5, received = _s5_fwd(proj_a, p['wb_re'], p['wb_im'], p['wc_re'], p['wc_im'], p['lamb_re'], p['lamb_im'],
                             p['d_skip'], name=n("s5_fwd"), ride=ride('s5') if ride else None)
    if on_receive is not None:
        on_receive('s5', received)
    z, = _rowwise(_gelu, [y_s5], [], [(sw, BF, 'tile')], name=n("gelu"))
    tglu, ys = _mm_fused(z, [p['w_glu']], [y_s5], lambda tv, yv, b: (tv, _gelu(yv) * _sigmoid(tv + b)),
                         [F32, BF], rows=[p['b_glu']], name=n("glu_mm"))

    cumx = _cum_fwd(flog, p['b_f_row'], nh, name=n("cum_fwd"))
    ya, lse, *received = _attn_fwd(proj_a, cumx, sw, name=n("attn_fwd"), ride=ride('attn') if ride else None)
    if on_receive is not None:
        on_receive('attn', received)

    am = _mm(ys, p['w_pa'], name=n("pa_mm"))
    bm, merged = _mm_fused(ya, [p['w_pb']], [am, (gates, 0), (gates, 1)],
                           lambda b, a, ga, gb: (b, _sigmoid(ga) * a + _sigmoid(gb) * b), [F32, BF],
                           name=n("pb_mm"))
    ym = _mm(merged, p['w_o'], name=n("o_mm"))
    def post_mix_pre_ffn(xv, yv, g, gt, g2, sc, sh):
        x2v = xv + gt * (yv * _rms(yv) * g)
        return x2v, (x2v * _rms(x2v) * g2) * (1.0 + sc) + sh

    x2, h2 = _rowwise(post_mix_pre_ffn, [x, ym], [p['g_post_mix'], gate_m, p['g_pre_ffn'], scale_f, shift_f],
                      [(d, F32, 'tile'), (d, BF, 'tile')], name=n("post_mix_pre_ffn"))
    gt, up, act = _mm_fused(h2, [p['w_ffn_gate'], p['w_ffn_up']], [], lambda g, u: (g, u, _silu(g) * u),
                            [F32, F32, BF], tb=True, name=n("gate_up_mm"))
    yf = _mm(act, p['w_ffn_down'], name=n("down_mm"))
    x3, = _rowwise(lambda xv, yv, g, gt_: xv + gt_ * (yv * _rms(yv) * g),
                   [x2, yf], [p['g_post_ffn'], gate_f], [(d, F32, 'tile')], name=n("post_ffn"))

    sv.update(x=x, h1=h1, proj_a=proj_a, flog=flog, gates=gates, y_s5=y_s5, z=z, tglu=tglu, ys=ys, cumx=cumx,
              ya=ya, lse=lse, am=am, bm=bm, merged=merged, ym=ym, x2=x2, h2=h2, gt=gt, up=up,
              act=act, yf=yf)
    return x3, sv


def _layer_bwd(dx3, sv, mod, p, l, make_ride=None, on_receive=None, carried=None, defer_tail=False):
    x, x2 = sv['x'], sv['x2']
    s, d = x.shape
    sw = d // 2
    nh = d // LANES
    shift_m, scale_m, gate_m, shift_f, scale_f, gate_f = mod
    n = lambda tag: f"{tag}{l}"
    gw, gs = {}, {}

    def post_bwd(dxo, yv, g, gate):
        r = _rms(yv)
        nf = yv * r
        dn = dxo * gate * g
        return _norm_bwd(dn, nf, r), dxo * (nf * g), dxo * gate * nf

    def pre_bwd(dh, dres, xv, g, sc):
        r = _rms(xv)
        xh = xv * r
        n3 = xh * g
        dn3 = dh * (1.0 + sc)
        return dres + _norm_bwd(dn3 * g, xh, r), dh, dh * n3, dn3 * xh

    dyf, dgate_f, gs['g_post_ffn'] = _rowwise(
        post_bwd, [dx3, sv['yf']], [p['g_post_ffn'], gate_f],
        [(d, BF, 'tile'), (d, F32, 'sum'), (d, F32, 'sum')], name=n("post_ffn_bwd"))
    gw['w_ffn_down'] = _mm(sv['act'], dyf, ta=True, out_dtype=BF, tm=1408, name=n("down_bwd_w"))

    def swiglu_bwd(da, g, u):
        sg = _sigmoid(g)
        return da * u * (sg * (1.0 + g * (1.0 - sg))), da * (g * sg)

    dgt, dup = _mm_fused(dyf, [p['w_ffn_down']], [sv['gt'], sv['up']], swiglu_bwd, [BF, BF], tb=True,
                         name=n("down_bwd_x"))
    dh2 = _mm(dgt, p['w_ffn_gate'], tm=1024, second=(dup, p['w_ffn_up']), name=n("gate_up_bwd_x"))
    gw['w_ffn_gate'] = _mm(dgt, sv['h2'], ta=True, out_dtype=BF, tm=1408, name=n("gate_bwd_w"))
    gw['w_ffn_up'] = _mm(dup, sv['h2'], ta=True, out_dtype=BF, tm=1408, name=n("up_bwd_w"))
    def pre_ffn_post_mix_bwd(dh, dres, xv, yv, g, sc, g2, gate):
        dx2v, dsh, dsc, dg = pre_bwd(dh, dres, xv, g, sc)
        return (dx2v, dsh, dsc, dg) + post_bwd(dx2v, yv, g2, gate)

    dx2, dshift_f, dscale_f, gs['g_pre_ffn'], dym, dgate_m, gs['g_post_mix'] = _rowwise(
        pre_ffn_post_mix_bwd, [dh2, dx3, x2, sv['ym']], [p['g_pre_ffn'], scale_f, p['g_post_mix'], gate_m],
        [(d, F32, 'tile'), (d, F32, 'sum'), (d, F32, 'sum'), (d, F32, 'sum'),
         (d, BF, 'tile'), (d, F32, 'sum'), (d, F32, 'sum')], name=n("pre_ffn_post_mix_bwd"))
    gw['w_o'] = _mm(sv['merged'], dym, ta=True, out_dtype=BF, name=n("o_bwd_w"))

    def merge_bwd(dm, a, b, ga, gb):
        sa, sb = _sigmoid(ga), _sigmoid(gb)
        return dm * sa, dm * sb, dm * a * sa * (1.0 - sa), dm * b * sb * (1.0 - sb)

    da_, db_, dga, dgb = _mm_fused(dym, [p['w_o']], [sv['am'], sv['bm'], (sv['gates'], 0), (sv['gates'], 1)],
                                   merge_bwd, [BF] * 4, tb=True, name=n("o_bwd_x"))
    dys = _mm(da_, p['w_pa'], tb=True, name=n("pa_bwd_x"))
    gw['w_pa'] = _mm(sv['ys'], da_, ta=True, out_dtype=BF, name=n("pa_bwd_w"))
    dya = _mm(db_, p['w_pb'], tb=True, name=n("pb_bwd_x"))
    gw['w_pb'] = _mm(sv['ya'], db_, ta=True, out_dtype=BF, name=n("pb_bwd_w"))

    sent = list(gw)
    dq, dk, dv, dkc, dqc, *received = _attn_bwd(
        sv['proj_a'], dya, sv['ya'], sv['lse'], sv['cumx'], sw, name=n("attn_bwd"),
        ride=make_ride({k: gw[k] for k in sent}) if make_ride is not None else None)
    if on_receive is not None:
        on_receive(sent, received)
    dcum = jnp.stack([-dkc.reshape(nh, s), dqc.reshape(nh, s)])
    dflog, dbf = _cum_bwd(dcum, sv['flog'], p['b_f_col'], name=n("cum_bwd"))
    gs['b_f'] = dbf.reshape(nh)

    def glu_bwd(dy_, yv, tv, b):
        zv = _gelu(yv)
        sg = _sigmoid(tv + b)
        dt = dy_ * zv * sg * (1.0 - sg)
        return dt, dy_ * sg, dt

    dt, dz1, gs['b_glu'] = _rowwise(glu_bwd, [dys, sv['y_s5'], sv['tglu']], [p['b_glu']],
                                    [(sw, BF, 'tile'), (sw, F32, 'tile'), (sw, F32, 'sum')], name=n("glu_bwd"))
    dy_s5, = _mm_fused(dt, [p['w_glu']], [dz1, sv['y_s5']], lambda dz2, a, yv: ((a + dz2) * _gelu_grad(yv),),
                       [F32], tb=True, name=n("glu_bwd_x"))
    gw['w_glu'] = _mm(sv['z'], dt, ta=True, out_dtype=BF, name=n("glu_bwd_w"))
    du, dwbr, dwbi, dwcr, dwci, dlr, dli, gs['d_skip'], *received = _s5_bwd(
        sv['proj_a'], dy_s5, p['wb_re'], p['wb_im'], p['wc_re'], p['wc_im'], p['lamb_re'], p['lamb_im'], p['d_skip'],
        name=n("s5_bwd"), ride=make_ride(carried[1]) if carried else None)
    if carried:
        carried[0](list(carried[1]), received)
    g_ = sw // SSM_H
    pst = p['lamb_re'].shape[1] // g_
    gs['lamb_re'], gs['lamb_im'] = dlr.reshape(g_, pst), dli.reshape(g_, pst)
    gs['bbar_re'] = _s5_block_grads(dwbr, SSM_H, pst, False)
    gs['bbar_im'] = _s5_block_grads(dwbi, SSM_H, pst, False)
    gs['c_re'] = _s5_block_grads(dwcr, pst, SSM_H, True).transpose(0, 2, 1)
    gs['c_im'] = _s5_block_grads(dwci, pst, SSM_H, True).transpose(0, 2, 1)

    dproj = jnp.concatenate([du.astype(BF), dq.astype(BF), dk.astype(BF), dv.astype(BF), dflog, dga, dgb], axis=1)
    gw['w_in'] = _mm(sv['h1'], dproj, ta=True, out_dtype=BF, tn=1408, name=n("proj_bwd_w"))
    if make_ride is not None:
        gw = {k: g for k, g in gw.items() if k not in sent}
    if make_ride is not None and not defer_tail:
        dh1, received = _mm(dproj, p['w_in_all'], tb=True, tk=1408, name=n("proj_bwd_x"), ride=make_ride(gw))
        on_receive(list(gw), received)
        gw = {}
    else:
        dh1 = _mm(dproj, p['w_in_all'], tb=True, tk=1408, name=n("proj_bwd_x"))
    dx, dshift_m, dscale_m, gs['g_pre_mix'] = _rowwise(
        pre_bwd, [dh1, dx2, x], [p['g_pre_mix'], scale_m],
        [(d, F32, 'tile'), (d, F32, 'sum'), (d, F32, 'sum'), (d, F32, 'sum')], name=n("pre_mix_bwd"))
    dmod = [dshift_m, dscale_m, dgate_m, dshift_f, dscale_f, dgate_f]
    return dx, gw, dmod, gs


def _unshard(k, blocks):
    if k in COL_SHARDED:
        return blocks.transpose(1, 0, 2).reshape(blocks.shape[1], NDEV * blocks.shape[2])
    return blocks.reshape(NDEV * blocks.shape[1], blocks.shape[2])


def _to_slabs(k, g):
    if k == 'w_in':
        d = g.shape[0]
        nh = d // LANES
        g = jnp.concatenate([g[:, :2 * d + nh], g[:, 2 * d + LANES:]], axis=1)
    if k in COL_SHARDED:
        return g.reshape(g.shape[0], NDEV, g.shape[1] // NDEV).transpose(1, 0, 2)
    return g.reshape(NDEV, g.shape[0] // NDEV, g.shape[1])


def _prep_w_in(w_in):
    d = w_in.shape[0]
    nh = d // LANES
    fcol = 2 * d
    p = {}
    p['w_in_a'] = w_in[:, :fcol]
    p['w_in_f'] = jnp.pad(w_in[:, fcol:fcol + nh], ((0, 0), (0, LANES - nh)))
    p['w_in_g'] = w_in[:, fcol + nh:]
    p['w_in_all'] = jnp.concatenate([p['w_in_a'], p['w_in_f'], p['w_in_g']], axis=1)
    return p


def _prep_small(small):
    nh = small['b_f'].shape[0]
    p = {}
    for k in ('g_pre_mix', 'g_post_mix', 'g_pre_ffn', 'g_post_ffn', 'd_skip', 'b_glu'):
        p[k] = _row(small[k])
    p['b_f_row'] = jnp.pad(_row(small['b_f']), ((0, 0), (0, LANES - nh)))
    p['b_f_col'] = small['b_f'].reshape(nh, 1)
    lbr, lbi, bbr, bbi = _s5_discretise(small['lam_re'], small['lam_im'], small['log_dt'], small['b_re'], small['b_im'])
    p['lamb_re'], p['lamb_im'] = _row(lbr), _row(lbi)
    p['wb_re'] = _s5_operand(bbr, True).astype(BF)
    p['wb_im'] = _s5_operand(bbi, True).astype(BF)
    p['wc_re'] = _s5_operand(small['c_re'].transpose(0, 2, 1), False).astype(BF)
    p['wc_im'] = _s5_operand(small['c_im'].transpose(0, 2, 1), False).astype(BF)
    return p


def _local_step(x, target, mods, ps, small, hooks=None):
    depth = len(ps)
    s, d = x.shape
    hooks = hooks or {}
    saved = []
    h = x
    for l in range(depth):
        h, sv = _layer_fwd(h, mods[l], ps[l], l, ride=functools.partial(hooks['fwd_ride'], l) if hooks else None,
                           on_receive=functools.partial(hooks['fwd_recv'], l) if hooks else None)
        saved.append(sv)

    def loss_fn(yv, tv):
        e = yv - tv
        return e * (1.0 / d), jnp.sum(e * e, axis=1, keepdims=True) * (0.5 / d)

    dy, loss = _rowwise(loss_fn, [h, target], [], [(d, F32, 'tile'), (1, F32, 'sum')], name="loss")
    dmods, gss = [None] * depth, [None] * depth
    unsent = {}
    carried = None
    for l in range(depth - 1, -1, -1):
        def on_receive(names, results, l=l):
            hooks['bwd_recv']([(k, l) for k in names], results)

        dy, gw, dmods[l], gs = _layer_bwd(dy, saved[l], mods[l], ps[l], l,
                                          make_ride=hooks['bwd_ride'] if hooks else None,
                                          on_receive=on_receive if hooks else None,
                                          carried=carried, defer_tail=bool(hooks) and l > 0)
        if hooks and l > 0:
            carried = (on_receive, gw)
        else:
            unsent.update({(k, l): g for k, g in gw.items()})
        sm = small[l]
        _, vjp = jax.vjp(_s5_discretise, sm['lam_re'], sm['lam_im'], sm['log_dt'], sm['b_re'], sm['b_im'])
        gs['lam_re'], gs['lam_im'], gs['log_dt'], gs['b_re'], gs['b_im'] = vjp(
            (gs.pop('lamb_re'), gs.pop('lamb_im'), gs.pop('bbar_re'), gs.pop('bbar_im')))
        gss[l] = gs
    return loss, dy, unsent, dmods, gss


SMALL_LOCAL = ['g_pre_mix', 'g_post_mix', 'g_pre_ffn', 'g_post_ffn', 'lam_re', 'lam_im', 'log_dt', 'b_re', 'b_im',
               'c_re', 'c_im', 'd_skip', 'b_glu', 'b_f']


def kernel(x, c, w_ada, b_ada, g_pre_mix, g_post_mix, g_pre_ffn, g_post_ffn, w_in, lam_re, lam_im, log_dt, b_re, b_im, c_re, c_im, d_skip, w_glu, b_glu, b_f, w_pa, w_pb, w_o, w_ffn_gate, w_ffn_up, w_ffn_down, loss_target, m_w_ada, m_b_ada, m_g_pre_mix, m_g_post_mix, m_g_pre_ffn, m_g_post_ffn, m_w_in, m_lam_re, m_lam_im, m_log_dt, m_b_re, m_b_im, m_c_re, m_c_im, m_d_skip, m_w_glu, m_b_glu, m_b_f, m_w_pa, m_w_pb, m_w_o, m_w_ffn_gate, m_w_ffn_up, m_w_ffn_down, v_w_ada, v_b_ada, v_g_pre_mix, v_g_post_mix, v_g_pre_ffn, v_g_post_ffn, v_w_in, v_lam_re, v_lam_im, v_log_dt, v_b_re, v_b_im, v_c_re, v_c_im, v_d_skip, v_w_glu, v_b_glu, v_b_f, v_w_pa, v_w_pb, v_w_o, v_w_ffn_gate, v_w_ffn_up, v_w_ffn_down):
    args = dict(locals())
    view = lambda k, a: jnp.swapaxes(a, -1, -2) if k in TRANSPOSED else a
    W = {k: view(k, args[k]) for k in WEIGHTS}
    M = {k: view(k, args['m_' + k]) for k in WEIGHTS}
    V = {k: view(k, args['v_' + k]) for k in WEIGHTS}
    depth, d, ncol = w_ada.shape
    s = x.shape[1]
    me = 4 * lax.axis_index("x") + 2 * lax.axis_index("y") + lax.axis_index("c")

    first = ['w_in', 'w_glu']
    c_all, *first_blocks = _exchange([jnp.pad(c, ((0, SUBLANES - 1), (0, 0)))] + [W[k][0].astype(BF) for k in first],
                                     True, name="gather_first")
    c_all = c_all[:, 0, :]

    cond, = _rowwise(_silu, [c_all], [], [(d, F32, 'tile')], name="cond")
    mod_part = jnp.stack([_mm(cond, w_ada[l], name=f"ada_mm{l}") for l in range(depth)], axis=1)
    mod_recv, = _exchange([mod_part.reshape(NDEV, depth, 1, ncol)], False, name="scatter_mod")
    mod_cat = mod_recv.reshape(NDEV, depth, ncol).transpose(1, 0, 2).reshape(depth, NDEV * ncol)
    mod, = _rowwise(lambda a, b: a + b, [mod_cat, b_ada], [], [(NDEV * ncol, F32, 'tile')], name="mod_bias")
    mods = [[mod[l:l + 1, i * d:(i + 1) * d] for i in range(6)] for l in range(depth)]

    small = [{k: W[k][l] for k in SMALL_LOCAL} for l in range(depth)]
    ps = [_prep_small(small[l]) for l in range(depth)]
    rest = [k for k in BIG if k not in first]
    riding = [{'attn': [(k, l) for k in rest], 's5': [(k, l + 1) for k in first if l + 1 < depth]}
              for l in range(depth)]

    def take_weights(keys, results):
        for (k, l), blocks in zip(keys, results):
            full = _unshard(k, blocks)
            ps[l].update(_prep_w_in(full) if k == 'w_in' else {k: full})

    take_weights([(k, 0) for k in first], first_blocks)

    def fwd_ride(l, where):
        blocks = [W[k][ll].astype(BF) for k, ll in riding[l][where]]
        return (_Exchange(blocks, True), blocks) if blocks else None

    grad_parts = {}

    def bwd_ride(grads):
        slabs = [_to_slabs(k, g) for k, g in grads.items()]
        return _Exchange(slabs, False), slabs

    hooks = dict(fwd_ride=fwd_ride, fwd_recv=lambda l, where, results: take_weights(riding[l][where], results),
                 bwd_ride=bwd_ride, bwd_recv=lambda keys, results: grad_parts.update(zip(keys, results)))

    loss, dx, unsent, dmods, gss = _local_step(x[0], loss_target[0], mods, ps, small, hooks)
    assert not unsent
    out = {}
    for k in BIG:
        out[k] = _adamw([grad_parts[(k, l)] for l in range(depth)], W[k], M[k], V[k], name=f"adamw_{k}")

    dmod_mine = jnp.stack([jnp.concatenate(dmods[l], axis=1)[0] for l in range(depth)])
    small_mine = [dmod_mine] + [jnp.stack([gss[l][k] for l in range(depth)]) for k in SMALL_LOCAL] + [loss]
    parts, = _exchange([_pack(small_mine)], True, name="gather_small")
    summed = _sum_parts(parts, name="sum_small")
    names = ['b_ada'] + SMALL_LOCAL
    *small_grads, loss = _unpack(summed, [W[k] for k in names] + [loss])
    loss = loss[0, 0]
    for k, g in zip(names, small_grads):
        shp = W[k].shape
        rows = lambda a: a.reshape(depth, -1, shp[-1])
        res = _adamw([rows(g)[l][None] for l in range(depth)], rows(W[k]), rows(M[k]), rows(V[k]), name=f"adamw_{k}")
        out[k] = [a.reshape(shp) for a in res]

    dmod_all = parts.reshape(NDEV, -1)[:, :depth * 6 * d].reshape(NDEV, depth, 6 * d)
    dmod_cols = lax.dynamic_slice_in_dim(dmod_all, me * ncol, ncol, axis=2)
    g_ada = [_mm(cond, dmod_cols[:, l], ta=True, precision=HI, name=f"ada_bwd{l}")[None] for l in range(depth)]
    out['w_ada'] = _adamw(g_ada, w_ada, m_w_ada, v_w_ada, name="adamw_w_ada")

    return (loss, dx[None], *[view(k, out[k][i]) for i in range(4) for k in WEIGHTS])
```

```python
import functools
import math

import jax
import jax.numpy as jnp
from jax import lax
from jax.experimental import pallas as pl
from jax.experimental.pallas import tpu as pltpu

F32 = jnp.float32
BF = jnp.bfloat16
NDEV = 8
LANES = 128
SUBLANES = 8
VMEM_LIMIT = 48 * 1024 * 1024

SSM_H = 16
HEAD_DIM = 64
RMS_EPS = 1e-6
EIG_CLIP = 1e-4
ADAM_LR = 0.001
ADAM_B1 = 0.9
ADAM_B2 = 0.999
ADAM_EPS = 1e-08
ADAM_WD = 0.01
ADAM_STEP = 10
NEG = -1e30
HI = lax.Precision.HIGHEST

WEIGHTS = ['w_ada', 'b_ada', 'g_pre_mix', 'g_post_mix', 'g_pre_ffn', 'g_post_ffn', 'w_in', 'lam_re', 'lam_im',
           'log_dt', 'b_re', 'b_im', 'c_re', 'c_im', 'd_skip', 'w_glu', 'b_glu', 'b_f', 'w_pa', 'w_pb', 'w_o',
           'w_ffn_gate', 'w_ffn_up', 'w_ffn_down']
TRANSPOSED = ['w_ffn_gate', 'w_ffn_up', 'b_re', 'b_im']
COL_SHARDED = ['w_in', 'w_pa', 'w_pb']
ROW_SHARDED = ['w_glu', 'w_o', 'w_ffn_down', 'w_ffn_gate', 'w_ffn_up']
BIG = COL_SHARDED + ROW_SHARDED
SMALL = ['b_ada', 'g_pre_mix', 'g_post_mix', 'g_pre_ffn', 'g_post_ffn', 'lam_re', 'lam_im', 'log_dt', 'b_re',
         'b_im', 'c_re', 'c_im', 'd_skip', 'b_glu', 'b_f']


def _fit(dim, target, align):
    if dim <= target:
        return dim
    t = (target // align) * align
    while t >= align:
        if dim % t == 0:
            return t
        t -= align
    return dim


def _params(**kw):
    return pltpu.CompilerParams(vmem_limit_bytes=VMEM_LIMIT, **kw)


def _mm(a, b, *, ta=False, tb=False, out_dtype=F32, tm=None, tn=512, tk=2048, precision=None, name, ride=None,
        second=None):
    m, k = (a.shape[1], a.shape[0]) if ta else a.shape
    n = b.shape[0] if tb else b.shape[1]
    assert (b.shape[1] if tb else b.shape[0]) == k
    tm = _fit(m, tm or (1024 if ta else 2048), LANES if ta else 16)
    tn = _fit(n, tn, LANES)
    tk = _fit(k, tk, LANES)
    nk = k // tk
    grid = (m // tm, n // tn, nk)
    dims = (((0 if ta else 1,), (1 if tb else 0,)), ((), ()))
    ex, ex_arrays = ride if ride is not None else (None, [])

    pairs = [(a, b)] + ([second] if second is not None else [])

    def kern(*refs):
        ab_refs, (o_ref,), comm, scratch = _ride_split(ex, refs, 2 * len(pairs), 1)
        step = (pl.program_id(0) * grid[1] + pl.program_id(1)) * grid[2] + pl.program_id(2)
        if ex is not None:
            @pl.when(step == 0)
            def _():
                ex.start(*comm)

            @pl.when(step == (grid[0] * grid[1] * grid[2]) // 2)
            def _():
                ex.forward(*comm)

        p = None
        for a_ref, b_ref in zip(ab_refs[::2], ab_refs[1::2]):
            av, bv = a_ref[...], b_ref[...]
            if precision is None:
                av, bv = av.astype(BF), bv.astype(BF)
            q = lax.dot_general(av, bv, dims, preferred_element_type=F32, precision=precision)
            p = q if p is None else p + q
        if nk == 1:
            o_ref[...] = p.astype(out_dtype)
        else:
            acc_ref, = scratch
            kk = pl.program_id(2)

            @pl.when(kk == 0)
            def _():
                acc_ref[...] = p

            @pl.when(kk > 0)
            def _():
                acc_ref[...] += p

            @pl.when(kk == nk - 1)
            def _():
                o_ref[...] = acc_ref[...].astype(out_dtype)

        if ex is not None:
            @pl.when(step == grid[0] * grid[1] * grid[2] - 1)
            def _():
                ex.wait(*comm)

    a_spec = pl.BlockSpec((tk, tm), lambda i, j, kk: (kk, i)) if ta else pl.BlockSpec((tm, tk), lambda i, j, kk: (i, kk))
    b_spec = pl.BlockSpec((tn, tk), lambda i, j, kk: (j, kk)) if tb else pl.BlockSpec((tk, tn), lambda i, j, kk: (kk, j))
    res = pl.pallas_call(
        kern, name=name,
        out_shape=[jax.ShapeDtypeStruct((m, n), out_dtype)] + (ex.out_shape if ex else []),
        grid=grid,
        in_specs=[a_spec, b_spec] * len(pairs) + (ex.specs if ex else []),
        out_specs=[pl.BlockSpec((tm, tn), lambda i, j, kk: (i, j))] + (ex.specs if ex else []),
        scratch_shapes=(ex.scratch if ex else []) + ([] if nk == 1 else [pltpu.VMEM((tm, tn), F32)]),
        compiler_params=_params(dimension_semantics=("arbitrary",) * 3 if ex else ("parallel", "parallel", "arbitrary"),
                                has_side_effects=ex is not None),
    )(*[x for pair in pairs for x in pair], *ex_arrays)
    return (res[0], res[1:]) if ex else res[0]


def _mm_fused(a, bs, extras, fn, out_dtypes, *, rows=(), tb=False, tm=2048, tn=256, name):
    m, k = a.shape
    n = bs[0].shape[0] if tb else bs[0].shape[1]
    tm = _fit(m, tm, 16)
    tn = _fit(n, tn, LANES)
    extras = [e if isinstance(e, tuple) else (e, 0) for e in extras]
    nb, ne, nr = len(bs), len(extras), len(rows)
    dims = (((1,), (1 if tb else 0,)), ((), ()))

    def kern(*refs):
        av = refs[0][...].astype(BF)
        prods = [lax.dot_general(av, r[...].astype(BF), dims, preferred_element_type=F32) for r in refs[1:1 + nb]]
        res = fn(*prods, *[r[...] for r in refs[1 + nb:1 + nb + ne + nr]])
        for o_ref, r, dt in zip(refs[1 + nb + ne + nr:], res, out_dtypes):
            o_ref[...] = r.astype(dt)

    tile = pl.BlockSpec((tm, tn), lambda i, j: (i, j))
    b_spec = pl.BlockSpec((tn, k), lambda i, j: (j, 0)) if tb else pl.BlockSpec((k, tn), lambda i, j: (0, j))
    return pl.pallas_call(
        kern, name=name, out_shape=[jax.ShapeDtypeStruct((m, n), dt) for dt in out_dtypes],
        grid=(m // tm, n // tn),
        in_specs=[pl.BlockSpec((tm, k), lambda i, j: (i, 0))] + [b_spec] * nb
        + [pl.BlockSpec((tm, tn), lambda i, j, c=c: (i, j + c * (n // tn))) for _, c in extras]
        + [pl.BlockSpec((1, tn), lambda i, j: (0, j))] * nr,
        out_specs=[tile] * len(out_dtypes),
        compiler_params=_params(dimension_semantics=("parallel", "parallel")),
    )(a, *bs, *[e for e, _ in extras], *rows)


def _rowwise(fn, tiles, params, outs, *, tr=256, name):
    tiles = [t if isinstance(t, tuple) else (t, t.shape[1], 0) for t in tiles]
    s = tiles[0][0].shape[0]
    tr = _fit(s, tr, 16)
    nt, npar = len(tiles), len(params)

    def kern(*refs):
        i = pl.program_id(0)
        res = fn(*[r[...] for r in refs[:nt + npar]])
        if not isinstance(res, (tuple, list)):
            res = (res,)
        for (w, dt, kind), o_ref, r in zip(outs, refs[nt + npar:], res):
            if kind == 'tile':
                o_ref[...] = r.astype(dt)
            else:
                part = jnp.sum(r.astype(F32), axis=0, keepdims=True)

                @pl.when(i == 0)
                def _(o_ref=o_ref, part=part):
                    o_ref[...] = part

                @pl.when(i > 0)
                def _(o_ref=o_ref, part=part):
                    o_ref[...] += part

    def tile_spec(w, cb):
        return pl.BlockSpec((tr, w), lambda i: (i, cb))

    in_specs = [tile_spec(w, cb) for _, w, cb in tiles]
    in_specs += [pl.BlockSpec(p.shape, lambda i, nd=p.ndim: (0,) * nd) for p in params]
    out_shape, out_specs = [], []
    for w, dt, kind in outs:
        if kind == 'tile':
            out_shape.append(jax.ShapeDtypeStruct((s, w), dt))
            out_specs.append(pl.BlockSpec((tr, w), lambda i: (i, 0)))
        else:
            out_shape.append(jax.ShapeDtypeStruct((1, w), F32))
            out_specs.append(pl.BlockSpec((1, w), lambda i: (0, 0)))
    res = pl.pallas_call(
        kern, name=name, out_shape=out_shape, grid=(s // tr,), in_specs=in_specs, out_specs=out_specs,
        compiler_params=_params(dimension_semantics=("arbitrary",)),
    )(*[t[0] for t in tiles], *params)
    return res


def _sigmoid(z):
    return 1.0 / (1.0 + jnp.exp(-z))


def _silu(z):
    return z * _sigmoid(z)


_GELU_K = math.sqrt(2.0 / math.pi)


def _gelu(y):
    return 0.5 * y * (1.0 + jnp.tanh(_GELU_K * (y + 0.044715 * y * y * y)))


def _gelu_grad(y):
    th = jnp.tanh(_GELU_K * (y + 0.044715 * y * y * y))
    return 0.5 * (1.0 + th) + 0.5 * y * (1.0 - th * th) * _GELU_K * (1.0 + 3.0 * 0.044715 * y * y)


def _rms(x):
    return lax.rsqrt(jnp.mean(x * x, axis=-1, keepdims=True) + RMS_EPS)


def _norm_bwd(dn, xhat, r):
    return r * (dn - xhat * jnp.mean(dn * xhat, axis=-1, keepdims=True))


def _cum_fwd(flog, bf_row, nh, *, name):
    s = flog.shape[0]
    w = nh * HEAD_DIM
    t = _fit(s, 256, SUBLANES)

    def kern(f_ref, b_ref, o_ref, carry_ref):
        i = pl.program_id(0)

        @pl.when(i == 0)
        def _():
            carry_ref[...] = jnp.zeros_like(carry_ref)

        z = f_ref[...] + b_ref[...]
        logf = jnp.minimum(z, 0.0) - jnp.log(1.0 + jnp.exp(-jnp.abs(z)))
        hh = lax.broadcasted_iota(jnp.int32, (LANES, w), 0)
        cc = lax.broadcasted_iota(jnp.int32, (LANES, w), 1)
        expand = (cc // HEAD_DIM == hh).astype(F32)
        lx = jnp.dot(logf, expand, preferred_element_type=F32, precision=HI)
        rr = lax.broadcasted_iota(jnp.int32, (t, t), 0)
        kk = lax.broadcasted_iota(jnp.int32, (t, t), 1)
        tri = (kk <= rr).astype(F32)
        cum = jnp.dot(tri, lx, preferred_element_type=F32, precision=HI) + carry_ref[...]
        o_ref[...] = cum
        carry_ref[...] = cum[t - 1:t, :]

    return pl.pallas_call(
        kern, name=name, out_shape=jax.ShapeDtypeStruct((s, w), F32), grid=(s // t,),
        in_specs=[pl.BlockSpec((t, LANES), lambda i: (i, 0)), pl.BlockSpec((1, LANES), lambda i: (0, 0))],
        out_specs=pl.BlockSpec((t, w), lambda i: (i, 0)),
        scratch_shapes=[pltpu.VMEM((1, w), F32)],
        compiler_params=_params(dimension_semantics=("arbitrary",)),
    )(flog, bf_row)


def _cum_bwd(dcrow, flog, bf_col, *, name):
    _, nh, s = dcrow.shape
    t = _fit(s, 512, LANES)
    nb = s // t

    def kern(d_ref, f_ref, b_ref, df_ref, db_ref):
        rr = lax.broadcasted_iota(jnp.int32, (t, t), 0)
        kk = lax.broadcasted_iota(jnp.int32, (t, t), 1)
        upper = (rr >= kk).astype(F32)
        pick = (lax.broadcasted_iota(jnp.int32, (nh, LANES), 0)
                == lax.broadcasted_iota(jnp.int32, (nh, LANES), 1)).astype(F32)
        carry = jnp.zeros((nh, 1), F32)
        db = jnp.zeros((nh, 1), F32)
        for blk in range(nb - 1, -1, -1):
            sl = slice(blk * t, (blk + 1) * t)
            rc = jnp.dot(d_ref[0, :, sl] + d_ref[1, :, sl], upper, preferred_element_type=F32, precision=HI) + carry
            carry = rc[:, 0:1]
            frow = lax.dot_general(pick, f_ref[sl, :], (((1,), (1,)), ((), ())), preferred_element_type=F32,
                                   precision=HI)
            df = rc * _sigmoid(-(frow + b_ref[...]))
            df_ref[sl, :] = lax.dot_general(df, pick, (((0,), (0,)), ((), ())), preferred_element_type=F32,
                                            precision=HI).astype(BF)
            db = db + jnp.sum(df, axis=1, keepdims=True)
        db_ref[...] = db

    return pl.pallas_call(
        kern, name=name,
        out_shape=[jax.ShapeDtypeStruct((s, LANES), BF), jax.ShapeDtypeStruct((nh, 1), F32)],
        compiler_params=_params(),
    )(dcrow, flog, bf_col)


def _ride_split(ex, refs, n_in, n_out):
    n = ex.n if ex is not None else 0
    own_in, srcs = refs[:n_in], refs[n_in:n_in + n]
    own_out, dsts = refs[n_in + n:n_in + n + n_out], refs[n_in + n + n_out:n_in + 2 * n + n_out]
    sems = refs[n_in + 2 * n + n_out:n_in + 2 * n + n_out + 3] if n else ()
    rest = refs[n_in + 2 * n + n_out + (3 if n else 0):]
    return own_in, own_out, (srcs, dsts, sems), rest


ATTN_TILE = 512
ATTN_TILE_BWD = 256
ATTN_STRIP = 32
BIAS_LANES = 3


def _head_masks(rows):
    lane = lax.broadcasted_iota(jnp.int32, (rows, LANES), 1)
    return [(lane >= HEAD_DIM * e) & (lane < HEAD_DIM * (e + 1)) for e in range(2)]


def _augment(feat, bias, e, *, bias_slot, ones_slot):
    rows = feat.shape[0]
    lane = lax.broadcasted_iota(jnp.int32, (rows, LANES), 1)
    own = (lane >= HEAD_DIM * e) & (lane < HEAD_DIM * (e + 1))
    off = lane - HEAD_DIM * (1 - e)
    out = jnp.where(own, feat, 0.0)
    if ones_slot is not None:
        out = jnp.where((off >= ones_slot * BIAS_LANES) & (off < (ones_slot + 1) * BIAS_LANES), 1.0, out)
    if bias is not None:
        rest = pltpu.roll(bias, HEAD_DIM, 1)
        for term in range(BIAS_LANES):
            part = rest.astype(BF).astype(F32)
            out = jnp.where(off == bias_slot * BIAS_LANES + term, part, out)
            rest = rest - part
    return out.astype(BF)


def _two_slot_pipeline(m, scores, tile):
    scores(0, 0)

    def pair(n, carry):
        k = 2 * n
        scores(k + 1, 1)
        tile(k, 0, False)
        scores(k + 2, 0)
        tile(k + 1, 1, False)
        return carry

    lax.fori_loop(0, m // 2, pair, 0)

    @pl.when(m % 2 == 0)
    def _():
        tile(m, 0, True)

    @pl.when(m % 2 == 1)
    def _():
        scores(m, 1)
        tile(m - 1, 0, False)
        tile(m, 1, True)


def _attn_fwd(proj, cumx, qcol, *, name, ride=None):
    s = proj.shape[0]
    w = cumx.shape[1]
    nhp = w // LANES
    t = _fit(s, ATTN_TILE, LANES)
    nq = s // t
    strip = _fit(t, ATTN_STRIP, 16)
    scale = HEAD_DIM ** -0.5
    qb, kb, vb = qcol // LANES, (qcol + w) // LANES, (qcol + 2 * w) // LANES
    ex, ex_arrays = ride if ride is not None else (None, [])
    nt_dims = (((1,), (1,)), ((), ()))

    def kern(*refs):
        own_in, (o_ref, l_ref), comm, scratch = _ride_split(ex, refs, 5, 2)
        q_ref, k_ref, v_ref, cxq_ref, cxk_ref = own_in
        ka_ref, vat_ref, s0_ref, s1_ref, p_ref, m_ref, acc_ref = scratch
        s_refs = (s0_ref, s1_ref)
        i = pl.program_id(1)
        if ex is not None:
            @pl.when((pl.program_id(0) == 0) & (i == 0))
            def _():
                ex.start(*comm)

            @pl.when((pl.program_id(0) == nhp - 1) & (i == 0))
            def _():
                ex.forward(*comm)

        msks = _head_masks(t)

        @pl.when(i == 0)
        def _():
            def build(c, carry):
                rows = pl.ds(pl.multiple_of(c * t, LANES), t)
                k2, v2, cx = k_ref[rows, :], v_ref[rows, :], cxk_ref[rows, :]
                for e in range(2):
                    ka_ref[e, rows, :] = _augment(k2, -cx, e, bias_slot=1, ones_slot=0)
                    vat_ref[e, :, rows] = jnp.where(msks[e], v2, 1.0).T.astype(BF)
                return carry
            lax.fori_loop(0, nq, build, 0)

        q2 = q_ref[...] * scale
        qa = [_augment(q2, cxq_ref[...], e, bias_slot=0, ones_slot=1) for e in range(2)]
        m_ref[...] = jnp.full(m_ref.shape, NEG, F32)
        acc_ref[...] = jnp.zeros(acc_ref.shape, F32)
        slabs = strip // SUBLANES

        def scores(j, slot):
            rows_k = pl.ds(pl.multiple_of(j * t, LANES), t)
            for e in range(2):
                st = lax.dot_general(ka_ref[e, rows_k, :], qa[e], nt_dims, preferred_element_type=F32)
                s_refs[slot][e] = st.reshape(t // SUBLANES, SUBLANES, t)

        def tile(j, slot, diagonal):
            rows_k = pl.ds(pl.multiple_of(j * t, LANES), t)
            s_ref = s_refs[slot]
            for e in range(2):
                mx = jnp.full((SUBLANES, t), NEG, F32)
                for r in range(t // strip):
                    sl = slice(r * slabs, (r + 1) * slabs)
                    sv = s_ref[e,sl]
                    if diagonal:
                        shape = (slabs, SUBLANES, t)
                        key = (r * strip + lax.broadcasted_iota(jnp.int32, shape, 0) * SUBLANES
                               + lax.broadcasted_iota(jnp.int32, shape, 1))
                        sv = jnp.where(key <= lax.broadcasted_iota(jnp.int32, shape, 2), sv, NEG)
                        s_ref[e,sl] = sv
                    mx = jnp.maximum(mx, jnp.max(sv, axis=0))
                for sh in (4, 2, 1):
                    mx = jnp.maximum(mx, pltpu.roll(mx, sh, 0))
                m_old = m_ref[e]
                m_new = jnp.maximum(m_old, mx)
                alpha = jnp.exp(m_old - m_new)
                m_ref[e] = m_new
                for r in range(t // strip):
                    p = jnp.exp(s_ref[e,r * slabs:(r + 1) * slabs] - m_new[None])
                    p_ref[e, r * strip:(r + 1) * strip, :] = p.reshape(strip, t).astype(BF)
                acc = acc_ref[e].reshape(LANES // SUBLANES, SUBLANES, t) * alpha[None]
                acc_ref[e] = acc.reshape(LANES, t) + jnp.dot(vat_ref[e, :, rows_k], p_ref[e],
                                                             preferred_element_type=F32)

        _two_slot_pipeline(i, scores, tile)

        outs, lses = [], []
        for e in range(2):
            acc = acc_ref[e]
            other = HEAD_DIM * (1 - e)
            den = acc[other:other + 1, :]
            outs.append(acc / den)
            lses.append(jnp.broadcast_to(m_ref[e][0:1, :] + jnp.log(den), (LANES, t)))
        upper = lax.broadcasted_iota(jnp.int32, (LANES, t), 0) < HEAD_DIM
        o_ref[...] = jnp.where(upper, outs[0], outs[1]).T.astype(BF)
        l_ref[...] = jnp.where(upper, lses[0], lses[1]).T
        if ex is not None:
            @pl.when((pl.program_id(0) == nhp - 1) & (i == nq - 1))
            def _():
                ex.wait(*comm)

    own_scratch = [pltpu.VMEM((2, s, LANES), BF), pltpu.VMEM((2, LANES, s), BF),
                   pltpu.VMEM((2, t // SUBLANES, SUBLANES, t), F32),
                   pltpu.VMEM((2, t // SUBLANES, SUBLANES, t), F32), pltpu.VMEM((2, t, t), BF),
                   pltpu.VMEM((2, SUBLANES, t), F32), pltpu.VMEM((2, LANES, t), F32)]
    return pl.pallas_call(
        kern, name=name,
        out_shape=[jax.ShapeDtypeStruct((s, w), BF), jax.ShapeDtypeStruct((nhp, s, LANES), F32)]
        + (ex.out_shape if ex else []),
        grid=(nhp, nq),
        in_specs=[pl.BlockSpec((t, LANES), lambda h, i: (i, qb + h)),
                  pl.BlockSpec((s, LANES), lambda h, i: (0, kb + h)),
                  pl.BlockSpec((s, LANES), lambda h, i: (0, vb + h)),
                  pl.BlockSpec((t, LANES), lambda h, i: (i, h)),
                  pl.BlockSpec((s, LANES), lambda h, i: (0, h))] + (ex.specs if ex else []),
        out_specs=[pl.BlockSpec((t, LANES), lambda h, i: (i, h)),
                   pl.BlockSpec((None, t, LANES), lambda h, i: (h, i, 0))] + (ex.specs if ex else []),
        scratch_shapes=(ex.scratch if ex else []) + own_scratch,
        compiler_params=_params(dimension_semantics=("arbitrary", "arbitrary"),
                                has_side_effects=ex is not None),
    )(proj, proj, proj, cumx, cumx, *ex_arrays)


def _attn_bwd(proj, do, o, lse, cumx, qcol, *, name, ride=None):
    s = proj.shape[0]
    w = cumx.shape[1]
    nhp = w // LANES
    t = _fit(s, ATTN_TILE_BWD, LANES)
    nq = s // t
    strip = _fit(t, ATTN_STRIP, 16)
    scale = HEAD_DIM ** -0.5
    qb, kb, vb = qcol // LANES, (qcol + w) // LANES, (qcol + 2 * w) // LANES
    tn_dims = (((0,), (0,)), ((), ()))
    nt_dims = (((1,), (1,)), ((), ()))
    ex, ex_arrays = ride if ride is not None else (None, [])

    def kern(*refs):
        own_in, own_out, comm, scratch = _ride_split(ex, refs, 7, 5)
        q_ref, k_ref, v_ref, do_ref, o_ref, l_ref, cx_ref = own_in
        dq_ref, dk_ref, dv_ref, dkc_ref, dqc_ref = own_out
        qa_ref, da_ref, dqa_ref, dka_ref, dva_ref, st0_ref, st1_ref, dpt0_ref, dpt1_ref, pt_ref, dst_ref = scratch
        st_refs, dpt_refs = (st0_ref, st1_ref), (dpt0_ref, dpt1_ref)
        j = pl.program_id(1)
        if ex is not None:
            @pl.when((pl.program_id(0) == 0) & (j == 0))
            def _():
                ex.start(*comm)

        msks = _head_masks(t)

        @pl.when(j == 0)
        def _():
            def build(c, carry):
                rows = pl.ds(pl.multiple_of(c * t, LANES), t)
                q2 = q_ref[rows, :] * scale
                do2 = do_ref[rows, :]
                dd = do2 * o_ref[rows, :].astype(F32)
                delta = jnp.where(msks[0], jnp.sum(jnp.where(msks[0], dd, 0.0), axis=1, keepdims=True),
                                  jnp.sum(jnp.where(msks[1], dd, 0.0), axis=1, keepdims=True))
                bias = cx_ref[rows, :] - l_ref[rows, :]
                for e in range(2):
                    qa_ref[e, rows, :] = _augment(q2, bias, e, bias_slot=0, ones_slot=1)
                    da_ref[e, rows, :] = _augment(do2, -delta, e, bias_slot=0, ones_slot=None)
                return carry
            lax.fori_loop(0, nq, build, 0)
            dqa_ref[...] = jnp.zeros(dqa_ref.shape, F32)

        rows_k = pl.ds(pl.multiple_of(j * t, LANES), t)
        k2, v2 = k_ref[...], v_ref[...]
        ka = [_augment(k2, -cx_ref[rows_k, :], e, bias_slot=1, ones_slot=0) for e in range(2)]
        va = [_augment(v2, None, e, bias_slot=None, ones_slot=0) for e in range(2)]
        dka_ref[...] = jnp.zeros(dka_ref.shape, F32)
        dva_ref[...] = jnp.zeros(dva_ref.shape, F32)

        def scores(k, slot):
            rows_q = pl.ds(pl.multiple_of((nq - 1 - k) * t, LANES), t)
            for e in range(2):
                st_refs[slot][e] = lax.dot_general(ka[e], qa_ref[e, rows_q, :], nt_dims,
                                                   preferred_element_type=F32)
                dpt_refs[slot][e] = lax.dot_general(va[e], da_ref[e, rows_q, :], nt_dims,
                                                    preferred_element_type=F32)

        def tile(k, slot, diagonal):
            rows_q = pl.ds(pl.multiple_of((nq - 1 - k) * t, LANES), t)
            st_ref, dpt_ref = st_refs[slot], dpt_refs[slot]
            for e in range(2):
                for r in range(t // strip):
                    rows = slice(r * strip, (r + 1) * strip)
                    sv = st_ref[e, rows, :]
                    if diagonal:
                        key = r * strip + lax.broadcasted_iota(jnp.int32, (strip, t), 0)
                        qry = lax.broadcasted_iota(jnp.int32, (strip, t), 1)
                        sv = jnp.where(key <= qry, sv, NEG)
                    p = jnp.exp(sv)
                    pt_ref[e, rows, :] = p.astype(BF)
                    dst_ref[e, rows, :] = (p * dpt_ref[e, rows, :]).astype(BF)
            for e in range(2):
                dva_ref[e] += jnp.dot(pt_ref[e], da_ref[e, rows_q, :], preferred_element_type=F32)
                dka_ref[e] += jnp.dot(dst_ref[e], qa_ref[e, rows_q, :], preferred_element_type=F32)
                dqa_ref[e, rows_q, :] += lax.dot_general(dst_ref[e], ka[e], tn_dims, preferred_element_type=F32)

        _two_slot_pipeline(nq - 1 - j, scores, tile)

        dk_ref[...] = jnp.where(msks[0], dka_ref[0], dka_ref[1])
        dv_ref[...] = jnp.where(msks[0], dva_ref[0], dva_ref[1])
        sums = jnp.where(msks[1], dka_ref[0], dka_ref[1]).T
        dkc_ref[0:1, :] = sums[HEAD_DIM + BIAS_LANES:HEAD_DIM + BIAS_LANES + 1, :]
        dkc_ref[1:2, :] = sums[BIAS_LANES:BIAS_LANES + 1, :]

        @pl.when(j == nq - 1)
        def _():
            def flush(c, carry):
                rows = pl.ds(pl.multiple_of(c * t, LANES), t)
                a0, a1 = dqa_ref[0, rows, :], dqa_ref[1, rows, :]
                dq_ref[rows, :] = jnp.where(msks[0], a0, a1) * scale
                sums = jnp.where(msks[1], a0, a1).T
                dqc_ref[0:1, rows] = sums[HEAD_DIM:HEAD_DIM + 1, :]
                dqc_ref[1:2, rows] = sums[0:1, :]
                return carry
            lax.fori_loop(0, nq, flush, 0)

        if ex is not None:
            @pl.when((pl.program_id(0) == nhp - 1) & (j == nq - 1))
            def _():
                ex.wait(*comm)

    full = lambda cb: pl.BlockSpec((s, LANES), lambda h, j: (0, cb + h))
    blk = lambda cb: pl.BlockSpec((t, LANES), lambda h, j: (j, cb + h))
    own_scratch = [pltpu.VMEM((2, s, LANES), BF), pltpu.VMEM((2, s, LANES), BF), pltpu.VMEM((2, s, LANES), F32),
                   pltpu.VMEM((2, t, LANES), F32), pltpu.VMEM((2, t, LANES), F32),
                   pltpu.VMEM((2, t, t), F32), pltpu.VMEM((2, t, t), F32),
                   pltpu.VMEM((2, t, t), F32), pltpu.VMEM((2, t, t), F32),
                   pltpu.VMEM((2, t, t), BF), pltpu.VMEM((2, t, t), BF)]
    return pl.pallas_call(
        kern, name=name,
        out_shape=[jax.ShapeDtypeStruct((s, w), F32)] * 3 + [jax.ShapeDtypeStruct((nhp, 2, s), F32)] * 2
        + (ex.out_shape if ex else []),
        grid=(nhp, nq),
        in_specs=[full(qb), blk(kb), blk(vb), full(0), full(0),
                  pl.BlockSpec((None, s, LANES), lambda h, j: (h, 0, 0)), full(0)] + (ex.specs if ex else []),
        out_specs=[full(0), blk(0), blk(0), pl.BlockSpec((None, 2, t), lambda h, j: (h, 0, j)),
                   pl.BlockSpec((None, 2, s), lambda h, j: (h, 0, 0))] + (ex.specs if ex else []),
        scratch_shapes=(ex.scratch if ex else []) + own_scratch,
        compiler_params=_params(dimension_semantics=("arbitrary", "arbitrary"),
                                has_side_effects=ex is not None),
    )(proj, proj, proj, do, o, lse, cumx, *ex_arrays)


S5_STATES = 256
S5_ROWS = 512


def _cmul(ar, ai, br, bi):
    return ar * br - ai * bi, ar * bi + ai * br


def _scan_tables(lr, li, reverse):
    w = lr.shape[1]
    row = lax.broadcasted_iota(jnp.int32, (SUBLANES, w), 0)
    if reverse:
        row = SUBLANES - 1 - row
    lr1, li1 = jnp.broadcast_to(lr, (SUBLANES, w)), jnp.broadcast_to(li, (SUBLANES, w))
    lr2, li2 = _cmul(lr1, li1, lr1, li1)
    lr4, li4 = _cmul(lr2, li2, lr2, li2)
    steps = []
    for d, (pr, pi) in zip((1, 2, 4), ((lr1, li1), (lr2, li2), (lr4, li4))):
        keep = row >= d
        steps.append((jnp.where(keep, pr, 0.0), jnp.where(keep, pi, 0.0)))
    cr, ci = lr1, li1
    for bit, (pr, pi) in zip((1, 2, 4), ((lr1, li1), (lr2, li2), (lr4, li4))):
        nr, ni = _cmul(cr, ci, pr, pi)
        has = (row & bit) != 0
        cr, ci = jnp.where(has, nr, cr), jnp.where(has, ni, ci)
    return steps, (cr, ci)


def _scan_local(xr, xi, steps, reverse):
    for d, (pr, pi) in zip((1, 2, 4), steps):
        sh = (SUBLANES - d) if reverse else d
        sr, si = pltpu.roll(xr, sh, 0), pltpu.roll(xi, sh, 0)
        xr, xi = xr + (pr * sr - pi * si), xi + (pr * si + pi * sr)
    return xr, xi


def _scan_carry(xr, xi, car_r, car_i, carry_pow):
    cr, ci = carry_pow
    return xr + (cr * car_r - ci * car_i), xi + (cr * car_i + ci * car_r)


SCAN_UNROLL = 4


def _s5_specs(s, ncb):
    u_spec = pl.BlockSpec((s, LANES), lambda cb, hf: (0, cb))
    wb_spec = pl.BlockSpec((None, None, LANES, S5_STATES), lambda cb, hf: (cb, hf, 0, 0))
    wc_spec = pl.BlockSpec((None, None, S5_STATES, LANES), lambda cb, hf: (cb, hf, 0, 0))
    lam_spec = pl.BlockSpec((1, S5_STATES), lambda cb, hf: (0, 2 * cb + hf))
    d_spec = pl.BlockSpec((1, LANES), lambda cb, hf: (0, cb))
    return u_spec, wb_spec, wc_spec, lam_spec, d_spec


def _s5_project_and_scan(u_ref, wbr_ref, wbi_ref, lr_ref, li_ref, xr_ref, xi_ref, s, rows):
    wbr, wbi = wbr_ref[...], wbi_ref[...]
    for r in range(s // rows):
        sl = pl.ds(r * rows, rows)
        ub = u_ref[sl, :].astype(BF)
        xr_ref[sl, :] = jnp.dot(ub, wbr, preferred_element_type=F32)
        xi_ref[sl, :] = jnp.dot(ub, wbi, preferred_element_type=F32)
    steps, cpow = _scan_tables(lr_ref[...], li_ref[...], False)

    unroll = _fit(s // SUBLANES, SCAN_UNROLL, 1)

    def body(b, carry):
        car_r, car_i = carry
        sls = [pl.ds(pl.multiple_of((b * unroll + q) * SUBLANES, SUBLANES), SUBLANES) for q in range(unroll)]
        blocks = [_scan_local(xr_ref[sl, :], xi_ref[sl, :], steps, False) for sl in sls]
        for sl, (xr, xi) in zip(sls, blocks):
            xr, xi = _scan_carry(xr, xi, car_r, car_i, cpow)
            xr_ref[sl, :] = xr
            xi_ref[sl, :] = xi
            car_r, car_i = xr[SUBLANES - 1:SUBLANES, :], xi[SUBLANES - 1:SUBLANES, :]
        return car_r, car_i

    zero = jnp.zeros((1, S5_STATES), F32)
    lax.fori_loop(0, s // SUBLANES // unroll, body, (zero, zero))


def _s5_fwd(proj, wb_re, wb_im, wc_re, wc_im, lam_re, lam_im, dskip, *, name, ride=None):
    s = proj.shape[0]
    w = dskip.shape[1]
    ncb = w // LANES
    rows = _fit(s, S5_ROWS, SUBLANES)
    ex, ex_arrays = ride if ride is not None else (None, [])

    def kern(*refs):
        own_in, (y_ref,), comm, (xr_ref, xi_ref) = _ride_split(ex, refs, 8, 1)
        u_ref, wbr_ref, wbi_ref, wcr_ref, wci_ref, lr_ref, li_ref, d_ref = own_in
        hf = pl.program_id(1)
        if ex is not None:
            @pl.when((pl.program_id(0) == 0) & (hf == 0))
            def _():
                ex.start(*comm)

            @pl.when((pl.program_id(0) == ncb - 1) & (hf == 0))
            def _():
                ex.forward(*comm)

        _s5_project_and_scan(u_ref, wbr_ref, wbi_ref, lr_ref, li_ref, xr_ref, xi_ref, s, rows)
        wcr, wci = wcr_ref[...], wci_ref[...]
        for r in range(s // rows):
            sl = pl.ds(r * rows, rows)
            y = (jnp.dot(xr_ref[sl, :].astype(BF), wcr, preferred_element_type=F32)
                 - jnp.dot(xi_ref[sl, :].astype(BF), wci, preferred_element_type=F32))

            @pl.when(hf == 0)
            def _(y=y, sl=sl):
                y_ref[sl, :] = y + d_ref[...] * u_ref[sl, :]

            @pl.when(hf == 1)
            def _(y=y, sl=sl):
                y_ref[sl, :] += y

        if ex is not None:
            @pl.when((pl.program_id(0) == ncb - 1) & (hf == 1))
            def _():
                ex.wait(*comm)

    u_spec, wb_spec, wc_spec, lam_spec, d_spec = _s5_specs(s, ncb)
    res = pl.pallas_call(
        kern, name=name, out_shape=[jax.ShapeDtypeStruct((s, w), F32)] + (ex.out_shape if ex else []),
        grid=(ncb, 2),
        in_specs=[u_spec, wb_spec, wb_spec, wc_spec, wc_spec, lam_spec, lam_spec, d_spec] + (ex.specs if ex else []),
        out_specs=[u_spec] + (ex.specs if ex else []),
        scratch_shapes=(ex.scratch if ex else []) + [pltpu.VMEM((s, S5_STATES), F32), pltpu.VMEM((s, S5_STATES), F32)],
        compiler_params=_params(dimension_semantics=("arbitrary", "arbitrary"), has_side_effects=ex is not None),
    )(proj, wb_re, wb_im, wc_re, wc_im, lam_re, lam_im, dskip, *ex_arrays)
    return res[0], res[1:]


def _s5_bwd(proj, dy, wb_re, wb_im, wc_re, wc_im, lam_re, lam_im, dskip, *, name, ride=None):
    s = proj.shape[0]
    w = dskip.shape[1]
    ncb = w // LANES
    rows = _fit(s, S5_ROWS, SUBLANES)
    tn_dims = (((0,), (0,)), ((), ()))
    nt_dims = (((1,), (1,)), ((), ()))
    ex, ex_arrays = ride if ride is not None else (None, [])

    def kern(*refs):
        own_in, own_out, comm, scratch = _ride_split(ex, refs, 9, 8)
        u_ref, dy_ref, wbr_ref, wbi_ref, wcr_ref, wci_ref, lr_ref, li_ref, d_ref = own_in
        du_ref, dwbr_ref, dwbi_ref, dwcr_ref, dwci_ref, dlr_ref, dli_ref, dd_ref = own_out
        xr_ref, xi_ref, gr_ref, gi_ref = scratch
        hf = pl.program_id(1)
        if ex is not None:
            @pl.when((pl.program_id(0) == 0) & (hf == 0))
            def _():
                ex.start(*comm)

        _s5_project_and_scan(u_ref, wbr_ref, wbi_ref, lr_ref, li_ref, xr_ref, xi_ref, s, rows)

        wcr, wci = wcr_ref[...], wci_ref[...]
        dwcr = jnp.zeros((S5_STATES, LANES), F32)
        dwci = jnp.zeros((S5_STATES, LANES), F32)
        ddsk = jnp.zeros((1, LANES), F32)
        for r in range(s // rows):
            sl = pl.ds(r * rows, rows)
            dyf = dy_ref[sl, :]
            dyb = dyf.astype(BF)
            gr_ref[sl, :] = lax.dot_general(dyb, wcr, nt_dims, preferred_element_type=F32)
            gi_ref[sl, :] = -lax.dot_general(dyb, wci, nt_dims, preferred_element_type=F32)
            dwcr = dwcr + lax.dot_general(xr_ref[sl, :].astype(BF), dyb, tn_dims, preferred_element_type=F32)
            dwci = dwci - lax.dot_general(xi_ref[sl, :].astype(BF), dyb, tn_dims, preferred_element_type=F32)
            ddsk = ddsk + jnp.sum(dyf * u_ref[sl, :], axis=0, keepdims=True)
        dwcr_ref[...] = dwcr
        dwci_ref[...] = dwci

        @pl.when(hf == 0)
        def _():
            dd_ref[...] = ddsk

        steps, cpow = _scan_tables(lr_ref[...], -li_ref[...], True)
        row = lax.broadcasted_iota(jnp.int32, (SUBLANES, S5_STATES), 0)
        nblk = s // SUBLANES

        unroll = _fit(nblk, SCAN_UNROLL, 1)

        def body(k, carry):
            car_r, car_i, ar, ai = carry
            sls = [pl.ds(pl.multiple_of((nblk - 1 - k * unroll - q) * SUBLANES, SUBLANES), SUBLANES)
                   for q in range(unroll)]
            blocks = [_scan_local(gr_ref[sl, :], gi_ref[sl, :], steps, True) for sl in sls]
            for sl, (g_r, g_i) in zip(sls, blocks):
                g_r, g_i = _scan_carry(g_r, g_i, car_r, car_i, cpow)
                gr_ref[sl, :] = g_r
                gi_ref[sl, :] = g_i
                nr = jnp.where(row == SUBLANES - 1, car_r, pltpu.roll(g_r, SUBLANES - 1, 0))
                ni = jnp.where(row == SUBLANES - 1, car_i, pltpu.roll(g_i, SUBLANES - 1, 0))
                xr, xi = xr_ref[sl, :], xi_ref[sl, :]
                ar = ar + (xr * nr + xi * ni)
                ai = ai + (xr * ni - xi * nr)
                car_r, car_i = g_r[0:1, :], g_i[0:1, :]
            return car_r, car_i, ar, ai

        zero = jnp.zeros((1, S5_STATES), F32)
        zacc = jnp.zeros((SUBLANES, S5_STATES), F32)
        _, _, ar, ai = lax.fori_loop(0, nblk // unroll, body, (zero, zero, zacc, zacc))
        dlr_ref[...] = jnp.sum(ar, axis=0, keepdims=True)
        dli_ref[...] = jnp.sum(ai, axis=0, keepdims=True)

        wbr, wbi = wbr_ref[...], wbi_ref[...]
        dwbr = jnp.zeros((LANES, S5_STATES), F32)
        dwbi = jnp.zeros((LANES, S5_STATES), F32)
        for r in range(s // rows):
            sl = pl.ds(r * rows, rows)
            grb, gib = gr_ref[sl, :].astype(BF), gi_ref[sl, :].astype(BF)
            ub = u_ref[sl, :].astype(BF)
            dwbr = dwbr + lax.dot_general(ub, grb, tn_dims, preferred_element_type=F32)
            dwbi = dwbi + lax.dot_general(ub, gib, tn_dims, preferred_element_type=F32)
            du = (lax.dot_general(grb, wbr, nt_dims, preferred_element_type=F32)
                  + lax.dot_general(gib, wbi, nt_dims, preferred_element_type=F32))

            @pl.when(hf == 0)
            def _(du=du, sl=sl):
                du_ref[sl, :] = du + d_ref[...] * dy_ref[sl, :]

            @pl.when(hf == 1)
            def _(du=du, sl=sl):
                du_ref[sl, :] += du
        dwbr_ref[...] = dwbr
        dwbi_ref[...] = dwbi
        if ex is not None:
            @pl.when((pl.program_id(0) == ncb - 1) & (hf == 1))
            def _():
                ex.wait(*comm)

    u_spec, wb_spec, wc_spec, lam_spec, d_spec = _s5_specs(s, ncb)
    dwb_spec = pl.BlockSpec((None, None, LANES, S5_STATES), lambda cb, hf: (cb, hf, 0, 0))
    dwc_spec = pl.BlockSpec((None, None, S5_STATES, LANES), lambda cb, hf: (cb, hf, 0, 0))
    state = pltpu.VMEM((s, S5_STATES), F32)
    return pl.pallas_call(
        kern, name=name,
        out_shape=[jax.ShapeDtypeStruct((s, w), F32),
                   jax.ShapeDtypeStruct((ncb, 2, LANES, S5_STATES), F32), jax.ShapeDtypeStruct((ncb, 2, LANES, S5_STATES), F32),
                   jax.ShapeDtypeStruct((ncb, 2, S5_STATES, LANES), F32), jax.ShapeDtypeStruct((ncb, 2, S5_STATES, LANES), F32),
                   jax.ShapeDtypeStruct((1, 4 * w), F32), jax.ShapeDtypeStruct((1, 4 * w), F32),
                   jax.ShapeDtypeStruct((1, w), F32)] + (ex.out_shape if ex else []),
        grid=(ncb, 2),
        in_specs=[u_spec, u_spec, wb_spec, wb_spec, wc_spec, wc_spec, lam_spec, lam_spec, d_spec]
        + (ex.specs if ex else []),
        out_specs=[u_spec, dwb_spec, dwb_spec, dwc_spec, dwc_spec, lam_spec, lam_spec, d_spec]
        + (ex.specs if ex else []),
        scratch_shapes=(ex.scratch if ex else []) + [state, state, state, state],
        compiler_params=_params(dimension_semantics=("arbitrary", "arbitrary"), has_side_effects=ex is not None),
    )(proj, dy, wb_re, wb_im, wc_re, wc_im, lam_re, lam_im, dskip, *ex_arrays)


def _s5_discretise(lam_re, lam_im, log_dt, b_re, b_im):
    lr = jnp.minimum(lam_re, -EIG_CLIP)
    li = lam_im
    dt = jnp.exp(log_dt)[:, None]
    mag = jnp.exp(lr * dt)
    lbr, lbi = mag * jnp.cos(li * dt), mag * jnp.sin(li * dt)
    den = lr * lr + li * li
    fr = ((lbr - 1.0) * lr + lbi * li) / den
    fi = (lbi * lr - (lbr - 1.0) * li) / den
    bbr = fr[:, None, :] * b_re - fi[:, None, :] * b_im
    bbi = fr[:, None, :] * b_im + fi[:, None, :] * b_re
    return lbr, lbi, bbr, bbi


def _s5_operand(mats, channels_first):
    g, a, b = mats.shape
    gl = LANES // 2 // SSM_H
    ncb = g // (2 * gl)
    m = mats.reshape(ncb, 2, gl, a, b)
    eye = jnp.eye(gl, dtype=mats.dtype)
    inner = (m[:, :, :, :, None, :] * eye[None, None, :, None, :, None]).reshape(ncb, 2, gl * a, gl * b)
    zeros = jnp.zeros_like(inner[:, 0])
    axis = 1 if channels_first else 2
    return jnp.stack([jnp.concatenate([inner[:, 0], zeros], axis=axis),
                      jnp.concatenate([zeros, inner[:, 1]], axis=axis)], axis=1)


def _s5_block_grads(dwb, a, b, transpose):
    ncb = dwb.shape[0]
    gl = LANES // 2 // (a if not transpose else b)
    if not transpose:
        d = dwb.reshape(ncb, 2, 2, gl, a, gl, b)
        parts = [[d[:, hf, hf, g, :, g, :] for g in range(gl)] for hf in range(2)]
    else:
        d = dwb.reshape(ncb, 2, gl, a, 2, gl, b)
        parts = [[d[:, hf, g, :, hf, g, :] for g in range(gl)] for hf in range(2)]
    st = jnp.stack([jnp.stack(p, axis=1) for p in parts], axis=1)
    return st.reshape(ncb * 2 * gl, a, b)


def _adamw(parts, w, m, v, *, name):
    depth, r, c = w.shape
    assert len(parts) == depth
    npart = parts[0].shape[0]
    row_bytes = 4 * (-(-c // LANES) * LANES)
    align = 16 if parts[0].dtype == BF else SUBLANES
    budget = VMEM_LIMIT // 2 // (2 * (depth * npart + 7) * row_bytes)
    tr = _fit(r, max(align, budget // align * align), align)
    nr = r // tr
    c1 = 1.0 / (1.0 - ADAM_B1 ** ADAM_STEP)
    c2 = 1.0 / (1.0 - ADAM_B2 ** ADAM_STEP)

    def kern(*refs):
        p_refs = refs[:depth]
        w_ref, m_ref, v_ref, g_ref, d_ref, nm_ref, nv_ref = refs[depth:]
        layer = pl.program_id(0)
        for l in range(depth):
            @pl.when(layer == l)
            def _(p_ref=p_refs[l]):
                g = p_ref[0].astype(F32)
                for q in range(1, npart):
                    g = g + p_ref[q].astype(F32)
                m2 = ADAM_B1 * m_ref[...] + (1.0 - ADAM_B1) * g
                v2 = ADAM_B2 * v_ref[...] + (1.0 - ADAM_B2) * (g * g)
                upd = (m2 * c1) / (jnp.sqrt(v2 * c2) + ADAM_EPS) + ADAM_WD * w_ref[...]
                g_ref[...] = g
                d_ref[...] = -ADAM_LR * upd
                nm_ref[...] = m2
                nv_ref[...] = v2

    def part_spec(l):
        return pl.BlockSpec((npart, tr, c),
                            lambda ly, i: (0, jnp.where(ly == l, i, jnp.where(ly < l, 0, nr - 1)), 0))

    spec = pl.BlockSpec((None, tr, c), lambda ly, i: (ly, i, 0))
    return pl.pallas_call(
        kern, name=name, out_shape=[jax.ShapeDtypeStruct((depth, r, c), F32)] * 4, grid=(depth, nr),
        in_specs=[part_spec(l) for l in range(depth)] + [spec, spec, spec],
        out_specs=[spec] * 4,
        compiler_params=_params(dimension_semantics=("arbitrary", "arbitrary")),
    )(*parts, w, m, v)


def _sum_parts(parts, *, name):
    npart, r, c = parts.shape

    def kern(p_ref, o_ref):
        g = p_ref[0]
        for q in range(1, npart):
            g = g + p_ref[q]
        o_ref[...] = g

    return pl.pallas_call(kern, name=name, out_shape=jax.ShapeDtypeStruct((r, c), F32), compiler_params=_params())(parts)


class _Exchange:
    def __init__(self, arrays, gather):
        self.n = len(arrays)
        self.gather = gather
        self.out_shape = [jax.ShapeDtypeStruct(((NDEV,) + a.shape) if gather else a.shape, a.dtype) for a in arrays]
        self.scratch = [pltpu.SemaphoreType.DMA((self.n, NDEV - 1)), pltpu.SemaphoreType.DMA((self.n, NDEV - 1)),
                        pltpu.SemaphoreType.DMA((self.n,))]
        self.specs = [pl.BlockSpec(memory_space=pl.ANY)] * self.n

    def _copies(self, srcs, dsts, sems):
        send_sems, recv_sems, local_sems = sems
        x, y, c = lax.axis_index("x"), lax.axis_index("y"), lax.axis_index("c")
        me = 4 * x + 2 * y + c
        local = [pltpu.make_async_copy(srcs[a] if self.gather else srcs[a].at[me], dsts[a].at[me], local_sems.at[a])
                 for a in range(self.n)]
        remote = []
        for k in (1, 2, 4, 3, 5, 6, 7):
            px, py, pc = x ^ ((k >> 2) & 1), y ^ ((k >> 1) & 1), c ^ (k & 1)
            peer = 4 * px + 2 * py + pc
            for a in range(self.n):
                src = srcs[a] if self.gather else srcs[a].at[peer]
                mk = functools.partial(
                    pltpu.make_async_remote_copy, src_ref=src,
                    send_sem=send_sems.at[a, k - 1], recv_sem=recv_sems.at[a, k - 1],
                    device_id=(px, py, pc), device_id_type=pl.DeviceIdType.MESH)
                remote.append((mk(dst_ref=dsts[a].at[me]), mk(dst_ref=dsts[a].at[peer])))
        return local, remote

    def _gather_copies(self, srcs, dsts, sems):
        send_sems, recv_sems, local_sems = sems
        x, y, c = lax.axis_index("x"), lax.axis_index("y"), lax.axis_index("c")
        block = lambda px, py, pc: 4 * px + 2 * py + pc
        me = block(x, y, c)
        chips = [(1 - x, y), (x, 1 - y), (1 - x, 1 - y)]
        local = [pltpu.make_async_copy(srcs[a], dsts[a].at[me], local_sems.at[a]) for a in range(self.n)]
        own, passed = [], []
        for a in range(self.n):
            def copy(k, blk, to, src=None, a=a):
                return pltpu.make_async_remote_copy(
                    src_ref=dsts[a].at[blk] if src is None else src, dst_ref=dsts[a].at[blk],
                    send_sem=send_sems.at[a, k], recv_sem=recv_sems.at[a, k],
                    device_id=to, device_id_type=pl.DeviceIdType.MESH)
            sib = (x, y, 1 - c)
            own.append((copy(0, me, sib, srcs[a]), copy(0, block(x, y, 1 - c), sib)))
            for j, (px, py) in enumerate(chips):
                own.append((copy(1 + j, me, (px, py, c), srcs[a]), copy(1 + j, block(px, py, c), (px, py, c))))
            for j, (px, py) in enumerate(chips):
                passed.append((copy(4 + j, block(px, py, c), sib), copy(4 + j, block(px, py, 1 - c), sib)))
        return local, own, passed

    def start(self, srcs, dsts, sems):
        if self.gather:
            local, own, _ = self._gather_copies(srcs, dsts, sems)
            for cp in local:
                cp.start()
            for send, _ in own:
                send.start()
            return
        local, remote = self._copies(srcs, dsts, sems)
        for cp in local:
            cp.start()
        for send, _ in remote:
            send.start()

    def forward(self, srcs, dsts, sems):
        if not self.gather:
            return
        _, own, passed = self._gather_copies(srcs, dsts, sems)
        for a in range(self.n):
            for j in range(3):
                own[4 * a + 1 + j][1].wait_recv()
                passed[3 * a + j][0].start()

    def wait(self, srcs, dsts, sems):
        if self.gather:
            local, own, passed = self._gather_copies(srcs, dsts, sems)
            for a in range(self.n):
                own[4 * a][1].wait_recv()
            for _, arrival in passed:
                arrival.wait_recv()
            for send, _ in own + passed:
                send.wait_send()
            for cp in local:
                cp.wait()
            return
        local, remote = self._copies(srcs, dsts, sems)
        for send, arrival in remote:
            send.wait_send()
            arrival.wait_recv()
        for cp in local:
            cp.wait()


def _exchange(arrays, gather, *, name):
    ex = _Exchange(arrays, gather)
    n = ex.n

    def kern(*refs):
        srcs, dsts, sems = refs[:n], refs[n:2 * n], refs[2 * n:]
        ex.start(srcs, dsts, sems)
        ex.forward(srcs, dsts, sems)
        ex.wait(srcs, dsts, sems)

    return pl.pallas_call(
        kern, name=name, out_shape=ex.out_shape, in_specs=ex.specs, out_specs=ex.specs, scratch_shapes=ex.scratch,
        compiler_params=pltpu.CompilerParams(has_side_effects=True),
    )(*arrays)


def _pack(arrays):
    flat = jnp.concatenate([a.reshape(-1).astype(F32) for a in arrays])
    pad = (-flat.shape[0]) % (SUBLANES * LANES)
    return jnp.pad(flat, (0, pad)).reshape(-1, LANES)


def _unpack(buf, like):
    flat = buf.reshape(-1)
    out, off = [], 0
    for a in like:
        sz = math.prod(a.shape)
        out.append(flat[off:off + sz].reshape(a.shape))
        off += sz
    return out


def _row(v):
    return v.reshape(1, -1)


def _layer_fwd(x, mod, p, l, ride=None, on_receive=None, target=None):
    s, d = x.shape
    sw = d // 2
    nh = d // LANES
    shift_m, scale_m, gate_m, shift_f, scale_f, gate_f = mod
    n = lambda tag: f"{tag}{l}"
    sv = {}

    h1, = _rowwise(lambda xv, g, sc, sh: (xv * _rms(xv) * g) * (1.0 + sc) + sh,
                   [x], [p['g_pre_mix'], scale_m, shift_m], [(d, BF, 'tile')], name=n("pre_mix"))
    proj_a = _mm(h1, p['w_in_a'], name=n("proj_a"))
    flog = _mm(h1, p['w_in_f'], name=n("proj_f"))
    gates = _mm(h1, p['w_in_g'], name=n("proj_g"))

    y_s5, received = _s5_fwd(proj_a, p['wb_re'], p['wb_im'], p['wc_re'], p['wc_im'], p['lamb_re'], p['lamb_im'],
                             p['d_skip'], name=n("s5_fwd"), ride=ride('s5') if ride else None)
    if on_receive is not None:
        on_receive('s5', received)
    z, = _rowwise(_gelu, [y_s5], [], [(sw, BF, 'tile')], name=n("gelu"))
    tglu, ys = _mm_fused(z, [p['w_glu']], [y_s5], lambda tv, yv, b: (tv, _gelu(yv) * _sigmoid(tv + b)),
                         [F32, BF], rows=[p['b_glu']], name=n("glu_mm"))

    cumx = _cum_fwd(flog, p['b_f_row'], nh, name=n("cum_fwd"))
    ya, lse, *received = _attn_fwd(proj_a, cumx, sw, name=n("attn_fwd"), ride=ride('attn') if ride else None)
    if on_receive is not None:
        on_receive('attn', received)

    am = _mm(ys, p['w_pa'], name=n("pa_mm"))
    bm, merged = _mm_fused(ya, [p['w_pb']], [am, (gates, 0), (gates, 1)],
                           lambda b, a, ga, gb: (b, _sigmoid(ga) * a + _sigmoid(gb) * b), [F32, BF],
                           name=n("pb_mm"))
    ym = _mm(merged, p['w_o'], name=n("o_mm"))
    def post_mix_pre_ffn(xv, yv, g, gt, g2, sc, sh):
        x2v = xv + gt * (yv * _rms(yv) * g)
        return x2v, (x2v * _rms(x2v) * g2) * (1.0 + sc) + sh

    x2, h2 = _rowwise(post_mix_pre_ffn, [x, ym], [p['g_post_mix'], gate_m, p['g_pre_ffn'], scale_f, shift_f],
                      [(d, F32, 'tile'), (d, BF, 'tile')], name=n("post_mix_pre_ffn"))
    gt, up, act = _mm_fused(h2, [p['w_ffn_gate'], p['w_ffn_up']], [], lambda g, u: (g, u, _silu(g) * u),
                            [F32, F32, BF], tb=True, name=n("gate_up_mm"))
    yf = _mm(act, p['w_ffn_down'], name=n("down_mm"))
    sv.update(x=x, h1=h1, proj_a=proj_a, flog=flog, gates=gates, y_s5=y_s5, z=z, tglu=tglu, ys=ys, cumx=cumx,
              ya=ya, lse=lse, am=am, bm=bm, merged=merged, ym=ym, x2=x2, h2=h2, gt=gt, up=up,
              act=act, yf=yf)
    if target is None:
        x3, = _rowwise(lambda xv, yv, g, gt_: xv + gt_ * (yv * _rms(yv) * g),
                       [x2, yf], [p['g_post_ffn'], gate_f], [(d, F32, 'tile')], name=n("post_ffn"))
        return x3, sv

    def output_and_loss(xv, yv, tv, g, gt_):
        r = _rms(yv)
        nf = yv * r
        e = xv + gt_ * (nf * g) - tv
        dy = e * (1.0 / d)
        return (dy, _norm_bwd(dy * gt_ * g, nf, r), dy * (nf * g), dy * gt_ * nf,
                jnp.sum(e * e, axis=1, keepdims=True) * (0.5 / d))

    dy, dyf, dgate_f, dg, loss = _rowwise(
        output_and_loss, [x2, yf, target], [p['g_post_ffn'], gate_f],
        [(d, F32, 'tile'), (d, BF, 'tile'), (d, F32, 'sum'), (d, F32, 'sum'), (1, F32, 'sum')], name="output_loss")
    sv['post_ffn_bwd'] = (dyf, dgate_f, dg)
    return (dy, loss), sv


def _layer_bwd(dx3, sv, mod, p, l, make_ride=None, on_receive=None, carried=None, defer_tail=False):
    x, x2 = sv['x'], sv['x2']
    s, d = x.shape
    sw = d // 2
    nh = d // LANES
    shift_m, scale_m, gate_m, shift_f, scale_f, gate_f = mod
    n = lambda tag: f"{tag}{l}"
    gw, gs = {}, {}

    def post_bwd(dxo, yv, g, gate):
        r = _rms(yv)
        nf = yv * r
        dn = dxo * gate * g
        return _norm_bwd(dn, nf, r), dxo * (nf * g), dxo * gate * nf

    def pre_bwd(dh, dres, xv, g, sc):
        r = _rms(xv)
        xh = xv * r
        n3 = xh * g
        dn3 = dh * (1.0 + sc)
        return dres + _norm_bwd(dn3 * g, xh, r), dh, dh * n3, dn3 * xh

    if 'post_ffn_bwd' in sv:
        dyf, dgate_f, gs['g_post_ffn'] = sv['post_ffn_bwd']
    else:
        dyf, dgate_f, gs['g_post_ffn'] = _rowwise(
            post_bwd, [dx3, sv['yf']], [p['g_post_ffn'], gate_f],
            [(d, BF, 'tile'), (d, F32, 'sum'), (d, F32, 'sum')], name=n("post_ffn_bwd"))
    gw['w_ffn_down'] = _mm(sv['act'], dyf, ta=True, out_dtype=BF, tm=1408, name=n("down_bwd_w"))

    def swiglu_bwd(da, g, u):
        sg = _sigmoid(g)
        return da * u * (sg * (1.0 + g * (1.0 - sg))), da * (g * sg)

    dgt, dup = _mm_fused(dyf, [p['w_ffn_down']], [sv['gt'], sv['up']], swiglu_bwd, [BF, BF], tb=True,
                         name=n("down_bwd_x"))
    dh2 = _mm(dgt, p['w_ffn_gate'], tm=1024, second=(dup, p['w_ffn_up']), name=n("gate_up_bwd_x"))
    gw['w_ffn_gate'] = _mm(dgt, sv['h2'], ta=True, out_dtype=BF, tm=1408, name=n("gate_bwd_w"))
    gw['w_ffn_up'] = _mm(dup, sv['h2'], ta=True, out_dtype=BF, tm=1408, name=n("up_bwd_w"))
    def pre_ffn_post_mix_bwd(dh, dres, xv, yv, g, sc, g2, gate):
        dx2v, dsh, dsc, dg = pre_bwd(dh, dres, xv, g, sc)
        return (dx2v, dsh, dsc, dg) + post_bwd(dx2v, yv, g2, gate)

    dx2, dshift_f, dscale_f, gs['g_pre_ffn'], dym, dgate_m, gs['g_post_mix'] = _rowwise(
        pre_ffn_post_mix_bwd, [dh2, dx3, x2, sv['ym']], [p['g_pre_ffn'], scale_f, p['g_post_mix'], gate_m],
        [(d, F32, 'tile'), (d, F32, 'sum'), (d, F32, 'sum'), (d, F32, 'sum'),
         (d, BF, 'tile'), (d, F32, 'sum'), (d, F32, 'sum')], name=n("pre_ffn_post_mix_bwd"))
    gw['w_o'] = _mm(sv['merged'], dym, ta=True, out_dtype=BF, name=n("o_bwd_w"))

    def merge_bwd(dm, a, b, ga, gb):
        sa, sb = _sigmoid(ga), _sigmoid(gb)
        return dm * sa, dm * sb, dm * a * sa * (1.0 - sa), dm * b * sb * (1.0 - sb)

    da_, db_, dga, dgb = _mm_fused(dym, [p['w_o']], [sv['am'], sv['bm'], (sv['gates'], 0), (sv['gates'], 1)],
                                   merge_bwd, [BF] * 4, tb=True, name=n("o_bwd_x"))
    dys = _mm(da_, p['w_pa'], tb=True, name=n("pa_bwd_x"))
    gw['w_pa'] = _mm(sv['ys'], da_, ta=True, out_dtype=BF, name=n("pa_bwd_w"))
    dya = _mm(db_, p['w_pb'], tb=True, name=n("pb_bwd_x"))
    gw['w_pb'] = _mm(sv['ya'], db_, ta=True, out_dtype=BF, name=n("pb_bwd_w"))

    sent = list(gw)
    dq, dk, dv, dkc, dqc, *received = _attn_bwd(
        sv['proj_a'], dya, sv['ya'], sv['lse'], sv['cumx'], sw, name=n("attn_bwd"),
        ride=make_ride({k: gw[k] for k in sent}) if make_ride is not None else None)
    if on_receive is not None:
        on_receive(sent, received)
    dcum = jnp.stack([-dkc.reshape(nh, s), dqc.reshape(nh, s)])
    dflog, dbf = _cum_bwd(dcum, sv['flog'], p['b_f_col'], name=n("cum_bwd"))
    gs['b_f'] = dbf.reshape(nh)

    def glu_bwd(dy_, yv, tv, b):
        zv = _gelu(yv)
        sg = _sigmoid(tv + b)
        dt = dy_ * zv * sg * (1.0 - sg)
        return dt, dy_ * sg, dt

    dt, dz1, gs['b_glu'] = _rowwise(glu_bwd, [dys, sv['y_s5'], sv['tglu']], [p['b_glu']],
                                    [(sw, BF, 'tile'), (sw, F32, 'tile'), (sw, F32, 'sum')], name=n("glu_bwd"))
    dy_s5, = _mm_fused(dt, [p['w_glu']], [dz1, sv['y_s5']], lambda dz2, a, yv: ((a + dz2) * _gelu_grad(yv),),
                       [F32], tb=True, name=n("glu_bwd_x"))
    gw['w_glu'] = _mm(sv['z'], dt, ta=True, out_dtype=BF, name=n("glu_bwd_w"))
    du, dwbr, dwbi, dwcr, dwci, dlr, dli, gs['d_skip'], *received = _s5_bwd(
        sv['proj_a'], dy_s5, p['wb_re'], p['wb_im'], p['wc_re'], p['wc_im'], p['lamb_re'], p['lamb_im'], p['d_skip'],
        name=n("s5_bwd"), ride=make_ride(carried[1]) if carried else None)
    if carried:
        carried[0](list(carried[1]), received)
    g_ = sw // SSM_H
    pst = p['lamb_re'].shape[1] // g_
    gs['lamb_re'], gs['lamb_im'] = dlr.reshape(g_, pst), dli.reshape(g_, pst)
    gs['bbar_re'] = _s5_block_grads(dwbr, SSM_H, pst, False)
    gs['bbar_im'] = _s5_block_grads(dwbi, SSM_H, pst, False)
    gs['c_re'] = _s5_block_grads(dwcr, pst, SSM_H, True).transpose(0, 2, 1)
    gs['c_im'] = _s5_block_grads(dwci, pst, SSM_H, True).transpose(0, 2, 1)

    dproj = jnp.concatenate([du.astype(BF), dq.astype(BF), dk.astype(BF), dv.astype(BF), dflog, dga, dgb], axis=1)
    gw['w_in'] = _mm(sv['h1'], dproj, ta=True, out_dtype=BF, tn=1408, name=n("proj_bwd_w"))
    if make_ride is not None:
        gw = {k: g for k, g in gw.items() if k not in sent}
    if make_ride is not None and not defer_tail:
        dh1, received = _mm(dproj, p['w_in_all'], tb=True, tk=1408, name=n("proj_bwd_x"), ride=make_ride(gw))
        on_receive(list(gw), received)
        gw = {}
    else:
        dh1 = _mm(dproj, p['w_in_all'], tb=True, tk=1408, name=n("proj_bwd_x"))
    dx, dshift_m, dscale_m, gs['g_pre_mix'] = _rowwise(
        pre_bwd, [dh1, dx2, x], [p['g_pre_mix'], scale_m],
        [(d, F32, 'tile'), (d, F32, 'sum'), (d, F32, 'sum'), (d, F32, 'sum')], name=n("pre_mix_bwd"))
    dmod = [dshift_m, dscale_m, dgate_m, dshift_f, dscale_f, dgate_f]
    return dx, gw, dmod, gs


def _unshard(k, blocks):
    if k in COL_SHARDED:
        return blocks.transpose(1, 0, 2).reshape(blocks.shape[1], NDEV * blocks.shape[2])
    return blocks.reshape(NDEV * blocks.shape[1], blocks.shape[2])


def _to_slabs(k, g):
    if k == 'w_in':
        d = g.shape[0]
        nh = d // LANES
        g = jnp.concatenate([g[:, :2 * d + nh], g[:, 2 * d + LANES:]], axis=1)
    if k in COL_SHARDED:
        return g.reshape(g.shape[0], NDEV, g.shape[1] // NDEV).transpose(1, 0, 2)
    return g.reshape(NDEV, g.shape[0] // NDEV, g.shape[1])


def _prep_w_in(w_in):
    d = w_in.shape[0]
    nh = d // LANES
    fcol = 2 * d
    p = {}
    p['w_in_a'] = w_in[:, :fcol]
    p['w_in_f'] = jnp.pad(w_in[:, fcol:fcol + nh], ((0, 0), (0, LANES - nh)))
    p['w_in_g'] = w_in[:, fcol + nh:]
    p['w_in_all'] = jnp.concatenate([p['w_in_a'], p['w_in_f'], p['w_in_g']], axis=1)
    return p


def _prep_small(small):
    nh = small['b_f'].shape[0]
    p = {}
    for k in ('g_pre_mix', 'g_post_mix', 'g_pre_ffn', 'g_post_ffn', 'd_skip', 'b_glu'):
        p[k] = _row(small[k])
    p['b_f_row'] = jnp.pad(_row(small['b_f']), ((0, 0), (0, LANES - nh)))
    p['b_f_col'] = small['b_f'].reshape(nh, 1)
    lbr, lbi, bbr, bbi = _s5_discretise(small['lam_re'], small['lam_im'], small['log_dt'], small['b_re'], small['b_im'])
    p['lamb_re'], p['lamb_im'] = _row(lbr), _row(lbi)
    p['wb_re'] = _s5_operand(bbr, True).astype(BF)
    p['wb_im'] = _s5_operand(bbi, True).astype(BF)
    p['wc_re'] = _s5_operand(small['c_re'].transpose(0, 2, 1), False).astype(BF)
    p['wc_im'] = _s5_operand(small['c_im'].transpose(0, 2, 1), False).astype(BF)
    return p


def _local_step(x, target, mods, ps, small, hooks=None):
    depth = len(ps)
    s, d = x.shape
    hooks = hooks or {}
    saved = []
    h = x
    for l in range(depth):
        h, sv = _layer_fwd(h, mods[l], ps[l], l, ride=functools.partial(hooks['fwd_ride'], l) if hooks else None,
                           on_receive=functools.partial(hooks['fwd_recv'], l) if hooks else None,
                           target=target if l == depth - 1 else None)
        saved.append(sv)

    dy, loss = h
    dmods, gss = [None] * depth, [None] * depth
    unsent = {}
    carried = None
    for l in range(depth - 1, -1, -1):
        def on_receive(names, results, l=l):
            hooks['bwd_recv']([(k, l) for k in names], results)

        dy, gw, dmods[l], gs = _layer_bwd(dy, saved[l], mods[l], ps[l], l,
                                          make_ride=hooks['bwd_ride'] if hooks else None,
                                          on_receive=on_receive if hooks else None,
                                          carried=carried, defer_tail=bool(hooks) and l > 0)
        if hooks and l > 0:
            carried = (on_receive, gw)
        else:
            unsent.update({(k, l): g for k, g in gw.items()})
        sm = small[l]
        _, vjp = jax.vjp(_s5_discretise, sm['lam_re'], sm['lam_im'], sm['log_dt'], sm['b_re'], sm['b_im'])
        gs['lam_re'], gs['lam_im'], gs['log_dt'], gs['b_re'], gs['b_im'] = vjp(
            (gs.pop('lamb_re'), gs.pop('lamb_im'), gs.pop('bbar_re'), gs.pop('bbar_im')))
        gss[l] = gs
    return loss, dy, unsent, dmods, gss


SMALL_LOCAL = ['g_pre_mix', 'g_post_mix', 'g_pre_ffn', 'g_post_ffn', 'lam_re', 'lam_im', 'log_dt', 'b_re', 'b_im',
               'c_re', 'c_im', 'd_skip', 'b_glu', 'b_f']


def kernel(x, c, w_ada, b_ada, g_pre_mix, g_post_mix, g_pre_ffn, g_post_ffn, w_in, lam_re, lam_im, log_dt, b_re, b_im, c_re, c_im, d_skip, w_glu, b_glu, b_f, w_pa, w_pb, w_o, w_ffn_gate, w_ffn_up, w_ffn_down, loss_target, m_w_ada, m_b_ada, m_g_pre_mix, m_g_post_mix, m_g_pre_ffn, m_g_post_ffn, m_w_in, m_lam_re, m_lam_im, m_log_dt, m_b_re, m_b_im, m_c_re, m_c_im, m_d_skip, m_w_glu, m_b_glu, m_b_f, m_w_pa, m_w_pb, m_w_o, m_w_ffn_gate, m_w_ffn_up, m_w_ffn_down, v_w_ada, v_b_ada, v_g_pre_mix, v_g_post_mix, v_g_pre_ffn, v_g_post_ffn, v_w_in, v_lam_re, v_lam_im, v_log_dt, v_b_re, v_b_im, v_c_re, v_c_im, v_d_skip, v_w_glu, v_b_glu, v_b_f, v_w_pa, v_w_pb, v_w_o, v_w_ffn_gate, v_w_ffn_up, v_w_ffn_down):
    args = dict(locals())
    view = lambda k, a: jnp.swapaxes(a, -1, -2) if k in TRANSPOSED else a
    W = {k: view(k, args[k]) for k in WEIGHTS}
    M = {k: view(k, args['m_' + k]) for k in WEIGHTS}
    V = {k: view(k, args['v_' + k]) for k in WEIGHTS}
    depth, d, ncol = w_ada.shape
    s = x.shape[1]
    me = 4 * lax.axis_index("x") + 2 * lax.axis_index("y") + lax.axis_index("c")

    first = ['w_in', 'w_glu']
    c_all, *first_blocks = _exchange([jnp.pad(c, ((0, SUBLANES - 1), (0, 0)))] + [W[k][0].astype(BF) for k in first],
                                     True, name="gather_first")
    c_all = c_all[:, 0, :]

    cond, = _rowwise(_silu, [c_all], [], [(d, F32, 'tile')], name="cond")
    mod_part = jnp.stack([_mm(cond, w_ada[l], name=f"ada_mm{l}") for l in range(depth)], axis=1)
    mod_recv, = _exchange([mod_part.reshape(NDEV, depth, 1, ncol)], False, name="scatter_mod")
    mod_cat = mod_recv.reshape(NDEV, depth, ncol).transpose(1, 0, 2).reshape(depth, NDEV * ncol)
    mod, = _rowwise(lambda a, b: a + b, [mod_cat, b_ada], [], [(NDEV * ncol, F32, 'tile')], name="mod_bias")
    mods = [[mod[l:l + 1, i * d:(i + 1) * d] for i in range(6)] for l in range(depth)]

    small = [{k: W[k][l] for k in SMALL_LOCAL} for l in range(depth)]
    ps = [_prep_small(small[l]) for l in range(depth)]
    rest = [k for k in BIG if k not in first]
    riding = [{'attn': [(k, l) for k in rest], 's5': [(k, l + 1) for k in first if l + 1 < depth]}
              for l in range(depth)]

    def take_weights(keys, results):
        for (k, l), blocks in zip(keys, results):
            full = _unshard(k, blocks)
            ps[l].update(_prep_w_in(full) if k == 'w_in' else {k: full})

    take_weights([(k, 0) for k in first], first_blocks)

    def fwd_ride(l, where):
        blocks = [W[k][ll].astype(BF) for k, ll in riding[l][where]]
        return (_Exchange(blocks, True), blocks) if blocks else None

    grad_parts = {}

    def bwd_ride(grads):
        slabs = [_to_slabs(k, g) for k, g in grads.items()]
        return _Exchange(slabs, False), slabs

    hooks = dict(fwd_ride=fwd_ride, fwd_recv=lambda l, where, results: take_weights(riding[l][where], results),
                 bwd_ride=bwd_ride, bwd_recv=lambda keys, results: grad_parts.update(zip(keys, results)))

    loss, dx, unsent, dmods, gss = _local_step(x[0], loss_target[0], mods, ps, small, hooks)
    assert not unsent
    out = {}
    for k in BIG:
        out[k] = _adamw([grad_parts[(k, l)] for l in range(depth)], W[k], M[k], V[k], name=f"adamw_{k}")

    dmod_mine = jnp.stack([jnp.concatenate(dmods[l], axis=1)[0] for l in range(depth)])
    small_mine = [dmod_mine] + [jnp.stack([gss[l][k] for l in range(depth)]) for k in SMALL_LOCAL] + [loss]
    parts, = _exchange([_pack(small_mine)], True, name="gather_small")
    summed = _sum_parts(parts, name="sum_small")
    names = ['b_ada'] + SMALL_LOCAL
    *small_grads, loss = _unpack(summed, [W[k] for k in names] + [loss])
    loss = loss[0, 0]
    for k, g in zip(names, small_grads):
        shp = W[k].shape
        rows = lambda a: a.reshape(depth, -1, shp[-1])
        res = _adamw([rows(g)[l][None] for l in range(depth)], rows(W[k]), rows(M[k]), rows(V[k]), name=f"adamw_{k}")
        out[k] = [a.reshape(shp) for a in res]

    dmod_all = parts.reshape(NDEV, -1)[:, :depth * 6 * d].reshape(NDEV, depth, 6 * d)
    dmod_cols = lax.dynamic_slice_in_dim(dmod_all, me * ncol, ncol, axis=2)
    g_ada = [_mm(cond, dmod_cols[:, l], ta=True, precision=HI, name=f"ada_bwd{l}")[None] for l in range(depth)]
    out['w_ada'] = _adamw(g_ada, w_ada, m_w_ada, v_w_ada, name="adamw_w_ada")

    return (loss, dx[None], *[view(k, out[k][i]) for i in range(4) for k in WEIGHTS])
```

```python
import functools
import math

import jax
import jax.numpy as jnp
from jax import lax
from jax.experimental import pallas as pl
from jax.experimental.pallas import tpu as pltpu

F32 = jnp.float32
BF = jnp.bfloat16
NDEV = 8
LANES = 128
SUBLANES = 8
VMEM_LIMIT = 48 * 1024 * 1024

SSM_H = 16
HEAD_DIM = 64
RMS_EPS = 1e-6
EIG_CLIP = 1e-4
ADAM_LR = 0.001
ADAM_B1 = 0.9
ADAM_B2 = 0.999
ADAM_EPS = 1e-08
ADAM_WD = 0.01
ADAM_STEP = 10
NEG = -1e30
HI = lax.Precision.HIGHEST

WEIGHTS = ['w_ada', 'b_ada', 'g_pre_mix', 'g_post_mix', 'g_pre_ffn', 'g_post_ffn', 'w_in', 'lam_re', 'lam_im',
           'log_dt', 'b_re', 'b_im', 'c_re', 'c_im', 'd_skip', 'w_glu', 'b_glu', 'b_f', 'w_pa', 'w_pb', 'w_o',
           'w_ffn_gate', 'w_ffn_up', 'w_ffn_down']
TRANSPOSED = ['w_ffn_gate', 'w_ffn_up', 'b_re', 'b_im']
COL_SHARDED = ['w_in', 'w_pa', 'w_pb']
ROW_SHARDED = ['w_glu', 'w_o', 'w_ffn_down', 'w_ffn_gate', 'w_ffn_up']
BIG = COL_SHARDED + ROW_SHARDED
SMALL = ['b_ada', 'g_pre_mix', 'g_post_mix', 'g_pre_ffn', 'g_post_ffn', 'lam_re', 'lam_im', 'log_dt', 'b_re',
         'b_im', 'c_re', 'c_im', 'd_skip', 'b_glu', 'b_f']


def _fit(dim, target, align):
    if dim <= target:
        return dim
    t = (target // align) * align
    while t >= align:
        if dim % t == 0:
            return t
        t -= align
    return dim


def _params(**kw):
    return pltpu.CompilerParams(vmem_limit_bytes=VMEM_LIMIT, **kw)


def _mm(a, b, *, ta=False, tb=False, out_dtype=F32, tm=None, tn=512, tk=2048, precision=None, name, ride=None,
        second=None):
    m, k = (a.shape[1], a.shape[0]) if ta else a.shape
    n = b.shape[0] if tb else b.shape[1]
    assert (b.shape[1] if tb else b.shape[0]) == k
    tm = _fit(m, tm or (1024 if ta else 2048), LANES if ta else 16)
    tn = _fit(n, tn, LANES)
    tk = _fit(k, tk, LANES)
    nk = k // tk
    grid = (m // tm, n // tn, nk)
    dims = (((0 if ta else 1,), (1 if tb else 0,)), ((), ()))
    ex, ex_arrays = ride if ride is not None else (None, [])

    pairs = [(a, b)] + ([second] if second is not None else [])

    def kern(*refs):
        ab_refs, (o_ref,), comm, scratch = _ride_split(ex, refs, 2 * len(pairs), 1)
        step = (pl.program_id(0) * grid[1] + pl.program_id(1)) * grid[2] + pl.program_id(2)
        if ex is not None:
            @pl.when(step == 0)
            def _():
                ex.start(*comm)

            @pl.when(step == (grid[0] * grid[1] * grid[2]) // 2)
            def _():
                ex.forward(*comm)

        p = None
        for a_ref, b_ref in zip(ab_refs[::2], ab_refs[1::2]):
            av, bv = a_ref[...], b_ref[...]
            if precision is None:
                av, bv = av.astype(BF), bv.astype(BF)
            q = lax.dot_general(av, bv, dims, preferred_element_type=F32, precision=precision)
            p = q if p is None else p + q
        if nk == 1:
            o_ref[...] = p.astype(out_dtype)
        else:
            acc_ref, = scratch
            kk = pl.program_id(2)

            @pl.when(kk == 0)
            def _():
                acc_ref[...] = p

            @pl.when(kk > 0)
            def _():
                acc_ref[...] += p

            @pl.when(kk == nk - 1)
            def _():
                o_ref[...] = acc_ref[...].astype(out_dtype)

        if ex is not None:
            @pl.when(step == grid[0] * grid[1] * grid[2] - 1)
            def _():
                ex.wait(*comm)

    a_spec = pl.BlockSpec((tk, tm), lambda i, j, kk: (kk, i)) if ta else pl.BlockSpec((tm, tk), lambda i, j, kk: (i, kk))
    b_spec = pl.BlockSpec((tn, tk), lambda i, j, kk: (j, kk)) if tb else pl.BlockSpec((tk, tn), lambda i, j, kk: (kk, j))
    res = pl.pallas_call(
        kern, name=name,
        out_shape=[jax.ShapeDtypeStruct((m, n), out_dtype)] + (ex.out_shape if ex else []),
        grid=grid,
        in_specs=[a_spec, b_spec] * len(pairs) + (ex.specs if ex else []),
        out_specs=[pl.BlockSpec((tm, tn), lambda i, j, kk: (i, j))] + (ex.specs if ex else []),
        scratch_shapes=(ex.scratch if ex else []) + ([] if nk == 1 else [pltpu.VMEM((tm, tn), F32)]),
        compiler_params=_params(dimension_semantics=("arbitrary",) * 3 if ex else ("parallel", "parallel", "arbitrary"),
                                has_side_effects=ex is not None),
    )(*[x for pair in pairs for x in pair], *ex_arrays)
    return (res[0], res[1:]) if ex else res[0]


def _mm_fused(a, bs, extras, fn, out_dtypes, *, rows=(), tb=False, tm=2048, tn=256, name):
    m, k = a.shape
    n = bs[0].shape[0] if tb else bs[0].shape[1]
    tm = _fit(m, tm, 16)
    tn = _fit(n, tn, LANES)
    extras = [e if isinstance(e, tuple) else (e, 0) for e in extras]
    nb, ne, nr = len(bs), len(extras), len(rows)
    dims = (((1,), (1 if tb else 0,)), ((), ()))

    def kern(*refs):
        av = refs[0][...].astype(BF)
        prods = [lax.dot_general(av, r[...].astype(BF), dims, preferred_element_type=F32) for r in refs[1:1 + nb]]
        res = fn(*prods, *[r[...] for r in refs[1 + nb:1 + nb + ne + nr]])
        for o_ref, r, dt in zip(refs[1 + nb + ne + nr:], res, out_dtypes):
            o_ref[...] = r.astype(dt)

    tile = pl.BlockSpec((tm, tn), lambda i, j: (i, j))
    b_spec = pl.BlockSpec((tn, k), lambda i, j: (j, 0)) if tb else pl.BlockSpec((k, tn), lambda i, j: (0, j))
    return pl.pallas_call(
        kern, name=name, out_shape=[jax.ShapeDtypeStruct((m, n), dt) for dt in out_dtypes],
        grid=(m // tm, n // tn),
        in_specs=[pl.BlockSpec((tm, k), lambda i, j: (i, 0))] + [b_spec] * nb
        + [pl.BlockSpec((tm, tn), lambda i, j, c=c: (i, j + c * (n // tn))) for _, c in extras]
        + [pl.BlockSpec((1, tn), lambda i, j: (0, j))] * nr,
        out_specs=[tile] * len(out_dtypes),
        compiler_params=_params(dimension_semantics=("parallel", "parallel")),
    )(a, *bs, *[e for e, _ in extras], *rows)


def _rowwise(fn, tiles, params, outs, *, tr=512, name):
    tiles = [t if isinstance(t, tuple) else (t, t.shape[1], 0) for t in tiles]
    s = tiles[0][0].shape[0]
    tr = _fit(s, tr, 16)
    nt, npar = len(tiles), len(params)

    def kern(*refs):
        i = pl.program_id(0)
        res = fn(*[r[...] for r in refs[:nt + npar]])
        if not isinstance(res, (tuple, list)):
            res = (res,)
        for (w, dt, kind), o_ref, r in zip(outs, refs[nt + npar:], res):
            if kind == 'tile':
                o_ref[...] = r.astype(dt)
            else:
                part = jnp.sum(r.astype(F32), axis=0, keepdims=True)

                @pl.when(i == 0)
                def _(o_ref=o_ref, part=part):
                    o_ref[...] = part

                @pl.when(i > 0)
                def _(o_ref=o_ref, part=part):
                    o_ref[...] += part

    def tile_spec(w, cb):
        return pl.BlockSpec((tr, w), lambda i: (i, cb))

    in_specs = [tile_spec(w, cb) for _, w, cb in tiles]
    in_specs += [pl.BlockSpec(p.shape, lambda i, nd=p.ndim: (0,) * nd) for p in params]
    out_shape, out_specs = [], []
    for w, dt, kind in outs:
        if kind == 'tile':
            out_shape.append(jax.ShapeDtypeStruct((s, w), dt))
            out_specs.append(pl.BlockSpec((tr, w), lambda i: (i, 0)))
        else:
            out_shape.append(jax.ShapeDtypeStruct((1, w), F32))
            out_specs.append(pl.BlockSpec((1, w), lambda i: (0, 0)))
    res = pl.pallas_call(
        kern, name=name, out_shape=out_shape, grid=(s // tr,), in_specs=in_specs, out_specs=out_specs,
        compiler_params=_params(dimension_semantics=("arbitrary",)),
    )(*[t[0] for t in tiles], *params)
    return res


def _sigmoid(z):
    return 1.0 / (1.0 + jnp.exp(-z))


def _silu(z):
    return z * _sigmoid(z)


_GELU_K = math.sqrt(2.0 / math.pi)


def _gelu(y):
    return 0.5 * y * (1.0 + jnp.tanh(_GELU_K * (y + 0.044715 * y * y * y)))


def _gelu_grad(y):
    th = jnp.tanh(_GELU_K * (y + 0.044715 * y * y * y))
    return 0.5 * (1.0 + th) + 0.5 * y * (1.0 - th * th) * _GELU_K * (1.0 + 3.0 * 0.044715 * y * y)


def _rms(x):
    return lax.rsqrt(jnp.mean(x * x, axis=-1, keepdims=True) + RMS_EPS)


def _norm_bwd(dn, xhat, r):
    return r * (dn - xhat * jnp.mean(dn * xhat, axis=-1, keepdims=True))


def _cum_fwd(flog, bf_row, nh, *, name):
    s = flog.shape[0]
    w = nh * HEAD_DIM
    t = _fit(s, 256, SUBLANES)

    def kern(f_ref, b_ref, o_ref, carry_ref):
        i = pl.program_id(0)

        @pl.when(i == 0)
        def _():
            carry_ref[...] = jnp.zeros_like(carry_ref)

        z = f_ref[...] + b_ref[...]
        logf = jnp.minimum(z, 0.0) - jnp.log(1.0 + jnp.exp(-jnp.abs(z)))
        hh = lax.broadcasted_iota(jnp.int32, (LANES, w), 0)
        cc = lax.broadcasted_iota(jnp.int32, (LANES, w), 1)
        expand = (cc // HEAD_DIM == hh).astype(F32)
        lx = jnp.dot(logf, expand, preferred_element_type=F32, precision=HI)
        rr = lax.broadcasted_iota(jnp.int32, (t, t), 0)
        kk = lax.broadcasted_iota(jnp.int32, (t, t), 1)
        tri = (kk <= rr).astype(F32)
        cum = jnp.dot(tri, lx, preferred_element_type=F32, precision=HI) + carry_ref[...]
        o_ref[...] = cum
        carry_ref[...] = cum[t - 1:t, :]

    return pl.pallas_call(
        kern, name=name, out_shape=jax.ShapeDtypeStruct((s, w), F32), grid=(s // t,),
        in_specs=[pl.BlockSpec((t, LANES), lambda i: (i, 0)), pl.BlockSpec((1, LANES), lambda i: (0, 0))],
        out_specs=pl.BlockSpec((t, w), lambda i: (i, 0)),
        scratch_shapes=[pltpu.VMEM((1, w), F32)],
        compiler_params=_params(dimension_semantics=("arbitrary",)),
    )(flog, bf_row)


def _cum_bwd(dcrow, flog, bf_col, *, name):
    _, nh, s = dcrow.shape
    t = _fit(s, 512, LANES)
    nb = s // t

    def kern(d_ref, f_ref, b_ref, df_ref, db_ref):
        rr = lax.broadcasted_iota(jnp.int32, (t, t), 0)
        kk = lax.broadcasted_iota(jnp.int32, (t, t), 1)
        upper = (rr >= kk).astype(F32)
        pick = (lax.broadcasted_iota(jnp.int32, (nh, LANES), 0)
                == lax.broadcasted_iota(jnp.int32, (nh, LANES), 1)).astype(F32)
        carry = jnp.zeros((nh, 1), F32)
        db = jnp.zeros((nh, 1), F32)
        for blk in range(nb - 1, -1, -1):
            sl = slice(blk * t, (blk + 1) * t)
            rc = jnp.dot(d_ref[0, :, sl] + d_ref[1, :, sl], upper, preferred_element_type=F32, precision=HI) + carry
            carry = rc[:, 0:1]
            frow = lax.dot_general(pick, f_ref[sl, :], (((1,), (1,)), ((), ())), preferred_element_type=F32,
                                   precision=HI)
            df = rc * _sigmoid(-(frow + b_ref[...]))
            df_ref[sl, :] = lax.dot_general(df, pick, (((0,), (0,)), ((), ())), preferred_element_type=F32,
                                            precision=HI).astype(BF)
            db = db + jnp.sum(df, axis=1, keepdims=True)
        db_ref[...] = db

    return pl.pallas_call(
        kern, name=name,
        out_shape=[jax.ShapeDtypeStruct((s, LANES), BF), jax.ShapeDtypeStruct((nh, 1), F32)],
        compiler_params=_params(),
    )(dcrow, flog, bf_col)


def _ride_split(ex, refs, n_in, n_out):
    n = ex.n if ex is not None else 0
    own_in, srcs = refs[:n_in], refs[n_in:n_in + n]
    own_out, dsts = refs[n_in + n:n_in + n + n_out], refs[n_in + n + n_out:n_in + 2 * n + n_out]
    sems = refs[n_in + 2 * n + n_out:n_in + 2 * n + n_out + 3] if n else ()
    rest = refs[n_in + 2 * n + n_out + (3 if n else 0):]
    return own_in, own_out, (srcs, dsts, sems), rest


ATTN_TILE = 512
ATTN_TILE_BWD = 256
ATTN_STRIP = 32
BIAS_LANES = 3


def _head_masks(rows):
    lane = lax.broadcasted_iota(jnp.int32, (rows, LANES), 1)
    return [(lane >= HEAD_DIM * e) & (lane < HEAD_DIM * (e + 1)) for e in range(2)]


def _augment(feat, bias, e, *, bias_slot, ones_slot):
    rows = feat.shape[0]
    lane = lax.broadcasted_iota(jnp.int32, (rows, LANES), 1)
    own = (lane >= HEAD_DIM * e) & (lane < HEAD_DIM * (e + 1))
    off = lane - HEAD_DIM * (1 - e)
    out = jnp.where(own, feat, 0.0)
    if ones_slot is not None:
        out = jnp.where((off >= ones_slot * BIAS_LANES) & (off < (ones_slot + 1) * BIAS_LANES), 1.0, out)
    if bias is not None:
        rest = pltpu.roll(bias, HEAD_DIM, 1)
        for term in range(BIAS_LANES):
            part = rest.astype(BF).astype(F32)
            out = jnp.where(off == bias_slot * BIAS_LANES + term, part, out)
            rest = rest - part
    return out.astype(BF)


def _two_slot_pipeline(m, scores, tile):
    scores(0, 0)

    def pair(n, carry):
        k = 2 * n
        scores(k + 1, 1)
        tile(k, 0, False)
        scores(k + 2, 0)
        tile(k + 1, 1, False)
        return carry

    lax.fori_loop(0, m // 2, pair, 0)

    @pl.when(m % 2 == 0)
    def _():
        tile(m, 0, True)

    @pl.when(m % 2 == 1)
    def _():
        scores(m, 1)
        tile(m - 1, 0, False)
        tile(m, 1, True)


def _attn_fwd(proj, cumx, qcol, *, name, ride=None):
    s = proj.shape[0]
    w = cumx.shape[1]
    nhp = w // LANES
    t = _fit(s, ATTN_TILE, LANES)
    nq = s // t
    strip = _fit(t, ATTN_STRIP, 16)
    scale = HEAD_DIM ** -0.5
    qb, kb, vb = qcol // LANES, (qcol + w) // LANES, (qcol + 2 * w) // LANES
    ex, ex_arrays = ride if ride is not None else (None, [])
    nt_dims = (((1,), (1,)), ((), ()))

    def kern(*refs):
        own_in, (o_ref, l_ref), comm, scratch = _ride_split(ex, refs, 5, 2)
        q_ref, k_ref, v_ref, cxq_ref, cxk_ref = own_in
        ka_ref, vat_ref, s0_ref, s1_ref, p_ref, m_ref, acc_ref = scratch
        s_refs = (s0_ref, s1_ref)
        i = pl.program_id(1)
        if ex is not None:
            @pl.when((pl.program_id(0) == 0) & (i == 0))
            def _():
                ex.start(*comm)

            @pl.when((pl.program_id(0) == nhp - 1) & (i == 0))
            def _():
                ex.forward(*comm)

        msks = _head_masks(t)

        @pl.when(i == 0)
        def _():
            def build(c, carry):
                rows = pl.ds(pl.multiple_of(c * t, LANES), t)
                k2, v2, cx = k_ref[rows, :], v_ref[rows, :], cxk_ref[rows, :]
                for e in range(2):
                    ka_ref[e, rows, :] = _augment(k2, -cx, e, bias_slot=1, ones_slot=0)
                    vat_ref[e, :, rows] = jnp.where(msks[e], v2, 1.0).T.astype(BF)
                return carry
            lax.fori_loop(0, nq, build, 0)

        q2 = q_ref[...] * scale
        qa = [_augment(q2, cxq_ref[...], e, bias_slot=0, ones_slot=1) for e in range(2)]
        m_ref[...] = jnp.full(m_ref.shape, NEG, F32)
        acc_ref[...] = jnp.zeros(acc_ref.shape, F32)
        slabs = strip // SUBLANES

        def scores(j, slot):
            rows_k = pl.ds(pl.multiple_of(j * t, LANES), t)
            for e in range(2):
                st = lax.dot_general(ka_ref[e, rows_k, :], qa[e], nt_dims, preferred_element_type=F32)
                s_refs[slot][e] = st.reshape(t // SUBLANES, SUBLANES, t)

        def tile(j, slot, diagonal):
            rows_k = pl.ds(pl.multiple_of(j * t, LANES), t)
            s_ref = s_refs[slot]
            for e in range(2):
                mx = jnp.full((SUBLANES, t), NEG, F32)
                for r in range(t // strip):
                    sl = slice(r * slabs, (r + 1) * slabs)
                    sv = s_ref[e,sl]
                    if diagonal:
                        shape = (slabs, SUBLANES, t)
                        key = (r * strip + lax.broadcasted_iota(jnp.int32, shape, 0) * SUBLANES
                               + lax.broadcasted_iota(jnp.int32, shape, 1))
                        sv = jnp.where(key <= lax.broadcasted_iota(jnp.int32, shape, 2), sv, NEG)
                        s_ref[e,sl] = sv
                    mx = jnp.maximum(mx, jnp.max(sv, axis=0))
                for sh in (4, 2, 1):
                    mx = jnp.maximum(mx, pltpu.roll(mx, sh, 0))
                m_old = m_ref[e]
                m_new = jnp.maximum(m_old, mx)
                alpha = jnp.exp(m_old - m_new)
                m_ref[e] = m_new
                for r in range(t // strip):
                    p = jnp.exp(s_ref[e,r * slabs:(r + 1) * slabs] - m_new[None])
                    p_ref[e, r * strip:(r + 1) * strip, :] = p.reshape(strip, t).astype(BF)
                acc = acc_ref[e].reshape(LANES // SUBLANES, SUBLANES, t) * alpha[None]
                acc_ref[e] = acc.reshape(LANES, t) + jnp.dot(vat_ref[e, :, rows_k], p_ref[e],
                                                             preferred_element_type=F32)

        _two_slot_pipeline(i, scores, tile)

        outs, lses = [], []
        for e in range(2):
            acc = acc_ref[e]
            other = HEAD_DIM * (1 - e)
            den = acc[other:other + 1, :]
            outs.append(acc / den)
            lses.append(jnp.broadcast_to(m_ref[e][0:1, :] + jnp.log(den), (LANES, t)))
        upper = lax.broadcasted_iota(jnp.int32, (LANES, t), 0) < HEAD_DIM
        o_ref[...] = jnp.where(upper, outs[0], outs[1]).T.astype(BF)
        l_ref[...] = jnp.where(upper, lses[0], lses[1]).T
        if ex is not None:
            @pl.when((pl.program_id(0) == nhp - 1) & (i == nq - 1))
            def _():
                ex.wait(*comm)

    own_scratch = [pltpu.VMEM((2, s, LANES), BF), pltpu.VMEM((2, LANES, s), BF),
                   pltpu.VMEM((2, t // SUBLANES, SUBLANES, t), F32),
                   pltpu.VMEM((2, t // SUBLANES, SUBLANES, t), F32), pltpu.VMEM((2, t, t), BF),
                   pltpu.VMEM((2, SUBLANES, t), F32), pltpu.VMEM((2, LANES, t), F32)]
    return pl.pallas_call(
        kern, name=name,
        out_shape=[jax.ShapeDtypeStruct((s, w), BF), jax.ShapeDtypeStruct((nhp, s, LANES), F32)]
        + (ex.out_shape if ex else []),
        grid=(nhp, nq),
        in_specs=[pl.BlockSpec((t, LANES), lambda h, i: (i, qb + h)),
                  pl.BlockSpec((s, LANES), lambda h, i: (0, kb + h)),
                  pl.BlockSpec((s, LANES), lambda h, i: (0, vb + h)),
                  pl.BlockSpec((t, LANES), lambda h, i: (i, h)),
                  pl.BlockSpec((s, LANES), lambda h, i: (0, h))] + (ex.specs if ex else []),
        out_specs=[pl.BlockSpec((t, LANES), lambda h, i: (i, h)),
                   pl.BlockSpec((None, t, LANES), lambda h, i: (h, i, 0))] + (ex.specs if ex else []),
        scratch_shapes=(ex.scratch if ex else []) + own_scratch,
        compiler_params=_params(dimension_semantics=("arbitrary", "arbitrary"),
                                has_side_effects=ex is not None),
    )(proj, proj, proj, cumx, cumx, *ex_arrays)


def _attn_bwd(proj, do, o, lse, cumx, qcol, *, name, ride=None):
    s = proj.shape[0]
    w = cumx.shape[1]
    nhp = w // LANES
    t = _fit(s, ATTN_TILE_BWD, LANES)
    nq = s // t
    strip = _fit(t, ATTN_STRIP, 16)
    scale = HEAD_DIM ** -0.5
    qb, kb, vb = qcol // LANES, (qcol + w) // LANES, (qcol + 2 * w) // LANES
    tn_dims = (((0,), (0,)), ((), ()))
    nt_dims = (((1,), (1,)), ((), ()))
    ex, ex_arrays = ride if ride is not None else (None, [])

    def kern(*refs):
        own_in, own_out, comm, scratch = _ride_split(ex, refs, 7, 5)
        q_ref, k_ref, v_ref, do_ref, o_ref, l_ref, cx_ref = own_in
        dq_ref, dk_ref, dv_ref, dkc_ref, dqc_ref = own_out
        qa_ref, da_ref, dqa_ref, dka_ref, dva_ref, st0_ref, st1_ref, dpt0_ref, dpt1_ref, pt_ref, dst_ref = scratch
        st_refs, dpt_refs = (st0_ref, st1_ref), (dpt0_ref, dpt1_ref)
        j = pl.program_id(1)
        if ex is not None:
            @pl.when((pl.program_id(0) == 0) & (j == 0))
            def _():
                ex.start(*comm)

        msks = _head_masks(t)

        @pl.when(j == 0)
        def _():
            def build(c, carry):
                rows = pl.ds(pl.multiple_of(c * t, LANES), t)
                q2 = q_ref[rows, :] * scale
                do2 = do_ref[rows, :]
                dd = do2 * o_ref[rows, :].astype(F32)
                delta = jnp.where(msks[0], jnp.sum(jnp.where(msks[0], dd, 0.0), axis=1, keepdims=True),
                                  jnp.sum(jnp.where(msks[1], dd, 0.0), axis=1, keepdims=True))
                bias = cx_ref[rows, :] - l_ref[rows, :]
                for e in range(2):
                    qa_ref[e, rows, :] = _augment(q2, bias, e, bias_slot=0, ones_slot=1)
                    da_ref[e, rows, :] = _augment(do2, -delta, e, bias_slot=0, ones_slot=None)
                return carry
            lax.fori_loop(0, nq, build, 0)
            dqa_ref[...] = jnp.zeros(dqa_ref.shape, F32)

        rows_k = pl.ds(pl.multiple_of(j * t, LANES), t)
        k2, v2 = k_ref[...], v_ref[...]
        ka = [_augment(k2, -cx_ref[rows_k, :], e, bias_slot=1, ones_slot=0) for e in range(2)]
        va = [_augment(v2, None, e, bias_slot=None, ones_slot=0) for e in range(2)]
        dka_ref[...] = jnp.zeros(dka_ref.shape, F32)
        dva_ref[...] = jnp.zeros(dva_ref.shape, F32)

        def scores(k, slot):
            rows_q = pl.ds(pl.multiple_of((nq - 1 - k) * t, LANES), t)
            for e in range(2):
                st_refs[slot][e] = lax.dot_general(ka[e], qa_ref[e, rows_q, :], nt_dims,
                                                   preferred_element_type=F32)
                dpt_refs[slot][e] = lax.dot_general(va[e], da_ref[e, rows_q, :], nt_dims,
                                                    preferred_element_type=F32)

        def tile(k, slot, diagonal):
            rows_q = pl.ds(pl.multiple_of((nq - 1 - k) * t, LANES), t)
            st_ref, dpt_ref = st_refs[slot], dpt_refs[slot]
            for e in range(2):
                for r in range(t // strip):
                    rows = slice(r * strip, (r + 1) * strip)
                    sv = st_ref[e, rows, :]
                    if diagonal:
                        key = r * strip + lax.broadcasted_iota(jnp.int32, (strip, t), 0)
                        qry = lax.broadcasted_iota(jnp.int32, (strip, t), 1)
                        sv = jnp.where(key <= qry, sv, NEG)
                    p = jnp.exp(sv)
                    pt_ref[e, rows, :] = p.astype(BF)
                    dst_ref[e, rows, :] = (p * dpt_ref[e, rows, :]).astype(BF)
            for e in range(2):
                dva_ref[e] += jnp.dot(pt_ref[e], da_ref[e, rows_q, :], preferred_element_type=F32)
                dka_ref[e] += jnp.dot(dst_ref[e], qa_ref[e, rows_q, :], preferred_element_type=F32)
                dqa_ref[e, rows_q, :] += lax.dot_general(dst_ref[e], ka[e], tn_dims, preferred_element_type=F32)

        _two_slot_pipeline(nq - 1 - j, scores, tile)

        dk_ref[...] = jnp.where(msks[0], dka_ref[0], dka_ref[1])
        dv_ref[...] = jnp.where(msks[0], dva_ref[0], dva_ref[1])
        sums = jnp.where(msks[1], dka_ref[0], dka_ref[1]).T
        dkc_ref[0:1, :] = sums[HEAD_DIM + BIAS_LANES:HEAD_DIM + BIAS_LANES + 1, :]
        dkc_ref[1:2, :] = sums[BIAS_LANES:BIAS_LANES + 1, :]

        @pl.when(j == nq - 1)
        def _():
            def flush(c, carry):
                rows = pl.ds(pl.multiple_of(c * t, LANES), t)
                a0, a1 = dqa_ref[0, rows, :], dqa_ref[1, rows, :]
                dq_ref[rows, :] = jnp.where(msks[0], a0, a1) * scale
                sums = jnp.where(msks[1], a0, a1).T
                dqc_ref[0:1, rows] = sums[HEAD_DIM:HEAD_DIM + 1, :]
                dqc_ref[1:2, rows] = sums[0:1, :]
                return carry
            lax.fori_loop(0, nq, flush, 0)

        if ex is not None:
            @pl.when((pl.program_id(0) == nhp - 1) & (j == nq - 1))
            def _():
                ex.wait(*comm)

    full = lambda cb: pl.BlockSpec((s, LANES), lambda h, j: (0, cb + h))
    blk = lambda cb: pl.BlockSpec((t, LANES), lambda h, j: (j, cb + h))
    own_scratch = [pltpu.VMEM((2, s, LANES), BF), pltpu.VMEM((2, s, LANES), BF), pltpu.VMEM((2, s, LANES), F32),
                   pltpu.VMEM((2, t, LANES), F32), pltpu.VMEM((2, t, LANES), F32),
                   pltpu.VMEM((2, t, t), F32), pltpu.VMEM((2, t, t), F32),
                   pltpu.VMEM((2, t, t), F32), pltpu.VMEM((2, t, t), F32),
                   pltpu.VMEM((2, t, t), BF), pltpu.VMEM((2, t, t), BF)]
    return pl.pallas_call(
        kern, name=name,
        out_shape=[jax.ShapeDtypeStruct((s, w), F32)] * 3 + [jax.ShapeDtypeStruct((nhp, 2, s), F32)] * 2
        + (ex.out_shape if ex else []),
        grid=(nhp, nq),
        in_specs=[full(qb), blk(kb), blk(vb), full(0), full(0),
                  pl.BlockSpec((None, s, LANES), lambda h, j: (h, 0, 0)), full(0)] + (ex.specs if ex else []),
        out_specs=[full(0), blk(0), blk(0), pl.BlockSpec((None, 2, t), lambda h, j: (h, 0, j)),
                   pl.BlockSpec((None, 2, s), lambda h, j: (h, 0, 0))] + (ex.specs if ex else []),
        scratch_shapes=(ex.scratch if ex else []) + own_scratch,
        compiler_params=_params(dimension_semantics=("arbitrary", "arbitrary"),
                                has_side_effects=ex is not None),
    )(proj, proj, proj, do, o, lse, cumx, *ex_arrays)


S5_STATES = 256
S5_ROWS = 512


def _cmul(ar, ai, br, bi):
    return ar * br - ai * bi, ar * bi + ai * br


def _scan_tables(lr, li, reverse):
    w = lr.shape[1]
    row = lax.broadcasted_iota(jnp.int32, (SUBLANES, w), 0)
    if reverse:
        row = SUBLANES - 1 - row
    lr1, li1 = jnp.broadcast_to(lr, (SUBLANES, w)), jnp.broadcast_to(li, (SUBLANES, w))
    lr2, li2 = _cmul(lr1, li1, lr1, li1)
    lr4, li4 = _cmul(lr2, li2, lr2, li2)
    steps = []
    for d, (pr, pi) in zip((1, 2, 4), ((lr1, li1), (lr2, li2), (lr4, li4))):
        keep = row >= d
        steps.append((jnp.where(keep, pr, 0.0), jnp.where(keep, pi, 0.0)))
    cr, ci = lr1, li1
    for bit, (pr, pi) in zip((1, 2, 4), ((lr1, li1), (lr2, li2), (lr4, li4))):
        nr, ni = _cmul(cr, ci, pr, pi)
        has = (row & bit) != 0
        cr, ci = jnp.where(has, nr, cr), jnp.where(has, ni, ci)
    return steps, (cr, ci)


def _scan_local(xr, xi, steps, reverse):
    for d, (pr, pi) in zip((1, 2, 4), steps):
        sh = (SUBLANES - d) if reverse else d
        sr, si = pltpu.roll(xr, sh, 0), pltpu.roll(xi, sh, 0)
        xr, xi = xr + (pr * sr - pi * si), xi + (pr * si + pi * sr)
    return xr, xi


def _scan_carry(xr, xi, car_r, car_i, carry_pow):
    cr, ci = carry_pow
    return xr + (cr * car_r - ci * car_i), xi + (cr * car_i + ci * car_r)


SCAN_UNROLL = 4


def _s5_specs(s, ncb):
    u_spec = pl.BlockSpec((s, LANES), lambda cb, hf: (0, cb))
    wb_spec = pl.BlockSpec((None, None, LANES, S5_STATES), lambda cb, hf: (cb, hf, 0, 0))
    wc_spec = pl.BlockSpec((None, None, S5_STATES, LANES), lambda cb, hf: (cb, hf, 0, 0))
    lam_spec = pl.BlockSpec((1, S5_STATES), lambda cb, hf: (0, 2 * cb + hf))
    d_spec = pl.BlockSpec((1, LANES), lambda cb, hf: (0, cb))
    return u_spec, wb_spec, wc_spec, lam_spec, d_spec


def _s5_project_and_scan(u_ref, wbr_ref, wbi_ref, lr_ref, li_ref, xr_ref, xi_ref, s, rows):
    wbr, wbi = wbr_ref[...], wbi_ref[...]
    for r in range(s // rows):
        sl = pl.ds(r * rows, rows)
        ub = u_ref[sl, :].astype(BF)
        xr_ref[sl, :] = jnp.dot(ub, wbr, preferred_element_type=F32)
        xi_ref[sl, :] = jnp.dot(ub, wbi, preferred_element_type=F32)
    steps, cpow = _scan_tables(lr_ref[...], li_ref[...], False)

    unroll = _fit(s // SUBLANES, SCAN_UNROLL, 1)

    def body(b, carry):
        car_r, car_i = carry
        sls = [pl.ds(pl.multiple_of((b * unroll + q) * SUBLANES, SUBLANES), SUBLANES) for q in range(unroll)]
        blocks = [_scan_local(xr_ref[sl, :], xi_ref[sl, :], steps, False) for sl in sls]
        for sl, (xr, xi) in zip(sls, blocks):
            xr, xi = _scan_carry(xr, xi, car_r, car_i, cpow)
            xr_ref[sl, :] = xr
            xi_ref[sl, :] = xi
            car_r, car_i = xr[SUBLANES - 1:SUBLANES, :], xi[SUBLANES - 1:SUBLANES, :]
        return car_r, car_i

    zero = jnp.zeros((1, S5_STATES), F32)
    lax.fori_loop(0, s // SUBLANES // unroll, body, (zero, zero))


def _s5_fwd(proj, wb_re, wb_im, wc_re, wc_im, lam_re, lam_im, dskip, *, name, ride=None):
    s = proj.shape[0]
    w = dskip.shape[1]
    ncb = w // LANES
    rows = _fit(s, S5_ROWS, SUBLANES)
    ex, ex_arrays = ride if ride is not None else (None, [])

    def kern(*refs):
        own_in, (y_ref,), comm, (xr_ref, xi_ref) = _ride_split(ex, refs, 8, 1)
        u_ref, wbr_ref, wbi_ref, wcr_ref, wci_ref, lr_ref, li_ref, d_ref = own_in
        hf = pl.program_id(1)
        if ex is not None:
            @pl.when((pl.program_id(0) == 0) & (hf == 0))
            def _():
                ex.start(*comm)

            @pl.when((pl.program_id(0) == ncb - 1) & (hf == 0))
            def _():
                ex.forward(*comm)

        _s5_project_and_scan(u_ref, wbr_ref, wbi_ref, lr_ref, li_ref, xr_ref, xi_ref, s, rows)
        wcr, wci = wcr_ref[...], wci_ref[...]
        for r in range(s // rows):
            sl = pl.ds(r * rows, rows)
            y = (jnp.dot(xr_ref[sl, :].astype(BF), wcr, preferred_element_type=F32)
                 - jnp.dot(xi_ref[sl, :].astype(BF), wci, preferred_element_type=F32))

            @pl.when(hf == 0)
            def _(y=y, sl=sl):
                y_ref[sl, :] = y + d_ref[...] * u_ref[sl, :]

            @pl.when(hf == 1)
            def _(y=y, sl=sl):
                y_ref[sl, :] += y

        if ex is not None:
            @pl.when((pl.program_id(0) == ncb - 1) & (hf == 1))
            def _():
                ex.wait(*comm)

    u_spec, wb_spec, wc_spec, lam_spec, d_spec = _s5_specs(s, ncb)
    res = pl.pallas_call(
        kern, name=name, out_shape=[jax.ShapeDtypeStruct((s, w), F32)] + (ex.out_shape if ex else []),
        grid=(ncb, 2),
        in_specs=[u_spec, wb_spec, wb_spec, wc_spec, wc_spec, lam_spec, lam_spec, d_spec] + (ex.specs if ex else []),
        out_specs=[u_spec] + (ex.specs if ex else []),
        scratch_shapes=(ex.scratch if ex else []) + [pltpu.VMEM((s, S5_STATES), F32), pltpu.VMEM((s, S5_STATES), F32)],
        compiler_params=_params(dimension_semantics=("arbitrary", "arbitrary"), has_side_effects=ex is not None),
    )(proj, wb_re, wb_im, wc_re, wc_im, lam_re, lam_im, dskip, *ex_arrays)
    return res[0], res[1:]


def _s5_bwd(proj, dy, wb_re, wb_im, wc_re, wc_im, lam_re, lam_im, dskip, *, name, ride=None):
    s = proj.shape[0]
    w = dskip.shape[1]
    ncb = w // LANES
    rows = _fit(s, S5_ROWS, SUBLANES)
    tn_dims = (((0,), (0,)), ((), ()))
    nt_dims = (((1,), (1,)), ((), ()))
    ex, ex_arrays = ride if ride is not None else (None, [])

    def kern(*refs):
        own_in, own_out, comm, scratch = _ride_split(ex, refs, 9, 8)
        u_ref, dy_ref, wbr_ref, wbi_ref, wcr_ref, wci_ref, lr_ref, li_ref, d_ref = own_in
        du_ref, dwbr_ref, dwbi_ref, dwcr_ref, dwci_ref, dlr_ref, dli_ref, dd_ref = own_out
        xr_ref, xi_ref, gr_ref, gi_ref = scratch
        hf = pl.program_id(1)
        if ex is not None:
            @pl.when((pl.program_id(0) == 0) & (hf == 0))
            def _():
                ex.start(*comm)

        _s5_project_and_scan(u_ref, wbr_ref, wbi_ref, lr_ref, li_ref, xr_ref, xi_ref, s, rows)

        wcr, wci = wcr_ref[...], wci_ref[...]
        dwcr = jnp.zeros((S5_STATES, LANES), F32)
        dwci = jnp.zeros((S5_STATES, LANES), F32)
        ddsk = jnp.zeros((1, LANES), F32)
        for r in range(s // rows):
            sl = pl.ds(r * rows, rows)
            dyf = dy_ref[sl, :]
            dyb = dyf.astype(BF)
            gr_ref[sl, :] = lax.dot_general(dyb, wcr, nt_dims, preferred_element_type=F32)
            gi_ref[sl, :] = -lax.dot_general(dyb, wci, nt_dims, preferred_element_type=F32)
            dwcr = dwcr + lax.dot_general(xr_ref[sl, :].astype(BF), dyb, tn_dims, preferred_element_type=F32)
            dwci = dwci - lax.dot_general(xi_ref[sl, :].astype(BF), dyb, tn_dims, preferred_element_type=F32)
            ddsk = ddsk + jnp.sum(dyf * u_ref[sl, :], axis=0, keepdims=True)
        dwcr_ref[...] = dwcr
        dwci_ref[...] = dwci

        @pl.when(hf == 0)
        def _():
            dd_ref[...] = ddsk

        steps, cpow = _scan_tables(lr_ref[...], -li_ref[...], True)
        row = lax.broadcasted_iota(jnp.int32, (SUBLANES, S5_STATES), 0)
        nblk = s // SUBLANES

        unroll = _fit(nblk, SCAN_UNROLL, 1)

        def body(k, carry):
            car_r, car_i, ar, ai = carry
            sls = [pl.ds(pl.multiple_of((nblk - 1 - k * unroll - q) * SUBLANES, SUBLANES), SUBLANES)
                   for q in range(unroll)]
            blocks = [_scan_local(gr_ref[sl, :], gi_ref[sl, :], steps, True) for sl in sls]
            for sl, (g_r, g_i) in zip(sls, blocks):
                g_r, g_i = _scan_carry(g_r, g_i, car_r, car_i, cpow)
                gr_ref[sl, :] = g_r
                gi_ref[sl, :] = g_i
                nr = jnp.where(row == SUBLANES - 1, car_r, pltpu.roll(g_r, SUBLANES - 1, 0))
                ni = jnp.where(row == SUBLANES - 1, car_i, pltpu.roll(g_i, SUBLANES - 1, 0))
                xr, xi = xr_ref[sl, :], xi_ref[sl, :]
                ar = ar + (xr * nr + xi * ni)
                ai = ai + (xr * ni - xi * nr)
                car_r, car_i = g_r[0:1, :], g_i[0:1, :]
            return car_r, car_i, ar, ai

        zero = jnp.zeros((1, S5_STATES), F32)
        zacc = jnp.zeros((SUBLANES, S5_STATES), F32)
        _, _, ar, ai = lax.fori_loop(0, nblk // unroll, body, (zero, zero, zacc, zacc))
        dlr_ref[...] = jnp.sum(ar, axis=0, keepdims=True)
        dli_ref[...] = jnp.sum(ai, axis=0, keepdims=True)

        wbr, wbi = wbr_ref[...], wbi_ref[...]
        dwbr = jnp.zeros((LANES, S5_STATES), F32)
        dwbi = jnp.zeros((LANES, S5_STATES), F32)
        for r in range(s // rows):
            sl = pl.ds(r * rows, rows)
            grb, gib = gr_ref[sl, :].astype(BF), gi_ref[sl, :].astype(BF)
            ub = u_ref[sl, :].astype(BF)
            dwbr = dwbr + lax.dot_general(ub, grb, tn_dims, preferred_element_type=F32)
            dwbi = dwbi + lax.dot_general(ub, gib, tn_dims, preferred_element_type=F32)
            du = (lax.dot_general(grb, wbr, nt_dims, preferred_element_type=F32)
                  + lax.dot_general(gib, wbi, nt_dims, preferred_element_type=F32))

            @pl.when(hf == 0)
            def _(du=du, sl=sl):
                du_ref[sl, :] = du + d_ref[...] * dy_ref[sl, :]

            @pl.when(hf == 1)
            def _(du=du, sl=sl):
                du_ref[sl, :] += du
        dwbr_ref[...] = dwbr
        dwbi_ref[...] = dwbi
        if ex is not None:
            @pl.when((pl.program_id(0) == ncb - 1) & (hf == 1))
            def _():
                ex.wait(*comm)

    u_spec, wb_spec, wc_spec, lam_spec, d_spec = _s5_specs(s, ncb)
    dwb_spec = pl.BlockSpec((None, None, LANES, S5_STATES), lambda cb, hf: (cb, hf, 0, 0))
    dwc_spec = pl.BlockSpec((None, None, S5_STATES, LANES), lambda cb, hf: (cb, hf, 0, 0))
    state = pltpu.VMEM((s, S5_STATES), F32)
    return pl.pallas_call(
        kern, name=name,
        out_shape=[jax.ShapeDtypeStruct((s, w), F32),
                   jax.ShapeDtypeStruct((ncb, 2, LANES, S5_STATES), F32), jax.ShapeDtypeStruct((ncb, 2, LANES, S5_STATES), F32),
                   jax.ShapeDtypeStruct((ncb, 2, S5_STATES, LANES), F32), jax.ShapeDtypeStruct((ncb, 2, S5_STATES, LANES), F32),
                   jax.ShapeDtypeStruct((1, 4 * w), F32), jax.ShapeDtypeStruct((1, 4 * w), F32),
                   jax.ShapeDtypeStruct((1, w), F32)] + (ex.out_shape if ex else []),
        grid=(ncb, 2),
        in_specs=[u_spec, u_spec, wb_spec, wb_spec, wc_spec, wc_spec, lam_spec, lam_spec, d_spec]
        + (ex.specs if ex else []),
        out_specs=[u_spec, dwb_spec, dwb_spec, dwc_spec, dwc_spec, lam_spec, lam_spec, d_spec]
        + (ex.specs if ex else []),
        scratch_shapes=(ex.scratch if ex else []) + [state, state, state, state],
        compiler_params=_params(dimension_semantics=("arbitrary", "arbitrary"), has_side_effects=ex is not None),
    )(proj, dy, wb_re, wb_im, wc_re, wc_im, lam_re, lam_im, dskip, *ex_arrays)


def _s5_discretise(lam_re, lam_im, log_dt, b_re, b_im):
    lr = jnp.minimum(lam_re, -EIG_CLIP)
    li = lam_im
    dt = jnp.exp(log_dt)[:, None]
    mag = jnp.exp(lr * dt)
    lbr, lbi = mag * jnp.cos(li * dt), mag * jnp.sin(li * dt)
    den = lr * lr + li * li
    fr = ((lbr - 1.0) * lr + lbi * li) / den
    fi = (lbi * lr - (lbr - 1.0) * li) / den
    bbr = fr[:, None, :] * b_re - fi[:, None, :] * b_im
    bbi = fr[:, None, :] * b_im + fi[:, None, :] * b_re
    return lbr, lbi, bbr, bbi


def _s5_operand(mats, channels_first):
    g, a, b = mats.shape
    gl = LANES // 2 // SSM_H
    ncb = g // (2 * gl)
    m = mats.reshape(ncb, 2, gl, a, b)
    eye = jnp.eye(gl, dtype=mats.dtype)
    inner = (m[:, :, :, :, None, :] * eye[None, None, :, None, :, None]).reshape(ncb, 2, gl * a, gl * b)
    zeros = jnp.zeros_like(inner[:, 0])
    axis = 1 if channels_first else 2
    return jnp.stack([jnp.concatenate([inner[:, 0], zeros], axis=axis),
                      jnp.concatenate([zeros, inner[:, 1]], axis=axis)], axis=1)


def _s5_block_grads(dwb, a, b, transpose):
    ncb = dwb.shape[0]
    gl = LANES // 2 // (a if not transpose else b)
    if not transpose:
        d = dwb.reshape(ncb, 2, 2, gl, a, gl, b)
        parts = [[d[:, hf, hf, g, :, g, :] for g in range(gl)] for hf in range(2)]
    else:
        d = dwb.reshape(ncb, 2, gl, a, 2, gl, b)
        parts = [[d[:, hf, g, :, hf, g, :] for g in range(gl)] for hf in range(2)]
    st = jnp.stack([jnp.stack(p, axis=1) for p in parts], axis=1)
    return st.reshape(ncb * 2 * gl, a, b)


def _adamw(parts, w, m, v, *, name):
    depth, r, c = w.shape
    assert len(parts) == depth
    npart = parts[0].shape[0]
    row_bytes = 4 * (-(-c // LANES) * LANES)
    align = 16 if parts[0].dtype == BF else SUBLANES
    budget = VMEM_LIMIT // 2 // (2 * (depth * npart + 7) * row_bytes)
    tr = _fit(r, max(align, budget // align * align), align)
    nr = r // tr
    c1 = 1.0 / (1.0 - ADAM_B1 ** ADAM_STEP)
    c2 = 1.0 / (1.0 - ADAM_B2 ** ADAM_STEP)

    def kern(*refs):
        p_refs = refs[:depth]
        w_ref, m_ref, v_ref, g_ref, d_ref, nm_ref, nv_ref = refs[depth:]
        layer = pl.program_id(0)
        for l in range(depth):
            @pl.when(layer == l)
            def _(p_ref=p_refs[l]):
                g = p_ref[0].astype(F32)
                for q in range(1, npart):
                    g = g + p_ref[q].astype(F32)
                m2 = ADAM_B1 * m_ref[...] + (1.0 - ADAM_B1) * g
                v2 = ADAM_B2 * v_ref[...] + (1.0 - ADAM_B2) * (g * g)
                upd = (m2 * c1) / (jnp.sqrt(v2 * c2) + ADAM_EPS) + ADAM_WD * w_ref[...]
                g_ref[...] = g
                d_ref[...] = -ADAM_LR * upd
                nm_ref[...] = m2
                nv_ref[...] = v2

    def part_spec(l):
        return pl.BlockSpec((npart, tr, c),
                            lambda ly, i: (0, jnp.where(ly == l, i, jnp.where(ly < l, 0, nr - 1)), 0))

    spec = pl.BlockSpec((None, tr, c), lambda ly, i: (ly, i, 0))
    return pl.pallas_call(
        kern, name=name, out_shape=[jax.ShapeDtypeStruct((depth, r, c), F32)] * 4, grid=(depth, nr),
        in_specs=[part_spec(l) for l in range(depth)] + [spec, spec, spec],
        out_specs=[spec] * 4,
        compiler_params=_params(dimension_semantics=("arbitrary", "arbitrary")),
    )(*parts, w, m, v)


def _sum_parts(parts, *, name):
    npart, r, c = parts.shape

    def kern(p_ref, o_ref):
        g = p_ref[0]
        for q in range(1, npart):
            g = g + p_ref[q]
        o_ref[...] = g

    return pl.pallas_call(kern, name=name, out_shape=jax.ShapeDtypeStruct((r, c), F32), compiler_params=_params())(parts)


class _Exchange:
    def __init__(self, arrays, gather):
        self.n = len(arrays)
        self.gather = gather
        self.out_shape = [jax.ShapeDtypeStruct(((NDEV,) + a.shape) if gather else a.shape, a.dtype) for a in arrays]
        self.scratch = [pltpu.SemaphoreType.DMA((self.n, NDEV - 1)), pltpu.SemaphoreType.DMA((self.n, NDEV - 1)),
                        pltpu.SemaphoreType.DMA((self.n,))]
        self.specs = [pl.BlockSpec(memory_space=pl.ANY)] * self.n

    def _copies(self, srcs, dsts, sems):
        send_sems, recv_sems, local_sems = sems
        x, y, c = lax.axis_index("x"), lax.axis_index("y"), lax.axis_index("c")
        me = 4 * x + 2 * y + c
        local = [pltpu.make_async_copy(srcs[a] if self.gather else srcs[a].at[me], dsts[a].at[me], local_sems.at[a])
                 for a in range(self.n)]
        remote = []
        for k in (1, 2, 4, 3, 5, 6, 7):
            px, py, pc = x ^ ((k >> 2) & 1), y ^ ((k >> 1) & 1), c ^ (k & 1)
            peer = 4 * px + 2 * py + pc
            for a in range(self.n):
                src = srcs[a] if self.gather else srcs[a].at[peer]
                mk = functools.partial(
                    pltpu.make_async_remote_copy, src_ref=src,
                    send_sem=send_sems.at[a, k - 1], recv_sem=recv_sems.at[a, k - 1],
                    device_id=(px, py, pc), device_id_type=pl.DeviceIdType.MESH)
                remote.append((mk(dst_ref=dsts[a].at[me]), mk(dst_ref=dsts[a].at[peer])))
        return local, remote

    def _gather_copies(self, srcs, dsts, sems):
        send_sems, recv_sems, local_sems = sems
        x, y, c = lax.axis_index("x"), lax.axis_index("y"), lax.axis_index("c")
        block = lambda px, py, pc: 4 * px + 2 * py + pc
        me = block(x, y, c)
        chips = [(1 - x, y), (x, 1 - y), (1 - x, 1 - y)]
        local = [pltpu.make_async_copy(srcs[a], dsts[a].at[me], local_sems.at[a]) for a in range(self.n)]
        own, passed = [], []
        for a in range(self.n):
            def copy(k, blk, to, src=None, a=a):
                return pltpu.make_async_remote_copy(
                    src_ref=dsts[a].at[blk] if src is None else src, dst_ref=dsts[a].at[blk],
                    send_sem=send_sems.at[a, k], recv_sem=recv_sems.at[a, k],
                    device_id=to, device_id_type=pl.DeviceIdType.MESH)
            sib = (x, y, 1 - c)
            own.append((copy(0, me, sib, srcs[a]), copy(0, block(x, y, 1 - c), sib)))
            for j, (px, py) in enumerate(chips):
                own.append((copy(1 + j, me, (px, py, c), srcs[a]), copy(1 + j, block(px, py, c), (px, py, c))))
            for j, (px, py) in enumerate(chips):
                passed.append((copy(4 + j, block(px, py, c), sib), copy(4 + j, block(px, py, 1 - c), sib)))
        return local, own, passed

    def start(self, srcs, dsts, sems):
        if self.gather:
            local, own, _ = self._gather_copies(srcs, dsts, sems)
            for cp in local:
                cp.start()
            for send, _ in own:
                send.start()
            return
        local, remote = self._copies(srcs, dsts, sems)
        for cp in local:
            cp.start()
        for send, _ in remote:
            send.start()

    def forward(self, srcs, dsts, sems):
        if not self.gather:
            return
        _, own, passed = self._gather_copies(srcs, dsts, sems)
        for a in range(self.n):
            for j in range(3):
                own[4 * a + 1 + j][1].wait_recv()
                passed[3 * a + j][0].start()

    def wait(self, srcs, dsts, sems):
        if self.gather:
            local, own, passed = self._gather_copies(srcs, dsts, sems)
            for a in range(self.n):
                own[4 * a][1].wait_recv()
            for _, arrival in passed:
                arrival.wait_recv()
            for send, _ in own + passed:
                send.wait_send()
            for cp in local:
                cp.wait()
            return
        local, remote = self._copies(srcs, dsts, sems)
        for send, arrival in remote:
            send.wait_send()
            arrival.wait_recv()
        for cp in local:
            cp.wait()


def _exchange(arrays, gather, *, name):
    ex = _Exchange(arrays, gather)
    n = ex.n

    def kern(*refs):
        srcs, dsts, sems = refs[:n], refs[n:2 * n], refs[2 * n:]
        ex.start(srcs, dsts, sems)
        ex.forward(srcs, dsts, sems)
        ex.wait(srcs, dsts, sems)

    return pl.pallas_call(
        kern, name=name, out_shape=ex.out_shape, in_specs=ex.specs, out_specs=ex.specs, scratch_shapes=ex.scratch,
        compiler_params=pltpu.CompilerParams(has_side_effects=True),
    )(*arrays)


def _pack(arrays):
    flat = jnp.concatenate([a.reshape(-1).astype(F32) for a in arrays])
    pad = (-flat.shape[0]) % (SUBLANES * LANES)
    return jnp.pad(flat, (0, pad)).reshape(-1, LANES)


def _unpack(buf, like):
    flat = buf.reshape(-1)
    out, off = [], 0
    for a in like:
        sz = math.prod(a.shape)
        out.append(flat[off:off + sz].reshape(a.shape))
        off += sz
    return out


def _row(v):
    return v.reshape(1, -1)


def _layer_fwd(x, mod, p, l, ride=None, on_receive=None, target=None):
    s, d = x.shape
    sw = d // 2
    nh = d // LANES
    shift_m, scale_m, gate_m, shift_f, scale_f, gate_f = mod
    n = lambda tag: f"{tag}{l}"
    sv = {}

    h1, = _rowwise(lambda xv, g, sc, sh: (xv * _rms(xv) * g) * (1.0 + sc) + sh,
                   [x], [p['g_pre_mix'], scale_m, shift_m], [(d, BF, 'tile')], name=n("pre_mix"))
    proj_a = _mm(h1, p['w_in_a'], name=n("proj_a"))
    flog = _mm(h1, p['w_in_f'], name=n("proj_f"))
    gates = _mm(h1, p['w_in_g'], name=n("proj_g"))

    y_s5, received = _s5_fwd(proj_a, p['wb_re'], p['wb_im'], p['wc_re'], p['wc_im'], p['lamb_re'], p['lamb_im'],
                             p['d_skip'], name=n("s5_fwd"), ride=ride('s5') if ride else None)
    if on_receive is not None:
        on_receive('s5', received)
    z, = _rowwise(_gelu, [y_s5], [], [(sw, BF, 'tile')], name=n("gelu"))
    tglu, ys = _mm_fused(z, [p['w_glu']], [y_s5], lambda tv, yv, b: (tv, _gelu(yv) * _sigmoid(tv + b)),
                         [F32, BF], rows=[p['b_glu']], name=n("glu_mm"))

    cumx = _cum_fwd(flog, p['b_f_row'], nh, name=n("cum_fwd"))
    ya, lse, *received = _attn_fwd(proj_a, cumx, sw, name=n("attn_fwd"), ride=ride('attn') if ride else None)
    if on_receive is not None:
        on_receive('attn', received)

    am = _mm(ys, p['w_pa'], name=n("pa_mm"))
    bm, merged = _mm_fused(ya, [p['w_pb']], [am, (gates, 0), (gates, 1)],
                           lambda b, a, ga, gb: (b, _sigmoid(ga) * a + _sigmoid(gb) * b), [F32, BF],
                           name=n("pb_mm"))
    ym = _mm(merged, p['w_o'], name=n("o_mm"))
    def post_mix_pre_ffn(xv, yv, g, gt, g2, sc, sh):
        x2v = xv + gt * (yv * _rms(yv) * g)
        return x2v, (x2v * _rms(x2v) * g2) * (1.0 + sc) + sh

    x2, h2 = _rowwise(post_mix_pre_ffn, [x, ym], [p['g_post_mix'], gate_m, p['g_pre_ffn'], scale_f, shift_f],
                      [(d, F32, 'tile'), (d, BF, 'tile')], name=n("post_mix_pre_ffn"))
    gt, up, act = _mm_fused(h2, [p['w_ffn_gate'], p['w_ffn_up']], [], lambda g, u: (g, u, _silu(g) * u),
                            [F32, F32, BF], tb=True, name=n("gate_up_mm"))
    yf = _mm(act, p['w_ffn_down'], name=n("down_mm"))
    sv.update(x=x, h1=h1, proj_a=proj_a, flog=flog, gates=gates, y_s5=y_s5, z=z, tglu=tglu, ys=ys, cumx=cumx,
              ya=ya, lse=lse, am=am, bm=bm, merged=merged, ym=ym, x2=x2, h2=h2, gt=gt, up=up,
              act=act, yf=yf)
    if target is None:
        x3, = _rowwise(lambda xv, yv, g, gt_: xv + gt_ * (yv * _rms(yv) * g),
                       [x2, yf], [p['g_post_ffn'], gate_f], [(d, F32, 'tile')], name=n("post_ffn"))
        return x3, sv

    def output_and_loss(xv, yv, tv, g, gt_):
        r = _rms(yv)
        nf = yv * r
        e = xv + gt_ * (nf * g) - tv
        dy = e * (1.0 / d)
        return (dy, _norm_bwd(dy * gt_ * g, nf, r), dy * (nf * g), dy * gt_ * nf,
                jnp.sum(e * e, axis=1, keepdims=True) * (0.5 / d))

    dy, dyf, dgate_f, dg, loss = _rowwise(
        output_and_loss, [x2, yf, target], [p['g_post_ffn'], gate_f],
        [(d, F32, 'tile'), (d, BF, 'tile'), (d, F32, 'sum'), (d, F32, 'sum'), (1, F32, 'sum')], name="output_loss")
    sv['post_ffn_bwd'] = (dyf, dgate_f, dg)
    return (dy, loss), sv


def _layer_bwd(dx3, sv, mod, p, l, make_ride=None, on_receive=None, carried=None, defer_tail=False):
    x, x2 = sv['x'], sv['x2']
    s, d = x.shape
    sw = d // 2
    nh = d // LANES
    shift_m, scale_m, gate_m, shift_f, scale_f, gate_f = mod
    n = lambda tag: f"{tag}{l}"
    gw, gs = {}, {}

    def post_bwd(dxo, yv, g, gate):
        r = _rms(yv)
        nf = yv * r
        dn = dxo * gate * g
        return _norm_bwd(dn, nf, r), dxo * (nf * g), dxo * gate * nf

    def pre_bwd(dh, dres, xv, g, sc):
        r = _rms(xv)
        xh = xv * r
        n3 = xh * g
        dn3 = dh * (1.0 + sc)
        return dres + _norm_bwd(dn3 * g, xh, r), dh, dh * n3, dn3 * xh

    if 'post_ffn_bwd' in sv:
        dyf, dgate_f, gs['g_post_ffn'] = sv['post_ffn_bwd']
    else:
        dyf, dgate_f, gs['g_post_ffn'] = _rowwise(
            post_bwd, [dx3, sv['yf']], [p['g_post_ffn'], gate_f],
            [(d, BF, 'tile'), (d, F32, 'sum'), (d, F32, 'sum')], name=n("post_ffn_bwd"))
    gw['w_ffn_down'] = _mm(sv['act'], dyf, ta=True, out_dtype=BF, tm=1408, name=n("down_bwd_w"))

    def swiglu_bwd(da, g, u):
        sg = _sigmoid(g)
        return da * u * (sg * (1.0 + g * (1.0 - sg))), da * (g * sg)

    dgt, dup = _mm_fused(dyf, [p['w_ffn_down']], [sv['gt'], sv['up']], swiglu_bwd, [BF, BF], tb=True,
                         name=n("down_bwd_x"))
    dh2 = _mm(dgt, p['w_ffn_gate'], tm=1024, second=(dup, p['w_ffn_up']), name=n("gate_up_bwd_x"))
    gw['w_ffn_gate'] = _mm(dgt, sv['h2'], ta=True, out_dtype=BF, tm=1408, name=n("gate_bwd_w"))
    gw['w_ffn_up'] = _mm(dup, sv['h2'], ta=True, out_dtype=BF, tm=1408, name=n("up_bwd_w"))
    def pre_ffn_post_mix_bwd(dh, dres, xv, yv, g, sc, g2, gate):
        dx2v, dsh, dsc, dg = pre_bwd(dh, dres, xv, g, sc)
        return (dx2v, dsh, dsc, dg) + post_bwd(dx2v, yv, g2, gate)

    dx2, dshift_f, dscale_f, gs['g_pre_ffn'], dym, dgate_m, gs['g_post_mix'] = _rowwise(
        pre_ffn_post_mix_bwd, [dh2, dx3, x2, sv['ym']], [p['g_pre_ffn'], scale_f, p['g_post_mix'], gate_m],
        [(d, F32, 'tile'), (d, F32, 'sum'), (d, F32, 'sum'), (d, F32, 'sum'),
         (d, BF, 'tile'), (d, F32, 'sum'), (d, F32, 'sum')], name=n("pre_ffn_post_mix_bwd"))
    gw['w_o'] = _mm(sv['merged'], dym, ta=True, out_dtype=BF, name=n("o_bwd_w"))

    def merge_bwd(dm, a, b, ga, gb):
        sa, sb = _sigmoid(ga), _sigmoid(gb)
        return dm * sa, dm * sb, dm * a * sa * (1.0 - sa), dm * b * sb * (1.0 - sb)

    da_, db_, dga, dgb = _mm_fused(dym, [p['w_o']], [sv['am'], sv['bm'], (sv['gates'], 0), (sv['gates'], 1)],
                                   merge_bwd, [BF] * 4, tb=True, name=n("o_bwd_x"))
    dys = _mm(da_, p['w_pa'], tb=True, name=n("pa_bwd_x"))
    gw['w_pa'] = _mm(sv['ys'], da_, ta=True, out_dtype=BF, name=n("pa_bwd_w"))
    dya = _mm(db_, p['w_pb'], tb=True, name=n("pb_bwd_x"))
    gw['w_pb'] = _mm(sv['ya'], db_, ta=True, out_dtype=BF, name=n("pb_bwd_w"))

    sent = list(gw)
    dq, dk, dv, dkc, dqc, *received = _attn_bwd(
        sv['proj_a'], dya, sv['ya'], sv['lse'], sv['cumx'], sw, name=n("attn_bwd"),
        ride=make_ride({k: gw[k] for k in sent}) if make_ride is not None else None)
    if on_receive is not None:
        on_receive(sent, received)
    dcum = jnp.stack([-dkc.reshape(nh, s), dqc.reshape(nh, s)])
    dflog, dbf = _cum_bwd(dcum, sv['flog'], p['b_f_col'], name=n("cum_bwd"))
    gs['b_f'] = dbf.reshape(nh)

    def glu_bwd(dy_, yv, tv, b):
        zv = _gelu(yv)
        sg = _sigmoid(tv + b)
        dt = dy_ * zv * sg * (1.0 - sg)
        return dt, dy_ * sg, dt

    dt, dz1, gs['b_glu'] = _rowwise(glu_bwd, [dys, sv['y_s5'], sv['tglu']], [p['b_glu']],
                                    [(sw, BF, 'tile'), (sw, F32, 'tile'), (sw, F32, 'sum')], name=n("glu_bwd"))
    dy_s5, = _mm_fused(dt, [p['w_glu']], [dz1, sv['y_s5']], lambda dz2, a, yv: ((a + dz2) * _gelu_grad(yv),),
                       [F32], tb=True, name=n("glu_bwd_x"))
    gw['w_glu'] = _mm(sv['z'], dt, ta=True, out_dtype=BF, name=n("glu_bwd_w"))
    du, dwbr, dwbi, dwcr, dwci, dlr, dli, gs['d_skip'], *received = _s5_bwd(
        sv['proj_a'], dy_s5, p['wb_re'], p['wb_im'], p['wc_re'], p['wc_im'], p['lamb_re'], p['lamb_im'], p['d_skip'],
        name=n("s5_bwd"), ride=make_ride(carried[1]) if carried else None)
    if carried:
        carried[0](list(carried[1]), received)
    g_ = sw // SSM_H
    pst = p['lamb_re'].shape[1] // g_
    gs['lamb_re'], gs['lamb_im'] = dlr.reshape(g_, pst), dli.reshape(g_, pst)
    gs['bbar_re'] = _s5_block_grads(dwbr, SSM_H, pst, False)
    gs['bbar_im'] = _s5_block_grads(dwbi, SSM_H, pst, False)
    gs['c_re'] = _s5_block_grads(dwcr, pst, SSM_H, True).transpose(0, 2, 1)
    gs['c_im'] = _s5_block_grads(dwci, pst, SSM_H, True).transpose(0, 2, 1)

    dproj = jnp.concatenate([du.astype(BF), dq.astype(BF), dk.astype(BF), dv.astype(BF), dflog, dga, dgb], axis=1)
    gw['w_in'] = _mm(sv['h1'], dproj, ta=True, out_dtype=BF, tn=1408, name=n("proj_bwd_w"))
    if make_ride is not None:
        gw = {k: g for k, g in gw.items() if k not in sent}
    if make_ride is not None and not defer_tail:
        dh1, received = _mm(dproj, p['w_in_all'], tb=True, tk=1408, name=n("proj_bwd_x"), ride=make_ride(gw))
        on_receive(list(gw), received)
        gw = {}
    else:
        dh1 = _mm(dproj, p['w_in_all'], tb=True, tk=1408, name=n("proj_bwd_x"))
    dx, dshift_m, dscale_m, gs['g_pre_mix'] = _rowwise(
        pre_bwd, [dh1, dx2, x], [p['g_pre_mix'], scale_m],
        [(d, F32, 'tile'), (d, F32, 'sum'), (d, F32, 'sum'), (d, F32, 'sum')], name=n("pre_mix_bwd"))
    dmod = [dshift_m, dscale_m, dgate_m, dshift_f, dscale_f, dgate_f]
    return dx, gw, dmod, gs


def _unshard(k, blocks):
    if k in COL_SHARDED:
        return blocks.transpose(1, 0, 2).reshape(blocks.shape[1], NDEV * blocks.shape[2])
    return blocks.reshape(NDEV * blocks.shape[1], blocks.shape[2])


def _to_slabs(k, g):
    if k == 'w_in':
        d = g.shape[0]
        nh = d // LANES
        g = jnp.concatenate([g[:, :2 * d + nh], g[:, 2 * d + LANES:]], axis=1)
    if k in COL_SHARDED:
        return g.reshape(g.shape[0], NDEV, g.shape[1] // NDEV).transpose(1, 0, 2)
    return g.reshape(NDEV, g.shape[0] // NDEV, g.shape[1])


def _prep_w_in(w_in):
    d = w_in.shape[0]
    nh = d // LANES
    fcol = 2 * d
    p = {}
    p['w_in_a'] = w_in[:, :fcol]
    p['w_in_f'] = jnp.pad(w_in[:, fcol:fcol + nh], ((0, 0), (0, LANES - nh)))
    p['w_in_g'] = w_in[:, fcol + nh:]
    p['w_in_all'] = jnp.concatenate([p['w_in_a'], p['w_in_f'], p['w_in_g']], axis=1)
    return p


def _prep_small(small):
    nh = small['b_f'].shape[0]
    p = {}
    for k in ('g_pre_mix', 'g_post_mix', 'g_pre_ffn', 'g_post_ffn', 'd_skip', 'b_glu'):
        p[k] = _row(small[k])
    p['b_f_row'] = jnp.pad(_row(small['b_f']), ((0, 0), (0, LANES - nh)))
    p['b_f_col'] = small['b_f'].reshape(nh, 1)
    lbr, lbi, bbr, bbi = _s5_discretise(small['lam_re'], small['lam_im'], small['log_dt'], small['b_re'], small['b_im'])
    p['lamb_re'], p['lamb_im'] = _row(lbr), _row(lbi)
    p['wb_re'] = _s5_operand(bbr, True).astype(BF)
    p['wb_im'] = _s5_operand(bbi, True).astype(BF)
    p['wc_re'] = _s5_operand(small['c_re'].transpose(0, 2, 1), False).astype(BF)
    p['wc_im'] = _s5_operand(small['c_im'].transpose(0, 2, 1), False).astype(BF)
    return p


def _local_step(x, target, mods, ps, small, hooks=None):
    depth = len(ps)
    s, d = x.shape
    hooks = hooks or {}
    saved = []
    h = x
    for l in range(depth):
        h, sv = _layer_fwd(h, mods[l], ps[l], l, ride=functools.partial(hooks['fwd_ride'], l) if hooks else None,
                           on_receive=functools.partial(hooks['fwd_recv'], l) if hooks else None,
                           target=target if l == depth - 1 else None)
        saved.append(sv)

    dy, loss = h
    dmods, gss = [None] * depth, [None] * depth
    unsent = {}
    carried = None
    for l in range(depth - 1, -1, -1):
        def on_receive(names, results, l=l):
            hooks['bwd_recv']([(k, l) for k in names], results)

        dy, gw, dmods[l], gs = _layer_bwd(dy, saved[l], mods[l], ps[l], l,
                                          make_ride=hooks['bwd_ride'] if hooks else None,
                                          on_receive=on_receive if hooks else None,
                                          carried=carried, defer_tail=bool(hooks) and l > 0)
        if hooks and l > 0:
            carried = (on_receive, gw)
        else:
            unsent.update({(k, l): g for k, g in gw.items()})
        sm = small[l]
        _, vjp = jax.vjp(_s5_discretise, sm['lam_re'], sm['lam_im'], sm['log_dt'], sm['b_re'], sm['b_im'])
        gs['lam_re'], gs['lam_im'], gs['log_dt'], gs['b_re'], gs['b_im'] = vjp(
            (gs.pop('lamb_re'), gs.pop('lamb_im'), gs.pop('bbar_re'), gs.pop('bbar_im')))
        gss[l] = gs
    return loss, dy, unsent, dmods, gss


SMALL_LOCAL = ['g_pre_mix', 'g_post_mix', 'g_pre_ffn', 'g_post_ffn', 'lam_re', 'lam_im', 'log_dt', 'b_re', 'b_im',
               'c_re', 'c_im', 'd_skip', 'b_glu', 'b_f']


def kernel(x, c, w_ada, b_ada, g_pre_mix, g_post_mix, g_pre_ffn, g_post_ffn, w_in, lam_re, lam_im, log_dt, b_re, b_im, c_re, c_im, d_skip, w_glu, b_glu, b_f, w_pa, w_pb, w_o, w_ffn_gate, w_ffn_up, w_ffn_down, loss_target, m_w_ada, m_b_ada, m_g_pre_mix, m_g_post_mix, m_g_pre_ffn, m_g_post_ffn, m_w_in, m_lam_re, m_lam_im, m_log_dt, m_b_re, m_b_im, m_c_re, m_c_im, m_d_skip, m_w_glu, m_b_glu, m_b_f, m_w_pa, m_w_pb, m_w_o, m_w_ffn_gate, m_w_ffn_up, m_w_ffn_down, v_w_ada, v_b_ada, v_g_pre_mix, v_g_post_mix, v_g_pre_ffn, v_g_post_ffn, v_w_in, v_lam_re, v_lam_im, v_log_dt, v_b_re, v_b_im, v_c_re, v_c_im, v_d_skip, v_w_glu, v_b_glu, v_b_f, v_w_pa, v_w_pb, v_w_o, v_w_ffn_gate, v_w_ffn_up, v_w_ffn_down):
    args = dict(locals())
    view = lambda k, a: jnp.swapaxes(a, -1, -2) if k in TRANSPOSED else a
    W = {k: view(k, args[k]) for k in WEIGHTS}
    M = {k: view(k, args['m_' + k]) for k in WEIGHTS}
    V = {k: view(k, args['v_' + k]) for k in WEIGHTS}
    depth, d, ncol = w_ada.shape
    s = x.shape[1]
    me = 4 * lax.axis_index("x") + 2 * lax.axis_index("y") + lax.axis_index("c")

    first = ['w_in', 'w_glu']
    c_all, *first_blocks = _exchange([jnp.pad(c, ((0, SUBLANES - 1), (0, 0)))] + [W[k][0].astype(BF) for k in first],
                                     True, name="gather_first")
    c_all = c_all[:, 0, :]

    cond, = _rowwise(_silu, [c_all], [], [(d, F32, 'tile')], name="cond")
    mod_part = jnp.stack([_mm(cond, w_ada[l], name=f"ada_mm{l}") for l in range(depth)], axis=1)
    mod_recv, = _exchange([mod_part.reshape(NDEV, depth, 1, ncol)], False, name="scatter_mod")
    mod_cat = mod_recv.reshape(NDEV, depth, ncol).transpose(1, 0, 2).reshape(depth, NDEV * ncol)
    mod, = _rowwise(lambda a, b: a + b, [mod_cat, b_ada], [], [(NDEV * ncol, F32, 'tile')], name="mod_bias")
    mods = [[mod[l:l + 1, i * d:(i + 1) * d] for i in range(6)] for l in range(depth)]

    small = [{k: W[k][l] for k in SMALL_LOCAL} for l in range(depth)]
    ps = [_prep_small(small[l]) for l in range(depth)]
    rest = [k for k in BIG if k not in first]
    riding = [{'attn': [(k, l) for k in rest], 's5': [(k, l + 1) for k in first if l + 1 < depth]}
              for l in range(depth)]

    def take_weights(keys, results):
        for (k, l), blocks in zip(keys, results):
            full = _unshard(k, blocks)
            ps[l].update(_prep_w_in(full) if k == 'w_in' else {k: full})

    take_weights([(k, 0) for k in first], first_blocks)

    def fwd_ride(l, where):
        blocks = [W[k][ll].astype(BF) for k, ll in riding[l][where]]
        return (_Exchange(blocks, True), blocks) if blocks else None

    grad_parts = {}

    def bwd_ride(grads):
        slabs = [_to_slabs(k, g) for k, g in grads.items()]
        return _Exchange(slabs, False), slabs

    hooks = dict(fwd_ride=fwd_ride, fwd_recv=lambda l, where, results: take_weights(riding[l][where], results),
                 bwd_ride=bwd_ride, bwd_recv=lambda keys, results: grad_parts.update(zip(keys, results)))

    loss, dx, unsent, dmods, gss = _local_step(x[0], loss_target[0], mods, ps, small, hooks)
    assert not unsent
    out = {}
    for k in BIG:
        out[k] = _adamw([grad_parts[(k, l)] for l in range(depth)], W[k], M[k], V[k], name=f"adamw_{k}")

    dmod_mine = jnp.stack([jnp.concatenate(dmods[l], axis=1)[0] for l in range(depth)])
    small_mine = [dmod_mine] + [jnp.stack([gss[l][k] for l in range(depth)]) for k in SMALL_LOCAL] + [loss]
    parts, = _exchange([_pack(small_mine)], True, name="gather_small")
    summed = _sum_parts(parts, name="sum_small")
    names = ['b_ada'] + SMALL_LOCAL
    *small_grads, loss = _unpack(summed, [W[k] for k in names] + [loss])
    loss = loss[0, 0]
    for k, g in zip(names, small_grads):
        shp = W[k].shape
        rows = lambda a: a.reshape(depth, -1, shp[-1])
        res = _adamw([rows(g)[l][None] for l in range(depth)], rows(W[k]), rows(M[k]), rows(V[k]), name=f"adamw_{k}")
        out[k] = [a.reshape(shp) for a in res]

    dmod_all = parts.reshape(NDEV, -1)[:, :depth * 6 * d].reshape(NDEV, depth, 6 * d)
    dmod_cols = lax.dynamic_slice_in_dim(dmod_all, me * ncol, ncol, axis=2)
    g_ada = [_mm(cond, dmod_cols[:, l], ta=True, precision=HI, name=f"ada_bwd{l}")[None] for l in range(depth)]
    out['w_ada'] = _adamw(g_ada, w_ada, m_w_ada, v_w_ada, name="adamw_w_ada")

    return (loss, dx[None], *[view(k, out[k][i]) for i in range(4) for k in WEIGHTS])
```

```python
import functools
import math

import jax
import jax.numpy as jnp
from jax import lax
from jax.experimental import pallas as pl
from jax.experimental.pallas import tpu as pltpu

F32 = jnp.float32
BF = jnp.bfloat16
NDEV = 8
LANES = 128
SUBLANES = 8
VMEM_LIMIT = 48 * 1024 * 1024

SSM_H = 16
HEAD_DIM = 64
RMS_EPS = 1e-6
EIG_CLIP = 1e-4
ADAM_LR = 0.001
ADAM_B1 = 0.9
ADAM_B2 = 0.999
ADAM_EPS = 1e-08
ADAM_WD = 0.01
ADAM_STEP = 10
NEG = -1e30
HI = lax.Precision.HIGHEST

WEIGHTS = ['w_ada', 'b_ada', 'g_pre_mix', 'g_post_mix', 'g_pre_ffn', 'g_post_ffn', 'w_in', 'lam_re', 'lam_im',
           'log_dt', 'b_re', 'b_im', 'c_re', 'c_im', 'd_skip', 'w_glu', 'b_glu', 'b_f', 'w_pa', 'w_pb', 'w_o',
           'w_ffn_gate', 'w_ffn_up', 'w_ffn_down']
TRANSPOSED = ['w_ffn_gate', 'w_ffn_up', 'b_re', 'b_im']
COL_SHARDED = ['w_in', 'w_pa', 'w_pb']
ROW_SHARDED = ['w_glu', 'w_o', 'w_ffn_down', 'w_ffn_gate', 'w_ffn_up']
BIG = COL_SHARDED + ROW_SHARDED
SMALL = ['b_ada', 'g_pre_mix', 'g_post_mix', 'g_pre_ffn', 'g_post_ffn', 'lam_re', 'lam_im', 'log_dt', 'b_re',
         'b_im', 'c_re', 'c_im', 'd_skip', 'b_glu', 'b_f']


def _fit(dim, target, align):
    if dim <= target:
        return dim
    t = (target // align) * align
    while t >= align:
        if dim % t == 0:
            return t
        t -= align
    return dim


def _params(**kw):
    return pltpu.CompilerParams(vmem_limit_bytes=VMEM_LIMIT, **kw)


def _mm(a, b, *, ta=False, tb=False, out_dtype=F32, tm=None, tn=512, tk=2048, precision=None, name, ride=None,
        second=None):
    m, k = (a.shape[1], a.shape[0]) if ta else a.shape
    n = b.shape[0] if tb else b.shape[1]
    assert (b.shape[1] if tb else b.shape[0]) == k
    tm = _fit(m, tm or (1024 if ta else 2048), LANES if ta else 16)
    tn = _fit(n, tn, LANES)
    tk = _fit(k, tk, LANES)
    nk = k // tk
    grid = (m // tm, n // tn, nk)
    dims = (((0 if ta else 1,), (1 if tb else 0,)), ((), ()))
    ex, ex_arrays = ride if ride is not None else (None, [])

    pairs = [(a, b)] + ([second] if second is not None else [])

    def kern(*refs):
        ab_refs, (o_ref,), comm, scratch = _ride_split(ex, refs, 2 * len(pairs), 1)
        step = (pl.program_id(0) * grid[1] + pl.program_id(1)) * grid[2] + pl.program_id(2)
        if ex is not None:
            @pl.when(step == 0)
            def _():
                ex.start(*comm)

            @pl.when(step == (grid[0] * grid[1] * grid[2]) // 2)
            def _():
                ex.forward(*comm)

        p = None
        for a_ref, b_ref in zip(ab_refs[::2], ab_refs[1::2]):
            av, bv = a_ref[...], b_ref[...]
            if precision is None:
                av, bv = av.astype(BF), bv.astype(BF)
            q = lax.dot_general(av, bv, dims, preferred_element_type=F32, precision=precision)
            p = q if p is None else p + q
        if nk == 1:
            o_ref[...] = p.astype(out_dtype)
        else:
            acc_ref, = scratch
            kk = pl.program_id(2)

            @pl.when(kk == 0)
            def _():
                acc_ref[...] = p

            @pl.when(kk > 0)
            def _():
                acc_ref[...] += p

            @pl.when(kk == nk - 1)
            def _():
                o_ref[...] = acc_ref[...].astype(out_dtype)

        if ex is not None:
            @pl.when(step == grid[0] * grid[1] * grid[2] - 1)
            def _():
                ex.wait(*comm)

    a_spec = pl.BlockSpec((tk, tm), lambda i, j, kk: (kk, i)) if ta else pl.BlockSpec((tm, tk), lambda i, j, kk: (i, kk))
    b_spec = pl.BlockSpec((tn, tk), lambda i, j, kk: (j, kk)) if tb else pl.BlockSpec((tk, tn), lambda i, j, kk: (kk, j))
    res = pl.pallas_call(
        kern, name=name,
        out_shape=[jax.ShapeDtypeStruct((m, n), out_dtype)] + (ex.out_shape if ex else []),
        grid=grid,
        in_specs=[a_spec, b_spec] * len(pairs) + (ex.specs if ex else []),
        out_specs=[pl.BlockSpec((tm, tn), lambda i, j, kk: (i, j))] + (ex.specs if ex else []),
        scratch_shapes=(ex.scratch if ex else []) + ([] if nk == 1 else [pltpu.VMEM((tm, tn), F32)]),
        compiler_params=_params(dimension_semantics=("arbitrary",) * 3 if ex else ("parallel", "parallel", "arbitrary"),
                                has_side_effects=ex is not None),
    )(*[x for pair in pairs for x in pair], *ex_arrays)
    return (res[0], res[1:]) if ex else res[0]


def _mm_fused(a, bs, extras, fn, out_dtypes, *, rows=(), tb=False, tm=2048, tn=256, name):
    m, k = a.shape
    n = bs[0].shape[0] if tb else bs[0].shape[1]
    tm = _fit(m, tm, 16)
    tn = _fit(n, tn, LANES)
    extras = [e if isinstance(e, tuple) else (e, 0) for e in extras]
    nb, ne, nr = len(bs), len(extras), len(rows)
    dims = (((1,), (1 if tb else 0,)), ((), ()))

    def kern(*refs):
        av = refs[0][...].astype(BF)
        prods = [lax.dot_general(av, r[...].astype(BF), dims, preferred_element_type=F32) for r in refs[1:1 + nb]]
        res = fn(*prods, *[r[...] for r in refs[1 + nb:1 + nb + ne + nr]])
        for o_ref, r, dt in zip(refs[1 + nb + ne + nr:], res, out_dtypes):
            o_ref[...] = r.astype(dt)

    tile = pl.BlockSpec((tm, tn), lambda i, j: (i, j))
    b_spec = pl.BlockSpec((tn, k), lambda i, j: (j, 0)) if tb else pl.BlockSpec((k, tn), lambda i, j: (0, j))
    return pl.pallas_call(
        kern, name=name, out_shape=[jax.ShapeDtypeStruct((m, n), dt) for dt in out_dtypes],
        grid=(m // tm, n // tn),
        in_specs=[pl.BlockSpec((tm, k), lambda i, j: (i, 0))] + [b_spec] * nb
        + [pl.BlockSpec((tm, tn), lambda i, j, c=c: (i, j + c * (n // tn))) for _, c in extras]
        + [pl.BlockSpec((1, tn), lambda i, j: (0, j))] * nr,
        out_specs=[tile] * len(out_dtypes),
        compiler_params=_params(dimension_semantics=("parallel", "parallel")),
    )(a, *bs, *[e for e, _ in extras], *rows)


def _rowwise(fn, tiles, params, outs, *, tr=1024, name):
    tiles = [t if isinstance(t, tuple) else (t, t.shape[1], 0) for t in tiles]
    s = tiles[0][0].shape[0]
    row_bytes = sum(w * t.dtype.itemsize for t, w, _ in tiles)
    row_bytes += sum(w * jnp.dtype(dt).itemsize for w, dt, kind in outs if kind == 'tile')
    tr = _fit(s, max(16, min(tr, VMEM_LIMIT // 2 // (2 * row_bytes) // 16 * 16)), 16)
    nt, npar = len(tiles), len(params)

    def kern(*refs):
        i = pl.program_id(0)
        res = fn(*[r[...] for r in refs[:nt + npar]])
        if not isinstance(res, (tuple, list)):
            res = (res,)
        for (w, dt, kind), o_ref, r in zip(outs, refs[nt + npar:], res):
            if kind == 'tile':
                o_ref[...] = r.astype(dt)
            else:
                part = jnp.sum(r.astype(F32), axis=0, keepdims=True)

                @pl.when(i == 0)
                def _(o_ref=o_ref, part=part):
                    o_ref[...] = part

                @pl.when(i > 0)
                def _(o_ref=o_ref, part=part):
                    o_ref[...] += part

    def tile_spec(w, cb):
        return pl.BlockSpec((tr, w), lambda i: (i, cb))

    in_specs = [tile_spec(w, cb) for _, w, cb in tiles]
    in_specs += [pl.BlockSpec(p.shape, lambda i, nd=p.ndim: (0,) * nd) for p in params]
    out_shape, out_specs = [], []
    for w, dt, kind in outs:
        if kind == 'tile':
            out_shape.append(jax.ShapeDtypeStruct((s, w), dt))
            out_specs.append(pl.BlockSpec((tr, w), lambda i: (i, 0)))
        else:
            out_shape.append(jax.ShapeDtypeStruct((1, w), F32))
            out_specs.append(pl.BlockSpec((1, w), lambda i: (0, 0)))
    res = pl.pallas_call(
        kern, name=name, out_shape=out_shape, grid=(s // tr,), in_specs=in_specs, out_specs=out_specs,
        compiler_params=_params(dimension_semantics=("arbitrary",)),
    )(*[t[0] for t in tiles], *params)
    return res


def _sigmoid(z):
    return 1.0 / (1.0 + jnp.exp(-z))


def _silu(z):
    return z * _sigmoid(z)


_GELU_K = math.sqrt(2.0 / math.pi)


def _gelu(y):
    return 0.5 * y * (1.0 + jnp.tanh(_GELU_K * (y + 0.044715 * y * y * y)))


def _gelu_grad(y):
    th = jnp.tanh(_GELU_K * (y + 0.044715 * y * y * y))
    return 0.5 * (1.0 + th) + 0.5 * y * (1.0 - th * th) * _GELU_K * (1.0 + 3.0 * 0.044715 * y * y)


def _rms(x):
    return lax.rsqrt(jnp.mean(x * x, axis=-1, keepdims=True) + RMS_EPS)


def _norm_bwd(dn, xhat, r):
    return r * (dn - xhat * jnp.mean(dn * xhat, axis=-1, keepdims=True))


def _cum_fwd(flog, bf_row, nh, *, name):
    s = flog.shape[0]
    w = nh * HEAD_DIM
    t = _fit(s, 256, SUBLANES)

    def kern(f_ref, b_ref, o_ref, carry_ref):
        i = pl.program_id(0)

        @pl.when(i == 0)
        def _():
            carry_ref[...] = jnp.zeros_like(carry_ref)

        z = f_ref[...] + b_ref[...]
        logf = jnp.minimum(z, 0.0) - jnp.log(1.0 + jnp.exp(-jnp.abs(z)))
        hh = lax.broadcasted_iota(jnp.int32, (LANES, w), 0)
        cc = lax.broadcasted_iota(jnp.int32, (LANES, w), 1)
        expand = (cc // HEAD_DIM == hh).astype(F32)
        lx = jnp.dot(logf, expand, preferred_element_type=F32, precision=HI)
        rr = lax.broadcasted_iota(jnp.int32, (t, t), 0)
        kk = lax.broadcasted_iota(jnp.int32, (t, t), 1)
        tri = (kk <= rr).astype(F32)
        cum = jnp.dot(tri, lx, preferred_element_type=F32, precision=HI) + carry_ref[...]
        o_ref[...] = cum
        carry_ref[...] = cum[t - 1:t, :]

    return pl.pallas_call(
        kern, name=name, out_shape=jax.ShapeDtypeStruct((s, w), F32), grid=(s // t,),
        in_specs=[pl.BlockSpec((t, LANES), lambda i: (i, 0)), pl.BlockSpec((1, LANES), lambda i: (0, 0))],
        out_specs=pl.BlockSpec((t, w), lambda i: (i, 0)),
        scratch_shapes=[pltpu.VMEM((1, w), F32)],
        compiler_params=_params(dimension_semantics=("arbitrary",)),
    )(flog, bf_row)


def _cum_bwd(dcrow, flog, bf_col, *, name):
    _, nh, s = dcrow.shape
    t = _fit(s, 512, LANES)
    nb = s // t

    def kern(d_ref, f_ref, b_ref, df_ref, db_ref):
        rr = lax.broadcasted_iota(jnp.int32, (t, t), 0)
        kk = lax.broadcasted_iota(jnp.int32, (t, t), 1)
        upper = (rr >= kk).astype(F32)
        pick = (lax.broadcasted_iota(jnp.int32, (nh, LANES), 0)
                == lax.broadcasted_iota(jnp.int32, (nh, LANES), 1)).astype(F32)
        carry = jnp.zeros((nh, 1), F32)
        db = jnp.zeros((nh, 1), F32)
        for blk in range(nb - 1, -1, -1):
            sl = slice(blk * t, (blk + 1) * t)
            rc = jnp.dot(d_ref[0, :, sl] + d_ref[1, :, sl], upper, preferred_element_type=F32, precision=HI) + carry
            carry = rc[:, 0:1]
            frow = lax.dot_general(pick, f_ref[sl, :], (((1,), (1,)), ((), ())), preferred_element_type=F32,
                                   precision=HI)
            df = rc * _sigmoid(-(frow + b_ref[...]))
            df_ref[sl, :] = lax.dot_general(df, pick, (((0,), (0,)), ((), ())), preferred_element_type=F32,
                                            precision=HI).astype(BF)
            db = db + jnp.sum(df, axis=1, keepdims=True)
        db_ref[...] = db

    return pl.pallas_call(
        kern, name=name,
        out_shape=[jax.ShapeDtypeStruct((s, LANES), BF), jax.ShapeDtypeStruct((nh, 1), F32)],
        compiler_params=_params(),
    )(dcrow, flog, bf_col)


def _ride_split(ex, refs, n_in, n_out):
    n = ex.n if ex is not None else 0
    own_in, srcs = refs[:n_in], refs[n_in:n_in + n]
    own_out, dsts = refs[n_in + n:n_in + n + n_out], refs[n_in + n + n_out:n_in + 2 * n + n_out]
    sems = refs[n_in + 2 * n + n_out:n_in + 2 * n + n_out + 3] if n else ()
    rest = refs[n_in + 2 * n + n_out + (3 if n else 0):]
    return own_in, own_out, (srcs, dsts, sems), rest


ATTN_TILE = 512
ATTN_TILE_BWD = 256
ATTN_STRIP = 32
BIAS_LANES = 3


def _head_masks(rows):
    lane = lax.broadcasted_iota(jnp.int32, (rows, LANES), 1)
    return [(lane >= HEAD_DIM * e) & (lane < HEAD_DIM * (e + 1)) for e in range(2)]


def _augment(feat, bias, e, *, bias_slot, ones_slot):
    rows = feat.shape[0]
    lane = lax.broadcasted_iota(jnp.int32, (rows, LANES), 1)
    own = (lane >= HEAD_DIM * e) & (lane < HEAD_DIM * (e + 1))
    off = lane - HEAD_DIM * (1 - e)
    out = jnp.where(own, feat, 0.0)
    if ones_slot is not None:
        out = jnp.where((off >= ones_slot * BIAS_LANES) & (off < (ones_slot + 1) * BIAS_LANES), 1.0, out)
    if bias is not None:
        rest = pltpu.roll(bias, HEAD_DIM, 1)
        for term in range(BIAS_LANES):
            part = rest.astype(BF).astype(F32)
            out = jnp.where(off == bias_slot * BIAS_LANES + term, part, out)
            rest = rest - part
    return out.astype(BF)


def _two_slot_pipeline(m, scores, tile):
    scores(0, 0)

    def pair(n, carry):
        k = 2 * n
        scores(k + 1, 1)
        tile(k, 0, False)
        scores(k + 2, 0)
        tile(k + 1, 1, False)
        return carry

    lax.fori_loop(0, m // 2, pair, 0)

    @pl.when(m % 2 == 0)
    def _():
        tile(m, 0, True)

    @pl.when(m % 2 == 1)
    def _():
        scores(m, 1)
        tile(m - 1, 0, False)
        tile(m, 1, True)


def _attn_fwd(proj, cumx, qcol, *, name, ride=None):
    s = proj.shape[0]
    w = cumx.shape[1]
    nhp = w // LANES
    t = _fit(s, ATTN_TILE, LANES)
    nq = s // t
    strip = _fit(t, ATTN_STRIP, 16)
    scale = HEAD_DIM ** -0.5
    qb, kb, vb = qcol // LANES, (qcol + w) // LANES, (qcol + 2 * w) // LANES
    ex, ex_arrays = ride if ride is not None else (None, [])
    nt_dims = (((1,), (1,)), ((), ()))

    def kern(*refs):
        own_in, (o_ref, l_ref), comm, scratch = _ride_split(ex, refs, 5, 2)
        q_ref, k_ref, v_ref, cxq_ref, cxk_ref = own_in
        ka_ref, vat_ref, s0_ref, s1_ref, p_ref, m_ref, acc_ref = scratch
        s_refs = (s0_ref, s1_ref)
        i = pl.program_id(1)
        if ex is not None:
            @pl.when((pl.program_id(0) == 0) & (i == 0))
            def _():
                ex.start(*comm)

            @pl.when((pl.program_id(0) == nhp - 1) & (i == 0))
            def _():
                ex.forward(*comm)

        msks = _head_masks(t)

        @pl.when(i == 0)
        def _():
            def build(c, carry):
                rows = pl.ds(pl.multiple_of(c * t, LANES), t)
                k2, v2, cx = k_ref[rows, :], v_ref[rows, :], cxk_ref[rows, :]
                for e in range(2):
                    ka_ref[e, rows, :] = _augment(k2, -cx, e, bias_slot=1, ones_slot=0)
                    vat_ref[e, :, rows] = jnp.where(msks[e], v2, 1.0).T.astype(BF)
                return carry
            lax.fori_loop(0, nq, build, 0)

        q2 = q_ref[...] * scale
        qa = [_augment(q2, cxq_ref[...], e, bias_slot=0, ones_slot=1) for e in range(2)]
        m_ref[...] = jnp.full(m_ref.shape, NEG, F32)
        acc_ref[...] = jnp.zeros(acc_ref.shape, F32)
        slabs = strip // SUBLANES

        def scores(j, slot):
            rows_k = pl.ds(pl.multiple_of(j * t, LANES), t)
            for e in range(2):
                st = lax.dot_general(ka_ref[e, rows_k, :], qa[e], nt_dims, preferred_element_type=F32)
                s_refs[slot][e] = st.reshape(t // SUBLANES, SUBLANES, t)

        def tile(j, slot, diagonal):
            rows_k = pl.ds(pl.multiple_of(j * t, LANES), t)
            s_ref = s_refs[slot]
            for e in range(2):
                mx = jnp.full((SUBLANES, t), NEG, F32)
                for r in range(t // strip):
                    sl = slice(r * slabs, (r + 1) * slabs)
                    sv = s_ref[e,sl]
                    if diagonal:
                        shape = (slabs, SUBLANES, t)
                        key = (r * strip + lax.broadcasted_iota(jnp.int32, shape, 0) * SUBLANES
                               + lax.broadcasted_iota(jnp.int32, shape, 1))
                        sv = jnp.where(key <= lax.broadcasted_iota(jnp.int32, shape, 2), sv, NEG)
                        s_ref[e,sl] = sv
                    mx = jnp.maximum(mx, jnp.max(sv, axis=0))
                for sh in (4, 2, 1):
                    mx = jnp.maximum(mx, pltpu.roll(mx, sh, 0))
                m_old = m_ref[e]
                m_new = jnp.maximum(m_old, mx)
                alpha = jnp.exp(m_old - m_new)
                m_ref[e] = m_new
                for r in range(t // strip):
                    p = jnp.exp(s_ref[e,r * slabs:(r + 1) * slabs] - m_new[None])
                    p_ref[e, r * strip:(r + 1) * strip, :] = p.reshape(strip, t).astype(BF)
                acc = acc_ref[e].reshape(LANES // SUBLANES, SUBLANES, t) * alpha[None]
                acc_ref[e] = acc.reshape(LANES, t) + jnp.dot(vat_ref[e, :, rows_k], p_ref[e],
                                                             preferred_element_type=F32)

        _two_slot_pipeline(i, scores, tile)

        outs, lses = [], []
        for e in range(2):
            acc = acc_ref[e]
            other = HEAD_DIM * (1 - e)
            den = acc[other:other + 1, :]
            outs.append(acc / den)
            lses.append(jnp.broadcast_to(m_ref[e][0:1, :] + jnp.log(den), (LANES, t)))
        upper = lax.broadcasted_iota(jnp.int32, (LANES, t), 0) < HEAD_DIM
        o_ref[...] = jnp.where(upper, outs[0], outs[1]).T.astype(BF)
        l_ref[...] = jnp.where(upper, lses[0], lses[1]).T
        if ex is not None:
            @pl.when((pl.program_id(0) == nhp - 1) & (i == nq - 1))
            def _():
                ex.wait(*comm)

    own_scratch = [pltpu.VMEM((2, s, LANES), BF), pltpu.VMEM((2, LANES, s), BF),
                   pltpu.VMEM((2, t // SUBLANES, SUBLANES, t), F32),
                   pltpu.VMEM((2, t // SUBLANES, SUBLANES, t), F32), pltpu.VMEM((2, t, t), BF),
                   pltpu.VMEM((2, SUBLANES, t), F32), pltpu.VMEM((2, LANES, t), F32)]
    return pl.pallas_call(
        kern, name=name,
        out_shape=[jax.ShapeDtypeStruct((s, w), BF), jax.ShapeDtypeStruct((nhp, s, LANES), F32)]
        + (ex.out_shape if ex else []),
        grid=(nhp, nq),
        in_specs=[pl.BlockSpec((t, LANES), lambda h, i: (i, qb + h)),
                  pl.BlockSpec((s, LANES), lambda h, i: (0, kb + h)),
                  pl.BlockSpec((s, LANES), lambda h, i: (0, vb + h)),
                  pl.BlockSpec((t, LANES), lambda h, i: (i, h)),
                  pl.BlockSpec((s, LANES), lambda h, i: (0, h))] + (ex.specs if ex else []),
        out_specs=[pl.BlockSpec((t, LANES), lambda h, i: (i, h)),
                   pl.BlockSpec((None, t, LANES), lambda h, i: (h, i, 0))] + (ex.specs if ex else []),
        scratch_shapes=(ex.scratch if ex else []) + own_scratch,
        compiler_params=_params(dimension_semantics=("arbitrary", "arbitrary"),
                                has_side_effects=ex is not None),
    )(proj, proj, proj, cumx, cumx, *ex_arrays)


def _attn_bwd(proj, do, o, lse, cumx, qcol, *, name, ride=None):
    s = proj.shape[0]
    w = cumx.shape[1]
    nhp = w // LANES
    t = _fit(s, ATTN_TILE_BWD, LANES)
    nq = s // t
    strip = _fit(t, ATTN_STRIP, 16)
    scale = HEAD_DIM ** -0.5
    qb, kb, vb = qcol // LANES, (qcol + w) // LANES, (qcol + 2 * w) // LANES
    tn_dims = (((0,), (0,)), ((), ()))
    nt_dims = (((1,), (1,)), ((), ()))
    ex, ex_arrays = ride if ride is not None else (None, [])

    def kern(*refs):
        own_in, own_out, comm, scratch = _ride_split(ex, refs, 7, 5)
        q_ref, k_ref, v_ref, do_ref, o_ref, l_ref, cx_ref = own_in
        dq_ref, dk_ref, dv_ref, dkc_ref, dqc_ref = own_out
        qa_ref, da_ref, dqa_ref, dka_ref, dva_ref, st0_ref, st1_ref, dpt0_ref, dpt1_ref, pt_ref, dst_ref = scratch
        st_refs, dpt_refs = (st0_ref, st1_ref), (dpt0_ref, dpt1_ref)
        j = pl.program_id(1)
        if ex is not None:
            @pl.when((pl.program_id(0) == 0) & (j == 0))
            def _():
                ex.start(*comm)

        msks = _head_masks(t)

        @pl.when(j == 0)
        def _():
            def build(c, carry):
                rows = pl.ds(pl.multiple_of(c * t, LANES), t)
                q2 = q_ref[rows, :] * scale
                do2 = do_ref[rows, :]
                dd = do2 * o_ref[rows, :].astype(F32)
                delta = jnp.where(msks[0], jnp.sum(jnp.where(msks[0], dd, 0.0), axis=1, keepdims=True),
                                  jnp.sum(jnp.where(msks[1], dd, 0.0), axis=1, keepdims=True))
                bias = cx_ref[rows, :] - l_ref[rows, :]
                for e in range(2):
                    qa_ref[e, rows, :] = _augment(q2, bias, e, bias_slot=0, ones_slot=1)
                    da_ref[e, rows, :] = _augment(do2, -delta, e, bias_slot=0, ones_slot=None)
                return carry
            lax.fori_loop(0, nq, build, 0)
            dqa_ref[...] = jnp.zeros(dqa_ref.shape, F32)

        rows_k = pl.ds(pl.multiple_of(j * t, LANES), t)
        k2, v2 = k_ref[...], v_ref[...]
        ka = [_augment(k2, -cx_ref[rows_k, :], e, bias_slot=1, ones_slot=0) for e in range(2)]
        va = [_augment(v2, None, e, bias_slot=None, ones_slot=0) for e in range(2)]
        dka_ref[...] = jnp.zeros(dka_ref.shape, F32)
        dva_ref[...] = jnp.zeros(dva_ref.shape, F32)

        def scores(k, slot):
            rows_q = pl.ds(pl.multiple_of((nq - 1 - k) * t, LANES), t)
            for e in range(2):
                st_refs[slot][e] = lax.dot_general(ka[e], qa_ref[e, rows_q, :], nt_dims,
                                                   preferred_element_type=F32)
                dpt_refs[slot][e] = lax.dot_general(va[e], da_ref[e, rows_q, :], nt_dims,
                                                    preferred_element_type=F32)

        def tile(k, slot, diagonal):
            rows_q = pl.ds(pl.multiple_of((nq - 1 - k) * t, LANES), t)
            st_ref, dpt_ref = st_refs[slot], dpt_refs[slot]
            for e in range(2):
                for r in range(t // strip):
                    rows = slice(r * strip, (r + 1) * strip)
                    sv = st_ref[e, rows, :]
                    if diagonal:
                        key = r * strip + lax.broadcasted_iota(jnp.int32, (strip, t), 0)
                        qry = lax.broadcasted_iota(jnp.int32, (strip, t), 1)
                        sv = jnp.where(key <= qry, sv, NEG)
                    p = jnp.exp(sv)
                    pt_ref[e, rows, :] = p.astype(BF)
                    dst_ref[e, rows, :] = (p * dpt_ref[e, rows, :]).astype(BF)
            for e in range(2):
                dva_ref[e] += jnp.dot(pt_ref[e], da_ref[e, rows_q, :], preferred_element_type=F32)
                dka_ref[e] += jnp.dot(dst_ref[e], qa_ref[e, rows_q, :], preferred_element_type=F32)
                dqa_ref[e, rows_q, :] += lax.dot_general(dst_ref[e], ka[e], tn_dims, preferred_element_type=F32)

        _two_slot_pipeline(nq - 1 - j, scores, tile)

        dk_ref[...] = jnp.where(msks[0], dka_ref[0], dka_ref[1])
        dv_ref[...] = jnp.where(msks[0], dva_ref[0], dva_ref[1])
        sums = jnp.where(msks[1], dka_ref[0], dka_ref[1]).T
        dkc_ref[0:1, :] = sums[HEAD_DIM + BIAS_LANES:HEAD_DIM + BIAS_LANES + 1, :]
        dkc_ref[1:2, :] = sums[BIAS_LANES:BIAS_LANES + 1, :]

        @pl.when(j == nq - 1)
        def _():
            def flush(c, carry):
                rows = pl.ds(pl.multiple_of(c * t, LANES), t)
                a0, a1 = dqa_ref[0, rows, :], dqa_ref[1, rows, :]
                dq_ref[rows, :] = jnp.where(msks[0], a0, a1) * scale
                sums = jnp.where(msks[1], a0, a1).T
                dqc_ref[0:1, rows] = sums[HEAD_DIM:HEAD_DIM + 1, :]
                dqc_ref[1:2, rows] = sums[0:1, :]
                return carry
            lax.fori_loop(0, nq, flush, 0)

        if ex is not None:
            @pl.when((pl.program_id(0) == nhp - 1) & (j == nq - 1))
            def _():
                ex.wait(*comm)

    full = lambda cb: pl.BlockSpec((s, LANES), lambda h, j: (0, cb + h))
    blk = lambda cb: pl.BlockSpec((t, LANES), lambda h, j: (j, cb + h))
    own_scratch = [pltpu.VMEM((2, s, LANES), BF), pltpu.VMEM((2, s, LANES), BF), pltpu.VMEM((2, s, LANES), F32),
                   pltpu.VMEM((2, t, LANES), F32), pltpu.VMEM((2, t, LANES), F32),
                   pltpu.VMEM((2, t, t), F32), pltpu.VMEM((2, t, t), F32),
                   pltpu.VMEM((2, t, t), F32), pltpu.VMEM((2, t, t), F32),
                   pltpu.VMEM((2, t, t), BF), pltpu.VMEM((2, t, t), BF)]
    return pl.pallas_call(
        kern, name=name,
        out_shape=[jax.ShapeDtypeStruct((s, w), F32)] * 3 + [jax.ShapeDtypeStruct((nhp, 2, s), F32)] * 2
        + (ex.out_shape if ex else []),
        grid=(nhp, nq),
        in_specs=[full(qb), blk(kb), blk(vb), full(0), full(0),
                  pl.BlockSpec((None, s, LANES), lambda h, j: (h, 0, 0)), full(0)] + (ex.specs if ex else []),
        out_specs=[full(0), blk(0), blk(0), pl.BlockSpec((None, 2, t), lambda h, j: (h, 0, j)),
                   pl.BlockSpec((None, 2, s), lambda h, j: (h, 0, 0))] + (ex.specs if ex else []),
        scratch_shapes=(ex.scratch if ex else []) + own_scratch,
        compiler_params=_params(dimension_semantics=("arbitrary", "arbitrary"),
                                has_side_effects=ex is not None),
    )(proj, proj, proj, do, o, lse, cumx, *ex_arrays)


S5_STATES = 256
S5_ROWS = 512


def _cmul(ar, ai, br, bi):
    return ar * br - ai * bi, ar * bi + ai * br


def _scan_tables(lr, li, reverse):
    w = lr.shape[1]
    row = lax.broadcasted_iota(jnp.int32, (SUBLANES, w), 0)
    if reverse:
        row = SUBLANES - 1 - row
    lr1, li1 = jnp.broadcast_to(lr, (SUBLANES, w)), jnp.broadcast_to(li, (SUBLANES, w))
    lr2, li2 = _cmul(lr1, li1, lr1, li1)
    lr4, li4 = _cmul(lr2, li2, lr2, li2)
    steps = []
    for d, (pr, pi) in zip((1, 2, 4), ((lr1, li1), (lr2, li2), (lr4, li4))):
        keep = row >= d
        steps.append((jnp.where(keep, pr, 0.0), jnp.where(keep, pi, 0.0)))
    cr, ci = lr1, li1
    for bit, (pr, pi) in zip((1, 2, 4), ((lr1, li1), (lr2, li2), (lr4, li4))):
        nr, ni = _cmul(cr, ci, pr, pi)
        has = (row & bit) != 0
        cr, ci = jnp.where(has, nr, cr), jnp.where(has, ni, ci)
    return steps, (cr, ci)


def _scan_local(xr, xi, steps, reverse):
    for d, (pr, pi) in zip((1, 2, 4), steps):
        sh = (SUBLANES - d) if reverse else d
        sr, si = pltpu.roll(xr, sh, 0), pltpu.roll(xi, sh, 0)
        xr, xi = xr + (pr * sr - pi * si), xi + (pr * si + pi * sr)
    return xr, xi


def _scan_carry(xr, xi, car_r, car_i, carry_pow):
    cr, ci = carry_pow
    return xr + (cr * car_r - ci * car_i), xi + (cr * car_i + ci * car_r)


SCAN_UNROLL = 4


def _s5_specs(s, ncb):
    u_spec = pl.BlockSpec((s, LANES), lambda cb, hf: (0, cb))
    wb_spec = pl.BlockSpec((None, None, LANES, S5_STATES), lambda cb, hf: (cb, hf, 0, 0))
    wc_spec = pl.BlockSpec((None, None, S5_STATES, LANES), lambda cb, hf: (cb, hf, 0, 0))
    lam_spec = pl.BlockSpec((1, S5_STATES), lambda cb, hf: (0, 2 * cb + hf))
    d_spec = pl.BlockSpec((1, LANES), lambda cb, hf: (0, cb))
    return u_spec, wb_spec, wc_spec, lam_spec, d_spec


def _s5_project_and_scan(u_ref, wbr_ref, wbi_ref, lr_ref, li_ref, xr_ref, xi_ref, s, rows):
    wbr, wbi = wbr_ref[...], wbi_ref[...]
    for r in range(s // rows):
        sl = pl.ds(r * rows, rows)
        ub = u_ref[sl, :].astype(BF)
        xr_ref[sl, :] = jnp.dot(ub, wbr, preferred_element_type=F32)
        xi_ref[sl, :] = jnp.dot(ub, wbi, preferred_element_type=F32)
    steps, cpow = _scan_tables(lr_ref[...], li_ref[...], False)

    unroll = _fit(s // SUBLANES, SCAN_UNROLL, 1)

    def body(b, carry):
        car_r, car_i = carry
        sls = [pl.ds(pl.multiple_of((b * unroll + q) * SUBLANES, SUBLANES), SUBLANES) for q in range(unroll)]
        blocks = [_scan_local(xr_ref[sl, :], xi_ref[sl, :], steps, False) for sl in sls]
        for sl, (xr, xi) in zip(sls, blocks):
            xr, xi = _scan_carry(xr, xi, car_r, car_i, cpow)
            xr_ref[sl, :] = xr
            xi_ref[sl, :] = xi
            car_r, car_i = xr[SUBLANES - 1:SUBLANES, :], xi[SUBLANES - 1:SUBLANES, :]
        return car_r, car_i

    zero = jnp.zeros((1, S5_STATES), F32)
    lax.fori_loop(0, s // SUBLANES // unroll, body, (zero, zero))


def _s5_fwd(proj, wb_re, wb_im, wc_re, wc_im, lam_re, lam_im, dskip, *, name, ride=None):
    s = proj.shape[0]
    w = dskip.shape[1]
    ncb = w // LANES
    rows = _fit(s, S5_ROWS, SUBLANES)
    ex, ex_arrays = ride if ride is not None else (None, [])

    def kern(*refs):
        own_in, (y_ref,), comm, (xr_ref, xi_ref) = _ride_split(ex, refs, 8, 1)
        u_ref, wbr_ref, wbi_ref, wcr_ref, wci_ref, lr_ref, li_ref, d_ref = own_in
        hf = pl.program_id(1)
        if ex is not None:
            @pl.when((pl.program_id(0) == 0) & (hf == 0))
            def _():
                ex.start(*comm)

            @pl.when((pl.program_id(0) == ncb - 1) & (hf == 0))
            def _():
                ex.forward(*comm)

        _s5_project_and_scan(u_ref, wbr_ref, wbi_ref, lr_ref, li_ref, xr_ref, xi_ref, s, rows)
        wcr, wci = wcr_ref[...], wci_ref[...]
        for r in range(s // rows):
            sl = pl.ds(r * rows, rows)
            y = (jnp.dot(xr_ref[sl, :].astype(BF), wcr, preferred_element_type=F32)
                 - jnp.dot(xi_ref[sl, :].astype(BF), wci, preferred_element_type=F32))

            @pl.when(hf == 0)
            def _(y=y, sl=sl):
                y_ref[sl, :] = y + d_ref[...] * u_ref[sl, :]

            @pl.when(hf == 1)
            def _(y=y, sl=sl):
                y_ref[sl, :] += y

        if ex is not None:
            @pl.when((pl.program_id(0) == ncb - 1) & (hf == 1))
            def _():
                ex.wait(*comm)

    u_spec, wb_spec, wc_spec, lam_spec, d_spec = _s5_specs(s, ncb)
    res = pl.pallas_call(
        kern, name=name, out_shape=[jax.ShapeDtypeStruct((s, w), F32)] + (ex.out_shape if ex else []),
        grid=(ncb, 2),
        in_specs=[u_spec, wb_spec, wb_spec, wc_spec, wc_spec, lam_spec, lam_spec, d_spec] + (ex.specs if ex else []),
        out_specs=[u_spec] + (ex.specs if ex else []),
        scratch_shapes=(ex.scratch if ex else []) + [pltpu.VMEM((s, S5_STATES), F32), pltpu.VMEM((s, S5_STATES), F32)],
        compiler_params=_params(dimension_semantics=("arbitrary", "arbitrary"), has_side_effects=ex is not None),
    )(proj, wb_re, wb_im, wc_re, wc_im, lam_re, lam_im, dskip, *ex_arrays)
    return res[0], res[1:]


def _s5_bwd(proj, dy, wb_re, wb_im, wc_re, wc_im, lam_re, lam_im, dskip, *, name, ride=None):
    s = proj.shape[0]
    w = dskip.shape[1]
    ncb = w // LANES
    rows = _fit(s, S5_ROWS, SUBLANES)
    tn_dims = (((0,), (0,)), ((), ()))
    nt_dims = (((1,), (1,)), ((), ()))
    ex, ex_arrays = ride if ride is not None else (None, [])

    def kern(*refs):
        own_in, own_out, comm, scratch = _ride_split(ex, refs, 9, 8)
        u_ref, dy_ref, wbr_ref, wbi_ref, wcr_ref, wci_ref, lr_ref, li_ref, d_ref = own_in
        du_ref, dwbr_ref, dwbi_ref, dwcr_ref, dwci_ref, dlr_ref, dli_ref, dd_ref = own_out
        xr_ref, xi_ref, gr_ref, gi_ref = scratch
        hf = pl.program_id(1)
        if ex is not None:
            @pl.when((pl.program_id(0) == 0) & (hf == 0))
            def _():
                ex.start(*comm)

        _s5_project_and_scan(u_ref, wbr_ref, wbi_ref, lr_ref, li_ref, xr_ref, xi_ref, s, rows)

        wcr, wci = wcr_ref[...], wci_ref[...]
        dwcr = jnp.zeros((S5_STATES, LANES), F32)
        dwci = jnp.zeros((S5_STATES, LANES), F32)
        ddsk = jnp.zeros((1, LANES), F32)
        for r in range(s // rows):
            sl = pl.ds(r * rows, rows)
            dyf = dy_ref[sl, :]
            dyb = dyf.astype(BF)
            gr_ref[sl, :] = lax.dot_general(dyb, wcr, nt_dims, preferred_element_type=F32)
            gi_ref[sl, :] = -lax.dot_general(dyb, wci, nt_dims, preferred_element_type=F32)
            dwcr = dwcr + lax.dot_general(xr_ref[sl, :].astype(BF), dyb, tn_dims, preferred_element_type=F32)
            dwci = dwci - lax.dot_general(xi_ref[sl, :].astype(BF), dyb, tn_dims, preferred_element_type=F32)
            ddsk = ddsk + jnp.sum(dyf * u_ref[sl, :], axis=0, keepdims=True)
        dwcr_ref[...] = dwcr
        dwci_ref[...] = dwci

        @pl.when(hf == 0)
        def _():
            dd_ref[...] = ddsk

        steps, cpow = _scan_tables(lr_ref[...], -li_ref[...], True)
        row = lax.broadcasted_iota(jnp.int32, (SUBLANES, S5_STATES), 0)
        nblk = s // SUBLANES

        unroll = _fit(nblk, SCAN_UNROLL, 1)

        def body(k, carry):
            car_r, car_i, ar, ai = carry
            sls = [pl.ds(pl.multiple_of((nblk - 1 - k * unroll - q) * SUBLANES, SUBLANES), SUBLANES)
                   for q in range(unroll)]
            blocks = [_scan_local(gr_ref[sl, :], gi_ref[sl, :], steps, True) for sl in sls]
            for sl, (g_r, g_i) in zip(sls, blocks):
                g_r, g_i = _scan_carry(g_r, g_i, car_r, car_i, cpow)
                gr_ref[sl, :] = g_r
                gi_ref[sl, :] = g_i
                nr = jnp.where(row == SUBLANES - 1, car_r, pltpu.roll(g_r, SUBLANES - 1, 0))
                ni = jnp.where(row == SUBLANES - 1, car_i, pltpu.roll(g_i, SUBLANES - 1, 0))
                xr, xi = xr_ref[sl, :], xi_ref[sl, :]
                ar = ar + (xr * nr + xi * ni)
                ai = ai + (xr * ni - xi * nr)
                car_r, car_i = g_r[0:1, :], g_i[0:1, :]
            return car_r, car_i, ar, ai

        zero = jnp.zeros((1, S5_STATES), F32)
        zacc = jnp.zeros((SUBLANES, S5_STATES), F32)
        _, _, ar, ai = lax.fori_loop(0, nblk // unroll, body, (zero, zero, zacc, zacc))
        dlr_ref[...] = jnp.sum(ar, axis=0, keepdims=True)
        dli_ref[...] = jnp.sum(ai, axis=0, keepdims=True)

        wbr, wbi = wbr_ref[...], wbi_ref[...]
        dwbr = jnp.zeros((LANES, S5_STATES), F32)
        dwbi = jnp.zeros((LANES, S5_STATES), F32)
        for r in range(s // rows):
            sl = pl.ds(r * rows, rows)
            grb, gib = gr_ref[sl, :].astype(BF), gi_ref[sl, :].astype(BF)
            ub = u_ref[sl, :].astype(BF)
            dwbr = dwbr + lax.dot_general(ub, grb, tn_dims, preferred_element_type=F32)
            dwbi = dwbi + lax.dot_general(ub, gib, tn_dims, preferred_element_type=F32)
            du = (lax.dot_general(grb, wbr, nt_dims, preferred_element_type=F32)
                  + lax.dot_general(gib, wbi, nt_dims, preferred_element_type=F32))

            @pl.when(hf == 0)
            def _(du=du, sl=sl):
                du_ref[sl, :] = du + d_ref[...] * dy_ref[sl, :]

            @pl.when(hf == 1)
            def _(du=du, sl=sl):
                du_ref[sl, :] += du
        dwbr_ref[...] = dwbr
        dwbi_ref[...] = dwbi
        if ex is not None:
            @pl.when((pl.program_id(0) == ncb - 1) & (hf == 1))
            def _():
                ex.wait(*comm)

    u_spec, wb_spec, wc_spec, lam_spec, d_spec = _s5_specs(s, ncb)
    dwb_spec = pl.BlockSpec((None, None, LANES, S5_STATES), lambda cb, hf: (cb, hf, 0, 0))
    dwc_spec = pl.BlockSpec((None, None, S5_STATES, LANES), lambda cb, hf: (cb, hf, 0, 0))
    state = pltpu.VMEM((s, S5_STATES), F32)
    return pl.pallas_call(
        kern, name=name,
        out_shape=[jax.ShapeDtypeStruct((s, w), F32),
                   jax.ShapeDtypeStruct((ncb, 2, LANES, S5_STATES), F32), jax.ShapeDtypeStruct((ncb, 2, LANES, S5_STATES), F32),
                   jax.ShapeDtypeStruct((ncb, 2, S5_STATES, LANES), F32), jax.ShapeDtypeStruct((ncb, 2, S5_STATES, LANES), F32),
                   jax.ShapeDtypeStruct((1, 4 * w), F32), jax.ShapeDtypeStruct((1, 4 * w), F32),
                   jax.ShapeDtypeStruct((1, w), F32)] + (ex.out_shape if ex else []),
        grid=(ncb, 2),
        in_specs=[u_spec, u_spec, wb_spec, wb_spec, wc_spec, wc_spec, lam_spec, lam_spec, d_spec]
        + (ex.specs if ex else []),
        out_specs=[u_spec, dwb_spec, dwb_spec, dwc_spec, dwc_spec, lam_spec, lam_spec, d_spec]
        + (ex.specs if ex else []),
        scratch_shapes=(ex.scratch if ex else []) + [state, state, state, state],
        compiler_params=_params(dimension_semantics=("arbitrary", "arbitrary"), has_side_effects=ex is not None),
    )(proj, dy, wb_re, wb_im, wc_re, wc_im, lam_re, lam_im, dskip, *ex_arrays)


def _s5_discretise(lam_re, lam_im, log_dt, b_re, b_im):
    lr = jnp.minimum(lam_re, -EIG_CLIP)
    li = lam_im
    dt = jnp.exp(log_dt)[:, None]
    mag = jnp.exp(lr * dt)
    lbr, lbi = mag * jnp.cos(li * dt), mag * jnp.sin(li * dt)
    den = lr * lr + li * li
    fr = ((lbr - 1.0) * lr + lbi * li) / den
    fi = (lbi * lr - (lbr - 1.0) * li) / den
    bbr = fr[:, None, :] * b_re - fi[:, None, :] * b_im
    bbi = fr[:, None, :] * b_im + fi[:, None, :] * b_re
    return lbr, lbi, bbr, bbi


def _s5_operand(mats, channels_first):
    g, a, b = mats.shape
    gl = LANES // 2 // SSM_H
    ncb = g // (2 * gl)
    m = mats.reshape(ncb, 2, gl, a, b)
    eye = jnp.eye(gl, dtype=mats.dtype)
    inner = (m[:, :, :, :, None, :] * eye[None, None, :, None, :, None]).reshape(ncb, 2, gl * a, gl * b)
    zeros = jnp.zeros_like(inner[:, 0])
    axis = 1 if channels_first else 2
    return jnp.stack([jnp.concatenate([inner[:, 0], zeros], axis=axis),
                      jnp.concatenate([zeros, inner[:, 1]], axis=axis)], axis=1)


def _s5_block_grads(dwb, a, b, transpose):
    ncb = dwb.shape[0]
    gl = LANES // 2 // (a if not transpose else b)
    if not transpose:
        d = dwb.reshape(ncb, 2, 2, gl, a, gl, b)
        parts = [[d[:, hf, hf, g, :, g, :] for g in range(gl)] for hf in range(2)]
    else:
        d = dwb.reshape(ncb, 2, gl, a, 2, gl, b)
        parts = [[d[:, hf, g, :, hf, g, :] for g in range(gl)] for hf in range(2)]
    st = jnp.stack([jnp.stack(p, axis=1) for p in parts], axis=1)
    return st.reshape(ncb * 2 * gl, a, b)


def _adamw(parts, w, m, v, *, name):
    depth, r, c = w.shape
    assert len(parts) == depth
    npart = parts[0].shape[0]
    row_bytes = 4 * (-(-c // LANES) * LANES)
    align = 16 if parts[0].dtype == BF else SUBLANES
    budget = VMEM_LIMIT // 2 // (2 * (depth * npart + 7) * row_bytes)
    tr = _fit(r, max(align, budget // align * align), align)
    nr = r // tr
    c1 = 1.0 / (1.0 - ADAM_B1 ** ADAM_STEP)
    c2 = 1.0 / (1.0 - ADAM_B2 ** ADAM_STEP)

    def kern(*refs):
        p_refs = refs[:depth]
        w_ref, m_ref, v_ref, g_ref, d_ref, nm_ref, nv_ref = refs[depth:]
        layer = pl.program_id(0)
        for l in range(depth):
            @pl.when(layer == l)
            def _(p_ref=p_refs[l]):
                g = p_ref[0].astype(F32)
                for q in range(1, npart):
                    g = g + p_ref[q].astype(F32)
                m2 = ADAM_B1 * m_ref[...] + (1.0 - ADAM_B1) * g
                v2 = ADAM_B2 * v_ref[...] + (1.0 - ADAM_B2) * (g * g)
                upd = (m2 * c1) / (jnp.sqrt(v2 * c2) + ADAM_EPS) + ADAM_WD * w_ref[...]
                g_ref[...] = g
                d_ref[...] = -ADAM_LR * upd
                nm_ref[...] = m2
                nv_ref[...] = v2

    def part_spec(l):
        return pl.BlockSpec((npart, tr, c),
                            lambda ly, i: (0, jnp.where(ly == l, i, jnp.where(ly < l, 0, nr - 1)), 0))

    spec = pl.BlockSpec((None, tr, c), lambda ly, i: (ly, i, 0))
    return pl.pallas_call(
        kern, name=name, out_shape=[jax.ShapeDtypeStruct((depth, r, c), F32)] * 4, grid=(depth, nr),
        in_specs=[part_spec(l) for l in range(depth)] + [spec, spec, spec],
        out_specs=[spec] * 4,
        compiler_params=_params(dimension_semantics=("arbitrary", "arbitrary")),
    )(*parts, w, m, v)


def _sum_parts(parts, *, name):
    npart, r, c = parts.shape

    def kern(p_ref, o_ref):
        g = p_ref[0]
        for q in range(1, npart):
            g = g + p_ref[q]
        o_ref[...] = g

    return pl.pallas_call(kern, name=name, out_shape=jax.ShapeDtypeStruct((r, c), F32), compiler_params=_params())(parts)


class _Exchange:
    def __init__(self, arrays, gather):
        self.n = len(arrays)
        self.gather = gather
        self.out_shape = [jax.ShapeDtypeStruct(((NDEV,) + a.shape) if gather else a.shape, a.dtype) for a in arrays]
        self.scratch = [pltpu.SemaphoreType.DMA((self.n, NDEV - 1)), pltpu.SemaphoreType.DMA((self.n, NDEV - 1)),
                        pltpu.SemaphoreType.DMA((self.n,))]
        self.specs = [pl.BlockSpec(memory_space=pl.ANY)] * self.n

    def _copies(self, srcs, dsts, sems):
        send_sems, recv_sems, local_sems = sems
        x, y, c = lax.axis_index("x"), lax.axis_index("y"), lax.axis_index("c")
        me = 4 * x + 2 * y + c
        local = [pltpu.make_async_copy(srcs[a] if self.gather else srcs[a].at[me], dsts[a].at[me], local_sems.at[a])
                 for a in range(self.n)]
        remote = []
        for k in (1, 2, 4, 3, 5, 6, 7):
            px, py, pc = x ^ ((k >> 2) & 1), y ^ ((k >> 1) & 1), c ^ (k & 1)
            peer = 4 * px + 2 * py + pc
            for a in range(self.n):
                src = srcs[a] if self.gather else srcs[a].at[peer]
                mk = functools.partial(
                    pltpu.make_async_remote_copy, src_ref=src,
                    send_sem=send_sems.at[a, k - 1], recv_sem=recv_sems.at[a, k - 1],
                    device_id=(px, py, pc), device_id_type=pl.DeviceIdType.MESH)
                remote.append((mk(dst_ref=dsts[a].at[me]), mk(dst_ref=dsts[a].at[peer])))
        return local, remote

    def _gather_copies(self, srcs, dsts, sems):
        send_sems, recv_sems, local_sems = sems
        x, y, c = lax.axis_index("x"), lax.axis_index("y"), lax.axis_index("c")
        block = lambda px, py, pc: 4 * px + 2 * py + pc
        me = block(x, y, c)
        chips = [(1 - x, y), (x, 1 - y), (1 - x, 1 - y)]
        local = [pltpu.make_async_copy(srcs[a], dsts[a].at[me], local_sems.at[a]) for a in range(self.n)]
        own, passed = [], []
        for a in range(self.n):
            def copy(k, blk, to, src=None, a=a):
                return pltpu.make_async_remote_copy(
                    src_ref=dsts[a].at[blk] if src is None else src, dst_ref=dsts[a].at[blk],
                    send_sem=send_sems.at[a, k], recv_sem=recv_sems.at[a, k],
                    device_id=to, device_id_type=pl.DeviceIdType.MESH)
            sib = (x, y, 1 - c)
            own.append((copy(0, me, sib, srcs[a]), copy(0, block(x, y, 1 - c), sib)))
            for j, (px, py) in enumerate(chips):
                own.append((copy(1 + j, me, (px, py, c), srcs[a]), copy(1 + j, block(px, py, c), (px, py, c))))
            for j, (px, py) in enumerate(chips):
                passed.append((copy(4 + j, block(px, py, c), sib), copy(4 + j, block(px, py, 1 - c), sib)))
        return local, own, passed

    def start(self, srcs, dsts, sems):
        if self.gather:
            local, own, _ = self._gather_copies(srcs, dsts, sems)
            for cp in local:
                cp.start()
            for send, _ in own:
                send.start()
            return
        local, remote = self._copies(srcs, dsts, sems)
        for cp in local:
            cp.start()
        for send, _ in remote:
            send.start()

    def forward(self, srcs, dsts, sems):
        if not self.gather:
            return
        _, own, passed = self._gather_copies(srcs, dsts, sems)
        for a in range(self.n):
            for j in range(3):
                own[4 * a + 1 + j][1].wait_recv()
                passed[3 * a + j][0].start()

    def wait(self, srcs, dsts, sems):
        if self.gather:
            local, own, passed = self._gather_copies(srcs, dsts, sems)
            for a in range(self.n):
                own[4 * a][1].wait_recv()
            for _, arrival in passed:
                arrival.wait_recv()
            for send, _ in own + passed:
                send.wait_send()
            for cp in local:
                cp.wait()
            return
        local, remote = self._copies(srcs, dsts, sems)
        for send, arrival in remote:
            send.wait_send()
            arrival.wait_recv()
        for cp in local:
            cp.wait()


def _exchange(arrays, gather, *, name):
    ex = _Exchange(arrays, gather)
    n = ex.n

    def kern(*refs):
        srcs, dsts, sems = refs[:n], refs[n:2 * n], refs[2 * n:]
        ex.start(srcs, dsts, sems)
        ex.forward(srcs, dsts, sems)
        ex.wait(srcs, dsts, sems)

    return pl.pallas_call(
        kern, name=name, out_shape=ex.out_shape, in_specs=ex.specs, out_specs=ex.specs, scratch_shapes=ex.scratch,
        compiler_params=pltpu.CompilerParams(has_side_effects=True),
    )(*arrays)


def _pack(arrays):
    flat = jnp.concatenate([a.reshape(-1).astype(F32) for a in arrays])
    pad = (-flat.shape[0]) % (SUBLANES * LANES)
    return jnp.pad(flat, (0, pad)).reshape(-1, LANES)


def _unpack(buf, like):
    flat = buf.reshape(-1)
    out, off = [], 0
    for a in like:
        sz = math.prod(a.shape)
        out.append(flat[off:off + sz].reshape(a.shape))
        off += sz
    return out


def _row(v):
    return v.reshape(1, -1)


def _layer_fwd(x, mod, p, l, ride=None, on_receive=None, target=None):
    s, d = x.shape
    sw = d // 2
    nh = d // LANES
    shift_m, scale_m, gate_m, shift_f, scale_f, gate_f = mod
    n = lambda tag: f"{tag}{l}"
    sv = {}

    h1, = _rowwise(lambda xv, g, sc, sh: (xv * _rms(xv) * g) * (1.0 + sc) + sh,
                   [x], [p['g_pre_mix'], scale_m, shift_m], [(d, BF, 'tile')], name=n("pre_mix"))
    proj_a = _mm(h1, p['w_in_a'], name=n("proj_a"))
    flog = _mm(h1, p['w_in_f'], name=n("proj_f"))
    gates = _mm(h1, p['w_in_g'], name=n("proj_g"))

    y_s5, received = _s5_fwd(proj_a, p['wb_re'], p['wb_im'], p['wc_re'], p['wc_im'], p['lamb_re'], p['lamb_im'],
                             p['d_skip'], name=n("s5_fwd"), ride=ride('s5') if ride else None)
    if on_receive is not None:
        on_receive('s5', received)
    z, = _rowwise(_gelu, [y_s5], [], [(sw, BF, 'tile')], name=n("gelu"))
    tglu, ys = _mm_fused(z, [p['w_glu']], [y_s5], lambda tv, yv, b: (tv, _gelu(yv) * _sigmoid(tv + b)),
                         [F32, BF], rows=[p['b_glu']], name=n("glu_mm"))

    cumx = _cum_fwd(flog, p['b_f_row'], nh, name=n("cum_fwd"))
    ya, lse, *received = _attn_fwd(proj_a, cumx, sw, name=n("attn_fwd"), ride=ride('attn') if ride else None)
    if on_receive is not None:
        on_receive('attn', received)

    am = _mm(ys, p['w_pa'], name=n("pa_mm"))
    bm, merged = _mm_fused(ya, [p['w_pb']], [am, (gates, 0), (gates, 1)],
                           lambda b, a, ga, gb: (b, _sigmoid(ga) * a + _sigmoid(gb) * b), [F32, BF],
                           name=n("pb_mm"))
    ym = _mm(merged, p['w_o'], name=n("o_mm"))
    def post_mix_pre_ffn(xv, yv, g, gt, g2, sc, sh):
        x2v = xv + gt * (yv * _rms(yv) * g)
        return x2v, (x2v * _rms(x2v) * g2) * (1.0 + sc) + sh

    x2, h2 = _rowwise(post_mix_pre_ffn, [x, ym], [p['g_post_mix'], gate_m, p['g_pre_ffn'], scale_f, shift_f],
                      [(d, F32, 'tile'), (d, BF, 'tile')], name=n("post_mix_pre_ffn"))
    gt, up, act = _mm_fused(h2, [p['w_ffn_gate'], p['w_ffn_up']], [], lambda g, u: (g, u, _silu(g) * u),
                            [F32, F32, BF], tb=True, name=n("gate_up_mm"))
    yf = _mm(act, p['w_ffn_down'], name=n("down_mm"))
    sv.update(x=x, h1=h1, proj_a=proj_a, flog=flog, gates=gates, y_s5=y_s5, z=z, tglu=tglu, ys=ys, cumx=cumx,
              ya=ya, lse=lse, am=am, bm=bm, merged=merged, ym=ym, x2=x2, h2=h2, gt=gt, up=up,
              act=act, yf=yf)
    if target is None:
        x3, = _rowwise(lambda xv, yv, g, gt_: xv + gt_ * (yv * _rms(yv) * g),
                       [x2, yf], [p['g_post_ffn'], gate_f], [(d, F32, 'tile')], name=n("post_ffn"))
        return x3, sv

    def output_and_loss(xv, yv, tv, g, gt_):
        r = _rms(yv)
        nf = yv * r
        e = xv + gt_ * (nf * g) - tv
        dy = e * (1.0 / d)
        return (dy, _norm_bwd(dy * gt_ * g, nf, r), dy * (nf * g), dy * gt_ * nf,
                jnp.sum(e * e, axis=1, keepdims=True) * (0.5 / d))

    dy, dyf, dgate_f, dg, loss = _rowwise(
        output_and_loss, [x2, yf, target], [p['g_post_ffn'], gate_f],
        [(d, F32, 'tile'), (d, BF, 'tile'), (d, F32, 'sum'), (d, F32, 'sum'), (1, F32, 'sum')], name="output_loss")
    sv['post_ffn_bwd'] = (dyf, dgate_f, dg)
    return (dy, loss), sv


def _layer_bwd(dx3, sv, mod, p, l, make_ride=None, on_receive=None, carried=None, defer_tail=False):
    x, x2 = sv['x'], sv['x2']
    s, d = x.shape
    sw = d // 2
    nh = d // LANES
    shift_m, scale_m, gate_m, shift_f, scale_f, gate_f = mod
    n = lambda tag: f"{tag}{l}"
    gw, gs = {}, {}

    def post_bwd(dxo, yv, g, gate):
        r = _rms(yv)
        nf = yv * r
        dn = dxo * gate * g
        return _norm_bwd(dn, nf, r), dxo * (nf * g), dxo * gate * nf

    def pre_bwd(dh, dres, xv, g, sc):
        r = _rms(xv)
        xh = xv * r
        n3 = xh * g
        dn3 = dh * (1.0 + sc)
        return dres + _norm_bwd(dn3 * g, xh, r), dh, dh * n3, dn3 * xh

    if 'post_ffn_bwd' in sv:
        dyf, dgate_f, gs['g_post_ffn'] = sv['post_ffn_bwd']
    else:
        dyf, dgate_f, gs['g_post_ffn'] = _rowwise(
            post_bwd, [dx3, sv['yf']], [p['g_post_ffn'], gate_f],
            [(d, BF, 'tile'), (d, F32, 'sum'), (d, F32, 'sum')], name=n("post_ffn_bwd"))
    gw['w_ffn_down'] = _mm(sv['act'], dyf, ta=True, out_dtype=BF, tm=1408, name=n("down_bwd_w"))

    def swiglu_bwd(da, g, u):
        sg = _sigmoid(g)
        return da * u * (sg * (1.0 + g * (1.0 - sg))), da * (g * sg)

    dgt, dup = _mm_fused(dyf, [p['w_ffn_down']], [sv['gt'], sv['up']], swiglu_bwd, [BF, BF], tb=True,
                         name=n("down_bwd_x"))
    dh2 = _mm(dgt, p['w_ffn_gate'], tm=1024, second=(dup, p['w_ffn_up']), name=n("gate_up_bwd_x"))
    gw['w_ffn_gate'] = _mm(dgt, sv['h2'], ta=True, out_dtype=BF, tm=1408, name=n("gate_bwd_w"))
    gw['w_ffn_up'] = _mm(dup, sv['h2'], ta=True, out_dtype=BF, tm=1408, name=n("up_bwd_w"))
    def pre_ffn_post_mix_bwd(dh, dres, xv, yv, g, sc, g2, gate):
        dx2v, dsh, dsc, dg = pre_bwd(dh, dres, xv, g, sc)
        return (dx2v, dsh, dsc, dg) + post_bwd(dx2v, yv, g2, gate)

    dx2, dshift_f, dscale_f, gs['g_pre_ffn'], dym, dgate_m, gs['g_post_mix'] = _rowwise(
        pre_ffn_post_mix_bwd, [dh2, dx3, x2, sv['ym']], [p['g_pre_ffn'], scale_f, p['g_post_mix'], gate_m],
        [(d, F32, 'tile'), (d, F32, 'sum'), (d, F32, 'sum'), (d, F32, 'sum'),
         (d, BF, 'tile'), (d, F32, 'sum'), (d, F32, 'sum')], name=n("pre_ffn_post_mix_bwd"))
    gw['w_o'] = _mm(sv['merged'], dym, ta=True, out_dtype=BF, name=n("o_bwd_w"))

    def merge_bwd(dm, a, b, ga, gb):
        sa, sb = _sigmoid(ga), _sigmoid(gb)
        return dm * sa, dm * sb, dm * a * sa * (1.0 - sa), dm * b * sb * (1.0 - sb)

    da_, db_, dga, dgb = _mm_fused(dym, [p['w_o']], [sv['am'], sv['bm'], (sv['gates'], 0), (sv['gates'], 1)],
                                   merge_bwd, [BF] * 4, tb=True, name=n("o_bwd_x"))
    dys = _mm(da_, p['w_pa'], tb=True, name=n("pa_bwd_x"))
    gw['w_pa'] = _mm(sv['ys'], da_, ta=True, out_dtype=BF, name=n("pa_bwd_w"))
    dya = _mm(db_, p['w_pb'], tb=True, name=n("pb_bwd_x"))
    gw['w_pb'] = _mm(sv['ya'], db_, ta=True, out_dtype=BF, name=n("pb_bwd_w"))

    sent = list(gw)
    dq, dk, dv, dkc, dqc, *received = _attn_bwd(
        sv['proj_a'], dya, sv['ya'], sv['lse'], sv['cumx'], sw, name=n("attn_bwd"),
        ride=make_ride({k: gw[k] for k in sent}) if make_ride is not None else None)
    if on_receive is not None:
        on_receive(sent, received)
    dcum = jnp.stack([-dkc.reshape(nh, s), dqc.reshape(nh, s)])
    dflog, dbf = _cum_bwd(dcum, sv['flog'], p['b_f_col'], name=n("cum_bwd"))
    gs['b_f'] = dbf.reshape(nh)

    def glu_bwd(dy_, yv, tv, b):
        zv = _gelu(yv)
        sg = _sigmoid(tv + b)
        dt = dy_ * zv * sg * (1.0 - sg)
        return dt, dy_ * sg, dt

    dt, dz1, gs['b_glu'] = _rowwise(glu_bwd, [dys, sv['y_s5'], sv['tglu']], [p['b_glu']],
                                    [(sw, BF, 'tile'), (sw, F32, 'tile'), (sw, F32, 'sum')], name=n("glu_bwd"))
    dy_s5, = _mm_fused(dt, [p['w_glu']], [dz1, sv['y_s5']], lambda dz2, a, yv: ((a + dz2) * _gelu_grad(yv),),
                       [F32], tb=True, name=n("glu_bwd_x"))
    gw['w_glu'] = _mm(sv['z'], dt, ta=True, out_dtype=BF, name=n("glu_bwd_w"))
    du, dwbr, dwbi, dwcr, dwci, dlr, dli, gs['d_skip'], *received = _s5_bwd(
        sv['proj_a'], dy_s5, p['wb_re'], p['wb_im'], p['wc_re'], p['wc_im'], p['lamb_re'], p['lamb_im'], p['d_skip'],
        name=n("s5_bwd"), ride=make_ride(carried[1]) if carried else None)
    if carried:
        carried[0](list(carried[1]), received)
    g_ = sw // SSM_H
    pst = p['lamb_re'].shape[1] // g_
    gs['lamb_re'], gs['lamb_im'] = dlr.reshape(g_, pst), dli.reshape(g_, pst)
    gs['bbar_re'] = _s5_block_grads(dwbr, SSM_H, pst, False)
    gs['bbar_im'] = _s5_block_grads(dwbi, SSM_H, pst, False)
    gs['c_re'] = _s5_block_grads(dwcr, pst, SSM_H, True).transpose(0, 2, 1)
    gs['c_im'] = _s5_block_grads(dwci, pst, SSM_H, True).transpose(0, 2, 1)

    dproj = jnp.concatenate([du.astype(BF), dq.astype(BF), dk.astype(BF), dv.astype(BF), dflog, dga, dgb], axis=1)
    gw['w_in'] = _mm(sv['h1'], dproj, ta=True, out_dtype=BF, tn=1408, name=n("proj_bwd_w"))
    if make_ride is not None:
        gw = {k: g for k, g in gw.items() if k not in sent}
    if make_ride is not None and not defer_tail:
        dh1, received = _mm(dproj, p['w_in_all'], tb=True, tk=1408, name=n("proj_bwd_x"), ride=make_ride(gw))
        on_receive(list(gw), received)
        gw = {}
    else:
        dh1 = _mm(dproj, p['w_in_all'], tb=True, tk=1408, name=n("proj_bwd_x"))
    dx, dshift_m, dscale_m, gs['g_pre_mix'] = _rowwise(
        pre_bwd, [dh1, dx2, x], [p['g_pre_mix'], scale_m],
        [(d, F32, 'tile'), (d, F32, 'sum'), (d, F32, 'sum'), (d, F32, 'sum')], name=n("pre_mix_bwd"))
    dmod = [dshift_m, dscale_m, dgate_m, dshift_f, dscale_f, dgate_f]
    return dx, gw, dmod, gs


def _unshard(k, blocks):
    if k in COL_SHARDED:
        return blocks.transpose(1, 0, 2).reshape(blocks.shape[1], NDEV * blocks.shape[2])
    return blocks.reshape(NDEV * blocks.shape[1], blocks.shape[2])


def _to_slabs(k, g):
    if k == 'w_in':
        d = g.shape[0]
        nh = d // LANES
        g = jnp.concatenate([g[:, :2 * d + nh], g[:, 2 * d + LANES:]], axis=1)
    if k in COL_SHARDED:
        return g.reshape(g.shape[0], NDEV, g.shape[1] // NDEV).transpose(1, 0, 2)
    return g.reshape(NDEV, g.shape[0] // NDEV, g.shape[1])


def _prep_w_in(w_in):
    d = w_in.shape[0]
    nh = d // LANES
    fcol = 2 * d
    p = {}
    p['w_in_a'] = w_in[:, :fcol]
    p['w_in_f'] = jnp.pad(w_in[:, fcol:fcol + nh], ((0, 0), (0, LANES - nh)))
    p['w_in_g'] = w_in[:, fcol + nh:]
    p['w_in_all'] = jnp.concatenate([p['w_in_a'], p['w_in_f'], p['w_in_g']], axis=1)
    return p


def _prep_small(small):
    nh = small['b_f'].shape[0]
    p = {}
    for k in ('g_pre_mix', 'g_post_mix', 'g_pre_ffn', 'g_post_ffn', 'd_skip', 'b_glu'):
        p[k] = _row(small[k])
    p['b_f_row'] = jnp.pad(_row(small['b_f']), ((0, 0), (0, LANES - nh)))
    p['b_f_col'] = small['b_f'].reshape(nh, 1)
    lbr, lbi, bbr, bbi = _s5_discretise(small['lam_re'], small['lam_im'], small['log_dt'], small['b_re'], small['b_im'])
    p['lamb_re'], p['lamb_im'] = _row(lbr), _row(lbi)
    p['wb_re'] = _s5_operand(bbr, True).astype(BF)
    p['wb_im'] = _s5_operand(bbi, True).astype(BF)
    p['wc_re'] = _s5_operand(small['c_re'].transpose(0, 2, 1), False).astype(BF)
    p['wc_im'] = _s5_operand(small['c_im'].transpose(0, 2, 1), False).astype(BF)
    return p


def _local_step(x, target, mods, ps, small, hooks=None):
    depth = len(ps)
    s, d = x.shape
    hooks = hooks or {}
    saved = []
    h = x
    for l in range(depth):
        h, sv = _layer_fwd(h, mods[l], ps[l], l, ride=functools.partial(hooks['fwd_ride'], l) if hooks else None,
                           on_receive=functools.partial(hooks['fwd_recv'], l) if hooks else None,
                           target=target if l == depth - 1 else None)
        saved.append(sv)

    dy, loss = h
    dmods, gss = [None] * depth, [None] * depth
    unsent = {}
    carried = None
    for l in range(depth - 1, -1, -1):
        def on_receive(names, results, l=l):
            hooks['bwd_recv']([(k, l) for k in names], results)

        dy, gw, dmods[l], gs = _layer_bwd(dy, saved[l], mods[l], ps[l], l,
                                          make_ride=hooks['bwd_ride'] if hooks else None,
                                          on_receive=on_receive if hooks else None,
                                          carried=carried, defer_tail=bool(hooks) and l > 0)
        if hooks and l > 0:
            carried = (on_receive, gw)
        else:
            unsent.update({(k, l): g for k, g in gw.items()})
        sm = small[l]
        _, vjp = jax.vjp(_s5_discretise, sm['lam_re'], sm['lam_im'], sm['log_dt'], sm['b_re'], sm['b_im'])
        gs['lam_re'], gs['lam_im'], gs['log_dt'], gs['b_re'], gs['b_im'] = vjp(
            (gs.pop('lamb_re'), gs.pop('lamb_im'), gs.pop('bbar_re'), gs.pop('bbar_im')))
        gss[l] = gs
    return loss, dy, unsent, dmods, gss


SMALL_LOCAL = ['g_pre_mix', 'g_post_mix', 'g_pre_ffn', 'g_post_ffn', 'lam_re', 'lam_im', 'log_dt', 'b_re', 'b_im',
               'c_re', 'c_im', 'd_skip', 'b_glu', 'b_f']


def kernel(x, c, w_ada, b_ada, g_pre_mix, g_post_mix, g_pre_ffn, g_post_ffn, w_in, lam_re, lam_im, log_dt, b_re, b_im, c_re, c_im, d_skip, w_glu, b_glu, b_f, w_pa, w_pb, w_o, w_ffn_gate, w_ffn_up, w_ffn_down, loss_target, m_w_ada, m_b_ada, m_g_pre_mix, m_g_post_mix, m_g_pre_ffn, m_g_post_ffn, m_w_in, m_lam_re, m_lam_im, m_log_dt, m_b_re, m_b_im, m_c_re, m_c_im, m_d_skip, m_w_glu, m_b_glu, m_b_f, m_w_pa, m_w_pb, m_w_o, m_w_ffn_gate, m_w_ffn_up, m_w_ffn_down, v_w_ada, v_b_ada, v_g_pre_mix, v_g_post_mix, v_g_pre_ffn, v_g_post_ffn, v_w_in, v_lam_re, v_lam_im, v_log_dt, v_b_re, v_b_im, v_c_re, v_c_im, v_d_skip, v_w_glu, v_b_glu, v_b_f, v_w_pa, v_w_pb, v_w_o, v_w_ffn_gate, v_w_ffn_up, v_w_ffn_down):
    args = dict(locals())
    view = lambda k, a: jnp.swapaxes(a, -1, -2) if k in TRANSPOSED else a
    W = {k: view(k, args[k]) for k in WEIGHTS}
    M = {k: view(k, args['m_' + k]) for k in WEIGHTS}
    V = {k: view(k, args['v_' + k]) for k in WEIGHTS}
    depth, d, ncol = w_ada.shape
    s = x.shape[1]
    me = 4 * lax.axis_index("x") + 2 * lax.axis_index("y") + lax.axis_index("c")

    first = ['w_in', 'w_glu']
    c_all, *first_blocks = _exchange([jnp.pad(c, ((0, SUBLANES - 1), (0, 0)))] + [W[k][0].astype(BF) for k in first],
                                     True, name="gather_first")
    c_all = c_all[:, 0, :]

    cond, = _rowwise(_silu, [c_all], [], [(d, F32, 'tile')], name="cond")
    mod_part = jnp.stack([_mm(cond, w_ada[l], name=f"ada_mm{l}") for l in range(depth)], axis=1)
    mod_recv, = _exchange([mod_part.reshape(NDEV, depth, 1, ncol)], False, name="scatter_mod")
    mod_cat = mod_recv.reshape(NDEV, depth, ncol).transpose(1, 0, 2).reshape(depth, NDEV * ncol)
    mod, = _rowwise(lambda a, b: a + b, [mod_cat, b_ada], [], [(NDEV * ncol, F32, 'tile')], name="mod_bias")
    mods = [[mod[l:l + 1, i * d:(i + 1) * d] for i in range(6)] for l in range(depth)]

    small = [{k: W[k][l] for k in SMALL_LOCAL} for l in range(depth)]
    ps = [_prep_small(small[l]) for l in range(depth)]
    rest = [k for k in BIG if k not in first]
    riding = [{'attn': [(k, l) for k in rest], 's5': [(k, l + 1) for k in first if l + 1 < depth]}
              for l in range(depth)]

    def take_weights(keys, results):
        for (k, l), blocks in zip(keys, results):
            full = _unshard(k, blocks)
            ps[l].update(_prep_w_in(full) if k == 'w_in' else {k: full})

    take_weights([(k, 0) for k in first], first_blocks)

    def fwd_ride(l, where):
        blocks = [W[k][ll].astype(BF) for k, ll in riding[l][where]]
        return (_Exchange(blocks, True), blocks) if blocks else None

    grad_parts = {}

    def bwd_ride(grads):
        slabs = [_to_slabs(k, g) for k, g in grads.items()]
        return _Exchange(slabs, False), slabs

    hooks = dict(fwd_ride=fwd_ride, fwd_recv=lambda l, where, results: take_weights(riding[l][where], results),
                 bwd_ride=bwd_ride, bwd_recv=lambda keys, results: grad_parts.update(zip(keys, results)))

    loss, dx, unsent, dmods, gss = _local_step(x[0], loss_target[0], mods, ps, small, hooks)
    assert not unsent
    out = {}
    for k in BIG:
        out[k] = _adamw([grad_parts[(k, l)] for l in range(depth)], W[k], M[k], V[k], name=f"adamw_{k}")

    dmod_mine = jnp.stack([jnp.concatenate(dmods[l], axis=1)[0] for l in range(depth)])
    small_mine = [dmod_mine] + [jnp.stack([gss[l][k] for l in range(depth)]) for k in SMALL_LOCAL] + [loss]
    parts, = _exchange([_pack(small_mine)], True, name="gather_small")
    summed = _sum_parts(parts, name="sum_small")
    names = ['b_ada'] + SMALL_LOCAL
    *small_grads, loss = _unpack(summed, [W[k] for k in names] + [loss])
    loss = loss[0, 0]
    for k, g in zip(names, small_grads):
        shp = W[k].shape
        rows = lambda a: a.reshape(depth, -1, shp[-1])
        res = _adamw([rows(g)[l][None] for l in range(depth)], rows(W[k]), rows(M[k]), rows(V[k]), name=f"adamw_{k}")
        out[k] = [a.reshape(shp) for a in res]

    dmod_all = parts.reshape(NDEV, -1)[:, :depth * 6 * d].reshape(NDEV, depth, 6 * d)
    dmod_cols = lax.dynamic_slice_in_dim(dmod_all, me * ncol, ncol, axis=2)
    g_ada = [_mm(cond, dmod_cols[:, l], ta=True, precision=HI, name=f"ada_bwd{l}")[None] for l in range(depth)]
    out['w_ada'] = _adamw(g_ada, w_ada, m_w_ada, v_w_ada, name="adamw_w_ada")

    return (loss, dx[None], *[view(k, out[k][i]) for i in range(4) for k in WEIGHTS])
```

```python
import functools
import math

import jax
import jax.numpy as jnp
from jax import lax
from jax.experimental import pallas as pl
from jax.experimental.pallas import tpu as pltpu

F32 = jnp.float32
BF = jnp.bfloat16
NDEV = 8
LANES = 128
SUBLANES = 8
VMEM_LIMIT = 48 * 1024 * 1024

SSM_H = 16
HEAD_DIM = 64
RMS_EPS = 1e-6
EIG_CLIP = 1e-4
ADAM_LR = 0.001
ADAM_B1 = 0.9
ADAM_B2 = 0.999
ADAM_EPS = 1e-08
ADAM_WD = 0.01
ADAM_STEP = 10
NEG = -1e30
HI = lax.Precision.HIGHEST

WEIGHTS = ['w_ada', 'b_ada', 'g_pre_mix', 'g_post_mix', 'g_pre_ffn', 'g_post_ffn', 'w_in', 'lam_re', 'lam_im',
           'log_dt', 'b_re', 'b_im', 'c_re', 'c_im', 'd_skip', 'w_glu', 'b_glu', 'b_f', 'w_pa', 'w_pb', 'w_o',
           'w_ffn_gate', 'w_ffn_up', 'w_ffn_down']
TRANSPOSED = ['w_ffn_gate', 'w_ffn_up', 'b_re', 'b_im']
COL_SHARDED = ['w_in', 'w_pa', 'w_pb']
ROW_SHARDED = ['w_glu', 'w_o', 'w_ffn_down', 'w_ffn_gate', 'w_ffn_up']
BIG = COL_SHARDED + ROW_SHARDED
SMALL = ['b_ada', 'g_pre_mix', 'g_post_mix', 'g_pre_ffn', 'g_post_ffn', 'lam_re', 'lam_im', 'log_dt', 'b_re',
         'b_im', 'c_re', 'c_im', 'd_skip', 'b_glu', 'b_f']


def _fit(dim, target, align):
    if dim <= target:
        return dim
    t = (target // align) * align
    while t >= align:
        if dim % t == 0:
            return t
        t -= align
    return dim


def _params(**kw):
    return pltpu.CompilerParams(vmem_limit_bytes=VMEM_LIMIT, **kw)


def _mm(a, b, *, ta=False, tb=False, out_dtype=F32, tm=None, tn=512, tk=2048, precision=None, name, ride=None,
        second=None):
    m, k = (a.shape[1], a.shape[0]) if ta else a.shape
    n = b.shape[0] if tb else b.shape[1]
    assert (b.shape[1] if tb else b.shape[0]) == k
    tm = _fit(m, tm or (1024 if ta else 2048), LANES if ta else 16)
    tn = _fit(n, tn, LANES)
    tk = _fit(k, tk, LANES)
    nk = k // tk
    grid = (m // tm, n // tn, nk)
    dims = (((0 if ta else 1,), (1 if tb else 0,)), ((), ()))
    ex, ex_arrays = ride if ride is not None else (None, [])

    pairs = [(a, b)] + ([second] if second is not None else [])

    def kern(*refs):
        ab_refs, (o_ref,), comm, scratch = _ride_split(ex, refs, 2 * len(pairs), 1)
        step = (pl.program_id(0) * grid[1] + pl.program_id(1)) * grid[2] + pl.program_id(2)
        if ex is not None:
            @pl.when(step == 0)
            def _():
                ex.start(*comm)

            @pl.when(step == (grid[0] * grid[1] * grid[2]) // 2)
            def _():
                ex.forward(*comm)

        p = None
        for a_ref, b_ref in zip(ab_refs[::2], ab_refs[1::2]):
            av, bv = a_ref[...], b_ref[...]
            if precision is None:
                av, bv = av.astype(BF), bv.astype(BF)
            q = lax.dot_general(av, bv, dims, preferred_element_type=F32, precision=precision)
            p = q if p is None else p + q
        if nk == 1:
            o_ref[...] = p.astype(out_dtype)
        else:
            acc_ref, = scratch
            kk = pl.program_id(2)

            @pl.when(kk == 0)
            def _():
                acc_ref[...] = p

            @pl.when(kk > 0)
            def _():
                acc_ref[...] += p

            @pl.when(kk == nk - 1)
            def _():
                o_ref[...] = acc_ref[...].astype(out_dtype)

        if ex is not None:
            @pl.when(step == grid[0] * grid[1] * grid[2] - 1)
            def _():
                ex.wait(*comm)

    a_spec = pl.BlockSpec((tk, tm), lambda i, j, kk: (kk, i)) if ta else pl.BlockSpec((tm, tk), lambda i, j, kk: (i, kk))
    b_spec = pl.BlockSpec((tn, tk), lambda i, j, kk: (j, kk)) if tb else pl.BlockSpec((tk, tn), lambda i, j, kk: (kk, j))
    res = pl.pallas_call(
        kern, name=name,
        out_shape=[jax.ShapeDtypeStruct((m, n), out_dtype)] + (ex.out_shape if ex else []),
        grid=grid,
        in_specs=[a_spec, b_spec] * len(pairs) + (ex.specs if ex else []),
        out_specs=[pl.BlockSpec((tm, tn), lambda i, j, kk: (i, j))] + (ex.specs if ex else []),
        scratch_shapes=(ex.scratch if ex else []) + ([] if nk == 1 else [pltpu.VMEM((tm, tn), F32)]),
        compiler_params=_params(dimension_semantics=("arbitrary",) * 3 if ex else ("parallel", "parallel", "arbitrary"),
                                has_side_effects=ex is not None),
    )(*[x for pair in pairs for x in pair], *ex_arrays)
    return (res[0], res[1:]) if ex else res[0]


def _mm_fused(a, bs, extras, fn, out_dtypes, *, rows=(), tb=False, tm=2048, tn=256, name):
    m, k = a.shape
    n = bs[0].shape[0] if tb else bs[0].shape[1]
    tm = _fit(m, tm, 16)
    tn = _fit(n, tn, LANES)
    extras = [e if isinstance(e, tuple) else (e, 0) for e in extras]
    nb, ne, nr = len(bs), len(extras), len(rows)
    dims = (((1,), (1 if tb else 0,)), ((), ()))

    def kern(*refs):
        av = refs[0][...].astype(BF)
        prods = [lax.dot_general(av, r[...].astype(BF), dims, preferred_element_type=F32) for r in refs[1:1 + nb]]
        res = fn(*prods, *[r[...] for r in refs[1 + nb:1 + nb + ne + nr]])
        for o_ref, r, dt in zip(refs[1 + nb + ne + nr:], res, out_dtypes):
            o_ref[...] = r.astype(dt)

    tile = pl.BlockSpec((tm, tn), lambda i, j: (i, j))
    b_spec = pl.BlockSpec((tn, k), lambda i, j: (j, 0)) if tb else pl.BlockSpec((k, tn), lambda i, j: (0, j))
    return pl.pallas_call(
        kern, name=name, out_shape=[jax.ShapeDtypeStruct((m, n), dt) for dt in out_dtypes],
        grid=(m // tm, n // tn),
        in_specs=[pl.BlockSpec((tm, k), lambda i, j: (i, 0))] + [b_spec] * nb
        + [pl.BlockSpec((tm, tn), lambda i, j, c=c: (i, j + c * (n // tn))) for _, c in extras]
        + [pl.BlockSpec((1, tn), lambda i, j: (0, j))] * nr,
        out_specs=[tile] * len(out_dtypes),
        compiler_params=_params(dimension_semantics=("parallel", "parallel")),
    )(a, *bs, *[e for e, _ in extras], *rows)


def _rowwise(fn, tiles, params, outs, *, tr=1024, name):
    tiles = [t if isinstance(t, tuple) else (t, t.shape[1], 0) for t in tiles]
    s = tiles[0][0].shape[0]
    row_bytes = sum(w * t.dtype.itemsize for t, w, _ in tiles)
    row_bytes += sum(w * jnp.dtype(dt).itemsize for w, dt, kind in outs if kind == 'tile')
    tr = _fit(s, max(16, min(tr, VMEM_LIMIT // 2 // (2 * row_bytes) // 16 * 16)), 16)
    nt, npar = len(tiles), len(params)

    def kern(*refs):
        i = pl.program_id(0)
        res = fn(*[r[...] for r in refs[:nt + npar]])
        if not isinstance(res, (tuple, list)):
            res = (res,)
        for (w, dt, kind), o_ref, r in zip(outs, refs[nt + npar:], res):
            if kind == 'tile':
                o_ref[...] = r.astype(dt)
            else:
                part = jnp.sum(r.astype(F32), axis=0, keepdims=True)

                @pl.when(i == 0)
                def _(o_ref=o_ref, part=part):
                    o_ref[...] = part

                @pl.when(i > 0)
                def _(o_ref=o_ref, part=part):
                    o_ref[...] += part

    def tile_spec(w, cb):
        return pl.BlockSpec((tr, w), lambda i: (i, cb))

    in_specs = [tile_spec(w, cb) for _, w, cb in tiles]
    in_specs += [pl.BlockSpec(p.shape, lambda i, nd=p.ndim: (0,) * nd) for p in params]
    out_shape, out_specs = [], []
    for w, dt, kind in outs:
        if kind == 'tile':
            out_shape.append(jax.ShapeDtypeStruct((s, w), dt))
            out_specs.append(pl.BlockSpec((tr, w), lambda i: (i, 0)))
        else:
            out_shape.append(jax.ShapeDtypeStruct((1, w), F32))
            out_specs.append(pl.BlockSpec((1, w), lambda i: (0, 0)))
    res = pl.pallas_call(
        kern, name=name, out_shape=out_shape, grid=(s // tr,), in_specs=in_specs, out_specs=out_specs,
        compiler_params=_params(dimension_semantics=("arbitrary",)),
    )(*[t[0] for t in tiles], *params)
    return res


def _sigmoid(z):
    return 1.0 / (1.0 + jnp.exp(-z))


def _silu(z):
    return z * _sigmoid(z)


_GELU_K = math.sqrt(2.0 / math.pi)


def _gelu(y):
    return 0.5 * y * (1.0 + jnp.tanh(_GELU_K * (y + 0.044715 * y * y * y)))


def _gelu_grad(y):
    th = jnp.tanh(_GELU_K * (y + 0.044715 * y * y * y))
    return 0.5 * (1.0 + th) + 0.5 * y * (1.0 - th * th) * _GELU_K * (1.0 + 3.0 * 0.044715 * y * y)


def _rms(x):
    return lax.rsqrt(jnp.mean(x * x, axis=-1, keepdims=True) + RMS_EPS)


def _norm_bwd(dn, xhat, r):
    return r * (dn - xhat * jnp.mean(dn * xhat, axis=-1, keepdims=True))


def _cum_fwd(flog, bf_row, nh, *, name):
    s = flog.shape[0]
    w = nh * HEAD_DIM
    t = _fit(s, 512, SUBLANES)

    def kern(f_ref, b_ref, o_ref, carry_ref):
        i = pl.program_id(0)

        @pl.when(i == 0)
        def _():
            carry_ref[...] = jnp.zeros_like(carry_ref)

        z = f_ref[...] + b_ref[...]
        logf = jnp.minimum(z, 0.0) - jnp.log(1.0 + jnp.exp(-jnp.abs(z)))
        hh = lax.broadcasted_iota(jnp.int32, (LANES, w), 0)
        cc = lax.broadcasted_iota(jnp.int32, (LANES, w), 1)
        expand = (cc // HEAD_DIM == hh).astype(F32)
        lx = jnp.dot(logf, expand, preferred_element_type=F32, precision=HI)
        rr = lax.broadcasted_iota(jnp.int32, (t, t), 0)
        kk = lax.broadcasted_iota(jnp.int32, (t, t), 1)
        tri = (kk <= rr).astype(F32)
        cum = jnp.dot(tri, lx, preferred_element_type=F32, precision=HI) + carry_ref[...]
        o_ref[...] = cum
        carry_ref[...] = cum[t - 1:t, :]

    return pl.pallas_call(
        kern, name=name, out_shape=jax.ShapeDtypeStruct((s, w), F32), grid=(s // t,),
        in_specs=[pl.BlockSpec((t, LANES), lambda i: (i, 0)), pl.BlockSpec((1, LANES), lambda i: (0, 0))],
        out_specs=pl.BlockSpec((t, w), lambda i: (i, 0)),
        scratch_shapes=[pltpu.VMEM((1, w), F32)],
        compiler_params=_params(dimension_semantics=("arbitrary",)),
    )(flog, bf_row)


def _cum_bwd(dcrow, flog, bf_col, *, name):
    _, nh, s = dcrow.shape
    t = _fit(s, 512, LANES)
    nb = s // t

    def kern(d_ref, f_ref, b_ref, df_ref, db_ref):
        rr = lax.broadcasted_iota(jnp.int32, (t, t), 0)
        kk = lax.broadcasted_iota(jnp.int32, (t, t), 1)
        upper = (rr >= kk).astype(F32)
        pick = (lax.broadcasted_iota(jnp.int32, (nh, LANES), 0)
                == lax.broadcasted_iota(jnp.int32, (nh, LANES), 1)).astype(F32)
        carry = jnp.zeros((nh, 1), F32)
        db = jnp.zeros((nh, 1), F32)
        for blk in range(nb - 1, -1, -1):
            sl = slice(blk * t, (blk + 1) * t)
            rc = jnp.dot(d_ref[0, :, sl] + d_ref[1, :, sl], upper, preferred_element_type=F32, precision=HI) + carry
            carry = rc[:, 0:1]
            frow = lax.dot_general(pick, f_ref[sl, :], (((1,), (1,)), ((), ())), preferred_element_type=F32,
                                   precision=HI)
            df = rc * _sigmoid(-(frow + b_ref[...]))
            df_ref[sl, :] = lax.dot_general(df, pick, (((0,), (0,)), ((), ())), preferred_element_type=F32,
                                            precision=HI).astype(BF)
            db = db + jnp.sum(df, axis=1, keepdims=True)
        db_ref[...] = db

    return pl.pallas_call(
        kern, name=name,
        out_shape=[jax.ShapeDtypeStruct((s, LANES), BF), jax.ShapeDtypeStruct((nh, 1), F32)],
        compiler_params=_params(),
    )(dcrow, flog, bf_col)


def _ride_split(ex, refs, n_in, n_out):
    n = ex.n if ex is not None else 0
    own_in, srcs = refs[:n_in], refs[n_in:n_in + n]
    own_out, dsts = refs[n_in + n:n_in + n + n_out], refs[n_in + n + n_out:n_in + 2 * n + n_out]
    sems = refs[n_in + 2 * n + n_out:n_in + 2 * n + n_out + 3] if n else ()
    rest = refs[n_in + 2 * n + n_out + (3 if n else 0):]
    return own_in, own_out, (srcs, dsts, sems), rest


ATTN_TILE = 512
ATTN_TILE_BWD = 256
ATTN_STRIP = 32
BIAS_LANES = 3


def _head_masks(rows):
    lane = lax.broadcasted_iota(jnp.int32, (rows, LANES), 1)
    return [(lane >= HEAD_DIM * e) & (lane < HEAD_DIM * (e + 1)) for e in range(2)]


def _augment(feat, bias, e, *, bias_slot, ones_slot):
    rows = feat.shape[0]
    lane = lax.broadcasted_iota(jnp.int32, (rows, LANES), 1)
    own = (lane >= HEAD_DIM * e) & (lane < HEAD_DIM * (e + 1))
    off = lane - HEAD_DIM * (1 - e)
    out = jnp.where(own, feat, 0.0)
    if ones_slot is not None:
        out = jnp.where((off >= ones_slot * BIAS_LANES) & (off < (ones_slot + 1) * BIAS_LANES), 1.0, out)
    if bias is not None:
        rest = pltpu.roll(bias, HEAD_DIM, 1)
        for term in range(BIAS_LANES):
            part = rest.astype(BF).astype(F32)
            out = jnp.where(off == bias_slot * BIAS_LANES + term, part, out)
            rest = rest - part
    return out.astype(BF)


def _two_slot_pipeline(m, scores, tile):
    scores(0, 0)

    def pair(n, carry):
        k = 2 * n
        scores(k + 1, 1)
        tile(k, 0, False)
        scores(k + 2, 0)
        tile(k + 1, 1, False)
        return carry

    lax.fori_loop(0, m // 2, pair, 0)

    @pl.when(m % 2 == 0)
    def _():
        tile(m, 0, True)

    @pl.when(m % 2 == 1)
    def _():
        scores(m, 1)
        tile(m - 1, 0, False)
        tile(m, 1, True)


def _attn_fwd(proj, cumx, qcol, *, name, ride=None):
    s = proj.shape[0]
    w = cumx.shape[1]
    nhp = w // LANES
    t = _fit(s, ATTN_TILE, LANES)
    nq = s // t
    strip = _fit(t, ATTN_STRIP, 16)
    scale = HEAD_DIM ** -0.5
    qb, kb, vb = qcol // LANES, (qcol + w) // LANES, (qcol + 2 * w) // LANES
    ex, ex_arrays = ride if ride is not None else (None, [])
    nt_dims = (((1,), (1,)), ((), ()))

    def kern(*refs):
        own_in, (o_ref, l_ref), comm, scratch = _ride_split(ex, refs, 5, 2)
        q_ref, k_ref, v_ref, cxq_ref, cxk_ref = own_in
        ka_ref, vat_ref, s0_ref, s1_ref, p_ref, m_ref, acc_ref = scratch
        s_refs = (s0_ref, s1_ref)
        i = pl.program_id(1)
        if ex is not None:
            @pl.when((pl.program_id(0) == 0) & (i == 0))
            def _():
                ex.start(*comm)

            @pl.when((pl.program_id(0) == nhp - 1) & (i == 0))
            def _():
                ex.forward(*comm)

        msks = _head_masks(t)

        @pl.when(i == 0)
        def _():
            def build(c, carry):
                rows = pl.ds(pl.multiple_of(c * t, LANES), t)
                k2, v2, cx = k_ref[rows, :], v_ref[rows, :], cxk_ref[rows, :]
                for e in range(2):
                    ka_ref[e, rows, :] = _augment(k2, -cx, e, bias_slot=1, ones_slot=0)
                    vat_ref[e, :, rows] = jnp.where(msks[e], v2, 1.0).T.astype(BF)
                return carry
            lax.fori_loop(0, nq, build, 0)

        q2 = q_ref[...] * scale
        qa = [_augment(q2, cxq_ref[...], e, bias_slot=0, ones_slot=1) for e in range(2)]
        m_ref[...] = jnp.full(m_ref.shape, NEG, F32)
        acc_ref[...] = jnp.zeros(acc_ref.shape, F32)
        slabs = strip // SUBLANES

        def scores(j, slot):
            rows_k = pl.ds(pl.multiple_of(j * t, LANES), t)
            for e in range(2):
                st = lax.dot_general(ka_ref[e, rows_k, :], qa[e], nt_dims, preferred_element_type=F32)
                s_refs[slot][e] = st.reshape(t // SUBLANES, SUBLANES, t)

        def tile(j, slot, diagonal):
            rows_k = pl.ds(pl.multiple_of(j * t, LANES), t)
            s_ref = s_refs[slot]
            for e in range(2):
                mx = jnp.full((SUBLANES, t), NEG, F32)
                for r in range(t // strip):
                    sl = slice(r * slabs, (r + 1) * slabs)
                    sv = s_ref[e,sl]
                    if diagonal:
                        shape = (slabs, SUBLANES, t)
                        key = (r * strip + lax.broadcasted_iota(jnp.int32, shape, 0) * SUBLANES
                               + lax.broadcasted_iota(jnp.int32, shape, 1))
                        sv = jnp.where(key <= lax.broadcasted_iota(jnp.int32, shape, 2), sv, NEG)
                        s_ref[e,sl] = sv
                    mx = jnp.maximum(mx, jnp.max(sv, axis=0))
                for sh in (4, 2, 1):
                    mx = jnp.maximum(mx, pltpu.roll(mx, sh, 0))
                m_old = m_ref[e]
                m_new = jnp.maximum(m_old, mx)
                alpha = jnp.exp(m_old - m_new)
                m_ref[e] = m_new
                for r in range(t // strip):
                    p = jnp.exp(s_ref[e,r * slabs:(r + 1) * slabs] - m_new[None])
                    p_ref[e, r * strip:(r + 1) * strip, :] = p.reshape(strip, t).astype(BF)
                acc = acc_ref[e].reshape(LANES // SUBLANES, SUBLANES, t) * alpha[None]
                acc_ref[e] = acc.reshape(LANES, t) + jnp.dot(vat_ref[e, :, rows_k], p_ref[e],
                                                             preferred_element_type=F32)

        _two_slot_pipeline(i, scores, tile)

        outs, lses = [], []
        for e in range(2):
            acc = acc_ref[e]
            other = HEAD_DIM * (1 - e)
            den = acc[other:other + 1, :]
            outs.append(acc / den)
            lses.append(jnp.broadcast_to(m_ref[e][0:1, :] + jnp.log(den), (LANES, t)))
        upper = lax.broadcasted_iota(jnp.int32, (LANES, t), 0) < HEAD_DIM
        o_ref[...] = jnp.where(upper, outs[0], outs[1]).T.astype(BF)
        l_ref[...] = jnp.where(upper, lses[0], lses[1]).T
        if ex is not None:
            @pl.when((pl.program_id(0) == nhp - 1) & (i == nq - 1))
            def _():
                ex.wait(*comm)

    own_scratch = [pltpu.VMEM((2, s, LANES), BF), pltpu.VMEM((2, LANES, s), BF),
                   pltpu.VMEM((2, t // SUBLANES, SUBLANES, t), F32),
                   pltpu.VMEM((2, t // SUBLANES, SUBLANES, t), F32), pltpu.VMEM((2, t, t), BF),
                   pltpu.VMEM((2, SUBLANES, t), F32), pltpu.VMEM((2, LANES, t), F32)]
    return pl.pallas_call(
        kern, name=name,
        out_shape=[jax.ShapeDtypeStruct((s, w), BF), jax.ShapeDtypeStruct((nhp, s, LANES), F32)]
        + (ex.out_shape if ex else []),
        grid=(nhp, nq),
        in_specs=[pl.BlockSpec((t, LANES), lambda h, i: (i, qb + h)),
                  pl.BlockSpec((s, LANES), lambda h, i: (0, kb + h)),
                  pl.BlockSpec((s, LANES), lambda h, i: (0, vb + h)),
                  pl.BlockSpec((t, LANES), lambda h, i: (i, h)),
                  pl.BlockSpec((s, LANES), lambda h, i: (0, h))] + (ex.specs if ex else []),
        out_specs=[pl.BlockSpec((t, LANES), lambda h, i: (i, h)),
                   pl.BlockSpec((None, t, LANES), lambda h, i: (h, i, 0))] + (ex.specs if ex else []),
        scratch_shapes=(ex.scratch if ex else []) + own_scratch,
        compiler_params=_params(dimension_semantics=("arbitrary", "arbitrary"),
                                has_side_effects=ex is not None),
    )(proj, proj, proj, cumx, cumx, *ex_arrays)


def _attn_bwd(proj, do, o, lse, cumx, qcol, *, name, ride=None):
    s = proj.shape[0]
    w = cumx.shape[1]
    nhp = w // LANES
    t = _fit(s, ATTN_TILE_BWD, LANES)
    nq = s // t
    strip = _fit(t, ATTN_STRIP, 16)
    scale = HEAD_DIM ** -0.5
    qb, kb, vb = qcol // LANES, (qcol + w) // LANES, (qcol + 2 * w) // LANES
    tn_dims = (((0,), (0,)), ((), ()))
    nt_dims = (((1,), (1,)), ((), ()))
    ex, ex_arrays = ride if ride is not None else (None, [])

    def kern(*refs):
        own_in, own_out, comm, scratch = _ride_split(ex, refs, 7, 5)
        q_ref, k_ref, v_ref, do_ref, o_ref, l_ref, cx_ref = own_in
        dq_ref, dk_ref, dv_ref, dkc_ref, dqc_ref = own_out
        qa_ref, da_ref, dqa_ref, dka_ref, dva_ref, st0_ref, st1_ref, dpt0_ref, dpt1_ref, pt_ref, dst_ref = scratch
        st_refs, dpt_refs = (st0_ref, st1_ref), (dpt0_ref, dpt1_ref)
        j = pl.program_id(1)
        if ex is not None:
            @pl.when((pl.program_id(0) == 0) & (j == 0))
            def _():
                ex.start(*comm)

        msks = _head_masks(t)

        @pl.when(j == 0)
        def _():
            def build(c, carry):
                rows = pl.ds(pl.multiple_of(c * t, LANES), t)
                q2 = q_ref[rows, :] * scale
                do2 = do_ref[rows, :]
                dd = do2 * o_ref[rows, :].astype(F32)
                delta = jnp.where(msks[0], jnp.sum(jnp.where(msks[0], dd, 0.0), axis=1, keepdims=True),
                                  jnp.sum(jnp.where(msks[1], dd, 0.0), axis=1, keepdims=True))
                bias = cx_ref[rows, :] - l_ref[rows, :]
                for e in range(2):
                    qa_ref[e, rows, :] = _augment(q2, bias, e, bias_slot=0, ones_slot=1)
                    da_ref[e, rows, :] = _augment(do2, -delta, e, bias_slot=0, ones_slot=None)
                return carry
            lax.fori_loop(0, nq, build, 0)
            dqa_ref[...] = jnp.zeros(dqa_ref.shape, F32)

        rows_k = pl.ds(pl.multiple_of(j * t, LANES), t)
        k2, v2 = k_ref[...], v_ref[...]
        ka = [_augment(k2, -cx_ref[rows_k, :], e, bias_slot=1, ones_slot=0) for e in range(2)]
        va = [_augment(v2, None, e, bias_slot=None, ones_slot=0) for e in range(2)]
        dka_ref[...] = jnp.zeros(dka_ref.shape, F32)
        dva_ref[...] = jnp.zeros(dva_ref.shape, F32)

        def scores(k, slot):
            rows_q = pl.ds(pl.multiple_of((nq - 1 - k) * t, LANES), t)
            for e in range(2):
                st_refs[slot][e] = lax.dot_general(ka[e], qa_ref[e, rows_q, :], nt_dims,
                                                   preferred_element_type=F32)
                dpt_refs[slot][e] = lax.dot_general(va[e], da_ref[e, rows_q, :], nt_dims,
                                                    preferred_element_type=F32)

        def tile(k, slot, diagonal):
            rows_q = pl.ds(pl.multiple_of((nq - 1 - k) * t, LANES), t)
            st_ref, dpt_ref = st_refs[slot], dpt_refs[slot]
            for e in range(2):
                for r in range(t // strip):
                    rows = slice(r * strip, (r + 1) * strip)
                    sv = st_ref[e, rows, :]
                    if diagonal:
                        key = r * strip + lax.broadcasted_iota(jnp.int32, (strip, t), 0)
                        qry = lax.broadcasted_iota(jnp.int32, (strip, t), 1)
                        sv = jnp.where(key <= qry, sv, NEG)
                    p = jnp.exp(sv)
                    pt_ref[e, rows, :] = p.astype(BF)
                    dst_ref[e, rows, :] = (p * dpt_ref[e, rows, :]).astype(BF)
            for e in range(2):
                dva_ref[e] += jnp.dot(pt_ref[e], da_ref[e, rows_q, :], preferred_element_type=F32)
                dka_ref[e] += jnp.dot(dst_ref[e], qa_ref[e, rows_q, :], preferred_element_type=F32)
                dqa_ref[e, rows_q, :] += lax.dot_general(dst_ref[e], ka[e], tn_dims, preferred_element_type=F32)

        _two_slot_pipeline(nq - 1 - j, scores, tile)

        dk_ref[...] = jnp.where(msks[0], dka_ref[0], dka_ref[1])
        dv_ref[...] = jnp.where(msks[0], dva_ref[0], dva_ref[1])
        sums = jnp.where(msks[1], dka_ref[0], dka_ref[1]).T
        dkc_ref[0:1, :] = sums[HEAD_DIM + BIAS_LANES:HEAD_DIM + BIAS_LANES + 1, :]
        dkc_ref[1:2, :] = sums[BIAS_LANES:BIAS_LANES + 1, :]

        @pl.when(j == nq - 1)
        def _():
            def flush(c, carry):
                rows = pl.ds(pl.multiple_of(c * t, LANES), t)
                a0, a1 = dqa_ref[0, rows, :], dqa_ref[1, rows, :]
                dq_ref[rows, :] = jnp.where(msks[0], a0, a1) * scale
                sums = jnp.where(msks[1], a0, a1).T
                dqc_ref[0:1, rows] = sums[HEAD_DIM:HEAD_DIM + 1, :]
                dqc_ref[1:2, rows] = sums[0:1, :]
                return carry
            lax.fori_loop(0, nq, flush, 0)

        if ex is not None:
            @pl.when((pl.program_id(0) == nhp - 1) & (j == nq - 1))
            def _():
                ex.wait(*comm)

    full = lambda cb: pl.BlockSpec((s, LANES), lambda h, j: (0, cb + h))
    blk = lambda cb: pl.BlockSpec((t, LANES), lambda h, j: (j, cb + h))
    own_scratch = [pltpu.VMEM((2, s, LANES), BF), pltpu.VMEM((2, s, LANES), BF), pltpu.VMEM((2, s, LANES), F32),
                   pltpu.VMEM((2, t, LANES), F32), pltpu.VMEM((2, t, LANES), F32),
                   pltpu.VMEM((2, t, t), F32), pltpu.VMEM((2, t, t), F32),
                   pltpu.VMEM((2, t, t), F32), pltpu.VMEM((2, t, t), F32),
                   pltpu.VMEM((2, t, t), BF), pltpu.VMEM((2, t, t), BF)]
    return pl.pallas_call(
        kern, name=name,
        out_shape=[jax.ShapeDtypeStruct((s, w), F32)] * 3 + [jax.ShapeDtypeStruct((nhp, 2, s), F32)] * 2
        + (ex.out_shape if ex else []),
        grid=(nhp, nq),
        in_specs=[full(qb), blk(kb), blk(vb), full(0), full(0),
                  pl.BlockSpec((None, s, LANES), lambda h, j: (h, 0, 0)), full(0)] + (ex.specs if ex else []),
        out_specs=[full(0), blk(0), blk(0), pl.BlockSpec((None, 2, t), lambda h, j: (h, 0, j)),
                   pl.BlockSpec((None, 2, s), lambda h, j: (h, 0, 0))] + (ex.specs if ex else []),
        scratch_shapes=(ex.scratch if ex else []) + own_scratch,
        compiler_params=_params(dimension_semantics=("arbitrary", "arbitrary"),
                                has_side_effects=ex is not None),
    )(proj, proj, proj, do, o, lse, cumx, *ex_arrays)


S5_STATES = 256
S5_ROWS = 512


def _cmul(ar, ai, br, bi):
    return ar * br - ai * bi, ar * bi + ai * br


def _scan_tables(lr, li, reverse):
    w = lr.shape[1]
    row = lax.broadcasted_iota(jnp.int32, (SUBLANES, w), 0)
    if reverse:
        row = SUBLANES - 1 - row
    lr1, li1 = jnp.broadcast_to(lr, (SUBLANES, w)), jnp.broadcast_to(li, (SUBLANES, w))
    lr2, li2 = _cmul(lr1, li1, lr1, li1)
    lr4, li4 = _cmul(lr2, li2, lr2, li2)
    steps = []
    for d, (pr, pi) in zip((1, 2, 4), ((lr1, li1), (lr2, li2), (lr4, li4))):
        keep = row >= d
        steps.append((jnp.where(keep, pr, 0.0), jnp.where(keep, pi, 0.0)))
    cr, ci = lr1, li1
    for bit, (pr, pi) in zip((1, 2, 4), ((lr1, li1), (lr2, li2), (lr4, li4))):
        nr, ni = _cmul(cr, ci, pr, pi)
        has = (row & bit) != 0
        cr, ci = jnp.where(has, nr, cr), jnp.where(has, ni, ci)
    return steps, (cr, ci)


def _scan_local(xr, xi, steps, reverse):
    for d, (pr, pi) in zip((1, 2, 4), steps):
        sh = (SUBLANES - d) if reverse else d
        sr, si = pltpu.roll(xr, sh, 0), pltpu.roll(xi, sh, 0)
        xr, xi = xr + (pr * sr - pi * si), xi + (pr * si + pi * sr)
    return xr, xi


def _scan_carry(xr, xi, car_r, car_i, carry_pow):
    cr, ci = carry_pow
    return xr + (cr * car_r - ci * car_i), xi + (cr * car_i + ci * car_r)


SCAN_UNROLL = 4


def _s5_specs(s, ncb):
    u_spec = pl.BlockSpec((s, LANES), lambda cb, hf: (0, cb))
    wb_spec = pl.BlockSpec((None, None, LANES, S5_STATES), lambda cb, hf: (cb, hf, 0, 0))
    wc_spec = pl.BlockSpec((None, None, S5_STATES, LANES), lambda cb, hf: (cb, hf, 0, 0))
    lam_spec = pl.BlockSpec((1, S5_STATES), lambda cb, hf: (0, 2 * cb + hf))
    d_spec = pl.BlockSpec((1, LANES), lambda cb, hf: (0, cb))
    return u_spec, wb_spec, wc_spec, lam_spec, d_spec


def _s5_project_and_scan(u_ref, wbr_ref, wbi_ref, lr_ref, li_ref, xr_ref, xi_ref, s, rows):
    wbr, wbi = wbr_ref[...], wbi_ref[...]
    for r in range(s // rows):
        sl = pl.ds(r * rows, rows)
        ub = u_ref[sl, :].astype(BF)
        xr_ref[sl, :] = jnp.dot(ub, wbr, preferred_element_type=F32)
        xi_ref[sl, :] = jnp.dot(ub, wbi, preferred_element_type=F32)
    steps, cpow = _scan_tables(lr_ref[...], li_ref[...], False)

    unroll = _fit(s // SUBLANES, SCAN_UNROLL, 1)

    def body(b, carry):
        car_r, car_i = carry
        sls = [pl.ds(pl.multiple_of((b * unroll + q) * SUBLANES, SUBLANES), SUBLANES) for q in range(unroll)]
        blocks = [_scan_local(xr_ref[sl, :], xi_ref[sl, :], steps, False) for sl in sls]
        for sl, (xr, xi) in zip(sls, blocks):
            xr, xi = _scan_carry(xr, xi, car_r, car_i, cpow)
            xr_ref[sl, :] = xr
            xi_ref[sl, :] = xi
            car_r, car_i = xr[SUBLANES - 1:SUBLANES, :], xi[SUBLANES - 1:SUBLANES, :]
        return car_r, car_i

    zero = jnp.zeros((1, S5_STATES), F32)
    lax.fori_loop(0, s // SUBLANES // unroll, body, (zero, zero))


def _s5_fwd(proj, wb_re, wb_im, wc_re, wc_im, lam_re, lam_im, dskip, *, name, ride=None):
    s = proj.shape[0]
    w = dskip.shape[1]
    ncb = w // LANES
    rows = _fit(s, S5_ROWS, SUBLANES)
    ex, ex_arrays = ride if ride is not None else (None, [])

    def kern(*refs):
        own_in, (y_ref,), comm, (xr_ref, xi_ref) = _ride_split(ex, refs, 8, 1)
        u_ref, wbr_ref, wbi_ref, wcr_ref, wci_ref, lr_ref, li_ref, d_ref = own_in
        hf = pl.program_id(1)
        if ex is not None:
            @pl.when((pl.program_id(0) == 0) & (hf == 0))
            def _():
                ex.start(*comm)

            @pl.when((pl.program_id(0) == ncb - 1) & (hf == 0))
            def _():
                ex.forward(*comm)

        _s5_project_and_scan(u_ref, wbr_ref, wbi_ref, lr_ref, li_ref, xr_ref, xi_ref, s, rows)
        wcr, wci = wcr_ref[...], wci_ref[...]
        for r in range(s // rows):
            sl = pl.ds(r * rows, rows)
            y = (jnp.dot(xr_ref[sl, :].astype(BF), wcr, preferred_element_type=F32)
                 - jnp.dot(xi_ref[sl, :].astype(BF), wci, preferred_element_type=F32))

            @pl.when(hf == 0)
            def _(y=y, sl=sl):
                y_ref[sl, :] = y + d_ref[...] * u_ref[sl, :]

            @pl.when(hf == 1)
            def _(y=y, sl=sl):
                y_ref[sl, :] += y

        if ex is not None:
            @pl.when((pl.program_id(0) == ncb - 1) & (hf == 1))
            def _():
                ex.wait(*comm)

    u_spec, wb_spec, wc_spec, lam_spec, d_spec = _s5_specs(s, ncb)
    res = pl.pallas_call(
        kern, name=name, out_shape=[jax.ShapeDtypeStruct((s, w), F32)] + (ex.out_shape if ex else []),
        grid=(ncb, 2),
        in_specs=[u_spec, wb_spec, wb_spec, wc_spec, wc_spec, lam_spec, lam_spec, d_spec] + (ex.specs if ex else []),
        out_specs=[u_spec] + (ex.specs if ex else []),
        scratch_shapes=(ex.scratch if ex else []) + [pltpu.VMEM((s, S5_STATES), F32), pltpu.VMEM((s, S5_STATES), F32)],
        compiler_params=_params(dimension_semantics=("arbitrary", "arbitrary"), has_side_effects=ex is not None),
    )(proj, wb_re, wb_im, wc_re, wc_im, lam_re, lam_im, dskip, *ex_arrays)
    return res[0], res[1:]


def _s5_bwd(proj, dy, wb_re, wb_im, wc_re, wc_im, lam_re, lam_im, dskip, *, name, ride=None):
    s = proj.shape[0]
    w = dskip.shape[1]
    ncb = w // LANES
    rows = _fit(s, S5_ROWS, SUBLANES)
    tn_dims = (((0,), (0,)), ((), ()))
    nt_dims = (((1,), (1,)), ((), ()))
    ex, ex_arrays = ride if ride is not None else (None, [])

    def kern(*refs):
        own_in, own_out, comm, scratch = _ride_split(ex, refs, 9, 8)
        u_ref, dy_ref, wbr_ref, wbi_ref, wcr_ref, wci_ref, lr_ref, li_ref, d_ref = own_in
        du_ref, dwbr_ref, dwbi_ref, dwcr_ref, dwci_ref, dlr_ref, dli_ref, dd_ref = own_out
        xr_ref, xi_ref, gr_ref, gi_ref = scratch
        hf = pl.program_id(1)
        if ex is not None:
            @pl.when((pl.program_id(0) == 0) & (hf == 0))
            def _():
                ex.start(*comm)

        _s5_project_and_scan(u_ref, wbr_ref, wbi_ref, lr_ref, li_ref, xr_ref, xi_ref, s, rows)

        wcr, wci = wcr_ref[...], wci_ref[...]
        dwcr = jnp.zeros((S5_STATES, LANES), F32)
        dwci = jnp.zeros((S5_STATES, LANES), F32)
        ddsk = jnp.zeros((1, LANES), F32)
        for r in range(s // rows):
            sl = pl.ds(r * rows, rows)
            dyf = dy_ref[sl, :]
            dyb = dyf.astype(BF)
            gr_ref[sl, :] = lax.dot_general(dyb, wcr, nt_dims, preferred_element_type=F32)
            gi_ref[sl, :] = -lax.dot_general(dyb, wci, nt_dims, preferred_element_type=F32)
            dwcr = dwcr + lax.dot_general(xr_ref[sl, :].astype(BF), dyb, tn_dims, preferred_element_type=F32)
            dwci = dwci - lax.dot_general(xi_ref[sl, :].astype(BF), dyb, tn_dims, preferred_element_type=F32)
            ddsk = ddsk + jnp.sum(dyf * u_ref[sl, :], axis=0, keepdims=True)
        dwcr_ref[...] = dwcr
        dwci_ref[...] = dwci

        @pl.when(hf == 0)
        def _():
            dd_ref[...] = ddsk

        steps, cpow = _scan_tables(lr_ref[...], -li_ref[...], True)
        row = lax.broadcasted_iota(jnp.int32, (SUBLANES, S5_STATES), 0)
        nblk = s // SUBLANES

        unroll = _fit(nblk, SCAN_UNROLL, 1)

        def body(k, carry):
            car_r, car_i, ar, ai = carry
            sls = [pl.ds(pl.multiple_of((nblk - 1 - k * unroll - q) * SUBLANES, SUBLANES), SUBLANES)
                   for q in range(unroll)]
            blocks = [_scan_local(gr_ref[sl, :], gi_ref[sl, :], steps, True) for sl in sls]
            for sl, (g_r, g_i) in zip(sls, blocks):
                g_r, g_i = _scan_carry(g_r, g_i, car_r, car_i, cpow)
                gr_ref[sl, :] = g_r
                gi_ref[sl, :] = g_i
                nr = jnp.where(row == SUBLANES - 1, car_r, pltpu.roll(g_r, SUBLANES - 1, 0))
                ni = jnp.where(row == SUBLANES - 1, car_i, pltpu.roll(g_i, SUBLANES - 1, 0))
                xr, xi = xr_ref[sl, :], xi_ref[sl, :]
                ar = ar + (xr * nr + xi * ni)
                ai = ai + (xr * ni - xi * nr)
                car_r, car_i = g_r[0:1, :], g_i[0:1, :]
            return car_r, car_i, ar, ai

        zero = jnp.zeros((1, S5_STATES), F32)
        zacc = jnp.zeros((SUBLANES, S5_STATES), F32)
        _, _, ar, ai = lax.fori_loop(0, nblk // unroll, body, (zero, zero, zacc, zacc))
        dlr_ref[...] = jnp.sum(ar, axis=0, keepdims=True)
        dli_ref[...] = jnp.sum(ai, axis=0, keepdims=True)

        wbr, wbi = wbr_ref[...], wbi_ref[...]
        dwbr = jnp.zeros((LANES, S5_STATES), F32)
        dwbi = jnp.zeros((LANES, S5_STATES), F32)
        for r in range(s // rows):
            sl = pl.ds(r * rows, rows)
            grb, gib = gr_ref[sl, :].astype(BF), gi_ref[sl, :].astype(BF)
            ub = u_ref[sl, :].astype(BF)
            dwbr = dwbr + lax.dot_general(ub, grb, tn_dims, preferred_element_type=F32)
            dwbi = dwbi + lax.dot_general(ub, gib, tn_dims, preferred_element_type=F32)
            du = (lax.dot_general(grb, wbr, nt_dims, preferred_element_type=F32)
                  + lax.dot_general(gib, wbi, nt_dims, preferred_element_type=F32))

            @pl.when(hf == 0)
            def _(du=du, sl=sl):
                du_ref[sl, :] = du + d_ref[...] * dy_ref[sl, :]

            @pl.when(hf == 1)
            def _(du=du, sl=sl):
                du_ref[sl, :] += du
        dwbr_ref[...] = dwbr
        dwbi_ref[...] = dwbi
        if ex is not None:
            @pl.when((pl.program_id(0) == ncb - 1) & (hf == 1))
            def _():
                ex.wait(*comm)

    u_spec, wb_spec, wc_spec, lam_spec, d_spec = _s5_specs(s, ncb)
    dwb_spec = pl.BlockSpec((None, None, LANES, S5_STATES), lambda cb, hf: (cb, hf, 0, 0))
    dwc_spec = pl.BlockSpec((None, None, S5_STATES, LANES), lambda cb, hf: (cb, hf, 0, 0))
    state = pltpu.VMEM((s, S5_STATES), F32)
    return pl.pallas_call(
        kern, name=name,
        out_shape=[jax.ShapeDtypeStruct((s, w), F32),
                   jax.ShapeDtypeStruct((ncb, 2, LANES, S5_STATES), F32), jax.ShapeDtypeStruct((ncb, 2, LANES, S5_STATES), F32),
                   jax.ShapeDtypeStruct((ncb, 2, S5_STATES, LANES), F32), jax.ShapeDtypeStruct((ncb, 2, S5_STATES, LANES), F32),
                   jax.ShapeDtypeStruct((1, 4 * w), F32), jax.ShapeDtypeStruct((1, 4 * w), F32),
                   jax.ShapeDtypeStruct((1, w), F32)] + (ex.out_shape if ex else []),
        grid=(ncb, 2),
        in_specs=[u_spec, u_spec, wb_spec, wb_spec, wc_spec, wc_spec, lam_spec, lam_spec, d_spec]
        + (ex.specs if ex else []),
        out_specs=[u_spec, dwb_spec, dwb_spec, dwc_spec, dwc_spec, lam_spec, lam_spec, d_spec]
        + (ex.specs if ex else []),
        scratch_shapes=(ex.scratch if ex else []) + [state, state, state, state],
        compiler_params=_params(dimension_semantics=("arbitrary", "arbitrary"), has_side_effects=ex is not None),
    )(proj, dy, wb_re, wb_im, wc_re, wc_im, lam_re, lam_im, dskip, *ex_arrays)


def _s5_discretise(lam_re, lam_im, log_dt, b_re, b_im):
    lr = jnp.minimum(lam_re, -EIG_CLIP)
    li = lam_im
    dt = jnp.exp(log_dt)[:, None]
    mag = jnp.exp(lr * dt)
    lbr, lbi = mag * jnp.cos(li * dt), mag * jnp.sin(li * dt)
    den = lr * lr + li * li
    fr = ((lbr - 1.0) * lr + lbi * li) / den
    fi = (lbi * lr - (lbr - 1.0) * li) / den
    bbr = fr[:, None, :] * b_re - fi[:, None, :] * b_im
    bbi = fr[:, None, :] * b_im + fi[:, None, :] * b_re
    return lbr, lbi, bbr, bbi


def _s5_operand(mats, channels_first):
    g, a, b = mats.shape
    gl = LANES // 2 // SSM_H
    ncb = g // (2 * gl)
    m = mats.reshape(ncb, 2, gl, a, b)
    eye = jnp.eye(gl, dtype=mats.dtype)
    inner = (m[:, :, :, :, None, :] * eye[None, None, :, None, :, None]).reshape(ncb, 2, gl * a, gl * b)
    zeros = jnp.zeros_like(inner[:, 0])
    axis = 1 if channels_first else 2
    return jnp.stack([jnp.concatenate([inner[:, 0], zeros], axis=axis),
                      jnp.concatenate([zeros, inner[:, 1]], axis=axis)], axis=1)


def _s5_block_grads(dwb, a, b, transpose):
    ncb = dwb.shape[0]
    gl = LANES // 2 // (a if not transpose else b)
    if not transpose:
        d = dwb.reshape(ncb, 2, 2, gl, a, gl, b)
        parts = [[d[:, hf, hf, g, :, g, :] for g in range(gl)] for hf in range(2)]
    else:
        d = dwb.reshape(ncb, 2, gl, a, 2, gl, b)
        parts = [[d[:, hf, g, :, hf, g, :] for g in range(gl)] for hf in range(2)]
    st = jnp.stack([jnp.stack(p, axis=1) for p in parts], axis=1)
    return st.reshape(ncb * 2 * gl, a, b)


def _adamw(parts, w, m, v, *, name):
    depth, r, c = w.shape
    assert len(parts) == depth
    npart = parts[0].shape[0]
    row_bytes = 4 * (-(-c // LANES) * LANES)
    align = 16 if parts[0].dtype == BF else SUBLANES
    budget = VMEM_LIMIT // 2 // (2 * (depth * npart + 7) * row_bytes)
    tr = _fit(r, max(align, budget // align * align), align)
    nr = r // tr
    c1 = 1.0 / (1.0 - ADAM_B1 ** ADAM_STEP)
    c2 = 1.0 / (1.0 - ADAM_B2 ** ADAM_STEP)

    def kern(*refs):
        p_refs = refs[:depth]
        w_ref, m_ref, v_ref, g_ref, d_ref, nm_ref, nv_ref = refs[depth:]
        layer = pl.program_id(0)
        for l in range(depth):
            @pl.when(layer == l)
            def _(p_ref=p_refs[l]):
                g = p_ref[0].astype(F32)
                for q in range(1, npart):
                    g = g + p_ref[q].astype(F32)
                m2 = ADAM_B1 * m_ref[...] + (1.0 - ADAM_B1) * g
                v2 = ADAM_B2 * v_ref[...] + (1.0 - ADAM_B2) * (g * g)
                upd = (m2 * c1) / (jnp.sqrt(v2 * c2) + ADAM_EPS) + ADAM_WD * w_ref[...]
                g_ref[...] = g
                d_ref[...] = -ADAM_LR * upd
                nm_ref[...] = m2
                nv_ref[...] = v2

    def part_spec(l):
        return pl.BlockSpec((npart, tr, c),
                            lambda ly, i: (0, jnp.where(ly == l, i, jnp.where(ly < l, 0, nr - 1)), 0))

    spec = pl.BlockSpec((None, tr, c), lambda ly, i: (ly, i, 0))
    return pl.pallas_call(
        kern, name=name, out_shape=[jax.ShapeDtypeStruct((depth, r, c), F32)] * 4, grid=(depth, nr),
        in_specs=[part_spec(l) for l in range(depth)] + [spec, spec, spec],
        out_specs=[spec] * 4,
        compiler_params=_params(dimension_semantics=("arbitrary", "arbitrary")),
    )(*parts, w, m, v)


def _sum_parts(parts, *, name):
    npart, r, c = parts.shape

    def kern(p_ref, o_ref):
        g = p_ref[0]
        for q in range(1, npart):
            g = g + p_ref[q]
        o_ref[...] = g

    return pl.pallas_call(kern, name=name, out_shape=jax.ShapeDtypeStruct((r, c), F32), compiler_params=_params())(parts)


class _Exchange:
    def __init__(self, arrays, gather):
        self.n = len(arrays)
        self.gather = gather
        self.out_shape = [jax.ShapeDtypeStruct(((NDEV,) + a.shape) if gather else a.shape, a.dtype) for a in arrays]
        self.scratch = [pltpu.SemaphoreType.DMA((self.n, NDEV - 1)), pltpu.SemaphoreType.DMA((self.n, NDEV - 1)),
                        pltpu.SemaphoreType.DMA((self.n,))]
        self.specs = [pl.BlockSpec(memory_space=pl.ANY)] * self.n

    def _copies(self, srcs, dsts, sems):
        send_sems, recv_sems, local_sems = sems
        x, y, c = lax.axis_index("x"), lax.axis_index("y"), lax.axis_index("c")
        me = 4 * x + 2 * y + c
        local = [pltpu.make_async_copy(srcs[a] if self.gather else srcs[a].at[me], dsts[a].at[me], local_sems.at[a])
                 for a in range(self.n)]
        remote = []
        for k in (1, 2, 4, 3, 5, 6, 7):
            px, py, pc = x ^ ((k >> 2) & 1), y ^ ((k >> 1) & 1), c ^ (k & 1)
            peer = 4 * px + 2 * py + pc
            for a in range(self.n):
                src = srcs[a] if self.gather else srcs[a].at[peer]
                mk = functools.partial(
                    pltpu.make_async_remote_copy, src_ref=src,
                    send_sem=send_sems.at[a, k - 1], recv_sem=recv_sems.at[a, k - 1],
                    device_id=(px, py, pc), device_id_type=pl.DeviceIdType.MESH)
                remote.append((mk(dst_ref=dsts[a].at[me]), mk(dst_ref=dsts[a].at[peer])))
        return local, remote

    def _gather_copies(self, srcs, dsts, sems):
        send_sems, recv_sems, local_sems = sems
        x, y, c = lax.axis_index("x"), lax.axis_index("y"), lax.axis_index("c")
        block = lambda px, py, pc: 4 * px + 2 * py + pc
        me = block(x, y, c)
        chips = [(1 - x, y), (x, 1 - y), (1 - x, 1 - y)]
        local = [pltpu.make_async_copy(srcs[a], dsts[a].at[me], local_sems.at[a]) for a in range(self.n)]
        own, passed = [], []
        for a in range(self.n):
            def copy(k, blk, to, src=None, a=a):
                return pltpu.make_async_remote_copy(
                    src_ref=dsts[a].at[blk] if src is None else src, dst_ref=dsts[a].at[blk],
                    send_sem=send_sems.at[a, k], recv_sem=recv_sems.at[a, k],
                    device_id=to, device_id_type=pl.DeviceIdType.MESH)
            sib = (x, y, 1 - c)
            own.append((copy(0, me, sib, srcs[a]), copy(0, block(x, y, 1 - c), sib)))
            for j, (px, py) in enumerate(chips):
                own.append((copy(1 + j, me, (px, py, c), srcs[a]), copy(1 + j, block(px, py, c), (px, py, c))))
            for j, (px, py) in enumerate(chips):
                passed.append((copy(4 + j, block(px, py, c), sib), copy(4 + j, block(px, py, 1 - c), sib)))
        return local, own, passed

    def start(self, srcs, dsts, sems):
        if self.gather:
            local, own, _ = self._gather_copies(srcs, dsts, sems)
            for cp in local:
                cp.start()
            for send, _ in own:
                send.start()
            return
        local, remote = self._copies(srcs, dsts, sems)
        for cp in local:
            cp.start()
        for send, _ in remote:
            send.start()

    def forward(self, srcs, dsts, sems):
        if not self.gather:
            return
        _, own, passed = self._gather_copies(srcs, dsts, sems)
        for a in range(self.n):
            for j in range(3):
                own[4 * a + 1 + j][1].wait_recv()
                passed[3 * a + j][0].start()

    def wait(self, srcs, dsts, sems):
        if self.gather:
            local, own, passed = self._gather_copies(srcs, dsts, sems)
            for a in range(self.n):
                own[4 * a][1].wait_recv()
            for _, arrival in passed:
                arrival.wait_recv()
            for send, _ in own + passed:
                send.wait_send()
            for cp in local:
                cp.wait()
            return
        local, remote = self._copies(srcs, dsts, sems)
        for send, arrival in remote:
            send.wait_send()
            arrival.wait_recv()
        for cp in local:
            cp.wait()


def _exchange(arrays, gather, *, name):
    ex = _Exchange(arrays, gather)
    n = ex.n

    def kern(*refs):
        srcs, dsts, sems = refs[:n], refs[n:2 * n], refs[2 * n:]
        ex.start(srcs, dsts, sems)
        ex.forward(srcs, dsts, sems)
        ex.wait(srcs, dsts, sems)

    return pl.pallas_call(
        kern, name=name, out_shape=ex.out_shape, in_specs=ex.specs, out_specs=ex.specs, scratch_shapes=ex.scratch,
        compiler_params=pltpu.CompilerParams(has_side_effects=True),
    )(*arrays)


def _pack(arrays):
    flat = jnp.concatenate([a.reshape(-1).astype(F32) for a in arrays])
    pad = (-flat.shape[0]) % (SUBLANES * LANES)
    return jnp.pad(flat, (0, pad)).reshape(-1, LANES)


def _unpack(buf, like):
    flat = buf.reshape(-1)
    out, off = [], 0
    for a in like:
        sz = math.prod(a.shape)
        out.append(flat[off:off + sz].reshape(a.shape))
        off += sz
    return out


def _row(v):
    return v.reshape(1, -1)


def _layer_fwd(x, mod, p, l, ride=None, on_receive=None, target=None):
    s, d = x.shape
    sw = d // 2
    nh = d // LANES
    shift_m, scale_m, gate_m, shift_f, scale_f, gate_f = mod
    n = lambda tag: f"{tag}{l}"
    sv = {}

    h1, = _rowwise(lambda xv, g, sc, sh: (xv * _rms(xv) * g) * (1.0 + sc) + sh,
                   [x], [p['g_pre_mix'], scale_m, shift_m], [(d, BF, 'tile')], name=n("pre_mix"))
    proj_a = _mm(h1, p['w_in_a'], name=n("proj_a"))
    flog = _mm(h1, p['w_in_f'], name=n("proj_f"))
    gates = _mm(h1, p['w_in_g'], name=n("proj_g"))

    y_s5, received = _s5_fwd(proj_a, p['wb_re'], p['wb_im'], p['wc_re'], p['wc_im'], p['lamb_re'], p['lamb_im'],
                             p['d_skip'], name=n("s5_fwd"), ride=ride('s5') if ride else None)
    if on_receive is not None:
        on_receive('s5', received)
    z, = _rowwise(_gelu, [y_s5], [], [(sw, BF, 'tile')], name=n("gelu"))
    tglu, ys = _mm_fused(z, [p['w_glu']], [y_s5], lambda tv, yv, b: (tv, _gelu(yv) * _sigmoid(tv + b)),
                         [F32, BF], rows=[p['b_glu']], name=n("glu_mm"))

    cumx = _cum_fwd(flog, p['b_f_row'], nh, name=n("cum_fwd"))
    ya, lse, *received = _attn_fwd(proj_a, cumx, sw, name=n("attn_fwd"), ride=ride('attn') if ride else None)
    if on_receive is not None:
        on_receive('attn', received)

    am = _mm(ys, p['w_pa'], name=n("pa_mm"))
    bm, merged = _mm_fused(ya, [p['w_pb']], [am, (gates, 0), (gates, 1)],
                           lambda b, a, ga, gb: (b, _sigmoid(ga) * a + _sigmoid(gb) * b), [F32, BF],
                           name=n("pb_mm"))
    ym = _mm(merged, p['w_o'], name=n("o_mm"))
    def post_mix_pre_ffn(xv, yv, g, gt, g2, sc, sh):
        x2v = xv + gt * (yv * _rms(yv) * g)
        return x2v, (x2v * _rms(x2v) * g2) * (1.0 + sc) + sh

    x2, h2 = _rowwise(post_mix_pre_ffn, [x, ym], [p['g_post_mix'], gate_m, p['g_pre_ffn'], scale_f, shift_f],
                      [(d, F32, 'tile'), (d, BF, 'tile')], name=n("post_mix_pre_ffn"))
    gt, up, act = _mm_fused(h2, [p['w_ffn_gate'], p['w_ffn_up']], [], lambda g, u: (g, u, _silu(g) * u),
                            [F32, F32, BF], tb=True, name=n("gate_up_mm"))
    yf = _mm(act, p['w_ffn_down'], name=n("down_mm"))
    sv.update(x=x, h1=h1, proj_a=proj_a, flog=flog, gates=gates, y_s5=y_s5, z=z, tglu=tglu, ys=ys, cumx=cumx,
              ya=ya, lse=lse, am=am, bm=bm, merged=merged, ym=ym, x2=x2, h2=h2, gt=gt, up=up,
              act=act, yf=yf)
    if target is None:
        x3, = _rowwise(lambda xv, yv, g, gt_: xv + gt_ * (yv * _rms(yv) * g),
                       [x2, yf], [p['g_post_ffn'], gate_f], [(d, F32, 'tile')], name=n("post_ffn"))
        return x3, sv

    def output_and_loss(xv, yv, tv, g, gt_):
        r = _rms(yv)
        nf = yv * r
        e = xv + gt_ * (nf * g) - tv
        dy = e * (1.0 / d)
        return (dy, _norm_bwd(dy * gt_ * g, nf, r), dy * (nf * g), dy * gt_ * nf,
                jnp.sum(e * e, axis=1, keepdims=True) * (0.5 / d))

    dy, dyf, dgate_f, dg, loss = _rowwise(
        output_and_loss, [x2, yf, target], [p['g_post_ffn'], gate_f],
        [(d, F32, 'tile'), (d, BF, 'tile'), (d, F32, 'sum'), (d, F32, 'sum'), (1, F32, 'sum')], name="output_loss")
    sv['post_ffn_bwd'] = (dyf, dgate_f, dg)
    return (dy, loss), sv


def _layer_bwd(dx3, sv, mod, p, l, make_ride=None, on_receive=None, carried=None, defer_tail=False):
    x, x2 = sv['x'], sv['x2']
    s, d = x.shape
    sw = d // 2
    nh = d // LANES
    shift_m, scale_m, gate_m, shift_f, scale_f, gate_f = mod
    n = lambda tag: f"{tag}{l}"
    gw, gs = {}, {}

    def post_bwd(dxo, yv, g, gate):
        r = _rms(yv)
        nf = yv * r
        dn = dxo * gate * g
        return _norm_bwd(dn, nf, r), dxo * (nf * g), dxo * gate * nf

    def pre_bwd(dh, dres, xv, g, sc):
        r = _rms(xv)
        xh = xv * r
        n3 = xh * g
        dn3 = dh * (1.0 + sc)
        return dres + _norm_bwd(dn3 * g, xh, r), dh, dh * n3, dn3 * xh

    if 'post_ffn_bwd' in sv:
        dyf, dgate_f, gs['g_post_ffn'] = sv['post_ffn_bwd']
    else:
        dyf, dgate_f, gs['g_post_ffn'] = _rowwise(
            post_bwd, [dx3, sv['yf']], [p['g_post_ffn'], gate_f],
            [(d, BF, 'tile'), (d, F32, 'sum'), (d, F32, 'sum')], name=n("post_ffn_bwd"))
    gw['w_ffn_down'] = _mm(sv['act'], dyf, ta=True, out_dtype=BF, tm=1408, name=n("down_bwd_w"))

    def swiglu_bwd(da, g, u):
        sg = _sigmoid(g)
        return da * u * (sg * (1.0 + g * (1.0 - sg))), da * (g * sg)

    dgt, dup = _mm_fused(dyf, [p['w_ffn_down']], [sv['gt'], sv['up']], swiglu_bwd, [BF, BF], tb=True,
                         name=n("down_bwd_x"))
    dh2 = _mm(dgt, p['w_ffn_gate'], tm=1024, second=(dup, p['w_ffn_up']), name=n("gate_up_bwd_x"))
    gw['w_ffn_gate'] = _mm(dgt, sv['h2'], ta=True, out_dtype=BF, tm=1408, name=n("gate_bwd_w"))
    gw['w_ffn_up'] = _mm(dup, sv['h2'], ta=True, out_dtype=BF, tm=1408, name=n("up_bwd_w"))
    def pre_ffn_post_mix_bwd(dh, dres, xv, yv, g, sc, g2, gate):
        dx2v, dsh, dsc, dg = pre_bwd(dh, dres, xv, g, sc)
        return (dx2v, dsh, dsc, dg) + post_bwd(dx2v, yv, g2, gate)

    dx2, dshift_f, dscale_f, gs['g_pre_ffn'], dym, dgate_m, gs['g_post_mix'] = _rowwise(
        pre_ffn_post_mix_bwd, [dh2, dx3, x2, sv['ym']], [p['g_pre_ffn'], scale_f, p['g_post_mix'], gate_m],
        [(d, F32, 'tile'), (d, F32, 'sum'), (d, F32, 'sum'), (d, F32, 'sum'),
         (d, BF, 'tile'), (d, F32, 'sum'), (d, F32, 'sum')], name=n("pre_ffn_post_mix_bwd"))
    gw['w_o'] = _mm(sv['merged'], dym, ta=True, out_dtype=BF, name=n("o_bwd_w"))

    def merge_bwd(dm, a, b, ga, gb):
        sa, sb = _sigmoid(ga), _sigmoid(gb)
        return dm * sa, dm * sb, dm * a * sa * (1.0 - sa), dm * b * sb * (1.0 - sb)

    da_, db_, dga, dgb = _mm_fused(dym, [p['w_o']], [sv['am'], sv['bm'], (sv['gates'], 0), (sv['gates'], 1)],
                                   merge_bwd, [BF] * 4, tb=True, name=n("o_bwd_x"))
    dys = _mm(da_, p['w_pa'], tb=True, name=n("pa_bwd_x"))
    gw['w_pa'] = _mm(sv['ys'], da_, ta=True, out_dtype=BF, name=n("pa_bwd_w"))
    dya = _mm(db_, p['w_pb'], tb=True, name=n("pb_bwd_x"))
    gw['w_pb'] = _mm(sv['ya'], db_, ta=True, out_dtype=BF, name=n("pb_bwd_w"))

    sent = list(gw)
    dq, dk, dv, dkc, dqc, *received = _attn_bwd(
        sv['proj_a'], dya, sv['ya'], sv['lse'], sv['cumx'], sw, name=n("attn_bwd"),
        ride=make_ride({k: gw[k] for k in sent}) if make_ride is not None else None)
    if on_receive is not None:
        on_receive(sent, received)
    dcum = jnp.stack([-dkc.reshape(nh, s), dqc.reshape(nh, s)])
    dflog, dbf = _cum_bwd(dcum, sv['flog'], p['b_f_col'], name=n("cum_bwd"))
    gs['b_f'] = dbf.reshape(nh)

    def glu_bwd(dy_, yv, tv, b):
        zv = _gelu(yv)
        sg = _sigmoid(tv + b)
        dt = dy_ * zv * sg * (1.0 - sg)
        return dt, dy_ * sg, dt

    dt, dz1, gs['b_glu'] = _rowwise(glu_bwd, [dys, sv['y_s5'], sv['tglu']], [p['b_glu']],
                                    [(sw, BF, 'tile'), (sw, F32, 'tile'), (sw, F32, 'sum')], name=n("glu_bwd"))
    dy_s5, = _mm_fused(dt, [p['w_glu']], [dz1, sv['y_s5']], lambda dz2, a, yv: ((a + dz2) * _gelu_grad(yv),),
                       [F32], tb=True, name=n("glu_bwd_x"))
    gw['w_glu'] = _mm(sv['z'], dt, ta=True, out_dtype=BF, name=n("glu_bwd_w"))
    du, dwbr, dwbi, dwcr, dwci, dlr, dli, gs['d_skip'], *received = _s5_bwd(
        sv['proj_a'], dy_s5, p['wb_re'], p['wb_im'], p['wc_re'], p['wc_im'], p['lamb_re'], p['lamb_im'], p['d_skip'],
        name=n("s5_bwd"), ride=make_ride(carried[1]) if carried else None)
    if carried:
        carried[0](list(carried[1]), received)
    g_ = sw // SSM_H
    pst = p['lamb_re'].shape[1] // g_
    gs['lamb_re'], gs['lamb_im'] = dlr.reshape(g_, pst), dli.reshape(g_, pst)
    gs['bbar_re'] = _s5_block_grads(dwbr, SSM_H, pst, False)
    gs['bbar_im'] = _s5_block_grads(dwbi, SSM_H, pst, False)
    gs['c_re'] = _s5_block_grads(dwcr, pst, SSM_H, True).transpose(0, 2, 1)
    gs['c_im'] = _s5_block_grads(dwci, pst, SSM_H, True).transpose(0, 2, 1)

    dproj = jnp.concatenate([du.astype(BF), dq.astype(BF), dk.astype(BF), dv.astype(BF), dflog, dga, dgb], axis=1)
    gw['w_in'] = _mm(sv['h1'], dproj, ta=True, out_dtype=BF, tn=1408, name=n("proj_bwd_w"))
    if make_ride is not None:
        gw = {k: g for k, g in gw.items() if k not in sent}
    if make_ride is not None and not defer_tail:
        dh1, received = _mm(dproj, p['w_in_all'], tb=True, tk=1408, name=n("proj_bwd_x"), ride=make_ride(gw))
        on_receive(list(gw), received)
        gw = {}
    else:
        dh1 = _mm(dproj, p['w_in_all'], tb=True, tk=1408, name=n("proj_bwd_x"))
    dx, dshift_m, dscale_m, gs['g_pre_mix'] = _rowwise(
        pre_bwd, [dh1, dx2, x], [p['g_pre_mix'], scale_m],
        [(d, F32, 'tile'), (d, F32, 'sum'), (d, F32, 'sum'), (d, F32, 'sum')], name=n("pre_mix_bwd"))
    dmod = [dshift_m, dscale_m, dgate_m, dshift_f, dscale_f, dgate_f]
    return dx, gw, dmod, gs


def _unshard(k, blocks):
    if k in COL_SHARDED:
        return blocks.transpose(1, 0, 2).reshape(blocks.shape[1], NDEV * blocks.shape[2])
    return blocks.reshape(NDEV * blocks.shape[1], blocks.shape[2])


def _to_slabs(k, g):
    if k == 'w_in':
        d = g.shape[0]
        nh = d // LANES
        g = jnp.concatenate([g[:, :2 * d + nh], g[:, 2 * d + LANES:]], axis=1)
    if k in COL_SHARDED:
        return g.reshape(g.shape[0], NDEV, g.shape[1] // NDEV).transpose(1, 0, 2)
    return g.reshape(NDEV, g.shape[0] // NDEV, g.shape[1])


def _prep_w_in(w_in):
    d = w_in.shape[0]
    nh = d // LANES
    fcol = 2 * d
    p = {}
    p['w_in_a'] = w_in[:, :fcol]
    p['w_in_f'] = jnp.pad(w_in[:, fcol:fcol + nh], ((0, 0), (0, LANES - nh)))
    p['w_in_g'] = w_in[:, fcol + nh:]
    p['w_in_all'] = jnp.concatenate([p['w_in_a'], p['w_in_f'], p['w_in_g']], axis=1)
    return p


def _prep_small(small):
    nh = small['b_f'].shape[0]
    p = {}
    for k in ('g_pre_mix', 'g_post_mix', 'g_pre_ffn', 'g_post_ffn', 'd_skip', 'b_glu'):
        p[k] = _row(small[k])
    p['b_f_row'] = jnp.pad(_row(small['b_f']), ((0, 0), (0, LANES - nh)))
    p['b_f_col'] = small['b_f'].reshape(nh, 1)
    lbr, lbi, bbr, bbi = _s5_discretise(small['lam_re'], small['lam_im'], small['log_dt'], small['b_re'], small['b_im'])
    p['lamb_re'], p['lamb_im'] = _row(lbr), _row(lbi)
    p['wb_re'] = _s5_operand(bbr, True).astype(BF)
    p['wb_im'] = _s5_operand(bbi, True).astype(BF)
    p['wc_re'] = _s5_operand(small['c_re'].transpose(0, 2, 1), False).astype(BF)
    p['wc_im'] = _s5_operand(small['c_im'].transpose(0, 2, 1), False).astype(BF)
    return p


def _local_step(x, target, mods, ps, small, hooks=None):
    depth = len(ps)
    s, d = x.shape
    hooks = hooks or {}
    saved = []
    h = x
    for l in range(depth):
        h, sv = _layer_fwd(h, mods[l], ps[l], l, ride=functools.partial(hooks['fwd_ride'], l) if hooks else None,
                           on_receive=functools.partial(hooks['fwd_recv'], l) if hooks else None,
                           target=target if l == depth - 1 else None)
        saved.append(sv)

    dy, loss = h
    dmods, gss = [None] * depth, [None] * depth
    unsent = {}
    carried = None
    for l in range(depth - 1, -1, -1):
        def on_receive(names, results, l=l):
            hooks['bwd_recv']([(k, l) for k in names], results)

        dy, gw, dmods[l], gs = _layer_bwd(dy, saved[l], mods[l], ps[l], l,
                                          make_ride=hooks['bwd_ride'] if hooks else None,
                                          on_receive=on_receive if hooks else None,
                                          carried=carried, defer_tail=bool(hooks) and l > 0)
        if hooks and l > 0:
            carried = (on_receive, gw)
        else:
            unsent.update({(k, l): g for k, g in gw.items()})
        sm = small[l]
        _, vjp = jax.vjp(_s5_discretise, sm['lam_re'], sm['lam_im'], sm['log_dt'], sm['b_re'], sm['b_im'])
        gs['lam_re'], gs['lam_im'], gs['log_dt'], gs['b_re'], gs['b_im'] = vjp(
            (gs.pop('lamb_re'), gs.pop('lamb_im'), gs.pop('bbar_re'), gs.pop('bbar_im')))
        gss[l] = gs
    return loss, dy, unsent, dmods, gss


SMALL_LOCAL = ['g_pre_mix', 'g_post_mix', 'g_pre_ffn', 'g_post_ffn', 'lam_re', 'lam_im', 'log_dt', 'b_re', 'b_im',
               'c_re', 'c_im', 'd_skip', 'b_glu', 'b_f']


def kernel(x, c, w_ada, b_ada, g_pre_mix, g_post_mix, g_pre_ffn, g_post_ffn, w_in, lam_re, lam_im, log_dt, b_re, b_im, c_re, c_im, d_skip, w_glu, b_glu, b_f, w_pa, w_pb, w_o, w_ffn_gate, w_ffn_up, w_ffn_down, loss_target, m_w_ada, m_b_ada, m_g_pre_mix, m_g_post_mix, m_g_pre_ffn, m_g_post_ffn, m_w_in, m_lam_re, m_lam_im, m_log_dt, m_b_re, m_b_im, m_c_re, m_c_im, m_d_skip, m_w_glu, m_b_glu, m_b_f, m_w_pa, m_w_pb, m_w_o, m_w_ffn_gate, m_w_ffn_up, m_w_ffn_down, v_w_ada, v_b_ada, v_g_pre_mix, v_g_post_mix, v_g_pre_ffn, v_g_post_ffn, v_w_in, v_lam_re, v_lam_im, v_log_dt, v_b_re, v_b_im, v_c_re, v_c_im, v_d_skip, v_w_glu, v_b_glu, v_b_f, v_w_pa, v_w_pb, v_w_o, v_w_ffn_gate, v_w_ffn_up, v_w_ffn_down):
    args = dict(locals())
    view = lambda k, a: jnp.swapaxes(a, -1, -2) if k in TRANSPOSED else a
    W = {k: view(k, args[k]) for k in WEIGHTS}
    M = {k: view(k, args['m_' + k]) for k in WEIGHTS}
    V = {k: view(k, args['v_' + k]) for k in WEIGHTS}
    depth, d, ncol = w_ada.shape
    s = x.shape[1]
    me = 4 * lax.axis_index("x") + 2 * lax.axis_index("y") + lax.axis_index("c")

    first = ['w_in', 'w_glu']
    c_all, *first_blocks = _exchange([jnp.pad(c, ((0, SUBLANES - 1), (0, 0)))] + [W[k][0].astype(BF) for k in first],
                                     True, name="gather_first")
    c_all = c_all[:, 0, :]

    cond, = _rowwise(_silu, [c_all], [], [(d, F32, 'tile')], name="cond")
    mod_part = jnp.stack([_mm(cond, w_ada[l], name=f"ada_mm{l}") for l in range(depth)], axis=1)
    mod_recv, = _exchange([mod_part.reshape(NDEV, depth, 1, ncol)], False, name="scatter_mod")
    mod_cat = mod_recv.reshape(NDEV, depth, ncol).transpose(1, 0, 2).reshape(depth, NDEV * ncol)
    mod, = _rowwise(lambda a, b: a + b, [mod_cat, b_ada], [], [(NDEV * ncol, F32, 'tile')], name="mod_bias")
    mods = [[mod[l:l + 1, i * d:(i + 1) * d] for i in range(6)] for l in range(depth)]

    small = [{k: W[k][l] for k in SMALL_LOCAL} for l in range(depth)]
    ps = [_prep_small(small[l]) for l in range(depth)]
    rest = [k for k in BIG if k not in first]
    riding = [{'attn': [(k, l) for k in rest], 's5': [(k, l + 1) for k in first if l + 1 < depth]}
              for l in range(depth)]

    def take_weights(keys, results):
        for (k, l), blocks in zip(keys, results):
            full = _unshard(k, blocks)
            ps[l].update(_prep_w_in(full) if k == 'w_in' else {k: full})

    take_weights([(k, 0) for k in first], first_blocks)

    def fwd_ride(l, where):
        blocks = [W[k][ll].astype(BF) for k, ll in riding[l][where]]
        return (_Exchange(blocks, True), blocks) if blocks else None

    grad_parts = {}

    def bwd_ride(grads):
        slabs = [_to_slabs(k, g) for k, g in grads.items()]
        return _Exchange(slabs, False), slabs

    hooks = dict(fwd_ride=fwd_ride, fwd_recv=lambda l, where, results: take_weights(riding[l][where], results),
                 bwd_ride=bwd_ride, bwd_recv=lambda keys, results: grad_parts.update(zip(keys, results)))

    loss, dx, unsent, dmods, gss = _local_step(x[0], loss_target[0], mods, ps, small, hooks)
    assert not unsent
    out = {}
    for k in BIG:
        out[k] = _adamw([grad_parts[(k, l)] for l in range(depth)], W[k], M[k], V[k], name=f"adamw_{k}")

    dmod_mine = jnp.stack([jnp.concatenate(dmods[l], axis=1)[0] for l in range(depth)])
    small_mine = [dmod_mine] + [jnp.stack([gss[l][k] for l in range(depth)]) for k in SMALL_LOCAL] + [loss]
    parts, = _exchange([_pack(small_mine)], True, name="gather_small")
    summed = _sum_parts(parts, name="sum_small")
    names = ['b_ada'] + SMALL_LOCAL
    *small_grads, loss = _unpack(summed, [W[k] for k in names] + [loss])
    loss = loss[0, 0]
    for k, g in zip(names, small_grads):
        shp = W[k].shape
        rows = lambda a: a.reshape(depth, -1, shp[-1])
        res = _adamw([rows(g)[l][None] for l in range(depth)], rows(W[k]), rows(M[k]), rows(V[k]), name=f"adamw_{k}")
        out[k] = [a.reshape(shp) for a in res]

    dmod_all = parts.reshape(NDEV, -1)[:, :depth * 6 * d].reshape(NDEV, depth, 6 * d)
    dmod_cols = lax.dynamic_slice_in_dim(dmod_all, me * ncol, ncol, axis=2)
    g_ada = [_mm(cond, dmod_cols[:, l], ta=True, precision=HI, name=f"ada_bwd{l}")[None] for l in range(depth)]
    out['w_ada'] = _adamw(g_ada, w_ada, m_w_ada, v_w_ada, name="adamw_w_ada")

    return (loss, dx[None], *[view(k, out[k][i]) for i in range(4) for k in WEIGHTS])
```

```python
import functools
import math

import jax
import jax.numpy as jnp
from jax import lax
from jax.experimental import pallas as pl
from jax.experimental.pallas import tpu as pltpu

F32 = jnp.float32
BF = jnp.bfloat16
NDEV = 8
LANES = 128
SUBLANES = 8
VMEM_LIMIT = 48 * 1024 * 1024

SSM_H = 16
HEAD_DIM = 64
RMS_EPS = 1e-6
EIG_CLIP = 1e-4
ADAM_LR = 0.001
ADAM_B1 = 0.9
ADAM_B2 = 0.999
ADAM_EPS = 1e-08
ADAM_WD = 0.01
ADAM_STEP = 10
NEG = -1e30
HI = lax.Precision.HIGHEST

WEIGHTS = ['w_ada', 'b_ada', 'g_pre_mix', 'g_post_mix', 'g_pre_ffn', 'g_post_ffn', 'w_in', 'lam_re', 'lam_im',
           'log_dt', 'b_re', 'b_im', 'c_re', 'c_im', 'd_skip', 'w_glu', 'b_glu', 'b_f', 'w_pa', 'w_pb', 'w_o',
           'w_ffn_gate', 'w_ffn_up', 'w_ffn_down']
TRANSPOSED = ['w_ffn_gate', 'w_ffn_up', 'b_re', 'b_im']
COL_SHARDED = ['w_in', 'w_pa', 'w_pb']
ROW_SHARDED = ['w_glu', 'w_o', 'w_ffn_down', 'w_ffn_gate', 'w_ffn_up']
BIG = COL_SHARDED + ROW_SHARDED
SMALL = ['b_ada', 'g_pre_mix', 'g_post_mix', 'g_pre_ffn', 'g_post_ffn', 'lam_re', 'lam_im', 'log_dt', 'b_re',
         'b_im', 'c_re', 'c_im', 'd_skip', 'b_glu', 'b_f']


def _fit(dim, target, align):
    if dim <= target:
        return dim
    t = (target // align) * align
    while t >= align:
        if dim % t == 0:
            return t
        t -= align
    return dim


def _params(**kw):
    return pltpu.CompilerParams(vmem_limit_bytes=VMEM_LIMIT, **kw)


def _mm(a, b, *, ta=False, tb=False, out_dtype=F32, tm=None, tn=512, tk=2048, precision=None, name, ride=None,
        second=None):
    m, k = (a.shape[1], a.shape[0]) if ta else a.shape
    n = b.shape[0] if tb else b.shape[1]
    assert (b.shape[1] if tb else b.shape[0]) == k
    tm = _fit(m, tm or (1024 if ta else 2048), LANES if ta else 16)
    tn = _fit(n, tn, LANES)
    tk = _fit(k, tk, LANES)
    nk = k // tk
    grid = (m // tm, n // tn, nk)
    dims = (((0 if ta else 1,), (1 if tb else 0,)), ((), ()))
    ex, ex_arrays = ride if ride is not None else (None, [])

    pairs = [(a, b)] + ([second] if second is not None else [])

    def kern(*refs):
        ab_refs, (o_ref,), comm, scratch = _ride_split(ex, refs, 2 * len(pairs), 1)
        step = (pl.program_id(0) * grid[1] + pl.program_id(1)) * grid[2] + pl.program_id(2)
        if ex is not None:
            @pl.when(step == 0)
            def _():
                ex.start(*comm)

            @pl.when(step == (grid[0] * grid[1] * grid[2]) // 2)
            def _():
                ex.forward(*comm)

        p = None
        for a_ref, b_ref in zip(ab_refs[::2], ab_refs[1::2]):
            av, bv = a_ref[...], b_ref[...]
            if precision is None:
                av, bv = av.astype(BF), bv.astype(BF)
            q = lax.dot_general(av, bv, dims, preferred_element_type=F32, precision=precision)
            p = q if p is None else p + q
        if nk == 1:
            o_ref[...] = p.astype(out_dtype)
        else:
            acc_ref, = scratch
            kk = pl.program_id(2)

            @pl.when(kk == 0)
            def _():
                acc_ref[...] = p

            @pl.when(kk > 0)
            def _():
                acc_ref[...] += p

            @pl.when(kk == nk - 1)
            def _():
                o_ref[...] = acc_ref[...].astype(out_dtype)

        if ex is not None:
            @pl.when(step == grid[0] * grid[1] * grid[2] - 1)
            def _():
                ex.wait(*comm)

    a_spec = pl.BlockSpec((tk, tm), lambda i, j, kk: (kk, i)) if ta else pl.BlockSpec((tm, tk), lambda i, j, kk: (i, kk))
    b_spec = pl.BlockSpec((tn, tk), lambda i, j, kk: (j, kk)) if tb else pl.BlockSpec((tk, tn), lambda i, j, kk: (kk, j))
    res = pl.pallas_call(
        kern, name=name,
        out_shape=[jax.ShapeDtypeStruct((m, n), out_dtype)] + (ex.out_shape if ex else []),
        grid=grid,
        in_specs=[a_spec, b_spec] * len(pairs) + (ex.specs if ex else []),
        out_specs=[pl.BlockSpec((tm, tn), lambda i, j, kk: (i, j))] + (ex.specs if ex else []),
        scratch_shapes=(ex.scratch if ex else []) + ([] if nk == 1 else [pltpu.VMEM((tm, tn), F32)]),
        compiler_params=_params(dimension_semantics=("arbitrary",) * 3 if ex else ("parallel", "parallel", "arbitrary"),
                                has_side_effects=ex is not None),
    )(*[x for pair in pairs for x in pair], *ex_arrays)
    return (res[0], res[1:]) if ex else res[0]


def _mm_fused(a, bs, extras, fn, out_dtypes, *, rows=(), tb=False, tm=2048, tn=256, name):
    m, k = a.shape
    n = bs[0].shape[0] if tb else bs[0].shape[1]
    tm = _fit(m, tm, 16)
    tn = _fit(n, tn, LANES)
    extras = [e if isinstance(e, tuple) else (e, 0) for e in extras]
    nb, ne, nr = len(bs), len(extras), len(rows)
    dims = (((1,), (1 if tb else 0,)), ((), ()))

    def kern(*refs):
        av = refs[0][...].astype(BF)
        prods = [lax.dot_general(av, r[...].astype(BF), dims, preferred_element_type=F32) for r in refs[1:1 + nb]]
        res = fn(*prods, *[r[...] for r in refs[1 + nb:1 + nb + ne + nr]])
        for o_ref, r, dt in zip(refs[1 + nb + ne + nr:], res, out_dtypes):
            o_ref[...] = r.astype(dt)

    tile = pl.BlockSpec((tm, tn), lambda i, j: (i, j))
    b_spec = pl.BlockSpec((tn, k), lambda i, j: (j, 0)) if tb else pl.BlockSpec((k, tn), lambda i, j: (0, j))
    return pl.pallas_call(
        kern, name=name, out_shape=[jax.ShapeDtypeStruct((m, n), dt) for dt in out_dtypes],
        grid=(m // tm, n // tn),
        in_specs=[pl.BlockSpec((tm, k), lambda i, j: (i, 0))] + [b_spec] * nb
        + [pl.BlockSpec((tm, tn), lambda i, j, c=c: (i, j + c * (n // tn))) for _, c in extras]
        + [pl.BlockSpec((1, tn), lambda i, j: (0, j))] * nr,
        out_specs=[tile] * len(out_dtypes),
        compiler_params=_params(dimension_semantics=("parallel", "parallel")),
    )(a, *bs, *[e for e, _ in extras], *rows)


def _rowwise(fn, tiles, params, outs, *, tr=1024, name):
    tiles = [t if isinstance(t, tuple) else (t, t.shape[1], 0) for t in tiles]
    s = tiles[0][0].shape[0]
    row_bytes = sum(w * t.dtype.itemsize for t, w, _ in tiles)
    row_bytes += sum(w * jnp.dtype(dt).itemsize for w, dt, kind in outs if kind == 'tile')
    tr = _fit(s, max(16, min(tr, VMEM_LIMIT // 2 // (2 * row_bytes) // 16 * 16)), 16)
    nt, npar = len(tiles), len(params)

    def kern(*refs):
        i = pl.program_id(0)
        res = fn(*[r[...] for r in refs[:nt + npar]])
        if not isinstance(res, (tuple, list)):
            res = (res,)
        for (w, dt, kind), o_ref, r in zip(outs, refs[nt + npar:], res):
            if kind == 'tile':
                o_ref[...] = r.astype(dt)
            else:
                part = jnp.sum(r.astype(F32), axis=0, keepdims=True)

                @pl.when(i == 0)
                def _(o_ref=o_ref, part=part):
                    o_ref[...] = part

                @pl.when(i > 0)
                def _(o_ref=o_ref, part=part):
                    o_ref[...] += part

    def tile_spec(w, cb):
        return pl.BlockSpec((tr, w), lambda i: (i, cb))

    in_specs = [tile_spec(w, cb) for _, w, cb in tiles]
    in_specs += [pl.BlockSpec(p.shape, lambda i, nd=p.ndim: (0,) * nd) for p in params]
    out_shape, out_specs = [], []
    for w, dt, kind in outs:
        if kind == 'tile':
            out_shape.append(jax.ShapeDtypeStruct((s, w), dt))
            out_specs.append(pl.BlockSpec((tr, w), lambda i: (i, 0)))
        else:
            out_shape.append(jax.ShapeDtypeStruct((1, w), F32))
            out_specs.append(pl.BlockSpec((1, w), lambda i: (0, 0)))
    res = pl.pallas_call(
        kern, name=name, out_shape=out_shape, grid=(s // tr,), in_specs=in_specs, out_specs=out_specs,
        compiler_params=_params(dimension_semantics=("arbitrary",)),
    )(*[t[0] for t in tiles], *params)
    return res


def _sigmoid(z):
    return 1.0 / (1.0 + jnp.exp(-z))


def _silu(z):
    return z * _sigmoid(z)


_GELU_K = math.sqrt(2.0 / math.pi)


def _gelu(y):
    return 0.5 * y * (1.0 + jnp.tanh(_GELU_K * (y + 0.044715 * y * y * y)))


def _gelu_grad(y):
    th = jnp.tanh(_GELU_K * (y + 0.044715 * y * y * y))
    return 0.5 * (1.0 + th) + 0.5 * y * (1.0 - th * th) * _GELU_K * (1.0 + 3.0 * 0.044715 * y * y)


def _rms(x):
    return lax.rsqrt(jnp.mean(x * x, axis=-1, keepdims=True) + RMS_EPS)


def _norm_bwd(dn, xhat, r):
    return r * (dn - xhat * jnp.mean(dn * xhat, axis=-1, keepdims=True))


def _cum_fwd(flog, bf_row, nh, *, name):
    s = flog.shape[0]
    w = nh * HEAD_DIM
    t = _fit(s, 512, SUBLANES)

    def kern(f_ref, b_ref, o_ref, carry_ref):
        i = pl.program_id(0)

        @pl.when(i == 0)
        def _():
            carry_ref[...] = jnp.zeros_like(carry_ref)

        z = f_ref[...] + b_ref[...]
        logf = jnp.minimum(z, 0.0) - jnp.log(1.0 + jnp.exp(-jnp.abs(z)))
        hh = lax.broadcasted_iota(jnp.int32, (LANES, w), 0)
        cc = lax.broadcasted_iota(jnp.int32, (LANES, w), 1)
        expand = (cc // HEAD_DIM == hh).astype(F32)
        lx = jnp.dot(logf, expand, preferred_element_type=F32, precision=HI)
        rr = lax.broadcasted_iota(jnp.int32, (t, t), 0)
        kk = lax.broadcasted_iota(jnp.int32, (t, t), 1)
        tri = (kk <= rr).astype(F32)
        cum = jnp.dot(tri, lx, preferred_element_type=F32, precision=HI) + carry_ref[...]
        o_ref[...] = cum
        carry_ref[...] = cum[t - 1:t, :]

    return pl.pallas_call(
        kern, name=name, out_shape=jax.ShapeDtypeStruct((s, w), F32), grid=(s // t,),
        in_specs=[pl.BlockSpec((t, LANES), lambda i: (i, 0)), pl.BlockSpec((1, LANES), lambda i: (0, 0))],
        out_specs=pl.BlockSpec((t, w), lambda i: (i, 0)),
        scratch_shapes=[pltpu.VMEM((1, w), F32)],
        compiler_params=_params(dimension_semantics=("arbitrary",)),
    )(flog, bf_row)


def _cum_bwd(dcrow, flog, bf_col, *, name):
    _, nh, s = dcrow.shape
    t = _fit(s, 512, LANES)
    nb = s // t

    def kern(d_ref, f_ref, b_ref, df_ref, db_ref):
        rr = lax.broadcasted_iota(jnp.int32, (t, t), 0)
        kk = lax.broadcasted_iota(jnp.int32, (t, t), 1)
        upper = (rr >= kk).astype(F32)
        pick = (lax.broadcasted_iota(jnp.int32, (nh, LANES), 0)
                == lax.broadcasted_iota(jnp.int32, (nh, LANES), 1)).astype(F32)
        carry = jnp.zeros((nh, 1), F32)
        db = jnp.zeros((nh, 1), F32)
        for blk in range(nb - 1, -1, -1):
            sl = slice(blk * t, (blk + 1) * t)
            rc = jnp.dot(d_ref[0, :, sl] + d_ref[1, :, sl], upper, preferred_element_type=F32, precision=HI) + carry
            carry = rc[:, 0:1]
            frow = lax.dot_general(pick, f_ref[sl, :], (((1,), (1,)), ((), ())), preferred_element_type=F32,
                                   precision=HI)
            df = rc * _sigmoid(-(frow + b_ref[...]))
            df_ref[sl, :] = lax.dot_general(df, pick, (((0,), (0,)), ((), ())), preferred_element_type=F32,
                                            precision=HI).astype(BF)
            db = db + jnp.sum(df, axis=1, keepdims=True)
        db_ref[...] = db

    return pl.pallas_call(
        kern, name=name,
        out_shape=[jax.ShapeDtypeStruct((s, LANES), BF), jax.ShapeDtypeStruct((nh, 1), F32)],
        compiler_params=_params(),
    )(dcrow, flog, bf_col)


def _ride_split(ex, refs, n_in, n_out):
    n = ex.n if ex is not None else 0
    own_in, srcs = refs[:n_in], refs[n_in:n_in + n]
    own_out, dsts = refs[n_in + n:n_in + n + n_out], refs[n_in + n + n_out:n_in + 2 * n + n_out]
    sems = refs[n_in + 2 * n + n_out:n_in + 2 * n + n_out + 3] if n else ()
    rest = refs[n_in + 2 * n + n_out + (3 if n else 0):]
    return own_in, own_out, (srcs, dsts, sems), rest


ATTN_TILE = 512
ATTN_TILE_BWD = 256
ATTN_STRIP = 32
BIAS_LANES = 3


def _head_masks(rows):
    lane = lax.broadcasted_iota(jnp.int32, (rows, LANES), 1)
    return [(lane >= HEAD_DIM * e) & (lane < HEAD_DIM * (e + 1)) for e in range(2)]


def _augment(feat, bias, e, *, bias_slot, ones_slot):
    rows = feat.shape[0]
    lane = lax.broadcasted_iota(jnp.int32, (rows, LANES), 1)
    own = (lane >= HEAD_DIM * e) & (lane < HEAD_DIM * (e + 1))
    off = lane - HEAD_DIM * (1 - e)
    out = jnp.where(own, feat, 0.0)
    if ones_slot is not None:
        out = jnp.where((off >= ones_slot * BIAS_LANES) & (off < (ones_slot + 1) * BIAS_LANES), 1.0, out)
    if bias is not None:
        rest = pltpu.roll(bias, HEAD_DIM, 1)
        for term in range(BIAS_LANES):
            part = rest.astype(BF).astype(F32)
            out = jnp.where(off == bias_slot * BIAS_LANES + term, part, out)
            rest = rest - part
    return out.astype(BF)


def _two_slot_pipeline(m, scores, tile):
    scores(0, 0)

    def pair(n, carry):
        k = 2 * n
        scores(k + 1, 1)
        tile(k, 0, False)
        scores(k + 2, 0)
        tile(k + 1, 1, False)
        return carry

    lax.fori_loop(0, m // 2, pair, 0)

    @pl.when(m % 2 == 0)
    def _():
        tile(m, 0, True)

    @pl.when(m % 2 == 1)
    def _():
        scores(m, 1)
        tile(m - 1, 0, False)
        tile(m, 1, True)


def _attn_fwd(proj, cumx, qcol, *, name, ride=None):
    s = proj.shape[0]
    w = cumx.shape[1]
    nhp = w // LANES
    t = _fit(s, ATTN_TILE, LANES)
    nq = s // t
    strip = _fit(t, ATTN_STRIP, 16)
    scale = HEAD_DIM ** -0.5
    qb, kb, vb = qcol // LANES, (qcol + w) // LANES, (qcol + 2 * w) // LANES
    ex, ex_arrays = ride if ride is not None else (None, [])
    nt_dims = (((1,), (1,)), ((), ()))

    def kern(*refs):
        own_in, (o_ref, l_ref), comm, scratch = _ride_split(ex, refs, 5, 2)
        q_ref, k_ref, v_ref, cxq_ref, cxk_ref = own_in
        ka_ref, vat_ref, s0_ref, s1_ref, p_ref, m_ref, acc_ref = scratch
        s_refs = (s0_ref, s1_ref)
        i = pl.program_id(1)
        if ex is not None:
            @pl.when((pl.program_id(0) == 0) & (i == 0))
            def _():
                ex.start(*comm)

            @pl.when((pl.program_id(0) == nhp - 1) & (i == 0))
            def _():
                ex.forward(*comm)

        msks = _head_masks(t)

        @pl.when(i == 0)
        def _():
            def build(c, carry):
                rows = pl.ds(pl.multiple_of(c * t, LANES), t)
                k2, v2, cx = k_ref[rows, :], v_ref[rows, :], cxk_ref[rows, :]
                for e in range(2):
                    ka_ref[e, rows, :] = _augment(k2, -cx, e, bias_slot=1, ones_slot=0)
                    vat_ref[e, :, rows] = jnp.where(msks[e], v2, 1.0).T.astype(BF)
                return carry
            lax.fori_loop(0, nq, build, 0)

        q2 = q_ref[...] * scale
        qa = [_augment(q2, cxq_ref[...], e, bias_slot=0, ones_slot=1) for e in range(2)]
        m_ref[...] = jnp.full(m_ref.shape, NEG, F32)
        acc_ref[...] = jnp.zeros(acc_ref.shape, F32)
        slabs = strip // SUBLANES

        def scores(j, slot):
            rows_k = pl.ds(pl.multiple_of(j * t, LANES), t)
            for e in range(2):
                st = lax.dot_general(ka_ref[e, rows_k, :], qa[e], nt_dims, preferred_element_type=F32)
                s_refs[slot][e] = st.reshape(t // SUBLANES, SUBLANES, t)

        def tile(j, slot, diagonal):
            rows_k = pl.ds(pl.multiple_of(j * t, LANES), t)
            s_ref = s_refs[slot]
            for e in range(2):
                mx = jnp.full((SUBLANES, t), NEG, F32)
                for r in range(t // strip):
                    sl = slice(r * slabs, (r + 1) * slabs)
                    sv = s_ref[e,sl]
                    if diagonal:
                        shape = (slabs, SUBLANES, t)
                        key = (r * strip + lax.broadcasted_iota(jnp.int32, shape, 0) * SUBLANES
                               + lax.broadcasted_iota(jnp.int32, shape, 1))
                        sv = jnp.where(key <= lax.broadcasted_iota(jnp.int32, shape, 2), sv, NEG)
                        s_ref[e,sl] = sv
                    mx = jnp.maximum(mx, jnp.max(sv, axis=0))
                for sh in (4, 2, 1):
                    mx = jnp.maximum(mx, pltpu.roll(mx, sh, 0))
                m_old = m_ref[e]
                m_new = jnp.maximum(m_old, mx)
                alpha = jnp.exp(m_old - m_new)
                m_ref[e] = m_new
                for r in range(t // strip):
                    p = jnp.exp(s_ref[e,r * slabs:(r + 1) * slabs] - m_new[None])
                    p_ref[e, r * strip:(r + 1) * strip, :] = p.reshape(strip, t).astype(BF)
                acc = acc_ref[e].reshape(LANES // SUBLANES, SUBLANES, t) * alpha[None]
                acc_ref[e] = acc.reshape(LANES, t) + jnp.dot(vat_ref[e, :, rows_k], p_ref[e],
                                                             preferred_element_type=F32)

        _two_slot_pipeline(i, scores, tile)

        outs, lses = [], []
        for e in range(2):
            acc = acc_ref[e]
            other = HEAD_DIM * (1 - e)
            den = acc[other:other + 1, :]
            outs.append(acc / den)
            lses.append(jnp.broadcast_to(m_ref[e][0:1, :] + jnp.log(den), (LANES, t)))
        upper = lax.broadcasted_iota(jnp.int32, (LANES, t), 0) < HEAD_DIM
        o_ref[...] = jnp.where(upper, outs[0], outs[1]).T.astype(BF)
        l_ref[...] = jnp.where(upper, lses[0], lses[1]).T
        if ex is not None:
            @pl.when((pl.program_id(0) == nhp - 1) & (i == nq - 1))
            def _():
                ex.wait(*comm)

    own_scratch = [pltpu.VMEM((2, s, LANES), BF), pltpu.VMEM((2, LANES, s), BF),
                   pltpu.VMEM((2, t // SUBLANES, SUBLANES, t), F32),
                   pltpu.VMEM((2, t // SUBLANES, SUBLANES, t), F32), pltpu.VMEM((2, t, t), BF),
                   pltpu.VMEM((2, SUBLANES, t), F32), pltpu.VMEM((2, LANES, t), F32)]
    return pl.pallas_call(
        kern, name=name,
        out_shape=[jax.ShapeDtypeStruct((s, w), BF), jax.ShapeDtypeStruct((nhp, s, LANES), F32)]
        + (ex.out_shape if ex else []),
        grid=(nhp, nq),
        in_specs=[pl.BlockSpec((t, LANES), lambda h, i: (i, qb + h)),
                  pl.BlockSpec((s, LANES), lambda h, i: (0, kb + h)),
                  pl.BlockSpec((s, LANES), lambda h, i: (0, vb + h)),
                  pl.BlockSpec((t, LANES), lambda h, i: (i, h)),
                  pl.BlockSpec((s, LANES), lambda h, i: (0, h))] + (ex.specs if ex else []),
        out_specs=[pl.BlockSpec((t, LANES), lambda h, i: (i, h)),
                   pl.BlockSpec((None, t, LANES), lambda h, i: (h, i, 0))] + (ex.specs if ex else []),
        scratch_shapes=(ex.scratch if ex else []) + own_scratch,
        compiler_params=_params(dimension_semantics=("arbitrary", "arbitrary"),
                                has_side_effects=ex is not None),
    )(proj, proj, proj, cumx, cumx, *ex_arrays)


def _attn_bwd(proj, do, o, lse, cumx, qcol, *, name, ride=None):
    s = proj.shape[0]
    w = cumx.shape[1]
    nhp = w // LANES
    t = _fit(s, ATTN_TILE_BWD, LANES)
    nq = s // t
    strip = _fit(t, ATTN_STRIP, 16)
    scale = HEAD_DIM ** -0.5
    qb, kb, vb = qcol // LANES, (qcol + w) // LANES, (qcol + 2 * w) // LANES
    tn_dims = (((0,), (0,)), ((), ()))
    nt_dims = (((1,), (1,)), ((), ()))
    ex, ex_arrays = ride if ride is not None else (None, [])

    def kern(*refs):
        own_in, own_out, comm, scratch = _ride_split(ex, refs, 7, 5)
        q_ref, k_ref, v_ref, do_ref, o_ref, l_ref, cx_ref = own_in
        dq_ref, dk_ref, dv_ref, dkc_ref, dqc_ref = own_out
        qa_ref, da_ref, dqa_ref, dka_ref, dva_ref, st0_ref, st1_ref, dpt0_ref, dpt1_ref, pt_ref, dst_ref = scratch
        st_refs, dpt_refs = (st0_ref, st1_ref), (dpt0_ref, dpt1_ref)
        j = pl.program_id(1)
        if ex is not None:
            @pl.when((pl.program_id(0) == 0) & (j == 0))
            def _():
                ex.start(*comm)

        msks = _head_masks(t)

        @pl.when(j == 0)
        def _():
            def build(c, carry):
                rows = pl.ds(pl.multiple_of(c * t, LANES), t)
                q2 = q_ref[rows, :] * scale
                do2 = do_ref[rows, :]
                dd = do2 * o_ref[rows, :].astype(F32)
                delta = jnp.where(msks[0], jnp.sum(jnp.where(msks[0], dd, 0.0), axis=1, keepdims=True),
                                  jnp.sum(jnp.where(msks[1], dd, 0.0), axis=1, keepdims=True))
                bias = cx_ref[rows, :] - l_ref[rows, :]
                for e in range(2):
                    qa_ref[e, rows, :] = _augment(q2, bias, e, bias_slot=0, ones_slot=1)
                    da_ref[e, rows, :] = _augment(do2, -delta, e, bias_slot=0, ones_slot=None)
                return carry
            lax.fori_loop(0, nq, build, 0)
            dqa_ref[...] = jnp.zeros(dqa_ref.shape, F32)

        rows_k = pl.ds(pl.multiple_of(j * t, LANES), t)
        k2, v2 = k_ref[...], v_ref[...]
        ka = [_augment(k2, -cx_ref[rows_k, :], e, bias_slot=1, ones_slot=0) for e in range(2)]
        va = [_augment(v2, None, e, bias_slot=None, ones_slot=0) for e in range(2)]
        dka_ref[...] = jnp.zeros(dka_ref.shape, F32)
        dva_ref[...] = jnp.zeros(dva_ref.shape, F32)

        def scores(k, slot):
            rows_q = pl.ds(pl.multiple_of((nq - 1 - k) * t, LANES), t)
            for e in range(2):
                st_refs[slot][e] = lax.dot_general(ka[e], qa_ref[e, rows_q, :], nt_dims,
                                                   preferred_element_type=F32)
                dpt_refs[slot][e] = lax.dot_general(va[e], da_ref[e, rows_q, :], nt_dims,
                                                    preferred_element_type=F32)

        def tile(k, slot, diagonal):
            rows_q = pl.ds(pl.multiple_of((nq - 1 - k) * t, LANES), t)
            st_ref, dpt_ref = st_refs[slot], dpt_refs[slot]
            for e in range(2):
                for r in range(t // strip):
                    rows = slice(r * strip, (r + 1) * strip)
                    sv = st_ref[e, rows, :]
                    if diagonal:
                        key = r * strip + lax.broadcasted_iota(jnp.int32, (strip, t), 0)
                        qry = lax.broadcasted_iota(jnp.int32, (strip, t), 1)
                        sv = jnp.where(key <= qry, sv, NEG)
                    p = jnp.exp(sv)
                    pt_ref[e, rows, :] = p.astype(BF)
                    dst_ref[e, rows, :] = (p * dpt_ref[e, rows, :]).astype(BF)
            for e in range(2):
                dva_ref[e] += jnp.dot(pt_ref[e], da_ref[e, rows_q, :], preferred_element_type=F32)
                dka_ref[e] += jnp.dot(dst_ref[e], qa_ref[e, rows_q, :], preferred_element_type=F32)
                dqa_ref[e, rows_q, :] += lax.dot_general(dst_ref[e], ka[e], tn_dims, preferred_element_type=F32)

        _two_slot_pipeline(nq - 1 - j, scores, tile)

        dk_ref[...] = jnp.where(msks[0], dka_ref[0], dka_ref[1])
        dv_ref[...] = jnp.where(msks[0], dva_ref[0], dva_ref[1])
        sums = jnp.where(msks[1], dka_ref[0], dka_ref[1]).T
        dkc_ref[0:1, :] = sums[HEAD_DIM + BIAS_LANES:HEAD_DIM + BIAS_LANES + 1, :]
        dkc_ref[1:2, :] = sums[BIAS_LANES:BIAS_LANES + 1, :]

        @pl.when(j == nq - 1)
        def _():
            def flush(c, carry):
                rows = pl.ds(pl.multiple_of(c * t, LANES), t)
                a0, a1 = dqa_ref[0, rows, :], dqa_ref[1, rows, :]
                dq_ref[rows, :] = jnp.where(msks[0], a0, a1) * scale
                sums = jnp.where(msks[1], a0, a1).T
                dqc_ref[0:1, rows] = sums[HEAD_DIM:HEAD_DIM + 1, :]
                dqc_ref[1:2, rows] = sums[0:1, :]
                return carry
            lax.fori_loop(0, nq, flush, 0)

        if ex is not None:
            @pl.when((pl.program_id(0) == nhp - 1) & (j == nq - 1))
            def _():
                ex.wait(*comm)

    full = lambda cb: pl.BlockSpec((s, LANES), lambda h, j: (0, cb + h))
    blk = lambda cb: pl.BlockSpec((t, LANES), lambda h, j: (j, cb + h))
    own_scratch = [pltpu.VMEM((2, s, LANES), BF), pltpu.VMEM((2, s, LANES), BF), pltpu.VMEM((2, s, LANES), F32),
                   pltpu.VMEM((2, t, LANES), F32), pltpu.VMEM((2, t, LANES), F32),
                   pltpu.VMEM((2, t, t), F32), pltpu.VMEM((2, t, t), F32),
                   pltpu.VMEM((2, t, t), F32), pltpu.VMEM((2, t, t), F32),
                   pltpu.VMEM((2, t, t), BF), pltpu.VMEM((2, t, t), BF)]
    return pl.pallas_call(
        kern, name=name,
        out_shape=[jax.ShapeDtypeStruct((s, w), F32)] * 3 + [jax.ShapeDtypeStruct((nhp, 2, s), F32)] * 2
        + (ex.out_shape if ex else []),
        grid=(nhp, nq),
        in_specs=[full(qb), blk(kb), blk(vb), full(0), full(0),
                  pl.BlockSpec((None, s, LANES), lambda h, j: (h, 0, 0)), full(0)] + (ex.specs if ex else []),
        out_specs=[full(0), blk(0), blk(0), pl.BlockSpec((None, 2, t), lambda h, j: (h, 0, j)),
                   pl.BlockSpec((None, 2, s), lambda h, j: (h, 0, 0))] + (ex.specs if ex else []),
        scratch_shapes=(ex.scratch if ex else []) + own_scratch,
        compiler_params=_params(dimension_semantics=("arbitrary", "arbitrary"),
                                has_side_effects=ex is not None),
    )(proj, proj, proj, do, o, lse, cumx, *ex_arrays)


S5_STATES = 256
S5_ROWS = 1024


def _cmul(ar, ai, br, bi):
    return ar * br - ai * bi, ar * bi + ai * br


def _scan_tables(lr, li, reverse):
    w = lr.shape[1]
    row = lax.broadcasted_iota(jnp.int32, (SUBLANES, w), 0)
    if reverse:
        row = SUBLANES - 1 - row
    lr1, li1 = jnp.broadcast_to(lr, (SUBLANES, w)), jnp.broadcast_to(li, (SUBLANES, w))
    lr2, li2 = _cmul(lr1, li1, lr1, li1)
    lr4, li4 = _cmul(lr2, li2, lr2, li2)
    steps = []
    for d, (pr, pi) in zip((1, 2, 4), ((lr1, li1), (lr2, li2), (lr4, li4))):
        keep = row >= d
        steps.append((jnp.where(keep, pr, 0.0), jnp.where(keep, pi, 0.0)))
    cr, ci = lr1, li1
    for bit, (pr, pi) in zip((1, 2, 4), ((lr1, li1), (lr2, li2), (lr4, li4))):
        nr, ni = _cmul(cr, ci, pr, pi)
        has = (row & bit) != 0
        cr, ci = jnp.where(has, nr, cr), jnp.where(has, ni, ci)
    return steps, (cr, ci)


def _scan_local(xr, xi, steps, reverse):
    for d, (pr, pi) in zip((1, 2, 4), steps):
        sh = (SUBLANES - d) if reverse else d
        sr, si = pltpu.roll(xr, sh, 0), pltpu.roll(xi, sh, 0)
        xr, xi = xr + (pr * sr - pi * si), xi + (pr * si + pi * sr)
    return xr, xi


def _scan_carry(xr, xi, car_r, car_i, carry_pow):
    cr, ci = carry_pow
    return xr + (cr * car_r - ci * car_i), xi + (cr * car_i + ci * car_r)


SCAN_UNROLL = 4


def _s5_specs(s, ncb):
    u_spec = pl.BlockSpec((s, LANES), lambda cb, hf: (0, cb))
    wb_spec = pl.BlockSpec((None, None, LANES, S5_STATES), lambda cb, hf: (cb, hf, 0, 0))
    wc_spec = pl.BlockSpec((None, None, S5_STATES, LANES), lambda cb, hf: (cb, hf, 0, 0))
    lam_spec = pl.BlockSpec((1, S5_STATES), lambda cb, hf: (0, 2 * cb + hf))
    d_spec = pl.BlockSpec((1, LANES), lambda cb, hf: (0, cb))
    return u_spec, wb_spec, wc_spec, lam_spec, d_spec


def _s5_project_and_scan(u_ref, wbr_ref, wbi_ref, lr_ref, li_ref, xr_ref, xi_ref, s, rows):
    wbr, wbi = wbr_ref[...], wbi_ref[...]
    for r in range(s // rows):
        sl = pl.ds(r * rows, rows)
        ub = u_ref[sl, :].astype(BF)
        xr_ref[sl, :] = jnp.dot(ub, wbr, preferred_element_type=F32)
        xi_ref[sl, :] = jnp.dot(ub, wbi, preferred_element_type=F32)
    steps, cpow = _scan_tables(lr_ref[...], li_ref[...], False)

    unroll = _fit(s // SUBLANES, SCAN_UNROLL, 1)

    def body(b, carry):
        car_r, car_i = carry
        sls = [pl.ds(pl.multiple_of((b * unroll + q) * SUBLANES, SUBLANES), SUBLANES) for q in range(unroll)]
        blocks = [_scan_local(xr_ref[sl, :], xi_ref[sl, :], steps, False) for sl in sls]
        for sl, (xr, xi) in zip(sls, blocks):
            xr, xi = _scan_carry(xr, xi, car_r, car_i, cpow)
            xr_ref[sl, :] = xr
            xi_ref[sl, :] = xi
            car_r, car_i = xr[SUBLANES - 1:SUBLANES, :], xi[SUBLANES - 1:SUBLANES, :]
        return car_r, car_i

    zero = jnp.zeros((1, S5_STATES), F32)
    lax.fori_loop(0, s // SUBLANES // unroll, body, (zero, zero))


def _s5_fwd(proj, wb_re, wb_im, wc_re, wc_im, lam_re, lam_im, dskip, *, name, ride=None):
    s = proj.shape[0]
    w = dskip.shape[1]
    ncb = w // LANES
    rows = _fit(s, S5_ROWS, SUBLANES)
    ex, ex_arrays = ride if ride is not None else (None, [])

    def kern(*refs):
        own_in, (y_ref,), comm, (xr_ref, xi_ref) = _ride_split(ex, refs, 8, 1)
        u_ref, wbr_ref, wbi_ref, wcr_ref, wci_ref, lr_ref, li_ref, d_ref = own_in
        hf = pl.program_id(1)
        if ex is not None:
            @pl.when((pl.program_id(0) == 0) & (hf == 0))
            def _():
                ex.start(*comm)

            @pl.when((pl.program_id(0) == ncb - 1) & (hf == 0))
            def _():
                ex.forward(*comm)

        _s5_project_and_scan(u_ref, wbr_ref, wbi_ref, lr_ref, li_ref, xr_ref, xi_ref, s, rows)
        wcr, wci = wcr_ref[...], wci_ref[...]
        for r in range(s // rows):
            sl = pl.ds(r * rows, rows)
            y = (jnp.dot(xr_ref[sl, :].astype(BF), wcr, preferred_element_type=F32)
                 - jnp.dot(xi_ref[sl, :].astype(BF), wci, preferred_element_type=F32))

            @pl.when(hf == 0)
            def _(y=y, sl=sl):
                y_ref[sl, :] = y + d_ref[...] * u_ref[sl, :]

            @pl.when(hf == 1)
            def _(y=y, sl=sl):
                y_ref[sl, :] += y

        if ex is not None:
            @pl.when((pl.program_id(0) == ncb - 1) & (hf == 1))
            def _():
                ex.wait(*comm)

    u_spec, wb_spec, wc_spec, lam_spec, d_spec = _s5_specs(s, ncb)
    res = pl.pallas_call(
        kern, name=name, out_shape=[jax.ShapeDtypeStruct((s, w), F32)] + (ex.out_shape if ex else []),
        grid=(ncb, 2),
        in_specs=[u_spec, wb_spec, wb_spec, wc_spec, wc_spec, lam_spec, lam_spec, d_spec] + (ex.specs if ex else []),
        out_specs=[u_spec] + (ex.specs if ex else []),
        scratch_shapes=(ex.scratch if ex else []) + [pltpu.VMEM((s, S5_STATES), F32), pltpu.VMEM((s, S5_STATES), F32)],
        compiler_params=_params(dimension_semantics=("arbitrary", "arbitrary"), has_side_effects=ex is not None),
    )(proj, wb_re, wb_im, wc_re, wc_im, lam_re, lam_im, dskip, *ex_arrays)
    return res[0], res[1:]


def _s5_bwd(proj, dy, wb_re, wb_im, wc_re, wc_im, lam_re, lam_im, dskip, *, name, ride=None):
    s = proj.shape[0]
    w = dskip.shape[1]
    ncb = w // LANES
    rows = _fit(s, S5_ROWS, SUBLANES)
    tn_dims = (((0,), (0,)), ((), ()))
    nt_dims = (((1,), (1,)), ((), ()))
    ex, ex_arrays = ride if ride is not None else (None, [])

    def kern(*refs):
        own_in, own_out, comm, scratch = _ride_split(ex, refs, 9, 8)
        u_ref, dy_ref, wbr_ref, wbi_ref, wcr_ref, wci_ref, lr_ref, li_ref, d_ref = own_in
        du_ref, dwbr_ref, dwbi_ref, dwcr_ref, dwci_ref, dlr_ref, dli_ref, dd_ref = own_out
        xr_ref, xi_ref, gr_ref, gi_ref = scratch
        hf = pl.program_id(1)
        if ex is not None:
            @pl.when((pl.program_id(0) == 0) & (hf == 0))
            def _():
                ex.start(*comm)

        _s5_project_and_scan(u_ref, wbr_ref, wbi_ref, lr_ref, li_ref, xr_ref, xi_ref, s, rows)

        wcr, wci = wcr_ref[...], wci_ref[...]
        dwcr = jnp.zeros((S5_STATES, LANES), F32)
        dwci = jnp.zeros((S5_STATES, LANES), F32)
        ddsk = jnp.zeros((1, LANES), F32)
        for r in range(s // rows):
            sl = pl.ds(r * rows, rows)
            dyf = dy_ref[sl, :]
            dyb = dyf.astype(BF)
            gr_ref[sl, :] = lax.dot_general(dyb, wcr, nt_dims, preferred_element_type=F32)
            gi_ref[sl, :] = -lax.dot_general(dyb, wci, nt_dims, preferred_element_type=F32)
            dwcr = dwcr + lax.dot_general(xr_ref[sl, :].astype(BF), dyb, tn_dims, preferred_element_type=F32)
            dwci = dwci - lax.dot_general(xi_ref[sl, :].astype(BF), dyb, tn_dims, preferred_element_type=F32)
            ddsk = ddsk + jnp.sum(dyf * u_ref[sl, :], axis=0, keepdims=True)
        dwcr_ref[...] = dwcr
        dwci_ref[...] = dwci

        @pl.when(hf == 0)
        def _():
            dd_ref[...] = ddsk

        steps, cpow = _scan_tables(lr_ref[...], -li_ref[...], True)
        row = lax.broadcasted_iota(jnp.int32, (SUBLANES, S5_STATES), 0)
        nblk = s // SUBLANES

        unroll = _fit(nblk, SCAN_UNROLL, 1)

        def body(k, carry):
            car_r, car_i, ar, ai = carry
            sls = [pl.ds(pl.multiple_of((nblk - 1 - k * unroll - q) * SUBLANES, SUBLANES), SUBLANES)
                   for q in range(unroll)]
            blocks = [_scan_local(gr_ref[sl, :], gi_ref[sl, :], steps, True) for sl in sls]
            for sl, (g_r, g_i) in zip(sls, blocks):
                g_r, g_i = _scan_carry(g_r, g_i, car_r, car_i, cpow)
                gr_ref[sl, :] = g_r
                gi_ref[sl, :] = g_i
                nr = jnp.where(row == SUBLANES - 1, car_r, pltpu.roll(g_r, SUBLANES - 1, 0))
                ni = jnp.where(row == SUBLANES - 1, car_i, pltpu.roll(g_i, SUBLANES - 1, 0))
                xr, xi = xr_ref[sl, :], xi_ref[sl, :]
                ar = ar + (xr * nr + xi * ni)
                ai = ai + (xr * ni - xi * nr)
                car_r, car_i = g_r[0:1, :], g_i[0:1, :]
            return car_r, car_i, ar, ai

        zero = jnp.zeros((1, S5_STATES), F32)
        zacc = jnp.zeros((SUBLANES, S5_STATES), F32)
        _, _, ar, ai = lax.fori_loop(0, nblk // unroll, body, (zero, zero, zacc, zacc))
        dlr_ref[...] = jnp.sum(ar, axis=0, keepdims=True)
        dli_ref[...] = jnp.sum(ai, axis=0, keepdims=True)

        wbr, wbi = wbr_ref[...], wbi_ref[...]
        dwbr = jnp.zeros((LANES, S5_STATES), F32)
        dwbi = jnp.zeros((LANES, S5_STATES), F32)
        for r in range(s // rows):
            sl = pl.ds(r * rows, rows)
            grb, gib = gr_ref[sl, :].astype(BF), gi_ref[sl, :].astype(BF)
            ub = u_ref[sl, :].astype(BF)
            dwbr = dwbr + lax.dot_general(ub, grb, tn_dims, preferred_element_type=F32)
            dwbi = dwbi + lax.dot_general(ub, gib, tn_dims, preferred_element_type=F32)
            du = (lax.dot_general(grb, wbr, nt_dims, preferred_element_type=F32)
                  + lax.dot_general(gib, wbi, nt_dims, preferred_element_type=F32))

            @pl.when(hf == 0)
            def _(du=du, sl=sl):
                du_ref[sl, :] = du + d_ref[...] * dy_ref[sl, :]

            @pl.when(hf == 1)
            def _(du=du, sl=sl):
                du_ref[sl, :] += du
        dwbr_ref[...] = dwbr
        dwbi_ref[...] = dwbi
        if ex is not None:
            @pl.when((pl.program_id(0) == ncb - 1) & (hf == 1))
            def _():
                ex.wait(*comm)

    u_spec, wb_spec, wc_spec, lam_spec, d_spec = _s5_specs(s, ncb)
    dwb_spec = pl.BlockSpec((None, None, LANES, S5_STATES), lambda cb, hf: (cb, hf, 0, 0))
    dwc_spec = pl.BlockSpec((None, None, S5_STATES, LANES), lambda cb, hf: (cb, hf, 0, 0))
    state = pltpu.VMEM((s, S5_STATES), F32)
    return pl.pallas_call(
        kern, name=name,
        out_shape=[jax.ShapeDtypeStruct((s, w), F32),
                   jax.ShapeDtypeStruct((ncb, 2, LANES, S5_STATES), F32), jax.ShapeDtypeStruct((ncb, 2, LANES, S5_STATES), F32),
                   jax.ShapeDtypeStruct((ncb, 2, S5_STATES, LANES), F32), jax.ShapeDtypeStruct((ncb, 2, S5_STATES, LANES), F32),
                   jax.ShapeDtypeStruct((1, 4 * w), F32), jax.ShapeDtypeStruct((1, 4 * w), F32),
                   jax.ShapeDtypeStruct((1, w), F32)] + (ex.out_shape if ex else []),
        grid=(ncb, 2),
        in_specs=[u_spec, u_spec, wb_spec, wb_spec, wc_spec, wc_spec, lam_spec, lam_spec, d_spec]
        + (ex.specs if ex else []),
        out_specs=[u_spec, dwb_spec, dwb_spec, dwc_spec, dwc_spec, lam_spec, lam_spec, d_spec]
        + (ex.specs if ex else []),
        scratch_shapes=(ex.scratch if ex else []) + [state, state, state, state],
        compiler_params=_params(dimension_semantics=("arbitrary", "arbitrary"), has_side_effects=ex is not None),
    )(proj, dy, wb_re, wb_im, wc_re, wc_im, lam_re, lam_im, dskip, *ex_arrays)


def _s5_discretise(lam_re, lam_im, log_dt, b_re, b_im):
    lr = jnp.minimum(lam_re, -EIG_CLIP)
    li = lam_im
    dt = jnp.exp(log_dt)[:, None]
    mag = jnp.exp(lr * dt)
    lbr, lbi = mag * jnp.cos(li * dt), mag * jnp.sin(li * dt)
    den = lr * lr + li * li
    fr = ((lbr - 1.0) * lr + lbi * li) / den
    fi = (lbi * lr - (lbr - 1.0) * li) / den
    bbr = fr[:, None, :] * b_re - fi[:, None, :] * b_im
    bbi = fr[:, None, :] * b_im + fi[:, None, :] * b_re
    return lbr, lbi, bbr, bbi


def _s5_operand(mats, channels_first):
    g, a, b = mats.shape
    gl = LANES // 2 // SSM_H
    ncb = g // (2 * gl)
    m = mats.reshape(ncb, 2, gl, a, b)
    eye = jnp.eye(gl, dtype=mats.dtype)
    inner = (m[:, :, :, :, None, :] * eye[None, None, :, None, :, None]).reshape(ncb, 2, gl * a, gl * b)
    zeros = jnp.zeros_like(inner[:, 0])
    axis = 1 if channels_first else 2
    return jnp.stack([jnp.concatenate([inner[:, 0], zeros], axis=axis),
                      jnp.concatenate([zeros, inner[:, 1]], axis=axis)], axis=1)


def _s5_block_grads(dwb, a, b, transpose):
    ncb = dwb.shape[0]
    gl = LANES // 2 // (a if not transpose else b)
    if not transpose:
        d = dwb.reshape(ncb, 2, 2, gl, a, gl, b)
        parts = [[d[:, hf, hf, g, :, g, :] for g in range(gl)] for hf in range(2)]
    else:
        d = dwb.reshape(ncb, 2, gl, a, 2, gl, b)
        parts = [[d[:, hf, g, :, hf, g, :] for g in range(gl)] for hf in range(2)]
    st = jnp.stack([jnp.stack(p, axis=1) for p in parts], axis=1)
    return st.reshape(ncb * 2 * gl, a, b)


def _adamw(parts, w, m, v, *, name):
    depth, r, c = w.shape
    assert len(parts) == depth
    npart = parts[0].shape[0]
    row_bytes = 4 * (-(-c // LANES) * LANES)
    align = 16 if parts[0].dtype == BF else SUBLANES
    budget = VMEM_LIMIT // 2 // (2 * (depth * npart + 7) * row_bytes)
    tr = _fit(r, max(align, budget // align * align), align)
    nr = r // tr
    c1 = 1.0 / (1.0 - ADAM_B1 ** ADAM_STEP)
    c2 = 1.0 / (1.0 - ADAM_B2 ** ADAM_STEP)

    def kern(*refs):
        p_refs = refs[:depth]
        w_ref, m_ref, v_ref, g_ref, d_ref, nm_ref, nv_ref = refs[depth:]
        layer = pl.program_id(0)
        for l in range(depth):
            @pl.when(layer == l)
            def _(p_ref=p_refs[l]):
                g = p_ref[0].astype(F32)
                for q in range(1, npart):
                    g = g + p_ref[q].astype(F32)
                m2 = ADAM_B1 * m_ref[...] + (1.0 - ADAM_B1) * g
                v2 = ADAM_B2 * v_ref[...] + (1.0 - ADAM_B2) * (g * g)
                upd = (m2 * c1) / (jnp.sqrt(v2 * c2) + ADAM_EPS) + ADAM_WD * w_ref[...]
                g_ref[...] = g
                d_ref[...] = -ADAM_LR * upd
                nm_ref[...] = m2
                nv_ref[...] = v2

    def part_spec(l):
        return pl.BlockSpec((npart, tr, c),
                            lambda ly, i: (0, jnp.where(ly == l, i, jnp.where(ly < l, 0, nr - 1)), 0))

    spec = pl.BlockSpec((None, tr, c), lambda ly, i: (ly, i, 0))
    return pl.pallas_call(
        kern, name=name, out_shape=[jax.ShapeDtypeStruct((depth, r, c), F32)] * 4, grid=(depth, nr),
        in_specs=[part_spec(l) for l in range(depth)] + [spec, spec, spec],
        out_specs=[spec] * 4,
        compiler_params=_params(dimension_semantics=("arbitrary", "arbitrary")),
    )(*parts, w, m, v)


def _sum_parts(parts, *, name):
    npart, r, c = parts.shape

    def kern(p_ref, o_ref):
        g = p_ref[0]
        for q in range(1, npart):
            g = g + p_ref[q]
        o_ref[...] = g

    return pl.pallas_call(kern, name=name, out_shape=jax.ShapeDtypeStruct((r, c), F32), compiler_params=_params())(parts)


class _Exchange:
    def __init__(self, arrays, gather):
        self.n = len(arrays)
        self.gather = gather
        self.out_shape = [jax.ShapeDtypeStruct(((NDEV,) + a.shape) if gather else a.shape, a.dtype) for a in arrays]
        self.scratch = [pltpu.SemaphoreType.DMA((self.n, NDEV - 1)), pltpu.SemaphoreType.DMA((self.n, NDEV - 1)),
                        pltpu.SemaphoreType.DMA((self.n,))]
        self.specs = [pl.BlockSpec(memory_space=pl.ANY)] * self.n

    def _copies(self, srcs, dsts, sems):
        send_sems, recv_sems, local_sems = sems
        x, y, c = lax.axis_index("x"), lax.axis_index("y"), lax.axis_index("c")
        me = 4 * x + 2 * y + c
        local = [pltpu.make_async_copy(srcs[a] if self.gather else srcs[a].at[me], dsts[a].at[me], local_sems.at[a])
                 for a in range(self.n)]
        remote = []
        for k in (1, 2, 4, 3, 5, 6, 7):
            px, py, pc = x ^ ((k >> 2) & 1), y ^ ((k >> 1) & 1), c ^ (k & 1)
            peer = 4 * px + 2 * py + pc
            for a in range(self.n):
                src = srcs[a] if self.gather else srcs[a].at[peer]
                mk = functools.partial(
                    pltpu.make_async_remote_copy, src_ref=src,
                    send_sem=send_sems.at[a, k - 1], recv_sem=recv_sems.at[a, k - 1],
                    device_id=(px, py, pc), device_id_type=pl.DeviceIdType.MESH)
                remote.append((mk(dst_ref=dsts[a].at[me]), mk(dst_ref=dsts[a].at[peer])))
        return local, remote

    def _gather_copies(self, srcs, dsts, sems):
        send_sems, recv_sems, local_sems = sems
        x, y, c = lax.axis_index("x"), lax.axis_index("y"), lax.axis_index("c")
        block = lambda px, py, pc: 4 * px + 2 * py + pc
        me = block(x, y, c)
        chips = [(1 - x, y), (x, 1 - y), (1 - x, 1 - y)]
        local = [pltpu.make_async_copy(srcs[a], dsts[a].at[me], local_sems.at[a]) for a in range(self.n)]
        own, passed = [], []
        for a in range(self.n):
            def copy(k, blk, to, src=None, a=a):
                return pltpu.make_async_remote_copy(
                    src_ref=dsts[a].at[blk] if src is None else src, dst_ref=dsts[a].at[blk],
                    send_sem=send_sems.at[a, k], recv_sem=recv_sems.at[a, k],
                    device_id=to, device_id_type=pl.DeviceIdType.MESH)
            sib = (x, y, 1 - c)
            own.append((copy(0, me, sib, srcs[a]), copy(0, block(x, y, 1 - c), sib)))
            for j, (px, py) in enumerate(chips):
                own.append((copy(1 + j, me, (px, py, c), srcs[a]), copy(1 + j, block(px, py, c), (px, py, c))))
            for j, (px, py) in enumerate(chips):
                passed.append((copy(4 + j, block(px, py, c), sib), copy(4 + j, block(px, py, 1 - c), sib)))
        return local, own, passed

    def start(self, srcs, dsts, sems):
        if self.gather:
            local, own, _ = self._gather_copies(srcs, dsts, sems)
            for cp in local:
                cp.start()
            for send, _ in own:
                send.start()
            return
        local, remote = self._copies(srcs, dsts, sems)
        for cp in local:
            cp.start()
        for send, _ in remote:
            send.start()

    def forward(self, srcs, dsts, sems):
        if not self.gather:
            return
        _, own, passed = self._gather_copies(srcs, dsts, sems)
        for a in range(self.n):
            for j in range(3):
                own[4 * a + 1 + j][1].wait_recv()
                passed[3 * a + j][0].start()

    def wait(self, srcs, dsts, sems):
        if self.gather:
            local, own, passed = self._gather_copies(srcs, dsts, sems)
            for a in range(self.n):
                own[4 * a][1].wait_recv()
            for _, arrival in passed:
                arrival.wait_recv()
            for send, _ in own + passed:
                send.wait_send()
            for cp in local:
                cp.wait()
            return
        local, remote = self._copies(srcs, dsts, sems)
        for send, arrival in remote:
            send.wait_send()
            arrival.wait_recv()
        for cp in local:
            cp.wait()


def _exchange(arrays, gather, *, name):
    ex = _Exchange(arrays, gather)
    n = ex.n

    def kern(*refs):
        srcs, dsts, sems = refs[:n], refs[n:2 * n], refs[2 * n:]
        ex.start(srcs, dsts, sems)
        ex.forward(srcs, dsts, sems)
        ex.wait(srcs, dsts, sems)

    return pl.pallas_call(
        kern, name=name, out_shape=ex.out_shape, in_specs=ex.specs, out_specs=ex.specs, scratch_shapes=ex.scratch,
        compiler_params=pltpu.CompilerParams(has_side_effects=True),
    )(*arrays)


def _pack(arrays):
    flat = jnp.concatenate([a.reshape(-1).astype(F32) for a in arrays])
    pad = (-flat.shape[0]) % (SUBLANES * LANES)
    return jnp.pad(flat, (0, pad)).reshape(-1, LANES)


def _unpack(buf, like):
    flat = buf.reshape(-1)
    out, off = [], 0
    for a in like:
        sz = math.prod(a.shape)
        out.append(flat[off:off + sz].reshape(a.shape))
        off += sz
    return out


def _row(v):
    return v.reshape(1, -1)


def _layer_fwd(x, mod, p, l, ride=None, on_receive=None, target=None):
    s, d = x.shape
    sw = d // 2
    nh = d // LANES
    shift_m, scale_m, gate_m, shift_f, scale_f, gate_f = mod
    n = lambda tag: f"{tag}{l}"
    sv = {}

    h1, = _rowwise(lambda xv, g, sc, sh: (xv * _rms(xv) * g) * (1.0 + sc) + sh,
                   [x], [p['g_pre_mix'], scale_m, shift_m], [(d, BF, 'tile')], name=n("pre_mix"))
    proj_a = _mm(h1, p['w_in_a'], name=n("proj_a"))
    flog = _mm(h1, p['w_in_f'], name=n("proj_f"))
    gates = _mm(h1, p['w_in_g'], name=n("proj_g"))

    y_s5, received = _s5_fwd(proj_a, p['wb_re'], p['wb_im'], p['wc_re'], p['wc_im'], p['lamb_re'], p['lamb_im'],
                             p['d_skip'], name=n("s5_fwd"), ride=ride('s5') if ride else None)
    if on_receive is not None:
        on_receive('s5', received)
    z, = _rowwise(_gelu, [y_s5], [], [(sw, BF, 'tile')], name=n("gelu"))
    tglu, ys = _mm_fused(z, [p['w_glu']], [y_s5], lambda tv, yv, b: (tv, _gelu(yv) * _sigmoid(tv + b)),
                         [F32, BF], rows=[p['b_glu']], name=n("glu_mm"))

    cumx = _cum_fwd(flog, p['b_f_row'], nh, name=n("cum_fwd"))
    ya, lse, *received = _attn_fwd(proj_a, cumx, sw, name=n("attn_fwd"), ride=ride('attn') if ride else None)
    if on_receive is not None:
        on_receive('attn', received)

    am = _mm(ys, p['w_pa'], name=n("pa_mm"))
    bm, merged = _mm_fused(ya, [p['w_pb']], [am, (gates, 0), (gates, 1)],
                           lambda b, a, ga, gb: (b, _sigmoid(ga) * a + _sigmoid(gb) * b), [F32, BF],
                           name=n("pb_mm"))
    ym = _mm(merged, p['w_o'], name=n("o_mm"))
    def post_mix_pre_ffn(xv, yv, g, gt, g2, sc, sh):
        x2v = xv + gt * (yv * _rms(yv) * g)
        return x2v, (x2v * _rms(x2v) * g2) * (1.0 + sc) + sh

    x2, h2 = _rowwise(post_mix_pre_ffn, [x, ym], [p['g_post_mix'], gate_m, p['g_pre_ffn'], scale_f, shift_f],
                      [(d, F32, 'tile'), (d, BF, 'tile')], name=n("post_mix_pre_ffn"))
    gt, up, act = _mm_fused(h2, [p['w_ffn_gate'], p['w_ffn_up']], [], lambda g, u: (g, u, _silu(g) * u),
                            [F32, F32, BF], tb=True, name=n("gate_up_mm"))
    yf = _mm(act, p['w_ffn_down'], name=n("down_mm"))
    sv.update(x=x, h1=h1, proj_a=proj_a, flog=flog, gates=gates, y_s5=y_s5, z=z, tglu=tglu, ys=ys, cumx=cumx,
              ya=ya, lse=lse, am=am, bm=bm, merged=merged, ym=ym, x2=x2, h2=h2, gt=gt, up=up,
              act=act, yf=yf)
    if target is None:
        x3, = _rowwise(lambda xv, yv, g, gt_: xv + gt_ * (yv * _rms(yv) * g),
                       [x2, yf], [p['g_post_ffn'], gate_f], [(d, F32, 'tile')], name=n("post_ffn"))
        return x3, sv

    def output_and_loss(xv, yv, tv, g, gt_):
        r = _rms(yv)
        nf = yv * r
        e = xv + gt_ * (nf * g) - tv
        dy = e * (1.0 / d)
        return (dy, _norm_bwd(dy * gt_ * g, nf, r), dy * (nf * g), dy * gt_ * nf,
                jnp.sum(e * e, axis=1, keepdims=True) * (0.5 / d))

    dy, dyf, dgate_f, dg, loss = _rowwise(
        output_and_loss, [x2, yf, target], [p['g_post_ffn'], gate_f],
        [(d, F32, 'tile'), (d, BF, 'tile'), (d, F32, 'sum'), (d, F32, 'sum'), (1, F32, 'sum')], name="output_loss")
    sv['post_ffn_bwd'] = (dyf, dgate_f, dg)
    return (dy, loss), sv


def _layer_bwd(dx3, sv, mod, p, l, make_ride=None, on_receive=None, carried=None, defer_tail=False):
    x, x2 = sv['x'], sv['x2']
    s, d = x.shape
    sw = d // 2
    nh = d // LANES
    shift_m, scale_m, gate_m, shift_f, scale_f, gate_f = mod
    n = lambda tag: f"{tag}{l}"
    gw, gs = {}, {}

    def post_bwd(dxo, yv, g, gate):
        r = _rms(yv)
        nf = yv * r
        dn = dxo * gate * g
        return _norm_bwd(dn, nf, r), dxo * (nf * g), dxo * gate * nf

    def pre_bwd(dh, dres, xv, g, sc):
        r = _rms(xv)
        xh = xv * r
        n3 = xh * g
        dn3 = dh * (1.0 + sc)
        return dres + _norm_bwd(dn3 * g, xh, r), dh, dh * n3, dn3 * xh

    if 'post_ffn_bwd' in sv:
        dyf, dgate_f, gs['g_post_ffn'] = sv['post_ffn_bwd']
    else:
        dyf, dgate_f, gs['g_post_ffn'] = _rowwise(
            post_bwd, [dx3, sv['yf']], [p['g_post_ffn'], gate_f],
            [(d, BF, 'tile'), (d, F32, 'sum'), (d, F32, 'sum')], name=n("post_ffn_bwd"))
    gw['w_ffn_down'] = _mm(sv['act'], dyf, ta=True, out_dtype=BF, tm=1408, name=n("down_bwd_w"))

    def swiglu_bwd(da, g, u):
        sg = _sigmoid(g)
        return da * u * (sg * (1.0 + g * (1.0 - sg))), da * (g * sg)

    dgt, dup = _mm_fused(dyf, [p['w_ffn_down']], [sv['gt'], sv['up']], swiglu_bwd, [BF, BF], tb=True,
                         name=n("down_bwd_x"))
    dh2 = _mm(dgt, p['w_ffn_gate'], tm=1024, second=(dup, p['w_ffn_up']), name=n("gate_up_bwd_x"))
    gw['w_ffn_gate'] = _mm(dgt, sv['h2'], ta=True, out_dtype=BF, tm=1408, name=n("gate_bwd_w"))
    gw['w_ffn_up'] = _mm(dup, sv['h2'], ta=True, out_dtype=BF, tm=1408, name=n("up_bwd_w"))
    def pre_ffn_post_mix_bwd(dh, dres, xv, yv, g, sc, g2, gate):
        dx2v, dsh, dsc, dg = pre_bwd(dh, dres, xv, g, sc)
        return (dx2v, dsh, dsc, dg) + post_bwd(dx2v, yv, g2, gate)

    dx2, dshift_f, dscale_f, gs['g_pre_ffn'], dym, dgate_m, gs['g_post_mix'] = _rowwise(
        pre_ffn_post_mix_bwd, [dh2, dx3, x2, sv['ym']], [p['g_pre_ffn'], scale_f, p['g_post_mix'], gate_m],
        [(d, F32, 'tile'), (d, F32, 'sum'), (d, F32, 'sum'), (d, F32, 'sum'),
         (d, BF, 'tile'), (d, F32, 'sum'), (d, F32, 'sum')], name=n("pre_ffn_post_mix_bwd"))
    gw['w_o'] = _mm(sv['merged'], dym, ta=True, out_dtype=BF, name=n("o_bwd_w"))

    def merge_bwd(dm, a, b, ga, gb):
        sa, sb = _sigmoid(ga), _sigmoid(gb)
        return dm * sa, dm * sb, dm * a * sa * (1.0 - sa), dm * b * sb * (1.0 - sb)

    da_, db_, dga, dgb = _mm_fused(dym, [p['w_o']], [sv['am'], sv['bm'], (sv['gates'], 0), (sv['gates'], 1)],
                                   merge_bwd, [BF] * 4, tb=True, name=n("o_bwd_x"))
    dys = _mm(da_, p['w_pa'], tb=True, name=n("pa_bwd_x"))
    gw['w_pa'] = _mm(sv['ys'], da_, ta=True, out_dtype=BF, name=n("pa_bwd_w"))
    dya = _mm(db_, p['w_pb'], tb=True, name=n("pb_bwd_x"))
    gw['w_pb'] = _mm(sv['ya'], db_, ta=True, out_dtype=BF, name=n("pb_bwd_w"))

    sent = list(gw)
    dq, dk, dv, dkc, dqc, *received = _attn_bwd(
        sv['proj_a'], dya, sv['ya'], sv['lse'], sv['cumx'], sw, name=n("attn_bwd"),
        ride=make_ride({k: gw[k] for k in sent}) if make_ride is not None else None)
    if on_receive is not None:
        on_receive(sent, received)
    dcum = jnp.stack([-dkc.reshape(nh, s), dqc.reshape(nh, s)])
    dflog, dbf = _cum_bwd(dcum, sv['flog'], p['b_f_col'], name=n("cum_bwd"))
    gs['b_f'] = dbf.reshape(nh)

    def glu_bwd(dy_, yv, tv, b):
        zv = _gelu(yv)
        sg = _sigmoid(tv + b)
        dt = dy_ * zv * sg * (1.0 - sg)
        return dt, dy_ * sg, dt

    dt, dz1, gs['b_glu'] = _rowwise(glu_bwd, [dys, sv['y_s5'], sv['tglu']], [p['b_glu']],
                                    [(sw, BF, 'tile'), (sw, F32, 'tile'), (sw, F32, 'sum')], name=n("glu_bwd"))
    dy_s5, = _mm_fused(dt, [p['w_glu']], [dz1, sv['y_s5']], lambda dz2, a, yv: ((a + dz2) * _gelu_grad(yv),),
                       [F32], tb=True, name=n("glu_bwd_x"))
    gw['w_glu'] = _mm(sv['z'], dt, ta=True, out_dtype=BF, name=n("glu_bwd_w"))
    du, dwbr, dwbi, dwcr, dwci, dlr, dli, gs['d_skip'], *received = _s5_bwd(
        sv['proj_a'], dy_s5, p['wb_re'], p['wb_im'], p['wc_re'], p['wc_im'], p['lamb_re'], p['lamb_im'], p['d_skip'],
        name=n("s5_bwd"), ride=make_ride(carried[1]) if carried else None)
    if carried:
        carried[0](list(carried[1]), received)
    g_ = sw // SSM_H
    pst = p['lamb_re'].shape[1] // g_
    gs['lamb_re'], gs['lamb_im'] = dlr.reshape(g_, pst), dli.reshape(g_, pst)
    gs['bbar_re'] = _s5_block_grads(dwbr, SSM_H, pst, False)
    gs['bbar_im'] = _s5_block_grads(dwbi, SSM_H, pst, False)
    gs['c_re'] = _s5_block_grads(dwcr, pst, SSM_H, True).transpose(0, 2, 1)
    gs['c_im'] = _s5_block_grads(dwci, pst, SSM_H, True).transpose(0, 2, 1)

    dproj = jnp.concatenate([du.astype(BF), dq.astype(BF), dk.astype(BF), dv.astype(BF), dflog, dga, dgb], axis=1)
    gw['w_in'] = _mm(sv['h1'], dproj, ta=True, out_dtype=BF, tn=1408, name=n("proj_bwd_w"))
    if make_ride is not None:
        gw = {k: g for k, g in gw.items() if k not in sent}
    if make_ride is not None and not defer_tail:
        dh1, received = _mm(dproj, p['w_in_all'], tb=True, tk=1408, name=n("proj_bwd_x"), ride=make_ride(gw))
        on_receive(list(gw), received)
        gw = {}
    else:
        dh1 = _mm(dproj, p['w_in_all'], tb=True, tk=1408, name=n("proj_bwd_x"))
    dx, dshift_m, dscale_m, gs['g_pre_mix'] = _rowwise(
        pre_bwd, [dh1, dx2, x], [p['g_pre_mix'], scale_m],
        [(d, F32, 'tile'), (d, F32, 'sum'), (d, F32, 'sum'), (d, F32, 'sum')], name=n("pre_mix_bwd"))
    dmod = [dshift_m, dscale_m, dgate_m, dshift_f, dscale_f, dgate_f]
    return dx, gw, dmod, gs


def _unshard(k, blocks):
    if k in COL_SHARDED:
        return blocks.transpose(1, 0, 2).reshape(blocks.shape[1], NDEV * blocks.shape[2])
    return blocks.reshape(NDEV * blocks.shape[1], blocks.shape[2])


def _to_slabs(k, g):
    if k == 'w_in':
        d = g.shape[0]
        nh = d // LANES
        g = jnp.concatenate([g[:, :2 * d + nh], g[:, 2 * d + LANES:]], axis=1)
    if k in COL_SHARDED:
        return g.reshape(g.shape[0], NDEV, g.shape[1] // NDEV).transpose(1, 0, 2)
    return g.reshape(NDEV, g.shape[0] // NDEV, g.shape[1])


def _prep_w_in(w_in):
    d = w_in.shape[0]
    nh = d // LANES
    fcol = 2 * d
    p = {}
    p['w_in_a'] = w_in[:, :fcol]
    p['w_in_f'] = jnp.pad(w_in[:, fcol:fcol + nh], ((0, 0), (0, LANES - nh)))
    p['w_in_g'] = w_in[:, fcol + nh:]
    p['w_in_all'] = jnp.concatenate([p['w_in_a'], p['w_in_f'], p['w_in_g']], axis=1)
    return p


def _prep_small(small):
    nh = small['b_f'].shape[0]
    p = {}
    for k in ('g_pre_mix', 'g_post_mix', 'g_pre_ffn', 'g_post_ffn', 'd_skip', 'b_glu'):
        p[k] = _row(small[k])
    p['b_f_row'] = jnp.pad(_row(small['b_f']), ((0, 0), (0, LANES - nh)))
    p['b_f_col'] = small['b_f'].reshape(nh, 1)
    lbr, lbi, bbr, bbi = _s5_discretise(small['lam_re'], small['lam_im'], small['log_dt'], small['b_re'], small['b_im'])
    p['lamb_re'], p['lamb_im'] = _row(lbr), _row(lbi)
    p['wb_re'] = _s5_operand(bbr, True).astype(BF)
    p['wb_im'] = _s5_operand(bbi, True).astype(BF)
    p['wc_re'] = _s5_operand(small['c_re'].transpose(0, 2, 1), False).astype(BF)
    p['wc_im'] = _s5_operand(small['c_im'].transpose(0, 2, 1), False).astype(BF)
    return p


def _local_step(x, target, mods, ps, small, hooks=None):
    depth = len(ps)
    s, d = x.shape
    hooks = hooks or {}
    saved = []
    h = x
    for l in range(depth):
        h, sv = _layer_fwd(h, mods[l], ps[l], l, ride=functools.partial(hooks['fwd_ride'], l) if hooks else None,
                           on_receive=functools.partial(hooks['fwd_recv'], l) if hooks else None,
                           target=target if l == depth - 1 else None)
        saved.append(sv)

    dy, loss = h
    dmods, gss = [None] * depth, [None] * depth
    unsent = {}
    carried = None
    for l in range(depth - 1, -1, -1):
        def on_receive(names, results, l=l):
            hooks['bwd_recv']([(k, l) for k in names], results)

        dy, gw, dmods[l], gs = _layer_bwd(dy, saved[l], mods[l], ps[l], l,
                                          make_ride=hooks['bwd_ride'] if hooks else None,
                                          on_receive=on_receive if hooks else None,
                                          carried=carried, defer_tail=bool(hooks) and l > 0)
        if hooks and l > 0:
            carried = (on_receive, gw)
        else:
            unsent.update({(k, l): g for k, g in gw.items()})
        sm = small[l]
        _, vjp = jax.vjp(_s5_discretise, sm['lam_re'], sm['lam_im'], sm['log_dt'], sm['b_re'], sm['b_im'])
        gs['lam_re'], gs['lam_im'], gs['log_dt'], gs['b_re'], gs['b_im'] = vjp(
            (gs.pop('lamb_re'), gs.pop('lamb_im'), gs.pop('bbar_re'), gs.pop('bbar_im')))
        gss[l] = gs
    return loss, dy, unsent, dmods, gss


SMALL_LOCAL = ['g_pre_mix', 'g_post_mix', 'g_pre_ffn', 'g_post_ffn', 'lam_re', 'lam_im', 'log_dt', 'b_re', 'b_im',
               'c_re', 'c_im', 'd_skip', 'b_glu', 'b_f']


def kernel(x, c, w_ada, b_ada, g_pre_mix, g_post_mix, g_pre_ffn, g_post_ffn, w_in, lam_re, lam_im, log_dt, b_re, b_im, c_re, c_im, d_skip, w_glu, b_glu, b_f, w_pa, w_pb, w_o, w_ffn_gate, w_ffn_up, w_ffn_down, loss_target, m_w_ada, m_b_ada, m_g_pre_mix, m_g_post_mix, m_g_pre_ffn, m_g_post_ffn, m_w_in, m_lam_re, m_lam_im, m_log_dt, m_b_re, m_b_im, m_c_re, m_c_im, m_d_skip, m_w_glu, m_b_glu, m_b_f, m_w_pa, m_w_pb, m_w_o, m_w_ffn_gate, m_w_ffn_up, m_w_ffn_down, v_w_ada, v_b_ada, v_g_pre_mix, v_g_post_mix, v_g_pre_ffn, v_g_post_ffn, v_w_in, v_lam_re, v_lam_im, v_log_dt, v_b_re, v_b_im, v_c_re, v_c_im, v_d_skip, v_w_glu, v_b_glu, v_b_f, v_w_pa, v_w_pb, v_w_o, v_w_ffn_gate, v_w_ffn_up, v_w_ffn_down):
    args = dict(locals())
    view = lambda k, a: jnp.swapaxes(a, -1, -2) if k in TRANSPOSED else a
    W = {k: view(k, args[k]) for k in WEIGHTS}
    M = {k: view(k, args['m_' + k]) for k in WEIGHTS}
    V = {k: view(k, args['v_' + k]) for k in WEIGHTS}
    depth, d, ncol = w_ada.shape
    s = x.shape[1]
    me = 4 * lax.axis_index("x") + 2 * lax.axis_index("y") + lax.axis_index("c")

    first = ['w_in', 'w_glu']
    c_all, *first_blocks = _exchange([jnp.pad(c, ((0, SUBLANES - 1), (0, 0)))] + [W[k][0].astype(BF) for k in first],
                                     True, name="gather_first")
    c_all = c_all[:, 0, :]

    cond, = _rowwise(_silu, [c_all], [], [(d, F32, 'tile')], name="cond")
    mod_part = jnp.stack([_mm(cond, w_ada[l], name=f"ada_mm{l}") for l in range(depth)], axis=1)
    mod_recv, = _exchange([mod_part.reshape(NDEV, depth, 1, ncol)], False, name="scatter_mod")
    mod_cat = mod_recv.reshape(NDEV, depth, ncol).transpose(1, 0, 2).reshape(depth, NDEV * ncol)
    mod, = _rowwise(lambda a, b: a + b, [mod_cat, b_ada], [], [(NDEV * ncol, F32, 'tile')], name="mod_bias")
    mods = [[mod[l:l + 1, i * d:(i + 1) * d] for i in range(6)] for l in range(depth)]

    small = [{k: W[k][l] for k in SMALL_LOCAL} for l in range(depth)]
    ps = [_prep_small(small[l]) for l in range(depth)]
    rest = [k for k in BIG if k not in first]
    riding = [{'attn': [(k, l) for k in rest], 's5': [(k, l + 1) for k in first if l + 1 < depth]}
              for l in range(depth)]

    def take_weights(keys, results):
        for (k, l), blocks in zip(keys, results):
            full = _unshard(k, blocks)
            ps[l].update(_prep_w_in(full) if k == 'w_in' else {k: full})

    take_weights([(k, 0) for k in first], first_blocks)

    def fwd_ride(l, where):
        blocks = [W[k][ll].astype(BF) for k, ll in riding[l][where]]
        return (_Exchange(blocks, True), blocks) if blocks else None

    grad_parts = {}

    def bwd_ride(grads):
        slabs = [_to_slabs(k, g) for k, g in grads.items()]
        return _Exchange(slabs, False), slabs

    hooks = dict(fwd_ride=fwd_ride, fwd_recv=lambda l, where, results: take_weights(riding[l][where], results),
                 bwd_ride=bwd_ride, bwd_recv=lambda keys, results: grad_parts.update(zip(keys, results)))

    loss, dx, unsent, dmods, gss = _local_step(x[0], loss_target[0], mods, ps, small, hooks)
    assert not unsent
    out = {}
    for k in BIG:
        out[k] = _adamw([grad_parts[(k, l)] for l in range(depth)], W[k], M[k], V[k], name=f"adamw_{k}")

    dmod_mine = jnp.stack([jnp.concatenate(dmods[l], axis=1)[0] for l in range(depth)])
    small_mine = [dmod_mine] + [jnp.stack([gss[l][k] for l in range(depth)]) for k in SMALL_LOCAL] + [loss]
    parts, = _exchange([_pack(small_mine)], True, name="gather_small")
    summed = _sum_parts(parts, name="sum_small")
    names = ['b_ada'] + SMALL_LOCAL
    *small_grads, loss = _unpack(summed, [W[k] for k in names] + [loss])
    loss = loss[0, 0]
    for k, g in zip(names, small_grads):
        shp = W[k].shape
        rows = lambda a: a.reshape(depth, -1, shp[-1])
        res = _adamw([rows(g)[l][None] for l in range(depth)], rows(W[k]), rows(M[k]), rows(V[k]), name=f"adamw_{k}")
        out[k] = [a.reshape(shp) for a in res]

    dmod_all = parts.reshape(NDEV, -1)[:, :depth * 6 * d].reshape(NDEV, depth, 6 * d)
    dmod_cols = lax.dynamic_slice_in_dim(dmod_all, me * ncol, ncol, axis=2)
    g_ada = [_mm(cond, dmod_cols[:, l], ta=True, precision=HI, name=f"ada_bwd{l}")[None] for l in range(depth)]
    out['w_ada'] = _adamw(g_ada, w_ada, m_w_ada, v_w_ada, name="adamw_w_ada")

    return (loss, dx[None], *[view(k, out[k][i]) for i in range(4) for k in WEIGHTS])
```
